```python
import math
import jax, jax.numpy as jnp
from jax import lax
import numpy as np

D_MODEL = 1024
BATCH = 2
SEQ = 16384
DEPTH = 2

GRID_W = 64
CTX_LEN = 256

DA_HEADS = 4
DA_QK_DIM = 64
DA_V_DIM = 2 * DA_QK_DIM
ML_HEADS = 4
ML_DIM = 128
ML_CHUNK = 128
ML_CONV = 3
FN_GROUPS = 4
FN_GROUP_DIM = D_MODEL // FN_GROUPS
N_EXPERTS = 16
N_EXPERT_GROUPS = 4
EXPERTS_PER_GROUP = N_EXPERTS // N_EXPERT_GROUPS
TOP_K = 2
D_EXPERT = 512
ROPE_BASE = 10000.0
Q_BLOCK = 128
NORM_EPS = 1e-6
N_EVEN = (DEPTH + 1) // 2
N_ODD = DEPTH // 2

DA_QK_W = DA_HEADS * 2 * DA_QK_DIM
DA_V_W = DA_HEADS * DA_V_DIM
ML_W = ML_HEADS * ML_DIM
ML_GATE_W = 4 * ML_HEADS
IN_SPLITS = [DA_QK_W, 2 * DA_QK_W, 2 * DA_QK_W + DA_V_W, 2 * DA_QK_W + DA_V_W + 2 * ML_W,
             2 * DA_QK_W + DA_V_W + 3 * ML_W, 2 * DA_QK_W + DA_V_W + 4 * ML_W]
IN_COLS = 2 * DA_QK_W + DA_V_W + 4 * ML_W + ML_GATE_W
MIX_W = DA_V_W + ML_W

kernel_name = "hybrid_diffattn_mlstm_fnet_groupmoe_prefix"


def rmsnorm(x, g):
    xf = x.astype(jnp.float32)
    y = xf * lax.rsqrt(jnp.mean(xf * xf, axis=-1, keepdims=True) + NORM_EPS)
    return (y * g.astype(jnp.float32)).astype(x.dtype)


def modulate(h, shift, scale):
    return h * (1 + scale) + shift


def tokens_to_heads(a, n_heads):
    b, t, w = a.shape
    return a.reshape(b, t, n_heads, w // n_heads).transpose(0, 2, 1, 3)


def heads_to_tokens(a):
    b, h, t, d = a.shape
    return a.transpose(0, 2, 1, 3).reshape(b, t, h * d)


def axial_rope_tables(row, col):
    n_pairs_axis = DA_QK_DIM // 4
    inv = ROPE_BASE ** (-jnp.arange(n_pairs_axis, dtype=jnp.float32) / n_pairs_axis)
    ang = jnp.concatenate([row.astype(jnp.float32)[:, None] * inv,
                           col.astype(jnp.float32)[:, None] * inv], axis=-1)
    return jnp.cos(ang), jnp.sin(ang)


def apply_rope(x, cos, sin):
    xp = x.reshape(*x.shape[:-1], -1, 2)
    x0, x1 = xp[..., 0], xp[..., 1]
    cs, sn = cos.astype(x.dtype), sin.astype(x.dtype)
    return jnp.stack([x0 * cs - x1 * sn, x0 * sn + x1 * cs], axis=-1).reshape(x.shape)


def centred_dwconv(x, w):
    k = w.shape[0]
    p = k // 2
    t = x.shape[1]
    xp = jnp.pad(x, ((0, 0), (p, p), (0, 0)))
    out = xp[:, 0:t] * w[0]
    for j in range(1, k):
        out = out + xp[:, j:j + t] * w[j]
    return out


def split_projection(u, conv_w, gate_b):
    b, t, _ = u.shape
    da_q, da_k, da_v, ml_qk, ml_v, ml_o, gates = jnp.split(u, IN_SPLITS, axis=-1)
    dq = da_q.reshape(b, t, DA_HEADS, 2, DA_QK_DIM).transpose(3, 0, 2, 1, 4)
    dk = da_k.reshape(b, t, DA_HEADS, 2, DA_QK_DIM).transpose(3, 0, 2, 1, 4)
    dv = tokens_to_heads(da_v, DA_HEADS)
    qk = jax.nn.silu(centred_dwconv(ml_qk, conv_w))
    mq, mk = jnp.split(qk, 2, axis=-1)
    mq = tokens_to_heads(mq, ML_HEADS)
    mk = tokens_to_heads(mk, ML_HEADS) * (ML_DIM ** -0.5)
    mv = tokens_to_heads(ml_v, ML_HEADS)
    og = jax.nn.sigmoid(ml_o)
    g = (gates.astype(jnp.float32).reshape(b, t, 4, ML_HEADS)
         + gate_b.astype(jnp.float32)).transpose(2, 0, 3, 1)
    g = jnp.stack([g[0], jax.nn.log_sigmoid(g[1]), g[2], jax.nn.log_sigmoid(g[3])])
    return dq, dk, dv, mq, mk, mv, og, g


def diff_attn_block(q, k, v, lam):
    s = jnp.einsum("mbhqd,mbhkd->mbhqk", q, k).astype(jnp.float32) * (DA_QK_DIM ** -0.5)
    p = jax.nn.softmax(s, axis=-1)
    w = p[0] - lam * p[1]
    return jnp.einsum("bhqk,bhkd->bhqd", w.astype(v.dtype), v)


def diff_attention(q_c, k_c, v_c, q_l, k_l, v_l, lam, subln_g, lam_init, need_ctx):
    k_all = jnp.concatenate([k_c, k_l], axis=3)
    v_all = jnp.concatenate([v_c, v_l], axis=2)
    _, b, h, s, dq = q_l.shape
    nb = s // Q_BLOCK
    q_blocks = jnp.moveaxis(q_l.reshape(2, b, h, nb, Q_BLOCK, dq), 3, 0)
    o_blocks = lax.map(lambda qb: diff_attn_block(qb, k_all, v_all, lam), q_blocks)
    o_l = jnp.moveaxis(o_blocks, 0, 2).reshape(b, h, s, DA_V_DIM)
    o_l = rmsnorm(o_l, subln_g) * (1.0 - lam_init)
    o_c = None
    if need_ctx:
        o_c = rmsnorm(diff_attn_block(q_c, k_c, v_c, lam), subln_g) * (1.0 - lam_init)
    return o_c, o_l


def mlstm_chunkwise(q, k, v, log_i, log_f, state, with_output):
    b, h, t, d = q.shape
    nc = t // ML_CHUNK
    lc = ML_CHUNK

    def to_chunks(a):
        a = a.astype(jnp.float32)
        return jnp.moveaxis(a.reshape(b, h, nc, lc, *a.shape[3:]), 2, 0)

    xs = (to_chunks(q), to_chunks(k), to_chunks(v), to_chunks(log_i), to_chunks(log_f))
    causal = jnp.tril(jnp.ones((lc, lc), dtype=bool))

    def step(carry, inp):
        c_st, n_st, m_st = carry
        qc, kc, vc, lic, lfc = inp
        bcum = jnp.cumsum(lfc, axis=-1)
        b_last = bcum[..., -1]
        logw = b_last[..., None] - bcum + lic
        m_new = jnp.maximum(b_last + m_st, jnp.max(logw, axis=-1))
        wgt = jnp.exp(logw - m_new[..., None])
        decay = jnp.exp(b_last + m_st - m_new)
        c_new = decay[..., None, None] * c_st + jnp.einsum("bhl,bhlv,bhlk->bhvk", wgt, vc, kc)
        n_new = decay[..., None] * n_st + jnp.einsum("bhl,bhlk->bhk", wgt, kc)
        if not with_output:
            return (c_new, n_new, m_new), None
        log_d = bcum[..., :, None] - bcum[..., None, :] + lic[..., None, :]
        log_d = jnp.where(causal, log_d, -jnp.inf)
        m_inter = bcum + m_st[..., None]
        m_t = jnp.maximum(m_inter, jnp.max(log_d, axis=-1))
        dmat = jnp.exp(log_d - m_t[..., None])
        inter = jnp.exp(m_inter - m_t)
        sc = jnp.einsum("bhqd,bhkd->bhqk", qc, kc) * dmat
        num = (inter[..., None] * jnp.einsum("bhvk,bhqk->bhqv", c_st, qc)
               + jnp.einsum("bhqk,bhkv->bhqv", sc, vc))
        nq = inter * jnp.einsum("bhk,bhqk->bhq", n_st, qc) + jnp.sum(sc, axis=-1)
        den = jnp.maximum(jnp.abs(nq), jnp.exp(-m_t))
        return (c_new, n_new, m_new), num / den[..., None]

    final, hs = lax.scan(step, state, xs)
    if not with_output:
        return final, None
    out = jnp.moveaxis(hs, 0, 2).reshape(b, h, t, d).astype(q.dtype)
    return final, out


def mlstm_bidir(ctx_in, lat_in, need_ctx):
    q_c, k_c, v_c, g_c = ctx_in
    q_l, k_l, v_l, g_l = lat_in
    b, h, _, d = q_c.shape
    init = (jnp.zeros((b, h, d, d), jnp.float32), jnp.zeros((b, h, d), jnp.float32),
            jnp.zeros((b, h), jnp.float32))
    fl = lambda a: jnp.flip(a, axis=2)
    st_fc, h_fc = mlstm_chunkwise(q_c, k_c, v_c, g_c[0], g_c[1], init, need_ctx)
    _, h_fl = mlstm_chunkwise(q_l, k_l, v_l, g_l[0], g_l[1], st_fc, True)
    st_bc, h_bc = mlstm_chunkwise(fl(q_c), fl(k_c), fl(v_c), fl(g_c[2]), fl(g_c[3]), init, need_ctx)
    _, h_bl = mlstm_chunkwise(fl(q_l), fl(k_l), fl(v_l), fl(g_l[2]), fl(g_l[3]), st_bc, True)
    h_l = h_fl + fl(h_bl)
    h_c = h_fc + fl(h_bc) if need_ctx else None
    return h_c, h_l


def merge_groups(a, m, og):
    return jnp.concatenate([heads_to_tokens(a), heads_to_tokens(m) * og], axis=-1)


def even_mixer(h_c, h_l, w_in, w_out, conv_w, gate_b, lam_vec, subln_g, lam_init, cos, sin, need_ctx):
    dq_c, dk_c, dv_c, mq_c, mk_c, mv_c, og_c, g_c = split_projection(h_c @ w_in, conv_w, gate_b)
    dq_l, dk_l, dv_l, mq_l, mk_l, mv_l, og_l, g_l = split_projection(h_l @ w_in, conv_w, gate_b)
    dq_l = apply_rope(dq_l, cos, sin)
    dk_l = apply_rope(dk_l, cos, sin)
    lv = lam_vec.astype(jnp.float32)
    lam = jnp.exp(jnp.sum(lv[0] * lv[1])) - jnp.exp(jnp.sum(lv[2] * lv[3])) + lam_init
    a_c, a_l = diff_attention(dq_c, dk_c, dv_c, dq_l, dk_l, dv_l, lam, subln_g, lam_init, need_ctx)
    m_c, m_l = mlstm_bidir((mq_c, mk_c, mv_c, g_c), (mq_l, mk_l, mv_l, g_l), need_ctx)
    out_l = merge_groups(a_l, m_l, og_l) @ w_out
    out_c = merge_groups(a_c, m_c, og_c) @ w_out if need_ctx else None
    return out_c, out_l


def fourier_mix(h, w_fnet):
    b, t, d = h.shape
    hg = h.astype(jnp.float32).reshape(b, t, FN_GROUPS, FN_GROUP_DIM)
    f = jnp.fft.fft2(hg, axes=(1, 3), norm="ortho").real
    return f.reshape(b, t, d).astype(h.dtype) @ w_fnet


def grouped_moe(h, router_w, router_b, w_gate, w_up, w_down):
    aff = jax.nn.sigmoid((h @ router_w).astype(jnp.float32))
    biased = aff + router_b.astype(jnp.float32)
    grp = biased.reshape(*biased.shape[:-1], N_EXPERT_GROUPS, EXPERTS_PER_GROUP)
    grp_score = jnp.sum(lax.top_k(grp, TOP_K)[0], axis=-1)
    sel_group = jnp.argmax(grp_score, axis=-1)
    in_group = (jnp.arange(N_EXPERTS) // EXPERTS_PER_GROUP) == sel_group[..., None]
    _, idx = lax.top_k(jnp.where(in_group, biased, -jnp.inf), TOP_K)
    sel_aff = jnp.take_along_axis(aff, idx, axis=-1)
    wts = sel_aff / jnp.sum(sel_aff, axis=-1, keepdims=True)
    combine = jnp.sum(jax.nn.one_hot(idx, N_EXPERTS, dtype=jnp.float32) * wts[..., None], axis=-2)
    combine = combine.astype(h.dtype)
    y = jnp.zeros_like(h)
    for e in range(N_EXPERTS):
        he = jax.nn.silu(h @ w_gate[e]) * (h @ w_up[e])
        y = y + combine[..., e:e + 1] * (he @ w_down[e])
    return y


def setup_inputs(seed: int = 0) -> dict:
    key = jax.random.key(seed)
    ks = jax.random.split(key, 21)
    nrm = lambda k, shape, s: jax.random.normal(k, shape, jnp.float32) * s
    d = D_MODEL
    gate_base = jnp.array([0.0, 3.0, 0.0, 3.0], jnp.float32)[None, :, None]
    return {
        "x": nrm(ks[0], (BATCH, SEQ, d), 1.0),
        "c": nrm(ks[1], (BATCH, d), 1.0),
        "ctx": nrm(ks[2], (BATCH, CTX_LEN, d), 1.0),
        "c_ctx": nrm(ks[3], (d,), 1.0),
        "ada_w": nrm(ks[4], (DEPTH, d, 6 * d), 0.5 * d ** -0.5),
        "ada_b": nrm(ks[5], (DEPTH, 6 * d), 0.02),
        "norm_mix_g": 1.0 + nrm(ks[6], (DEPTH, d), 0.02),
        "norm_ffn_g": 1.0 + nrm(ks[7], (DEPTH, d), 0.02),
        "even_w_in": nrm(ks[8], (N_EVEN, d, IN_COLS), d ** -0.5),
        "even_w_out": nrm(ks[9], (N_EVEN, MIX_W, d), MIX_W ** -0.5),
        "even_conv_w": nrm(ks[10], (N_EVEN, ML_CONV, 2 * ML_W), ML_CONV ** -0.5),
        "even_gate_b": gate_base + nrm(ks[11], (N_EVEN, 4, ML_HEADS), 0.1),
        "even_lam": nrm(ks[12], (N_EVEN, 4, DA_QK_DIM), 0.1),
        "even_subln_g": 1.0 + nrm(ks[13], (N_EVEN, DA_V_DIM), 0.02),
        "odd_w_fnet": nrm(ks[14], (N_ODD, d, d), d ** -0.5),
        "router_w": nrm(ks[15], (d, N_EXPERTS), d ** -0.5),
        "router_b": nrm(ks[16], (N_EXPERTS,), 0.01),
        "exp_w_gate": nrm(ks[17], (DEPTH, N_EXPERTS, d, D_EXPERT), d ** -0.5),
        "exp_w_up": nrm(ks[18], (DEPTH, N_EXPERTS, d, D_EXPERT), d ** -0.5),
        "exp_w_down": nrm(ks[19], (DEPTH, N_EXPERTS, D_EXPERT, d), D_EXPERT ** -0.5),
        "final_g": 1.0 + nrm(ks[20], (d,), 0.02),
    }


def reference(x, c, ctx, c_ctx, ada_w, ada_b, norm_mix_g, norm_ffn_g, even_w_in, even_w_out,
              even_conv_w, even_gate_b, even_lam, even_subln_g, odd_w_fnet, router_w, router_b,
              exp_w_gate, exp_w_up, exp_w_down, final_g):
    n_lat = x.shape[1]
    rows = n_lat // GRID_W
    row = jnp.repeat(jnp.arange(rows, dtype=jnp.int32), GRID_W)
    col = jnp.tile(jnp.arange(GRID_W, dtype=jnp.int32), rows)
    cos, sin = axial_rope_tables(row, col)
    s_c = jax.nn.silu(c)
    s_cc = jax.nn.silu(c_ctx)
    x_l, x_c = x, ctx
    for i in range(DEPTH):
        last = i == DEPTH - 1
        even = i % 2 == 0
        ctx_out = not last
        ctx_in = ctx_out or even
        j = i // 2
        mod_l = jnp.split((s_c @ ada_w[i] + ada_b[i])[:, None, :], 6, axis=-1)
        h_l = modulate(rmsnorm(x_l, norm_mix_g[i]), mod_l[0], mod_l[1])
        h_c = None
        mod_c = None
        if ctx_in:
            mod_c = jnp.split(s_cc @ ada_w[i] + ada_b[i], 6, axis=-1)
            h_c = modulate(rmsnorm(x_c, norm_mix_g[i]), mod_c[0], mod_c[1])
        if even:
            lam_init = 0.8 - 0.6 * math.exp(-0.3 * i)
            o_c, o_l = even_mixer(h_c, h_l, even_w_in[j], even_w_out[j], even_conv_w[j],
                                  even_gate_b[j], even_lam[j], even_subln_g[j], lam_init,
                                  cos, sin, ctx_out)
        else:
            o_l = fourier_mix(h_l, odd_w_fnet[j])
            o_c = fourier_mix(h_c, odd_w_fnet[j]) if ctx_out else None
        x_l = x_l + mod_l[2] * o_l
        f_l = modulate(rmsnorm(x_l, norm_ffn_g[i]), mod_l[3], mod_l[4])
        if ctx_out:
            x_c = x_c + mod_c[2] * o_c
            f_c = modulate(rmsnorm(x_c, norm_ffn_g[i]), mod_c[3], mod_c[4])
            n_ctx = x_c.shape[1]
            y = grouped_moe(jnp.concatenate([f_c, f_l], axis=1), router_w, router_b,
                            exp_w_gate[i], exp_w_up[i], exp_w_down[i])
            x_c = x_c + mod_c[5] * y[:, :n_ctx]
            y_l = y[:, n_ctx:]
        else:
            y_l = grouped_moe(f_l, router_w, router_b, exp_w_gate[i], exp_w_up[i], exp_w_down[i])
        x_l = x_l + mod_l[5] * y_l
    return rmsnorm(x_l, final_g)
```

```python
import functools
import math

import jax
import jax.numpy as jnp
import numpy as np
from jax import lax
from jax.experimental import pallas as pl
from jax.experimental.pallas import tpu as pltpu

F32 = jnp.float32
BF16 = jnp.bfloat16

NORM_EPS = 1e-6
GRID_W = 64
DA_HEADS = 4
DA_QK_DIM = 64
DA_V_DIM = 128
ML_HEADS = 4
ML_DIM = 128
ML_CHUNK = 128
FN_GROUPS = 4
N_EXPERTS = 16
EXPERTS_PER_GROUP = 4
ROPE_BASE = 10000.0
LANES = 128
ROW_TILE = 256
VMEM_LIMIT_BYTES = 56 * 1024 * 1024
HI = lax.Precision.HIGHEST


def _params(*sem):
    return pltpu.CompilerParams(dimension_semantics=sem, vmem_limit_bytes=VMEM_LIMIT_BYTES)


def _dot(a, b, precision=None):
    return jnp.dot(a, b, preferred_element_type=F32, precision=precision)


def _dot_nt(a, b, precision=None):
    return lax.dot_general(a, b, (((1,), (1,)), ((), ())), preferred_element_type=F32,
                           precision=precision)


def _sigmoid(x):
    return 1.0 / (1.0 + jnp.exp(-x))


def _silu(x):
    return x * _sigmoid(x)


def _norm_mod(x, g, shift, scale):
    y = x * lax.rsqrt(jnp.mean(x * x, axis=-1, keepdims=True) + NORM_EPS) * g
    return y * (1.0 + scale) + shift


def _ada_kernel(c_ref, w_ref, b_ref, o_ref):
    o_ref[...] = _dot(_silu(c_ref[...]), w_ref[...], HI) + b_ref[...]


def _ada_mods(cond8, w, b):
    d, n = w.shape
    tn = n // 6
    return pl.pallas_call(
        _ada_kernel,
        grid=(6,),
        in_specs=[pl.BlockSpec((8, d), lambda j: (0, 0)),
                  pl.BlockSpec((d, tn), lambda j: (0, j)),
                  pl.BlockSpec((1, tn), lambda j: (0, j))],
        out_specs=pl.BlockSpec((8, tn), lambda j: (0, j)),
        out_shape=jax.ShapeDtypeStruct((8, n), F32),
        compiler_params=_params("arbitrary"),
        name="ada_mods",
    )(cond8, w, b.reshape(1, n))


def _inproj_kernel(x_ref, mod_ref, g_ref, cos_ref, sin_ref, wq_ref, wk_ref, wv_ref, wmqk_ref,
                   wmv_ref, wmo_ref, wg_ref, gb_ref,
                   q_ref, k_ref, v_ref, mqk_ref, mv_ref, og_ref, gate_ref):
    x = x_ref[0]
    h = _norm_mod(x, g_ref[...], mod_ref[0, 0:1, :], mod_ref[0, 1:2, :]).astype(BF16)
    tm = x.shape[0]
    width = q_ref.shape[2]
    cos = jnp.concatenate([cos_ref[...]] * (width // LANES), axis=1)
    sin = jnp.concatenate([sin_ref[...]] * (width // LANES), axis=1)
    lane = lax.broadcasted_iota(jnp.int32, (tm, width), 1)
    lower = (lane & (DA_QK_DIM - 1)) < (DA_QK_DIM // 2)

    def rope(u):
        swapped = jnp.where(lower, pltpu.roll(u, width - DA_QK_DIM // 2, 1),
                            pltpu.roll(u, DA_QK_DIM // 2, 1))
        return u * cos + swapped * sin

    q_ref[0] = rope(_dot(h, wq_ref[...])).astype(BF16)
    k_ref[0] = rope(_dot(h, wk_ref[...])).astype(BF16)
    v_ref[0] = _dot(h, wv_ref[...]).astype(BF16)
    mqk_ref[0] = _dot(h, wmqk_ref[...]).astype(BF16)
    mv_ref[0] = _dot(h, wmv_ref[...]).astype(BF16)
    og_ref[0] = _sigmoid(_dot(h, wmo_ref[...])).astype(BF16)
    g = _dot(h, wg_ref[...]) + gb_ref[...]
    glane = lax.broadcasted_iota(jnp.int32, g.shape, 1)
    is_forget = ((glane // ML_HEADS) & 1) == 1
    log_sig = jnp.minimum(g, 0.0) - jnp.log(1.0 + jnp.exp(-jnp.abs(g)))
    gate_ref[0] = jnp.where(is_forget, log_sig, g)


def _inproj(xs, modtab, g, cos, sin, ws, gate_b, n_ctx):
    b, t, d = xs.shape
    tm = ROW_TILE
    nt = t // tm
    ctx_tiles = n_ctx // tm
    row = lambda w: pl.BlockSpec((1, tm, w), lambda bi, i: (bi, i, 0))
    full = lambda a: pl.BlockSpec(a.shape, lambda bi, i: (0,) * a.ndim)
    widths = [w.shape[1] for w in ws]
    out_dtypes = [BF16] * 6 + [F32]
    return pl.pallas_call(
        _inproj_kernel,
        grid=(b, nt),
        in_specs=[row(d),
                  pl.BlockSpec((1, 8, d), lambda bi, i: (2 * bi + (i >= ctx_tiles).astype(jnp.int32), 0, 0)),
                  full(g),
                  pl.BlockSpec((tm, LANES), lambda bi, i: (i, 0)),
                  pl.BlockSpec((tm, LANES), lambda bi, i: (i, 0))]
                 + [full(w) for w in ws] + [full(gate_b)],
        out_specs=[row(w) for w in widths],
        out_shape=[jax.ShapeDtypeStruct((b, t, w), dt) for w, dt in zip(widths, out_dtypes)],
        compiler_params=_params("parallel", "parallel"),
        name="inproj",
    )(xs, modtab, g, cos, sin, *ws, gate_b)


def _split3(x):
    x1 = x.astype(BF16)
    r1 = x - x1.astype(F32)
    x2 = r1.astype(BF16)
    x3 = (r1 - x2.astype(F32)).astype(BF16)
    return x1, x2, x3


def _mlprep_kernel(cur_ref, prev_ref, next_ref, cw_ref, gate_ref,
                   mq_ref, mk_ref, gc_ref, gr_ref, *, ctx_tiles, n_tiles):
    i = pl.program_id(1)
    cur = cur_ref[0].astype(F32)
    tm, w = cur.shape
    prev_ok = i != ctx_tiles
    if ctx_tiles > 0:
        prev_ok = jnp.logical_and(prev_ok, i != 0)
        next_ok = jnp.logical_and(i != ctx_tiles - 1, i != n_tiles - 1)
    else:
        next_ok = i != n_tiles - 1
    prev_row = jnp.where(prev_ok, prev_ref[0, 7:8, :].astype(F32), 0.0)
    next_row = jnp.where(next_ok, next_ref[0, 0:1, :].astype(F32), 0.0)
    ridx = lax.broadcasted_iota(jnp.int32, (tm, w), 0)
    before = jnp.where(ridx == 0, prev_row, pltpu.roll(cur, 1, 0))
    after = jnp.where(ridx == tm - 1, next_row, pltpu.roll(cur, tm - 1, 0))
    y = _silu(before * cw_ref[0:1, :] + cur * cw_ref[1:2, :] + after * cw_ref[2:3, :])
    half = w // 2
    mq_ref[0] = y[:, :half].astype(BF16)
    mk_ref[0] = (y[:, half:] * (ML_DIM ** -0.5)).astype(BF16)

    n_g = 4 * ML_HEADS
    r = lax.broadcasted_iota(jnp.int32, (ML_CHUNK, ML_CHUNK), 0)
    c = lax.broadcasted_iota(jnp.int32, (ML_CHUNK, ML_CHUNK), 1)
    lower = (c <= r).astype(BF16)
    upper = (c >= r).astype(BF16)
    for ci in range(tm // ML_CHUNK):
        gch = gate_ref[0, ci * ML_CHUNK:(ci + 1) * ML_CHUNK, :]
        gm = jnp.where(c < n_g, gch, 0.0)
        pre = sum(_dot(lower, p) for p in _split3(pltpu.roll(gm, n_g, 1)))
        suf = sum(_dot(upper, p) for p in _split3(pltpu.roll(gm, 2 * n_g, 1)))
        col = gm + pre + suf
        gc_ref[0, ci * ML_CHUNK:(ci + 1) * ML_CHUNK, :] = col
        gr_ref[0, ci] = col.T


def _mlprep(mqk, conv_w8, gates, n_ctx):
    b, t, w = mqk.shape
    tm = ROW_TILE
    nt = t // tm
    sub = tm // 8
    nsub = t // 8
    cpt = tm // ML_CHUNK
    kern = functools.partial(_mlprep_kernel, ctx_tiles=n_ctx // tm, n_tiles=nt)
    return pl.pallas_call(
        kern,
        grid=(b, nt),
        in_specs=[pl.BlockSpec((1, tm, w), lambda bi, i: (bi, i, 0)),
                  pl.BlockSpec((1, 8, w), lambda bi, i: (bi, jnp.maximum(i * sub - 1, 0), 0)),
                  pl.BlockSpec((1, 8, w), lambda bi, i: (bi, jnp.minimum((i + 1) * sub, nsub - 1), 0)),
                  pl.BlockSpec(conv_w8.shape, lambda bi, i: (0, 0)),
                  pl.BlockSpec((1, tm, LANES), lambda bi, i: (bi, i, 0))],
        out_specs=[pl.BlockSpec((1, tm, w // 2), lambda bi, i: (bi, i, 0)),
                   pl.BlockSpec((1, tm, w // 2), lambda bi, i: (bi, i, 0)),
                   pl.BlockSpec((1, tm, LANES), lambda bi, i: (bi, i, 0)),
                   pl.BlockSpec((1, cpt, ML_CHUNK, LANES), lambda bi, i: (bi, i, 0, 0))],
        out_shape=[jax.ShapeDtypeStruct((b, t, w // 2), BF16),
                   jax.ShapeDtypeStruct((b, t, w // 2), BF16),
                   jax.ShapeDtypeStruct((b, t, LANES), F32),
                   jax.ShapeDtypeStruct((b, t // ML_CHUNK, ML_CHUNK, LANES), F32)],
        compiler_params=_params("parallel", "parallel"),
        name="mlstm_prep",
    )(mqk, mqk, mqk, conv_w8, gates)


def _mlstm_kernel(qf_ref, kf_ref, vf_ref, gcf_ref, grf_ref, qb_ref, kb_ref, vb_ref, gcb_ref, grb_ref,
                  hf_ref, hb_ref, state_ref, m_ref):
    s = pl.program_id(0)
    nb = qf_ref.shape[0]
    n_g = 4 * ML_HEADS

    @pl.when(s == 0)
    def _():
        state_ref[...] = jnp.zeros(state_ref.shape, F32)
        m_ref[...] = jnp.zeros(m_ref.shape, F32)

    qi = lax.broadcasted_iota(jnp.int32, (ML_CHUNK, ML_CHUNK), 0)
    ki = lax.broadcasted_iota(jnp.int32, (ML_CHUNK, ML_CHUNK), 1)
    ones_col = (ki == 0).astype(BF16)

    for bi in range(nb):
        for direction in range(2):
            q_ref, k_ref, v_ref, gc_ref, gr_ref, h_ref = (
                (qf_ref, kf_ref, vf_ref, gcf_ref, grf_ref, hf_ref) if direction == 0 else
                (qb_ref, kb_ref, vb_ref, gcb_ref, grb_ref, hb_ref))
            visible = (ki <= qi) if direction == 0 else (ki >= qi)
            gc = gc_ref[bi]
            gr = gr_ref[bi, 0]
            for hd in range(ML_HEADS):
                chain = (bi * 2 + direction) * ML_HEADS + hd
                c_li = 2 * direction * ML_HEADS + hd
                c_lf = c_li + ML_HEADS
                c_cs = c_lf + (n_g if direction == 0 else 2 * n_g)
                lo, hi = hd * ML_DIM, (hd + 1) * ML_DIM
                q = q_ref[bi, :, lo:hi]
                k = k_ref[bi, :, lo:hi]
                v_aug = jnp.concatenate([v_ref[bi, :, lo:hi], ones_col], axis=1)
                li_col = gc[:, c_li:c_li + 1]
                cs_col = gc[:, c_cs:c_cs + 1]
                li_row = gr[c_li:c_li + 1, :]
                lf_row = gr[c_lf:c_lf + 1, :]
                cs_row = gr[c_cs:c_cs + 1, :]
                m_st = m_ref[chain][:, 0:1]
                st = state_ref[chain]

                b_last = jnp.sum(lf_row, axis=1, keepdims=True)
                logw = b_last - cs_col + li_col
                m_new = jnp.maximum(b_last + m_st, jnp.max(logw, axis=0, keepdims=True))
                wgt = jnp.exp(logw - m_new)
                decay = jnp.exp(b_last + m_st - m_new)

                log_d = jnp.where(visible, cs_col - cs_row + li_row, -jnp.inf)
                m_inter = cs_col + m_st
                m_t = jnp.maximum(m_inter, jnp.max(log_d, axis=1, keepdims=True))
                dmat = jnp.exp(log_d - m_t)
                inter = jnp.exp(m_inter - m_t)

                sc = (_dot_nt(q, k) * dmat).astype(BF16)
                tot = inter * _dot(q, st.astype(BF16)) + _dot(sc, v_aug)
                den = jnp.maximum(jnp.abs(tot[:, ML_DIM:ML_DIM + 1]), jnp.exp(-m_t))
                h_ref[bi, :, lo:hi] = tot[:, :ML_DIM] / den

                kw_t = (k.astype(F32) * wgt).T.astype(BF16)
                state_ref[chain] = decay * st + _dot(kw_t, v_aug)
                m_ref[chain] = jnp.broadcast_to(m_new, (1, LANES))


def _mlstm(mq, mk, mv, gc, gr, n_ctx):
    b, t, w = mq.shape
    nc = t // ML_CHUNK
    ncc = n_ctx // ML_CHUNK
    fwd = lambda s: s
    bwd = lambda s: jnp.where(s < ncc, ncc - 1 - s, nc - 1 - s + ncc)
    tok = lambda f: pl.BlockSpec((b, ML_CHUNK, w), lambda s: (0, f(s), 0))
    gcs = lambda f: pl.BlockSpec((b, ML_CHUNK, LANES), lambda s: (0, f(s), 0))
    grs = lambda f: pl.BlockSpec((b, 1, ML_CHUNK, LANES), lambda s: (0, f(s), 0, 0))
    n_chain = b * 2 * ML_HEADS
    return pl.pallas_call(
        _mlstm_kernel,
        grid=(nc,),
        in_specs=[tok(fwd), tok(fwd), tok(fwd), gcs(fwd), grs(fwd),
                  tok(bwd), tok(bwd), tok(bwd), gcs(bwd), grs(bwd)],
        out_specs=[tok(fwd), tok(bwd)],
        out_shape=[jax.ShapeDtypeStruct((b, t, w), F32)] * 2,
        scratch_shapes=[pltpu.VMEM((n_chain, ML_DIM, 2 * ML_DIM), F32),
                        pltpu.VMEM((n_chain, 1, LANES), F32)],
        compiler_params=_params("arbitrary"),
        name="mlstm",
    )(mq, mk, mv, gc, gr, mq, mk, mv, gc, gr)


def _attn_kernel(q_ref, k_ref, v_ref, lam_ref, sg_ref, o_ref, m_scr, l_scr, acc_scr,
                 *, n_ctx, n_lat, tk, lam_init):
    i = pl.program_id(2)
    q = q_ref[0]
    tq = q.shape[0]
    lane = lax.broadcasted_iota(jnp.int32, q.shape, 1)
    zero = jnp.zeros_like(q)
    q_maps = (jnp.where(lane < DA_QK_DIM, q, zero), jnp.where(lane >= DA_QK_DIM, q, zero))
    m_scr[...] = jnp.full(m_scr.shape, -jnp.inf, F32)
    l_scr[...] = jnp.zeros(l_scr.shape, F32)
    acc_scr[...] = jnp.zeros(acc_scr.shape, F32)

    def step(start, size):
        kc = k_ref[0, pl.ds(start, size), :]
        vc = v_ref[0, pl.ds(start, size), :]
        for mp in range(2):
            sc = _dot_nt(q_maps[mp], kc)
            m_prev = m_scr[mp]
            m_new = jnp.maximum(m_prev, jnp.max(sc, axis=1, keepdims=True))
            alpha = jnp.exp(m_prev - m_new)
            p = jnp.exp(sc - m_new)
            l_scr[mp] = alpha * l_scr[mp] + jnp.sum(p, axis=1, keepdims=True)
            acc_scr[mp] = alpha * acc_scr[mp] + _dot(p.astype(BF16), vc)
            m_scr[mp] = m_new

    step(0, n_ctx)

    @pl.when(i >= n_ctx // tq)
    def _():
        def body(j, carry):
            step(pl.multiple_of(n_ctx + j * tk, tk), tk)
            return carry
        lax.fori_loop(0, n_lat // tk, body, 0)

    lv = lam_ref[...]
    dot01 = jnp.sum(lv[0:1, :] * lv[1:2, :], axis=1, keepdims=True)
    dot23 = jnp.sum(lv[2:3, :] * lv[3:4, :], axis=1, keepdims=True)
    lam = jnp.exp(dot01) - jnp.exp(dot23) + lam_init
    o = acc_scr[0] / l_scr[0] - lam * (acc_scr[1] / l_scr[1])
    o = o * lax.rsqrt(jnp.mean(o * o, axis=-1, keepdims=True) + NORM_EPS) * sg_ref[...]
    o_ref[0] = (o * (1.0 - lam_init)).astype(BF16)


def _diff_attention(q, k, v, lam8, subln_g, n_ctx, lam_init):
    b, t, w = q.shape
    tq = ROW_TILE
    n_lat = t - n_ctx
    tk = 512 if n_lat % 512 == 0 else ROW_TILE
    kern = functools.partial(_attn_kernel, n_ctx=n_ctx, n_lat=n_lat, tk=tk, lam_init=lam_init)
    return pl.pallas_call(
        kern,
        grid=(b, DA_HEADS, t // tq),
        in_specs=[pl.BlockSpec((1, tq, LANES), lambda bi, h, i: (bi, i, h)),
                  pl.BlockSpec((1, t, LANES), lambda bi, h, i: (bi, 0, h)),
                  pl.BlockSpec((1, t, LANES), lambda bi, h, i: (bi, 0, h)),
                  pl.BlockSpec(lam8.shape, lambda bi, h, i: (0, 0)),
                  pl.BlockSpec(subln_g.shape, lambda bi, h, i: (0, 0))],
        out_specs=pl.BlockSpec((1, tq, LANES), lambda bi, h, i: (bi, i, h)),
        out_shape=jax.ShapeDtypeStruct((b, t, w), BF16),
        scratch_shapes=[pltpu.VMEM((2, tq, 1), F32), pltpu.VMEM((2, tq, 1), F32),
                        pltpu.VMEM((2, tq, DA_V_DIM), F32)],
        compiler_params=_params("parallel", "parallel", "arbitrary"),
        name="diff_attention",
    )(q, k, v, lam8, subln_g)


def _top2_sum(a, b, c, d):
    hi1, lo1 = jnp.maximum(a, b), jnp.minimum(a, b)
    hi2, lo2 = jnp.maximum(c, d), jnp.minimum(c, d)
    return jnp.maximum(hi1, hi2) + jnp.maximum(jnp.minimum(hi1, hi2), jnp.maximum(lo1, lo2))


def _route(f, rwt_ref, rb_ref):
    tm = f.shape[0]
    aff = _sigmoid(_dot_nt(rwt_ref[...], f, HI))
    biased = aff + rb_ref[:, 0:1]
    bz = [biased[e:e + 1, :] for e in range(N_EXPERTS)]
    af = [aff[e:e + 1, :] for e in range(N_EXPERTS)]
    n_grp = N_EXPERTS // EXPERTS_PER_GROUP
    scores = [_top2_sum(*bz[EXPERTS_PER_GROUP * g:EXPERTS_PER_GROUP * (g + 1)]) for g in range(n_grp)]
    best = scores[0]
    sel_grp = jnp.zeros_like(best, dtype=jnp.int32)
    for g in range(1, n_grp):
        better = scores[g] > best
        sel_grp = jnp.where(better, g, sel_grp)
        best = jnp.where(better, scores[g], best)
    chosen = []
    for e in range(N_EXPERTS):
        g = e // EXPERTS_PER_GROUP
        rank = jnp.zeros_like(sel_grp)
        for o in range(EXPERTS_PER_GROUP * g, EXPERTS_PER_GROUP * (g + 1)):
            if o == e:
                continue
            beats = (bz[o] > bz[e]) if o > e else (bz[o] >= bz[e])
            rank = rank + beats.astype(jnp.int32)
        chosen.append(jnp.logical_and(sel_grp == g, rank < 2))
    denom = sum(jnp.where(chosen[e], af[e], 0.0) for e in range(N_EXPERTS))
    row = lax.broadcasted_iota(jnp.int32, (LANES, tm), 0)
    comb_t = jnp.zeros((LANES, tm), F32)
    for e in range(N_EXPERTS):
        comb_t = jnp.where(jnp.logical_and(row == e, chosen[e]), af[e] / denom, comb_t)
    return comb_t.T


def _post_kernel(*refs, even):
    if even:
        (x_ref, a_ref, hf_ref, hb_ref, og_ref, mod_ref, wa_ref, wm_ref, gf_ref, rwt_ref, rb_ref,
         xo_ref, f_ref, comb_ref) = refs
        m = ((hf_ref[0] + hb_ref[0]) * og_ref[0].astype(F32)).astype(BF16)
        o = _dot(a_ref[0], wa_ref[...]) + _dot(m, wm_ref[...])
    else:
        x_ref, a_ref, mod_ref, wa_ref, gf_ref, rwt_ref, rb_ref, xo_ref, f_ref, comb_ref = refs
        o = _dot(a_ref[0].astype(BF16), wa_ref[...])
    x = x_ref[0] + mod_ref[0, 2:3, :] * o
    xo_ref[0] = x
    f = _norm_mod(x, gf_ref[...], mod_ref[0, 3:4, :], mod_ref[0, 4:5, :])
    f_ref[0] = f.astype(BF16)
    comb_ref[0] = _route(f, rwt_ref, rb_ref)


def _post_mixer(x, acts, weights, modtab, gffn, rwt, rb, n_ctx, x_row_off, even):
    b, t, _ = acts[0].shape
    d = x.shape[2]
    tm = ROW_TILE
    nt = t // tm
    ctx_tiles = n_ctx // tm
    off = x_row_off // tm
    full = lambda a: pl.BlockSpec(a.shape, lambda bi, i: (0,) * a.ndim)
    row = lambda w: pl.BlockSpec((1, tm, w), lambda bi, i: (bi, i, 0))
    mod_spec = pl.BlockSpec(
        (1, 8, d), lambda bi, i: (2 * bi + (i + off >= ctx_tiles).astype(jnp.int32), 0, 0))
    in_specs = ([pl.BlockSpec((1, tm, d), lambda bi, i: (bi, i + off, 0))]
                + [row(a.shape[2]) for a in acts] + [mod_spec]
                + [full(w) for w in weights] + [full(gffn), full(rwt), full(rb)])
    return pl.pallas_call(
        functools.partial(_post_kernel, even=even),
        grid=(b, nt),
        in_specs=in_specs,
        out_specs=[row(d), row(d), row(LANES)],
        out_shape=[jax.ShapeDtypeStruct((b, t, d), F32), jax.ShapeDtypeStruct((b, t, d), BF16),
                   jax.ShapeDtypeStruct((b, t, LANES), F32)],
        compiler_params=_params("parallel", "parallel"),
        name="post_mixer_even" if even else "post_mixer_odd",
    )(x, *acts, modtab, *weights, gffn, rwt, rb)


def _moe_kernel(f_ref, comb_ref, wg_ref, wu_ref, wd_ref, x_ref, modc_ref, modl_ref, fg_ref,
                o_ref, acc_ref, *, n_ctx, final_norm):
    i = pl.program_id(1)
    e = pl.program_id(2)
    tm = f_ref.shape[1]

    @pl.when(e == 0)
    def _():
        acc_ref[...] = jnp.zeros(acc_ref.shape, F32)

    fb = f_ref[0]
    he = _silu(_dot(fb, wg_ref[0])) * _dot(fb, wu_ref[0])
    d_e = he.shape[1]
    pick = (lax.broadcasted_iota(jnp.int32, (LANES, d_e), 0) == e).astype(BF16)
    comb = comb_ref[0]
    comb_hi = comb.astype(BF16)
    comb_lo = (comb - comb_hi.astype(F32)).astype(BF16)
    weight = _dot(comb_hi, pick) + _dot(comb_lo, pick)
    acc_ref[...] += _dot((he * weight).astype(BF16), wd_ref[0])

    @pl.when(e == pl.num_programs(2) - 1)
    def _():
        rows = i * tm + lax.broadcasted_iota(jnp.int32, (tm, 1), 0)
        gate = jnp.where(rows < n_ctx, modc_ref[0, 5:6, :], modl_ref[0, 5:6, :])
        y = x_ref[0] + gate * acc_ref[...]
        if final_norm:
            y = y * lax.rsqrt(jnp.mean(y * y, axis=-1, keepdims=True) + NORM_EPS) * fg_ref[...]
        o_ref[0] = y


def _moe_tile(t):
    for tm in (640, 512, 384, 256, 128):
        if t % tm == 0:
            return tm
    raise ValueError(f"unsupported token count {t}")


def _moe(f, comb, wg, wu, wd, xmid, modtab, final_g, n_ctx, final_norm):
    b, t, d = f.shape
    n_e, _, d_e = wg.shape
    tm = _moe_tile(t)
    row = lambda w: pl.BlockSpec((1, tm, w), lambda bi, i, e: (bi, i, 0))
    return pl.pallas_call(
        functools.partial(_moe_kernel, n_ctx=n_ctx, final_norm=final_norm),
        grid=(b, t // tm, n_e),
        in_specs=[row(d), row(LANES),
                  pl.BlockSpec((1, d, d_e), lambda bi, i, e: (e, 0, 0)),
                  pl.BlockSpec((1, d, d_e), lambda bi, i, e: (e, 0, 0)),
                  pl.BlockSpec((1, d_e, d), lambda bi, i, e: (e, 0, 0)),
                  row(d),
                  pl.BlockSpec((1, 8, d), lambda bi, i, e: (2 * bi, 0, 0)),
                  pl.BlockSpec((1, 8, d), lambda bi, i, e: (2 * bi + 1, 0, 0)),
                  pl.BlockSpec(final_g.shape, lambda bi, i, e: (0, 0))],
        out_specs=row(d),
        out_shape=jax.ShapeDtypeStruct((b, t, d), F32),
        scratch_shapes=[pltpu.VMEM((tm, d), F32)],
        compiler_params=_params("parallel", "parallel", "arbitrary"),
        name="moe",
    )(f, comb, wg, wu, wd, xmid, modtab, modtab, final_g)


def _chan_dft_kernel(x_ref, mod_ref, g_ref, w_ref, zr_ref, zi_ref):
    h = _norm_mod(x_ref[0], g_ref[...], mod_ref[0, 0:1, :], mod_ref[0, 1:2, :])
    gd = w_ref.shape[0]
    for gi in range(h.shape[1] // gd):
        z = _dot(h[:, gi * gd:(gi + 1) * gd], w_ref[...], HI)
        zr_ref[0, :, gi * gd:(gi + 1) * gd] = z[:, :gd]
        zi_ref[0, :, gi * gd:(gi + 1) * gd] = z[:, gd:]


def _chan_dft(x, modtab, g, w_cs, x_row_off, t):
    b, _, d = x.shape
    tm = ROW_TILE
    off = x_row_off // tm
    row = pl.BlockSpec((1, tm, d), lambda bi, i: (bi, i, 0))
    return pl.pallas_call(
        _chan_dft_kernel,
        grid=(b, t // tm),
        in_specs=[pl.BlockSpec((1, tm, d), lambda bi, i: (bi, i + off, 0)),
                  pl.BlockSpec((1, 8, d), lambda bi, i: (2 * bi + 1, 0, 0)),
                  pl.BlockSpec(g.shape, lambda bi, i: (0, 0)),
                  pl.BlockSpec(w_cs.shape, lambda bi, i: (0, 0))],
        out_specs=[row, row],
        out_shape=[jax.ShapeDtypeStruct((b, t, d), F32)] * 2,
        compiler_params=_params("parallel", "parallel"),
        name="chan_dft",
    )(x, modtab, g, w_cs)


def _dft1_kernel(zr_ref, zi_ref, w_ref, yr_ref, yi_ref):
    n1 = zr_ref.shape[1]
    y = _dot(w_ref[...], jnp.concatenate([zr_ref[0], zi_ref[0]], axis=0), HI)
    yr_ref[0] = y[:n1]
    yi_ref[0] = y[n1:]


def _dft1(zr, zi, w1):
    b, n1, cols = zr.shape
    tn = min(cols, 4096)
    blk = pl.BlockSpec((1, n1, tn), lambda bi, j: (bi, 0, j))
    return pl.pallas_call(
        _dft1_kernel,
        grid=(b, cols // tn),
        in_specs=[blk, blk, pl.BlockSpec(w1.shape, lambda bi, j: (0, 0))],
        out_specs=[blk, blk],
        out_shape=[jax.ShapeDtypeStruct(zr.shape, F32)] * 2,
        compiler_params=_params("parallel", "parallel"),
        name="dft_stage1",
    )(zr, zi, w1)


def _dft2_kernel(yr_ref, yi_ref, tab_ref, o_ref):
    y = jnp.concatenate([yr_ref[0, 0], yi_ref[0, 0]], axis=0)
    o_ref[0, 0] = _dot(tab_ref[0], y, HI)


def _dft2(yr, yi, tab):
    b, n1, n2, d = yr.shape
    blk = pl.BlockSpec((1, 1, n2, d), lambda bi, k1: (bi, k1, 0, 0))
    return pl.pallas_call(
        _dft2_kernel,
        grid=(b, n1),
        in_specs=[blk, blk, pl.BlockSpec((1, n2, 2 * n2), lambda bi, k1: (k1, 0, 0))],
        out_specs=blk,
        out_shape=jax.ShapeDtypeStruct(yr.shape, F32),
        compiler_params=_params("parallel", "parallel"),
        name="dft_stage2",
    )(yr, yi, tab)


def _dft_tables(t, gd):
    n2 = ML_CHUNK
    n1 = t // n2
    def cs(num, den):
        ang = (2.0 * np.pi / den) * (num % den).astype(np.float64)
        return np.cos(ang), np.sin(ang)
    c = np.arange(gd)
    cc, sc = cs(np.outer(c, c), gd)
    w_cs = np.concatenate([cc, -sc], axis=1)
    a = np.arange(n1)
    c1, s1 = cs(np.outer(a, a), n1)
    w1 = np.block([[c1, s1], [-s1, c1]])
    k = a[:, None, None] + n1 * np.arange(n2)[None, :, None]
    c2, s2 = cs(k * np.arange(n2)[None, None, :], t)
    tab = np.concatenate([c2, s2], axis=2) / math.sqrt(t * gd)
    return (jnp.asarray(w_cs, F32), jnp.asarray(w1, F32), jnp.asarray(tab, F32))


def _rope_tables(n_ctx, n_lat):
    pos = jnp.arange(n_lat, dtype=jnp.int32)
    n_axis = DA_QK_DIM // 4
    inv = ROPE_BASE ** (-jnp.arange(n_axis, dtype=F32) / n_axis)
    ang = jnp.concatenate([(pos // GRID_W).astype(F32)[:, None] * inv,
                           (pos % GRID_W).astype(F32)[:, None] * inv], axis=-1)
    cos, sin = jnp.cos(ang), jnp.sin(ang)
    cos = jnp.concatenate([jnp.ones((n_ctx, 2 * n_axis), F32), cos], axis=0)
    sin = jnp.concatenate([jnp.zeros((n_ctx, 2 * n_axis), F32), sin], axis=0)
    cos128 = jnp.concatenate([cos, cos, cos, cos], axis=1)
    sin128 = jnp.concatenate([-sin, sin, -sin, sin], axis=1)
    return cos128, sin128


def _deinterleave(w):
    d, n = w.shape
    w = w.reshape(d, n // DA_QK_DIM, DA_QK_DIM // 2, 2)
    return jnp.concatenate([w[..., 0], w[..., 1]], axis=-1).reshape(d, n)


def _pad_rows(a, rows):
    return jnp.concatenate([a, jnp.zeros((rows - a.shape[0],) + a.shape[1:], a.dtype)], axis=0)


def _pad_cols(a, cols):
    return jnp.concatenate([a, jnp.zeros(a.shape[:-1] + (cols - a.shape[-1],), a.dtype)], axis=-1)


def kernel(x, c, ctx, c_ctx, ada_w, ada_b, norm_mix_g, norm_ffn_g, even_w_in, even_w_out,
           even_conv_w, even_gate_b, even_lam, even_subln_g, odd_w_fnet, router_w, router_b,
           exp_w_gate, exp_w_up, exp_w_down, final_g):
    b, n_lat, d = x.shape
    n_ctx = ctx.shape[1]
    depth = ada_w.shape[0]
    assert depth == 2 and b + 1 <= 8
    assert n_ctx % ROW_TILE == 0 and n_lat % ROW_TILE == 0

    cond8 = _pad_rows(jnp.concatenate([c_ctx[None, :], c], axis=0), 8)
    rwt = router_w.T
    rb = jnp.broadcast_to(router_b[:, None], (N_EXPERTS, LANES))
    row2 = lambda v: v.reshape(1, -1)

    def modtab_for(layer):
        mods = _ada_mods(cond8, ada_w[layer], ada_b[layer]).reshape(8, 6, d)
        mods = jnp.concatenate([mods, jnp.zeros((8, 2, d), F32)], axis=1)
        idx = np.array([[0, 1 + bi] for bi in range(b)]).reshape(-1)
        return mods[idx]

    xs = jnp.concatenate([ctx, x], axis=1)
    modtab = modtab_for(0)
    w_in = even_w_in[0]
    o1 = DA_HEADS * 2 * DA_QK_DIM
    o2 = 2 * o1
    o3 = o2 + DA_HEADS * DA_V_DIM
    o4 = o3 + 2 * ML_HEADS * ML_DIM
    o5 = o4 + ML_HEADS * ML_DIM
    o6 = o5 + ML_HEADS * ML_DIM
    ws = [(_deinterleave(w_in[:, :o1]) * (DA_QK_DIM ** -0.5)).astype(BF16),
          _deinterleave(w_in[:, o1:o2]).astype(BF16),
          w_in[:, o2:o3].astype(BF16), w_in[:, o3:o4].astype(BF16),
          w_in[:, o4:o5].astype(BF16), w_in[:, o5:o6].astype(BF16),
          _pad_cols(w_in[:, o6:], LANES).astype(BF16)]
    gate_b = _pad_cols(even_gate_b[0].reshape(1, -1), LANES)
    cos128, sin128 = _rope_tables(n_ctx, n_lat)
    daq, dak, dav, mqk, mv, og, gates = _inproj(xs, modtab, row2(norm_mix_g[0]), cos128, sin128,
                                                 ws, gate_b, n_ctx)
    mq, mk, gc, gr = _mlprep(mqk, _pad_rows(even_conv_w[0], 8), gates, n_ctx)
    hf, hb = _mlstm(mq, mk, mv, gc, gr, n_ctx)
    lam_init = 0.8 - 0.6 * math.exp(-0.3 * 0)
    lam8 = _pad_rows(even_lam[0], 8)
    att = _diff_attention(daq, dak, dav, lam8, row2(even_subln_g[0]), n_ctx, lam_init)
    w_out = even_w_out[0].astype(BF16)
    half = DA_HEADS * DA_V_DIM
    xmid, f, comb = _post_mixer(xs, [att, hf, hb, og], [w_out[:half], w_out[half:]], modtab,
                                row2(norm_ffn_g[0]), rwt, rb, n_ctx, 0, True)
    xs = _moe(f, comb, exp_w_gate[0].astype(BF16), exp_w_up[0].astype(BF16),
              exp_w_down[0].astype(BF16), xmid, modtab, row2(final_g), n_ctx, False)

    modtab = modtab_for(1)
    gd = d // FN_GROUPS
    w_cs, w1, tab = _dft_tables(n_lat, gd)
    n2 = ML_CHUNK
    n1 = n_lat // n2
    zr, zi = _chan_dft(xs, modtab, row2(norm_mix_g[1]), w_cs, n_ctx, n_lat)
    yr, yi = _dft1(zr.reshape(b, n1, n2 * d), zi.reshape(b, n1, n2 * d), w1)
    fo = _dft2(yr.reshape(b, n1, n2, d), yi.reshape(b, n1, n2, d), tab)
    fo = fo.transpose(0, 2, 1, 3).reshape(b, n_lat, d)
    xmid, f, comb = _post_mixer(xs, [fo], [odd_w_fnet[0].astype(BF16)], modtab,
                                row2(norm_ffn_g[1]), rwt, rb, 0, n_ctx, False)
    return _moe(f, comb, exp_w_gate[1].astype(BF16), exp_w_up[1].astype(BF16),
                exp_w_down[1].astype(BF16), xmid, modtab, row2(final_g), 0, True)
```

```python
import functools
import math

import jax
import jax.numpy as jnp
import numpy as np
from jax import lax
from jax.experimental import pallas as pl
from jax.experimental.pallas import tpu as pltpu

F32 = jnp.float32
BF16 = jnp.bfloat16

NORM_EPS = 1e-6
GRID_W = 64
DA_HEADS = 4
DA_QK_DIM = 64
DA_V_DIM = 128
ML_HEADS = 4
ML_DIM = 128
ML_CHUNK = 128
FN_GROUPS = 4
N_EXPERTS = 16
EXPERTS_PER_GROUP = 4
ROPE_BASE = 10000.0
LANES = 128
ROW_TILE = 256
VMEM_LIMIT_BYTES = 56 * 1024 * 1024
HI = lax.Precision.HIGHEST
LOG2_E = math.log2(math.e)


def _params(*sem):
    return pltpu.CompilerParams(dimension_semantics=sem, vmem_limit_bytes=VMEM_LIMIT_BYTES)


def _dot(a, b, precision=None):
    return jnp.dot(a, b, preferred_element_type=F32, precision=precision)


def _dot_nt(a, b, precision=None):
    return lax.dot_general(a, b, (((1,), (1,)), ((), ())), preferred_element_type=F32,
                           precision=precision)


def _sigmoid(x):
    return 1.0 / (1.0 + jnp.exp(-x))


def _silu(x):
    return x * _sigmoid(x)


def _norm_mod(x, g, shift, scale):
    y = x * lax.rsqrt(jnp.mean(x * x, axis=-1, keepdims=True) + NORM_EPS) * g
    return y * (1.0 + scale) + shift


def _ada_kernel(c_ref, w_ref, b_ref, o_ref):
    o_ref[...] = _dot(_silu(c_ref[...]), w_ref[...], HI) + b_ref[...]


def _ada_mods(cond8, w, b):
    d, n = w.shape
    tn = n // 6
    return pl.pallas_call(
        _ada_kernel,
        grid=(6,),
        in_specs=[pl.BlockSpec((8, d), lambda j: (0, 0)),
                  pl.BlockSpec((d, tn), lambda j: (0, j)),
                  pl.BlockSpec((1, tn), lambda j: (0, j))],
        out_specs=pl.BlockSpec((8, tn), lambda j: (0, j)),
        out_shape=jax.ShapeDtypeStruct((8, n), F32),
        compiler_params=_params("arbitrary"),
        name="ada_mods",
    )(cond8, w, b.reshape(1, n))


def _inproj_kernel(x_ref, mod_ref, g_ref, cos_ref, sin_ref, wq_ref, wk_ref, wv_ref, wmqk_ref,
                   wmv_ref, wmo_ref, wg_ref, gb_ref,
                   q_ref, k_ref, v_ref, mqk_ref, mv_ref, og_ref, gate_ref):
    x = x_ref[0]
    h = _norm_mod(x, g_ref[...], mod_ref[0, 0:1, :], mod_ref[0, 1:2, :]).astype(BF16)
    tm = x.shape[0]
    width = q_ref.shape[2]
    cos = jnp.concatenate([cos_ref[...]] * (width // LANES), axis=1)
    sin = jnp.concatenate([sin_ref[...]] * (width // LANES), axis=1)
    lane = lax.broadcasted_iota(jnp.int32, (tm, width), 1)
    lower = (lane & (DA_QK_DIM - 1)) < (DA_QK_DIM // 2)

    def rope(u):
        swapped = jnp.where(lower, pltpu.roll(u, width - DA_QK_DIM // 2, 1),
                            pltpu.roll(u, DA_QK_DIM // 2, 1))
        return u * cos + swapped * sin

    q_ref[0] = (rope(_dot(h, wq_ref[...])) * LOG2_E).astype(BF16)
    k_ref[0] = rope(_dot(h, wk_ref[...])).astype(BF16)
    v = _dot(h, wv_ref[...]).astype(BF16)
    ones_col = (lax.broadcasted_iota(jnp.int32, (tm, DA_V_DIM), 1) == 0).astype(BF16)
    v_ref[0] = jnp.concatenate(
        [blk for hd in range(DA_HEADS) for blk in (v[:, hd * DA_V_DIM:(hd + 1) * DA_V_DIM], ones_col)],
        axis=1)
    mqk_ref[0] = _dot(h, wmqk_ref[...]).astype(BF16)
    mv_ref[0] = _dot(h, wmv_ref[...]).astype(BF16)
    og_ref[0] = _sigmoid(_dot(h, wmo_ref[...])).astype(BF16)
    g = _dot(h, wg_ref[...]) + gb_ref[...]
    glane = lax.broadcasted_iota(jnp.int32, g.shape, 1)
    is_forget = ((glane // ML_HEADS) & 1) == 1
    log_sig = jnp.minimum(g, 0.0) - jnp.log(1.0 + jnp.exp(-jnp.abs(g)))
    gate_ref[0] = jnp.where(is_forget, log_sig, g)


def _inproj(xs, modtab, g, cos, sin, ws, gate_b, n_ctx):
    b, t, d = xs.shape
    tm = ROW_TILE
    nt = t // tm
    ctx_tiles = n_ctx // tm
    row = lambda w: pl.BlockSpec((1, tm, w), lambda bi, i: (bi, i, 0))
    full = lambda a: pl.BlockSpec(a.shape, lambda bi, i: (0,) * a.ndim)
    widths = [w.shape[1] for w in ws]
    widths[2] *= 2
    out_dtypes = [BF16] * 6 + [F32]
    return pl.pallas_call(
        _inproj_kernel,
        grid=(b, nt),
        in_specs=[row(d),
                  pl.BlockSpec((1, 8, d), lambda bi, i: (2 * bi + (i >= ctx_tiles).astype(jnp.int32), 0, 0)),
                  full(g),
                  pl.BlockSpec((tm, LANES), lambda bi, i: (i, 0)),
                  pl.BlockSpec((tm, LANES), lambda bi, i: (i, 0))]
                 + [full(w) for w in ws] + [full(gate_b)],
        out_specs=[row(w) for w in widths],
        out_shape=[jax.ShapeDtypeStruct((b, t, w), dt) for w, dt in zip(widths, out_dtypes)],
        compiler_params=_params("parallel", "parallel"),
        name="inproj",
    )(xs, modtab, g, cos, sin, *ws, gate_b)


def _split3(x):
    x1 = x.astype(BF16)
    r1 = x - x1.astype(F32)
    x2 = r1.astype(BF16)
    x3 = (r1 - x2.astype(F32)).astype(BF16)
    return x1, x2, x3


def _mlprep_kernel(cur_ref, prev_ref, next_ref, cw_ref, gate_ref,
                   mq_ref, mk_ref, gc_ref, gr_ref, *, ctx_tiles, n_tiles):
    i = pl.program_id(1)
    cur = cur_ref[0].astype(F32)
    tm, w = cur.shape
    prev_ok = i != ctx_tiles
    if ctx_tiles > 0:
        prev_ok = jnp.logical_and(prev_ok, i != 0)
        next_ok = jnp.logical_and(i != ctx_tiles - 1, i != n_tiles - 1)
    else:
        next_ok = i != n_tiles - 1
    prev_row = jnp.where(prev_ok, prev_ref[0, 7:8, :].astype(F32), 0.0)
    next_row = jnp.where(next_ok, next_ref[0, 0:1, :].astype(F32), 0.0)
    ridx = lax.broadcasted_iota(jnp.int32, (tm, w), 0)
    before = jnp.where(ridx == 0, prev_row, pltpu.roll(cur, 1, 0))
    after = jnp.where(ridx == tm - 1, next_row, pltpu.roll(cur, tm - 1, 0))
    y = _silu(before * cw_ref[0:1, :] + cur * cw_ref[1:2, :] + after * cw_ref[2:3, :])
    half = w // 2
    mq_ref[0] = y[:, :half].astype(BF16)
    mk_ref[0] = (y[:, half:] * (ML_DIM ** -0.5)).astype(BF16)

    n_g = 4 * ML_HEADS
    r = lax.broadcasted_iota(jnp.int32, (ML_CHUNK, ML_CHUNK), 0)
    c = lax.broadcasted_iota(jnp.int32, (ML_CHUNK, ML_CHUNK), 1)
    lower = (c <= r).astype(BF16)
    upper = (c >= r).astype(BF16)
    for ci in range(tm // ML_CHUNK):
        gch = gate_ref[0, ci * ML_CHUNK:(ci + 1) * ML_CHUNK, :]
        gm = jnp.where(c < n_g, gch, 0.0)
        pre = sum(_dot(lower, p) for p in _split3(pltpu.roll(gm, n_g, 1)))
        suf = sum(_dot(upper, p) for p in _split3(pltpu.roll(gm, 2 * n_g, 1)))
        col = gm + pre + suf
        gc_ref[0, ci * ML_CHUNK:(ci + 1) * ML_CHUNK, :] = col
        gr_ref[0, ci] = col.T


def _mlprep(mqk, conv_w8, gates, n_ctx):
    b, t, w = mqk.shape
    tm = ROW_TILE
    nt = t // tm
    sub = tm // 8
    nsub = t // 8
    cpt = tm // ML_CHUNK
    kern = functools.partial(_mlprep_kernel, ctx_tiles=n_ctx // tm, n_tiles=nt)
    return pl.pallas_call(
        kern,
        grid=(b, nt),
        in_specs=[pl.BlockSpec((1, tm, w), lambda bi, i: (bi, i, 0)),
                  pl.BlockSpec((1, 8, w), lambda bi, i: (bi, jnp.maximum(i * sub - 1, 0), 0)),
                  pl.BlockSpec((1, 8, w), lambda bi, i: (bi, jnp.minimum((i + 1) * sub, nsub - 1), 0)),
                  pl.BlockSpec(conv_w8.shape, lambda bi, i: (0, 0)),
                  pl.BlockSpec((1, tm, LANES), lambda bi, i: (bi, i, 0))],
        out_specs=[pl.BlockSpec((1, tm, w // 2), lambda bi, i: (bi, i, 0)),
                   pl.BlockSpec((1, tm, w // 2), lambda bi, i: (bi, i, 0)),
                   pl.BlockSpec((1, tm, LANES), lambda bi, i: (bi, i, 0)),
                   pl.BlockSpec((1, cpt, ML_CHUNK, LANES), lambda bi, i: (bi, i, 0, 0))],
        out_shape=[jax.ShapeDtypeStruct((b, t, w // 2), BF16),
                   jax.ShapeDtypeStruct((b, t, w // 2), BF16),
                   jax.ShapeDtypeStruct((b, t, LANES), F32),
                   jax.ShapeDtypeStruct((b, t // ML_CHUNK, ML_CHUNK, LANES), F32)],
        compiler_params=_params("parallel", "parallel"),
        name="mlstm_prep",
    )(mqk, mqk, mqk, conv_w8, gates)


def _mlstm_kernel(qf_ref, kf_ref, vf_ref, gcf_ref, grf_ref, qb_ref, kb_ref, vb_ref, gcb_ref, grb_ref,
                  hf_ref, hb_ref, state_ref, m_ref):
    s = pl.program_id(0)
    nb = qf_ref.shape[0]
    n_g = 4 * ML_HEADS

    @pl.when(s == 0)
    def _():
        state_ref[...] = jnp.zeros(state_ref.shape, F32)
        m_ref[...] = jnp.zeros(m_ref.shape, F32)

    qi = lax.broadcasted_iota(jnp.int32, (ML_CHUNK, ML_CHUNK), 0)
    ki = lax.broadcasted_iota(jnp.int32, (ML_CHUNK, ML_CHUNK), 1)
    ones_col = (ki == 0).astype(BF16)

    for bi in range(nb):
        for direction in range(2):
            q_ref, k_ref, v_ref, gc_ref, gr_ref, h_ref = (
                (qf_ref, kf_ref, vf_ref, gcf_ref, grf_ref, hf_ref) if direction == 0 else
                (qb_ref, kb_ref, vb_ref, gcb_ref, grb_ref, hb_ref))
            visible = (ki <= qi) if direction == 0 else (ki >= qi)
            gc = gc_ref[bi]
            gr = gr_ref[bi, 0]
            for hd in range(ML_HEADS):
                chain = (bi * 2 + direction) * ML_HEADS + hd
                c_li = 2 * direction * ML_HEADS + hd
                c_lf = c_li + ML_HEADS
                c_cs = c_lf + (n_g if direction == 0 else 2 * n_g)
                lo, hi = hd * ML_DIM, (hd + 1) * ML_DIM
                q = q_ref[bi, :, lo:hi]
                k = k_ref[bi, :, lo:hi]
                v_aug = jnp.concatenate([v_ref[bi, :, lo:hi], ones_col], axis=1)
                li_col = gc[:, c_li:c_li + 1]
                cs_col = gc[:, c_cs:c_cs + 1]
                li_row = gr[c_li:c_li + 1, :]
                lf_row = gr[c_lf:c_lf + 1, :]
                cs_row = gr[c_cs:c_cs + 1, :]
                m_st = m_ref[chain][:, 0:1]
                st = state_ref[chain]

                b_last = jnp.sum(lf_row, axis=1, keepdims=True)
                logw = b_last - cs_col + li_col
                m_new = jnp.maximum(b_last + m_st, jnp.max(logw, axis=0, keepdims=True))
                wgt = jnp.exp(logw - m_new)
                decay = jnp.exp(b_last + m_st - m_new)

                log_d = jnp.where(visible, cs_col - cs_row + li_row, -jnp.inf)
                m_inter = cs_col + m_st
                m_t = jnp.maximum(m_inter, jnp.max(log_d, axis=1, keepdims=True))
                dmat = jnp.exp(log_d - m_t)
                inter = jnp.exp(m_inter - m_t)

                sc = (_dot_nt(q, k) * dmat).astype(BF16)
                tot = inter * _dot(q, st.astype(BF16)) + _dot(sc, v_aug)
                den = jnp.maximum(jnp.abs(tot[:, ML_DIM:ML_DIM + 1]), jnp.exp(-m_t))
                h_ref[bi, :, lo:hi] = tot[:, :ML_DIM] / den

                kw_t = (k.astype(F32) * wgt).T.astype(BF16)
                state_ref[chain] = decay * st + _dot(kw_t, v_aug)
                m_ref[chain] = jnp.broadcast_to(m_new, (1, LANES))


def _mlstm(mq, mk, mv, gc, gr, n_ctx):
    b, t, w = mq.shape
    nc = t // ML_CHUNK
    ncc = n_ctx // ML_CHUNK
    fwd = lambda s: s
    bwd = lambda s: jnp.where(s < ncc, ncc - 1 - s, nc - 1 - s + ncc)
    tok = lambda f: pl.BlockSpec((b, ML_CHUNK, w), lambda s: (0, f(s), 0))
    gcs = lambda f: pl.BlockSpec((b, ML_CHUNK, LANES), lambda s: (0, f(s), 0))
    grs = lambda f: pl.BlockSpec((b, 1, ML_CHUNK, LANES), lambda s: (0, f(s), 0, 0))
    n_chain = b * 2 * ML_HEADS
    return pl.pallas_call(
        _mlstm_kernel,
        grid=(nc,),
        in_specs=[tok(fwd), tok(fwd), tok(fwd), gcs(fwd), grs(fwd),
                  tok(bwd), tok(bwd), tok(bwd), gcs(bwd), grs(bwd)],
        out_specs=[tok(fwd), tok(bwd)],
        out_shape=[jax.ShapeDtypeStruct((b, t, w), F32)] * 2,
        scratch_shapes=[pltpu.VMEM((n_chain, ML_DIM, 2 * ML_DIM), F32),
                        pltpu.VMEM((n_chain, 1, LANES), F32)],
        compiler_params=_params("arbitrary"),
        name="mlstm",
    )(mq, mk, mv, gc, gr, mq, mk, mv, gc, gr)


KEY_BLOCK = 1024


def _attn_kernel(q_ref, k_ref, v_ref, lam_ref, sg_ref, o_ref, qm_scr, mx_scr, acc_scr,
                 *, ctx_tiles, n_ctx, n_blocks, lam_init):
    i = pl.program_id(2)
    tq = q_ref.shape[1]
    dv = DA_V_DIM
    q = q_ref[0]
    lane = lax.broadcasted_iota(jnp.int32, q.shape, 1)
    zero = jnp.zeros_like(q)
    qm_scr[0] = jnp.where(lane < DA_QK_DIM, q, zero)
    qm_scr[1] = jnp.where(lane >= DA_QK_DIM, q, zero)
    is_latent = i >= ctx_tiles
    align = math.gcd(n_ctx, KEY_BLOCK)

    def scores(mp, start, size):
        return _dot_nt(qm_scr[mp], k_ref[0, pl.ds(start, size), :])

    def lane_max(sc):
        m = sc[:, 0:LANES]
        for t in range(1, sc.shape[1] // LANES):
            m = jnp.maximum(m, sc[:, t * LANES:(t + 1) * LANES])
        return m

    def weighted_values(mp, start, size):
        sc = scores(mp, start, size)
        m = mx_scr[mp]
        p = jnp.concatenate([jnp.exp2(sc[:, t * LANES:(t + 1) * LANES] - m)
                             for t in range(size // LANES)], axis=1)
        return _dot(p.astype(BF16), v_ref[0, pl.ds(start, size), :])

    for mp in range(2):
        mx_scr[mp] = lane_max(scores(mp, 0, n_ctx))

    @pl.when(is_latent)
    def _():
        def body(j, carry):
            start = pl.multiple_of(n_ctx + j * KEY_BLOCK, align)
            for mp in range(2):
                mx_scr[mp] = jnp.maximum(mx_scr[mp], lane_max(scores(mp, start, KEY_BLOCK)))
            return carry
        lax.fori_loop(0, n_blocks, body, 0)

    for mp in range(2):
        mx_scr[mp] = jnp.broadcast_to(jnp.max(mx_scr[mp], axis=1, keepdims=True), (tq, LANES))
        acc_scr[mp] = weighted_values(mp, 0, n_ctx)

    @pl.when(is_latent)
    def _():
        def body(j, carry):
            start = pl.multiple_of(n_ctx + j * KEY_BLOCK, align)
            for mp in range(2):
                acc_scr[mp] += weighted_values(mp, start, KEY_BLOCK)
            return carry
        lax.fori_loop(0, n_blocks, body, 0)

    lv = lam_ref[...]
    dot01 = jnp.sum(lv[0:1, :] * lv[1:2, :], axis=1, keepdims=True)
    dot23 = jnp.sum(lv[2:3, :] * lv[3:4, :], axis=1, keepdims=True)
    lam = jnp.exp(dot01) - jnp.exp(dot23) + lam_init
    a0 = acc_scr[0]
    a1 = acc_scr[1]
    o = a0[:, 0:dv] / a0[:, dv:dv + 1] - lam * (a1[:, 0:dv] / a1[:, dv:dv + 1])
    o = o * lax.rsqrt(jnp.mean(o * o, axis=-1, keepdims=True) + NORM_EPS) * sg_ref[...]
    o_ref[0] = (o * (1.0 - lam_init)).astype(BF16)


def _diff_attention(q, k, v_aug, lam8, subln_g, n_ctx, lam_init):
    b, t, w = q.shape
    tq = ROW_TILE
    n_lat = t - n_ctx
    assert n_lat % KEY_BLOCK == 0 and n_ctx % tq == 0 and n_ctx % LANES == 0
    kern = functools.partial(_attn_kernel, ctx_tiles=n_ctx // tq, n_ctx=n_ctx,
                             n_blocks=n_lat // KEY_BLOCK, lam_init=lam_init)
    return pl.pallas_call(
        kern,
        grid=(b, DA_HEADS, t // tq),
        in_specs=[pl.BlockSpec((1, tq, LANES), lambda bi, h, i: (bi, i, h)),
                  pl.BlockSpec((1, t, LANES), lambda bi, h, i: (bi, 0, h)),
                  pl.BlockSpec((1, t, 2 * DA_V_DIM), lambda bi, h, i: (bi, 0, h)),
                  pl.BlockSpec(lam8.shape, lambda bi, h, i: (0, 0)),
                  pl.BlockSpec(subln_g.shape, lambda bi, h, i: (0, 0))],
        out_specs=pl.BlockSpec((1, tq, LANES), lambda bi, h, i: (bi, i, h)),
        out_shape=jax.ShapeDtypeStruct((b, t, w), BF16),
        scratch_shapes=[pltpu.VMEM((2, tq, LANES), BF16),
                        pltpu.VMEM((2, tq, LANES), F32),
                        pltpu.VMEM((2, tq, 2 * DA_V_DIM), F32)],
        compiler_params=_params("parallel", "parallel", "arbitrary"),
        name="diff_attention",
    )(q, k, v_aug, lam8, subln_g)


def _top2_sum(a, b, c, d):
    hi1, lo1 = jnp.maximum(a, b), jnp.minimum(a, b)
    hi2, lo2 = jnp.maximum(c, d), jnp.minimum(c, d)
    return jnp.maximum(hi1, hi2) + jnp.maximum(jnp.minimum(hi1, hi2), jnp.maximum(lo1, lo2))


def _route(f, rwt_ref, rb_ref):
    tm = f.shape[0]
    aff = _sigmoid(_dot_nt(rwt_ref[...], f, HI))
    biased = aff + rb_ref[:, 0:1]
    bz = [biased[e:e + 1, :] for e in range(N_EXPERTS)]
    af = [aff[e:e + 1, :] for e in range(N_EXPERTS)]
    n_grp = N_EXPERTS // EXPERTS_PER_GROUP
    scores = [_top2_sum(*bz[EXPERTS_PER_GROUP * g:EXPERTS_PER_GROUP * (g + 1)]) for g in range(n_grp)]
    best = scores[0]
    sel_grp = jnp.zeros_like(best, dtype=jnp.int32)
    for g in range(1, n_grp):
        better = scores[g] > best
        sel_grp = jnp.where(better, g, sel_grp)
        best = jnp.where(better, scores[g], best)
    chosen = []
    for e in range(N_EXPERTS):
        g = e // EXPERTS_PER_GROUP
        rank = jnp.zeros_like(sel_grp)
        for o in range(EXPERTS_PER_GROUP * g, EXPERTS_PER_GROUP * (g + 1)):
            if o == e:
                continue
            beats = (bz[o] > bz[e]) if o > e else (bz[o] >= bz[e])
            rank = rank + beats.astype(jnp.int32)
        chosen.append(jnp.logical_and(sel_grp == g, rank < 2))
    denom = sum(jnp.where(chosen[e], af[e], 0.0) for e in range(N_EXPERTS))
    row = lax.broadcasted_iota(jnp.int32, (LANES, tm), 0)
    comb_t = jnp.zeros((LANES, tm), F32)
    for e in range(N_EXPERTS):
        comb_t = jnp.where(jnp.logical_and(row == e, chosen[e]), af[e] / denom, comb_t)
    return comb_t.T


def _post_kernel(*refs, even):
    if even:
        (x_ref, a_ref, hf_ref, hb_ref, og_ref, mod_ref, wa_ref, wm_ref, gf_ref, rwt_ref, rb_ref,
         xo_ref, f_ref, comb_ref) = refs
        m = ((hf_ref[0] + hb_ref[0]) * og_ref[0].astype(F32)).astype(BF16)
        o = _dot(a_ref[0], wa_ref[...]) + _dot(m, wm_ref[...])
    else:
        x_ref, a_ref, mod_ref, wa_ref, gf_ref, rwt_ref, rb_ref, xo_ref, f_ref, comb_ref = refs
        o = _dot(a_ref[0].astype(BF16), wa_ref[...])
    x = x_ref[0] + mod_ref[0, 2:3, :] * o
    xo_ref[0] = x
    f = _norm_mod(x, gf_ref[...], mod_ref[0, 3:4, :], mod_ref[0, 4:5, :])
    f_ref[0] = f.astype(BF16)
    comb_ref[0] = _route(f, rwt_ref, rb_ref)


def _post_mixer(x, acts, weights, modtab, gffn, rwt, rb, n_ctx, x_row_off, even):
    b, t, _ = acts[0].shape
    d = x.shape[2]
    tm = ROW_TILE
    nt = t // tm
    ctx_tiles = n_ctx // tm
    off = x_row_off // tm
    full = lambda a: pl.BlockSpec(a.shape, lambda bi, i: (0,) * a.ndim)
    row = lambda w: pl.BlockSpec((1, tm, w), lambda bi, i: (bi, i, 0))
    mod_spec = pl.BlockSpec(
        (1, 8, d), lambda bi, i: (2 * bi + (i + off >= ctx_tiles).astype(jnp.int32), 0, 0))
    in_specs = ([pl.BlockSpec((1, tm, d), lambda bi, i: (bi, i + off, 0))]
                + [row(a.shape[2]) for a in acts] + [mod_spec]
                + [full(w) for w in weights] + [full(gffn), full(rwt), full(rb)])
    return pl.pallas_call(
        functools.partial(_post_kernel, even=even),
        grid=(b, nt),
        in_specs=in_specs,
        out_specs=[row(d), row(d), row(LANES)],
        out_shape=[jax.ShapeDtypeStruct((b, t, d), F32), jax.ShapeDtypeStruct((b, t, d), BF16),
                   jax.ShapeDtypeStruct((b, t, LANES), F32)],
        compiler_params=_params("parallel", "parallel"),
        name="post_mixer_even" if even else "post_mixer_odd",
    )(x, *acts, modtab, *weights, gffn, rwt, rb)


def _moe_kernel(f_ref, comb_ref, wg_ref, wu_ref, wd_ref, x_ref, modc_ref, modl_ref, fg_ref,
                o_ref, acc_ref, *, n_ctx, final_norm):
    i = pl.program_id(1)
    e = pl.program_id(2)
    tm = f_ref.shape[1]

    @pl.when(e == 0)
    def _():
        acc_ref[...] = jnp.zeros(acc_ref.shape, F32)

    fb = f_ref[0]
    he = _silu(_dot(fb, wg_ref[0])) * _dot(fb, wu_ref[0])
    d_e = he.shape[1]
    pick = (lax.broadcasted_iota(jnp.int32, (LANES, d_e), 0) == e).astype(BF16)
    comb = comb_ref[0]
    comb_hi = comb.astype(BF16)
    comb_lo = (comb - comb_hi.astype(F32)).astype(BF16)
    weight = _dot(comb_hi, pick) + _dot(comb_lo, pick)
    acc_ref[...] += _dot((he * weight).astype(BF16), wd_ref[0])

    @pl.when(e == pl.num_programs(2) - 1)
    def _():
        rows = i * tm + lax.broadcasted_iota(jnp.int32, (tm, 1), 0)
        gate = jnp.where(rows < n_ctx, modc_ref[0, 5:6, :], modl_ref[0, 5:6, :])
        y = x_ref[0] + gate * acc_ref[...]
        if final_norm:
            y = y * lax.rsqrt(jnp.mean(y * y, axis=-1, keepdims=True) + NORM_EPS) * fg_ref[...]
        o_ref[0] = y


def _moe_tile(t):
    for tm in (640, 512, 384, 256, 128):
        if t % tm == 0:
            return tm
    raise ValueError(f"unsupported token count {t}")


def _moe(f, comb, wg, wu, wd, xmid, modtab, final_g, n_ctx, final_norm):
    b, t, d = f.shape
    n_e, _, d_e = wg.shape
    tm = _moe_tile(t)
    row = lambda w: pl.BlockSpec((1, tm, w), lambda bi, i, e: (bi, i, 0))
    return pl.pallas_call(
        functools.partial(_moe_kernel, n_ctx=n_ctx, final_norm=final_norm),
        grid=(b, t // tm, n_e),
        in_specs=[row(d), row(LANES),
                  pl.BlockSpec((1, d, d_e), lambda bi, i, e: (e, 0, 0)),
                  pl.BlockSpec((1, d, d_e), lambda bi, i, e: (e, 0, 0)),
                  pl.BlockSpec((1, d_e, d), lambda bi, i, e: (e, 0, 0)),
                  row(d),
                  pl.BlockSpec((1, 8, d), lambda bi, i, e: (2 * bi, 0, 0)),
                  pl.BlockSpec((1, 8, d), lambda bi, i, e: (2 * bi + 1, 0, 0)),
                  pl.BlockSpec(final_g.shape, lambda bi, i, e: (0, 0))],
        out_specs=row(d),
        out_shape=jax.ShapeDtypeStruct((b, t, d), F32),
        scratch_shapes=[pltpu.VMEM((tm, d), F32)],
        compiler_params=_params("parallel", "parallel", "arbitrary"),
        name="moe",
    )(f, comb, wg, wu, wd, xmid, modtab, modtab, final_g)


def _chan_dft_kernel(x_ref, mod_ref, g_ref, w_ref, zr_ref, zi_ref):
    h = _norm_mod(x_ref[0], g_ref[...], mod_ref[0, 0:1, :], mod_ref[0, 1:2, :])
    gd = w_ref.shape[0]
    for gi in range(h.shape[1] // gd):
        z = _dot(h[:, gi * gd:(gi + 1) * gd], w_ref[...], HI)
        zr_ref[0, :, gi * gd:(gi + 1) * gd] = z[:, :gd]
        zi_ref[0, :, gi * gd:(gi + 1) * gd] = z[:, gd:]


def _chan_dft(x, modtab, g, w_cs, x_row_off, t):
    b, _, d = x.shape
    tm = ROW_TILE
    off = x_row_off // tm
    row = pl.BlockSpec((1, tm, d), lambda bi, i: (bi, i, 0))
    return pl.pallas_call(
        _chan_dft_kernel,
        grid=(b, t // tm),
        in_specs=[pl.BlockSpec((1, tm, d), lambda bi, i: (bi, i + off, 0)),
                  pl.BlockSpec((1, 8, d), lambda bi, i: (2 * bi + 1, 0, 0)),
                  pl.BlockSpec(g.shape, lambda bi, i: (0, 0)),
                  pl.BlockSpec(w_cs.shape, lambda bi, i: (0, 0))],
        out_specs=[row, row],
        out_shape=[jax.ShapeDtypeStruct((b, t, d), F32)] * 2,
        compiler_params=_params("parallel", "parallel"),
        name="chan_dft",
    )(x, modtab, g, w_cs)


def _dft1_kernel(zr_ref, zi_ref, w_ref, yr_ref, yi_ref):
    n1 = zr_ref.shape[1]
    y = _dot(w_ref[...], jnp.concatenate([zr_ref[0], zi_ref[0]], axis=0), HI)
    yr_ref[0] = y[:n1]
    yi_ref[0] = y[n1:]


def _dft1(zr, zi, w1):
    b, n1, cols = zr.shape
    tn = min(cols, 4096)
    blk = pl.BlockSpec((1, n1, tn), lambda bi, j: (bi, 0, j))
    return pl.pallas_call(
        _dft1_kernel,
        grid=(b, cols // tn),
        in_specs=[blk, blk, pl.BlockSpec(w1.shape, lambda bi, j: (0, 0))],
        out_specs=[blk, blk],
        out_shape=[jax.ShapeDtypeStruct(zr.shape, F32)] * 2,
        compiler_params=_params("parallel", "parallel"),
        name="dft_stage1",
    )(zr, zi, w1)


def _dft2_kernel(yr_ref, yi_ref, tab_ref, o_ref):
    y = jnp.concatenate([yr_ref[0, 0], yi_ref[0, 0]], axis=0)
    o_ref[0, 0] = _dot(tab_ref[0], y, HI)


def _dft2(yr, yi, tab):
    b, n1, n2, d = yr.shape
    blk = pl.BlockSpec((1, 1, n2, d), lambda bi, k1: (bi, k1, 0, 0))
    return pl.pallas_call(
        _dft2_kernel,
        grid=(b, n1),
        in_specs=[blk, blk, pl.BlockSpec((1, n2, 2 * n2), lambda bi, k1: (k1, 0, 0))],
        out_specs=blk,
        out_shape=jax.ShapeDtypeStruct(yr.shape, F32),
        compiler_params=_params("parallel", "parallel"),
        name="dft_stage2",
    )(yr, yi, tab)


def _dft_tables(t, gd):
    n2 = ML_CHUNK
    n1 = t // n2
    def cs(num, den):
        ang = (2.0 * np.pi / den) * (num % den).astype(np.float64)
        return np.cos(ang), np.sin(ang)
    c = np.arange(gd)
    cc, sc = cs(np.outer(c, c), gd)
    w_cs = np.concatenate([cc, -sc], axis=1)
    a = np.arange(n1)
    c1, s1 = cs(np.outer(a, a), n1)
    w1 = np.block([[c1, s1], [-s1, c1]])
    k = a[:, None, None] + n1 * np.arange(n2)[None, :, None]
    c2, s2 = cs(k * np.arange(n2)[None, None, :], t)
    tab = np.concatenate([c2, s2], axis=2) / math.sqrt(t * gd)
    return (jnp.asarray(w_cs, F32), jnp.asarray(w1, F32), jnp.asarray(tab, F32))


def _rope_tables(n_ctx, n_lat):
    pos = jnp.arange(n_lat, dtype=jnp.int32)
    n_axis = DA_QK_DIM // 4
    inv = ROPE_BASE ** (-jnp.arange(n_axis, dtype=F32) / n_axis)
    ang = jnp.concatenate([(pos // GRID_W).astype(F32)[:, None] * inv,
                           (pos % GRID_W).astype(F32)[:, None] * inv], axis=-1)
    cos, sin = jnp.cos(ang), jnp.sin(ang)
    cos = jnp.concatenate([jnp.ones((n_ctx, 2 * n_axis), F32), cos], axis=0)
    sin = jnp.concatenate([jnp.zeros((n_ctx, 2 * n_axis), F32), sin], axis=0)
    cos128 = jnp.concatenate([cos, cos, cos, cos], axis=1)
    sin128 = jnp.concatenate([-sin, sin, -sin, sin], axis=1)
    return cos128, sin128


def _deinterleave(w):
    d, n = w.shape
    w = w.reshape(d, n // DA_QK_DIM, DA_QK_DIM // 2, 2)
    return jnp.concatenate([w[..., 0], w[..., 1]], axis=-1).reshape(d, n)


def _pad_rows(a, rows):
    return jnp.concatenate([a, jnp.zeros((rows - a.shape[0],) + a.shape[1:], a.dtype)], axis=0)


def _pad_cols(a, cols):
    return jnp.concatenate([a, jnp.zeros(a.shape[:-1] + (cols - a.shape[-1],), a.dtype)], axis=-1)


def kernel(x, c, ctx, c_ctx, ada_w, ada_b, norm_mix_g, norm_ffn_g, even_w_in, even_w_out,
           even_conv_w, even_gate_b, even_lam, even_subln_g, odd_w_fnet, router_w, router_b,
           exp_w_gate, exp_w_up, exp_w_down, final_g):
    b, n_lat, d = x.shape
    n_ctx = ctx.shape[1]
    depth = ada_w.shape[0]
    assert depth == 2 and b + 1 <= 8
    assert n_ctx % ROW_TILE == 0 and n_lat % ROW_TILE == 0

    cond8 = _pad_rows(jnp.concatenate([c_ctx[None, :], c], axis=0), 8)
    rwt = router_w.T
    rb = jnp.broadcast_to(router_b[:, None], (N_EXPERTS, LANES))
    row2 = lambda v: v.reshape(1, -1)

    def modtab_for(layer):
        mods = _ada_mods(cond8, ada_w[layer], ada_b[layer]).reshape(8, 6, d)
        mods = jnp.concatenate([mods, jnp.zeros((8, 2, d), F32)], axis=1)
        idx = np.array([[0, 1 + bi] for bi in range(b)]).reshape(-1)
        return mods[idx]

    xs = jnp.concatenate([ctx, x], axis=1)
    modtab = modtab_for(0)
    w_in = even_w_in[0]
    o1 = DA_HEADS * 2 * DA_QK_DIM
    o2 = 2 * o1
    o3 = o2 + DA_HEADS * DA_V_DIM
    o4 = o3 + 2 * ML_HEADS * ML_DIM
    o5 = o4 + ML_HEADS * ML_DIM
    o6 = o5 + ML_HEADS * ML_DIM
    ws = [(_deinterleave(w_in[:, :o1]) * (DA_QK_DIM ** -0.5)).astype(BF16),
          _deinterleave(w_in[:, o1:o2]).astype(BF16),
          w_in[:, o2:o3].astype(BF16), w_in[:, o3:o4].astype(BF16),
          w_in[:, o4:o5].astype(BF16), w_in[:, o5:o6].astype(BF16),
          _pad_cols(w_in[:, o6:], LANES).astype(BF16)]
    gate_b = _pad_cols(even_gate_b[0].reshape(1, -1), LANES)
    cos128, sin128 = _rope_tables(n_ctx, n_lat)
    daq, dak, dav, mqk, mv, og, gates = _inproj(xs, modtab, row2(norm_mix_g[0]), cos128, sin128,
                                                 ws, gate_b, n_ctx)
    mq, mk, gc, gr = _mlprep(mqk, _pad_rows(even_conv_w[0], 8), gates, n_ctx)
    hf, hb = _mlstm(mq, mk, mv, gc, gr, n_ctx)
    lam_init = 0.8 - 0.6 * math.exp(-0.3 * 0)
    lam8 = _pad_rows(even_lam[0], 8)
    att = _diff_attention(daq, dak, dav, lam8, row2(even_subln_g[0]), n_ctx, lam_init)
    w_out = even_w_out[0].astype(BF16)
    half = DA_HEADS * DA_V_DIM
    xmid, f, comb = _post_mixer(xs, [att, hf, hb, og], [w_out[:half], w_out[half:]], modtab,
                                row2(norm_ffn_g[0]), rwt, rb, n_ctx, 0, True)
    xs = _moe(f, comb, exp_w_gate[0].astype(BF16), exp_w_up[0].astype(BF16),
              exp_w_down[0].astype(BF16), xmid, modtab, row2(final_g), n_ctx, False)

    modtab = modtab_for(1)
    gd = d // FN_GROUPS
    w_cs, w1, tab = _dft_tables(n_lat, gd)
    n2 = ML_CHUNK
    n1 = n_lat // n2
    zr, zi = _chan_dft(xs, modtab, row2(norm_mix_g[1]), w_cs, n_ctx, n_lat)
    yr, yi = _dft1(zr.reshape(b, n1, n2 * d), zi.reshape(b, n1, n2 * d), w1)
    fo = _dft2(yr.reshape(b, n1, n2, d), yi.reshape(b, n1, n2, d), tab)
    fo = fo.transpose(0, 2, 1, 3).reshape(b, n_lat, d)
    xmid, f, comb = _post_mixer(xs, [fo], [odd_w_fnet[0].astype(BF16)], modtab,
                                row2(norm_ffn_g[1]), rwt, rb, 0, n_ctx, False)
    return _moe(f, comb, exp_w_gate[1].astype(BF16), exp_w_up[1].astype(BF16),
                exp_w_down[1].astype(BF16), xmid, modtab, row2(final_g), 0, True)
```

```python
import functools
import math

import jax
import jax.numpy as jnp
import numpy as np
from jax import lax
from jax.experimental import pallas as pl
from jax.experimental.pallas import tpu as pltpu

F32 = jnp.float32
BF16 = jnp.bfloat16

NORM_EPS = 1e-6
GRID_W = 64
DA_HEADS = 4
DA_QK_DIM = 64
DA_V_DIM = 128
ML_HEADS = 4
ML_DIM = 128
ML_CHUNK = 128
FN_GROUPS = 4
N_EXPERTS = 16
EXPERTS_PER_GROUP = 4
ROPE_BASE = 10000.0
LANES = 128
ROW_TILE = 256
VMEM_LIMIT_BYTES = 56 * 1024 * 1024
HI = lax.Precision.HIGHEST
LOG2_E = math.log2(math.e)


def _params(*sem):
    return pltpu.CompilerParams(dimension_semantics=sem, vmem_limit_bytes=VMEM_LIMIT_BYTES)


def _dot(a, b, precision=None):
    return jnp.dot(a, b, preferred_element_type=F32, precision=precision)


def _dot_nt(a, b, precision=None):
    return lax.dot_general(a, b, (((1,), (1,)), ((), ())), preferred_element_type=F32,
                           precision=precision)


def _sigmoid(x):
    return 1.0 / (1.0 + jnp.exp(-x))


def _silu(x):
    return x * _sigmoid(x)


def _norm_mod(x, g, shift, scale):
    y = x * lax.rsqrt(jnp.mean(x * x, axis=-1, keepdims=True) + NORM_EPS) * g
    return y * (1.0 + scale) + shift


def _ada_kernel(c_ref, w_ref, b_ref, o_ref):
    o_ref[...] = _dot(_silu(c_ref[...]), w_ref[...], HI) + b_ref[...]


def _ada_mods(cond8, w, b):
    d, n = w.shape
    tn = n // 6
    return pl.pallas_call(
        _ada_kernel,
        grid=(6,),
        in_specs=[pl.BlockSpec((8, d), lambda j: (0, 0)),
                  pl.BlockSpec((d, tn), lambda j: (0, j)),
                  pl.BlockSpec((1, tn), lambda j: (0, j))],
        out_specs=pl.BlockSpec((8, tn), lambda j: (0, j)),
        out_shape=jax.ShapeDtypeStruct((8, n), F32),
        compiler_params=_params("arbitrary"),
        name="ada_mods",
    )(cond8, w, b.reshape(1, n))


def _inproj_kernel(x_ref, mod_ref, g_ref, cos_ref, sin_ref, wq_ref, wk_ref, wv_ref, wmqk_ref,
                   wmv_ref, wmo_ref, wg_ref, gb_ref,
                   q_ref, k_ref, v_ref, mqk_ref, mv_ref, og_ref, gate_ref):
    x = x_ref[0]
    h = _norm_mod(x, g_ref[...], mod_ref[0, 0:1, :], mod_ref[0, 1:2, :]).astype(BF16)
    tm = x.shape[0]
    width = q_ref.shape[2]
    cos = jnp.concatenate([cos_ref[...]] * (width // LANES), axis=1)
    sin = jnp.concatenate([sin_ref[...]] * (width // LANES), axis=1)
    lane = lax.broadcasted_iota(jnp.int32, (tm, width), 1)
    lower = (lane & (DA_QK_DIM - 1)) < (DA_QK_DIM // 2)

    def rope(u):
        swapped = jnp.where(lower, pltpu.roll(u, width - DA_QK_DIM // 2, 1),
                            pltpu.roll(u, DA_QK_DIM // 2, 1))
        return u * cos + swapped * sin

    q_ref[0] = (rope(_dot(h, wq_ref[...])) * LOG2_E).astype(BF16)
    k_ref[0] = rope(_dot(h, wk_ref[...])).astype(BF16)
    v = _dot(h, wv_ref[...]).astype(BF16)
    ones_col = (lax.broadcasted_iota(jnp.int32, (tm, DA_V_DIM), 1) == 0).astype(BF16)
    v_ref[0] = jnp.concatenate(
        [blk for hd in range(DA_HEADS) for blk in (v[:, hd * DA_V_DIM:(hd + 1) * DA_V_DIM], ones_col)],
        axis=1)
    mqk_ref[0] = _dot(h, wmqk_ref[...]).astype(BF16)
    mv_ref[0] = _dot(h, wmv_ref[...]).astype(BF16)
    og_ref[0] = _sigmoid(_dot(h, wmo_ref[...])).astype(BF16)
    g = _dot(h, wg_ref[...]) + gb_ref[...]
    glane = lax.broadcasted_iota(jnp.int32, g.shape, 1)
    is_forget = ((glane // ML_HEADS) & 1) == 1
    log_sig = jnp.minimum(g, 0.0) - jnp.log(1.0 + jnp.exp(-jnp.abs(g)))
    gate_ref[0] = jnp.where(is_forget, log_sig, g)


def _inproj(xs, modtab, g, cos, sin, ws, gate_b, n_ctx):
    b, t, d = xs.shape
    tm = ROW_TILE
    nt = t // tm
    ctx_tiles = n_ctx // tm
    row = lambda w: pl.BlockSpec((1, tm, w), lambda bi, i: (bi, i, 0))
    full = lambda a: pl.BlockSpec(a.shape, lambda bi, i: (0,) * a.ndim)
    widths = [w.shape[1] for w in ws]
    widths[2] *= 2
    out_dtypes = [BF16] * 6 + [F32]
    return pl.pallas_call(
        _inproj_kernel,
        grid=(b, nt),
        in_specs=[row(d),
                  pl.BlockSpec((1, 8, d), lambda bi, i: (2 * bi + (i >= ctx_tiles).astype(jnp.int32), 0, 0)),
                  full(g),
                  pl.BlockSpec((tm, LANES), lambda bi, i: (i, 0)),
                  pl.BlockSpec((tm, LANES), lambda bi, i: (i, 0))]
                 + [full(w) for w in ws] + [full(gate_b)],
        out_specs=[row(w) for w in widths],
        out_shape=[jax.ShapeDtypeStruct((b, t, w), dt) for w, dt in zip(widths, out_dtypes)],
        compiler_params=_params("parallel", "parallel"),
        name="inproj",
    )(xs, modtab, g, cos, sin, *ws, gate_b)


def _split3(x):
    x1 = x.astype(BF16)
    r1 = x - x1.astype(F32)
    x2 = r1.astype(BF16)
    x3 = (r1 - x2.astype(F32)).astype(BF16)
    return x1, x2, x3


def _mlprep_kernel(cur_ref, prev_ref, next_ref, cw_ref, gate_ref,
                   mq_ref, mk_ref, gc_ref, gr_ref, *, ctx_tiles, n_tiles):
    i = pl.program_id(1)
    cur = cur_ref[0].astype(F32)
    tm, w = cur.shape
    prev_ok = i != ctx_tiles
    if ctx_tiles > 0:
        prev_ok = jnp.logical_and(prev_ok, i != 0)
        next_ok = jnp.logical_and(i != ctx_tiles - 1, i != n_tiles - 1)
    else:
        next_ok = i != n_tiles - 1
    prev_row = jnp.where(prev_ok, prev_ref[0, 7:8, :].astype(F32), 0.0)
    next_row = jnp.where(next_ok, next_ref[0, 0:1, :].astype(F32), 0.0)
    ridx = lax.broadcasted_iota(jnp.int32, (tm, w), 0)
    before = jnp.where(ridx == 0, prev_row, pltpu.roll(cur, 1, 0))
    after = jnp.where(ridx == tm - 1, next_row, pltpu.roll(cur, tm - 1, 0))
    y = _silu(before * cw_ref[0:1, :] + cur * cw_ref[1:2, :] + after * cw_ref[2:3, :])
    half = w // 2
    mq_ref[0] = y[:, :half].astype(BF16)
    mk_ref[0] = (y[:, half:] * (ML_DIM ** -0.5)).astype(BF16)

    n_g = 4 * ML_HEADS
    r = lax.broadcasted_iota(jnp.int32, (ML_CHUNK, ML_CHUNK), 0)
    c = lax.broadcasted_iota(jnp.int32, (ML_CHUNK, ML_CHUNK), 1)
    lower = (c <= r).astype(BF16)
    upper = (c >= r).astype(BF16)
    for ci in range(tm // ML_CHUNK):
        gch = gate_ref[0, ci * ML_CHUNK:(ci + 1) * ML_CHUNK, :]
        gm = jnp.where(c < n_g, gch, 0.0)
        pre = sum(_dot(lower, p) for p in _split3(pltpu.roll(gm, n_g, 1)))
        suf = sum(_dot(upper, p) for p in _split3(pltpu.roll(gm, 2 * n_g, 1)))
        col = gm + pre + suf
        gc_ref[0, ci * ML_CHUNK:(ci + 1) * ML_CHUNK, :] = col
        gr_ref[0, ci] = col.T


def _mlprep(mqk, conv_w8, gates, n_ctx):
    b, t, w = mqk.shape
    tm = ROW_TILE
    nt = t // tm
    sub = tm // 8
    nsub = t // 8
    cpt = tm // ML_CHUNK
    kern = functools.partial(_mlprep_kernel, ctx_tiles=n_ctx // tm, n_tiles=nt)
    return pl.pallas_call(
        kern,
        grid=(b, nt),
        in_specs=[pl.BlockSpec((1, tm, w), lambda bi, i: (bi, i, 0)),
                  pl.BlockSpec((1, 8, w), lambda bi, i: (bi, jnp.maximum(i * sub - 1, 0), 0)),
                  pl.BlockSpec((1, 8, w), lambda bi, i: (bi, jnp.minimum((i + 1) * sub, nsub - 1), 0)),
                  pl.BlockSpec(conv_w8.shape, lambda bi, i: (0, 0)),
                  pl.BlockSpec((1, tm, LANES), lambda bi, i: (bi, i, 0))],
        out_specs=[pl.BlockSpec((1, tm, w // 2), lambda bi, i: (bi, i, 0)),
                   pl.BlockSpec((1, tm, w // 2), lambda bi, i: (bi, i, 0)),
                   pl.BlockSpec((1, tm, LANES), lambda bi, i: (bi, i, 0)),
                   pl.BlockSpec((1, cpt, ML_CHUNK, LANES), lambda bi, i: (bi, i, 0, 0))],
        out_shape=[jax.ShapeDtypeStruct((b, t, w // 2), BF16),
                   jax.ShapeDtypeStruct((b, t, w // 2), BF16),
                   jax.ShapeDtypeStruct((b, t, LANES), F32),
                   jax.ShapeDtypeStruct((b, t // ML_CHUNK, ML_CHUNK, LANES), F32)],
        compiler_params=_params("parallel", "parallel"),
        name="mlstm_prep",
    )(mqk, mqk, mqk, conv_w8, gates)


def _mlstm_kernel(qf_ref, kf_ref, vf_ref, gcf_ref, grf_ref, qb_ref, kb_ref, vb_ref, gcb_ref, grb_ref,
                  hf_ref, hb_ref, state_ref, m_ref):
    s = pl.program_id(0)
    nb = qf_ref.shape[0]
    n_g = 4 * ML_HEADS

    @pl.when(s == 0)
    def _():
        state_ref[...] = jnp.zeros(state_ref.shape, F32)
        m_ref[...] = jnp.zeros(m_ref.shape, F32)

    qi = lax.broadcasted_iota(jnp.int32, (ML_CHUNK, ML_CHUNK), 0)
    ki = lax.broadcasted_iota(jnp.int32, (ML_CHUNK, ML_CHUNK), 1)
    ones_col = (ki == 0).astype(BF16)

    for bi in range(nb):
        for direction in range(2):
            q_ref, k_ref, v_ref, gc_ref, gr_ref, h_ref = (
                (qf_ref, kf_ref, vf_ref, gcf_ref, grf_ref, hf_ref) if direction == 0 else
                (qb_ref, kb_ref, vb_ref, gcb_ref, grb_ref, hb_ref))
            visible = (ki <= qi) if direction == 0 else (ki >= qi)
            gc = gc_ref[bi]
            gr = gr_ref[bi, 0]
            for hd in range(ML_HEADS):
                chain = (bi * 2 + direction) * ML_HEADS + hd
                c_li = 2 * direction * ML_HEADS + hd
                c_lf = c_li + ML_HEADS
                c_cs = c_lf + (n_g if direction == 0 else 2 * n_g)
                lo, hi = hd * ML_DIM, (hd + 1) * ML_DIM
                q = q_ref[bi, :, lo:hi]
                k = k_ref[bi, :, lo:hi]
                v_aug = jnp.concatenate([v_ref[bi, :, lo:hi], ones_col], axis=1)
                li_col = gc[:, c_li:c_li + 1]
                cs_col = gc[:, c_cs:c_cs + 1]
                li_row = gr[c_li:c_li + 1, :]
                lf_row = gr[c_lf:c_lf + 1, :]
                cs_row = gr[c_cs:c_cs + 1, :]
                m_st = m_ref[chain][:, 0:1]
                st = state_ref[chain]

                b_last = jnp.sum(lf_row, axis=1, keepdims=True)
                logw = b_last - cs_col + li_col
                m_new = jnp.maximum(b_last + m_st, jnp.max(logw, axis=0, keepdims=True))
                wgt = jnp.exp(logw - m_new)
                decay = jnp.exp(b_last + m_st - m_new)

                log_d = jnp.where(visible, cs_col - cs_row + li_row, -jnp.inf)
                m_inter = cs_col + m_st
                m_t = jnp.maximum(m_inter, jnp.max(log_d, axis=1, keepdims=True))
                dmat = jnp.exp(log_d - m_t)
                inter = jnp.exp(m_inter - m_t)

                sc = (_dot_nt(q, k) * dmat).astype(BF16)
                tot = inter * _dot(q, st.astype(BF16)) + _dot(sc, v_aug)
                den = jnp.maximum(jnp.abs(tot[:, ML_DIM:ML_DIM + 1]), jnp.exp(-m_t))
                h_ref[bi, :, lo:hi] = tot[:, :ML_DIM] / den

                kw_t = (k.astype(F32) * wgt).T.astype(BF16)
                state_ref[chain] = decay * st + _dot(kw_t, v_aug)
                m_ref[chain] = jnp.broadcast_to(m_new, (1, LANES))


def _mlstm(mq, mk, mv, gc, gr, n_ctx):
    b, t, w = mq.shape
    nc = t // ML_CHUNK
    ncc = n_ctx // ML_CHUNK
    fwd = lambda s: s
    bwd = lambda s: jnp.where(s < ncc, ncc - 1 - s, nc - 1 - s + ncc)
    tok = lambda f: pl.BlockSpec((b, ML_CHUNK, w), lambda s: (0, f(s), 0))
    gcs = lambda f: pl.BlockSpec((b, ML_CHUNK, LANES), lambda s: (0, f(s), 0))
    grs = lambda f: pl.BlockSpec((b, 1, ML_CHUNK, LANES), lambda s: (0, f(s), 0, 0))
    n_chain = b * 2 * ML_HEADS
    return pl.pallas_call(
        _mlstm_kernel,
        grid=(nc,),
        in_specs=[tok(fwd), tok(fwd), tok(fwd), gcs(fwd), grs(fwd),
                  tok(bwd), tok(bwd), tok(bwd), gcs(bwd), grs(bwd)],
        out_specs=[tok(fwd), tok(bwd)],
        out_shape=[jax.ShapeDtypeStruct((b, t, w), F32)] * 2,
        scratch_shapes=[pltpu.VMEM((n_chain, ML_DIM, 2 * ML_DIM), F32),
                        pltpu.VMEM((n_chain, 1, LANES), F32)],
        compiler_params=_params("arbitrary"),
        name="mlstm",
    )(mq, mk, mv, gc, gr, mq, mk, mv, gc, gr)


def _attn_kernel(q_ref, k_ref, v_ref, lam_ref, sg_ref, o_ref, qm_scr, *scr,
                 ctx_tiles, n_ctx, key_block, n_blocks, lam_init):
    s_scr = (scr[0:2], scr[2:4])
    bm_scr = (scr[4:6], scr[6:8])
    m_scr = scr[8:10]
    acc_scr = scr[10:12]
    i = pl.program_id(2)
    dv = DA_V_DIM
    q = q_ref[0]
    lane = lax.broadcasted_iota(jnp.int32, q.shape, 1)
    zero = jnp.zeros_like(q)
    qm_scr[0] = jnp.where(lane < DA_QK_DIM, q, zero)
    qm_scr[1] = jnp.where(lane >= DA_QK_DIM, q, zero)
    align = math.gcd(n_ctx, key_block)

    def scores(mp, start, size):
        return _dot_nt(qm_scr[mp], k_ref[0, pl.ds(start, size), :])

    def lane_max(sc):
        m = sc[:, 0:LANES]
        for t in range(1, sc.shape[1] // LANES):
            m = jnp.maximum(m, sc[:, t * LANES:(t + 1) * LANES])
        return m

    def row_max(bm):
        return jnp.broadcast_to(jnp.max(bm, axis=1, keepdims=True), bm.shape)

    def weights(sc, m):
        return jnp.concatenate([jnp.exp2(sc[:, t * LANES:(t + 1) * LANES] - m)
                                for t in range(sc.shape[1] // LANES)], axis=1).astype(BF16)

    for mp in range(2):
        sc = scores(mp, 0, n_ctx)
        m = row_max(lane_max(sc))
        m_scr[mp][...] = m
        acc_scr[mp][...] = _dot(weights(sc, m), v_ref[0, 0:n_ctx, :])

    def stage(j, slot):
        start = pl.multiple_of(n_ctx + j * key_block, align)
        for mp in range(2):
            sc = scores(mp, start, key_block)
            s_scr[mp][slot][...] = sc
            bm_scr[mp][slot][...] = lane_max(sc)

    def consume(j, slot):
        start = pl.multiple_of(n_ctx + j * key_block, align)
        for mp in range(2):
            m_old = m_scr[mp][...]
            m_new = jnp.maximum(m_old, row_max(bm_scr[mp][slot][...]))
            alpha = jnp.exp2(m_old - m_new)
            pv = _dot(weights(s_scr[mp][slot][...], m_new), v_ref[0, pl.ds(start, key_block), :])
            acc_scr[mp][...] = jnp.concatenate([alpha, alpha], axis=1) * acc_scr[mp][...] + pv
            m_scr[mp][...] = m_new

    @pl.when(i >= ctx_tiles)
    def _():
        stage(0, 0)

        def body(g, carry):
            stage(2 * g + 1, 1)
            consume(2 * g, 0)
            stage(2 * g + 2, 0)
            consume(2 * g + 1, 1)
            return carry
        lax.fori_loop(0, n_blocks // 2 - 1, body, 0)
        stage(n_blocks - 1, 1)
        consume(n_blocks - 2, 0)
        consume(n_blocks - 1, 1)

    lv = lam_ref[...]
    dot01 = jnp.sum(lv[0:1, :] * lv[1:2, :], axis=1, keepdims=True)
    dot23 = jnp.sum(lv[2:3, :] * lv[3:4, :], axis=1, keepdims=True)
    lam = jnp.exp(dot01) - jnp.exp(dot23) + lam_init
    a0 = acc_scr[0][...]
    a1 = acc_scr[1][...]
    o = a0[:, 0:dv] / a0[:, dv:dv + 1] - lam * (a1[:, 0:dv] / a1[:, dv:dv + 1])
    o = o * lax.rsqrt(jnp.mean(o * o, axis=-1, keepdims=True) + NORM_EPS) * sg_ref[...]
    o_ref[0] = (o * (1.0 - lam_init)).astype(BF16)


def _diff_attention(q, k, v_aug, lam8, subln_g, n_ctx, lam_init):
    b, t, w = q.shape
    tq = ROW_TILE
    n_lat = t - n_ctx
    key_block = 1024 if n_lat % 2048 == 0 else 512
    n_blocks = n_lat // key_block
    assert n_lat % (2 * key_block) == 0 and n_ctx % tq == 0 and n_ctx % LANES == 0
    kern = functools.partial(_attn_kernel, ctx_tiles=n_ctx // tq, n_ctx=n_ctx, key_block=key_block,
                             n_blocks=n_blocks, lam_init=lam_init)
    return pl.pallas_call(
        kern,
        grid=(b, DA_HEADS, t // tq),
        in_specs=[pl.BlockSpec((1, tq, LANES), lambda bi, h, i: (bi, i, h)),
                  pl.BlockSpec((1, t, LANES), lambda bi, h, i: (bi, 0, h)),
                  pl.BlockSpec((1, t, 2 * DA_V_DIM), lambda bi, h, i: (bi, 0, h)),
                  pl.BlockSpec(lam8.shape, lambda bi, h, i: (0, 0)),
                  pl.BlockSpec(subln_g.shape, lambda bi, h, i: (0, 0))],
        out_specs=pl.BlockSpec((1, tq, LANES), lambda bi, h, i: (bi, i, h)),
        out_shape=jax.ShapeDtypeStruct((b, t, w), BF16),
        scratch_shapes=[pltpu.VMEM((2, tq, LANES), BF16)]
                       + [pltpu.VMEM((tq, key_block), F32)] * 4
                       + [pltpu.VMEM((tq, LANES), F32)] * 6
                       + [pltpu.VMEM((tq, 2 * DA_V_DIM), F32)] * 2,
        compiler_params=_params("parallel", "parallel", "arbitrary"),
        name="diff_attention",
    )(q, k, v_aug, lam8, subln_g)


def _top2_sum(a, b, c, d):
    hi1, lo1 = jnp.maximum(a, b), jnp.minimum(a, b)
    hi2, lo2 = jnp.maximum(c, d), jnp.minimum(c, d)
    return jnp.maximum(hi1, hi2) + jnp.maximum(jnp.minimum(hi1, hi2), jnp.maximum(lo1, lo2))


def _route(f, rwt_ref, rb_ref):
    tm = f.shape[0]
    aff = _sigmoid(_dot_nt(rwt_ref[...], f, HI))
    biased = aff + rb_ref[:, 0:1]
    bz = [biased[e:e + 1, :] for e in range(N_EXPERTS)]
    af = [aff[e:e + 1, :] for e in range(N_EXPERTS)]
    n_grp = N_EXPERTS // EXPERTS_PER_GROUP
    scores = [_top2_sum(*bz[EXPERTS_PER_GROUP * g:EXPERTS_PER_GROUP * (g + 1)]) for g in range(n_grp)]
    best = scores[0]
    sel_grp = jnp.zeros_like(best, dtype=jnp.int32)
    for g in range(1, n_grp):
        better = scores[g] > best
        sel_grp = jnp.where(better, g, sel_grp)
        best = jnp.where(better, scores[g], best)
    chosen = []
    for e in range(N_EXPERTS):
        g = e // EXPERTS_PER_GROUP
        rank = jnp.zeros_like(sel_grp)
        for o in range(EXPERTS_PER_GROUP * g, EXPERTS_PER_GROUP * (g + 1)):
            if o == e:
                continue
            beats = (bz[o] > bz[e]) if o > e else (bz[o] >= bz[e])
            rank = rank + beats.astype(jnp.int32)
        chosen.append(jnp.logical_and(sel_grp == g, rank < 2))
    denom = sum(jnp.where(chosen[e], af[e], 0.0) for e in range(N_EXPERTS))
    row = lax.broadcasted_iota(jnp.int32, (LANES, tm), 0)
    comb_t = jnp.zeros((LANES, tm), F32)
    for e in range(N_EXPERTS):
        comb_t = jnp.where(jnp.logical_and(row == e, chosen[e]), af[e] / denom, comb_t)
    return comb_t.T


def _post_kernel(*refs, even):
    if even:
        (x_ref, a_ref, hf_ref, hb_ref, og_ref, mod_ref, wa_ref, wm_ref, gf_ref, rwt_ref, rb_ref,
         xo_ref, f_ref, comb_ref) = refs
        m = ((hf_ref[0] + hb_ref[0]) * og_ref[0].astype(F32)).astype(BF16)
        o = _dot(a_ref[0], wa_ref[...]) + _dot(m, wm_ref[...])
    else:
        x_ref, a_ref, mod_ref, wa_ref, gf_ref, rwt_ref, rb_ref, xo_ref, f_ref, comb_ref = refs
        o = _dot(a_ref[0].astype(BF16), wa_ref[...])
    x = x_ref[0] + mod_ref[0, 2:3, :] * o
    xo_ref[0] = x
    f = _norm_mod(x, gf_ref[...], mod_ref[0, 3:4, :], mod_ref[0, 4:5, :])
    f_ref[0] = f.astype(BF16)
    comb_ref[0] = _route(f, rwt_ref, rb_ref)


def _post_mixer(x, acts, weights, modtab, gffn, rwt, rb, n_ctx, x_row_off, even):
    b, t, _ = acts[0].shape
    d = x.shape[2]
    tm = ROW_TILE
    nt = t // tm
    ctx_tiles = n_ctx // tm
    off = x_row_off // tm
    full = lambda a: pl.BlockSpec(a.shape, lambda bi, i: (0,) * a.ndim)
    row = lambda w: pl.BlockSpec((1, tm, w), lambda bi, i: (bi, i, 0))
    mod_spec = pl.BlockSpec(
        (1, 8, d), lambda bi, i: (2 * bi + (i + off >= ctx_tiles).astype(jnp.int32), 0, 0))
    in_specs = ([pl.BlockSpec((1, tm, d), lambda bi, i: (bi, i + off, 0))]
                + [row(a.shape[2]) for a in acts] + [mod_spec]
                + [full(w) for w in weights] + [full(gffn), full(rwt), full(rb)])
    return pl.pallas_call(
        functools.partial(_post_kernel, even=even),
        grid=(b, nt),
        in_specs=in_specs,
        out_specs=[row(d), row(d), row(LANES)],
        out_shape=[jax.ShapeDtypeStruct((b, t, d), F32), jax.ShapeDtypeStruct((b, t, d), BF16),
                   jax.ShapeDtypeStruct((b, t, LANES), F32)],
        compiler_params=_params("parallel", "parallel"),
        name="post_mixer_even" if even else "post_mixer_odd",
    )(x, *acts, modtab, *weights, gffn, rwt, rb)


def _moe_kernel(f_ref, comb_ref, wg_ref, wu_ref, wd_ref, x_ref, modc_ref, modl_ref, fg_ref,
                o_ref, acc_ref, *, n_ctx, final_norm):
    i = pl.program_id(1)
    e = pl.program_id(2)
    tm = f_ref.shape[1]

    @pl.when(e == 0)
    def _():
        acc_ref[...] = jnp.zeros(acc_ref.shape, F32)

    fb = f_ref[0]
    he = _silu(_dot(fb, wg_ref[0])) * _dot(fb, wu_ref[0])
    d_e = he.shape[1]
    pick = (lax.broadcasted_iota(jnp.int32, (LANES, d_e), 0) == e).astype(BF16)
    comb = comb_ref[0]
    comb_hi = comb.astype(BF16)
    comb_lo = (comb - comb_hi.astype(F32)).astype(BF16)
    weight = _dot(comb_hi, pick) + _dot(comb_lo, pick)
    acc_ref[...] += _dot((he * weight).astype(BF16), wd_ref[0])

    @pl.when(e == pl.num_programs(2) - 1)
    def _():
        rows = i * tm + lax.broadcasted_iota(jnp.int32, (tm, 1), 0)
        gate = jnp.where(rows < n_ctx, modc_ref[0, 5:6, :], modl_ref[0, 5:6, :])
        y = x_ref[0] + gate * acc_ref[...]
        if final_norm:
            y = y * lax.rsqrt(jnp.mean(y * y, axis=-1, keepdims=True) + NORM_EPS) * fg_ref[...]
        o_ref[0] = y


def _moe_tile(t):
    for tm in (640, 512, 384, 256, 128):
        if t % tm == 0:
            return tm
    raise ValueError(f"unsupported token count {t}")


def _moe(f, comb, wg, wu, wd, xmid, modtab, final_g, n_ctx, final_norm):
    b, t, d = f.shape
    n_e, _, d_e = wg.shape
    tm = _moe_tile(t)
    row = lambda w: pl.BlockSpec((1, tm, w), lambda bi, i, e: (bi, i, 0))
    return pl.pallas_call(
        functools.partial(_moe_kernel, n_ctx=n_ctx, final_norm=final_norm),
        grid=(b, t // tm, n_e),
        in_specs=[row(d), row(LANES),
                  pl.BlockSpec((1, d, d_e), lambda bi, i, e: (e, 0, 0)),
                  pl.BlockSpec((1, d, d_e), lambda bi, i, e: (e, 0, 0)),
                  pl.BlockSpec((1, d_e, d), lambda bi, i, e: (e, 0, 0)),
                  row(d),
                  pl.BlockSpec((1, 8, d), lambda bi, i, e: (2 * bi, 0, 0)),
                  pl.BlockSpec((1, 8, d), lambda bi, i, e: (2 * bi + 1, 0, 0)),
                  pl.BlockSpec(final_g.shape, lambda bi, i, e: (0, 0))],
        out_specs=row(d),
        out_shape=jax.ShapeDtypeStruct((b, t, d), F32),
        scratch_shapes=[pltpu.VMEM((tm, d), F32)],
        compiler_params=_params("parallel", "parallel", "arbitrary"),
        name="moe",
    )(f, comb, wg, wu, wd, xmid, modtab, modtab, final_g)


def _chan_dft_kernel(x_ref, mod_ref, g_ref, w_ref, zr_ref, zi_ref):
    h = _norm_mod(x_ref[0], g_ref[...], mod_ref[0, 0:1, :], mod_ref[0, 1:2, :])
    gd = w_ref.shape[0]
    for gi in range(h.shape[1] // gd):
        z = _dot(h[:, gi * gd:(gi + 1) * gd], w_ref[...], HI)
        zr_ref[0, :, gi * gd:(gi + 1) * gd] = z[:, :gd]
        zi_ref[0, :, gi * gd:(gi + 1) * gd] = z[:, gd:]


def _chan_dft(x, modtab, g, w_cs, x_row_off, t):
    b, _, d = x.shape
    tm = ROW_TILE
    off = x_row_off // tm
    row = pl.BlockSpec((1, tm, d), lambda bi, i: (bi, i, 0))
    return pl.pallas_call(
        _chan_dft_kernel,
        grid=(b, t // tm),
        in_specs=[pl.BlockSpec((1, tm, d), lambda bi, i: (bi, i + off, 0)),
                  pl.BlockSpec((1, 8, d), lambda bi, i: (2 * bi + 1, 0, 0)),
                  pl.BlockSpec(g.shape, lambda bi, i: (0, 0)),
                  pl.BlockSpec(w_cs.shape, lambda bi, i: (0, 0))],
        out_specs=[row, row],
        out_shape=[jax.ShapeDtypeStruct((b, t, d), F32)] * 2,
        compiler_params=_params("parallel", "parallel"),
        name="chan_dft",
    )(x, modtab, g, w_cs)


def _dft1_kernel(zr_ref, zi_ref, w_ref, yr_ref, yi_ref):
    n1 = zr_ref.shape[1]
    y = _dot(w_ref[...], jnp.concatenate([zr_ref[0], zi_ref[0]], axis=0), HI)
    yr_ref[0] = y[:n1]
    yi_ref[0] = y[n1:]


def _dft1(zr, zi, w1):
    b, n1, cols = zr.shape
    tn = min(cols, 4096)
    blk = pl.BlockSpec((1, n1, tn), lambda bi, j: (bi, 0, j))
    return pl.pallas_call(
        _dft1_kernel,
        grid=(b, cols // tn),
        in_specs=[blk, blk, pl.BlockSpec(w1.shape, lambda bi, j: (0, 0))],
        out_specs=[blk, blk],
        out_shape=[jax.ShapeDtypeStruct(zr.shape, F32)] * 2,
        compiler_params=_params("parallel", "parallel"),
        name="dft_stage1",
    )(zr, zi, w1)


def _dft2_kernel(yr_ref, yi_ref, tab_ref, o_ref):
    y = jnp.concatenate([yr_ref[0, 0], yi_ref[0, 0]], axis=0)
    o_ref[0, 0] = _dot(tab_ref[0], y, HI)


def _dft2(yr, yi, tab):
    b, n1, n2, d = yr.shape
    blk = pl.BlockSpec((1, 1, n2, d), lambda bi, k1: (bi, k1, 0, 0))
    return pl.pallas_call(
        _dft2_kernel,
        grid=(b, n1),
        in_specs=[blk, blk, pl.BlockSpec((1, n2, 2 * n2), lambda bi, k1: (k1, 0, 0))],
        out_specs=blk,
        out_shape=jax.ShapeDtypeStruct(yr.shape, F32),
        compiler_params=_params("parallel", "parallel"),
        name="dft_stage2",
    )(yr, yi, tab)


def _dft_tables(t, gd):
    n2 = ML_CHUNK
    n1 = t // n2
    def cs(num, den):
        ang = (2.0 * np.pi / den) * (num % den).astype(np.float64)
        return np.cos(ang), np.sin(ang)
    c = np.arange(gd)
    cc, sc = cs(np.outer(c, c), gd)
    w_cs = np.concatenate([cc, -sc], axis=1)
    a = np.arange(n1)
    c1, s1 = cs(np.outer(a, a), n1)
    w1 = np.block([[c1, s1], [-s1, c1]])
    k = a[:, None, None] + n1 * np.arange(n2)[None, :, None]
    c2, s2 = cs(k * np.arange(n2)[None, None, :], t)
    tab = np.concatenate([c2, s2], axis=2) / math.sqrt(t * gd)
    return (jnp.asarray(w_cs, F32), jnp.asarray(w1, F32), jnp.asarray(tab, F32))


def _rope_tables(n_ctx, n_lat):
    pos = jnp.arange(n_lat, dtype=jnp.int32)
    n_axis = DA_QK_DIM // 4
    inv = ROPE_BASE ** (-jnp.arange(n_axis, dtype=F32) / n_axis)
    ang = jnp.concatenate([(pos // GRID_W).astype(F32)[:, None] * inv,
                           (pos % GRID_W).astype(F32)[:, None] * inv], axis=-1)
    cos, sin = jnp.cos(ang), jnp.sin(ang)
    cos = jnp.concatenate([jnp.ones((n_ctx, 2 * n_axis), F32), cos], axis=0)
    sin = jnp.concatenate([jnp.zeros((n_ctx, 2 * n_axis), F32), sin], axis=0)
    cos128 = jnp.concatenate([cos, cos, cos, cos], axis=1)
    sin128 = jnp.concatenate([-sin, sin, -sin, sin], axis=1)
    return cos128, sin128


def _deinterleave(w):
    d, n = w.shape
    w = w.reshape(d, n // DA_QK_DIM, DA_QK_DIM // 2, 2)
    return jnp.concatenate([w[..., 0], w[..., 1]], axis=-1).reshape(d, n)


def _pad_rows(a, rows):
    return jnp.concatenate([a, jnp.zeros((rows - a.shape[0],) + a.shape[1:], a.dtype)], axis=0)


def _pad_cols(a, cols):
    return jnp.concatenate([a, jnp.zeros(a.shape[:-1] + (cols - a.shape[-1],), a.dtype)], axis=-1)


def kernel(x, c, ctx, c_ctx, ada_w, ada_b, norm_mix_g, norm_ffn_g, even_w_in, even_w_out,
           even_conv_w, even_gate_b, even_lam, even_subln_g, odd_w_fnet, router_w, router_b,
           exp_w_gate, exp_w_up, exp_w_down, final_g):
    b, n_lat, d = x.shape
    n_ctx = ctx.shape[1]
    depth = ada_w.shape[0]
    assert depth == 2 and b + 1 <= 8
    assert n_ctx % ROW_TILE == 0 and n_lat % ROW_TILE == 0

    cond8 = _pad_rows(jnp.concatenate([c_ctx[None, :], c], axis=0), 8)
    rwt = router_w.T
    rb = jnp.broadcast_to(router_b[:, None], (N_EXPERTS, LANES))
    row2 = lambda v: v.reshape(1, -1)

    def modtab_for(layer):
        mods = _ada_mods(cond8, ada_w[layer], ada_b[layer]).reshape(8, 6, d)
        mods = jnp.concatenate([mods, jnp.zeros((8, 2, d), F32)], axis=1)
        idx = np.array([[0, 1 + bi] for bi in range(b)]).reshape(-1)
        return mods[idx]

    xs = jnp.concatenate([ctx, x], axis=1)
    modtab = modtab_for(0)
    w_in = even_w_in[0]
    o1 = DA_HEADS * 2 * DA_QK_DIM
    o2 = 2 * o1
    o3 = o2 + DA_HEADS * DA_V_DIM
    o4 = o3 + 2 * ML_HEADS * ML_DIM
    o5 = o4 + ML_HEADS * ML_DIM
    o6 = o5 + ML_HEADS * ML_DIM
    ws = [(_deinterleave(w_in[:, :o1]) * (DA_QK_DIM ** -0.5)).astype(BF16),
          _deinterleave(w_in[:, o1:o2]).astype(BF16),
          w_in[:, o2:o3].astype(BF16), w_in[:, o3:o4].astype(BF16),
          w_in[:, o4:o5].astype(BF16), w_in[:, o5:o6].astype(BF16),
          _pad_cols(w_in[:, o6:], LANES).astype(BF16)]
    gate_b = _pad_cols(even_gate_b[0].reshape(1, -1), LANES)
    cos128, sin128 = _rope_tables(n_ctx, n_lat)
    daq, dak, dav, mqk, mv, og, gates = _inproj(xs, modtab, row2(norm_mix_g[0]), cos128, sin128,
                                                 ws, gate_b, n_ctx)
    mq, mk, gc, gr = _mlprep(mqk, _pad_rows(even_conv_w[0], 8), gates, n_ctx)
    hf, hb = _mlstm(mq, mk, mv, gc, gr, n_ctx)
    lam_init = 0.8 - 0.6 * math.exp(-0.3 * 0)
    lam8 = _pad_rows(even_lam[0], 8)
    att = _diff_attention(daq, dak, dav, lam8, row2(even_subln_g[0]), n_ctx, lam_init)
    w_out = even_w_out[0].astype(BF16)
    half = DA_HEADS * DA_V_DIM
    xmid, f, comb = _post_mixer(xs, [att, hf, hb, og], [w_out[:half], w_out[half:]], modtab,
                                row2(norm_ffn_g[0]), rwt, rb, n_ctx, 0, True)
    xs = _moe(f, comb, exp_w_gate[0].astype(BF16), exp_w_up[0].astype(BF16),
              exp_w_down[0].astype(BF16), xmid, modtab, row2(final_g), n_ctx, False)

    modtab = modtab_for(1)
    gd = d // FN_GROUPS
    w_cs, w1, tab = _dft_tables(n_lat, gd)
    n2 = ML_CHUNK
    n1 = n_lat // n2
    zr, zi = _chan_dft(xs, modtab, row2(norm_mix_g[1]), w_cs, n_ctx, n_lat)
    yr, yi = _dft1(zr.reshape(b, n1, n2 * d), zi.reshape(b, n1, n2 * d), w1)
    fo = _dft2(yr.reshape(b, n1, n2, d), yi.reshape(b, n1, n2, d), tab)
    fo = fo.transpose(0, 2, 1, 3).reshape(b, n_lat, d)
    xmid, f, comb = _post_mixer(xs, [fo], [odd_w_fnet[0].astype(BF16)], modtab,
                                row2(norm_ffn_g[1]), rwt, rb, 0, n_ctx, False)
    return _moe(f, comb, exp_w_gate[1].astype(BF16), exp_w_up[1].astype(BF16),
                exp_w_down[1].astype(BF16), xmid, modtab, row2(final_g), 0, True)
```

```python
import functools
import math

import jax
import jax.numpy as jnp
import numpy as np
from jax import lax
from jax.experimental import pallas as pl
from jax.experimental.pallas import tpu as pltpu

F32 = jnp.float32
BF16 = jnp.bfloat16

NORM_EPS = 1e-6
GRID_W = 64
DA_HEADS = 4
DA_QK_DIM = 64
DA_V_DIM = 128
ML_HEADS = 4
ML_DIM = 128
ML_CHUNK = 128
FN_GROUPS = 4
N_EXPERTS = 16
EXPERTS_PER_GROUP = 4
ROPE_BASE = 10000.0
LANES = 128
ROW_TILE = 256
VMEM_LIMIT_BYTES = 56 * 1024 * 1024
HI = lax.Precision.HIGHEST
LOG2_E = math.log2(math.e)


def _params(*sem):
    return pltpu.CompilerParams(dimension_semantics=sem, vmem_limit_bytes=VMEM_LIMIT_BYTES)


def _dot(a, b, precision=None):
    return jnp.dot(a, b, preferred_element_type=F32, precision=precision)


def _dot_nt(a, b, precision=None):
    return lax.dot_general(a, b, (((1,), (1,)), ((), ())), preferred_element_type=F32,
                           precision=precision)


def _sigmoid(x):
    return 1.0 / (1.0 + jnp.exp(-x))


def _silu(x):
    return x * _sigmoid(x)


def _norm_mod(x, g, shift, scale):
    y = x * lax.rsqrt(jnp.mean(x * x, axis=-1, keepdims=True) + NORM_EPS) * g
    return y * (1.0 + scale) + shift


def _ada_kernel(c_ref, w_ref, b_ref, o_ref):
    o_ref[...] = _dot(_silu(c_ref[...]), w_ref[...], HI) + b_ref[...]


def _ada_mods(cond8, w, b):
    d, n = w.shape
    tn = n // 6
    return pl.pallas_call(
        _ada_kernel,
        grid=(6,),
        in_specs=[pl.BlockSpec((8, d), lambda j: (0, 0)),
                  pl.BlockSpec((d, tn), lambda j: (0, j)),
                  pl.BlockSpec((1, tn), lambda j: (0, j))],
        out_specs=pl.BlockSpec((8, tn), lambda j: (0, j)),
        out_shape=jax.ShapeDtypeStruct((8, n), F32),
        compiler_params=_params("arbitrary"),
        name="ada_mods",
    )(cond8, w, b.reshape(1, n))


def _inproj_kernel(x_ref, mod_ref, g_ref, cos_ref, sin_ref, wq_ref, wk_ref, wv_ref, wmqk_ref,
                   wmv_ref, wmo_ref, wg_ref, gb_ref,
                   q_ref, k_ref, v_ref, mqk_ref, mv_ref, og_ref, gate_ref):
    x = x_ref[0]
    h = _norm_mod(x, g_ref[...], mod_ref[0, 0:1, :], mod_ref[0, 1:2, :]).astype(BF16)
    tm = x.shape[0]
    width = q_ref.shape[2]
    cos = jnp.concatenate([cos_ref[...]] * (width // LANES), axis=1)
    sin = jnp.concatenate([sin_ref[...]] * (width // LANES), axis=1)
    lane = lax.broadcasted_iota(jnp.int32, (tm, width), 1)
    lower = (lane & (DA_QK_DIM - 1)) < (DA_QK_DIM // 2)

    def rope(u):
        swapped = jnp.where(lower, pltpu.roll(u, width - DA_QK_DIM // 2, 1),
                            pltpu.roll(u, DA_QK_DIM // 2, 1))
        return u * cos + swapped * sin

    q_ref[0] = (rope(_dot(h, wq_ref[...])) * LOG2_E).astype(BF16)
    k_ref[0] = rope(_dot(h, wk_ref[...])).astype(BF16)
    v = _dot(h, wv_ref[...]).astype(BF16)
    ones_col = (lax.broadcasted_iota(jnp.int32, (tm, DA_V_DIM), 1) == 0).astype(BF16)
    v_ref[0] = jnp.concatenate(
        [blk for hd in range(DA_HEADS) for blk in (v[:, hd * DA_V_DIM:(hd + 1) * DA_V_DIM], ones_col)],
        axis=1)
    mqk_ref[0] = _dot(h, wmqk_ref[...]).astype(BF16)
    mv_ref[0] = _dot(h, wmv_ref[...]).astype(BF16)
    og_ref[0] = _sigmoid(_dot(h, wmo_ref[...])).astype(BF16)
    g = _dot(h, wg_ref[...]) + gb_ref[...]
    glane = lax.broadcasted_iota(jnp.int32, g.shape, 1)
    is_forget = ((glane // ML_HEADS) & 1) == 1
    log_sig = jnp.minimum(g, 0.0) - jnp.log(1.0 + jnp.exp(-jnp.abs(g)))
    gate_ref[0] = jnp.where(is_forget, log_sig, g)


def _inproj(xs, modtab, g, cos, sin, ws, gate_b, n_ctx):
    b, t, d = xs.shape
    tm = ROW_TILE
    nt = t // tm
    ctx_tiles = n_ctx // tm
    row = lambda w: pl.BlockSpec((1, tm, w), lambda bi, i: (bi, i, 0))
    full = lambda a: pl.BlockSpec(a.shape, lambda bi, i: (0,) * a.ndim)
    widths = [w.shape[1] for w in ws]
    widths[2] *= 2
    out_dtypes = [BF16] * 6 + [F32]
    return pl.pallas_call(
        _inproj_kernel,
        grid=(b, nt),
        in_specs=[row(d),
                  pl.BlockSpec((1, 8, d), lambda bi, i: (2 * bi + (i >= ctx_tiles).astype(jnp.int32), 0, 0)),
                  full(g),
                  pl.BlockSpec((tm, LANES), lambda bi, i: (i, 0)),
                  pl.BlockSpec((tm, LANES), lambda bi, i: (i, 0))]
                 + [full(w) for w in ws] + [full(gate_b)],
        out_specs=[row(w) for w in widths],
        out_shape=[jax.ShapeDtypeStruct((b, t, w), dt) for w, dt in zip(widths, out_dtypes)],
        compiler_params=_params("parallel", "parallel"),
        name="inproj",
    )(xs, modtab, g, cos, sin, *ws, gate_b)


def _split3(x):
    x1 = x.astype(BF16)
    r1 = x - x1.astype(F32)
    x2 = r1.astype(BF16)
    x3 = (r1 - x2.astype(F32)).astype(BF16)
    return x1, x2, x3


def _mlprep_kernel(cur_ref, prev_ref, next_ref, cw_ref, gate_ref,
                   mq_ref, mk_ref, gc_ref, gr_ref, *, ctx_tiles, n_tiles):
    i = pl.program_id(1)
    cur = cur_ref[0].astype(F32)
    tm, w = cur.shape
    prev_ok = i != ctx_tiles
    if ctx_tiles > 0:
        prev_ok = jnp.logical_and(prev_ok, i != 0)
        next_ok = jnp.logical_and(i != ctx_tiles - 1, i != n_tiles - 1)
    else:
        next_ok = i != n_tiles - 1
    prev_row = jnp.where(prev_ok, prev_ref[0, 7:8, :].astype(F32), 0.0)
    next_row = jnp.where(next_ok, next_ref[0, 0:1, :].astype(F32), 0.0)
    ridx = lax.broadcasted_iota(jnp.int32, (tm, w), 0)
    before = jnp.where(ridx == 0, prev_row, pltpu.roll(cur, 1, 0))
    after = jnp.where(ridx == tm - 1, next_row, pltpu.roll(cur, tm - 1, 0))
    y = _silu(before * cw_ref[0:1, :] + cur * cw_ref[1:2, :] + after * cw_ref[2:3, :])
    half = w // 2
    mq_ref[0] = y[:, :half].astype(BF16)
    mk_ref[0] = (y[:, half:] * (ML_DIM ** -0.5)).astype(BF16)

    n_g = 4 * ML_HEADS
    r = lax.broadcasted_iota(jnp.int32, (ML_CHUNK, ML_CHUNK), 0)
    c = lax.broadcasted_iota(jnp.int32, (ML_CHUNK, ML_CHUNK), 1)
    lower = (c <= r).astype(BF16)
    upper = (c >= r).astype(BF16)
    for ci in range(tm // ML_CHUNK):
        gch = gate_ref[0, ci * ML_CHUNK:(ci + 1) * ML_CHUNK, :]
        gm = jnp.where(c < n_g, gch, 0.0)
        pre = sum(_dot(lower, p) for p in _split3(pltpu.roll(gm, n_g, 1)))
        suf = sum(_dot(upper, p) for p in _split3(pltpu.roll(gm, 2 * n_g, 1)))
        col = gm + pre + suf
        gc_ref[0, ci * ML_CHUNK:(ci + 1) * ML_CHUNK, :] = col
        gr_ref[0, ci] = col.T


def _mlprep(mqk, conv_w8, gates, n_ctx):
    b, t, w = mqk.shape
    tm = ROW_TILE
    nt = t // tm
    sub = tm // 8
    nsub = t // 8
    cpt = tm // ML_CHUNK
    kern = functools.partial(_mlprep_kernel, ctx_tiles=n_ctx // tm, n_tiles=nt)
    return pl.pallas_call(
        kern,
        grid=(b, nt),
        in_specs=[pl.BlockSpec((1, tm, w), lambda bi, i: (bi, i, 0)),
                  pl.BlockSpec((1, 8, w), lambda bi, i: (bi, jnp.maximum(i * sub - 1, 0), 0)),
                  pl.BlockSpec((1, 8, w), lambda bi, i: (bi, jnp.minimum((i + 1) * sub, nsub - 1), 0)),
                  pl.BlockSpec(conv_w8.shape, lambda bi, i: (0, 0)),
                  pl.BlockSpec((1, tm, LANES), lambda bi, i: (bi, i, 0))],
        out_specs=[pl.BlockSpec((1, tm, w // 2), lambda bi, i: (bi, i, 0)),
                   pl.BlockSpec((1, tm, w // 2), lambda bi, i: (bi, i, 0)),
                   pl.BlockSpec((1, tm, LANES), lambda bi, i: (bi, i, 0)),
                   pl.BlockSpec((1, cpt, ML_CHUNK, LANES), lambda bi, i: (bi, i, 0, 0))],
        out_shape=[jax.ShapeDtypeStruct((b, t, w // 2), BF16),
                   jax.ShapeDtypeStruct((b, t, w // 2), BF16),
                   jax.ShapeDtypeStruct((b, t, LANES), F32),
                   jax.ShapeDtypeStruct((b, t // ML_CHUNK, ML_CHUNK, LANES), F32)],
        compiler_params=_params("parallel", "parallel"),
        name="mlstm_prep",
    )(mqk, mqk, mqk, conv_w8, gates)


def _mlstm_kernel(qf_ref, kf_ref, vf_ref, gcf_ref, grf_ref, qb_ref, kb_ref, vb_ref, gcb_ref, grb_ref,
                  hf_ref, hb_ref, state_ref, m_ref):
    s = pl.program_id(0)
    nb = qf_ref.shape[0]
    n_g = 4 * ML_HEADS

    @pl.when(s == 0)
    def _():
        state_ref[...] = jnp.zeros(state_ref.shape, F32)
        m_ref[...] = jnp.zeros(m_ref.shape, F32)

    qi = lax.broadcasted_iota(jnp.int32, (ML_CHUNK, ML_CHUNK), 0)
    ki = lax.broadcasted_iota(jnp.int32, (ML_CHUNK, ML_CHUNK), 1)
    ones_col = (ki == 0).astype(BF16)

    for bi in range(nb):
        for direction in range(2):
            q_ref, k_ref, v_ref, gc_ref, gr_ref, h_ref = (
                (qf_ref, kf_ref, vf_ref, gcf_ref, grf_ref, hf_ref) if direction == 0 else
                (qb_ref, kb_ref, vb_ref, gcb_ref, grb_ref, hb_ref))
            visible = (ki <= qi) if direction == 0 else (ki >= qi)
            gc = gc_ref[bi]
            gr = gr_ref[bi, 0]
            for hd in range(ML_HEADS):
                chain = (bi * 2 + direction) * ML_HEADS + hd
                c_li = 2 * direction * ML_HEADS + hd
                c_lf = c_li + ML_HEADS
                c_cs = c_lf + (n_g if direction == 0 else 2 * n_g)
                lo, hi = hd * ML_DIM, (hd + 1) * ML_DIM
                q = q_ref[bi, :, lo:hi]
                k = k_ref[bi, :, lo:hi]
                v_aug = jnp.concatenate([v_ref[bi, :, lo:hi], ones_col], axis=1)
                li_col = gc[:, c_li:c_li + 1]
                cs_col = gc[:, c_cs:c_cs + 1]
                li_row = gr[c_li:c_li + 1, :]
                lf_row = gr[c_lf:c_lf + 1, :]
                cs_row = gr[c_cs:c_cs + 1, :]
                m_st = m_ref[chain][:, 0:1]
                st = state_ref[chain]

                b_last = jnp.sum(lf_row, axis=1, keepdims=True)
                logw = b_last - cs_col + li_col
                m_new = jnp.maximum(b_last + m_st, jnp.max(logw, axis=0, keepdims=True))
                wgt = jnp.exp(logw - m_new)
                decay = jnp.exp(b_last + m_st - m_new)

                log_d = jnp.where(visible, cs_col - cs_row + li_row, -jnp.inf)
                m_inter = cs_col + m_st
                m_t = jnp.maximum(m_inter, jnp.max(log_d, axis=1, keepdims=True))
                dmat = jnp.exp(log_d - m_t)
                inter = jnp.exp(m_inter - m_t)

                sc = (_dot_nt(q, k) * dmat).astype(BF16)
                tot = inter * _dot(q, st.astype(BF16)) + _dot(sc, v_aug)
                den = jnp.maximum(jnp.abs(tot[:, ML_DIM:ML_DIM + 1]), jnp.exp(-m_t))
                h_ref[bi, :, lo:hi] = tot[:, :ML_DIM] / den

                kw_t = (k.astype(F32) * wgt).T.astype(BF16)
                state_ref[chain] = decay * st + _dot(kw_t, v_aug)
                m_ref[chain] = jnp.broadcast_to(m_new, (1, LANES))


def _mlstm(mq, mk, mv, gc, gr, n_ctx):
    b, t, w = mq.shape
    nc = t // ML_CHUNK
    ncc = n_ctx // ML_CHUNK
    fwd = lambda s: s
    bwd = lambda s: jnp.where(s < ncc, ncc - 1 - s, nc - 1 - s + ncc)
    tok = lambda f: pl.BlockSpec((b, ML_CHUNK, w), lambda s: (0, f(s), 0))
    gcs = lambda f: pl.BlockSpec((b, ML_CHUNK, LANES), lambda s: (0, f(s), 0))
    grs = lambda f: pl.BlockSpec((b, 1, ML_CHUNK, LANES), lambda s: (0, f(s), 0, 0))
    n_chain = b * 2 * ML_HEADS
    return pl.pallas_call(
        _mlstm_kernel,
        grid=(nc,),
        in_specs=[tok(fwd), tok(fwd), tok(fwd), gcs(fwd), grs(fwd),
                  tok(bwd), tok(bwd), tok(bwd), gcs(bwd), grs(bwd)],
        out_specs=[tok(fwd), tok(bwd)],
        out_shape=[jax.ShapeDtypeStruct((b, t, w), F32)] * 2,
        scratch_shapes=[pltpu.VMEM((n_chain, ML_DIM, 2 * ML_DIM), F32),
                        pltpu.VMEM((n_chain, 1, LANES), F32)],
        compiler_params=_params("arbitrary"),
        name="mlstm",
    )(mq, mk, mv, gc, gr, mq, mk, mv, gc, gr)


def _attn_kernel(q_ref, k_ref, v_ref, lam_ref, sg_ref, o_ref, qm_scr, *scr,
                 ctx_tiles, n_ctx, key_block, n_blocks, lam_init):
    s_scr = (scr[0:2], scr[2:4])
    bm_scr = (scr[4:6], scr[6:8])
    m_scr = scr[8:10]
    acc_scr = scr[10:12]
    i = pl.program_id(2)
    dv = DA_V_DIM
    q = q_ref[0]
    lane = lax.broadcasted_iota(jnp.int32, q.shape, 1)
    zero = jnp.zeros_like(q)
    qm_scr[0] = jnp.where(lane < DA_QK_DIM, q, zero)
    qm_scr[1] = jnp.where(lane >= DA_QK_DIM, q, zero)
    align = math.gcd(n_ctx, key_block)

    def scores(mp, start, size):
        return _dot_nt(qm_scr[mp], k_ref[0, pl.ds(start, size), :])

    def lane_max(sc):
        m = sc[:, 0:LANES]
        for t in range(1, sc.shape[1] // LANES):
            m = jnp.maximum(m, sc[:, t * LANES:(t + 1) * LANES])
        return m

    def row_max(bm):
        return jnp.broadcast_to(jnp.max(bm, axis=1, keepdims=True), bm.shape)

    def weights(sc, m):
        return jnp.concatenate([jnp.exp2(sc[:, t * LANES:(t + 1) * LANES] - m)
                                for t in range(sc.shape[1] // LANES)], axis=1).astype(BF16)

    for mp in range(2):
        sc = scores(mp, 0, n_ctx)
        m = row_max(lane_max(sc))
        m_scr[mp][...] = m
        acc_scr[mp][...] = _dot(weights(sc, m), v_ref[0, 0:n_ctx, :])

    def stage(j, slot):
        start = pl.multiple_of(n_ctx + j * key_block, align)
        for mp in range(2):
            sc = scores(mp, start, key_block)
            s_scr[mp][slot][...] = sc
            bm_scr[mp][slot][...] = lane_max(sc)

    def consume(j, slot):
        start = pl.multiple_of(n_ctx + j * key_block, align)
        for mp in range(2):
            m_old = m_scr[mp][...]
            m_new = jnp.maximum(m_old, row_max(bm_scr[mp][slot][...]))
            alpha = jnp.exp2(m_old - m_new)
            pv = _dot(weights(s_scr[mp][slot][...], m_new), v_ref[0, pl.ds(start, key_block), :])
            acc_scr[mp][...] = jnp.concatenate([alpha, alpha], axis=1) * acc_scr[mp][...] + pv
            m_scr[mp][...] = m_new

    @pl.when(i >= ctx_tiles)
    def _():
        stage(0, 0)

        def body(g, carry):
            stage(2 * g + 1, 1)
            consume(2 * g, 0)
            stage(2 * g + 2, 0)
            consume(2 * g + 1, 1)
            return carry
        lax.fori_loop(0, n_blocks // 2 - 1, body, 0)
        stage(n_blocks - 1, 1)
        consume(n_blocks - 2, 0)
        consume(n_blocks - 1, 1)

    lv = lam_ref[...]
    dot01 = jnp.sum(lv[0:1, :] * lv[1:2, :], axis=1, keepdims=True)
    dot23 = jnp.sum(lv[2:3, :] * lv[3:4, :], axis=1, keepdims=True)
    lam = jnp.exp(dot01) - jnp.exp(dot23) + lam_init
    a0 = acc_scr[0][...]
    a1 = acc_scr[1][...]
    o = a0[:, 0:dv] / a0[:, dv:dv + 1] - lam * (a1[:, 0:dv] / a1[:, dv:dv + 1])
    o = o * lax.rsqrt(jnp.mean(o * o, axis=-1, keepdims=True) + NORM_EPS) * sg_ref[...]
    o_ref[0] = (o * (1.0 - lam_init)).astype(BF16)


def _diff_attention(q, k, v_aug, lam8, subln_g, n_ctx, lam_init):
    b, t, w = q.shape
    tq = ROW_TILE
    n_lat = t - n_ctx
    key_block = 1024 if n_lat % 2048 == 0 else 512
    n_blocks = n_lat // key_block
    assert n_lat % (2 * key_block) == 0 and n_ctx % tq == 0 and n_ctx % LANES == 0
    kern = functools.partial(_attn_kernel, ctx_tiles=n_ctx // tq, n_ctx=n_ctx, key_block=key_block,
                             n_blocks=n_blocks, lam_init=lam_init)
    return pl.pallas_call(
        kern,
        grid=(b, DA_HEADS, t // tq),
        in_specs=[pl.BlockSpec((1, tq, LANES), lambda bi, h, i: (bi, i, h)),
                  pl.BlockSpec((1, t, LANES), lambda bi, h, i: (bi, 0, h)),
                  pl.BlockSpec((1, t, 2 * DA_V_DIM), lambda bi, h, i: (bi, 0, h)),
                  pl.BlockSpec(lam8.shape, lambda bi, h, i: (0, 0)),
                  pl.BlockSpec(subln_g.shape, lambda bi, h, i: (0, 0))],
        out_specs=pl.BlockSpec((1, tq, LANES), lambda bi, h, i: (bi, i, h)),
        out_shape=jax.ShapeDtypeStruct((b, t, w), BF16),
        scratch_shapes=[pltpu.VMEM((2, tq, LANES), BF16)]
                       + [pltpu.VMEM((tq, key_block), F32)] * 4
                       + [pltpu.VMEM((tq, LANES), F32)] * 6
                       + [pltpu.VMEM((tq, 2 * DA_V_DIM), F32)] * 2,
        compiler_params=_params("parallel", "parallel", "arbitrary"),
        name="diff_attention",
    )(q, k, v_aug, lam8, subln_g)


def _top2_sum(a, b, c, d):
    hi1, lo1 = jnp.maximum(a, b), jnp.minimum(a, b)
    hi2, lo2 = jnp.maximum(c, d), jnp.minimum(c, d)
    return jnp.maximum(hi1, hi2) + jnp.maximum(jnp.minimum(hi1, hi2), jnp.maximum(lo1, lo2))


def _route(f, rwt_ref, rb_ref, cnt_ref):
    tm = f.shape[0]
    aff = _sigmoid(_dot_nt(rwt_ref[...], f, HI))
    biased = aff + rb_ref[:, 0:1]
    bz = [biased[e:e + 1, :] for e in range(N_EXPERTS)]
    af = [aff[e:e + 1, :] for e in range(N_EXPERTS)]
    n_grp = N_EXPERTS // EXPERTS_PER_GROUP
    scores = [_top2_sum(*bz[EXPERTS_PER_GROUP * g:EXPERTS_PER_GROUP * (g + 1)]) for g in range(n_grp)]
    best = scores[0]
    sel_grp = jnp.zeros_like(best, dtype=jnp.int32)
    for g in range(1, n_grp):
        better = scores[g] > best
        sel_grp = jnp.where(better, g, sel_grp)
        best = jnp.where(better, scores[g], best)
    chosen = []
    for e in range(N_EXPERTS):
        g = e // EXPERTS_PER_GROUP
        rank = jnp.zeros_like(sel_grp)
        for o in range(EXPERTS_PER_GROUP * g, EXPERTS_PER_GROUP * (g + 1)):
            if o == e:
                continue
            beats = (bz[o] > bz[e]) if o > e else (bz[o] >= bz[e])
            rank = rank + beats.astype(jnp.int32)
        chosen.append(jnp.logical_and(sel_grp == g, rank < 2))
    denom = sum(jnp.where(chosen[e], af[e], 0.0) for e in range(N_EXPERTS))
    erow = lax.broadcasted_iota(jnp.int32, (N_EXPERTS, tm), 0)
    one_hot = jnp.zeros((N_EXPERTS, tm), F32)
    for e in range(N_EXPERTS):
        one_hot = jnp.where(jnp.logical_and(erow == e, chosen[e]), 1.0, one_hot)
    earlier = (lax.broadcasted_iota(jnp.int32, (tm, tm), 0)
               < lax.broadcasted_iota(jnp.int32, (tm, tm), 1)).astype(BF16)
    rank_all = _dot(one_hot.astype(BF16), earlier) + cnt_ref[:, 0:1]
    cnt_ref[...] = cnt_ref[...] + jnp.sum(one_hot, axis=1, keepdims=True)

    seen = jnp.zeros((1, tm), jnp.bool_)
    e_a = e_b = jnp.zeros((1, tm), jnp.int32)
    r_a = r_b = w_a = w_b = jnp.zeros((1, tm), F32)
    for e in range(N_EXPERTS):
        first = jnp.logical_and(chosen[e], jnp.logical_not(seen))
        second = jnp.logical_and(chosen[e], seen)
        rk = rank_all[e:e + 1, :]
        wt = af[e] / denom
        e_a, e_b = jnp.where(first, e, e_a), jnp.where(second, e, e_b)
        r_a, r_b = jnp.where(first, rk, r_a), jnp.where(second, rk, r_b)
        w_a, w_b = jnp.where(first, wt, w_a), jnp.where(second, wt, w_b)
        seen = jnp.logical_or(seen, chosen[e])
    r8 = lax.broadcasted_iota(jnp.int32, (8, tm), 0)
    sel = jnp.where(r8 == 0, e_a, jnp.where(r8 == 1, e_b, jnp.where(
        r8 == 2, r_a.astype(jnp.int32), jnp.where(r8 == 3, r_b.astype(jnp.int32), 0))))
    row = lax.broadcasted_iota(jnp.int32, (LANES, tm), 0)
    w_t = jnp.where(row == 0, w_a, jnp.where(row == 1, w_b, 0.0))
    return sel, w_t.T


def _post_kernel(*refs, even):
    if even:
        (x_ref, a_ref, hf_ref, hb_ref, og_ref, mod_ref, wa_ref, wm_ref, gf_ref, rwt_ref, rb_ref,
         xo_ref, f_ref, sel_ref, w_ref, cnt_ref, cnt_scr) = refs
        m = ((hf_ref[0] + hb_ref[0]) * og_ref[0].astype(F32)).astype(BF16)
        o = _dot(a_ref[0], wa_ref[...]) + _dot(m, wm_ref[...])
    else:
        (x_ref, a_ref, mod_ref, wa_ref, gf_ref, rwt_ref, rb_ref,
         xo_ref, f_ref, sel_ref, w_ref, cnt_ref, cnt_scr) = refs
        o = _dot(a_ref[0].astype(BF16), wa_ref[...])

    @pl.when(jnp.logical_and(pl.program_id(0) == 0, pl.program_id(1) == 0))
    def _():
        cnt_scr[...] = jnp.zeros(cnt_scr.shape, F32)

    x = x_ref[0] + mod_ref[0, 2:3, :] * o
    xo_ref[0] = x
    f = _norm_mod(x, gf_ref[...], mod_ref[0, 3:4, :], mod_ref[0, 4:5, :])
    f_ref[0] = f
    sel, w_col = _route(f, rwt_ref, rb_ref, cnt_scr)
    sel_ref[0, 0] = sel
    w_ref[0] = w_col
    cnt_ref[...] = cnt_scr[...]


def _post_mixer(x, acts, weights, modtab, gffn, rwt, rb, n_ctx, x_row_off, even):
    b, t, _ = acts[0].shape
    d = x.shape[2]
    tm = ROW_TILE
    nt = t // tm
    ctx_tiles = n_ctx // tm
    off = x_row_off // tm
    full = lambda a: pl.BlockSpec(a.shape, lambda bi, i: (0,) * a.ndim)
    row = lambda w: pl.BlockSpec((1, tm, w), lambda bi, i: (bi, i, 0))
    mod_spec = pl.BlockSpec(
        (1, 8, d), lambda bi, i: (2 * bi + (i + off >= ctx_tiles).astype(jnp.int32), 0, 0))
    in_specs = ([pl.BlockSpec((1, tm, d), lambda bi, i: (bi, i + off, 0))]
                + [row(a.shape[2]) for a in acts] + [mod_spec]
                + [full(w) for w in weights] + [full(gffn), full(rwt), full(rb)])
    return pl.pallas_call(
        functools.partial(_post_kernel, even=even),
        grid=(b, nt),
        in_specs=in_specs,
        out_specs=[row(d), row(d),
                   pl.BlockSpec((1, 1, 8, tm), lambda bi, i: (bi, i, 0, 0)),
                   row(LANES),
                   pl.BlockSpec((N_EXPERTS, LANES), lambda bi, i: (0, 0))],
        out_shape=[jax.ShapeDtypeStruct((b, t, d), F32), jax.ShapeDtypeStruct((b, t, d), F32),
                   jax.ShapeDtypeStruct((b, nt, 8, tm), jnp.int32),
                   jax.ShapeDtypeStruct((b, t, LANES), F32),
                   jax.ShapeDtypeStruct((N_EXPERTS, LANES), F32)],
        scratch_shapes=[pltpu.VMEM((N_EXPERTS, LANES), F32)],
        compiler_params=_params("arbitrary", "arbitrary"),
        name="post_mixer_even" if even else "post_mixer_odd",
    )(x, *acts, modtab, *weights, gffn, rwt, rb)


EXPERT_ROW_TILE = 256


def _moe_plan(sel, counts, n_tok):
    e_a, e_b, r_a, r_b = (sel[:, :, k, :].reshape(-1) for k in range(4))
    cnt = counts[:, 0].astype(jnp.int32)
    padded = ((cnt + EXPERT_ROW_TILE - 1) // EXPERT_ROW_TILE) * EXPERT_ROW_TILE
    ends = jnp.cumsum(padded)
    starts = ends - padded
    pos = jnp.concatenate([starts[e_a] + r_a, starts[e_b] + r_b]).astype(jnp.int32)
    n_tiles = 2 * n_tok // EXPERT_ROW_TILE + N_EXPERTS
    tile_start = jnp.arange(n_tiles, dtype=jnp.int32) * EXPERT_ROW_TILE
    tile_expert = jnp.minimum(jnp.searchsorted(ends, tile_start, side="right"),
                              N_EXPERTS - 1).astype(jnp.int32)
    tiles_used = (ends[-1:] // EXPERT_ROW_TILE).astype(jnp.int32)
    return pos, tile_expert, tiles_used, n_tiles


def _dispatch_kernel(pos_ref, f_ref, init_ref, out_ref, sem, *, n_tok):
    del init_ref
    tm = f_ref.shape[1]
    base = (pl.program_id(0) * pl.num_programs(1) + pl.program_id(1)) * tm

    def row_copy(r, k):
        dst = pos_ref[k * n_tok + base + r]
        return pltpu.make_async_copy(f_ref.at[0, pl.ds(r, 1), :], out_ref.at[pl.ds(dst, 1), :], sem)

    def issue(r, carry):
        row_copy(r, 0).start()
        row_copy(r, 1).start()
        return carry

    def drain(r, carry):
        row_copy(r, 0).wait()
        row_copy(r, 1).wait()
        return carry

    lax.fori_loop(0, tm, issue, 0)
    lax.fori_loop(0, tm, drain, 0)


def _dispatch(pos, f, n_rows):
    b, t, d = f.shape
    tm = ROW_TILE
    grid_spec = pltpu.PrefetchScalarGridSpec(
        num_scalar_prefetch=1,
        grid=(b, t // tm),
        in_specs=[pl.BlockSpec((1, tm, d), lambda bi, i, pos_ref: (bi, i, 0)),
                  pl.BlockSpec(memory_space=pl.ANY)],
        out_specs=pl.BlockSpec(memory_space=pl.ANY),
        scratch_shapes=[pltpu.SemaphoreType.DMA(())])
    return pl.pallas_call(
        functools.partial(_dispatch_kernel, n_tok=b * t),
        grid_spec=grid_spec,
        out_shape=jax.ShapeDtypeStruct((n_rows, d), F32),
        input_output_aliases={2: 0},
        compiler_params=_params("arbitrary", "arbitrary"),
        name="moe_dispatch",
    )(pos, f, jnp.zeros((n_rows, d), F32))


def _expert_ffn_kernel(te_ref, used_ref, x_ref, wg_ref, wu_ref, wd_ref, y_ref):
    del te_ref
    live = pl.program_id(0) < used_ref[0]

    @pl.when(live)
    def _():
        xb = x_ref[...].astype(BF16)
        he = _silu(_dot(xb, wg_ref[0])) * _dot(xb, wu_ref[0])
        y_ref[...] = _dot(he.astype(BF16), wd_ref[0])

    @pl.when(jnp.logical_not(live))
    def _():
        y_ref[...] = jnp.zeros(y_ref.shape, F32)


def _expert_ffn(tile_expert, tiles_used, xs, wg, wu, wd, n_tiles):
    n_rows, d = xs.shape
    _, _, d_e = wg.shape
    tr = EXPERT_ROW_TILE
    grid_spec = pltpu.PrefetchScalarGridSpec(
        num_scalar_prefetch=2,
        grid=(n_tiles,),
        in_specs=[pl.BlockSpec((tr, d), lambda j, te, used: (j, 0)),
                  pl.BlockSpec((1, d, d_e), lambda j, te, used: (te[j], 0, 0)),
                  pl.BlockSpec((1, d, d_e), lambda j, te, used: (te[j], 0, 0)),
                  pl.BlockSpec((1, d_e, d), lambda j, te, used: (te[j], 0, 0))],
        out_specs=pl.BlockSpec((tr, d), lambda j, te, used: (j, 0)))
    return pl.pallas_call(
        _expert_ffn_kernel,
        grid_spec=grid_spec,
        out_shape=jax.ShapeDtypeStruct((n_rows, d), F32),
        compiler_params=_params("arbitrary"),
        name="moe_expert_ffn",
    )(tile_expert, tiles_used, xs, wg, wu, wd)


def _combine_kernel(pos_ref, y_ref, w_ref, x_ref, modc_ref, modl_ref, fg_ref, o_ref, buf, sem,
                    *, n_tok, n_ctx, final_norm):
    i = pl.program_id(1)
    tm = x_ref.shape[1]
    base = (pl.program_id(0) * pl.num_programs(1) + i) * tm

    def row_copy(r, k):
        src = pos_ref[k * n_tok + base + r]
        return pltpu.make_async_copy(y_ref.at[pl.ds(src, 1), :], buf.at[k, pl.ds(r, 1), :], sem)

    def issue(r, carry):
        row_copy(r, 0).start()
        row_copy(r, 1).start()
        return carry

    def drain(r, carry):
        row_copy(r, 0).wait()
        row_copy(r, 1).wait()
        return carry

    lax.fori_loop(0, tm, issue, 0)
    lax.fori_loop(0, tm, drain, 0)
    w = w_ref[0]
    y = buf[0] * w[:, 0:1] + buf[1] * w[:, 1:2]
    rows = i * tm + lax.broadcasted_iota(jnp.int32, (tm, 1), 0)
    gate = jnp.where(rows < n_ctx, modc_ref[0, 5:6, :], modl_ref[0, 5:6, :])
    out = x_ref[0] + gate * y
    if final_norm:
        out = out * lax.rsqrt(jnp.mean(out * out, axis=-1, keepdims=True) + NORM_EPS) * fg_ref[...]
    o_ref[0] = out


def _combine(pos, ys, w_col, xmid, modtab, final_g, n_ctx, final_norm):
    b, t, d = xmid.shape
    tm = ROW_TILE
    row = lambda w: pl.BlockSpec((1, tm, w), lambda bi, i, pos_ref: (bi, i, 0))
    grid_spec = pltpu.PrefetchScalarGridSpec(
        num_scalar_prefetch=1,
        grid=(b, t // tm),
        in_specs=[pl.BlockSpec(memory_space=pl.ANY), row(LANES), row(d),
                  pl.BlockSpec((1, 8, d), lambda bi, i, pos_ref: (2 * bi, 0, 0)),
                  pl.BlockSpec((1, 8, d), lambda bi, i, pos_ref: (2 * bi + 1, 0, 0)),
                  pl.BlockSpec(final_g.shape, lambda bi, i, pos_ref: (0, 0))],
        out_specs=row(d),
        scratch_shapes=[pltpu.VMEM((2, tm, d), F32), pltpu.SemaphoreType.DMA(())])
    return pl.pallas_call(
        functools.partial(_combine_kernel, n_tok=b * t, n_ctx=n_ctx, final_norm=final_norm),
        grid_spec=grid_spec,
        out_shape=jax.ShapeDtypeStruct((b, t, d), F32),
        compiler_params=_params("arbitrary", "arbitrary"),
        name="moe_combine",
    )(pos, ys, w_col, xmid, modtab, modtab, final_g)


def _moe(f, sel, w_col, counts, wg, wu, wd, xmid, modtab, final_g, n_ctx, final_norm):
    b, t, _ = f.shape
    pos, tile_expert, tiles_used, n_tiles = _moe_plan(sel, counts, b * t)
    xs = _dispatch(pos, f, n_tiles * EXPERT_ROW_TILE)
    ys = _expert_ffn(tile_expert, tiles_used, xs, wg, wu, wd, n_tiles)
    return _combine(pos, ys, w_col, xmid, modtab, final_g, n_ctx, final_norm)


def _chan_dft_kernel(x_ref, mod_ref, g_ref, w_ref, zr_ref, zi_ref):
    h = _norm_mod(x_ref[0], g_ref[...], mod_ref[0, 0:1, :], mod_ref[0, 1:2, :])
    gd = w_ref.shape[0]
    for gi in range(h.shape[1] // gd):
        z = _dot(h[:, gi * gd:(gi + 1) * gd], w_ref[...], HI)
        zr_ref[0, :, gi * gd:(gi + 1) * gd] = z[:, :gd]
        zi_ref[0, :, gi * gd:(gi + 1) * gd] = z[:, gd:]


def _chan_dft(x, modtab, g, w_cs, x_row_off, t):
    b, _, d = x.shape
    tm = ROW_TILE
    off = x_row_off // tm
    row = pl.BlockSpec((1, tm, d), lambda bi, i: (bi, i, 0))
    return pl.pallas_call(
        _chan_dft_kernel,
        grid=(b, t // tm),
        in_specs=[pl.BlockSpec((1, tm, d), lambda bi, i: (bi, i + off, 0)),
                  pl.BlockSpec((1, 8, d), lambda bi, i: (2 * bi + 1, 0, 0)),
                  pl.BlockSpec(g.shape, lambda bi, i: (0, 0)),
                  pl.BlockSpec(w_cs.shape, lambda bi, i: (0, 0))],
        out_specs=[row, row],
        out_shape=[jax.ShapeDtypeStruct((b, t, d), F32)] * 2,
        compiler_params=_params("parallel", "parallel"),
        name="chan_dft",
    )(x, modtab, g, w_cs)


def _dft1_kernel(zr_ref, zi_ref, w_ref, yr_ref, yi_ref):
    n1 = zr_ref.shape[1]
    y = _dot(w_ref[...], jnp.concatenate([zr_ref[0], zi_ref[0]], axis=0), HI)
    yr_ref[0] = y[:n1]
    yi_ref[0] = y[n1:]


def _dft1(zr, zi, w1):
    b, n1, cols = zr.shape
    tn = min(cols, 4096)
    blk = pl.BlockSpec((1, n1, tn), lambda bi, j: (bi, 0, j))
    return pl.pallas_call(
        _dft1_kernel,
        grid=(b, cols // tn),
        in_specs=[blk, blk, pl.BlockSpec(w1.shape, lambda bi, j: (0, 0))],
        out_specs=[blk, blk],
        out_shape=[jax.ShapeDtypeStruct(zr.shape, F32)] * 2,
        compiler_params=_params("parallel", "parallel"),
        name="dft_stage1",
    )(zr, zi, w1)


def _dft2_kernel(yr_ref, yi_ref, tab_ref, o_ref):
    y = jnp.concatenate([yr_ref[0, 0], yi_ref[0, 0]], axis=0)
    o_ref[0, 0] = _dot(tab_ref[0], y, HI)


def _dft2(yr, yi, tab):
    b, n1, n2, d = yr.shape
    blk = pl.BlockSpec((1, 1, n2, d), lambda bi, k1: (bi, k1, 0, 0))
    return pl.pallas_call(
        _dft2_kernel,
        grid=(b, n1),
        in_specs=[blk, blk, pl.BlockSpec((1, n2, 2 * n2), lambda bi, k1: (k1, 0, 0))],
        out_specs=blk,
        out_shape=jax.ShapeDtypeStruct(yr.shape, F32),
        compiler_params=_params("parallel", "parallel"),
        name="dft_stage2",
    )(yr, yi, tab)


def _dft_tables(t, gd):
    n2 = ML_CHUNK
    n1 = t // n2
    def cs(num, den):
        ang = (2.0 * np.pi / den) * (num % den).astype(np.float64)
        return np.cos(ang), np.sin(ang)
    c = np.arange(gd)
    cc, sc = cs(np.outer(c, c), gd)
    w_cs = np.concatenate([cc, -sc], axis=1)
    a = np.arange(n1)
    c1, s1 = cs(np.outer(a, a), n1)
    w1 = np.block([[c1, s1], [-s1, c1]])
    k = a[:, None, None] + n1 * np.arange(n2)[None, :, None]
    c2, s2 = cs(k * np.arange(n2)[None, None, :], t)
    tab = np.concatenate([c2, s2], axis=2) / math.sqrt(t * gd)
    return (jnp.asarray(w_cs, F32), jnp.asarray(w1, F32), jnp.asarray(tab, F32))


def _rope_tables(n_ctx, n_lat):
    pos = jnp.arange(n_lat, dtype=jnp.int32)
    n_axis = DA_QK_DIM // 4
    inv = ROPE_BASE ** (-jnp.arange(n_axis, dtype=F32) / n_axis)
    ang = jnp.concatenate([(pos // GRID_W).astype(F32)[:, None] * inv,
                           (pos % GRID_W).astype(F32)[:, None] * inv], axis=-1)
    cos, sin = jnp.cos(ang), jnp.sin(ang)
    cos = jnp.concatenate([jnp.ones((n_ctx, 2 * n_axis), F32), cos], axis=0)
    sin = jnp.concatenate([jnp.zeros((n_ctx, 2 * n_axis), F32), sin], axis=0)
    cos128 = jnp.concatenate([cos, cos, cos, cos], axis=1)
    sin128 = jnp.concatenate([-sin, sin, -sin, sin], axis=1)
    return cos128, sin128


def _deinterleave(w):
    d, n = w.shape
    w = w.reshape(d, n // DA_QK_DIM, DA_QK_DIM // 2, 2)
    return jnp.concatenate([w[..., 0], w[..., 1]], axis=-1).reshape(d, n)


def _pad_rows(a, rows):
    return jnp.concatenate([a, jnp.zeros((rows - a.shape[0],) + a.shape[1:], a.dtype)], axis=0)


def _pad_cols(a, cols):
    return jnp.concatenate([a, jnp.zeros(a.shape[:-1] + (cols - a.shape[-1],), a.dtype)], axis=-1)


def kernel(x, c, ctx, c_ctx, ada_w, ada_b, norm_mix_g, norm_ffn_g, even_w_in, even_w_out,
           even_conv_w, even_gate_b, even_lam, even_subln_g, odd_w_fnet, router_w, router_b,
           exp_w_gate, exp_w_up, exp_w_down, final_g):
    b, n_lat, d = x.shape
    n_ctx = ctx.shape[1]
    depth = ada_w.shape[0]
    assert depth == 2 and b + 1 <= 8
    assert n_ctx % ROW_TILE == 0 and n_lat % ROW_TILE == 0

    cond8 = _pad_rows(jnp.concatenate([c_ctx[None, :], c], axis=0), 8)
    rwt = router_w.T
    rb = jnp.broadcast_to(router_b[:, None], (N_EXPERTS, LANES))
    row2 = lambda v: v.reshape(1, -1)

    def modtab_for(layer):
        mods = _ada_mods(cond8, ada_w[layer], ada_b[layer]).reshape(8, 6, d)
        mods = jnp.concatenate([mods, jnp.zeros((8, 2, d), F32)], axis=1)
        idx = np.array([[0, 1 + bi] for bi in range(b)]).reshape(-1)
        return mods[idx]

    xs = jnp.concatenate([ctx, x], axis=1)
    modtab = modtab_for(0)
    w_in = even_w_in[0]
    o1 = DA_HEADS * 2 * DA_QK_DIM
    o2 = 2 * o1
    o3 = o2 + DA_HEADS * DA_V_DIM
    o4 = o3 + 2 * ML_HEADS * ML_DIM
    o5 = o4 + ML_HEADS * ML_DIM
    o6 = o5 + ML_HEADS * ML_DIM
    ws = [(_deinterleave(w_in[:, :o1]) * (DA_QK_DIM ** -0.5)).astype(BF16),
          _deinterleave(w_in[:, o1:o2]).astype(BF16),
          w_in[:, o2:o3].astype(BF16), w_in[:, o3:o4].astype(BF16),
          w_in[:, o4:o5].astype(BF16), w_in[:, o5:o6].astype(BF16),
          _pad_cols(w_in[:, o6:], LANES).astype(BF16)]
    gate_b = _pad_cols(even_gate_b[0].reshape(1, -1), LANES)
    cos128, sin128 = _rope_tables(n_ctx, n_lat)
    daq, dak, dav, mqk, mv, og, gates = _inproj(xs, modtab, row2(norm_mix_g[0]), cos128, sin128,
                                                 ws, gate_b, n_ctx)
    mq, mk, gc, gr = _mlprep(mqk, _pad_rows(even_conv_w[0], 8), gates, n_ctx)
    hf, hb = _mlstm(mq, mk, mv, gc, gr, n_ctx)
    lam_init = 0.8 - 0.6 * math.exp(-0.3 * 0)
    lam8 = _pad_rows(even_lam[0], 8)
    att = _diff_attention(daq, dak, dav, lam8, row2(even_subln_g[0]), n_ctx, lam_init)
    w_out = even_w_out[0].astype(BF16)
    half = DA_HEADS * DA_V_DIM
    xmid, f, sel, w_col, counts = _post_mixer(xs, [att, hf, hb, og], [w_out[:half], w_out[half:]],
                                              modtab, row2(norm_ffn_g[0]), rwt, rb, n_ctx, 0, True)
    xs = _moe(f, sel, w_col, counts, exp_w_gate[0].astype(BF16), exp_w_up[0].astype(BF16),
              exp_w_down[0].astype(BF16), xmid, modtab, row2(final_g), n_ctx, False)

    modtab = modtab_for(1)
    gd = d // FN_GROUPS
    w_cs, w1, tab = _dft_tables(n_lat, gd)
    n2 = ML_CHUNK
    n1 = n_lat // n2
    zr, zi = _chan_dft(xs, modtab, row2(norm_mix_g[1]), w_cs, n_ctx, n_lat)
    yr, yi = _dft1(zr.reshape(b, n1, n2 * d), zi.reshape(b, n1, n2 * d), w1)
    fo = _dft2(yr.reshape(b, n1, n2, d), yi.reshape(b, n1, n2, d), tab)
    fo = fo.transpose(0, 2, 1, 3).reshape(b, n_lat, d)
    xmid, f, sel, w_col, counts = _post_mixer(xs, [fo], [odd_w_fnet[0].astype(BF16)], modtab,
                                              row2(norm_ffn_g[1]), rwt, rb, 0, n_ctx, False)
    return _moe(f, sel, w_col, counts, exp_w_gate[1].astype(BF16), exp_w_up[1].astype(BF16),
                exp_w_down[1].astype(BF16), xmid, modtab, row2(final_g), 0, True)
```

```python
import functools
import math

import jax
import jax.numpy as jnp
import numpy as np
from jax import lax
from jax.experimental import pallas as pl
from jax.experimental.pallas import tpu as pltpu

F32 = jnp.float32
BF16 = jnp.bfloat16

NORM_EPS = 1e-6
GRID_W = 64
DA_HEADS = 4
DA_QK_DIM = 64
DA_V_DIM = 128
ML_HEADS = 4
ML_DIM = 128
ML_CHUNK = 128
FN_GROUPS = 4
N_EXPERTS = 16
EXPERTS_PER_GROUP = 4
ROPE_BASE = 10000.0
LANES = 128
ROW_TILE = 256
VMEM_LIMIT_BYTES = 56 * 1024 * 1024
HI = lax.Precision.HIGHEST
LOG2_E = math.log2(math.e)


def _params(*sem):
    return pltpu.CompilerParams(dimension_semantics=sem, vmem_limit_bytes=VMEM_LIMIT_BYTES)


def _dot(a, b, precision=None):
    return jnp.dot(a, b, preferred_element_type=F32, precision=precision)


def _dot_nt(a, b, precision=None):
    return lax.dot_general(a, b, (((1,), (1,)), ((), ())), preferred_element_type=F32,
                           precision=precision)


def _dot_split(a, b):
    a_hi = a.astype(BF16)
    b_hi = b.astype(BF16)
    a_lo = (a - a_hi.astype(F32)).astype(BF16)
    b_lo = (b - b_hi.astype(F32)).astype(BF16)
    return _dot(a_hi, b_hi) + (_dot(a_hi, b_lo) + _dot(a_lo, b_hi))


def _sigmoid(x):
    return 1.0 / (1.0 + jnp.exp(-x))


def _silu(x):
    return x * _sigmoid(x)


def _norm_mod(x, g, shift, scale):
    y = x * lax.rsqrt(jnp.mean(x * x, axis=-1, keepdims=True) + NORM_EPS) * g
    return y * (1.0 + scale) + shift


def _ada_kernel(c_ref, w_ref, b_ref, o_ref):
    o_ref[...] = _dot(_silu(c_ref[...]), w_ref[...], HI) + b_ref[...]


def _ada_mods(cond8, w, b):
    d, n = w.shape
    tn = n // 6
    return pl.pallas_call(
        _ada_kernel,
        grid=(6,),
        in_specs=[pl.BlockSpec((8, d), lambda j: (0, 0)),
                  pl.BlockSpec((d, tn), lambda j: (0, j)),
                  pl.BlockSpec((1, tn), lambda j: (0, j))],
        out_specs=pl.BlockSpec((8, tn), lambda j: (0, j)),
        out_shape=jax.ShapeDtypeStruct((8, n), F32),
        compiler_params=_params("arbitrary"),
        name="ada_mods",
    )(cond8, w, b.reshape(1, n))


def _inproj_kernel(x_ref, mod_ref, g_ref, cos_ref, sin_ref, wq_ref, wk_ref, wv_ref, wmqk_ref,
                   wmv_ref, wmo_ref, wg_ref, gb_ref,
                   q_ref, k_ref, v_ref, mqk_ref, mv_ref, og_ref, gate_ref):
    x = x_ref[0]
    h = _norm_mod(x, g_ref[...], mod_ref[0, 0:1, :], mod_ref[0, 1:2, :]).astype(BF16)
    tm = x.shape[0]
    width = q_ref.shape[2]
    cos = jnp.concatenate([cos_ref[...]] * (width // LANES), axis=1)
    sin = jnp.concatenate([sin_ref[...]] * (width // LANES), axis=1)
    lane = lax.broadcasted_iota(jnp.int32, (tm, width), 1)
    lower = (lane & (DA_QK_DIM - 1)) < (DA_QK_DIM // 2)

    def rope(u):
        swapped = jnp.where(lower, pltpu.roll(u, width - DA_QK_DIM // 2, 1),
                            pltpu.roll(u, DA_QK_DIM // 2, 1))
        return u * cos + swapped * sin

    q_ref[0] = (rope(_dot(h, wq_ref[...])) * LOG2_E).astype(BF16)
    k_ref[0] = rope(_dot(h, wk_ref[...])).astype(BF16)
    v = _dot(h, wv_ref[...]).astype(BF16)
    ones_col = (lax.broadcasted_iota(jnp.int32, (tm, DA_V_DIM), 1) == 0).astype(BF16)
    v_ref[0] = jnp.concatenate(
        [blk for hd in range(DA_HEADS) for blk in (v[:, hd * DA_V_DIM:(hd + 1) * DA_V_DIM], ones_col)],
        axis=1)
    mqk_ref[0] = _dot(h, wmqk_ref[...]).astype(BF16)
    mv_ref[0] = _dot(h, wmv_ref[...]).astype(BF16)
    og_ref[0] = _sigmoid(_dot(h, wmo_ref[...])).astype(BF16)
    g = _dot(h, wg_ref[...]) + gb_ref[...]
    glane = lax.broadcasted_iota(jnp.int32, g.shape, 1)
    is_forget = ((glane // ML_HEADS) & 1) == 1
    log_sig = jnp.minimum(g, 0.0) - jnp.log(1.0 + jnp.exp(-jnp.abs(g)))
    gate_ref[0] = jnp.where(is_forget, log_sig, g)


def _inproj(xs, modtab, g, cos, sin, ws, gate_b, n_ctx):
    b, t, d = xs.shape
    tm = ROW_TILE
    nt = t // tm
    ctx_tiles = n_ctx // tm
    row = lambda w: pl.BlockSpec((1, tm, w), lambda bi, i: (bi, i, 0))
    full = lambda a: pl.BlockSpec(a.shape, lambda bi, i: (0,) * a.ndim)
    widths = [w.shape[1] for w in ws]
    widths[2] *= 2
    out_dtypes = [BF16] * 6 + [F32]
    return pl.pallas_call(
        _inproj_kernel,
        grid=(b, nt),
        in_specs=[row(d),
                  pl.BlockSpec((1, 8, d), lambda bi, i: (2 * bi + (i >= ctx_tiles).astype(jnp.int32), 0, 0)),
                  full(g),
                  pl.BlockSpec((tm, LANES), lambda bi, i: (i, 0)),
                  pl.BlockSpec((tm, LANES), lambda bi, i: (i, 0))]
                 + [full(w) for w in ws] + [full(gate_b)],
        out_specs=[row(w) for w in widths],
        out_shape=[jax.ShapeDtypeStruct((b, t, w), dt) for w, dt in zip(widths, out_dtypes)],
        compiler_params=_params("parallel", "parallel"),
        name="inproj",
    )(xs, modtab, g, cos, sin, *ws, gate_b)


def _split3(x):
    x1 = x.astype(BF16)
    r1 = x - x1.astype(F32)
    x2 = r1.astype(BF16)
    x3 = (r1 - x2.astype(F32)).astype(BF16)
    return x1, x2, x3


def _mlprep_kernel(cur_ref, prev_ref, next_ref, cw_ref, gate_ref,
                   mq_ref, mk_ref, gc_ref, gr_ref, *, ctx_tiles, n_tiles):
    i = pl.program_id(1)
    cur = cur_ref[0].astype(F32)
    tm, w = cur.shape
    prev_ok = i != ctx_tiles
    if ctx_tiles > 0:
        prev_ok = jnp.logical_and(prev_ok, i != 0)
        next_ok = jnp.logical_and(i != ctx_tiles - 1, i != n_tiles - 1)
    else:
        next_ok = i != n_tiles - 1
    prev_row = jnp.where(prev_ok, prev_ref[0, 7:8, :].astype(F32), 0.0)
    next_row = jnp.where(next_ok, next_ref[0, 0:1, :].astype(F32), 0.0)
    ridx = lax.broadcasted_iota(jnp.int32, (tm, w), 0)
    before = jnp.where(ridx == 0, prev_row, pltpu.roll(cur, 1, 0))
    after = jnp.where(ridx == tm - 1, next_row, pltpu.roll(cur, tm - 1, 0))
    y = _silu(before * cw_ref[0:1, :] + cur * cw_ref[1:2, :] + after * cw_ref[2:3, :])
    half = w // 2
    mq_ref[0] = y[:, :half].astype(BF16)
    mk_ref[0] = (y[:, half:] * (ML_DIM ** -0.5)).astype(BF16)

    n_g = 4 * ML_HEADS
    r = lax.broadcasted_iota(jnp.int32, (ML_CHUNK, ML_CHUNK), 0)
    c = lax.broadcasted_iota(jnp.int32, (ML_CHUNK, ML_CHUNK), 1)
    lower = (c <= r).astype(BF16)
    upper = (c >= r).astype(BF16)
    for ci in range(tm // ML_CHUNK):
        gch = gate_ref[0, ci * ML_CHUNK:(ci + 1) * ML_CHUNK, :]
        gm = jnp.where(c < n_g, gch, 0.0)
        pre = sum(_dot(lower, p) for p in _split3(pltpu.roll(gm, n_g, 1)))
        suf = sum(_dot(upper, p) for p in _split3(pltpu.roll(gm, 2 * n_g, 1)))
        col = gm + pre + suf
        gc_ref[0, ci * ML_CHUNK:(ci + 1) * ML_CHUNK, :] = col
        gr_ref[0, ci] = col.T


def _mlprep(mqk, conv_w8, gates, n_ctx):
    b, t, w = mqk.shape
    tm = ROW_TILE
    nt = t // tm
    sub = tm // 8
    nsub = t // 8
    cpt = tm // ML_CHUNK
    kern = functools.partial(_mlprep_kernel, ctx_tiles=n_ctx // tm, n_tiles=nt)
    return pl.pallas_call(
        kern,
        grid=(b, nt),
        in_specs=[pl.BlockSpec((1, tm, w), lambda bi, i: (bi, i, 0)),
                  pl.BlockSpec((1, 8, w), lambda bi, i: (bi, jnp.maximum(i * sub - 1, 0), 0)),
                  pl.BlockSpec((1, 8, w), lambda bi, i: (bi, jnp.minimum((i + 1) * sub, nsub - 1), 0)),
                  pl.BlockSpec(conv_w8.shape, lambda bi, i: (0, 0)),
                  pl.BlockSpec((1, tm, LANES), lambda bi, i: (bi, i, 0))],
        out_specs=[pl.BlockSpec((1, tm, w // 2), lambda bi, i: (bi, i, 0)),
                   pl.BlockSpec((1, tm, w // 2), lambda bi, i: (bi, i, 0)),
                   pl.BlockSpec((1, tm, LANES), lambda bi, i: (bi, i, 0)),
                   pl.BlockSpec((1, cpt, ML_CHUNK, LANES), lambda bi, i: (bi, i, 0, 0))],
        out_shape=[jax.ShapeDtypeStruct((b, t, w // 2), BF16),
                   jax.ShapeDtypeStruct((b, t, w // 2), BF16),
                   jax.ShapeDtypeStruct((b, t, LANES), F32),
                   jax.ShapeDtypeStruct((b, t // ML_CHUNK, ML_CHUNK, LANES), F32)],
        compiler_params=_params("parallel", "parallel"),
        name="mlstm_prep",
    )(mqk, mqk, mqk, conv_w8, gates)


def _mlstm_kernel(qf_ref, kf_ref, vf_ref, gcf_ref, grf_ref, qb_ref, kb_ref, vb_ref, gcb_ref, grb_ref,
                  hf_ref, hb_ref, state_ref, m_ref):
    s = pl.program_id(0)
    nb = qf_ref.shape[0]
    n_g = 4 * ML_HEADS

    @pl.when(s == 0)
    def _():
        state_ref[...] = jnp.zeros(state_ref.shape, F32)
        m_ref[...] = jnp.zeros(m_ref.shape, F32)

    qi = lax.broadcasted_iota(jnp.int32, (ML_CHUNK, ML_CHUNK), 0)
    ki = lax.broadcasted_iota(jnp.int32, (ML_CHUNK, ML_CHUNK), 1)
    ones_col = (ki == 0).astype(BF16)

    for bi in range(nb):
        for direction in range(2):
            q_ref, k_ref, v_ref, gc_ref, gr_ref, h_ref = (
                (qf_ref, kf_ref, vf_ref, gcf_ref, grf_ref, hf_ref) if direction == 0 else
                (qb_ref, kb_ref, vb_ref, gcb_ref, grb_ref, hb_ref))
            visible = (ki <= qi) if direction == 0 else (ki >= qi)
            gc = gc_ref[bi]
            gr = gr_ref[bi, 0]
            for hd in range(ML_HEADS):
                chain = (bi * 2 + direction) * ML_HEADS + hd
                c_li = 2 * direction * ML_HEADS + hd
                c_lf = c_li + ML_HEADS
                c_cs = c_lf + (n_g if direction == 0 else 2 * n_g)
                lo, hi = hd * ML_DIM, (hd + 1) * ML_DIM
                q = q_ref[bi, :, lo:hi]
                k = k_ref[bi, :, lo:hi]
                v_aug = jnp.concatenate([v_ref[bi, :, lo:hi], ones_col], axis=1)
                li_col = gc[:, c_li:c_li + 1]
                cs_col = gc[:, c_cs:c_cs + 1]
                li_row = gr[c_li:c_li + 1, :]
                lf_row = gr[c_lf:c_lf + 1, :]
                cs_row = gr[c_cs:c_cs + 1, :]
                m_st = m_ref[chain][:, 0:1]
                st = state_ref[chain]

                b_last = jnp.sum(lf_row, axis=1, keepdims=True)
                logw = b_last - cs_col + li_col
                m_new = jnp.maximum(b_last + m_st, jnp.max(logw, axis=0, keepdims=True))
                wgt = jnp.exp(logw - m_new)
                decay = jnp.exp(b_last + m_st - m_new)

                log_d = jnp.where(visible, cs_col - cs_row + li_row, -jnp.inf)
                m_inter = cs_col + m_st
                m_t = jnp.maximum(m_inter, jnp.max(log_d, axis=1, keepdims=True))
                dmat = jnp.exp(log_d - m_t)
                inter = jnp.exp(m_inter - m_t)

                sc = (_dot_nt(q, k) * dmat).astype(BF16)
                tot = inter * _dot(q, st.astype(BF16)) + _dot(sc, v_aug)
                den = jnp.maximum(jnp.abs(tot[:, ML_DIM:ML_DIM + 1]), jnp.exp(-m_t))
                h_ref[bi, :, lo:hi] = tot[:, :ML_DIM] / den

                kw_t = (k.astype(F32) * wgt).T.astype(BF16)
                state_ref[chain] = decay * st + _dot(kw_t, v_aug)
                m_ref[chain] = jnp.broadcast_to(m_new, (1, LANES))


def _mlstm(mq, mk, mv, gc, gr, n_ctx):
    b, t, w = mq.shape
    nc = t // ML_CHUNK
    ncc = n_ctx // ML_CHUNK
    fwd = lambda s: s
    bwd = lambda s: jnp.where(s < ncc, ncc - 1 - s, nc - 1 - s + ncc)
    tok = lambda f: pl.BlockSpec((b, ML_CHUNK, w), lambda s: (0, f(s), 0))
    gcs = lambda f: pl.BlockSpec((b, ML_CHUNK, LANES), lambda s: (0, f(s), 0))
    grs = lambda f: pl.BlockSpec((b, 1, ML_CHUNK, LANES), lambda s: (0, f(s), 0, 0))
    n_chain = b * 2 * ML_HEADS
    return pl.pallas_call(
        _mlstm_kernel,
        grid=(nc,),
        in_specs=[tok(fwd), tok(fwd), tok(fwd), gcs(fwd), grs(fwd),
                  tok(bwd), tok(bwd), tok(bwd), gcs(bwd), grs(bwd)],
        out_specs=[tok(fwd), tok(bwd)],
        out_shape=[jax.ShapeDtypeStruct((b, t, w), F32)] * 2,
        scratch_shapes=[pltpu.VMEM((n_chain, ML_DIM, 2 * ML_DIM), F32),
                        pltpu.VMEM((n_chain, 1, LANES), F32)],
        compiler_params=_params("arbitrary"),
        name="mlstm",
    )(mq, mk, mv, gc, gr, mq, mk, mv, gc, gr)


def _attn_kernel(q_ref, k_ref, v_ref, lam_ref, sg_ref, o_ref, qm_scr, *scr,
                 ctx_tiles, n_ctx, key_block, n_blocks, lam_init):
    s_scr = (scr[0:2], scr[2:4])
    bm_scr = (scr[4:6], scr[6:8])
    m_scr = scr[8:10]
    acc_scr = scr[10:12]
    i = pl.program_id(2)
    dv = DA_V_DIM
    q = q_ref[0]
    lane = lax.broadcasted_iota(jnp.int32, q.shape, 1)
    zero = jnp.zeros_like(q)
    qm_scr[0] = jnp.where(lane < DA_QK_DIM, q, zero)
    qm_scr[1] = jnp.where(lane >= DA_QK_DIM, q, zero)
    align = math.gcd(n_ctx, key_block)

    def scores(mp, start, size):
        return _dot_nt(qm_scr[mp], k_ref[0, pl.ds(start, size), :])

    def lane_max(sc):
        m = sc[:, 0:LANES]
        for t in range(1, sc.shape[1] // LANES):
            m = jnp.maximum(m, sc[:, t * LANES:(t + 1) * LANES])
        return m

    def row_max(bm):
        return jnp.broadcast_to(jnp.max(bm, axis=1, keepdims=True), bm.shape)

    def weights(sc, m):
        return jnp.concatenate([jnp.exp2(sc[:, t * LANES:(t + 1) * LANES] - m)
                                for t in range(sc.shape[1] // LANES)], axis=1).astype(BF16)

    for mp in range(2):
        sc = scores(mp, 0, n_ctx)
        m = row_max(lane_max(sc))
        m_scr[mp][...] = m
        acc_scr[mp][...] = _dot(weights(sc, m), v_ref[0, 0:n_ctx, :])

    def stage(j, slot):
        start = pl.multiple_of(n_ctx + j * key_block, align)
        for mp in range(2):
            sc = scores(mp, start, key_block)
            s_scr[mp][slot][...] = sc
            bm_scr[mp][slot][...] = lane_max(sc)

    def consume(j, slot):
        start = pl.multiple_of(n_ctx + j * key_block, align)
        for mp in range(2):
            m_old = m_scr[mp][...]
            m_new = jnp.maximum(m_old, row_max(bm_scr[mp][slot][...]))
            alpha = jnp.exp2(m_old - m_new)
            pv = _dot(weights(s_scr[mp][slot][...], m_new), v_ref[0, pl.ds(start, key_block), :])
            acc_scr[mp][...] = jnp.concatenate([alpha, alpha], axis=1) * acc_scr[mp][...] + pv
            m_scr[mp][...] = m_new

    @pl.when(i >= ctx_tiles)
    def _():
        stage(0, 0)

        def body(g, carry):
            stage(2 * g + 1, 1)
            consume(2 * g, 0)
            stage(2 * g + 2, 0)
            consume(2 * g + 1, 1)
            return carry
        lax.fori_loop(0, n_blocks // 2 - 1, body, 0)
        stage(n_blocks - 1, 1)
        consume(n_blocks - 2, 0)
        consume(n_blocks - 1, 1)

    lv = lam_ref[...]
    dot01 = jnp.sum(lv[0:1, :] * lv[1:2, :], axis=1, keepdims=True)
    dot23 = jnp.sum(lv[2:3, :] * lv[3:4, :], axis=1, keepdims=True)
    lam = jnp.exp(dot01) - jnp.exp(dot23) + lam_init
    a0 = acc_scr[0][...]
    a1 = acc_scr[1][...]
    o = a0[:, 0:dv] / a0[:, dv:dv + 1] - lam * (a1[:, 0:dv] / a1[:, dv:dv + 1])
    o = o * lax.rsqrt(jnp.mean(o * o, axis=-1, keepdims=True) + NORM_EPS) * sg_ref[...]
    o_ref[0] = (o * (1.0 - lam_init)).astype(BF16)


def _diff_attention(q, k, v_aug, lam8, subln_g, n_ctx, lam_init):
    b, t, w = q.shape
    tq = ROW_TILE
    n_lat = t - n_ctx
    key_block = 1024 if n_lat % 2048 == 0 else 512
    n_blocks = n_lat // key_block
    assert n_lat % (2 * key_block) == 0 and n_ctx % tq == 0 and n_ctx % LANES == 0
    kern = functools.partial(_attn_kernel, ctx_tiles=n_ctx // tq, n_ctx=n_ctx, key_block=key_block,
                             n_blocks=n_blocks, lam_init=lam_init)
    return pl.pallas_call(
        kern,
        grid=(b, DA_HEADS, t // tq),
        in_specs=[pl.BlockSpec((1, tq, LANES), lambda bi, h, i: (bi, i, h)),
                  pl.BlockSpec((1, t, LANES), lambda bi, h, i: (bi, 0, h)),
                  pl.BlockSpec((1, t, 2 * DA_V_DIM), lambda bi, h, i: (bi, 0, h)),
                  pl.BlockSpec(lam8.shape, lambda bi, h, i: (0, 0)),
                  pl.BlockSpec(subln_g.shape, lambda bi, h, i: (0, 0))],
        out_specs=pl.BlockSpec((1, tq, LANES), lambda bi, h, i: (bi, i, h)),
        out_shape=jax.ShapeDtypeStruct((b, t, w), BF16),
        scratch_shapes=[pltpu.VMEM((2, tq, LANES), BF16)]
                       + [pltpu.VMEM((tq, key_block), F32)] * 4
                       + [pltpu.VMEM((tq, LANES), F32)] * 6
                       + [pltpu.VMEM((tq, 2 * DA_V_DIM), F32)] * 2,
        compiler_params=_params("parallel", "parallel", "arbitrary"),
        name="diff_attention",
    )(q, k, v_aug, lam8, subln_g)


def _top2_sum(a, b, c, d):
    hi1, lo1 = jnp.maximum(a, b), jnp.minimum(a, b)
    hi2, lo2 = jnp.maximum(c, d), jnp.minimum(c, d)
    return jnp.maximum(hi1, hi2) + jnp.maximum(jnp.minimum(hi1, hi2), jnp.maximum(lo1, lo2))


def _route(f, rwt_ref, rb_ref, cnt_ref):
    tm = f.shape[0]
    aff = _sigmoid(_dot_nt(rwt_ref[...], f, HI))
    biased = aff + rb_ref[:, 0:1]
    bz = [biased[e:e + 1, :] for e in range(N_EXPERTS)]
    af = [aff[e:e + 1, :] for e in range(N_EXPERTS)]
    n_grp = N_EXPERTS // EXPERTS_PER_GROUP
    scores = [_top2_sum(*bz[EXPERTS_PER_GROUP * g:EXPERTS_PER_GROUP * (g + 1)]) for g in range(n_grp)]
    best = scores[0]
    sel_grp = jnp.zeros_like(best, dtype=jnp.int32)
    for g in range(1, n_grp):
        better = scores[g] > best
        sel_grp = jnp.where(better, g, sel_grp)
        best = jnp.where(better, scores[g], best)
    chosen = []
    for e in range(N_EXPERTS):
        g = e // EXPERTS_PER_GROUP
        rank = jnp.zeros_like(sel_grp)
        for o in range(EXPERTS_PER_GROUP * g, EXPERTS_PER_GROUP * (g + 1)):
            if o == e:
                continue
            beats = (bz[o] > bz[e]) if o > e else (bz[o] >= bz[e])
            rank = rank + beats.astype(jnp.int32)
        chosen.append(jnp.logical_and(sel_grp == g, rank < 2))
    denom = sum(jnp.where(chosen[e], af[e], 0.0) for e in range(N_EXPERTS))
    erow = lax.broadcasted_iota(jnp.int32, (N_EXPERTS, tm), 0)
    one_hot = jnp.zeros((N_EXPERTS, tm), F32)
    for e in range(N_EXPERTS):
        one_hot = jnp.where(jnp.logical_and(erow == e, chosen[e]), 1.0, one_hot)
    earlier = (lax.broadcasted_iota(jnp.int32, (tm, tm), 0)
               < lax.broadcasted_iota(jnp.int32, (tm, tm), 1)).astype(BF16)
    rank_all = _dot(one_hot.astype(BF16), earlier) + cnt_ref[:, 0:1]
    cnt_ref[...] = cnt_ref[...] + jnp.sum(one_hot, axis=1, keepdims=True)

    seen = jnp.zeros((1, tm), jnp.bool_)
    e_a = e_b = jnp.zeros((1, tm), jnp.int32)
    r_a = r_b = w_a = w_b = jnp.zeros((1, tm), F32)
    for e in range(N_EXPERTS):
        first = jnp.logical_and(chosen[e], jnp.logical_not(seen))
        second = jnp.logical_and(chosen[e], seen)
        rk = rank_all[e:e + 1, :]
        wt = af[e] / denom
        e_a, e_b = jnp.where(first, e, e_a), jnp.where(second, e, e_b)
        r_a, r_b = jnp.where(first, rk, r_a), jnp.where(second, rk, r_b)
        w_a, w_b = jnp.where(first, wt, w_a), jnp.where(second, wt, w_b)
        seen = jnp.logical_or(seen, chosen[e])
    r8 = lax.broadcasted_iota(jnp.int32, (8, tm), 0)
    sel = jnp.where(r8 == 0, e_a, jnp.where(r8 == 1, e_b, jnp.where(
        r8 == 2, r_a.astype(jnp.int32), jnp.where(r8 == 3, r_b.astype(jnp.int32), 0))))
    row = lax.broadcasted_iota(jnp.int32, (LANES, tm), 0)
    w_t = jnp.where(row == 0, w_a, jnp.where(row == 1, w_b, 0.0))
    return sel, w_t.T


def _post_kernel(*refs, even):
    if even:
        (x_ref, a_ref, hf_ref, hb_ref, og_ref, mod_ref, wa_ref, wm_ref, gf_ref, rwt_ref, rb_ref,
         xo_ref, f_ref, sel_ref, w_ref, cnt_ref, cnt_scr) = refs
        m = ((hf_ref[0] + hb_ref[0]) * og_ref[0].astype(F32)).astype(BF16)
        o = _dot(a_ref[0], wa_ref[...]) + _dot(m, wm_ref[...])
    else:
        (x_ref, a_ref, mod_ref, wa_ref, gf_ref, rwt_ref, rb_ref,
         xo_ref, f_ref, sel_ref, w_ref, cnt_ref, cnt_scr) = refs
        o = _dot(a_ref[0].astype(BF16), wa_ref[...])

    @pl.when(jnp.logical_and(pl.program_id(0) == 0, pl.program_id(1) == 0))
    def _():
        cnt_scr[...] = jnp.zeros(cnt_scr.shape, F32)

    x = x_ref[0] + mod_ref[0, 2:3, :] * o
    xo_ref[0] = x
    f = _norm_mod(x, gf_ref[...], mod_ref[0, 3:4, :], mod_ref[0, 4:5, :])
    f_ref[0] = f
    sel, w_col = _route(f, rwt_ref, rb_ref, cnt_scr)
    sel_ref[0, 0] = sel
    w_ref[0] = w_col
    cnt_ref[...] = cnt_scr[...]


def _post_mixer(x, acts, weights, modtab, gffn, rwt, rb, n_ctx, x_row_off, even):
    b, t, _ = acts[0].shape
    d = x.shape[2]
    tm = ROW_TILE
    nt = t // tm
    ctx_tiles = n_ctx // tm
    off = x_row_off // tm
    full = lambda a: pl.BlockSpec(a.shape, lambda bi, i: (0,) * a.ndim)
    row = lambda w: pl.BlockSpec((1, tm, w), lambda bi, i: (bi, i, 0))
    mod_spec = pl.BlockSpec(
        (1, 8, d), lambda bi, i: (2 * bi + (i + off >= ctx_tiles).astype(jnp.int32), 0, 0))
    in_specs = ([pl.BlockSpec((1, tm, d), lambda bi, i: (bi, i + off, 0))]
                + [row(a.shape[2]) for a in acts] + [mod_spec]
                + [full(w) for w in weights] + [full(gffn), full(rwt), full(rb)])
    return pl.pallas_call(
        functools.partial(_post_kernel, even=even),
        grid=(b, nt),
        in_specs=in_specs,
        out_specs=[row(d), row(d),
                   pl.BlockSpec((1, 1, 8, tm), lambda bi, i: (bi, i, 0, 0)),
                   row(LANES),
                   pl.BlockSpec((N_EXPERTS, LANES), lambda bi, i: (0, 0))],
        out_shape=[jax.ShapeDtypeStruct((b, t, d), F32), jax.ShapeDtypeStruct((b, t, d), F32),
                   jax.ShapeDtypeStruct((b, nt, 8, tm), jnp.int32),
                   jax.ShapeDtypeStruct((b, t, LANES), F32),
                   jax.ShapeDtypeStruct((N_EXPERTS, LANES), F32)],
        scratch_shapes=[pltpu.VMEM((N_EXPERTS, LANES), F32)],
        compiler_params=_params("arbitrary", "arbitrary"),
        name="post_mixer_even" if even else "post_mixer_odd",
    )(x, *acts, modtab, *weights, gffn, rwt, rb)


EXPERT_ROW_TILE = 256
DMA_ISSUE_UNROLL = 8


def _moe_plan(sel, counts, n_tok):
    e_a, e_b, r_a, r_b = (sel[:, :, k, :].reshape(-1) for k in range(4))
    cnt = counts[:, 0].astype(jnp.int32)
    padded = ((cnt + EXPERT_ROW_TILE - 1) // EXPERT_ROW_TILE) * EXPERT_ROW_TILE
    ends = jnp.cumsum(padded)
    starts = ends - padded
    pos = jnp.concatenate([starts[e_a] + r_a, starts[e_b] + r_b]).astype(jnp.int32)
    n_tiles = 2 * n_tok // EXPERT_ROW_TILE + N_EXPERTS
    tile_start = jnp.arange(n_tiles, dtype=jnp.int32) * EXPERT_ROW_TILE
    tile_expert = jnp.minimum(jnp.sum(tile_start[:, None] >= ends[None, :], axis=1),
                              N_EXPERTS - 1).astype(jnp.int32)
    tiles_used = (ends[-1:] // EXPERT_ROW_TILE).astype(jnp.int32)
    return pos, tile_expert, tiles_used, n_tiles


def _dispatch_kernel(pos_ref, f_ref, init_ref, out_ref, sem, *, n_tok):
    del init_ref
    tm = f_ref.shape[1]
    base = (pl.program_id(0) * pl.num_programs(1) + pl.program_id(1)) * tm

    def row_copy(r, k):
        dst = pos_ref[k * n_tok + base + r]
        return pltpu.make_async_copy(f_ref.at[0, pl.ds(r, 1), :], out_ref.at[pl.ds(dst, 1), :], sem)

    def issue(r, carry):
        row_copy(r, 0).start()
        row_copy(r, 1).start()
        return carry

    lax.fori_loop(0, tm, issue, 0, unroll=DMA_ISSUE_UNROLL)
    for _ in range(2):
        pltpu.make_async_copy(f_ref.at[0], out_ref.at[pl.ds(0, tm), :], sem).wait()


def _dispatch(pos, f, n_rows):
    b, t, d = f.shape
    tm = ROW_TILE
    grid_spec = pltpu.PrefetchScalarGridSpec(
        num_scalar_prefetch=1,
        grid=(b, t // tm),
        in_specs=[pl.BlockSpec((1, tm, d), lambda bi, i, pos_ref: (bi, i, 0)),
                  pl.BlockSpec(memory_space=pl.ANY)],
        out_specs=pl.BlockSpec(memory_space=pl.ANY),
        scratch_shapes=[pltpu.SemaphoreType.DMA(())])
    return pl.pallas_call(
        functools.partial(_dispatch_kernel, n_tok=b * t),
        grid_spec=grid_spec,
        out_shape=jax.ShapeDtypeStruct((n_rows, d), F32),
        input_output_aliases={2: 0},
        compiler_params=_params("arbitrary", "arbitrary"),
        name="moe_dispatch",
    )(pos, f, jnp.zeros((n_rows, d), F32))


def _expert_ffn_kernel(te_ref, used_ref, x_ref, wg_ref, wu_ref, wd_ref, y_ref):
    del te_ref
    live = pl.program_id(0) < used_ref[0]

    @pl.when(live)
    def _():
        xb = x_ref[...].astype(BF16)
        he = _silu(_dot(xb, wg_ref[0])) * _dot(xb, wu_ref[0])
        y_ref[...] = _dot(he.astype(BF16), wd_ref[0])

    @pl.when(jnp.logical_not(live))
    def _():
        y_ref[...] = jnp.zeros(y_ref.shape, F32)


def _expert_ffn(tile_expert, tiles_used, xs, wg, wu, wd, n_tiles):
    n_rows, d = xs.shape
    _, _, d_e = wg.shape
    tr = EXPERT_ROW_TILE
    grid_spec = pltpu.PrefetchScalarGridSpec(
        num_scalar_prefetch=2,
        grid=(n_tiles,),
        in_specs=[pl.BlockSpec((tr, d), lambda j, te, used: (j, 0)),
                  pl.BlockSpec((1, d, d_e), lambda j, te, used: (te[j], 0, 0)),
                  pl.BlockSpec((1, d, d_e), lambda j, te, used: (te[j], 0, 0)),
                  pl.BlockSpec((1, d_e, d), lambda j, te, used: (te[j], 0, 0))],
        out_specs=pl.BlockSpec((tr, d), lambda j, te, used: (j, 0)))
    return pl.pallas_call(
        _expert_ffn_kernel,
        grid_spec=grid_spec,
        out_shape=jax.ShapeDtypeStruct((n_rows, d), F32),
        compiler_params=_params("arbitrary"),
        name="moe_expert_ffn",
    )(tile_expert, tiles_used, xs, wg, wu, wd)


def _combine_kernel(pos_ref, y_ref, w_ref, x_ref, modc_ref, modl_ref, fg_ref, o_ref, buf, sem,
                    *, n_tok, n_ctx, final_norm):
    i = pl.program_id(1)
    tm = x_ref.shape[1]
    base = (pl.program_id(0) * pl.num_programs(1) + i) * tm

    def row_copy(r, k):
        src = pos_ref[k * n_tok + base + r]
        return pltpu.make_async_copy(y_ref.at[pl.ds(src, 1), :], buf.at[k, pl.ds(r, 1), :], sem)

    def issue(r, carry):
        row_copy(r, 0).start()
        row_copy(r, 1).start()
        return carry

    lax.fori_loop(0, tm, issue, 0, unroll=DMA_ISSUE_UNROLL)
    for k in range(2):
        pltpu.make_async_copy(y_ref.at[pl.ds(0, tm), :], buf.at[k], sem).wait()
    w = w_ref[0]
    y = buf[0] * w[:, 0:1] + buf[1] * w[:, 1:2]
    rows = i * tm + lax.broadcasted_iota(jnp.int32, (tm, 1), 0)
    gate = jnp.where(rows < n_ctx, modc_ref[0, 5:6, :], modl_ref[0, 5:6, :])
    out = x_ref[0] + gate * y
    if final_norm:
        out = out * lax.rsqrt(jnp.mean(out * out, axis=-1, keepdims=True) + NORM_EPS) * fg_ref[...]
    o_ref[0] = out


def _combine(pos, ys, w_col, xmid, modtab, final_g, n_ctx, final_norm):
    b, t, d = xmid.shape
    tm = ROW_TILE
    row = lambda w: pl.BlockSpec((1, tm, w), lambda bi, i, pos_ref: (bi, i, 0))
    grid_spec = pltpu.PrefetchScalarGridSpec(
        num_scalar_prefetch=1,
        grid=(b, t // tm),
        in_specs=[pl.BlockSpec(memory_space=pl.ANY), row(LANES), row(d),
                  pl.BlockSpec((1, 8, d), lambda bi, i, pos_ref: (2 * bi, 0, 0)),
                  pl.BlockSpec((1, 8, d), lambda bi, i, pos_ref: (2 * bi + 1, 0, 0)),
                  pl.BlockSpec(final_g.shape, lambda bi, i, pos_ref: (0, 0))],
        out_specs=row(d),
        scratch_shapes=[pltpu.VMEM((2, tm, d), F32), pltpu.SemaphoreType.DMA(())])
    return pl.pallas_call(
        functools.partial(_combine_kernel, n_tok=b * t, n_ctx=n_ctx, final_norm=final_norm),
        grid_spec=grid_spec,
        out_shape=jax.ShapeDtypeStruct((b, t, d), F32),
        compiler_params=_params("arbitrary", "arbitrary"),
        name="moe_combine",
    )(pos, ys, w_col, xmid, modtab, modtab, final_g)


def _moe(f, sel, w_col, counts, wg, wu, wd, xmid, modtab, final_g, n_ctx, final_norm):
    b, t, _ = f.shape
    pos, tile_expert, tiles_used, n_tiles = _moe_plan(sel, counts, b * t)
    xs = _dispatch(pos, f, n_tiles * EXPERT_ROW_TILE)
    ys = _expert_ffn(tile_expert, tiles_used, xs, wg, wu, wd, n_tiles)
    return _combine(pos, ys, w_col, xmid, modtab, final_g, n_ctx, final_norm)


def _chan_dft_kernel(x_ref, mod_ref, g_ref, w_ref, zr_ref, zi_ref):
    h = _norm_mod(x_ref[0], g_ref[...], mod_ref[0, 0:1, :], mod_ref[0, 1:2, :])
    gd = w_ref.shape[0]
    for gi in range(h.shape[1] // gd):
        z = _dot_split(h[:, gi * gd:(gi + 1) * gd], w_ref[...])
        zr_ref[0, :, gi * gd:(gi + 1) * gd] = z[:, :gd]
        zi_ref[0, :, gi * gd:(gi + 1) * gd] = z[:, gd:]


def _chan_dft(x, modtab, g, w_cs, x_row_off, t):
    b, _, d = x.shape
    tm = ROW_TILE
    off = x_row_off // tm
    row = pl.BlockSpec((1, tm, d), lambda bi, i: (bi, i, 0))
    return pl.pallas_call(
        _chan_dft_kernel,
        grid=(b, t // tm),
        in_specs=[pl.BlockSpec((1, tm, d), lambda bi, i: (bi, i + off, 0)),
                  pl.BlockSpec((1, 8, d), lambda bi, i: (2 * bi + 1, 0, 0)),
                  pl.BlockSpec(g.shape, lambda bi, i: (0, 0)),
                  pl.BlockSpec(w_cs.shape, lambda bi, i: (0, 0))],
        out_specs=[row, row],
        out_shape=[jax.ShapeDtypeStruct((b, t, d), F32)] * 2,
        compiler_params=_params("parallel", "parallel"),
        name="chan_dft",
    )(x, modtab, g, w_cs)


def _dft1_kernel(zr_ref, zi_ref, w_ref, yr_ref, yi_ref):
    n1 = zr_ref.shape[1]
    y = _dot_split(w_ref[...], jnp.concatenate([zr_ref[0], zi_ref[0]], axis=0))
    yr_ref[0] = y[:n1]
    yi_ref[0] = y[n1:]


def _dft1(zr, zi, w1):
    b, n1, cols = zr.shape
    tn = min(cols, 4096)
    blk = pl.BlockSpec((1, n1, tn), lambda bi, j: (bi, 0, j))
    return pl.pallas_call(
        _dft1_kernel,
        grid=(b, cols // tn),
        in_specs=[blk, blk, pl.BlockSpec(w1.shape, lambda bi, j: (0, 0))],
        out_specs=[blk, blk],
        out_shape=[jax.ShapeDtypeStruct(zr.shape, F32)] * 2,
        compiler_params=_params("parallel", "parallel"),
        name="dft_stage1",
    )(zr, zi, w1)


def _dft2_kernel(yr_ref, yi_ref, tab_ref, o_ref):
    y = jnp.concatenate([yr_ref[0, 0], yi_ref[0, 0]], axis=0)
    o_ref[0, 0] = _dot_split(tab_ref[0], y)


def _dft2(yr, yi, tab):
    b, n1, n2, d = yr.shape
    blk = pl.BlockSpec((1, 1, n2, d), lambda bi, k1: (bi, k1, 0, 0))
    return pl.pallas_call(
        _dft2_kernel,
        grid=(b, n1),
        in_specs=[blk, blk, pl.BlockSpec((1, n2, 2 * n2), lambda bi, k1: (k1, 0, 0))],
        out_specs=blk,
        out_shape=jax.ShapeDtypeStruct(yr.shape, F32),
        compiler_params=_params("parallel", "parallel"),
        name="dft_stage2",
    )(yr, yi, tab)


def _dft_tables(t, gd):
    n2 = ML_CHUNK
    n1 = t // n2
    def cs(num, den):
        ang = (2.0 * np.pi / den) * (num % den).astype(np.float64)
        return np.cos(ang), np.sin(ang)
    c = np.arange(gd)
    cc, sc = cs(np.outer(c, c), gd)
    w_cs = np.concatenate([cc, -sc], axis=1)
    a = np.arange(n1)
    c1, s1 = cs(np.outer(a, a), n1)
    w1 = np.block([[c1, s1], [-s1, c1]])
    k = a[:, None, None] + n1 * np.arange(n2)[None, :, None]
    c2, s2 = cs(k * np.arange(n2)[None, None, :], t)
    tab = np.concatenate([c2, s2], axis=2) / math.sqrt(t * gd)
    return (jnp.asarray(w_cs, F32), jnp.asarray(w1, F32), jnp.asarray(tab, F32))


def _rope_tables(n_ctx, n_lat):
    pos = jnp.arange(n_lat, dtype=jnp.int32)
    n_axis = DA_QK_DIM // 4
    inv = ROPE_BASE ** (-jnp.arange(n_axis, dtype=F32) / n_axis)
    ang = jnp.concatenate([(pos // GRID_W).astype(F32)[:, None] * inv,
                           (pos % GRID_W).astype(F32)[:, None] * inv], axis=-1)
    cos, sin = jnp.cos(ang), jnp.sin(ang)
    cos = jnp.concatenate([jnp.ones((n_ctx, 2 * n_axis), F32), cos], axis=0)
    sin = jnp.concatenate([jnp.zeros((n_ctx, 2 * n_axis), F32), sin], axis=0)
    cos128 = jnp.concatenate([cos, cos, cos, cos], axis=1)
    sin128 = jnp.concatenate([-sin, sin, -sin, sin], axis=1)
    return cos128, sin128


def _deinterleave(w):
    d, n = w.shape
    w = w.reshape(d, n // DA_QK_DIM, DA_QK_DIM // 2, 2)
    return jnp.concatenate([w[..., 0], w[..., 1]], axis=-1).reshape(d, n)


def _pad_rows(a, rows):
    return jnp.concatenate([a, jnp.zeros((rows - a.shape[0],) + a.shape[1:], a.dtype)], axis=0)


def _pad_cols(a, cols):
    return jnp.concatenate([a, jnp.zeros(a.shape[:-1] + (cols - a.shape[-1],), a.dtype)], axis=-1)


def kernel(x, c, ctx, c_ctx, ada_w, ada_b, norm_mix_g, norm_ffn_g, even_w_in, even_w_out,
           even_conv_w, even_gate_b, even_lam, even_subln_g, odd_w_fnet, router_w, router_b,
           exp_w_gate, exp_w_up, exp_w_down, final_g):
    b, n_lat, d = x.shape
    n_ctx = ctx.shape[1]
    depth = ada_w.shape[0]
    assert depth == 2 and b + 1 <= 8
    assert n_ctx % ROW_TILE == 0 and n_lat % ROW_TILE == 0

    cond8 = _pad_rows(jnp.concatenate([c_ctx[None, :], c], axis=0), 8)
    rwt = router_w.T
    rb = jnp.broadcast_to(router_b[:, None], (N_EXPERTS, LANES))
    row2 = lambda v: v.reshape(1, -1)

    def modtab_for(layer):
        mods = _ada_mods(cond8, ada_w[layer], ada_b[layer]).reshape(8, 6, d)
        mods = jnp.concatenate([mods, jnp.zeros((8, 2, d), F32)], axis=1)
        idx = np.array([[0, 1 + bi] for bi in range(b)]).reshape(-1)
        return mods[idx]

    xs = jnp.concatenate([ctx, x], axis=1)
    modtab = modtab_for(0)
    w_in = even_w_in[0]
    o1 = DA_HEADS * 2 * DA_QK_DIM
    o2 = 2 * o1
    o3 = o2 + DA_HEADS * DA_V_DIM
    o4 = o3 + 2 * ML_HEADS * ML_DIM
    o5 = o4 + ML_HEADS * ML_DIM
    o6 = o5 + ML_HEADS * ML_DIM
    ws = [(_deinterleave(w_in[:, :o1]) * (DA_QK_DIM ** -0.5)).astype(BF16),
          _deinterleave(w_in[:, o1:o2]).astype(BF16),
          w_in[:, o2:o3].astype(BF16), w_in[:, o3:o4].astype(BF16),
          w_in[:, o4:o5].astype(BF16), w_in[:, o5:o6].astype(BF16),
          _pad_cols(w_in[:, o6:], LANES).astype(BF16)]
    gate_b = _pad_cols(even_gate_b[0].reshape(1, -1), LANES)
    cos128, sin128 = _rope_tables(n_ctx, n_lat)
    daq, dak, dav, mqk, mv, og, gates = _inproj(xs, modtab, row2(norm_mix_g[0]), cos128, sin128,
                                                 ws, gate_b, n_ctx)
    mq, mk, gc, gr = _mlprep(mqk, _pad_rows(even_conv_w[0], 8), gates, n_ctx)
    hf, hb = _mlstm(mq, mk, mv, gc, gr, n_ctx)
    lam_init = 0.8 - 0.6 * math.exp(-0.3 * 0)
    lam8 = _pad_rows(even_lam[0], 8)
    att = _diff_attention(daq, dak, dav, lam8, row2(even_subln_g[0]), n_ctx, lam_init)
    w_out = even_w_out[0].astype(BF16)
    half = DA_HEADS * DA_V_DIM
    xmid, f, sel, w_col, counts = _post_mixer(xs, [att, hf, hb, og], [w_out[:half], w_out[half:]],
                                              modtab, row2(norm_ffn_g[0]), rwt, rb, n_ctx, 0, True)
    xs = _moe(f, sel, w_col, counts, exp_w_gate[0].astype(BF16), exp_w_up[0].astype(BF16),
              exp_w_down[0].astype(BF16), xmid, modtab, row2(final_g), n_ctx, False)

    modtab = modtab_for(1)
    gd = d // FN_GROUPS
    w_cs, w1, tab = _dft_tables(n_lat, gd)
    n2 = ML_CHUNK
    n1 = n_lat // n2
    zr, zi = _chan_dft(xs, modtab, row2(norm_mix_g[1]), w_cs, n_ctx, n_lat)
    yr, yi = _dft1(zr.reshape(b, n1, n2 * d), zi.reshape(b, n1, n2 * d), w1)
    fo = _dft2(yr.reshape(b, n1, n2, d), yi.reshape(b, n1, n2, d), tab)
    fo = fo.transpose(0, 2, 1, 3).reshape(b, n_lat, d)
    xmid, f, sel, w_col, counts = _post_mixer(xs, [fo], [odd_w_fnet[0].astype(BF16)], modtab,
                                              row2(norm_ffn_g[1]), rwt, rb, 0, n_ctx, False)
    return _moe(f, sel, w_col, counts, exp_w_gate[1].astype(BF16), exp_w_up[1].astype(BF16),
                exp_w_down[1].astype(BF16), xmid, modtab, row2(final_g), 0, True)
```

```python
import functools
import math

import jax
import jax.numpy as jnp
import numpy as np
from jax import lax
from jax.experimental import pallas as pl
from jax.experimental.pallas import tpu as pltpu

F32 = jnp.float32
BF16 = jnp.bfloat16

NORM_EPS = 1e-6
GRID_W = 64
DA_HEADS = 4
DA_QK_DIM = 64
DA_V_DIM = 128
ML_HEADS = 4
ML_DIM = 128
ML_CHUNK = 128
FN_GROUPS = 4
N_EXPERTS = 16
EXPERTS_PER_GROUP = 4
ROPE_BASE = 10000.0
LANES = 128
ROW_TILE = 256
VMEM_LIMIT_BYTES = 56 * 1024 * 1024
HI = lax.Precision.HIGHEST
LOG2_E = math.log2(math.e)


def _params(*sem):
    return pltpu.CompilerParams(dimension_semantics=sem, vmem_limit_bytes=VMEM_LIMIT_BYTES)


def _dot(a, b, precision=None):
    return jnp.dot(a, b, preferred_element_type=F32, precision=precision)


def _dot_nt(a, b, precision=None):
    return lax.dot_general(a, b, (((1,), (1,)), ((), ())), preferred_element_type=F32,
                           precision=precision)


def _dot_split(a, b):
    a_hi = a.astype(BF16)
    b_hi = b.astype(BF16)
    a_lo = (a - a_hi.astype(F32)).astype(BF16)
    b_lo = (b - b_hi.astype(F32)).astype(BF16)
    return _dot(a_hi, b_hi) + (_dot(a_hi, b_lo) + _dot(a_lo, b_hi))


def _sigmoid(x):
    return 1.0 / (1.0 + jnp.exp(-x))


def _silu(x):
    return x * _sigmoid(x)


def _norm_mod(x, g, shift, scale):
    y = x * lax.rsqrt(jnp.mean(x * x, axis=-1, keepdims=True) + NORM_EPS) * g
    return y * (1.0 + scale) + shift


def _ada_kernel(c_ref, w_ref, b_ref, o_ref):
    o_ref[...] = _dot(_silu(c_ref[...]), w_ref[...], HI) + b_ref[...]


def _ada_mods(cond8, w, b):
    d, n = w.shape
    tn = n // 6
    return pl.pallas_call(
        _ada_kernel,
        grid=(6,),
        in_specs=[pl.BlockSpec((8, d), lambda j: (0, 0)),
                  pl.BlockSpec((d, tn), lambda j: (0, j)),
                  pl.BlockSpec((1, tn), lambda j: (0, j))],
        out_specs=pl.BlockSpec((8, tn), lambda j: (0, j)),
        out_shape=jax.ShapeDtypeStruct((8, n), F32),
        compiler_params=_params("arbitrary"),
        name="ada_mods",
    )(cond8, w, b.reshape(1, n))


def _inproj_kernel(x_ref, mod_ref, g_ref, cos_ref, sin_ref, wq_ref, wk_ref, wv_ref, wmqk_ref,
                   wmv_ref, wmo_ref, wg_ref, gb_ref,
                   q_ref, k_ref, v_ref, mqk_ref, mv_ref, og_ref, gate_ref):
    x = x_ref[0]
    h = _norm_mod(x, g_ref[...], mod_ref[0, 0:1, :], mod_ref[0, 1:2, :]).astype(BF16)
    tm = x.shape[0]
    width = q_ref.shape[2]
    cos = jnp.concatenate([cos_ref[...]] * (width // LANES), axis=1)
    sin = jnp.concatenate([sin_ref[...]] * (width // LANES), axis=1)
    lane = lax.broadcasted_iota(jnp.int32, (tm, width), 1)
    lower = (lane & (DA_QK_DIM - 1)) < (DA_QK_DIM // 2)

    def rope(u):
        swapped = jnp.where(lower, pltpu.roll(u, width - DA_QK_DIM // 2, 1),
                            pltpu.roll(u, DA_QK_DIM // 2, 1))
        return u * cos + swapped * sin

    q_ref[0] = (rope(_dot(h, wq_ref[...])) * LOG2_E).astype(BF16)
    k_ref[0] = rope(_dot(h, wk_ref[...])).astype(BF16)
    v = _dot(h, wv_ref[...]).astype(BF16)
    ones_col = (lax.broadcasted_iota(jnp.int32, (tm, DA_V_DIM), 1) == 0).astype(BF16)
    v_ref[0] = jnp.concatenate(
        [blk for hd in range(DA_HEADS) for blk in (v[:, hd * DA_V_DIM:(hd + 1) * DA_V_DIM], ones_col)],
        axis=1)
    mqk_ref[0] = _dot(h, wmqk_ref[...]).astype(BF16)
    mv_ref[0] = _dot(h, wmv_ref[...]).astype(BF16)
    og_ref[0] = _sigmoid(_dot(h, wmo_ref[...])).astype(BF16)
    g = _dot(h, wg_ref[...]) + gb_ref[...]
    glane = lax.broadcasted_iota(jnp.int32, g.shape, 1)
    is_forget = ((glane // ML_HEADS) & 1) == 1
    log_sig = jnp.minimum(g, 0.0) - jnp.log(1.0 + jnp.exp(-jnp.abs(g)))
    gate_ref[0] = jnp.where(is_forget, log_sig, g)


def _inproj(xs, modtab, g, cos, sin, ws, gate_b, n_ctx):
    b, t, d = xs.shape
    tm = ROW_TILE
    nt = t // tm
    ctx_tiles = n_ctx // tm
    row = lambda w: pl.BlockSpec((1, tm, w), lambda bi, i: (bi, i, 0))
    full = lambda a: pl.BlockSpec(a.shape, lambda bi, i: (0,) * a.ndim)
    widths = [w.shape[1] for w in ws]
    widths[2] *= 2
    out_dtypes = [BF16] * 6 + [F32]
    return pl.pallas_call(
        _inproj_kernel,
        grid=(b, nt),
        in_specs=[row(d),
                  pl.BlockSpec((1, 8, d), lambda bi, i: (2 * bi + (i >= ctx_tiles).astype(jnp.int32), 0, 0)),
                  full(g),
                  pl.BlockSpec((tm, LANES), lambda bi, i: (i, 0)),
                  pl.BlockSpec((tm, LANES), lambda bi, i: (i, 0))]
                 + [full(w) for w in ws] + [full(gate_b)],
        out_specs=[row(w) for w in widths],
        out_shape=[jax.ShapeDtypeStruct((b, t, w), dt) for w, dt in zip(widths, out_dtypes)],
        compiler_params=_params("parallel", "parallel"),
        name="inproj",
    )(xs, modtab, g, cos, sin, *ws, gate_b)


def _split3(x):
    x1 = x.astype(BF16)
    r1 = x - x1.astype(F32)
    x2 = r1.astype(BF16)
    x3 = (r1 - x2.astype(F32)).astype(BF16)
    return x1, x2, x3


def _mlprep_kernel(cur_ref, prev_ref, next_ref, cw_ref, gate_ref,
                   mq_ref, mk_ref, gc_ref, gr_ref, *, ctx_tiles, n_tiles):
    i = pl.program_id(1)
    cur = cur_ref[0].astype(F32)
    tm, w = cur.shape
    prev_ok = i != ctx_tiles
    if ctx_tiles > 0:
        prev_ok = jnp.logical_and(prev_ok, i != 0)
        next_ok = jnp.logical_and(i != ctx_tiles - 1, i != n_tiles - 1)
    else:
        next_ok = i != n_tiles - 1
    prev_row = jnp.where(prev_ok, prev_ref[0, 7:8, :].astype(F32), 0.0)
    next_row = jnp.where(next_ok, next_ref[0, 0:1, :].astype(F32), 0.0)
    ridx = lax.broadcasted_iota(jnp.int32, (tm, w), 0)
    before = jnp.where(ridx == 0, prev_row, pltpu.roll(cur, 1, 0))
    after = jnp.where(ridx == tm - 1, next_row, pltpu.roll(cur, tm - 1, 0))
    y = _silu(before * cw_ref[0:1, :] + cur * cw_ref[1:2, :] + after * cw_ref[2:3, :])
    half = w // 2
    mq_ref[0] = y[:, :half].astype(BF16)
    mk_ref[0] = (y[:, half:] * (ML_DIM ** -0.5)).astype(BF16)

    n_g = 4 * ML_HEADS
    r = lax.broadcasted_iota(jnp.int32, (ML_CHUNK, ML_CHUNK), 0)
    c = lax.broadcasted_iota(jnp.int32, (ML_CHUNK, ML_CHUNK), 1)
    lower = (c <= r).astype(BF16)
    upper = (c >= r).astype(BF16)
    for ci in range(tm // ML_CHUNK):
        gch = gate_ref[0, ci * ML_CHUNK:(ci + 1) * ML_CHUNK, :]
        gm = jnp.where(c < n_g, gch, 0.0)
        pre = sum(_dot(lower, p) for p in _split3(pltpu.roll(gm, n_g, 1)))
        suf = sum(_dot(upper, p) for p in _split3(pltpu.roll(gm, 2 * n_g, 1)))
        col = gm + pre + suf
        gc_ref[0, ci * ML_CHUNK:(ci + 1) * ML_CHUNK, :] = col
        gr_ref[0, ci] = col.T


def _mlprep(mqk, conv_w8, gates, n_ctx):
    b, t, w = mqk.shape
    tm = ROW_TILE
    nt = t // tm
    sub = tm // 8
    nsub = t // 8
    cpt = tm // ML_CHUNK
    kern = functools.partial(_mlprep_kernel, ctx_tiles=n_ctx // tm, n_tiles=nt)
    return pl.pallas_call(
        kern,
        grid=(b, nt),
        in_specs=[pl.BlockSpec((1, tm, w), lambda bi, i: (bi, i, 0)),
                  pl.BlockSpec((1, 8, w), lambda bi, i: (bi, jnp.maximum(i * sub - 1, 0), 0)),
                  pl.BlockSpec((1, 8, w), lambda bi, i: (bi, jnp.minimum((i + 1) * sub, nsub - 1), 0)),
                  pl.BlockSpec(conv_w8.shape, lambda bi, i: (0, 0)),
                  pl.BlockSpec((1, tm, LANES), lambda bi, i: (bi, i, 0))],
        out_specs=[pl.BlockSpec((1, tm, w // 2), lambda bi, i: (bi, i, 0)),
                   pl.BlockSpec((1, tm, w // 2), lambda bi, i: (bi, i, 0)),
                   pl.BlockSpec((1, tm, LANES), lambda bi, i: (bi, i, 0)),
                   pl.BlockSpec((1, cpt, ML_CHUNK, LANES), lambda bi, i: (bi, i, 0, 0))],
        out_shape=[jax.ShapeDtypeStruct((b, t, w // 2), BF16),
                   jax.ShapeDtypeStruct((b, t, w // 2), BF16),
                   jax.ShapeDtypeStruct((b, t, LANES), F32),
                   jax.ShapeDtypeStruct((b, t // ML_CHUNK, ML_CHUNK, LANES), F32)],
        compiler_params=_params("parallel", "parallel"),
        name="mlstm_prep",
    )(mqk, mqk, mqk, conv_w8, gates)


def _mlstm_kernel(qf_ref, kf_ref, vf_ref, gcf_ref, grf_ref, qb_ref, kb_ref, vb_ref, gcb_ref, grb_ref,
                  hf_ref, hb_ref, state_ref, m_ref):
    s = pl.program_id(0)
    nb = qf_ref.shape[0]
    n_g = 4 * ML_HEADS

    @pl.when(s == 0)
    def _():
        state_ref[...] = jnp.zeros(state_ref.shape, F32)
        m_ref[...] = jnp.zeros(m_ref.shape, F32)

    qi = lax.broadcasted_iota(jnp.int32, (ML_CHUNK, ML_CHUNK), 0)
    ki = lax.broadcasted_iota(jnp.int32, (ML_CHUNK, ML_CHUNK), 1)
    ones_col = (ki == 0).astype(BF16)

    for bi in range(nb):
        for direction in range(2):
            q_ref, k_ref, v_ref, gc_ref, gr_ref, h_ref = (
                (qf_ref, kf_ref, vf_ref, gcf_ref, grf_ref, hf_ref) if direction == 0 else
                (qb_ref, kb_ref, vb_ref, gcb_ref, grb_ref, hb_ref))
            visible = (ki <= qi) if direction == 0 else (ki >= qi)
            gc = gc_ref[bi]
            gr = gr_ref[bi, 0]
            for hd in range(ML_HEADS):
                chain = (bi * 2 + direction) * ML_HEADS + hd
                c_li = 2 * direction * ML_HEADS + hd
                c_lf = c_li + ML_HEADS
                c_cs = c_lf + (n_g if direction == 0 else 2 * n_g)
                lo, hi = hd * ML_DIM, (hd + 1) * ML_DIM
                q = q_ref[bi, :, lo:hi]
                k = k_ref[bi, :, lo:hi]
                v_aug = jnp.concatenate([v_ref[bi, :, lo:hi], ones_col], axis=1)
                li_col = gc[:, c_li:c_li + 1]
                cs_col = gc[:, c_cs:c_cs + 1]
                li_row = gr[c_li:c_li + 1, :]
                lf_row = gr[c_lf:c_lf + 1, :]
                cs_row = gr[c_cs:c_cs + 1, :]
                m_st = m_ref[chain][:, 0:1]
                st = state_ref[chain]

                b_last = jnp.sum(lf_row, axis=1, keepdims=True)
                logw = b_last - cs_col + li_col
                m_new = jnp.maximum(b_last + m_st, jnp.max(logw, axis=0, keepdims=True))
                wgt = jnp.exp(logw - m_new)
                decay = jnp.exp(b_last + m_st - m_new)

                log_d = jnp.where(visible, cs_col - cs_row + li_row, -jnp.inf)
                m_inter = cs_col + m_st
                m_t = jnp.maximum(m_inter, jnp.max(log_d, axis=1, keepdims=True))
                dmat = jnp.exp(log_d - m_t)
                inter = jnp.exp(m_inter - m_t)

                sc = (_dot_nt(q, k) * dmat).astype(BF16)
                tot = inter * _dot(q, st.astype(BF16)) + _dot(sc, v_aug)
                den = jnp.maximum(jnp.abs(tot[:, ML_DIM:ML_DIM + 1]), jnp.exp(-m_t))
                h_ref[bi, :, lo:hi] = tot[:, :ML_DIM] / den

                kw_t = (k.astype(F32) * wgt).T.astype(BF16)
                state_ref[chain] = decay * st + _dot(kw_t, v_aug)
                m_ref[chain] = jnp.broadcast_to(m_new, (1, LANES))


def _mlstm(mq, mk, mv, gc, gr, n_ctx):
    b, t, w = mq.shape
    nc = t // ML_CHUNK
    ncc = n_ctx // ML_CHUNK
    fwd = lambda s: s
    bwd = lambda s: jnp.where(s < ncc, ncc - 1 - s, nc - 1 - s + ncc)
    tok = lambda f: pl.BlockSpec((b, ML_CHUNK, w), lambda s: (0, f(s), 0))
    gcs = lambda f: pl.BlockSpec((b, ML_CHUNK, LANES), lambda s: (0, f(s), 0))
    grs = lambda f: pl.BlockSpec((b, 1, ML_CHUNK, LANES), lambda s: (0, f(s), 0, 0))
    n_chain = b * 2 * ML_HEADS
    return pl.pallas_call(
        _mlstm_kernel,
        grid=(nc,),
        in_specs=[tok(fwd), tok(fwd), tok(fwd), gcs(fwd), grs(fwd),
                  tok(bwd), tok(bwd), tok(bwd), gcs(bwd), grs(bwd)],
        out_specs=[tok(fwd), tok(bwd)],
        out_shape=[jax.ShapeDtypeStruct((b, t, w), F32)] * 2,
        scratch_shapes=[pltpu.VMEM((n_chain, ML_DIM, 2 * ML_DIM), F32),
                        pltpu.VMEM((n_chain, 1, LANES), F32)],
        compiler_params=_params("arbitrary"),
        name="mlstm",
    )(mq, mk, mv, gc, gr, mq, mk, mv, gc, gr)


def _attn_kernel(q_ref, k_ref, v_ref, lam_ref, sg_ref, o_ref, qm_scr, *scr,
                 ctx_tiles, n_ctx, key_block, n_blocks, lam_init):
    s_scr = (scr[0:2], scr[2:4])
    bm_scr = (scr[4:6], scr[6:8])
    m_scr = scr[8:10]
    acc_scr = scr[10:12]
    i = pl.program_id(2)
    dv = DA_V_DIM
    q = q_ref[0]
    lane = lax.broadcasted_iota(jnp.int32, q.shape, 1)
    zero = jnp.zeros_like(q)
    qm_scr[0] = jnp.where(lane < DA_QK_DIM, q, zero)
    qm_scr[1] = jnp.where(lane >= DA_QK_DIM, q, zero)
    align = math.gcd(n_ctx, key_block)

    def scores(mp, start, size):
        return _dot_nt(qm_scr[mp], k_ref[0, pl.ds(start, size), :])

    def lane_max(sc):
        m = sc[:, 0:LANES]
        for t in range(1, sc.shape[1] // LANES):
            m = jnp.maximum(m, sc[:, t * LANES:(t + 1) * LANES])
        return m

    def row_max(bm):
        return jnp.broadcast_to(jnp.max(bm, axis=1, keepdims=True), bm.shape)

    def weights(sc, m):
        return jnp.concatenate([jnp.exp2(sc[:, t * LANES:(t + 1) * LANES] - m)
                                for t in range(sc.shape[1] // LANES)], axis=1).astype(BF16)

    for mp in range(2):
        sc = scores(mp, 0, n_ctx)
        m = row_max(lane_max(sc))
        m_scr[mp][...] = m
        acc_scr[mp][...] = _dot(weights(sc, m), v_ref[0, 0:n_ctx, :])

    def stage(j, slot):
        start = pl.multiple_of(n_ctx + j * key_block, align)
        for mp in range(2):
            sc = scores(mp, start, key_block)
            s_scr[mp][slot][...] = sc
            bm_scr[mp][slot][...] = lane_max(sc)

    def consume(j, slot):
        start = pl.multiple_of(n_ctx + j * key_block, align)
        for mp in range(2):
            m_old = m_scr[mp][...]
            m_new = jnp.maximum(m_old, row_max(bm_scr[mp][slot][...]))
            alpha = jnp.exp2(m_old - m_new)
            pv = _dot(weights(s_scr[mp][slot][...], m_new), v_ref[0, pl.ds(start, key_block), :])
            acc_scr[mp][...] = jnp.concatenate([alpha, alpha], axis=1) * acc_scr[mp][...] + pv
            m_scr[mp][...] = m_new

    @pl.when(i >= ctx_tiles)
    def _():
        stage(0, 0)

        def body(g, carry):
            stage(2 * g + 1, 1)
            consume(2 * g, 0)
            stage(2 * g + 2, 0)
            consume(2 * g + 1, 1)
            return carry
        lax.fori_loop(0, n_blocks // 2 - 1, body, 0)
        stage(n_blocks - 1, 1)
        consume(n_blocks - 2, 0)
        consume(n_blocks - 1, 1)

    lv = lam_ref[...]
    dot01 = jnp.sum(lv[0:1, :] * lv[1:2, :], axis=1, keepdims=True)
    dot23 = jnp.sum(lv[2:3, :] * lv[3:4, :], axis=1, keepdims=True)
    lam = jnp.exp(dot01) - jnp.exp(dot23) + lam_init
    a0 = acc_scr[0][...]
    a1 = acc_scr[1][...]
    o = a0[:, 0:dv] / a0[:, dv:dv + 1] - lam * (a1[:, 0:dv] / a1[:, dv:dv + 1])
    o = o * lax.rsqrt(jnp.mean(o * o, axis=-1, keepdims=True) + NORM_EPS) * sg_ref[...]
    o_ref[0] = (o * (1.0 - lam_init)).astype(BF16)


def _diff_attention(q, k, v_aug, lam8, subln_g, n_ctx, lam_init):
    b, t, w = q.shape
    tq = ROW_TILE
    n_lat = t - n_ctx
    key_block = 1024 if n_lat % 2048 == 0 else 512
    n_blocks = n_lat // key_block
    assert n_lat % (2 * key_block) == 0 and n_ctx % tq == 0 and n_ctx % LANES == 0
    kern = functools.partial(_attn_kernel, ctx_tiles=n_ctx // tq, n_ctx=n_ctx, key_block=key_block,
                             n_blocks=n_blocks, lam_init=lam_init)
    return pl.pallas_call(
        kern,
        grid=(b, DA_HEADS, t // tq),
        in_specs=[pl.BlockSpec((1, tq, LANES), lambda bi, h, i: (bi, i, h)),
                  pl.BlockSpec((1, t, LANES), lambda bi, h, i: (bi, 0, h)),
                  pl.BlockSpec((1, t, 2 * DA_V_DIM), lambda bi, h, i: (bi, 0, h)),
                  pl.BlockSpec(lam8.shape, lambda bi, h, i: (0, 0)),
                  pl.BlockSpec(subln_g.shape, lambda bi, h, i: (0, 0))],
        out_specs=pl.BlockSpec((1, tq, LANES), lambda bi, h, i: (bi, i, h)),
        out_shape=jax.ShapeDtypeStruct((b, t, w), BF16),
        scratch_shapes=[pltpu.VMEM((2, tq, LANES), BF16)]
                       + [pltpu.VMEM((tq, key_block), F32)] * 4
                       + [pltpu.VMEM((tq, LANES), F32)] * 6
                       + [pltpu.VMEM((tq, 2 * DA_V_DIM), F32)] * 2,
        compiler_params=_params("parallel", "parallel", "arbitrary"),
        name="diff_attention",
    )(q, k, v_aug, lam8, subln_g)


def _top2_sum(a, b, c, d):
    hi1, lo1 = jnp.maximum(a, b), jnp.minimum(a, b)
    hi2, lo2 = jnp.maximum(c, d), jnp.minimum(c, d)
    return jnp.maximum(hi1, hi2) + jnp.maximum(jnp.minimum(hi1, hi2), jnp.maximum(lo1, lo2))


def _route(f, rwt_ref, rb_ref, cnt_ref):
    tm = f.shape[0]
    aff = _sigmoid(_dot_nt(rwt_ref[...], f, HI))
    biased = aff + rb_ref[:, 0:1]
    bz = [biased[e:e + 1, :] for e in range(N_EXPERTS)]
    af = [aff[e:e + 1, :] for e in range(N_EXPERTS)]
    n_grp = N_EXPERTS // EXPERTS_PER_GROUP
    scores = [_top2_sum(*bz[EXPERTS_PER_GROUP * g:EXPERTS_PER_GROUP * (g + 1)]) for g in range(n_grp)]
    best = scores[0]
    sel_grp = jnp.zeros_like(best, dtype=jnp.int32)
    for g in range(1, n_grp):
        better = scores[g] > best
        sel_grp = jnp.where(better, g, sel_grp)
        best = jnp.where(better, scores[g], best)
    chosen = []
    for e in range(N_EXPERTS):
        g = e // EXPERTS_PER_GROUP
        rank = jnp.zeros_like(sel_grp)
        for o in range(EXPERTS_PER_GROUP * g, EXPERTS_PER_GROUP * (g + 1)):
            if o == e:
                continue
            beats = (bz[o] > bz[e]) if o > e else (bz[o] >= bz[e])
            rank = rank + beats.astype(jnp.int32)
        chosen.append(jnp.logical_and(sel_grp == g, rank < 2))
    denom = sum(jnp.where(chosen[e], af[e], 0.0) for e in range(N_EXPERTS))
    erow = lax.broadcasted_iota(jnp.int32, (N_EXPERTS, tm), 0)
    one_hot = jnp.zeros((N_EXPERTS, tm), F32)
    for e in range(N_EXPERTS):
        one_hot = jnp.where(jnp.logical_and(erow == e, chosen[e]), 1.0, one_hot)
    earlier = (lax.broadcasted_iota(jnp.int32, (tm, tm), 0)
               < lax.broadcasted_iota(jnp.int32, (tm, tm), 1)).astype(BF16)
    rank_all = _dot(one_hot.astype(BF16), earlier) + cnt_ref[:, 0:1]
    cnt_ref[...] = cnt_ref[...] + jnp.sum(one_hot, axis=1, keepdims=True)

    seen = jnp.zeros((1, tm), jnp.bool_)
    e_a = e_b = jnp.zeros((1, tm), jnp.int32)
    r_a = r_b = w_a = w_b = jnp.zeros((1, tm), F32)
    for e in range(N_EXPERTS):
        first = jnp.logical_and(chosen[e], jnp.logical_not(seen))
        second = jnp.logical_and(chosen[e], seen)
        rk = rank_all[e:e + 1, :]
        wt = af[e] / denom
        e_a, e_b = jnp.where(first, e, e_a), jnp.where(second, e, e_b)
        r_a, r_b = jnp.where(first, rk, r_a), jnp.where(second, rk, r_b)
        w_a, w_b = jnp.where(first, wt, w_a), jnp.where(second, wt, w_b)
        seen = jnp.logical_or(seen, chosen[e])
    r8 = lax.broadcasted_iota(jnp.int32, (8, tm), 0)
    sel = jnp.where(r8 == 0, e_a, jnp.where(r8 == 1, e_b, jnp.where(
        r8 == 2, r_a.astype(jnp.int32), jnp.where(r8 == 3, r_b.astype(jnp.int32), 0))))
    row = lax.broadcasted_iota(jnp.int32, (LANES, tm), 0)
    w_t = jnp.where(row == 0, w_a, jnp.where(row == 1, w_b, 0.0))
    return sel, w_t.T


def _post_kernel(*refs, even):
    if even:
        (x_ref, a_ref, hf_ref, hb_ref, og_ref, mod_ref, wa_ref, wm_ref, gf_ref, rwt_ref, rb_ref,
         xo_ref, f_ref, sel_ref, w_ref, cnt_ref, cnt_scr) = refs
        m = ((hf_ref[0] + hb_ref[0]) * og_ref[0].astype(F32)).astype(BF16)
        o = _dot(a_ref[0], wa_ref[...]) + _dot(m, wm_ref[...])
    else:
        (x_ref, a_ref, mod_ref, wa_ref, gf_ref, rwt_ref, rb_ref,
         xo_ref, f_ref, sel_ref, w_ref, cnt_ref, cnt_scr) = refs
        o = _dot(a_ref[0].astype(BF16), wa_ref[...])

    @pl.when(jnp.logical_and(pl.program_id(0) == 0, pl.program_id(1) == 0))
    def _():
        cnt_scr[...] = jnp.zeros(cnt_scr.shape, F32)

    x = x_ref[0] + mod_ref[0, 2:3, :] * o
    xo_ref[0] = x
    f = _norm_mod(x, gf_ref[...], mod_ref[0, 3:4, :], mod_ref[0, 4:5, :])
    _store_row_tiles(f_ref.at[0], f)
    sel, w_col = _route(f, rwt_ref, rb_ref, cnt_scr)
    sel_ref[0, 0] = sel
    w_ref[0] = w_col
    cnt_ref[...] = cnt_scr[...]


def _post_mixer(x, acts, weights, modtab, gffn, rwt, rb, n_ctx, x_row_off, even):
    b, t, _ = acts[0].shape
    d = x.shape[2]
    tm = ROW_TILE
    nt = t // tm
    ctx_tiles = n_ctx // tm
    off = x_row_off // tm
    full = lambda a: pl.BlockSpec(a.shape, lambda bi, i: (0,) * a.ndim)
    row = lambda w: pl.BlockSpec((1, tm, w), lambda bi, i: (bi, i, 0))
    mod_spec = pl.BlockSpec(
        (1, 8, d), lambda bi, i: (2 * bi + (i + off >= ctx_tiles).astype(jnp.int32), 0, 0))
    in_specs = ([pl.BlockSpec((1, tm, d), lambda bi, i: (bi, i + off, 0))]
                + [row(a.shape[2]) for a in acts] + [mod_spec]
                + [full(w) for w in weights] + [full(gffn), full(rwt), full(rb)])
    return pl.pallas_call(
        functools.partial(_post_kernel, even=even),
        grid=(b, nt),
        in_specs=in_specs,
        out_specs=[row(d), pl.BlockSpec((1, tm * d // LANES, LANES), lambda bi, i: (bi, i, 0)),
                   pl.BlockSpec((1, 1, 8, tm), lambda bi, i: (bi, i, 0, 0)),
                   row(LANES),
                   pl.BlockSpec((N_EXPERTS, LANES), lambda bi, i: (0, 0))],
        out_shape=[jax.ShapeDtypeStruct((b, t, d), F32),
                   jax.ShapeDtypeStruct((b, t * d // LANES, LANES), F32),
                   jax.ShapeDtypeStruct((b, nt, 8, tm), jnp.int32),
                   jax.ShapeDtypeStruct((b, t, LANES), F32),
                   jax.ShapeDtypeStruct((N_EXPERTS, LANES), F32)],
        scratch_shapes=[pltpu.VMEM((N_EXPERTS, LANES), F32)],
        compiler_params=_params("arbitrary", "arbitrary"),
        name="post_mixer_even" if even else "post_mixer_odd",
    )(x, *acts, modtab, *weights, gffn, rwt, rb)


EXPERT_ROW_TILE = 256
DMA_ISSUE_UNROLL = 8


def _moe_plan(sel, counts, n_tok):
    e_a, e_b, r_a, r_b = (sel[:, :, k, :].reshape(-1) for k in range(4))
    cnt = counts[:, 0].astype(jnp.int32)
    padded = ((cnt + EXPERT_ROW_TILE - 1) // EXPERT_ROW_TILE) * EXPERT_ROW_TILE
    ends = jnp.cumsum(padded)
    starts = ends - padded
    pos = jnp.concatenate([starts[e_a] + r_a, starts[e_b] + r_b]).astype(jnp.int32)
    n_tiles = 2 * n_tok // EXPERT_ROW_TILE + N_EXPERTS
    tile_start = jnp.arange(n_tiles, dtype=jnp.int32) * EXPERT_ROW_TILE
    tile_expert = jnp.minimum(jnp.sum(tile_start[:, None] >= ends[None, :], axis=1),
                              N_EXPERTS - 1).astype(jnp.int32)
    tiles_used = (ends[-1:] // EXPERT_ROW_TILE).astype(jnp.int32)
    return pos, tile_expert, tiles_used, n_tiles


def _store_row_tiles(ref, x):
    rows, d = x.shape
    n_sub = d // LANES
    for s in range(n_sub):
        ref[pl.ds(s, rows, stride=n_sub), :] = x[:, s * LANES:(s + 1) * LANES]


def _load_row_tiles(ref, rows):
    n_sub = ref.shape[0] // rows
    return jnp.concatenate([ref[pl.ds(s, rows, stride=n_sub), :] for s in range(n_sub)], axis=1)


def _dispatch_kernel(pos_ref, f_ref, init_ref, out_ref, sem, *, n_tok, tm):
    del init_ref
    n_sub = f_ref.shape[1] // tm
    base = (pl.program_id(0) * pl.num_programs(1) + pl.program_id(1)) * tm

    def row_copy(r, k):
        dst = pos_ref[k * n_tok + base + r]
        src = f_ref.at[0, pl.ds(pl.multiple_of(r * n_sub, n_sub), n_sub), :]
        return pltpu.make_async_copy(src, out_ref.at[dst], sem)

    def issue(r, carry):
        row_copy(r, 0).start()
        row_copy(r, 1).start()
        return carry

    lax.fori_loop(0, tm, issue, 0, unroll=DMA_ISSUE_UNROLL)
    for _ in range(2):
        pltpu.make_async_copy(out_ref.at[pl.ds(0, tm)], out_ref.at[pl.ds(0, tm)], sem).wait()


def _dispatch(pos, f, n_rows, n_tok):
    b, rows, _ = f.shape
    n_sub = rows * b // n_tok
    tm = ROW_TILE
    grid_spec = pltpu.PrefetchScalarGridSpec(
        num_scalar_prefetch=1,
        grid=(b, n_tok // b // tm),
        in_specs=[pl.BlockSpec((1, tm * n_sub, LANES), lambda bi, i, pos_ref: (bi, i, 0)),
                  pl.BlockSpec(memory_space=pl.ANY)],
        out_specs=pl.BlockSpec(memory_space=pl.ANY),
        scratch_shapes=[pltpu.SemaphoreType.DMA(())])
    return pl.pallas_call(
        functools.partial(_dispatch_kernel, n_tok=n_tok, tm=tm),
        grid_spec=grid_spec,
        out_shape=jax.ShapeDtypeStruct((n_rows, n_sub, LANES), F32),
        input_output_aliases={2: 0},
        compiler_params=_params("arbitrary", "arbitrary"),
        name="moe_dispatch",
    )(pos, f, jnp.zeros((n_rows, n_sub, LANES), F32))


def _expert_ffn_kernel(te_ref, used_ref, x_ref, wg_ref, wu_ref, wd_ref, y_ref):
    del te_ref
    live = pl.program_id(0) < used_ref[0]

    @pl.when(live)
    def _():
        xb = _load_row_tiles(x_ref, EXPERT_ROW_TILE).astype(BF16)
        he = _silu(_dot(xb, wg_ref[0])) * _dot(xb, wu_ref[0])
        _store_row_tiles(y_ref, _dot(he.astype(BF16), wd_ref[0]))

    @pl.when(jnp.logical_not(live))
    def _():
        y_ref[...] = jnp.zeros(y_ref.shape, F32)


def _expert_ffn(tile_expert, tiles_used, xs, wg, wu, wd, n_tiles):
    n_rows, n_sub, _ = xs.shape
    _, d, d_e = wg.shape
    tr = EXPERT_ROW_TILE
    grid_spec = pltpu.PrefetchScalarGridSpec(
        num_scalar_prefetch=2,
        grid=(n_tiles,),
        in_specs=[pl.BlockSpec((tr * n_sub, LANES), lambda j, te, used: (j, 0)),
                  pl.BlockSpec((1, d, d_e), lambda j, te, used: (te[j], 0, 0)),
                  pl.BlockSpec((1, d, d_e), lambda j, te, used: (te[j], 0, 0)),
                  pl.BlockSpec((1, d_e, d), lambda j, te, used: (te[j], 0, 0))],
        out_specs=pl.BlockSpec((tr * n_sub, LANES), lambda j, te, used: (j, 0)))
    ys = pl.pallas_call(
        _expert_ffn_kernel,
        grid_spec=grid_spec,
        out_shape=jax.ShapeDtypeStruct((n_rows * n_sub, LANES), F32),
        compiler_params=_params("arbitrary"),
        name="moe_expert_ffn",
    )(tile_expert, tiles_used, xs.reshape(n_rows * n_sub, LANES), wg, wu, wd)
    return ys.reshape(n_rows, n_sub, LANES)


def _combine_kernel(pos_ref, y_ref, w_ref, x_ref, modc_ref, modl_ref, fg_ref, o_ref, buf, sem,
                    *, n_tok, n_ctx, final_norm):
    i = pl.program_id(1)
    tm = x_ref.shape[1]
    base = (pl.program_id(0) * pl.num_programs(1) + i) * tm

    n_sub = y_ref.shape[1]

    def row_copy(r, k):
        src = pos_ref[k * n_tok + base + r]
        dst = buf.at[k, pl.ds(pl.multiple_of(r * n_sub, n_sub), n_sub), :]
        return pltpu.make_async_copy(y_ref.at[src], dst, sem)

    def issue(r, carry):
        row_copy(r, 0).start()
        row_copy(r, 1).start()
        return carry

    lax.fori_loop(0, tm, issue, 0, unroll=DMA_ISSUE_UNROLL)
    for _ in range(2):
        pltpu.make_async_copy(y_ref.at[pl.ds(0, tm)], y_ref.at[pl.ds(0, tm)], sem).wait()
    w = w_ref[0]
    y = _load_row_tiles(buf.at[0], tm) * w[:, 0:1] + _load_row_tiles(buf.at[1], tm) * w[:, 1:2]
    rows = i * tm + lax.broadcasted_iota(jnp.int32, (tm, 1), 0)
    gate = jnp.where(rows < n_ctx, modc_ref[0, 5:6, :], modl_ref[0, 5:6, :])
    out = x_ref[0] + gate * y
    if final_norm:
        out = out * lax.rsqrt(jnp.mean(out * out, axis=-1, keepdims=True) + NORM_EPS) * fg_ref[...]
    o_ref[0] = out


def _combine(pos, ys, w_col, xmid, modtab, final_g, n_ctx, final_norm):
    b, t, d = xmid.shape
    tm = ROW_TILE
    row = lambda w: pl.BlockSpec((1, tm, w), lambda bi, i, pos_ref: (bi, i, 0))
    grid_spec = pltpu.PrefetchScalarGridSpec(
        num_scalar_prefetch=1,
        grid=(b, t // tm),
        in_specs=[pl.BlockSpec(memory_space=pl.ANY), row(LANES), row(d),
                  pl.BlockSpec((1, 8, d), lambda bi, i, pos_ref: (2 * bi, 0, 0)),
                  pl.BlockSpec((1, 8, d), lambda bi, i, pos_ref: (2 * bi + 1, 0, 0)),
                  pl.BlockSpec(final_g.shape, lambda bi, i, pos_ref: (0, 0))],
        out_specs=row(d),
        scratch_shapes=[pltpu.VMEM((2, tm * d // LANES, LANES), F32), pltpu.SemaphoreType.DMA(())])
    return pl.pallas_call(
        functools.partial(_combine_kernel, n_tok=b * t, n_ctx=n_ctx, final_norm=final_norm),
        grid_spec=grid_spec,
        out_shape=jax.ShapeDtypeStruct((b, t, d), F32),
        compiler_params=_params("arbitrary", "arbitrary"),
        name="moe_combine",
    )(pos, ys, w_col, xmid, modtab, modtab, final_g)


def _moe(f, sel, w_col, counts, wg, wu, wd, xmid, modtab, final_g, n_ctx, final_norm):
    b, t, _ = xmid.shape
    pos, tile_expert, tiles_used, n_tiles = _moe_plan(sel, counts, b * t)
    xs = _dispatch(pos, f, n_tiles * EXPERT_ROW_TILE, b * t)
    ys = _expert_ffn(tile_expert, tiles_used, xs, wg, wu, wd, n_tiles)
    return _combine(pos, ys, w_col, xmid, modtab, final_g, n_ctx, final_norm)


def _chan_dft_kernel(x_ref, mod_ref, g_ref, w_ref, zr_ref, zi_ref):
    h = _norm_mod(x_ref[0], g_ref[...], mod_ref[0, 0:1, :], mod_ref[0, 1:2, :])
    gd = w_ref.shape[0]
    for gi in range(h.shape[1] // gd):
        z = _dot_split(h[:, gi * gd:(gi + 1) * gd], w_ref[...])
        zr_ref[0, :, gi * gd:(gi + 1) * gd] = z[:, :gd]
        zi_ref[0, :, gi * gd:(gi + 1) * gd] = z[:, gd:]


def _chan_dft(x, modtab, g, w_cs, x_row_off, t):
    b, _, d = x.shape
    tm = ROW_TILE
    off = x_row_off // tm
    row = pl.BlockSpec((1, tm, d), lambda bi, i: (bi, i, 0))
    return pl.pallas_call(
        _chan_dft_kernel,
        grid=(b, t // tm),
        in_specs=[pl.BlockSpec((1, tm, d), lambda bi, i: (bi, i + off, 0)),
                  pl.BlockSpec((1, 8, d), lambda bi, i: (2 * bi + 1, 0, 0)),
                  pl.BlockSpec(g.shape, lambda bi, i: (0, 0)),
                  pl.BlockSpec(w_cs.shape, lambda bi, i: (0, 0))],
        out_specs=[row, row],
        out_shape=[jax.ShapeDtypeStruct((b, t, d), F32)] * 2,
        compiler_params=_params("parallel", "parallel"),
        name="chan_dft",
    )(x, modtab, g, w_cs)


def _dft1_kernel(zr_ref, zi_ref, w_ref, yr_ref, yi_ref):
    n1 = zr_ref.shape[1]
    y = _dot_split(w_ref[...], jnp.concatenate([zr_ref[0], zi_ref[0]], axis=0))
    yr_ref[0] = y[:n1]
    yi_ref[0] = y[n1:]


def _dft1(zr, zi, w1):
    b, n1, cols = zr.shape
    tn = min(cols, 4096)
    blk = pl.BlockSpec((1, n1, tn), lambda bi, j: (bi, 0, j))
    return pl.pallas_call(
        _dft1_kernel,
        grid=(b, cols // tn),
        in_specs=[blk, blk, pl.BlockSpec(w1.shape, lambda bi, j: (0, 0))],
        out_specs=[blk, blk],
        out_shape=[jax.ShapeDtypeStruct(zr.shape, F32)] * 2,
        compiler_params=_params("parallel", "parallel"),
        name="dft_stage1",
    )(zr, zi, w1)


def _dft2_kernel(yr_ref, yi_ref, tab_ref, o_ref):
    y = jnp.concatenate([yr_ref[0, 0], yi_ref[0, 0]], axis=0)
    o_ref[0, 0] = _dot_split(tab_ref[0], y)


def _dft2(yr, yi, tab):
    b, n1, n2, d = yr.shape
    blk = pl.BlockSpec((1, 1, n2, d), lambda bi, k1: (bi, k1, 0, 0))
    return pl.pallas_call(
        _dft2_kernel,
        grid=(b, n1),
        in_specs=[blk, blk, pl.BlockSpec((1, n2, 2 * n2), lambda bi, k1: (k1, 0, 0))],
        out_specs=blk,
        out_shape=jax.ShapeDtypeStruct(yr.shape, F32),
        compiler_params=_params("parallel", "parallel"),
        name="dft_stage2",
    )(yr, yi, tab)


def _dft_tables(t, gd):
    n2 = ML_CHUNK
    n1 = t // n2
    def cs(num, den):
        ang = (2.0 * np.pi / den) * (num % den).astype(np.float64)
        return np.cos(ang), np.sin(ang)
    c = np.arange(gd)
    cc, sc = cs(np.outer(c, c), gd)
    w_cs = np.concatenate([cc, -sc], axis=1)
    a = np.arange(n1)
    c1, s1 = cs(np.outer(a, a), n1)
    w1 = np.block([[c1, s1], [-s1, c1]])
    k = a[:, None, None] + n1 * np.arange(n2)[None, :, None]
    c2, s2 = cs(k * np.arange(n2)[None, None, :], t)
    tab = np.concatenate([c2, s2], axis=2) / math.sqrt(t * gd)
    return (jnp.asarray(w_cs, F32), jnp.asarray(w1, F32), jnp.asarray(tab, F32))


def _rope_tables(n_ctx, n_lat):
    pos = jnp.arange(n_lat, dtype=jnp.int32)
    n_axis = DA_QK_DIM // 4
    inv = ROPE_BASE ** (-jnp.arange(n_axis, dtype=F32) / n_axis)
    ang = jnp.concatenate([(pos // GRID_W).astype(F32)[:, None] * inv,
                           (pos % GRID_W).astype(F32)[:, None] * inv], axis=-1)
    cos, sin = jnp.cos(ang), jnp.sin(ang)
    cos = jnp.concatenate([jnp.ones((n_ctx, 2 * n_axis), F32), cos], axis=0)
    sin = jnp.concatenate([jnp.zeros((n_ctx, 2 * n_axis), F32), sin], axis=0)
    cos128 = jnp.concatenate([cos, cos, cos, cos], axis=1)
    sin128 = jnp.concatenate([-sin, sin, -sin, sin], axis=1)
    return cos128, sin128


def _deinterleave(w):
    d, n = w.shape
    w = w.reshape(d, n // DA_QK_DIM, DA_QK_DIM // 2, 2)
    return jnp.concatenate([w[..., 0], w[..., 1]], axis=-1).reshape(d, n)


def _pad_rows(a, rows):
    return jnp.concatenate([a, jnp.zeros((rows - a.shape[0],) + a.shape[1:], a.dtype)], axis=0)


def _pad_cols(a, cols):
    return jnp.concatenate([a, jnp.zeros(a.shape[:-1] + (cols - a.shape[-1],), a.dtype)], axis=-1)


def kernel(x, c, ctx, c_ctx, ada_w, ada_b, norm_mix_g, norm_ffn_g, even_w_in, even_w_out,
           even_conv_w, even_gate_b, even_lam, even_subln_g, odd_w_fnet, router_w, router_b,
           exp_w_gate, exp_w_up, exp_w_down, final_g):
    b, n_lat, d = x.shape
    n_ctx = ctx.shape[1]
    depth = ada_w.shape[0]
    assert depth == 2 and b + 1 <= 8
    assert n_ctx % ROW_TILE == 0 and n_lat % ROW_TILE == 0

    cond8 = _pad_rows(jnp.concatenate([c_ctx[None, :], c], axis=0), 8)
    rwt = router_w.T
    rb = jnp.broadcast_to(router_b[:, None], (N_EXPERTS, LANES))
    row2 = lambda v: v.reshape(1, -1)

    def modtab_for(layer):
        mods = _ada_mods(cond8, ada_w[layer], ada_b[layer]).reshape(8, 6, d)
        mods = jnp.concatenate([mods, jnp.zeros((8, 2, d), F32)], axis=1)
        idx = np.array([[0, 1 + bi] for bi in range(b)]).reshape(-1)
        return mods[idx]

    xs = jnp.concatenate([ctx, x], axis=1)
    modtab = modtab_for(0)
    w_in = even_w_in[0]
    o1 = DA_HEADS * 2 * DA_QK_DIM
    o2 = 2 * o1
    o3 = o2 + DA_HEADS * DA_V_DIM
    o4 = o3 + 2 * ML_HEADS * ML_DIM
    o5 = o4 + ML_HEADS * ML_DIM
    o6 = o5 + ML_HEADS * ML_DIM
    ws = [(_deinterleave(w_in[:, :o1]) * (DA_QK_DIM ** -0.5)).astype(BF16),
          _deinterleave(w_in[:, o1:o2]).astype(BF16),
          w_in[:, o2:o3].astype(BF16), w_in[:, o3:o4].astype(BF16),
          w_in[:, o4:o5].astype(BF16), w_in[:, o5:o6].astype(BF16),
          _pad_cols(w_in[:, o6:], LANES).astype(BF16)]
    gate_b = _pad_cols(even_gate_b[0].reshape(1, -1), LANES)
    cos128, sin128 = _rope_tables(n_ctx, n_lat)
    daq, dak, dav, mqk, mv, og, gates = _inproj(xs, modtab, row2(norm_mix_g[0]), cos128, sin128,
                                                 ws, gate_b, n_ctx)
    mq, mk, gc, gr = _mlprep(mqk, _pad_rows(even_conv_w[0], 8), gates, n_ctx)
    hf, hb = _mlstm(mq, mk, mv, gc, gr, n_ctx)
    lam_init = 0.8 - 0.6 * math.exp(-0.3 * 0)
    lam8 = _pad_rows(even_lam[0], 8)
    att = _diff_attention(daq, dak, dav, lam8, row2(even_subln_g[0]), n_ctx, lam_init)
    w_out = even_w_out[0].astype(BF16)
    half = DA_HEADS * DA_V_DIM
    xmid, f, sel, w_col, counts = _post_mixer(xs, [att, hf, hb, og], [w_out[:half], w_out[half:]],
                                              modtab, row2(norm_ffn_g[0]), rwt, rb, n_ctx, 0, True)
    xs = _moe(f, sel, w_col, counts, exp_w_gate[0].astype(BF16), exp_w_up[0].astype(BF16),
              exp_w_down[0].astype(BF16), xmid, modtab, row2(final_g), n_ctx, False)

    modtab = modtab_for(1)
    gd = d // FN_GROUPS
    w_cs, w1, tab = _dft_tables(n_lat, gd)
    n2 = ML_CHUNK
    n1 = n_lat // n2
    zr, zi = _chan_dft(xs, modtab, row2(norm_mix_g[1]), w_cs, n_ctx, n_lat)
    yr, yi = _dft1(zr.reshape(b, n1, n2 * d), zi.reshape(b, n1, n2 * d), w1)
    fo = _dft2(yr.reshape(b, n1, n2, d), yi.reshape(b, n1, n2, d), tab)
    fo = fo.transpose(0, 2, 1, 3).reshape(b, n_lat, d)
    xmid, f, sel, w_col, counts = _post_mixer(xs, [fo], [odd_w_fnet[0].astype(BF16)], modtab,
                                              row2(norm_ffn_g[1]), rwt, rb, 0, n_ctx, False)
    return _moe(f, sel, w_col, counts, exp_w_gate[1].astype(BF16), exp_w_up[1].astype(BF16),
                exp_w_down[1].astype(BF16), xmid, modtab, row2(final_g), 0, True)
```

```python
import functools
import math

import jax
import jax.numpy as jnp
import numpy as np
from jax import lax
from jax.experimental import pallas as pl
from jax.experimental.pallas import tpu as pltpu

F32 = jnp.float32
BF16 = jnp.bfloat16

NORM_EPS = 1e-6
GRID_W = 64
DA_HEADS = 4
DA_QK_DIM = 64
DA_V_DIM = 128
ML_HEADS = 4
ML_DIM = 128
ML_CHUNK = 128
FN_GROUPS = 4
N_EXPERTS = 16
EXPERTS_PER_GROUP = 4
ROPE_BASE = 10000.0
LANES = 128
ROW_TILE = 256
VMEM_LIMIT_BYTES = 56 * 1024 * 1024
HI = lax.Precision.HIGHEST
LOG2_E = math.log2(math.e)


def _params(*sem):
    return pltpu.CompilerParams(dimension_semantics=sem, vmem_limit_bytes=VMEM_LIMIT_BYTES)


def _dot(a, b, precision=None):
    return jnp.dot(a, b, preferred_element_type=F32, precision=precision)


def _dot_nt(a, b, precision=None):
    return lax.dot_general(a, b, (((1,), (1,)), ((), ())), preferred_element_type=F32,
                           precision=precision)


def _dot_split(a, b):
    a_hi = a.astype(BF16)
    b_hi = b.astype(BF16)
    a_lo = (a - a_hi.astype(F32)).astype(BF16)
    b_lo = (b - b_hi.astype(F32)).astype(BF16)
    return _dot(a_hi, b_hi) + (_dot(a_hi, b_lo) + _dot(a_lo, b_hi))


def _sigmoid(x):
    return 1.0 / (1.0 + jnp.exp(-x))


def _silu(x):
    return x * _sigmoid(x)


def _norm_mod(x, g, shift, scale):
    y = x * lax.rsqrt(jnp.mean(x * x, axis=-1, keepdims=True) + NORM_EPS) * g
    return y * (1.0 + scale) + shift


def _ada_kernel(c_ref, w_ref, b_ref, o_ref):
    o_ref[...] = _dot(_silu(c_ref[...]), w_ref[...], HI) + b_ref[...]


def _ada_mods(cond8, w, b):
    d, n = w.shape
    tn = n // 6
    return pl.pallas_call(
        _ada_kernel,
        grid=(6,),
        in_specs=[pl.BlockSpec((8, d), lambda j: (0, 0)),
                  pl.BlockSpec((d, tn), lambda j: (0, j)),
                  pl.BlockSpec((1, tn), lambda j: (0, j))],
        out_specs=pl.BlockSpec((8, tn), lambda j: (0, j)),
        out_shape=jax.ShapeDtypeStruct((8, n), F32),
        compiler_params=_params("arbitrary"),
        name="ada_mods",
    )(cond8, w, b.reshape(1, n))


def _inproj_kernel(x_ref, mod_ref, g_ref, cos_ref, sin_ref, wq_ref, wk_ref, wv_ref, wmqk_ref,
                   wmv_ref, wmo_ref, wg_ref, gb_ref,
                   q_ref, k_ref, v_ref, mqk_ref, mv_ref, og_ref, gate_ref):
    x = x_ref[0]
    h = _norm_mod(x, g_ref[...], mod_ref[0, 0:1, :], mod_ref[0, 1:2, :]).astype(BF16)
    tm = x.shape[0]
    width = q_ref.shape[2]
    cos = jnp.concatenate([cos_ref[...]] * (width // LANES), axis=1)
    sin = jnp.concatenate([sin_ref[...]] * (width // LANES), axis=1)
    lane = lax.broadcasted_iota(jnp.int32, (tm, width), 1)
    lower = (lane & (DA_QK_DIM - 1)) < (DA_QK_DIM // 2)

    def rope(u):
        swapped = jnp.where(lower, pltpu.roll(u, width - DA_QK_DIM // 2, 1),
                            pltpu.roll(u, DA_QK_DIM // 2, 1))
        return u * cos + swapped * sin

    q_ref[0] = (rope(_dot(h, wq_ref[...])) * LOG2_E).astype(BF16)
    k_ref[0] = rope(_dot(h, wk_ref[...])).astype(BF16)
    v = _dot(h, wv_ref[...]).astype(BF16)
    ones_col = (lax.broadcasted_iota(jnp.int32, (tm, DA_V_DIM), 1) == 0).astype(BF16)
    v_ref[0] = jnp.concatenate(
        [blk for hd in range(DA_HEADS) for blk in (v[:, hd * DA_V_DIM:(hd + 1) * DA_V_DIM], ones_col)],
        axis=1)
    mqk_ref[0] = _dot(h, wmqk_ref[...]).astype(BF16)
    mv_ref[0] = _dot(h, wmv_ref[...]).astype(BF16)
    og_ref[0] = _sigmoid(_dot(h, wmo_ref[...])).astype(BF16)
    g = _dot(h, wg_ref[...]) + gb_ref[...]
    glane = lax.broadcasted_iota(jnp.int32, g.shape, 1)
    is_forget = ((glane // ML_HEADS) & 1) == 1
    log_sig = jnp.minimum(g, 0.0) - jnp.log(1.0 + jnp.exp(-jnp.abs(g)))
    gate_ref[0] = jnp.where(is_forget, log_sig, g)


def _inproj(xs, modtab, g, cos, sin, ws, gate_b, n_ctx):
    b, t, d = xs.shape
    tm = ROW_TILE
    nt = t // tm
    ctx_tiles = n_ctx // tm
    row = lambda w: pl.BlockSpec((1, tm, w), lambda bi, i: (bi, i, 0))
    full = lambda a: pl.BlockSpec(a.shape, lambda bi, i: (0,) * a.ndim)
    widths = [w.shape[1] for w in ws]
    widths[2] *= 2
    out_dtypes = [BF16] * 6 + [F32]
    return pl.pallas_call(
        _inproj_kernel,
        grid=(b, nt),
        in_specs=[row(d),
                  pl.BlockSpec((1, 8, d), lambda bi, i: (2 * bi + (i >= ctx_tiles).astype(jnp.int32), 0, 0)),
                  full(g),
                  pl.BlockSpec((tm, LANES), lambda bi, i: (i, 0)),
                  pl.BlockSpec((tm, LANES), lambda bi, i: (i, 0))]
                 + [full(w) for w in ws] + [full(gate_b)],
        out_specs=[row(w) for w in widths],
        out_shape=[jax.ShapeDtypeStruct((b, t, w), dt) for w, dt in zip(widths, out_dtypes)],
        compiler_params=_params("parallel", "parallel"),
        name="inproj",
    )(xs, modtab, g, cos, sin, *ws, gate_b)


def _split3(x):
    x1 = x.astype(BF16)
    r1 = x - x1.astype(F32)
    x2 = r1.astype(BF16)
    x3 = (r1 - x2.astype(F32)).astype(BF16)
    return x1, x2, x3


def _mlprep_kernel(cur_ref, prev_ref, next_ref, cw_ref, gate_ref,
                   mq_ref, mk_ref, gc_ref, gr_ref, *, ctx_tiles, n_tiles):
    i = pl.program_id(1)
    cur = cur_ref[0].astype(F32)
    tm, w = cur.shape
    prev_ok = i != ctx_tiles
    if ctx_tiles > 0:
        prev_ok = jnp.logical_and(prev_ok, i != 0)
        next_ok = jnp.logical_and(i != ctx_tiles - 1, i != n_tiles - 1)
    else:
        next_ok = i != n_tiles - 1
    prev_row = jnp.where(prev_ok, prev_ref[0, 7:8, :].astype(F32), 0.0)
    next_row = jnp.where(next_ok, next_ref[0, 0:1, :].astype(F32), 0.0)
    ridx = lax.broadcasted_iota(jnp.int32, (tm, w), 0)
    before = jnp.where(ridx == 0, prev_row, pltpu.roll(cur, 1, 0))
    after = jnp.where(ridx == tm - 1, next_row, pltpu.roll(cur, tm - 1, 0))
    y = _silu(before * cw_ref[0:1, :] + cur * cw_ref[1:2, :] + after * cw_ref[2:3, :])
    half = w // 2
    mq_ref[0] = y[:, :half].astype(BF16)
    mk_ref[0] = (y[:, half:] * (ML_DIM ** -0.5)).astype(BF16)

    n_g = 4 * ML_HEADS
    r = lax.broadcasted_iota(jnp.int32, (ML_CHUNK, ML_CHUNK), 0)
    c = lax.broadcasted_iota(jnp.int32, (ML_CHUNK, ML_CHUNK), 1)
    lower = (c <= r).astype(BF16)
    upper = (c >= r).astype(BF16)
    for ci in range(tm // ML_CHUNK):
        gch = gate_ref[0, ci * ML_CHUNK:(ci + 1) * ML_CHUNK, :]
        gm = jnp.where(c < n_g, gch, 0.0)
        pre = sum(_dot(lower, p) for p in _split3(pltpu.roll(gm, n_g, 1)))
        suf = sum(_dot(upper, p) for p in _split3(pltpu.roll(gm, 2 * n_g, 1)))
        col = gm + pre + suf
        gc_ref[0, ci * ML_CHUNK:(ci + 1) * ML_CHUNK, :] = col
        gr_ref[0, ci] = col.T


def _mlprep(mqk, conv_w8, gates, n_ctx):
    b, t, w = mqk.shape
    tm = ROW_TILE
    nt = t // tm
    sub = tm // 8
    nsub = t // 8
    cpt = tm // ML_CHUNK
    kern = functools.partial(_mlprep_kernel, ctx_tiles=n_ctx // tm, n_tiles=nt)
    return pl.pallas_call(
        kern,
        grid=(b, nt),
        in_specs=[pl.BlockSpec((1, tm, w), lambda bi, i: (bi, i, 0)),
                  pl.BlockSpec((1, 8, w), lambda bi, i: (bi, jnp.maximum(i * sub - 1, 0), 0)),
                  pl.BlockSpec((1, 8, w), lambda bi, i: (bi, jnp.minimum((i + 1) * sub, nsub - 1), 0)),
                  pl.BlockSpec(conv_w8.shape, lambda bi, i: (0, 0)),
                  pl.BlockSpec((1, tm, LANES), lambda bi, i: (bi, i, 0))],
        out_specs=[pl.BlockSpec((1, tm, w // 2), lambda bi, i: (bi, i, 0)),
                   pl.BlockSpec((1, tm, w // 2), lambda bi, i: (bi, i, 0)),
                   pl.BlockSpec((1, tm, LANES), lambda bi, i: (bi, i, 0)),
                   pl.BlockSpec((1, cpt, ML_CHUNK, LANES), lambda bi, i: (bi, i, 0, 0))],
        out_shape=[jax.ShapeDtypeStruct((b, t, w // 2), BF16),
                   jax.ShapeDtypeStruct((b, t, w // 2), BF16),
                   jax.ShapeDtypeStruct((b, t, LANES), F32),
                   jax.ShapeDtypeStruct((b, t // ML_CHUNK, ML_CHUNK, LANES), F32)],
        compiler_params=_params("parallel", "parallel"),
        name="mlstm_prep",
    )(mqk, mqk, mqk, conv_w8, gates)


def _mlstm_kernel(qf_ref, kf_ref, vf_ref, gcf_ref, grf_ref, qb_ref, kb_ref, vb_ref, gcb_ref, grb_ref,
                  hf_ref, hb_ref, *scr):
    s = pl.program_id(0)
    nb = qf_ref.shape[0]
    n_g = 4 * ML_HEADS
    state_refs = scr[:len(scr) // 2]
    m_refs = scr[len(scr) // 2:]

    @pl.when(s == 0)
    def _():
        for ref in scr:
            ref[...] = jnp.zeros(ref.shape, F32)

    qi = lax.broadcasted_iota(jnp.int32, (ML_CHUNK, ML_CHUNK), 0)
    ki = lax.broadcasted_iota(jnp.int32, (ML_CHUNK, ML_CHUNK), 1)
    ones_col = (ki == 0).astype(BF16)

    for bi in range(nb):
        for direction in range(2):
            q_ref, k_ref, v_ref, gc_ref, gr_ref, h_ref = (
                (qf_ref, kf_ref, vf_ref, gcf_ref, grf_ref, hf_ref) if direction == 0 else
                (qb_ref, kb_ref, vb_ref, gcb_ref, grb_ref, hb_ref))
            visible = (ki <= qi) if direction == 0 else (ki >= qi)
            gc = gc_ref[bi]
            gr = gr_ref[bi, 0]
            for hd in range(ML_HEADS):
                chain = (bi * 2 + direction) * ML_HEADS + hd
                c_li = 2 * direction * ML_HEADS + hd
                c_lf = c_li + ML_HEADS
                c_cs = c_lf + (n_g if direction == 0 else 2 * n_g)
                lo, hi = hd * ML_DIM, (hd + 1) * ML_DIM
                q = q_ref[bi, :, lo:hi]
                k = k_ref[bi, :, lo:hi]
                v_aug = jnp.concatenate([v_ref[bi, :, lo:hi], ones_col], axis=1)
                li_col = gc[:, c_li:c_li + 1]
                cs_col = gc[:, c_cs:c_cs + 1]
                li_row = gr[c_li:c_li + 1, :]
                lf_row = gr[c_lf:c_lf + 1, :]
                cs_row = gr[c_cs:c_cs + 1, :]
                m_st = m_refs[chain][:, 0:1]
                st = state_refs[chain][...]

                b_last = jnp.sum(lf_row, axis=1, keepdims=True)
                logw = b_last - cs_col + li_col
                m_new = jnp.maximum(b_last + m_st, jnp.max(logw, axis=0, keepdims=True))
                wgt = jnp.exp(logw - m_new)
                decay = jnp.exp(b_last + m_st - m_new)

                log_d = jnp.where(visible, cs_col - cs_row + li_row, -jnp.inf)
                m_inter = cs_col + m_st
                m_t = jnp.maximum(m_inter, jnp.max(log_d, axis=1, keepdims=True))
                dmat = jnp.exp(log_d - m_t)
                inter = jnp.exp(m_inter - m_t)

                sc = (_dot_nt(q, k) * dmat).astype(BF16)
                tot = inter * _dot(q, st.astype(BF16)) + _dot(sc, v_aug)
                den = jnp.maximum(jnp.abs(tot[:, ML_DIM:ML_DIM + 1]), jnp.exp(-m_t))
                h_ref[bi, :, lo:hi] = tot[:, :ML_DIM] / den

                kw_t = (k.astype(F32) * wgt).T.astype(BF16)
                state_refs[chain][...] = decay * st + _dot(kw_t, v_aug)
                m_refs[chain][...] = jnp.broadcast_to(m_new, (1, LANES))


def _mlstm(mq, mk, mv, gc, gr, n_ctx):
    b, t, w = mq.shape
    nc = t // ML_CHUNK
    ncc = n_ctx // ML_CHUNK
    fwd = lambda s: s
    bwd = lambda s: jnp.where(s < ncc, ncc - 1 - s, nc - 1 - s + ncc)
    tok = lambda f: pl.BlockSpec((b, ML_CHUNK, w), lambda s: (0, f(s), 0))
    gcs = lambda f: pl.BlockSpec((b, ML_CHUNK, LANES), lambda s: (0, f(s), 0))
    grs = lambda f: pl.BlockSpec((b, 1, ML_CHUNK, LANES), lambda s: (0, f(s), 0, 0))
    n_chain = b * 2 * ML_HEADS
    return pl.pallas_call(
        _mlstm_kernel,
        grid=(nc,),
        in_specs=[tok(fwd), tok(fwd), tok(fwd), gcs(fwd), grs(fwd),
                  tok(bwd), tok(bwd), tok(bwd), gcs(bwd), grs(bwd)],
        out_specs=[tok(fwd), tok(bwd)],
        out_shape=[jax.ShapeDtypeStruct((b, t, w), F32)] * 2,
        scratch_shapes=[pltpu.VMEM((ML_DIM, 2 * ML_DIM), F32)] * n_chain
                       + [pltpu.VMEM((1, LANES), F32)] * n_chain,
        compiler_params=_params("arbitrary"),
        name="mlstm",
    )(mq, mk, mv, gc, gr, mq, mk, mv, gc, gr)


def _attn_kernel(q_ref, k_ref, v_ref, lam_ref, sg_ref, o_ref, qm_scr, *scr,
                 ctx_tiles, n_ctx, key_block, n_blocks, lam_init):
    s_scr = (scr[0:2], scr[2:4])
    bm_scr = (scr[4:6], scr[6:8])
    m_scr = scr[8:10]
    acc_scr = scr[10:12]
    i = pl.program_id(2)
    dv = DA_V_DIM
    q = q_ref[0]
    lane = lax.broadcasted_iota(jnp.int32, q.shape, 1)
    zero = jnp.zeros_like(q)
    qm_scr[0] = jnp.where(lane < DA_QK_DIM, q, zero)
    qm_scr[1] = jnp.where(lane >= DA_QK_DIM, q, zero)
    align = math.gcd(n_ctx, key_block)

    def scores(mp, start, size):
        return _dot_nt(qm_scr[mp], k_ref[0, pl.ds(start, size), :])

    def lane_max(sc):
        m = sc[:, 0:LANES]
        for t in range(1, sc.shape[1] // LANES):
            m = jnp.maximum(m, sc[:, t * LANES:(t + 1) * LANES])
        return m

    def row_max(bm):
        return jnp.broadcast_to(jnp.max(bm, axis=1, keepdims=True), bm.shape)

    def weights(sc, m):
        return jnp.concatenate([jnp.exp2(sc[:, t * LANES:(t + 1) * LANES] - m)
                                for t in range(sc.shape[1] // LANES)], axis=1).astype(BF16)

    for mp in range(2):
        sc = scores(mp, 0, n_ctx)
        m = row_max(lane_max(sc))
        m_scr[mp][...] = m
        acc_scr[mp][...] = _dot(weights(sc, m), v_ref[0, 0:n_ctx, :])

    def stage(j, slot):
        start = pl.multiple_of(n_ctx + j * key_block, align)
        for mp in range(2):
            sc = scores(mp, start, key_block)
            s_scr[mp][slot][...] = sc
            bm_scr[mp][slot][...] = lane_max(sc)

    def consume(j, slot):
        start = pl.multiple_of(n_ctx + j * key_block, align)
        for mp in range(2):
            m_old = m_scr[mp][...]
            m_new = jnp.maximum(m_old, row_max(bm_scr[mp][slot][...]))
            alpha = jnp.exp2(m_old - m_new)
            pv = _dot(weights(s_scr[mp][slot][...], m_new), v_ref[0, pl.ds(start, key_block), :])
            acc_scr[mp][...] = jnp.concatenate([alpha, alpha], axis=1) * acc_scr[mp][...] + pv
            m_scr[mp][...] = m_new

    @pl.when(i >= ctx_tiles)
    def _():
        stage(0, 0)

        def body(g, carry):
            stage(2 * g + 1, 1)
            consume(2 * g, 0)
            stage(2 * g + 2, 0)
            consume(2 * g + 1, 1)
            return carry
        lax.fori_loop(0, n_blocks // 2 - 1, body, 0)
        stage(n_blocks - 1, 1)
        consume(n_blocks - 2, 0)
        consume(n_blocks - 1, 1)

    lv = lam_ref[...]
    dot01 = jnp.sum(lv[0:1, :] * lv[1:2, :], axis=1, keepdims=True)
    dot23 = jnp.sum(lv[2:3, :] * lv[3:4, :], axis=1, keepdims=True)
    lam = jnp.exp(dot01) - jnp.exp(dot23) + lam_init
    a0 = acc_scr[0][...]
    a1 = acc_scr[1][...]
    o = a0[:, 0:dv] / a0[:, dv:dv + 1] - lam * (a1[:, 0:dv] / a1[:, dv:dv + 1])
    o = o * lax.rsqrt(jnp.mean(o * o, axis=-1, keepdims=True) + NORM_EPS) * sg_ref[...]
    o_ref[0] = (o * (1.0 - lam_init)).astype(BF16)


def _diff_attention(q, k, v_aug, lam8, subln_g, n_ctx, lam_init):
    b, t, w = q.shape
    tq = ROW_TILE
    n_lat = t - n_ctx
    key_block = 1024 if n_lat % 2048 == 0 else 512
    n_blocks = n_lat // key_block
    assert n_lat % (2 * key_block) == 0 and n_ctx % tq == 0 and n_ctx % LANES == 0
    kern = functools.partial(_attn_kernel, ctx_tiles=n_ctx // tq, n_ctx=n_ctx, key_block=key_block,
                             n_blocks=n_blocks, lam_init=lam_init)
    return pl.pallas_call(
        kern,
        grid=(b, DA_HEADS, t // tq),
        in_specs=[pl.BlockSpec((1, tq, LANES), lambda bi, h, i: (bi, i, h)),
                  pl.BlockSpec((1, t, LANES), lambda bi, h, i: (bi, 0, h)),
                  pl.BlockSpec((1, t, 2 * DA_V_DIM), lambda bi, h, i: (bi, 0, h)),
                  pl.BlockSpec(lam8.shape, lambda bi, h, i: (0, 0)),
                  pl.BlockSpec(subln_g.shape, lambda bi, h, i: (0, 0))],
        out_specs=pl.BlockSpec((1, tq, LANES), lambda bi, h, i: (bi, i, h)),
        out_shape=jax.ShapeDtypeStruct((b, t, w), BF16),
        scratch_shapes=[pltpu.VMEM((2, tq, LANES), BF16)]
                       + [pltpu.VMEM((tq, key_block), F32)] * 4
                       + [pltpu.VMEM((tq, LANES), F32)] * 6
                       + [pltpu.VMEM((tq, 2 * DA_V_DIM), F32)] * 2,
        compiler_params=_params("parallel", "parallel", "arbitrary"),
        name="diff_attention",
    )(q, k, v_aug, lam8, subln_g)


def _top2_sum(a, b, c, d):
    hi1, lo1 = jnp.maximum(a, b), jnp.minimum(a, b)
    hi2, lo2 = jnp.maximum(c, d), jnp.minimum(c, d)
    return jnp.maximum(hi1, hi2) + jnp.maximum(jnp.minimum(hi1, hi2), jnp.maximum(lo1, lo2))


def _route(f, rwt_ref, rb_ref, cnt_ref):
    tm = f.shape[0]
    aff = _sigmoid(_dot_nt(rwt_ref[...], f, HI))
    biased = aff + rb_ref[:, 0:1]
    bz = [biased[e:e + 1, :] for e in range(N_EXPERTS)]
    af = [aff[e:e + 1, :] for e in range(N_EXPERTS)]
    n_grp = N_EXPERTS // EXPERTS_PER_GROUP
    scores = [_top2_sum(*bz[EXPERTS_PER_GROUP * g:EXPERTS_PER_GROUP * (g + 1)]) for g in range(n_grp)]
    best = scores[0]
    sel_grp = jnp.zeros_like(best, dtype=jnp.int32)
    for g in range(1, n_grp):
        better = scores[g] > best
        sel_grp = jnp.where(better, g, sel_grp)
        best = jnp.where(better, scores[g], best)
    chosen = []
    for e in range(N_EXPERTS):
        g = e // EXPERTS_PER_GROUP
        rank = jnp.zeros_like(sel_grp)
        for o in range(EXPERTS_PER_GROUP * g, EXPERTS_PER_GROUP * (g + 1)):
            if o == e:
                continue
            beats = (bz[o] > bz[e]) if o > e else (bz[o] >= bz[e])
            rank = rank + beats.astype(jnp.int32)
        chosen.append(jnp.logical_and(sel_grp == g, rank < 2))
    denom = sum(jnp.where(chosen[e], af[e], 0.0) for e in range(N_EXPERTS))
    erow = lax.broadcasted_iota(jnp.int32, (N_EXPERTS, tm), 0)
    one_hot = jnp.zeros((N_EXPERTS, tm), F32)
    for e in range(N_EXPERTS):
        one_hot = jnp.where(jnp.logical_and(erow == e, chosen[e]), 1.0, one_hot)
    earlier = (lax.broadcasted_iota(jnp.int32, (tm, tm), 0)
               < lax.broadcasted_iota(jnp.int32, (tm, tm), 1)).astype(BF16)
    rank_all = _dot(one_hot.astype(BF16), earlier) + cnt_ref[:, 0:1]
    cnt_ref[...] = cnt_ref[...] + jnp.sum(one_hot, axis=1, keepdims=True)

    seen = jnp.zeros((1, tm), jnp.bool_)
    e_a = e_b = jnp.zeros((1, tm), jnp.int32)
    r_a = r_b = w_a = w_b = jnp.zeros((1, tm), F32)
    for e in range(N_EXPERTS):
        first = jnp.logical_and(chosen[e], jnp.logical_not(seen))
        second = jnp.logical_and(chosen[e], seen)
        rk = rank_all[e:e + 1, :]
        wt = af[e] / denom
        e_a, e_b = jnp.where(first, e, e_a), jnp.where(second, e, e_b)
        r_a, r_b = jnp.where(first, rk, r_a), jnp.where(second, rk, r_b)
        w_a, w_b = jnp.where(first, wt, w_a), jnp.where(second, wt, w_b)
        seen = jnp.logical_or(seen, chosen[e])
    r8 = lax.broadcasted_iota(jnp.int32, (8, tm), 0)
    sel = jnp.where(r8 == 0, e_a, jnp.where(r8 == 1, e_b, jnp.where(
        r8 == 2, r_a.astype(jnp.int32), jnp.where(r8 == 3, r_b.astype(jnp.int32), 0))))
    row = lax.broadcasted_iota(jnp.int32, (LANES, tm), 0)
    w_t = jnp.where(row == 0, w_a, jnp.where(row == 1, w_b, 0.0))
    return sel, w_t.T


def _post_kernel(*refs, even):
    if even:
        (x_ref, a_ref, hf_ref, hb_ref, og_ref, mod_ref, wa_ref, wm_ref, gf_ref, rwt_ref, rb_ref,
         xo_ref, f_ref, sel_ref, w_ref, cnt_ref, cnt_scr) = refs
        m = ((hf_ref[0] + hb_ref[0]) * og_ref[0].astype(F32)).astype(BF16)
        o = _dot(a_ref[0], wa_ref[...]) + _dot(m, wm_ref[...])
    else:
        (x_ref, a_ref, mod_ref, wa_ref, gf_ref, rwt_ref, rb_ref,
         xo_ref, f_ref, sel_ref, w_ref, cnt_ref, cnt_scr) = refs
        o = _dot(a_ref[0].astype(BF16), wa_ref[...])

    @pl.when(jnp.logical_and(pl.program_id(0) == 0, pl.program_id(1) == 0))
    def _():
        cnt_scr[...] = jnp.zeros(cnt_scr.shape, F32)

    x = x_ref[0] + mod_ref[0, 2:3, :] * o
    xo_ref[0] = x
    f = _norm_mod(x, gf_ref[...], mod_ref[0, 3:4, :], mod_ref[0, 4:5, :])
    _store_row_tiles(f_ref.at[0], f)
    sel, w_col = _route(f, rwt_ref, rb_ref, cnt_scr)
    sel_ref[0, 0] = sel
    w_ref[0] = w_col
    cnt_ref[...] = cnt_scr[...]


def _post_mixer(x, acts, weights, modtab, gffn, rwt, rb, n_ctx, x_row_off, even):
    b, t, _ = acts[0].shape
    d = x.shape[2]
    tm = ROW_TILE
    nt = t // tm
    ctx_tiles = n_ctx // tm
    off = x_row_off // tm
    full = lambda a: pl.BlockSpec(a.shape, lambda bi, i: (0,) * a.ndim)
    row = lambda w: pl.BlockSpec((1, tm, w), lambda bi, i: (bi, i, 0))
    mod_spec = pl.BlockSpec(
        (1, 8, d), lambda bi, i: (2 * bi + (i + off >= ctx_tiles).astype(jnp.int32), 0, 0))
    in_specs = ([pl.BlockSpec((1, tm, d), lambda bi, i: (bi, i + off, 0))]
                + [row(a.shape[2]) for a in acts] + [mod_spec]
                + [full(w) for w in weights] + [full(gffn), full(rwt), full(rb)])
    return pl.pallas_call(
        functools.partial(_post_kernel, even=even),
        grid=(b, nt),
        in_specs=in_specs,
        out_specs=[row(d), pl.BlockSpec((1, tm * d // LANES, LANES), lambda bi, i: (bi, i, 0)),
                   pl.BlockSpec((1, 1, 8, tm), lambda bi, i: (bi, i, 0, 0)),
                   row(LANES),
                   pl.BlockSpec((N_EXPERTS, LANES), lambda bi, i: (0, 0))],
        out_shape=[jax.ShapeDtypeStruct((b, t, d), F32),
                   jax.ShapeDtypeStruct((b, t * d // LANES, LANES), F32),
                   jax.ShapeDtypeStruct((b, nt, 8, tm), jnp.int32),
                   jax.ShapeDtypeStruct((b, t, LANES), F32),
                   jax.ShapeDtypeStruct((N_EXPERTS, LANES), F32)],
        scratch_shapes=[pltpu.VMEM((N_EXPERTS, LANES), F32)],
        compiler_params=_params("arbitrary", "arbitrary"),
        name="post_mixer_even" if even else "post_mixer_odd",
    )(x, *acts, modtab, *weights, gffn, rwt, rb)


EXPERT_ROW_TILE = 256
DMA_ISSUE_UNROLL = 8


def _moe_plan(sel, counts, n_tok):
    e_a, e_b, r_a, r_b = (sel[:, :, k, :].reshape(-1) for k in range(4))
    cnt = counts[:, 0].astype(jnp.int32)
    padded = ((cnt + EXPERT_ROW_TILE - 1) // EXPERT_ROW_TILE) * EXPERT_ROW_TILE
    ends = jnp.cumsum(padded)
    starts = ends - padded
    pos = jnp.concatenate([starts[e_a] + r_a, starts[e_b] + r_b]).astype(jnp.int32)
    n_tiles = 2 * n_tok // EXPERT_ROW_TILE + N_EXPERTS
    tile_start = jnp.arange(n_tiles, dtype=jnp.int32) * EXPERT_ROW_TILE
    tile_expert = jnp.minimum(jnp.sum(tile_start[:, None] >= ends[None, :], axis=1),
                              N_EXPERTS - 1).astype(jnp.int32)
    tiles_used = (ends[-1:] // EXPERT_ROW_TILE).astype(jnp.int32)
    return pos, tile_expert, tiles_used, n_tiles


def _store_row_tiles(ref, x):
    rows, d = x.shape
    n_sub = d // LANES
    for s in range(n_sub):
        ref[pl.ds(s, rows, stride=n_sub), :] = x[:, s * LANES:(s + 1) * LANES]


def _load_row_tiles(ref, rows):
    n_sub = ref.shape[0] // rows
    return jnp.concatenate([ref[pl.ds(s, rows, stride=n_sub), :] for s in range(n_sub)], axis=1)


def _dispatch_kernel(pos_ref, f_ref, init_ref, out_ref, sem, *, n_tok, tm):
    del init_ref
    n_sub = f_ref.shape[1] // tm
    base = (pl.program_id(0) * pl.num_programs(1) + pl.program_id(1)) * tm

    def row_copy(r, k):
        dst = pos_ref[k * n_tok + base + r]
        src = f_ref.at[0, pl.ds(pl.multiple_of(r * n_sub, n_sub), n_sub), :]
        return pltpu.make_async_copy(src, out_ref.at[dst], sem)

    def issue(r, carry):
        row_copy(r, 0).start()
        row_copy(r, 1).start(priority=1)
        return carry

    lax.fori_loop(0, tm, issue, 0, unroll=DMA_ISSUE_UNROLL)
    for _ in range(2):
        pltpu.make_async_copy(out_ref.at[pl.ds(0, tm)], out_ref.at[pl.ds(0, tm)], sem).wait()


def _dispatch(pos, f, n_rows, n_tok):
    b, rows, _ = f.shape
    n_sub = rows * b // n_tok
    tm = ROW_TILE
    grid_spec = pltpu.PrefetchScalarGridSpec(
        num_scalar_prefetch=1,
        grid=(b, n_tok // b // tm),
        in_specs=[pl.BlockSpec((1, tm * n_sub, LANES), lambda bi, i, pos_ref: (bi, i, 0)),
                  pl.BlockSpec(memory_space=pl.ANY)],
        out_specs=pl.BlockSpec(memory_space=pl.ANY),
        scratch_shapes=[pltpu.SemaphoreType.DMA(())])
    return pl.pallas_call(
        functools.partial(_dispatch_kernel, n_tok=n_tok, tm=tm),
        grid_spec=grid_spec,
        out_shape=jax.ShapeDtypeStruct((n_rows, n_sub, LANES), F32),
        input_output_aliases={2: 0},
        compiler_params=_params("arbitrary", "arbitrary"),
        name="moe_dispatch",
    )(pos, f, jnp.zeros((n_rows, n_sub, LANES), F32))


def _expert_ffn_kernel(te_ref, used_ref, x_ref, wg_ref, wu_ref, wd_ref, y_ref):
    del te_ref
    live = pl.program_id(0) < used_ref[0]

    @pl.when(live)
    def _():
        xb = _load_row_tiles(x_ref, EXPERT_ROW_TILE).astype(BF16)
        he = _silu(_dot(xb, wg_ref[0])) * _dot(xb, wu_ref[0])
        _store_row_tiles(y_ref, _dot(he.astype(BF16), wd_ref[0]))

    @pl.when(jnp.logical_not(live))
    def _():
        y_ref[...] = jnp.zeros(y_ref.shape, F32)


def _expert_ffn(tile_expert, tiles_used, xs, wg, wu, wd, n_tiles):
    n_rows, n_sub, _ = xs.shape
    _, d, d_e = wg.shape
    tr = EXPERT_ROW_TILE
    grid_spec = pltpu.PrefetchScalarGridSpec(
        num_scalar_prefetch=2,
        grid=(n_tiles,),
        in_specs=[pl.BlockSpec((tr * n_sub, LANES), lambda j, te, used: (j, 0)),
                  pl.BlockSpec((1, d, d_e), lambda j, te, used: (te[j], 0, 0)),
                  pl.BlockSpec((1, d, d_e), lambda j, te, used: (te[j], 0, 0)),
                  pl.BlockSpec((1, d_e, d), lambda j, te, used: (te[j], 0, 0))],
        out_specs=pl.BlockSpec((tr * n_sub, LANES), lambda j, te, used: (j, 0)))
    ys = pl.pallas_call(
        _expert_ffn_kernel,
        grid_spec=grid_spec,
        out_shape=jax.ShapeDtypeStruct((n_rows * n_sub, LANES), F32),
        compiler_params=_params("arbitrary"),
        name="moe_expert_ffn",
    )(tile_expert, tiles_used, xs.reshape(n_rows * n_sub, LANES), wg, wu, wd)
    return ys.reshape(n_rows, n_sub, LANES)


def _combine_kernel(pos_ref, y_ref, w_ref, x_ref, modc_ref, modl_ref, fg_ref, o_ref, buf, sem,
                    *, n_tok, n_ctx, final_norm):
    i = pl.program_id(1)
    tm = x_ref.shape[1]
    base = (pl.program_id(0) * pl.num_programs(1) + i) * tm

    n_sub = y_ref.shape[1]

    def row_copy(r, k):
        src = pos_ref[k * n_tok + base + r]
        dst = buf.at[k, pl.ds(pl.multiple_of(r * n_sub, n_sub), n_sub), :]
        return pltpu.make_async_copy(y_ref.at[src], dst, sem)

    def issue(r, carry):
        row_copy(r, 0).start()
        row_copy(r, 1).start(priority=1)
        return carry

    lax.fori_loop(0, tm, issue, 0, unroll=DMA_ISSUE_UNROLL)
    for _ in range(2):
        pltpu.make_async_copy(y_ref.at[pl.ds(0, tm)], y_ref.at[pl.ds(0, tm)], sem).wait()
    w = w_ref[0]
    y = _load_row_tiles(buf.at[0], tm) * w[:, 0:1] + _load_row_tiles(buf.at[1], tm) * w[:, 1:2]
    rows = i * tm + lax.broadcasted_iota(jnp.int32, (tm, 1), 0)
    gate = jnp.where(rows < n_ctx, modc_ref[0, 5:6, :], modl_ref[0, 5:6, :])
    out = x_ref[0] + gate * y
    if final_norm:
        out = out * lax.rsqrt(jnp.mean(out * out, axis=-1, keepdims=True) + NORM_EPS) * fg_ref[...]
    o_ref[0] = out


def _combine(pos, ys, w_col, xmid, modtab, final_g, n_ctx, final_norm):
    b, t, d = xmid.shape
    tm = ROW_TILE
    row = lambda w: pl.BlockSpec((1, tm, w), lambda bi, i, pos_ref: (bi, i, 0))
    grid_spec = pltpu.PrefetchScalarGridSpec(
        num_scalar_prefetch=1,
        grid=(b, t // tm),
        in_specs=[pl.BlockSpec(memory_space=pl.ANY), row(LANES), row(d),
                  pl.BlockSpec((1, 8, d), lambda bi, i, pos_ref: (2 * bi, 0, 0)),
                  pl.BlockSpec((1, 8, d), lambda bi, i, pos_ref: (2 * bi + 1, 0, 0)),
                  pl.BlockSpec(final_g.shape, lambda bi, i, pos_ref: (0, 0))],
        out_specs=row(d),
        scratch_shapes=[pltpu.VMEM((2, tm * d // LANES, LANES), F32), pltpu.SemaphoreType.DMA(())])
    return pl.pallas_call(
        functools.partial(_combine_kernel, n_tok=b * t, n_ctx=n_ctx, final_norm=final_norm),
        grid_spec=grid_spec,
        out_shape=jax.ShapeDtypeStruct((b, t, d), F32),
        compiler_params=_params("arbitrary", "arbitrary"),
        name="moe_combine",
    )(pos, ys, w_col, xmid, modtab, modtab, final_g)


def _moe(f, sel, w_col, counts, wg, wu, wd, xmid, modtab, final_g, n_ctx, final_norm):
    b, t, _ = xmid.shape
    pos, tile_expert, tiles_used, n_tiles = _moe_plan(sel, counts, b * t)
    xs = _dispatch(pos, f, n_tiles * EXPERT_ROW_TILE, b * t)
    ys = _expert_ffn(tile_expert, tiles_used, xs, wg, wu, wd, n_tiles)
    return _combine(pos, ys, w_col, xmid, modtab, final_g, n_ctx, final_norm)


def _chan_dft_kernel(x_ref, mod_ref, g_ref, w_ref, zr_ref, zi_ref):
    h = _norm_mod(x_ref[0], g_ref[...], mod_ref[0, 0:1, :], mod_ref[0, 1:2, :])
    gd = w_ref.shape[0]
    for gi in range(h.shape[1] // gd):
        z = _dot_split(h[:, gi * gd:(gi + 1) * gd], w_ref[...])
        zr_ref[0, :, gi * gd:(gi + 1) * gd] = z[:, :gd]
        zi_ref[0, :, gi * gd:(gi + 1) * gd] = z[:, gd:]


def _chan_dft(x, modtab, g, w_cs, x_row_off, t):
    b, _, d = x.shape
    tm = ROW_TILE
    off = x_row_off // tm
    row = pl.BlockSpec((1, tm, d), lambda bi, i: (bi, i, 0))
    return pl.pallas_call(
        _chan_dft_kernel,
        grid=(b, t // tm),
        in_specs=[pl.BlockSpec((1, tm, d), lambda bi, i: (bi, i + off, 0)),
                  pl.BlockSpec((1, 8, d), lambda bi, i: (2 * bi + 1, 0, 0)),
                  pl.BlockSpec(g.shape, lambda bi, i: (0, 0)),
                  pl.BlockSpec(w_cs.shape, lambda bi, i: (0, 0))],
        out_specs=[row, row],
        out_shape=[jax.ShapeDtypeStruct((b, t, d), F32)] * 2,
        compiler_params=_params("parallel", "parallel"),
        name="chan_dft",
    )(x, modtab, g, w_cs)


def _dft1_kernel(zr_ref, zi_ref, w_ref, yr_ref, yi_ref):
    n1 = zr_ref.shape[1]
    y = _dot_split(w_ref[...], jnp.concatenate([zr_ref[0], zi_ref[0]], axis=0))
    yr_ref[0] = y[:n1]
    yi_ref[0] = y[n1:]


def _dft1(zr, zi, w1):
    b, n1, cols = zr.shape
    tn = min(cols, 4096)
    blk = pl.BlockSpec((1, n1, tn), lambda bi, j: (bi, 0, j))
    return pl.pallas_call(
        _dft1_kernel,
        grid=(b, cols // tn),
        in_specs=[blk, blk, pl.BlockSpec(w1.shape, lambda bi, j: (0, 0))],
        out_specs=[blk, blk],
        out_shape=[jax.ShapeDtypeStruct(zr.shape, F32)] * 2,
        compiler_params=_params("parallel", "parallel"),
        name="dft_stage1",
    )(zr, zi, w1)


def _dft2_kernel(yr_ref, yi_ref, tab_ref, o_ref):
    y = jnp.concatenate([yr_ref[0, 0], yi_ref[0, 0]], axis=0)
    o_ref[0, 0] = _dot_split(tab_ref[0], y)


def _dft2(yr, yi, tab):
    b, n1, n2, d = yr.shape
    blk = pl.BlockSpec((1, 1, n2, d), lambda bi, k1: (bi, k1, 0, 0))
    return pl.pallas_call(
        _dft2_kernel,
        grid=(b, n1),
        in_specs=[blk, blk, pl.BlockSpec((1, n2, 2 * n2), lambda bi, k1: (k1, 0, 0))],
        out_specs=blk,
        out_shape=jax.ShapeDtypeStruct(yr.shape, F32),
        compiler_params=_params("parallel", "parallel"),
        name="dft_stage2",
    )(yr, yi, tab)


def _dft_tables(t, gd):
    n2 = ML_CHUNK
    n1 = t // n2
    def cs(num, den):
        ang = (2.0 * np.pi / den) * (num % den).astype(np.float64)
        return np.cos(ang), np.sin(ang)
    c = np.arange(gd)
    cc, sc = cs(np.outer(c, c), gd)
    w_cs = np.concatenate([cc, -sc], axis=1)
    a = np.arange(n1)
    c1, s1 = cs(np.outer(a, a), n1)
    w1 = np.block([[c1, s1], [-s1, c1]])
    k = a[:, None, None] + n1 * np.arange(n2)[None, :, None]
    c2, s2 = cs(k * np.arange(n2)[None, None, :], t)
    tab = np.concatenate([c2, s2], axis=2) / math.sqrt(t * gd)
    return (jnp.asarray(w_cs, F32), jnp.asarray(w1, F32), jnp.asarray(tab, F32))


def _rope_tables(n_ctx, n_lat):
    pos = jnp.arange(n_lat, dtype=jnp.int32)
    n_axis = DA_QK_DIM // 4
    inv = ROPE_BASE ** (-jnp.arange(n_axis, dtype=F32) / n_axis)
    ang = jnp.concatenate([(pos // GRID_W).astype(F32)[:, None] * inv,
                           (pos % GRID_W).astype(F32)[:, None] * inv], axis=-1)
    cos, sin = jnp.cos(ang), jnp.sin(ang)
    cos = jnp.concatenate([jnp.ones((n_ctx, 2 * n_axis), F32), cos], axis=0)
    sin = jnp.concatenate([jnp.zeros((n_ctx, 2 * n_axis), F32), sin], axis=0)
    cos128 = jnp.concatenate([cos, cos, cos, cos], axis=1)
    sin128 = jnp.concatenate([-sin, sin, -sin, sin], axis=1)
    return cos128, sin128


def _deinterleave(w):
    d, n = w.shape
    w = w.reshape(d, n // DA_QK_DIM, DA_QK_DIM // 2, 2)
    return jnp.concatenate([w[..., 0], w[..., 1]], axis=-1).reshape(d, n)


def _pad_rows(a, rows):
    return jnp.concatenate([a, jnp.zeros((rows - a.shape[0],) + a.shape[1:], a.dtype)], axis=0)


def _pad_cols(a, cols):
    return jnp.concatenate([a, jnp.zeros(a.shape[:-1] + (cols - a.shape[-1],), a.dtype)], axis=-1)


def kernel(x, c, ctx, c_ctx, ada_w, ada_b, norm_mix_g, norm_ffn_g, even_w_in, even_w_out,
           even_conv_w, even_gate_b, even_lam, even_subln_g, odd_w_fnet, router_w, router_b,
           exp_w_gate, exp_w_up, exp_w_down, final_g):
    b, n_lat, d = x.shape
    n_ctx = ctx.shape[1]
    depth = ada_w.shape[0]
    assert depth == 2 and b + 1 <= 8
    assert n_ctx % ROW_TILE == 0 and n_lat % ROW_TILE == 0

    cond8 = _pad_rows(jnp.concatenate([c_ctx[None, :], c], axis=0), 8)
    rwt = router_w.T
    rb = jnp.broadcast_to(router_b[:, None], (N_EXPERTS, LANES))
    row2 = lambda v: v.reshape(1, -1)

    def modtab_for(layer):
        mods = _ada_mods(cond8, ada_w[layer], ada_b[layer]).reshape(8, 6, d)
        mods = jnp.concatenate([mods, jnp.zeros((8, 2, d), F32)], axis=1)
        idx = np.array([[0, 1 + bi] for bi in range(b)]).reshape(-1)
        return mods[idx]

    xs = jnp.concatenate([ctx, x], axis=1)
    modtab = modtab_for(0)
    w_in = even_w_in[0]
    o1 = DA_HEADS * 2 * DA_QK_DIM
    o2 = 2 * o1
    o3 = o2 + DA_HEADS * DA_V_DIM
    o4 = o3 + 2 * ML_HEADS * ML_DIM
    o5 = o4 + ML_HEADS * ML_DIM
    o6 = o5 + ML_HEADS * ML_DIM
    ws = [(_deinterleave(w_in[:, :o1]) * (DA_QK_DIM ** -0.5)).astype(BF16),
          _deinterleave(w_in[:, o1:o2]).astype(BF16),
          w_in[:, o2:o3].astype(BF16), w_in[:, o3:o4].astype(BF16),
          w_in[:, o4:o5].astype(BF16), w_in[:, o5:o6].astype(BF16),
          _pad_cols(w_in[:, o6:], LANES).astype(BF16)]
    gate_b = _pad_cols(even_gate_b[0].reshape(1, -1), LANES)
    cos128, sin128 = _rope_tables(n_ctx, n_lat)
    daq, dak, dav, mqk, mv, og, gates = _inproj(xs, modtab, row2(norm_mix_g[0]), cos128, sin128,
                                                 ws, gate_b, n_ctx)
    mq, mk, gc, gr = _mlprep(mqk, _pad_rows(even_conv_w[0], 8), gates, n_ctx)
    hf, hb = _mlstm(mq, mk, mv, gc, gr, n_ctx)
    lam_init = 0.8 - 0.6 * math.exp(-0.3 * 0)
    lam8 = _pad_rows(even_lam[0], 8)
    att = _diff_attention(daq, dak, dav, lam8, row2(even_subln_g[0]), n_ctx, lam_init)
    w_out = even_w_out[0].astype(BF16)
    half = DA_HEADS * DA_V_DIM
    xmid, f, sel, w_col, counts = _post_mixer(xs, [att, hf, hb, og], [w_out[:half], w_out[half:]],
                                              modtab, row2(norm_ffn_g[0]), rwt, rb, n_ctx, 0, True)
    xs = _moe(f, sel, w_col, counts, exp_w_gate[0].astype(BF16), exp_w_up[0].astype(BF16),
              exp_w_down[0].astype(BF16), xmid, modtab, row2(final_g), n_ctx, False)

    modtab = modtab_for(1)
    gd = d // FN_GROUPS
    w_cs, w1, tab = _dft_tables(n_lat, gd)
    n2 = ML_CHUNK
    n1 = n_lat // n2
    zr, zi = _chan_dft(xs, modtab, row2(norm_mix_g[1]), w_cs, n_ctx, n_lat)
    yr, yi = _dft1(zr.reshape(b, n1, n2 * d), zi.reshape(b, n1, n2 * d), w1)
    fo = _dft2(yr.reshape(b, n1, n2, d), yi.reshape(b, n1, n2, d), tab)
    fo = fo.transpose(0, 2, 1, 3).reshape(b, n_lat, d)
    xmid, f, sel, w_col, counts = _post_mixer(xs, [fo], [odd_w_fnet[0].astype(BF16)], modtab,
                                              row2(norm_ffn_g[1]), rwt, rb, 0, n_ctx, False)
    return _moe(f, sel, w_col, counts, exp_w_gate[1].astype(BF16), exp_w_up[1].astype(BF16),
                exp_w_down[1].astype(BF16), xmid, modtab, row2(final_g), 0, True)
```

```python
import functools
import math

import jax
import jax.numpy as jnp
import numpy as np
from jax import lax
from jax.experimental import pallas as pl
from jax.experimental.pallas import tpu as pltpu

F32 = jnp.float32
BF16 = jnp.bfloat16

NORM_EPS = 1e-6
GRID_W = 64
DA_HEADS = 4
DA_QK_DIM = 64
DA_V_DIM = 128
ML_HEADS = 4
ML_DIM = 128
ML_CHUNK = 128
FN_GROUPS = 4
N_EXPERTS = 16
EXPERTS_PER_GROUP = 4
ROPE_BASE = 10000.0
LANES = 128
ROW_TILE = 256
VMEM_LIMIT_BYTES = 56 * 1024 * 1024
HI = lax.Precision.HIGHEST
LOG2_E = math.log2(math.e)


def _params(*sem):
    return pltpu.CompilerParams(dimension_semantics=sem, vmem_limit_bytes=VMEM_LIMIT_BYTES)


def _dot(a, b, precision=None):
    return jnp.dot(a, b, preferred_element_type=F32, precision=precision)


def _dot_nt(a, b, precision=None):
    return lax.dot_general(a, b, (((1,), (1,)), ((), ())), preferred_element_type=F32,
                           precision=precision)


def _dot_split(a, b):
    a_hi = a.astype(BF16)
    b_hi = b.astype(BF16)
    a_lo = (a - a_hi.astype(F32)).astype(BF16)
    b_lo = (b - b_hi.astype(F32)).astype(BF16)
    return _dot(a_hi, b_hi) + (_dot(a_hi, b_lo) + _dot(a_lo, b_hi))


def _sigmoid(x):
    return 1.0 / (1.0 + jnp.exp(-x))


def _silu(x):
    return x * _sigmoid(x)


def _norm_mod(x, g, shift, scale):
    y = x * lax.rsqrt(jnp.mean(x * x, axis=-1, keepdims=True) + NORM_EPS) * g
    return y * (1.0 + scale) + shift


def _ada_kernel(c_ref, w_ref, b_ref, o_ref):
    o_ref[...] = _dot(_silu(c_ref[...]), w_ref[...], HI) + b_ref[...]


def _ada_mods(cond8, w, b):
    d, n = w.shape
    tn = n // 6
    return pl.pallas_call(
        _ada_kernel,
        grid=(6,),
        in_specs=[pl.BlockSpec((8, d), lambda j: (0, 0)),
                  pl.BlockSpec((d, tn), lambda j: (0, j)),
                  pl.BlockSpec((1, tn), lambda j: (0, j))],
        out_specs=pl.BlockSpec((8, tn), lambda j: (0, j)),
        out_shape=jax.ShapeDtypeStruct((8, n), F32),
        compiler_params=_params("arbitrary"),
        name="ada_mods",
    )(cond8, w, b.reshape(1, n))


def _inproj_kernel(x_ref, mod_ref, g_ref, cos_ref, sin_ref, wq_ref, wk_ref, wv_ref, wmqk_ref,
                   wmv_ref, wmo_ref, wg_ref, gb_ref,
                   q_ref, k_ref, v_ref, mqk_ref, mv_ref, og_ref, gate_ref):
    x = x_ref[0]
    h = _norm_mod(x, g_ref[...], mod_ref[0, 0:1, :], mod_ref[0, 1:2, :]).astype(BF16)
    tm = x.shape[0]
    width = q_ref.shape[2]
    cos = jnp.concatenate([cos_ref[...]] * (width // LANES), axis=1)
    sin = jnp.concatenate([sin_ref[...]] * (width // LANES), axis=1)
    lane = lax.broadcasted_iota(jnp.int32, (tm, width), 1)
    lower = (lane & (DA_QK_DIM - 1)) < (DA_QK_DIM // 2)

    def rope(u):
        swapped = jnp.where(lower, pltpu.roll(u, width - DA_QK_DIM // 2, 1),
                            pltpu.roll(u, DA_QK_DIM // 2, 1))
        return u * cos + swapped * sin

    q_ref[0] = (rope(_dot(h, wq_ref[...])) * LOG2_E).astype(BF16)
    k_ref[0] = rope(_dot(h, wk_ref[...])).astype(BF16)
    v = _dot(h, wv_ref[...]).astype(BF16)
    ones_col = (lax.broadcasted_iota(jnp.int32, (tm, DA_V_DIM), 1) == 0).astype(BF16)
    v_ref[0] = jnp.concatenate(
        [blk for hd in range(DA_HEADS) for blk in (v[:, hd * DA_V_DIM:(hd + 1) * DA_V_DIM], ones_col)],
        axis=1)
    mqk_ref[0] = _dot(h, wmqk_ref[...]).astype(BF16)
    mv_ref[0] = _dot(h, wmv_ref[...]).astype(BF16)
    og_ref[0] = _sigmoid(_dot(h, wmo_ref[...])).astype(BF16)
    g = _dot(h, wg_ref[...]) + gb_ref[...]
    glane = lax.broadcasted_iota(jnp.int32, g.shape, 1)
    is_forget = ((glane // ML_HEADS) & 1) == 1
    log_sig = jnp.minimum(g, 0.0) - jnp.log(1.0 + jnp.exp(-jnp.abs(g)))
    gate_ref[0] = jnp.where(is_forget, log_sig, g)


def _inproj(xs, modtab, g, cos, sin, ws, gate_b, n_ctx):
    b, t, d = xs.shape
    tm = ROW_TILE
    nt = t // tm
    ctx_tiles = n_ctx // tm
    row = lambda w: pl.BlockSpec((1, tm, w), lambda bi, i: (bi, i, 0))
    full = lambda a: pl.BlockSpec(a.shape, lambda bi, i: (0,) * a.ndim)
    widths = [w.shape[1] for w in ws]
    widths[2] *= 2
    out_dtypes = [BF16] * 6 + [F32]
    return pl.pallas_call(
        _inproj_kernel,
        grid=(b, nt),
        in_specs=[row(d),
                  pl.BlockSpec((1, 8, d), lambda bi, i: (2 * bi + (i >= ctx_tiles).astype(jnp.int32), 0, 0)),
                  full(g),
                  pl.BlockSpec((tm, LANES), lambda bi, i: (i, 0)),
                  pl.BlockSpec((tm, LANES), lambda bi, i: (i, 0))]
                 + [full(w) for w in ws] + [full(gate_b)],
        out_specs=[row(w) for w in widths],
        out_shape=[jax.ShapeDtypeStruct((b, t, w), dt) for w, dt in zip(widths, out_dtypes)],
        compiler_params=_params("parallel", "parallel"),
        name="inproj",
    )(xs, modtab, g, cos, sin, *ws, gate_b)


def _split3(x):
    x1 = x.astype(BF16)
    r1 = x - x1.astype(F32)
    x2 = r1.astype(BF16)
    x3 = (r1 - x2.astype(F32)).astype(BF16)
    return x1, x2, x3


def _mlprep_kernel(cur_ref, prev_ref, next_ref, cw_ref, gate_ref,
                   mq_ref, mk_ref, gc_ref, gr_ref, *, ctx_tiles, n_tiles):
    i = pl.program_id(1)
    cur = cur_ref[0].astype(F32)
    tm, w = cur.shape
    prev_ok = i != ctx_tiles
    if ctx_tiles > 0:
        prev_ok = jnp.logical_and(prev_ok, i != 0)
        next_ok = jnp.logical_and(i != ctx_tiles - 1, i != n_tiles - 1)
    else:
        next_ok = i != n_tiles - 1
    prev_row = jnp.where(prev_ok, prev_ref[0, 7:8, :].astype(F32), 0.0)
    next_row = jnp.where(next_ok, next_ref[0, 0:1, :].astype(F32), 0.0)
    ridx = lax.broadcasted_iota(jnp.int32, (tm, w), 0)
    before = jnp.where(ridx == 0, prev_row, pltpu.roll(cur, 1, 0))
    after = jnp.where(ridx == tm - 1, next_row, pltpu.roll(cur, tm - 1, 0))
    y = _silu(before * cw_ref[0:1, :] + cur * cw_ref[1:2, :] + after * cw_ref[2:3, :])
    half = w // 2
    mq_ref[0] = y[:, :half].astype(BF16)
    mk_ref[0] = (y[:, half:] * (ML_DIM ** -0.5)).astype(BF16)

    n_g = 4 * ML_HEADS
    r = lax.broadcasted_iota(jnp.int32, (ML_CHUNK, ML_CHUNK), 0)
    c = lax.broadcasted_iota(jnp.int32, (ML_CHUNK, ML_CHUNK), 1)
    lower = (c <= r).astype(BF16)
    upper = (c >= r).astype(BF16)
    for ci in range(tm // ML_CHUNK):
        gch = gate_ref[0, ci * ML_CHUNK:(ci + 1) * ML_CHUNK, :]
        gm = jnp.where(c < n_g, gch, 0.0)
        pre = sum(_dot(lower, p) for p in _split3(pltpu.roll(gm, n_g, 1)))
        suf = sum(_dot(upper, p) for p in _split3(pltpu.roll(gm, 2 * n_g, 1)))
        col = gm + pre + suf
        gc_ref[0, ci * ML_CHUNK:(ci + 1) * ML_CHUNK, :] = col
        gr_ref[0, ci] = col.T


def _mlprep(mqk, conv_w8, gates, n_ctx):
    b, t, w = mqk.shape
    tm = ROW_TILE
    nt = t // tm
    sub = tm // 8
    nsub = t // 8
    cpt = tm // ML_CHUNK
    kern = functools.partial(_mlprep_kernel, ctx_tiles=n_ctx // tm, n_tiles=nt)
    return pl.pallas_call(
        kern,
        grid=(b, nt),
        in_specs=[pl.BlockSpec((1, tm, w), lambda bi, i: (bi, i, 0)),
                  pl.BlockSpec((1, 8, w), lambda bi, i: (bi, jnp.maximum(i * sub - 1, 0), 0)),
                  pl.BlockSpec((1, 8, w), lambda bi, i: (bi, jnp.minimum((i + 1) * sub, nsub - 1), 0)),
                  pl.BlockSpec(conv_w8.shape, lambda bi, i: (0, 0)),
                  pl.BlockSpec((1, tm, LANES), lambda bi, i: (bi, i, 0))],
        out_specs=[pl.BlockSpec((1, tm, w // 2), lambda bi, i: (bi, i, 0)),
                   pl.BlockSpec((1, tm, w // 2), lambda bi, i: (bi, i, 0)),
                   pl.BlockSpec((1, tm, LANES), lambda bi, i: (bi, i, 0)),
                   pl.BlockSpec((1, cpt, ML_CHUNK, LANES), lambda bi, i: (bi, i, 0, 0))],
        out_shape=[jax.ShapeDtypeStruct((b, t, w // 2), BF16),
                   jax.ShapeDtypeStruct((b, t, w // 2), BF16),
                   jax.ShapeDtypeStruct((b, t, LANES), F32),
                   jax.ShapeDtypeStruct((b, t // ML_CHUNK, ML_CHUNK, LANES), F32)],
        compiler_params=_params("parallel", "parallel"),
        name="mlstm_prep",
    )(mqk, mqk, mqk, conv_w8, gates)


def _mlstm_kernel(qf_ref, kf_ref, vf_ref, gcf_ref, grf_ref, qb_ref, kb_ref, vb_ref, gcb_ref, grb_ref,
                  hf_ref, hb_ref, *scr):
    s = pl.program_id(0)
    nb = qf_ref.shape[0]
    n_g = 4 * ML_HEADS
    state_refs = scr[:len(scr) // 2]
    m_refs = scr[len(scr) // 2:]

    @pl.when(s == 0)
    def _():
        for ref in scr:
            ref[...] = jnp.zeros(ref.shape, F32)

    qi = lax.broadcasted_iota(jnp.int32, (ML_CHUNK, ML_CHUNK), 0)
    ki = lax.broadcasted_iota(jnp.int32, (ML_CHUNK, ML_CHUNK), 1)
    ones_col = (ki == 0).astype(BF16)

    for bi in range(nb):
        for direction in range(2):
            q_ref, k_ref, v_ref, gc_ref, gr_ref, h_ref = (
                (qf_ref, kf_ref, vf_ref, gcf_ref, grf_ref, hf_ref) if direction == 0 else
                (qb_ref, kb_ref, vb_ref, gcb_ref, grb_ref, hb_ref))
            visible = (ki <= qi) if direction == 0 else (ki >= qi)
            gc = gc_ref[bi]
            gr = gr_ref[bi, 0]
            for hd in range(ML_HEADS):
                chain = (bi * 2 + direction) * ML_HEADS + hd
                c_li = 2 * direction * ML_HEADS + hd
                c_lf = c_li + ML_HEADS
                c_cs = c_lf + (n_g if direction == 0 else 2 * n_g)
                lo, hi = hd * ML_DIM, (hd + 1) * ML_DIM
                q = q_ref[bi, :, lo:hi]
                k = k_ref[bi, :, lo:hi]
                v_aug = jnp.concatenate([v_ref[bi, :, lo:hi], ones_col], axis=1)
                li_col = gc[:, c_li:c_li + 1]
                cs_col = gc[:, c_cs:c_cs + 1]
                li_row = gr[c_li:c_li + 1, :]
                lf_row = gr[c_lf:c_lf + 1, :]
                cs_row = gr[c_cs:c_cs + 1, :]
                m_st = m_refs[chain][:, 0:1]
                st = state_refs[chain][...]

                b_last = jnp.sum(lf_row, axis=1, keepdims=True)
                logw = b_last - cs_col + li_col
                m_new = jnp.maximum(b_last + m_st, jnp.max(logw, axis=0, keepdims=True))
                wgt = jnp.exp(logw - m_new)
                decay = jnp.exp(b_last + m_st - m_new)

                log_d = jnp.where(visible, cs_col - cs_row + li_row, -jnp.inf)
                m_inter = cs_col + m_st
                m_t = jnp.maximum(m_inter, jnp.max(log_d, axis=1, keepdims=True))
                dmat = jnp.exp(log_d - m_t)
                inter = jnp.exp(m_inter - m_t)

                sc = (_dot_nt(q, k) * dmat).astype(BF16)
                tot = inter * _dot(q, st.astype(BF16)) + _dot(sc, v_aug)
                den = jnp.maximum(jnp.abs(tot[:, ML_DIM:ML_DIM + 1]), jnp.exp(-m_t))
                h_ref[bi, :, lo:hi] = tot[:, :ML_DIM] / den

                kw_t = (k.astype(F32) * wgt).T.astype(BF16)
                state_refs[chain][...] = decay * st + _dot(kw_t, v_aug)
                m_refs[chain][...] = jnp.broadcast_to(m_new, (1, LANES))


def _mlstm(mq, mk, mv, gc, gr, n_ctx):
    b, t, w = mq.shape
    nc = t // ML_CHUNK
    ncc = n_ctx // ML_CHUNK
    fwd = lambda s: s
    bwd = lambda s: jnp.where(s < ncc, ncc - 1 - s, nc - 1 - s + ncc)
    tok = lambda f: pl.BlockSpec((b, ML_CHUNK, w), lambda s: (0, f(s), 0))
    gcs = lambda f: pl.BlockSpec((b, ML_CHUNK, LANES), lambda s: (0, f(s), 0))
    grs = lambda f: pl.BlockSpec((b, 1, ML_CHUNK, LANES), lambda s: (0, f(s), 0, 0))
    n_chain = b * 2 * ML_HEADS
    return pl.pallas_call(
        _mlstm_kernel,
        grid=(nc,),
        in_specs=[tok(fwd), tok(fwd), tok(fwd), gcs(fwd), grs(fwd),
                  tok(bwd), tok(bwd), tok(bwd), gcs(bwd), grs(bwd)],
        out_specs=[tok(fwd), tok(bwd)],
        out_shape=[jax.ShapeDtypeStruct((b, t, w), F32)] * 2,
        scratch_shapes=[pltpu.VMEM((ML_DIM, 2 * ML_DIM), F32)] * n_chain
                       + [pltpu.VMEM((1, LANES), F32)] * n_chain,
        compiler_params=_params("arbitrary"),
        name="mlstm",
    )(mq, mk, mv, gc, gr, mq, mk, mv, gc, gr)


def _attn_kernel(q_ref, qn_ref, k_ref, v_ref, lam_ref, sg_ref, o_ref, qm_scr, *scr,
                 ctx_tiles, n_ctx, key_block, n_blocks, lam_init):
    s_scr = (scr[0:2], scr[2:4])
    bm_scr = (scr[4:6], scr[6:8])
    m_scr = scr[8:10]
    acc_scr = scr[10:12]
    i = pl.program_id(2)
    dv = DA_V_DIM
    q = q_ref[0]
    lane = lax.broadcasted_iota(jnp.int32, q.shape, 1)
    zero = jnp.zeros_like(q)

    def map_halves(qv):
        return jnp.where(lane < DA_QK_DIM, qv, zero), jnp.where(lane >= DA_QK_DIM, qv, zero)

    qm_scr[0], qm_scr[1] = map_halves(q)
    align = math.gcd(n_ctx, key_block)

    def scores(mp, start, size, q_maps=None):
        q_map = qm_scr[mp] if q_maps is None else q_maps[mp]
        return _dot_nt(q_map, k_ref[0, pl.ds(start, size), :])

    def lane_max(sc):
        m = sc[:, 0:LANES]
        for t in range(1, sc.shape[1] // LANES):
            m = jnp.maximum(m, sc[:, t * LANES:(t + 1) * LANES])
        return m

    def row_max(bm):
        return jnp.broadcast_to(jnp.max(bm, axis=1, keepdims=True), bm.shape)

    def weights(sc, m):
        return jnp.concatenate([jnp.exp2(sc[:, t * LANES:(t + 1) * LANES] - m)
                                for t in range(sc.shape[1] // LANES)], axis=1).astype(BF16)

    for mp in range(2):
        sc = scores(mp, 0, n_ctx)
        m = row_max(lane_max(sc))
        m_scr[mp][...] = m
        acc_scr[mp][...] = _dot(weights(sc, m), v_ref[0, 0:n_ctx, :])

    def stage(j, slot, q_maps=None):
        start = pl.multiple_of(n_ctx + j * key_block, align)
        for mp in range(2):
            sc = scores(mp, start, key_block, q_maps)
            s_scr[mp][slot][...] = sc
            bm_scr[mp][slot][...] = lane_max(sc)

    def consume(j, slot):
        start = pl.multiple_of(n_ctx + j * key_block, align)
        for mp in range(2):
            m_old = m_scr[mp][...]
            m_new = jnp.maximum(m_old, row_max(bm_scr[mp][slot][...]))
            alpha = jnp.exp2(m_old - m_new)
            pv = _dot(weights(s_scr[mp][slot][...], m_new), v_ref[0, pl.ds(start, key_block), :])
            acc_scr[mp][...] = jnp.concatenate([alpha, alpha], axis=1) * acc_scr[mp][...] + pv
            m_scr[mp][...] = m_new

    @pl.when(i >= ctx_tiles)
    def _():
        @pl.when(i == ctx_tiles)
        def _():
            stage(0, 0)

        def body(g, carry):
            stage(2 * g + 1, 1)
            consume(2 * g, 0)
            stage(2 * g + 2, 0)
            consume(2 * g + 1, 1)
            return carry
        lax.fori_loop(0, n_blocks // 2 - 1, body, 0)
        stage(n_blocks - 1, 1)
        consume(n_blocks - 2, 0)
        stage(0, 0, map_halves(qn_ref[0]))
        consume(n_blocks - 1, 1)

    lv = lam_ref[...]
    dot01 = jnp.sum(lv[0:1, :] * lv[1:2, :], axis=1, keepdims=True)
    dot23 = jnp.sum(lv[2:3, :] * lv[3:4, :], axis=1, keepdims=True)
    lam = jnp.exp(dot01) - jnp.exp(dot23) + lam_init
    a0 = acc_scr[0][...]
    a1 = acc_scr[1][...]
    o = a0[:, 0:dv] / a0[:, dv:dv + 1] - lam * (a1[:, 0:dv] / a1[:, dv:dv + 1])
    o = o * lax.rsqrt(jnp.mean(o * o, axis=-1, keepdims=True) + NORM_EPS) * sg_ref[...]
    o_ref[0] = (o * (1.0 - lam_init)).astype(BF16)


def _diff_attention(q, k, v_aug, lam8, subln_g, n_ctx, lam_init):
    b, t, w = q.shape
    tq = ROW_TILE
    n_lat = t - n_ctx
    key_block = 1024 if n_lat % 2048 == 0 else 512
    n_blocks = n_lat // key_block
    assert n_lat % (2 * key_block) == 0 and n_ctx % tq == 0 and n_ctx % LANES == 0
    kern = functools.partial(_attn_kernel, ctx_tiles=n_ctx // tq, n_ctx=n_ctx, key_block=key_block,
                             n_blocks=n_blocks, lam_init=lam_init)
    return pl.pallas_call(
        kern,
        grid=(b, DA_HEADS, t // tq),
        in_specs=[pl.BlockSpec((1, tq, LANES), lambda bi, h, i: (bi, i, h)),
                  pl.BlockSpec((1, tq, LANES), lambda bi, h, i: (bi, jnp.minimum(i + 1, t // tq - 1), h)),
                  pl.BlockSpec((1, t, LANES), lambda bi, h, i: (bi, 0, h)),
                  pl.BlockSpec((1, t, 2 * DA_V_DIM), lambda bi, h, i: (bi, 0, h)),
                  pl.BlockSpec(lam8.shape, lambda bi, h, i: (0, 0)),
                  pl.BlockSpec(subln_g.shape, lambda bi, h, i: (0, 0))],
        out_specs=pl.BlockSpec((1, tq, LANES), lambda bi, h, i: (bi, i, h)),
        out_shape=jax.ShapeDtypeStruct((b, t, w), BF16),
        scratch_shapes=[pltpu.VMEM((2, tq, LANES), BF16)]
                       + [pltpu.VMEM((tq, key_block), F32)] * 4
                       + [pltpu.VMEM((tq, LANES), F32)] * 6
                       + [pltpu.VMEM((tq, 2 * DA_V_DIM), F32)] * 2,
        compiler_params=_params("arbitrary", "arbitrary", "arbitrary"),
        name="diff_attention",
    )(q, q, k, v_aug, lam8, subln_g)


def _top2_sum(a, b, c, d):
    hi1, lo1 = jnp.maximum(a, b), jnp.minimum(a, b)
    hi2, lo2 = jnp.maximum(c, d), jnp.minimum(c, d)
    return jnp.maximum(hi1, hi2) + jnp.maximum(jnp.minimum(hi1, hi2), jnp.maximum(lo1, lo2))


def _route(f, rwt_ref, rb_ref, cnt_ref):
    tm = f.shape[0]
    aff = _sigmoid(_dot_nt(rwt_ref[...], f, HI))
    biased = aff + rb_ref[:, 0:1]
    bz = [biased[e:e + 1, :] for e in range(N_EXPERTS)]
    af = [aff[e:e + 1, :] for e in range(N_EXPERTS)]
    n_grp = N_EXPERTS // EXPERTS_PER_GROUP
    scores = [_top2_sum(*bz[EXPERTS_PER_GROUP * g:EXPERTS_PER_GROUP * (g + 1)]) for g in range(n_grp)]
    best = scores[0]
    sel_grp = jnp.zeros_like(best, dtype=jnp.int32)
    for g in range(1, n_grp):
        better = scores[g] > best
        sel_grp = jnp.where(better, g, sel_grp)
        best = jnp.where(better, scores[g], best)
    chosen = []
    for e in range(N_EXPERTS):
        g = e // EXPERTS_PER_GROUP
        rank = jnp.zeros_like(sel_grp)
        for o in range(EXPERTS_PER_GROUP * g, EXPERTS_PER_GROUP * (g + 1)):
            if o == e:
                continue
            beats = (bz[o] > bz[e]) if o > e else (bz[o] >= bz[e])
            rank = rank + beats.astype(jnp.int32)
        chosen.append(jnp.logical_and(sel_grp == g, rank < 2))
    denom = sum(jnp.where(chosen[e], af[e], 0.0) for e in range(N_EXPERTS))
    erow = lax.broadcasted_iota(jnp.int32, (N_EXPERTS, tm), 0)
    one_hot = jnp.zeros((N_EXPERTS, tm), F32)
    for e in range(N_EXPERTS):
        one_hot = jnp.where(jnp.logical_and(erow == e, chosen[e]), 1.0, one_hot)
    earlier = (lax.broadcasted_iota(jnp.int32, (tm, tm), 0)
               < lax.broadcasted_iota(jnp.int32, (tm, tm), 1)).astype(BF16)
    rank_all = _dot(one_hot.astype(BF16), earlier) + cnt_ref[:, 0:1]
    cnt_ref[...] = cnt_ref[...] + jnp.sum(one_hot, axis=1, keepdims=True)

    seen = jnp.zeros((1, tm), jnp.bool_)
    e_a = e_b = jnp.zeros((1, tm), jnp.int32)
    r_a = r_b = w_a = w_b = jnp.zeros((1, tm), F32)
    for e in range(N_EXPERTS):
        first = jnp.logical_and(chosen[e], jnp.logical_not(seen))
        second = jnp.logical_and(chosen[e], seen)
        rk = rank_all[e:e + 1, :]
        wt = af[e] / denom
        e_a, e_b = jnp.where(first, e, e_a), jnp.where(second, e, e_b)
        r_a, r_b = jnp.where(first, rk, r_a), jnp.where(second, rk, r_b)
        w_a, w_b = jnp.where(first, wt, w_a), jnp.where(second, wt, w_b)
        seen = jnp.logical_or(seen, chosen[e])
    r8 = lax.broadcasted_iota(jnp.int32, (8, tm), 0)
    sel = jnp.where(r8 == 0, e_a, jnp.where(r8 == 1, e_b, jnp.where(
        r8 == 2, r_a.astype(jnp.int32), jnp.where(r8 == 3, r_b.astype(jnp.int32), 0))))
    row = lax.broadcasted_iota(jnp.int32, (LANES, tm), 0)
    w_t = jnp.where(row == 0, w_a, jnp.where(row == 1, w_b, 0.0))
    return sel, w_t.T


def _post_kernel(*refs, even):
    if even:
        (x_ref, a_ref, hf_ref, hb_ref, og_ref, mod_ref, wa_ref, wm_ref, gf_ref, rwt_ref, rb_ref,
         xo_ref, f_ref, sel_ref, w_ref, cnt_ref, cnt_scr) = refs
        m = ((hf_ref[0] + hb_ref[0]) * og_ref[0].astype(F32)).astype(BF16)
        o = _dot(a_ref[0], wa_ref[...]) + _dot(m, wm_ref[...])
    else:
        (x_ref, a_ref, mod_ref, wa_ref, gf_ref, rwt_ref, rb_ref,
         xo_ref, f_ref, sel_ref, w_ref, cnt_ref, cnt_scr) = refs
        o = _dot(a_ref[0].astype(BF16), wa_ref[...])

    @pl.when(jnp.logical_and(pl.program_id(0) == 0, pl.program_id(1) == 0))
    def _():
        cnt_scr[...] = jnp.zeros(cnt_scr.shape, F32)

    x = x_ref[0] + mod_ref[0, 2:3, :] * o
    xo_ref[0] = x
    f = _norm_mod(x, gf_ref[...], mod_ref[0, 3:4, :], mod_ref[0, 4:5, :])
    _store_row_tiles(f_ref.at[0], f)
    sel, w_col = _route(f, rwt_ref, rb_ref, cnt_scr)
    sel_ref[0, 0] = sel
    w_ref[0] = w_col
    cnt_ref[...] = cnt_scr[...]


def _post_mixer(x, acts, weights, modtab, gffn, rwt, rb, n_ctx, x_row_off, even):
    b, t, _ = acts[0].shape
    d = x.shape[2]
    tm = ROW_TILE
    nt = t // tm
    ctx_tiles = n_ctx // tm
    off = x_row_off // tm
    full = lambda a: pl.BlockSpec(a.shape, lambda bi, i: (0,) * a.ndim)
    row = lambda w: pl.BlockSpec((1, tm, w), lambda bi, i: (bi, i, 0))
    mod_spec = pl.BlockSpec(
        (1, 8, d), lambda bi, i: (2 * bi + (i + off >= ctx_tiles).astype(jnp.int32), 0, 0))
    in_specs = ([pl.BlockSpec((1, tm, d), lambda bi, i: (bi, i + off, 0))]
                + [row(a.shape[2]) for a in acts] + [mod_spec]
                + [full(w) for w in weights] + [full(gffn), full(rwt), full(rb)])
    return pl.pallas_call(
        functools.partial(_post_kernel, even=even),
        grid=(b, nt),
        in_specs=in_specs,
        out_specs=[row(d), pl.BlockSpec((1, tm * d // LANES, LANES), lambda bi, i: (bi, i, 0)),
                   pl.BlockSpec((1, 1, 8, tm), lambda bi, i: (bi, i, 0, 0)),
                   row(LANES),
                   pl.BlockSpec((N_EXPERTS, LANES), lambda bi, i: (0, 0))],
        out_shape=[jax.ShapeDtypeStruct((b, t, d), F32),
                   jax.ShapeDtypeStruct((b, t * d // LANES, LANES), F32),
                   jax.ShapeDtypeStruct((b, nt, 8, tm), jnp.int32),
                   jax.ShapeDtypeStruct((b, t, LANES), F32),
                   jax.ShapeDtypeStruct((N_EXPERTS, LANES), F32)],
        scratch_shapes=[pltpu.VMEM((N_EXPERTS, LANES), F32)],
        compiler_params=_params("arbitrary", "arbitrary"),
        name="post_mixer_even" if even else "post_mixer_odd",
    )(x, *acts, modtab, *weights, gffn, rwt, rb)


EXPERT_ROW_TILE = 256
DMA_ISSUE_UNROLL = 8


def _moe_plan(sel, counts, n_tok):
    e_a, e_b, r_a, r_b = (sel[:, :, k, :].reshape(-1) for k in range(4))
    cnt = counts[:, 0].astype(jnp.int32)
    padded = ((cnt + EXPERT_ROW_TILE - 1) // EXPERT_ROW_TILE) * EXPERT_ROW_TILE
    ends = jnp.cumsum(padded)
    starts = ends - padded
    pos = jnp.concatenate([starts[e_a] + r_a, starts[e_b] + r_b]).astype(jnp.int32)
    n_tiles = 2 * n_tok // EXPERT_ROW_TILE + N_EXPERTS
    tile_start = jnp.arange(n_tiles, dtype=jnp.int32) * EXPERT_ROW_TILE
    tile_expert = jnp.minimum(jnp.sum(tile_start[:, None] >= ends[None, :], axis=1),
                              N_EXPERTS - 1).astype(jnp.int32)
    tiles_used = (ends[-1:] // EXPERT_ROW_TILE).astype(jnp.int32)
    return pos, tile_expert, tiles_used, n_tiles


def _store_row_tiles(ref, x):
    rows, d = x.shape
    n_sub = d // LANES
    for s in range(n_sub):
        ref[pl.ds(s, rows, stride=n_sub), :] = x[:, s * LANES:(s + 1) * LANES]


def _load_row_tiles(ref, rows):
    n_sub = ref.shape[0] // rows
    return jnp.concatenate([ref[pl.ds(s, rows, stride=n_sub), :] for s in range(n_sub)], axis=1)


def _dispatch_kernel(pos_ref, f_ref, init_ref, out_ref, sem, *, n_tok, tm):
    del init_ref
    n_sub = f_ref.shape[1] // tm
    base = (pl.program_id(0) * pl.num_programs(1) + pl.program_id(1)) * tm

    def row_copy(r, k):
        dst = pos_ref[k * n_tok + base + r]
        src = f_ref.at[0, pl.ds(pl.multiple_of(r * n_sub, n_sub), n_sub), :]
        return pltpu.make_async_copy(src, out_ref.at[dst], sem)

    def issue(r, carry):
        row_copy(r, 0).start()
        row_copy(r, 1).start(priority=1)
        return carry

    lax.fori_loop(0, tm, issue, 0, unroll=DMA_ISSUE_UNROLL)
    for _ in range(2):
        pltpu.make_async_copy(out_ref.at[pl.ds(0, tm)], out_ref.at[pl.ds(0, tm)], sem).wait()


def _dispatch(pos, f, n_rows, n_tok):
    b, rows, _ = f.shape
    n_sub = rows * b // n_tok
    tm = ROW_TILE
    grid_spec = pltpu.PrefetchScalarGridSpec(
        num_scalar_prefetch=1,
        grid=(b, n_tok // b // tm),
        in_specs=[pl.BlockSpec((1, tm * n_sub, LANES), lambda bi, i, pos_ref: (bi, i, 0)),
                  pl.BlockSpec(memory_space=pl.ANY)],
        out_specs=pl.BlockSpec(memory_space=pl.ANY),
        scratch_shapes=[pltpu.SemaphoreType.DMA(())])
    return pl.pallas_call(
        functools.partial(_dispatch_kernel, n_tok=n_tok, tm=tm),
        grid_spec=grid_spec,
        out_shape=jax.ShapeDtypeStruct((n_rows, n_sub, LANES), F32),
        input_output_aliases={2: 0},
        compiler_params=_params("arbitrary", "arbitrary"),
        name="moe_dispatch",
    )(pos, f, jnp.zeros((n_rows, n_sub, LANES), F32))


def _expert_ffn_kernel(te_ref, used_ref, x_ref, wg_ref, wu_ref, wd_ref, y_ref):
    del te_ref
    live = pl.program_id(0) < used_ref[0]

    @pl.when(live)
    def _():
        xb = _load_row_tiles(x_ref, EXPERT_ROW_TILE).astype(BF16)
        he = _silu(_dot(xb, wg_ref[0])) * _dot(xb, wu_ref[0])
        _store_row_tiles(y_ref, _dot(he.astype(BF16), wd_ref[0]))

    @pl.when(jnp.logical_not(live))
    def _():
        y_ref[...] = jnp.zeros(y_ref.shape, F32)


def _expert_ffn(tile_expert, tiles_used, xs, wg, wu, wd, n_tiles):
    n_rows, n_sub, _ = xs.shape
    _, d, d_e = wg.shape
    tr = EXPERT_ROW_TILE
    grid_spec = pltpu.PrefetchScalarGridSpec(
        num_scalar_prefetch=2,
        grid=(n_tiles,),
        in_specs=[pl.BlockSpec((tr * n_sub, LANES), lambda j, te, used: (j, 0)),
                  pl.BlockSpec((1, d, d_e), lambda j, te, used: (te[j], 0, 0)),
                  pl.BlockSpec((1, d, d_e), lambda j, te, used: (te[j], 0, 0)),
                  pl.BlockSpec((1, d_e, d), lambda j, te, used: (te[j], 0, 0))],
        out_specs=pl.BlockSpec((tr * n_sub, LANES), lambda j, te, used: (j, 0)))
    ys = pl.pallas_call(
        _expert_ffn_kernel,
        grid_spec=grid_spec,
        out_shape=jax.ShapeDtypeStruct((n_rows * n_sub, LANES), F32),
        compiler_params=_params("arbitrary"),
        name="moe_expert_ffn",
    )(tile_expert, tiles_used, xs.reshape(n_rows * n_sub, LANES), wg, wu, wd)
    return ys.reshape(n_rows, n_sub, LANES)


def _combine_kernel(pos_ref, y_ref, w_ref, x_ref, modc_ref, modl_ref, fg_ref, o_ref, buf, sem,
                    *, n_tok, n_ctx, final_norm):
    i = pl.program_id(1)
    tm = x_ref.shape[1]
    base = (pl.program_id(0) * pl.num_programs(1) + i) * tm

    n_sub = y_ref.shape[1]

    def row_copy(r, k):
        src = pos_ref[k * n_tok + base + r]
        dst = buf.at[k, pl.ds(pl.multiple_of(r * n_sub, n_sub), n_sub), :]
        return pltpu.make_async_copy(y_ref.at[src], dst, sem)

    def issue(r, carry):
        row_copy(r, 0).start()
        row_copy(r, 1).start(priority=1)
        return carry

    lax.fori_loop(0, tm, issue, 0, unroll=DMA_ISSUE_UNROLL)
    for _ in range(2):
        pltpu.make_async_copy(y_ref.at[pl.ds(0, tm)], y_ref.at[pl.ds(0, tm)], sem).wait()
    w = w_ref[0]
    y = _load_row_tiles(buf.at[0], tm) * w[:, 0:1] + _load_row_tiles(buf.at[1], tm) * w[:, 1:2]
    rows = i * tm + lax.broadcasted_iota(jnp.int32, (tm, 1), 0)
    gate = jnp.where(rows < n_ctx, modc_ref[0, 5:6, :], modl_ref[0, 5:6, :])
    out = x_ref[0] + gate * y
    if final_norm:
        out = out * lax.rsqrt(jnp.mean(out * out, axis=-1, keepdims=True) + NORM_EPS) * fg_ref[...]
    o_ref[0] = out


def _combine(pos, ys, w_col, xmid, modtab, final_g, n_ctx, final_norm):
    b, t, d = xmid.shape
    tm = ROW_TILE
    row = lambda w: pl.BlockSpec((1, tm, w), lambda bi, i, pos_ref: (bi, i, 0))
    grid_spec = pltpu.PrefetchScalarGridSpec(
        num_scalar_prefetch=1,
        grid=(b, t // tm),
        in_specs=[pl.BlockSpec(memory_space=pl.ANY), row(LANES), row(d),
                  pl.BlockSpec((1, 8, d), lambda bi, i, pos_ref: (2 * bi, 0, 0)),
                  pl.BlockSpec((1, 8, d), lambda bi, i, pos_ref: (2 * bi + 1, 0, 0)),
                  pl.BlockSpec(final_g.shape, lambda bi, i, pos_ref: (0, 0))],
        out_specs=row(d),
        scratch_shapes=[pltpu.VMEM((2, tm * d // LANES, LANES), F32), pltpu.SemaphoreType.DMA(())])
    return pl.pallas_call(
        functools.partial(_combine_kernel, n_tok=b * t, n_ctx=n_ctx, final_norm=final_norm),
        grid_spec=grid_spec,
        out_shape=jax.ShapeDtypeStruct((b, t, d), F32),
        compiler_params=_params("arbitrary", "arbitrary"),
        name="moe_combine",
    )(pos, ys, w_col, xmid, modtab, modtab, final_g)


def _moe(f, sel, w_col, counts, wg, wu, wd, xmid, modtab, final_g, n_ctx, final_norm):
    b, t, _ = xmid.shape
    pos, tile_expert, tiles_used, n_tiles = _moe_plan(sel, counts, b * t)
    xs = _dispatch(pos, f, n_tiles * EXPERT_ROW_TILE, b * t)
    ys = _expert_ffn(tile_expert, tiles_used, xs, wg, wu, wd, n_tiles)
    return _combine(pos, ys, w_col, xmid, modtab, final_g, n_ctx, final_norm)


def _chan_dft_kernel(x_ref, mod_ref, g_ref, w_ref, zr_ref, zi_ref):
    h = _norm_mod(x_ref[0], g_ref[...], mod_ref[0, 0:1, :], mod_ref[0, 1:2, :])
    gd = w_ref.shape[0]
    for gi in range(h.shape[1] // gd):
        z = _dot_split(h[:, gi * gd:(gi + 1) * gd], w_ref[...])
        zr_ref[0, :, gi * gd:(gi + 1) * gd] = z[:, :gd]
        zi_ref[0, :, gi * gd:(gi + 1) * gd] = z[:, gd:]


def _chan_dft(x, modtab, g, w_cs, x_row_off, t):
    b, _, d = x.shape
    tm = ROW_TILE
    off = x_row_off // tm
    row = pl.BlockSpec((1, tm, d), lambda bi, i: (bi, i, 0))
    return pl.pallas_call(
        _chan_dft_kernel,
        grid=(b, t // tm),
        in_specs=[pl.BlockSpec((1, tm, d), lambda bi, i: (bi, i + off, 0)),
                  pl.BlockSpec((1, 8, d), lambda bi, i: (2 * bi + 1, 0, 0)),
                  pl.BlockSpec(g.shape, lambda bi, i: (0, 0)),
                  pl.BlockSpec(w_cs.shape, lambda bi, i: (0, 0))],
        out_specs=[row, row],
        out_shape=[jax.ShapeDtypeStruct((b, t, d), F32)] * 2,
        compiler_params=_params("parallel", "parallel"),
        name="chan_dft",
    )(x, modtab, g, w_cs)


def _dft1_kernel(zr_ref, zi_ref, w_ref, yr_ref, yi_ref):
    n1 = zr_ref.shape[1]
    y = _dot_split(w_ref[...], jnp.concatenate([zr_ref[0], zi_ref[0]], axis=0))
    yr_ref[0] = y[:n1]
    yi_ref[0] = y[n1:]


def _dft1(zr, zi, w1):
    b, n1, cols = zr.shape
    tn = min(cols, 4096)
    blk = pl.BlockSpec((1, n1, tn), lambda bi, j: (bi, 0, j))
    return pl.pallas_call(
        _dft1_kernel,
        grid=(b, cols // tn),
        in_specs=[blk, blk, pl.BlockSpec(w1.shape, lambda bi, j: (0, 0))],
        out_specs=[blk, blk],
        out_shape=[jax.ShapeDtypeStruct(zr.shape, F32)] * 2,
        compiler_params=_params("parallel", "parallel"),
        name="dft_stage1",
    )(zr, zi, w1)


def _dft2_kernel(yr_ref, yi_ref, tab_ref, o_ref):
    y = jnp.concatenate([yr_ref[0, 0], yi_ref[0, 0]], axis=0)
    o_ref[0, 0] = _dot_split(tab_ref[0], y)


def _dft2(yr, yi, tab):
    b, n1, n2, d = yr.shape
    blk = pl.BlockSpec((1, 1, n2, d), lambda bi, k1: (bi, k1, 0, 0))
    return pl.pallas_call(
        _dft2_kernel,
        grid=(b, n1),
        in_specs=[blk, blk, pl.BlockSpec((1, n2, 2 * n2), lambda bi, k1: (k1, 0, 0))],
        out_specs=blk,
        out_shape=jax.ShapeDtypeStruct(yr.shape, F32),
        compiler_params=_params("parallel", "parallel"),
        name="dft_stage2",
    )(yr, yi, tab)


def _dft_tables(t, gd):
    n2 = ML_CHUNK
    n1 = t // n2
    def cs(num, den):
        ang = (2.0 * np.pi / den) * (num % den).astype(np.float64)
        return np.cos(ang), np.sin(ang)
    c = np.arange(gd)
    cc, sc = cs(np.outer(c, c), gd)
    w_cs = np.concatenate([cc, -sc], axis=1)
    a = np.arange(n1)
    c1, s1 = cs(np.outer(a, a), n1)
    w1 = np.block([[c1, s1], [-s1, c1]])
    k = a[:, None, None] + n1 * np.arange(n2)[None, :, None]
    c2, s2 = cs(k * np.arange(n2)[None, None, :], t)
    tab = np.concatenate([c2, s2], axis=2) / math.sqrt(t * gd)
    return (jnp.asarray(w_cs, F32), jnp.asarray(w1, F32), jnp.asarray(tab, F32))


def _rope_tables(n_ctx, n_lat):
    pos = jnp.arange(n_lat, dtype=jnp.int32)
    n_axis = DA_QK_DIM // 4
    inv = ROPE_BASE ** (-jnp.arange(n_axis, dtype=F32) / n_axis)
    ang = jnp.concatenate([(pos // GRID_W).astype(F32)[:, None] * inv,
                           (pos % GRID_W).astype(F32)[:, None] * inv], axis=-1)
    cos, sin = jnp.cos(ang), jnp.sin(ang)
    cos = jnp.concatenate([jnp.ones((n_ctx, 2 * n_axis), F32), cos], axis=0)
    sin = jnp.concatenate([jnp.zeros((n_ctx, 2 * n_axis), F32), sin], axis=0)
    cos128 = jnp.concatenate([cos, cos, cos, cos], axis=1)
    sin128 = jnp.concatenate([-sin, sin, -sin, sin], axis=1)
    return cos128, sin128


def _deinterleave(w):
    d, n = w.shape
    w = w.reshape(d, n // DA_QK_DIM, DA_QK_DIM // 2, 2)
    return jnp.concatenate([w[..., 0], w[..., 1]], axis=-1).reshape(d, n)


def _pad_rows(a, rows):
    return jnp.concatenate([a, jnp.zeros((rows - a.shape[0],) + a.shape[1:], a.dtype)], axis=0)


def _pad_cols(a, cols):
    return jnp.concatenate([a, jnp.zeros(a.shape[:-1] + (cols - a.shape[-1],), a.dtype)], axis=-1)


def kernel(x, c, ctx, c_ctx, ada_w, ada_b, norm_mix_g, norm_ffn_g, even_w_in, even_w_out,
           even_conv_w, even_gate_b, even_lam, even_subln_g, odd_w_fnet, router_w, router_b,
           exp_w_gate, exp_w_up, exp_w_down, final_g):
    b, n_lat, d = x.shape
    n_ctx = ctx.shape[1]
    depth = ada_w.shape[0]
    assert depth == 2 and b + 1 <= 8
    assert n_ctx % ROW_TILE == 0 and n_lat % ROW_TILE == 0

    cond8 = _pad_rows(jnp.concatenate([c_ctx[None, :], c], axis=0), 8)
    rwt = router_w.T
    rb = jnp.broadcast_to(router_b[:, None], (N_EXPERTS, LANES))
    row2 = lambda v: v.reshape(1, -1)

    def modtab_for(layer):
        mods = _ada_mods(cond8, ada_w[layer], ada_b[layer]).reshape(8, 6, d)
        mods = jnp.concatenate([mods, jnp.zeros((8, 2, d), F32)], axis=1)
        idx = np.array([[0, 1 + bi] for bi in range(b)]).reshape(-1)
        return mods[idx]

    xs = jnp.concatenate([ctx, x], axis=1)
    modtab = modtab_for(0)
    w_in = even_w_in[0]
    o1 = DA_HEADS * 2 * DA_QK_DIM
    o2 = 2 * o1
    o3 = o2 + DA_HEADS * DA_V_DIM
    o4 = o3 + 2 * ML_HEADS * ML_DIM
    o5 = o4 + ML_HEADS * ML_DIM
    o6 = o5 + ML_HEADS * ML_DIM
    ws = [(_deinterleave(w_in[:, :o1]) * (DA_QK_DIM ** -0.5)).astype(BF16),
          _deinterleave(w_in[:, o1:o2]).astype(BF16),
          w_in[:, o2:o3].astype(BF16), w_in[:, o3:o4].astype(BF16),
          w_in[:, o4:o5].astype(BF16), w_in[:, o5:o6].astype(BF16),
          _pad_cols(w_in[:, o6:], LANES).astype(BF16)]
    gate_b = _pad_cols(even_gate_b[0].reshape(1, -1), LANES)
    cos128, sin128 = _rope_tables(n_ctx, n_lat)
    daq, dak, dav, mqk, mv, og, gates = _inproj(xs, modtab, row2(norm_mix_g[0]), cos128, sin128,
                                                 ws, gate_b, n_ctx)
    mq, mk, gc, gr = _mlprep(mqk, _pad_rows(even_conv_w[0], 8), gates, n_ctx)
    hf, hb = _mlstm(mq, mk, mv, gc, gr, n_ctx)
    lam_init = 0.8 - 0.6 * math.exp(-0.3 * 0)
    lam8 = _pad_rows(even_lam[0], 8)
    att = _diff_attention(daq, dak, dav, lam8, row2(even_subln_g[0]), n_ctx, lam_init)
    w_out = even_w_out[0].astype(BF16)
    half = DA_HEADS * DA_V_DIM
    xmid, f, sel, w_col, counts = _post_mixer(xs, [att, hf, hb, og], [w_out[:half], w_out[half:]],
                                              modtab, row2(norm_ffn_g[0]), rwt, rb, n_ctx, 0, True)
    xs = _moe(f, sel, w_col, counts, exp_w_gate[0].astype(BF16), exp_w_up[0].astype(BF16),
              exp_w_down[0].astype(BF16), xmid, modtab, row2(final_g), n_ctx, False)

    modtab = modtab_for(1)
    gd = d // FN_GROUPS
    w_cs, w1, tab = _dft_tables(n_lat, gd)
    n2 = ML_CHUNK
    n1 = n_lat // n2
    zr, zi = _chan_dft(xs, modtab, row2(norm_mix_g[1]), w_cs, n_ctx, n_lat)
    yr, yi = _dft1(zr.reshape(b, n1, n2 * d), zi.reshape(b, n1, n2 * d), w1)
    fo = _dft2(yr.reshape(b, n1, n2, d), yi.reshape(b, n1, n2, d), tab)
    fo = fo.transpose(0, 2, 1, 3).reshape(b, n_lat, d)
    xmid, f, sel, w_col, counts = _post_mixer(xs, [fo], [odd_w_fnet[0].astype(BF16)], modtab,
                                              row2(norm_ffn_g[1]), rwt, rb, 0, n_ctx, False)
    return _moe(f, sel, w_col, counts, exp_w_gate[1].astype(BF16), exp_w_up[1].astype(BF16),
                exp_w_down[1].astype(BF16), xmid, modtab, row2(final_g), 0, True)
```

```python
import functools
import math

import jax
import jax.numpy as jnp
import numpy as np
from jax import lax
from jax.experimental import pallas as pl
from jax.experimental.pallas import tpu as pltpu

F32 = jnp.float32
BF16 = jnp.bfloat16

NORM_EPS = 1e-6
GRID_W = 64
DA_HEADS = 4
DA_QK_DIM = 64
DA_V_DIM = 128
ML_HEADS = 4
ML_DIM = 128
ML_CHUNK = 128
FN_GROUPS = 4
N_EXPERTS = 16
EXPERTS_PER_GROUP = 4
ROPE_BASE = 10000.0
LANES = 128
ROW_TILE = 256
VMEM_LIMIT_BYTES = 56 * 1024 * 1024
HI = lax.Precision.HIGHEST
LOG2_E = math.log2(math.e)


def _params(*sem):
    return pltpu.CompilerParams(dimension_semantics=sem, vmem_limit_bytes=VMEM_LIMIT_BYTES)


def _dot(a, b, precision=None):
    return jnp.dot(a, b, preferred_element_type=F32, precision=precision)


def _dot_nt(a, b, precision=None):
    return lax.dot_general(a, b, (((1,), (1,)), ((), ())), preferred_element_type=F32,
                           precision=precision)


def _dot_split(a, b):
    a_hi = a.astype(BF16)
    b_hi = b.astype(BF16)
    a_lo = (a - a_hi.astype(F32)).astype(BF16)
    b_lo = (b - b_hi.astype(F32)).astype(BF16)
    return _dot(a_hi, b_hi) + (_dot(a_hi, b_lo) + _dot(a_lo, b_hi))


def _sigmoid(x):
    return 1.0 / (1.0 + jnp.exp(-x))


def _silu(x):
    return x * _sigmoid(x)


def _norm_mod(x, g, shift, scale):
    y = x * lax.rsqrt(jnp.mean(x * x, axis=-1, keepdims=True) + NORM_EPS) * g
    return y * (1.0 + scale) + shift


def _ada_kernel(c_ref, w_ref, b_ref, o_ref):
    o_ref[...] = _dot(_silu(c_ref[...]), w_ref[...], HI) + b_ref[...]


def _ada_mods(cond8, w, b):
    d, n = w.shape
    tn = n // 6
    return pl.pallas_call(
        _ada_kernel,
        grid=(6,),
        in_specs=[pl.BlockSpec((8, d), lambda j: (0, 0)),
                  pl.BlockSpec((d, tn), lambda j: (0, j)),
                  pl.BlockSpec((1, tn), lambda j: (0, j))],
        out_specs=pl.BlockSpec((8, tn), lambda j: (0, j)),
        out_shape=jax.ShapeDtypeStruct((8, n), F32),
        compiler_params=_params("arbitrary"),
        name="ada_mods",
    )(cond8, w, b.reshape(1, n))


def _inproj_kernel(x_ref, mod_ref, g_ref, cos_ref, sin_ref, wq_ref, wk_ref, wv_ref, wmqk_ref,
                   wmv_ref, wmo_ref, wg_ref, gb_ref,
                   q_ref, k_ref, v_ref, mqk_ref, mv_ref, og_ref, gate_ref):
    x = x_ref[0]
    h = _norm_mod(x, g_ref[...], mod_ref[0, 0:1, :], mod_ref[0, 1:2, :]).astype(BF16)
    tm = x.shape[0]
    width = q_ref.shape[2]
    cos = jnp.concatenate([cos_ref[...]] * (width // LANES), axis=1)
    sin = jnp.concatenate([sin_ref[...]] * (width // LANES), axis=1)
    lane = lax.broadcasted_iota(jnp.int32, (tm, width), 1)
    lower = (lane & (DA_QK_DIM - 1)) < (DA_QK_DIM // 2)

    def rope(u):
        swapped = jnp.where(lower, pltpu.roll(u, width - DA_QK_DIM // 2, 1),
                            pltpu.roll(u, DA_QK_DIM // 2, 1))
        return u * cos + swapped * sin

    q_ref[0] = (rope(_dot(h, wq_ref[...])) * LOG2_E).astype(BF16)
    k_ref[0] = rope(_dot(h, wk_ref[...])).astype(BF16)
    v = _dot(h, wv_ref[...]).astype(BF16)
    ones_col = (lax.broadcasted_iota(jnp.int32, (tm, DA_V_DIM), 1) == 0).astype(BF16)
    v_ref[0] = jnp.concatenate(
        [blk for hd in range(DA_HEADS) for blk in (v[:, hd * DA_V_DIM:(hd + 1) * DA_V_DIM], ones_col)],
        axis=1)
    mqk_ref[0] = _dot(h, wmqk_ref[...]).astype(BF16)
    mv_ref[0] = _dot(h, wmv_ref[...]).astype(BF16)
    og_ref[0] = _sigmoid(_dot(h, wmo_ref[...])).astype(BF16)
    g = _dot(h, wg_ref[...]) + gb_ref[...]
    glane = lax.broadcasted_iota(jnp.int32, g.shape, 1)
    is_forget = ((glane // ML_HEADS) & 1) == 1
    log_sig = jnp.minimum(g, 0.0) - jnp.log(1.0 + jnp.exp(-jnp.abs(g)))
    gate_ref[0] = jnp.where(is_forget, log_sig, g)


def _inproj(xs, modtab, g, cos, sin, ws, gate_b, n_ctx):
    b, t, d = xs.shape
    tm = ROW_TILE
    nt = t // tm
    ctx_tiles = n_ctx // tm
    row = lambda w: pl.BlockSpec((1, tm, w), lambda bi, i: (bi, i, 0))
    full = lambda a: pl.BlockSpec(a.shape, lambda bi, i: (0,) * a.ndim)
    widths = [w.shape[1] for w in ws]
    widths[2] *= 2
    out_dtypes = [BF16] * 6 + [F32]
    return pl.pallas_call(
        _inproj_kernel,
        grid=(b, nt),
        in_specs=[row(d),
                  pl.BlockSpec((1, 8, d), lambda bi, i: (2 * bi + (i >= ctx_tiles).astype(jnp.int32), 0, 0)),
                  full(g),
                  pl.BlockSpec((tm, LANES), lambda bi, i: (i, 0)),
                  pl.BlockSpec((tm, LANES), lambda bi, i: (i, 0))]
                 + [full(w) for w in ws] + [full(gate_b)],
        out_specs=[row(w) for w in widths],
        out_shape=[jax.ShapeDtypeStruct((b, t, w), dt) for w, dt in zip(widths, out_dtypes)],
        compiler_params=_params("parallel", "parallel"),
        name="inproj",
    )(xs, modtab, g, cos, sin, *ws, gate_b)


def _split3(x):
    x1 = x.astype(BF16)
    r1 = x - x1.astype(F32)
    x2 = r1.astype(BF16)
    x3 = (r1 - x2.astype(F32)).astype(BF16)
    return x1, x2, x3


VT_ROWS = ML_DIM + 16


def _mlprep_kernel(cur_ref, prev_ref, next_ref, cw_ref, gate_ref, v_ref,
                   mq_ref, mk_ref, gc_ref, gr_ref, vt_ref, *, ctx_tiles, n_tiles):
    i = pl.program_id(1)
    cur = cur_ref[0].astype(F32)
    tm, w = cur.shape
    prev_ok = i != ctx_tiles
    if ctx_tiles > 0:
        prev_ok = jnp.logical_and(prev_ok, i != 0)
        next_ok = jnp.logical_and(i != ctx_tiles - 1, i != n_tiles - 1)
    else:
        next_ok = i != n_tiles - 1
    prev_row = jnp.where(prev_ok, prev_ref[0, 7:8, :].astype(F32), 0.0)
    next_row = jnp.where(next_ok, next_ref[0, 0:1, :].astype(F32), 0.0)
    ridx = lax.broadcasted_iota(jnp.int32, (tm, w), 0)
    before = jnp.where(ridx == 0, prev_row, pltpu.roll(cur, 1, 0))
    after = jnp.where(ridx == tm - 1, next_row, pltpu.roll(cur, tm - 1, 0))
    y = _silu(before * cw_ref[0:1, :] + cur * cw_ref[1:2, :] + after * cw_ref[2:3, :])
    half = w // 2
    mq_ref[0] = y[:, :half].astype(BF16)
    mk_ref[0] = (y[:, half:] * (ML_DIM ** -0.5)).astype(BF16)

    n_g = 4 * ML_HEADS
    hds = ML_HEADS
    r = lax.broadcasted_iota(jnp.int32, (ML_CHUNK, ML_CHUNK), 0)
    c = lax.broadcasted_iota(jnp.int32, (ML_CHUNK, ML_CHUNK), 1)
    lower = (c <= r).astype(BF16)
    upper = (c >= r).astype(BF16)
    ones_row = (r == 0).astype(BF16)[0:VT_ROWS - ML_DIM]
    for ci in range(tm // ML_CHUNK):
        rows = slice(ci * ML_CHUNK, (ci + 1) * ML_CHUNK)
        gm = jnp.where(c < n_g, gate_ref[0, rows, :], 0.0)
        pre = sum(_dot(lower, p) for p in _split3(pltpu.roll(gm, n_g, 1)))
        suf = sum(_dot(upper, p) for p in _split3(pltpu.roll(gm, 2 * n_g, 1)))
        col = gm + pre + suf
        u_f = pltpu.roll(col, 3 * n_g, 1) - pltpu.roll(col, 3 * n_g - (n_g + hds), 1)
        u_b = pltpu.roll(col, 3 * n_g - hds, 1) - pltpu.roll(col, 2 * hds, 1)
        col = col + jnp.where(jnp.logical_and(c >= 3 * n_g, c < 3 * n_g + hds), u_f,
                              jnp.where(jnp.logical_and(c >= 3 * n_g + hds, c < 3 * n_g + 2 * hds),
                                        u_b, 0.0))
        gc_ref[0, rows, :] = col
        gr_ref[0, ci] = col.T
        for hd in range(ML_HEADS):
            v_t = v_ref[0, rows, hd * ML_DIM:(hd + 1) * ML_DIM].astype(F32).T.astype(BF16)
            vt_ref[0, ci, hd * VT_ROWS:hd * VT_ROWS + ML_DIM, :] = v_t
            vt_ref[0, ci, hd * VT_ROWS + ML_DIM:(hd + 1) * VT_ROWS, :] = ones_row


def _mlprep(mqk, conv_w8, gates, mv, n_ctx):
    b, t, w = mqk.shape
    tm = ROW_TILE
    nt = t // tm
    sub = tm // 8
    nsub = t // 8
    cpt = tm // ML_CHUNK
    kern = functools.partial(_mlprep_kernel, ctx_tiles=n_ctx // tm, n_tiles=nt)
    return pl.pallas_call(
        kern,
        grid=(b, nt),
        in_specs=[pl.BlockSpec((1, tm, w), lambda bi, i: (bi, i, 0)),
                  pl.BlockSpec((1, 8, w), lambda bi, i: (bi, jnp.maximum(i * sub - 1, 0), 0)),
                  pl.BlockSpec((1, 8, w), lambda bi, i: (bi, jnp.minimum((i + 1) * sub, nsub - 1), 0)),
                  pl.BlockSpec(conv_w8.shape, lambda bi, i: (0, 0)),
                  pl.BlockSpec((1, tm, LANES), lambda bi, i: (bi, i, 0)),
                  pl.BlockSpec((1, tm, w // 2), lambda bi, i: (bi, i, 0))],
        out_specs=[pl.BlockSpec((1, tm, w // 2), lambda bi, i: (bi, i, 0)),
                   pl.BlockSpec((1, tm, w // 2), lambda bi, i: (bi, i, 0)),
                   pl.BlockSpec((1, tm, LANES), lambda bi, i: (bi, i, 0)),
                   pl.BlockSpec((1, cpt, ML_CHUNK, LANES), lambda bi, i: (bi, i, 0, 0)),
                   pl.BlockSpec((1, cpt, ML_HEADS * VT_ROWS, ML_CHUNK), lambda bi, i: (bi, i, 0, 0))],
        out_shape=[jax.ShapeDtypeStruct((b, t, w // 2), BF16),
                   jax.ShapeDtypeStruct((b, t, w // 2), BF16),
                   jax.ShapeDtypeStruct((b, t, LANES), F32),
                   jax.ShapeDtypeStruct((b, t // ML_CHUNK, ML_CHUNK, LANES), F32),
                   jax.ShapeDtypeStruct((b, t // ML_CHUNK, ML_HEADS * VT_ROWS, ML_CHUNK), BF16)],
        compiler_params=_params("parallel", "parallel"),
        name="mlstm_prep",
    )(mqk, mqk, mqk, conv_w8, gates, mv)


def _mlstm_kernel(qf_ref, kf_ref, vf_ref, gcf_ref, grf_ref, qb_ref, kb_ref, vb_ref, gcb_ref, grb_ref,
                  hf_ref, hb_ref, *scr):
    s = pl.program_id(0)
    nb = qf_ref.shape[0]
    n_g = 4 * ML_HEADS
    state_refs = scr[:len(scr) // 2]
    m_refs = scr[len(scr) // 2:]

    @pl.when(s == 0)
    def _():
        for ref in scr:
            ref[...] = jnp.zeros(ref.shape, F32)

    ki = lax.broadcasted_iota(jnp.int32, (ML_CHUNK, ML_CHUNK), 0)
    qi = lax.broadcasted_iota(jnp.int32, (ML_CHUNK, ML_CHUNK), 1)

    for bi in range(nb):
        for direction in range(2):
            q_ref, k_ref, vt_ref, gc_ref, gr_ref, h_ref = (
                (qf_ref, kf_ref, vf_ref, gcf_ref, grf_ref, hf_ref) if direction == 0 else
                (qb_ref, kb_ref, vb_ref, gcb_ref, grb_ref, hb_ref))
            visible = (ki <= qi) if direction == 0 else (ki >= qi)
            gc = gc_ref[bi]
            gr = gr_ref[bi, 0]
            for hd in range(ML_HEADS):
                chain = (bi * 2 + direction) * ML_HEADS + hd
                c_lf = (2 * direction + 1) * ML_HEADS + hd
                c_cs = c_lf + (n_g if direction == 0 else 2 * n_g)
                c_u = 3 * n_g + direction * ML_HEADS + hd
                lo, hi = hd * ML_DIM, (hd + 1) * ML_DIM
                q = q_ref[bi, :, lo:hi]
                k = k_ref[bi, :, lo:hi]
                v_t = vt_ref[bi, 0, hd * VT_ROWS:(hd + 1) * VT_ROWS, :]
                u_row = gr[c_u:c_u + 1, :]
                lf_row = gr[c_lf:c_lf + 1, :]
                cs_row = gr[c_cs:c_cs + 1, :]
                m_st = m_refs[chain][:, 0:1]
                st_t = state_refs[chain][...]

                u_vis = jnp.where(visible, jnp.broadcast_to(gc[:, c_u:c_u + 1], visible.shape), -jnp.inf)
                v_row = jnp.maximum(m_st, jnp.max(u_vis, axis=0, keepdims=True))
                d_t = jnp.exp(u_vis - v_row)
                inter = jnp.exp(m_st - v_row)
                sc_t = (_dot_nt(k, q) * d_t).astype(BF16)
                tot_t = inter * _dot_nt(st_t.astype(BF16), q) + _dot(v_t, sc_t)
                den = jnp.maximum(jnp.abs(tot_t[ML_DIM:ML_DIM + 1, :]), jnp.exp(-(cs_row + v_row)))
                h_ref[bi, :, lo:hi] = (tot_t[:ML_DIM, :] / den).T

                b_last = jnp.sum(lf_row, axis=1, keepdims=True)
                m_sc = jnp.maximum(m_st, jnp.max(u_row, axis=1, keepdims=True))
                wgt = jnp.exp(u_row - m_sc)
                vw_t = (v_t.astype(F32) * wgt).astype(BF16)
                state_refs[chain][...] = jnp.exp(m_st - m_sc) * st_t + _dot(vw_t, k)
                m_refs[chain][...] = jnp.broadcast_to(b_last + m_sc, (1, LANES))


def _mlstm(mq, mk, mvt, gc, gr, n_ctx):
    b, t, w = mq.shape
    nc = t // ML_CHUNK
    ncc = n_ctx // ML_CHUNK
    fwd = lambda s: s
    bwd = lambda s: jnp.where(s < ncc, ncc - 1 - s, nc - 1 - s + ncc)
    tok = lambda f: pl.BlockSpec((b, ML_CHUNK, w), lambda s: (0, f(s), 0))
    gcs = lambda f: pl.BlockSpec((b, ML_CHUNK, LANES), lambda s: (0, f(s), 0))
    grs = lambda f: pl.BlockSpec((b, 1, ML_CHUNK, LANES), lambda s: (0, f(s), 0, 0))
    vts = lambda f: pl.BlockSpec((b, 1, ML_HEADS * VT_ROWS, ML_CHUNK), lambda s: (0, f(s), 0, 0))
    n_chain = b * 2 * ML_HEADS
    return pl.pallas_call(
        _mlstm_kernel,
        grid=(nc,),
        in_specs=[tok(fwd), tok(fwd), vts(fwd), gcs(fwd), grs(fwd),
                  tok(bwd), tok(bwd), vts(bwd), gcs(bwd), grs(bwd)],
        out_specs=[tok(fwd), tok(bwd)],
        out_shape=[jax.ShapeDtypeStruct((b, t, w), F32)] * 2,
        scratch_shapes=[pltpu.VMEM((VT_ROWS, ML_DIM), F32)] * n_chain
                       + [pltpu.VMEM((1, LANES), F32)] * n_chain,
        compiler_params=_params("arbitrary"),
        name="mlstm",
    )(mq, mk, mvt, gc, gr, mq, mk, mvt, gc, gr)


def _attn_kernel(q_ref, qn_ref, k_ref, v_ref, lam_ref, sg_ref, o_ref, qm_scr, *scr,
                 ctx_tiles, n_ctx, key_block, n_blocks, lam_init):
    s_scr = (scr[0:2], scr[2:4])
    bm_scr = (scr[4:6], scr[6:8])
    m_scr = scr[8:10]
    acc_scr = scr[10:12]
    i = pl.program_id(2)
    dv = DA_V_DIM
    q = q_ref[0]
    lane = lax.broadcasted_iota(jnp.int32, q.shape, 1)
    zero = jnp.zeros_like(q)

    def map_halves(qv):
        return jnp.where(lane < DA_QK_DIM, qv, zero), jnp.where(lane >= DA_QK_DIM, qv, zero)

    qm_scr[0], qm_scr[1] = map_halves(q)
    align = math.gcd(n_ctx, key_block)

    def scores(mp, start, size, q_maps=None):
        q_map = qm_scr[mp] if q_maps is None else q_maps[mp]
        return _dot_nt(q_map, k_ref[0, pl.ds(start, size), :])

    def lane_max(sc):
        m = sc[:, 0:LANES]
        for t in range(1, sc.shape[1] // LANES):
            m = jnp.maximum(m, sc[:, t * LANES:(t + 1) * LANES])
        return m

    def row_max(bm):
        return jnp.broadcast_to(jnp.max(bm, axis=1, keepdims=True), bm.shape)

    def weights(sc, m):
        return jnp.concatenate([jnp.exp2(sc[:, t * LANES:(t + 1) * LANES] - m)
                                for t in range(sc.shape[1] // LANES)], axis=1).astype(BF16)

    for mp in range(2):
        sc = scores(mp, 0, n_ctx)
        m = row_max(lane_max(sc))
        m_scr[mp][...] = m
        acc_scr[mp][...] = _dot(weights(sc, m), v_ref[0, 0:n_ctx, :])

    def stage(j, slot, q_maps=None):
        start = pl.multiple_of(n_ctx + j * key_block, align)
        for mp in range(2):
            sc = scores(mp, start, key_block, q_maps)
            s_scr[mp][slot][...] = sc
            bm_scr[mp][slot][...] = lane_max(sc)

    def consume(j, slot):
        start = pl.multiple_of(n_ctx + j * key_block, align)
        for mp in range(2):
            m_old = m_scr[mp][...]
            m_new = jnp.maximum(m_old, row_max(bm_scr[mp][slot][...]))
            alpha = jnp.exp2(m_old - m_new)
            pv = _dot(weights(s_scr[mp][slot][...], m_new), v_ref[0, pl.ds(start, key_block), :])
            acc_scr[mp][...] = jnp.concatenate([alpha, alpha], axis=1) * acc_scr[mp][...] + pv
            m_scr[mp][...] = m_new

    @pl.when(i >= ctx_tiles)
    def _():
        @pl.when(i == ctx_tiles)
        def _():
            stage(0, 0)

        def body(g, carry):
            stage(2 * g + 1, 1)
            consume(2 * g, 0)
            stage(2 * g + 2, 0)
            consume(2 * g + 1, 1)
            return carry
        lax.fori_loop(0, n_blocks // 2 - 1, body, 0)
        stage(n_blocks - 1, 1)
        consume(n_blocks - 2, 0)
        stage(0, 0, map_halves(qn_ref[0]))
        consume(n_blocks - 1, 1)

    lv = lam_ref[...]
    dot01 = jnp.sum(lv[0:1, :] * lv[1:2, :], axis=1, keepdims=True)
    dot23 = jnp.sum(lv[2:3, :] * lv[3:4, :], axis=1, keepdims=True)
    lam = jnp.exp(dot01) - jnp.exp(dot23) + lam_init
    a0 = acc_scr[0][...]
    a1 = acc_scr[1][...]
    o = a0[:, 0:dv] / a0[:, dv:dv + 1] - lam * (a1[:, 0:dv] / a1[:, dv:dv + 1])
    o = o * lax.rsqrt(jnp.mean(o * o, axis=-1, keepdims=True) + NORM_EPS) * sg_ref[...]
    o_ref[0] = (o * (1.0 - lam_init)).astype(BF16)


def _diff_attention(q, k, v_aug, lam8, subln_g, n_ctx, lam_init):
    b, t, w = q.shape
    tq = ROW_TILE
    n_lat = t - n_ctx
    key_block = 1024 if n_lat % 2048 == 0 else 512
    n_blocks = n_lat // key_block
    assert n_lat % (2 * key_block) == 0 and n_ctx % tq == 0 and n_ctx % LANES == 0
    kern = functools.partial(_attn_kernel, ctx_tiles=n_ctx // tq, n_ctx=n_ctx, key_block=key_block,
                             n_blocks=n_blocks, lam_init=lam_init)
    return pl.pallas_call(
        kern,
        grid=(b, DA_HEADS, t // tq),
        in_specs=[pl.BlockSpec((1, tq, LANES), lambda bi, h, i: (bi, i, h)),
                  pl.BlockSpec((1, tq, LANES), lambda bi, h, i: (bi, jnp.minimum(i + 1, t // tq - 1), h)),
                  pl.BlockSpec((1, t, LANES), lambda bi, h, i: (bi, 0, h)),
                  pl.BlockSpec((1, t, 2 * DA_V_DIM), lambda bi, h, i: (bi, 0, h)),
                  pl.BlockSpec(lam8.shape, lambda bi, h, i: (0, 0)),
                  pl.BlockSpec(subln_g.shape, lambda bi, h, i: (0, 0))],
        out_specs=pl.BlockSpec((1, tq, LANES), lambda bi, h, i: (bi, i, h)),
        out_shape=jax.ShapeDtypeStruct((b, t, w), BF16),
        scratch_shapes=[pltpu.VMEM((2, tq, LANES), BF16)]
                       + [pltpu.VMEM((tq, key_block), F32)] * 4
                       + [pltpu.VMEM((tq, LANES), F32)] * 6
                       + [pltpu.VMEM((tq, 2 * DA_V_DIM), F32)] * 2,
        compiler_params=_params("arbitrary", "arbitrary", "arbitrary"),
        name="diff_attention",
    )(q, q, k, v_aug, lam8, subln_g)


def _top2_sum(a, b, c, d):
    hi1, lo1 = jnp.maximum(a, b), jnp.minimum(a, b)
    hi2, lo2 = jnp.maximum(c, d), jnp.minimum(c, d)
    return jnp.maximum(hi1, hi2) + jnp.maximum(jnp.minimum(hi1, hi2), jnp.maximum(lo1, lo2))


def _route(f, rwt_ref, rb_ref, cnt_ref):
    tm = f.shape[0]
    aff = _sigmoid(_dot_nt(rwt_ref[...], f, HI))
    biased = aff + rb_ref[:, 0:1]
    bz = [biased[e:e + 1, :] for e in range(N_EXPERTS)]
    af = [aff[e:e + 1, :] for e in range(N_EXPERTS)]
    n_grp = N_EXPERTS // EXPERTS_PER_GROUP
    scores = [_top2_sum(*bz[EXPERTS_PER_GROUP * g:EXPERTS_PER_GROUP * (g + 1)]) for g in range(n_grp)]
    best = scores[0]
    sel_grp = jnp.zeros_like(best, dtype=jnp.int32)
    for g in range(1, n_grp):
        better = scores[g] > best
        sel_grp = jnp.where(better, g, sel_grp)
        best = jnp.where(better, scores[g], best)
    chosen = []
    for e in range(N_EXPERTS):
        g = e // EXPERTS_PER_GROUP
        rank = jnp.zeros_like(sel_grp)
        for o in range(EXPERTS_PER_GROUP * g, EXPERTS_PER_GROUP * (g + 1)):
            if o == e:
                continue
            beats = (bz[o] > bz[e]) if o > e else (bz[o] >= bz[e])
            rank = rank + beats.astype(jnp.int32)
        chosen.append(jnp.logical_and(sel_grp == g, rank < 2))
    denom = sum(jnp.where(chosen[e], af[e], 0.0) for e in range(N_EXPERTS))
    erow = lax.broadcasted_iota(jnp.int32, (N_EXPERTS, tm), 0)
    one_hot = jnp.zeros((N_EXPERTS, tm), F32)
    for e in range(N_EXPERTS):
        one_hot = jnp.where(jnp.logical_and(erow == e, chosen[e]), 1.0, one_hot)
    earlier = (lax.broadcasted_iota(jnp.int32, (tm, tm), 0)
               < lax.broadcasted_iota(jnp.int32, (tm, tm), 1)).astype(BF16)
    rank_all = _dot(one_hot.astype(BF16), earlier) + cnt_ref[:, 0:1]
    cnt_ref[...] = cnt_ref[...] + jnp.sum(one_hot, axis=1, keepdims=True)

    seen = jnp.zeros((1, tm), jnp.bool_)
    e_a = e_b = jnp.zeros((1, tm), jnp.int32)
    r_a = r_b = w_a = w_b = jnp.zeros((1, tm), F32)
    for e in range(N_EXPERTS):
        first = jnp.logical_and(chosen[e], jnp.logical_not(seen))
        second = jnp.logical_and(chosen[e], seen)
        rk = rank_all[e:e + 1, :]
        wt = af[e] / denom
        e_a, e_b = jnp.where(first, e, e_a), jnp.where(second, e, e_b)
        r_a, r_b = jnp.where(first, rk, r_a), jnp.where(second, rk, r_b)
        w_a, w_b = jnp.where(first, wt, w_a), jnp.where(second, wt, w_b)
        seen = jnp.logical_or(seen, chosen[e])
    r8 = lax.broadcasted_iota(jnp.int32, (8, tm), 0)
    sel = jnp.where(r8 == 0, e_a, jnp.where(r8 == 1, e_b, jnp.where(
        r8 == 2, r_a.astype(jnp.int32), jnp.where(r8 == 3, r_b.astype(jnp.int32), 0))))
    row = lax.broadcasted_iota(jnp.int32, (LANES, tm), 0)
    w_t = jnp.where(row == 0, w_a, jnp.where(row == 1, w_b, 0.0))
    return sel, w_t.T


def _post_kernel(*refs, even):
    if even:
        (x_ref, a_ref, hf_ref, hb_ref, og_ref, mod_ref, wa_ref, wm_ref, gf_ref, rwt_ref, rb_ref,
         xo_ref, f_ref, sel_ref, w_ref, cnt_ref, cnt_scr) = refs
        m = ((hf_ref[0] + hb_ref[0]) * og_ref[0].astype(F32)).astype(BF16)
        o = _dot(a_ref[0], wa_ref[...]) + _dot(m, wm_ref[...])
    else:
        (x_ref, a_ref, mod_ref, wa_ref, gf_ref, rwt_ref, rb_ref,
         xo_ref, f_ref, sel_ref, w_ref, cnt_ref, cnt_scr) = refs
        o = _dot(a_ref[0].astype(BF16), wa_ref[...])

    @pl.when(jnp.logical_and(pl.program_id(0) == 0, pl.program_id(1) == 0))
    def _():
        cnt_scr[...] = jnp.zeros(cnt_scr.shape, F32)

    x = x_ref[0] + mod_ref[0, 2:3, :] * o
    xo_ref[0] = x
    f = _norm_mod(x, gf_ref[...], mod_ref[0, 3:4, :], mod_ref[0, 4:5, :])
    _store_row_tiles(f_ref.at[0], f)
    sel, w_col = _route(f, rwt_ref, rb_ref, cnt_scr)
    sel_ref[0, 0] = sel
    w_ref[0] = w_col
    cnt_ref[...] = cnt_scr[...]


def _post_mixer(x, acts, weights, modtab, gffn, rwt, rb, n_ctx, x_row_off, even):
    b, t, _ = acts[0].shape
    d = x.shape[2]
    tm = ROW_TILE
    nt = t // tm
    ctx_tiles = n_ctx // tm
    off = x_row_off // tm
    full = lambda a: pl.BlockSpec(a.shape, lambda bi, i: (0,) * a.ndim)
    row = lambda w: pl.BlockSpec((1, tm, w), lambda bi, i: (bi, i, 0))
    mod_spec = pl.BlockSpec(
        (1, 8, d), lambda bi, i: (2 * bi + (i + off >= ctx_tiles).astype(jnp.int32), 0, 0))
    in_specs = ([pl.BlockSpec((1, tm, d), lambda bi, i: (bi, i + off, 0))]
                + [row(a.shape[2]) for a in acts] + [mod_spec]
                + [full(w) for w in weights] + [full(gffn), full(rwt), full(rb)])
    return pl.pallas_call(
        functools.partial(_post_kernel, even=even),
        grid=(b, nt),
        in_specs=in_specs,
        out_specs=[row(d), pl.BlockSpec((1, tm * d // LANES, LANES), lambda bi, i: (bi, i, 0)),
                   pl.BlockSpec((1, 1, 8, tm), lambda bi, i: (bi, i, 0, 0)),
                   row(LANES),
                   pl.BlockSpec((N_EXPERTS, LANES), lambda bi, i: (0, 0))],
        out_shape=[jax.ShapeDtypeStruct((b, t, d), F32),
                   jax.ShapeDtypeStruct((b, t * d // LANES, LANES), F32),
                   jax.ShapeDtypeStruct((b, nt, 8, tm), jnp.int32),
                   jax.ShapeDtypeStruct((b, t, LANES), F32),
                   jax.ShapeDtypeStruct((N_EXPERTS, LANES), F32)],
        scratch_shapes=[pltpu.VMEM((N_EXPERTS, LANES), F32)],
        compiler_params=_params("arbitrary", "arbitrary"),
        name="post_mixer_even" if even else "post_mixer_odd",
    )(x, *acts, modtab, *weights, gffn, rwt, rb)


EXPERT_ROW_TILE = 256
DMA_ISSUE_UNROLL = 8


def _moe_plan(sel, counts, n_tok):
    e_a, e_b, r_a, r_b = (sel[:, :, k, :].reshape(-1) for k in range(4))
    cnt = counts[:, 0].astype(jnp.int32)
    padded = ((cnt + EXPERT_ROW_TILE - 1) // EXPERT_ROW_TILE) * EXPERT_ROW_TILE
    ends = jnp.cumsum(padded)
    starts = ends - padded
    pos = jnp.concatenate([starts[e_a] + r_a, starts[e_b] + r_b]).astype(jnp.int32)
    n_tiles = 2 * n_tok // EXPERT_ROW_TILE + N_EXPERTS
    tile_start = jnp.arange(n_tiles, dtype=jnp.int32) * EXPERT_ROW_TILE
    tile_expert = jnp.minimum(jnp.sum(tile_start[:, None] >= ends[None, :], axis=1),
                              N_EXPERTS - 1).astype(jnp.int32)
    tiles_used = (ends[-1:] // EXPERT_ROW_TILE).astype(jnp.int32)
    return pos, tile_expert, tiles_used, n_tiles


def _store_row_tiles(ref, x):
    rows, d = x.shape
    n_sub = d // LANES
    for s in range(n_sub):
        ref[pl.ds(s, rows, stride=n_sub), :] = x[:, s * LANES:(s + 1) * LANES]


def _load_row_tiles(ref, rows):
    n_sub = ref.shape[0] // rows
    return jnp.concatenate([ref[pl.ds(s, rows, stride=n_sub), :] for s in range(n_sub)], axis=1)


def _dispatch_kernel(pos_ref, f_ref, init_ref, out_ref, sem, *, n_tok, tm):
    del init_ref
    n_sub = f_ref.shape[1] // tm
    base = (pl.program_id(0) * pl.num_programs(1) + pl.program_id(1)) * tm

    def row_copy(r, k):
        dst = pos_ref[k * n_tok + base + r]
        src = f_ref.at[0, pl.ds(pl.multiple_of(r * n_sub, n_sub), n_sub), :]
        return pltpu.make_async_copy(src, out_ref.at[dst], sem)

    def issue(r, carry):
        row_copy(r, 0).start()
        row_copy(r, 1).start(priority=1)
        return carry

    lax.fori_loop(0, tm, issue, 0, unroll=DMA_ISSUE_UNROLL)
    for _ in range(2):
        pltpu.make_async_copy(out_ref.at[pl.ds(0, tm)], out_ref.at[pl.ds(0, tm)], sem).wait()


def _dispatch(pos, f, n_rows, n_tok):
    b, rows, _ = f.shape
    n_sub = rows * b // n_tok
    tm = ROW_TILE
    grid_spec = pltpu.PrefetchScalarGridSpec(
        num_scalar_prefetch=1,
        grid=(b, n_tok // b // tm),
        in_specs=[pl.BlockSpec((1, tm * n_sub, LANES), lambda bi, i, pos_ref: (bi, i, 0)),
                  pl.BlockSpec(memory_space=pl.ANY)],
        out_specs=pl.BlockSpec(memory_space=pl.ANY),
        scratch_shapes=[pltpu.SemaphoreType.DMA(())])
    return pl.pallas_call(
        functools.partial(_dispatch_kernel, n_tok=n_tok, tm=tm),
        grid_spec=grid_spec,
        out_shape=jax.ShapeDtypeStruct((n_rows, n_sub, LANES), F32),
        input_output_aliases={2: 0},
        compiler_params=_params("arbitrary", "arbitrary"),
        name="moe_dispatch",
    )(pos, f, jnp.zeros((n_rows, n_sub, LANES), F32))


def _expert_ffn_kernel(te_ref, used_ref, x_ref, wg_ref, wu_ref, wd_ref, y_ref):
    del te_ref
    live = pl.program_id(0) < used_ref[0]

    @pl.when(live)
    def _():
        xb = _load_row_tiles(x_ref, EXPERT_ROW_TILE).astype(BF16)
        he = _silu(_dot(xb, wg_ref[0])) * _dot(xb, wu_ref[0])
        _store_row_tiles(y_ref, _dot(he.astype(BF16), wd_ref[0]))

    @pl.when(jnp.logical_not(live))
    def _():
        y_ref[...] = jnp.zeros(y_ref.shape, F32)


def _expert_ffn(tile_expert, tiles_used, xs, wg, wu, wd, n_tiles):
    n_rows, n_sub, _ = xs.shape
    _, d, d_e = wg.shape
    tr = EXPERT_ROW_TILE
    grid_spec = pltpu.PrefetchScalarGridSpec(
        num_scalar_prefetch=2,
        grid=(n_tiles,),
        in_specs=[pl.BlockSpec((tr * n_sub, LANES), lambda j, te, used: (j, 0)),
                  pl.BlockSpec((1, d, d_e), lambda j, te, used: (te[j], 0, 0)),
                  pl.BlockSpec((1, d, d_e), lambda j, te, used: (te[j], 0, 0)),
                  pl.BlockSpec((1, d_e, d), lambda j, te, used: (te[j], 0, 0))],
        out_specs=pl.BlockSpec((tr * n_sub, LANES), lambda j, te, used: (j, 0)))
    ys = pl.pallas_call(
        _expert_ffn_kernel,
        grid_spec=grid_spec,
        out_shape=jax.ShapeDtypeStruct((n_rows * n_sub, LANES), F32),
        compiler_params=_params("arbitrary"),
        name="moe_expert_ffn",
    )(tile_expert, tiles_used, xs.reshape(n_rows * n_sub, LANES), wg, wu, wd)
    return ys.reshape(n_rows, n_sub, LANES)


def _combine_kernel(pos_ref, y_ref, w_ref, x_ref, modc_ref, modl_ref, fg_ref, o_ref, buf, sem,
                    *, n_tok, n_ctx, final_norm):
    i = pl.program_id(1)
    tm = x_ref.shape[1]
    base = (pl.program_id(0) * pl.num_programs(1) + i) * tm

    n_sub = y_ref.shape[1]

    def row_copy(r, k):
        src = pos_ref[k * n_tok + base + r]
        dst = buf.at[k, pl.ds(pl.multiple_of(r * n_sub, n_sub), n_sub), :]
        return pltpu.make_async_copy(y_ref.at[src], dst, sem)

    def issue(r, carry):
        row_copy(r, 0).start()
        row_copy(r, 1).start(priority=1)
        return carry

    lax.fori_loop(0, tm, issue, 0, unroll=DMA_ISSUE_UNROLL)
    for _ in range(2):
        pltpu.make_async_copy(y_ref.at[pl.ds(0, tm)], y_ref.at[pl.ds(0, tm)], sem).wait()
    w = w_ref[0]
    y = _load_row_tiles(buf.at[0], tm) * w[:, 0:1] + _load_row_tiles(buf.at[1], tm) * w[:, 1:2]
    rows = i * tm + lax.broadcasted_iota(jnp.int32, (tm, 1), 0)
    gate = jnp.where(rows < n_ctx, modc_ref[0, 5:6, :], modl_ref[0, 5:6, :])
    out = x_ref[0] + gate * y
    if final_norm:
        out = out * lax.rsqrt(jnp.mean(out * out, axis=-1, keepdims=True) + NORM_EPS) * fg_ref[...]
    o_ref[0] = out


def _combine(pos, ys, w_col, xmid, modtab, final_g, n_ctx, final_norm):
    b, t, d = xmid.shape
    tm = ROW_TILE
    row = lambda w: pl.BlockSpec((1, tm, w), lambda bi, i, pos_ref: (bi, i, 0))
    grid_spec = pltpu.PrefetchScalarGridSpec(
        num_scalar_prefetch=1,
        grid=(b, t // tm),
        in_specs=[pl.BlockSpec(memory_space=pl.ANY), row(LANES), row(d),
                  pl.BlockSpec((1, 8, d), lambda bi, i, pos_ref: (2 * bi, 0, 0)),
                  pl.BlockSpec((1, 8, d), lambda bi, i, pos_ref: (2 * bi + 1, 0, 0)),
                  pl.BlockSpec(final_g.shape, lambda bi, i, pos_ref: (0, 0))],
        out_specs=row(d),
        scratch_shapes=[pltpu.VMEM((2, tm * d // LANES, LANES), F32), pltpu.SemaphoreType.DMA(())])
    return pl.pallas_call(
        functools.partial(_combine_kernel, n_tok=b * t, n_ctx=n_ctx, final_norm=final_norm),
        grid_spec=grid_spec,
        out_shape=jax.ShapeDtypeStruct((b, t, d), F32),
        compiler_params=_params("arbitrary", "arbitrary"),
        name="moe_combine",
    )(pos, ys, w_col, xmid, modtab, modtab, final_g)


def _moe(f, sel, w_col, counts, wg, wu, wd, xmid, modtab, final_g, n_ctx, final_norm):
    b, t, _ = xmid.shape
    pos, tile_expert, tiles_used, n_tiles = _moe_plan(sel, counts, b * t)
    xs = _dispatch(pos, f, n_tiles * EXPERT_ROW_TILE, b * t)
    ys = _expert_ffn(tile_expert, tiles_used, xs, wg, wu, wd, n_tiles)
    return _combine(pos, ys, w_col, xmid, modtab, final_g, n_ctx, final_norm)


def _chan_dft_kernel(x_ref, mod_ref, g_ref, w_ref, zr_ref, zi_ref):
    h = _norm_mod(x_ref[0], g_ref[...], mod_ref[0, 0:1, :], mod_ref[0, 1:2, :])
    gd = w_ref.shape[0]
    for gi in range(h.shape[1] // gd):
        z = _dot_split(h[:, gi * gd:(gi + 1) * gd], w_ref[...])
        zr_ref[0, :, gi * gd:(gi + 1) * gd] = z[:, :gd]
        zi_ref[0, :, gi * gd:(gi + 1) * gd] = z[:, gd:]


def _chan_dft(x, modtab, g, w_cs, x_row_off, t):
    b, _, d = x.shape
    tm = ROW_TILE
    off = x_row_off // tm
    row = pl.BlockSpec((1, tm, d), lambda bi, i: (bi, i, 0))
    return pl.pallas_call(
        _chan_dft_kernel,
        grid=(b, t // tm),
        in_specs=[pl.BlockSpec((1, tm, d), lambda bi, i: (bi, i + off, 0)),
                  pl.BlockSpec((1, 8, d), lambda bi, i: (2 * bi + 1, 0, 0)),
                  pl.BlockSpec(g.shape, lambda bi, i: (0, 0)),
                  pl.BlockSpec(w_cs.shape, lambda bi, i: (0, 0))],
        out_specs=[row, row],
        out_shape=[jax.ShapeDtypeStruct((b, t, d), F32)] * 2,
        compiler_params=_params("parallel", "parallel"),
        name="chan_dft",
    )(x, modtab, g, w_cs)


def _dft1_kernel(zr_ref, zi_ref, w_ref, yr_ref, yi_ref):
    n1 = zr_ref.shape[1]
    y = _dot_split(w_ref[...], jnp.concatenate([zr_ref[0], zi_ref[0]], axis=0))
    yr_ref[0] = y[:n1]
    yi_ref[0] = y[n1:]


def _dft1(zr, zi, w1):
    b, n1, cols = zr.shape
    tn = min(cols, 4096)
    blk = pl.BlockSpec((1, n1, tn), lambda bi, j: (bi, 0, j))
    return pl.pallas_call(
        _dft1_kernel,
        grid=(b, cols // tn),
        in_specs=[blk, blk, pl.BlockSpec(w1.shape, lambda bi, j: (0, 0))],
        out_specs=[blk, blk],
        out_shape=[jax.ShapeDtypeStruct(zr.shape, F32)] * 2,
        compiler_params=_params("parallel", "parallel"),
        name="dft_stage1",
    )(zr, zi, w1)


def _dft2_kernel(yr_ref, yi_ref, tab_ref, o_ref):
    y = jnp.concatenate([yr_ref[0, 0], yi_ref[0, 0]], axis=0)
    o_ref[0, 0] = _dot_split(tab_ref[0], y)


def _dft2(yr, yi, tab):
    b, n1, n2, d = yr.shape
    blk = pl.BlockSpec((1, 1, n2, d), lambda bi, k1: (bi, k1, 0, 0))
    return pl.pallas_call(
        _dft2_kernel,
        grid=(b, n1),
        in_specs=[blk, blk, pl.BlockSpec((1, n2, 2 * n2), lambda bi, k1: (k1, 0, 0))],
        out_specs=blk,
        out_shape=jax.ShapeDtypeStruct(yr.shape, F32),
        compiler_params=_params("parallel", "parallel"),
        name="dft_stage2",
    )(yr, yi, tab)


def _dft_tables(t, gd):
    n2 = ML_CHUNK
    n1 = t // n2
    def cs(num, den):
        ang = (2.0 * np.pi / den) * (num % den).astype(np.float64)
        return np.cos(ang), np.sin(ang)
    c = np.arange(gd)
    cc, sc = cs(np.outer(c, c), gd)
    w_cs = np.concatenate([cc, -sc], axis=1)
    a = np.arange(n1)
    c1, s1 = cs(np.outer(a, a), n1)
    w1 = np.block([[c1, s1], [-s1, c1]])
    k = a[:, None, None] + n1 * np.arange(n2)[None, :, None]
    c2, s2 = cs(k * np.arange(n2)[None, None, :], t)
    tab = np.concatenate([c2, s2], axis=2) / math.sqrt(t * gd)
    return (jnp.asarray(w_cs, F32), jnp.asarray(w1, F32), jnp.asarray(tab, F32))


def _rope_tables(n_ctx, n_lat):
    pos = jnp.arange(n_lat, dtype=jnp.int32)
    n_axis = DA_QK_DIM // 4
    inv = ROPE_BASE ** (-jnp.arange(n_axis, dtype=F32) / n_axis)
    ang = jnp.concatenate([(pos // GRID_W).astype(F32)[:, None] * inv,
                           (pos % GRID_W).astype(F32)[:, None] * inv], axis=-1)
    cos, sin = jnp.cos(ang), jnp.sin(ang)
    cos = jnp.concatenate([jnp.ones((n_ctx, 2 * n_axis), F32), cos], axis=0)
    sin = jnp.concatenate([jnp.zeros((n_ctx, 2 * n_axis), F32), sin], axis=0)
    cos128 = jnp.concatenate([cos, cos, cos, cos], axis=1)
    sin128 = jnp.concatenate([-sin, sin, -sin, sin], axis=1)
    return cos128, sin128


def _deinterleave(w):
    d, n = w.shape
    w = w.reshape(d, n // DA_QK_DIM, DA_QK_DIM // 2, 2)
    return jnp.concatenate([w[..., 0], w[..., 1]], axis=-1).reshape(d, n)


def _pad_rows(a, rows):
    return jnp.concatenate([a, jnp.zeros((rows - a.shape[0],) + a.shape[1:], a.dtype)], axis=0)


def _pad_cols(a, cols):
    return jnp.concatenate([a, jnp.zeros(a.shape[:-1] + (cols - a.shape[-1],), a.dtype)], axis=-1)


def kernel(x, c, ctx, c_ctx, ada_w, ada_b, norm_mix_g, norm_ffn_g, even_w_in, even_w_out,
           even_conv_w, even_gate_b, even_lam, even_subln_g, odd_w_fnet, router_w, router_b,
           exp_w_gate, exp_w_up, exp_w_down, final_g):
    b, n_lat, d = x.shape
    n_ctx = ctx.shape[1]
    depth = ada_w.shape[0]
    assert depth == 2 and b + 1 <= 8
    assert n_ctx % ROW_TILE == 0 and n_lat % ROW_TILE == 0

    cond8 = _pad_rows(jnp.concatenate([c_ctx[None, :], c], axis=0), 8)
    rwt = router_w.T
    rb = jnp.broadcast_to(router_b[:, None], (N_EXPERTS, LANES))
    row2 = lambda v: v.reshape(1, -1)

    def modtab_for(layer):
        mods = _ada_mods(cond8, ada_w[layer], ada_b[layer]).reshape(8, 6, d)
        mods = jnp.concatenate([mods, jnp.zeros((8, 2, d), F32)], axis=1)
        idx = np.array([[0, 1 + bi] for bi in range(b)]).reshape(-1)
        return mods[idx]

    xs = jnp.concatenate([ctx, x], axis=1)
    modtab = modtab_for(0)
    w_in = even_w_in[0]
    o1 = DA_HEADS * 2 * DA_QK_DIM
    o2 = 2 * o1
    o3 = o2 + DA_HEADS * DA_V_DIM
    o4 = o3 + 2 * ML_HEADS * ML_DIM
    o5 = o4 + ML_HEADS * ML_DIM
    o6 = o5 + ML_HEADS * ML_DIM
    ws = [(_deinterleave(w_in[:, :o1]) * (DA_QK_DIM ** -0.5)).astype(BF16),
          _deinterleave(w_in[:, o1:o2]).astype(BF16),
          w_in[:, o2:o3].astype(BF16), w_in[:, o3:o4].astype(BF16),
          w_in[:, o4:o5].astype(BF16), w_in[:, o5:o6].astype(BF16),
          _pad_cols(w_in[:, o6:], LANES).astype(BF16)]
    gate_b = _pad_cols(even_gate_b[0].reshape(1, -1), LANES)
    cos128, sin128 = _rope_tables(n_ctx, n_lat)
    daq, dak, dav, mqk, mv, og, gates = _inproj(xs, modtab, row2(norm_mix_g[0]), cos128, sin128,
                                                 ws, gate_b, n_ctx)
    mq, mk, gc, gr, mvt = _mlprep(mqk, _pad_rows(even_conv_w[0], 8), gates, mv, n_ctx)
    hf, hb = _mlstm(mq, mk, mvt, gc, gr, n_ctx)
    lam_init = 0.8 - 0.6 * math.exp(-0.3 * 0)
    lam8 = _pad_rows(even_lam[0], 8)
    att = _diff_attention(daq, dak, dav, lam8, row2(even_subln_g[0]), n_ctx, lam_init)
    w_out = even_w_out[0].astype(BF16)
    half = DA_HEADS * DA_V_DIM
    xmid, f, sel, w_col, counts = _post_mixer(xs, [att, hf, hb, og], [w_out[:half], w_out[half:]],
                                              modtab, row2(norm_ffn_g[0]), rwt, rb, n_ctx, 0, True)
    xs = _moe(f, sel, w_col, counts, exp_w_gate[0].astype(BF16), exp_w_up[0].astype(BF16),
              exp_w_down[0].astype(BF16), xmid, modtab, row2(final_g), n_ctx, False)

    modtab = modtab_for(1)
    gd = d // FN_GROUPS
    w_cs, w1, tab = _dft_tables(n_lat, gd)
    n2 = ML_CHUNK
    n1 = n_lat // n2
    zr, zi = _chan_dft(xs, modtab, row2(norm_mix_g[1]), w_cs, n_ctx, n_lat)
    yr, yi = _dft1(zr.reshape(b, n1, n2 * d), zi.reshape(b, n1, n2 * d), w1)
    fo = _dft2(yr.reshape(b, n1, n2, d), yi.reshape(b, n1, n2, d), tab)
    fo = fo.transpose(0, 2, 1, 3).reshape(b, n_lat, d)
    xmid, f, sel, w_col, counts = _post_mixer(xs, [fo], [odd_w_fnet[0].astype(BF16)], modtab,
                                              row2(norm_ffn_g[1]), rwt, rb, 0, n_ctx, False)
    return _moe(f, sel, w_col, counts, exp_w_gate[1].astype(BF16), exp_w_up[1].astype(BF16),
                exp_w_down[1].astype(BF16), xmid, modtab, row2(final_g), 0, True)
```

```python
import functools
import math

import jax
import jax.numpy as jnp
import numpy as np
from jax import lax
from jax.experimental import pallas as pl
from jax.experimental.pallas import tpu as pltpu

F32 = jnp.float32
BF16 = jnp.bfloat16

NORM_EPS = 1e-6
GRID_W = 64
DA_HEADS = 4
DA_QK_DIM = 64
DA_V_DIM = 128
ML_HEADS = 4
ML_DIM = 128
ML_CHUNK = 128
FN_GROUPS = 4
N_EXPERTS = 16
EXPERTS_PER_GROUP = 4
ROPE_BASE = 10000.0
LANES = 128
ROW_TILE = 256
VMEM_LIMIT_BYTES = 56 * 1024 * 1024
HI = lax.Precision.HIGHEST
LOG2_E = math.log2(math.e)


def _params(*sem):
    return pltpu.CompilerParams(dimension_semantics=sem, vmem_limit_bytes=VMEM_LIMIT_BYTES)


def _dot(a, b, precision=None):
    return jnp.dot(a, b, preferred_element_type=F32, precision=precision)


def _dot_nt(a, b, precision=None):
    return lax.dot_general(a, b, (((1,), (1,)), ((), ())), preferred_element_type=F32,
                           precision=precision)


def _dot_split(a, b):
    a_hi = a.astype(BF16)
    b_hi = b.astype(BF16)
    a_lo = (a - a_hi.astype(F32)).astype(BF16)
    b_lo = (b - b_hi.astype(F32)).astype(BF16)
    return _dot(a_hi, b_hi) + (_dot(a_hi, b_lo) + _dot(a_lo, b_hi))


def _sigmoid(x):
    return 1.0 / (1.0 + jnp.exp(-x))


def _silu(x):
    return x * _sigmoid(x)


def _norm_mod(x, g, shift, scale):
    y = x * lax.rsqrt(jnp.mean(x * x, axis=-1, keepdims=True) + NORM_EPS) * g
    return y * (1.0 + scale) + shift


def _ada_kernel(c_ref, w_ref, b_ref, o_ref):
    o_ref[...] = _dot(_silu(c_ref[...]), w_ref[...], HI) + b_ref[...]


def _ada_mods(cond8, w, b):
    d, n = w.shape
    tn = n // 6
    return pl.pallas_call(
        _ada_kernel,
        grid=(6,),
        in_specs=[pl.BlockSpec((8, d), lambda j: (0, 0)),
                  pl.BlockSpec((d, tn), lambda j: (0, j)),
                  pl.BlockSpec((1, tn), lambda j: (0, j))],
        out_specs=pl.BlockSpec((8, tn), lambda j: (0, j)),
        out_shape=jax.ShapeDtypeStruct((8, n), F32),
        compiler_params=_params("arbitrary"),
        name="ada_mods",
    )(cond8, w, b.reshape(1, n))


def _inproj_kernel(x_ref, mod_ref, g_ref, cos_ref, sin_ref, wq_ref, wk_ref, wv_ref, wmqk_ref,
                   wmv_ref, wmo_ref, wg_ref, gb_ref,
                   q_ref, k_ref, v_ref, mqk_ref, mv_ref, og_ref, gate_ref):
    x = x_ref[0]
    h = _norm_mod(x, g_ref[...], mod_ref[0, 0:1, :], mod_ref[0, 1:2, :]).astype(BF16)
    tm = x.shape[0]
    width = q_ref.shape[2]
    cos = jnp.concatenate([cos_ref[...]] * (width // LANES), axis=1)
    sin = jnp.concatenate([sin_ref[...]] * (width // LANES), axis=1)
    lane = lax.broadcasted_iota(jnp.int32, (tm, width), 1)
    lower = (lane & (DA_QK_DIM - 1)) < (DA_QK_DIM // 2)

    def rope(u):
        swapped = jnp.where(lower, pltpu.roll(u, width - DA_QK_DIM // 2, 1),
                            pltpu.roll(u, DA_QK_DIM // 2, 1))
        return u * cos + swapped * sin

    q_ref[0] = (rope(_dot(h, wq_ref[...])) * LOG2_E).astype(BF16)
    k_ref[0] = rope(_dot(h, wk_ref[...])).astype(BF16)
    v = _dot(h, wv_ref[...])
    pad = VT_ROWS - DA_V_DIM
    ones_row = (lax.broadcasted_iota(jnp.int32, (pad, tm), 0) == 0).astype(BF16)
    for hd in range(DA_HEADS):
        v_ref[0, hd, 0, 0:DA_V_DIM, :] = v[:, hd * DA_V_DIM:(hd + 1) * DA_V_DIM].T.astype(BF16)
        v_ref[0, hd, 0, DA_V_DIM:VT_ROWS, :] = ones_row
    mqk_ref[0] = _dot(h, wmqk_ref[...]).astype(BF16)
    mv_ref[0] = _dot(h, wmv_ref[...]).astype(BF16)
    og_ref[0] = _sigmoid(_dot(h, wmo_ref[...])).astype(BF16)
    g = _dot(h, wg_ref[...]) + gb_ref[...]
    glane = lax.broadcasted_iota(jnp.int32, g.shape, 1)
    is_forget = ((glane // ML_HEADS) & 1) == 1
    log_sig = jnp.minimum(g, 0.0) - jnp.log(1.0 + jnp.exp(-jnp.abs(g)))
    gate_ref[0] = jnp.where(is_forget, log_sig, g)


def _inproj(xs, modtab, g, cos, sin, ws, gate_b, n_ctx):
    b, t, d = xs.shape
    tm = ROW_TILE
    nt = t // tm
    ctx_tiles = n_ctx // tm
    row = lambda w: pl.BlockSpec((1, tm, w), lambda bi, i: (bi, i, 0))
    full = lambda a: pl.BlockSpec(a.shape, lambda bi, i: (0,) * a.ndim)
    widths = [w.shape[1] for w in ws]
    out_dtypes = [BF16] * 6 + [F32]
    out_specs = [row(w) for w in widths]
    out_shape = [jax.ShapeDtypeStruct((b, t, w), dt) for w, dt in zip(widths, out_dtypes)]
    out_specs[2] = pl.BlockSpec((1, DA_HEADS, 1, VT_ROWS, tm), lambda bi, i: (bi, 0, i, 0, 0))
    out_shape[2] = jax.ShapeDtypeStruct((b, DA_HEADS, nt, VT_ROWS, tm), BF16)
    return pl.pallas_call(
        _inproj_kernel,
        grid=(b, nt),
        in_specs=[row(d),
                  pl.BlockSpec((1, 8, d), lambda bi, i: (2 * bi + (i >= ctx_tiles).astype(jnp.int32), 0, 0)),
                  full(g),
                  pl.BlockSpec((tm, LANES), lambda bi, i: (i, 0)),
                  pl.BlockSpec((tm, LANES), lambda bi, i: (i, 0))]
                 + [full(w) for w in ws] + [full(gate_b)],
        out_specs=out_specs,
        out_shape=out_shape,
        compiler_params=_params("parallel", "parallel"),
        name="inproj",
    )(xs, modtab, g, cos, sin, *ws, gate_b)


def _split3(x):
    x1 = x.astype(BF16)
    r1 = x - x1.astype(F32)
    x2 = r1.astype(BF16)
    x3 = (r1 - x2.astype(F32)).astype(BF16)
    return x1, x2, x3


VT_ROWS = ML_DIM + 16


def _mlprep_kernel(cur_ref, prev_ref, next_ref, cw_ref, gate_ref, v_ref,
                   mq_ref, mk_ref, gc_ref, gr_ref, vt_ref, *, ctx_tiles, n_tiles):
    i = pl.program_id(1)
    cur = cur_ref[0].astype(F32)
    tm, w = cur.shape
    prev_ok = i != ctx_tiles
    if ctx_tiles > 0:
        prev_ok = jnp.logical_and(prev_ok, i != 0)
        next_ok = jnp.logical_and(i != ctx_tiles - 1, i != n_tiles - 1)
    else:
        next_ok = i != n_tiles - 1
    prev_row = jnp.where(prev_ok, prev_ref[0, 7:8, :].astype(F32), 0.0)
    next_row = jnp.where(next_ok, next_ref[0, 0:1, :].astype(F32), 0.0)
    ridx = lax.broadcasted_iota(jnp.int32, (tm, w), 0)
    before = jnp.where(ridx == 0, prev_row, pltpu.roll(cur, 1, 0))
    after = jnp.where(ridx == tm - 1, next_row, pltpu.roll(cur, tm - 1, 0))
    y = _silu(before * cw_ref[0:1, :] + cur * cw_ref[1:2, :] + after * cw_ref[2:3, :])
    half = w // 2
    mq_ref[0] = y[:, :half].astype(BF16)
    mk_ref[0] = (y[:, half:] * (ML_DIM ** -0.5)).astype(BF16)

    n_g = 4 * ML_HEADS
    hds = ML_HEADS
    r = lax.broadcasted_iota(jnp.int32, (ML_CHUNK, ML_CHUNK), 0)
    c = lax.broadcasted_iota(jnp.int32, (ML_CHUNK, ML_CHUNK), 1)
    lower = (c <= r).astype(BF16)
    upper = (c >= r).astype(BF16)
    ones_row = (r == 0).astype(BF16)[0:VT_ROWS - ML_DIM]
    for ci in range(tm // ML_CHUNK):
        rows = slice(ci * ML_CHUNK, (ci + 1) * ML_CHUNK)
        gm = jnp.where(c < n_g, gate_ref[0, rows, :], 0.0)
        pre = sum(_dot(lower, p) for p in _split3(pltpu.roll(gm, n_g, 1)))
        suf = sum(_dot(upper, p) for p in _split3(pltpu.roll(gm, 2 * n_g, 1)))
        col = gm + pre + suf
        u_f = pltpu.roll(col, 3 * n_g, 1) - pltpu.roll(col, 3 * n_g - (n_g + hds), 1)
        u_b = pltpu.roll(col, 3 * n_g - hds, 1) - pltpu.roll(col, 2 * hds, 1)
        col = col + jnp.where(jnp.logical_and(c >= 3 * n_g, c < 3 * n_g + hds), u_f,
                              jnp.where(jnp.logical_and(c >= 3 * n_g + hds, c < 3 * n_g + 2 * hds),
                                        u_b, 0.0))
        gc_ref[0, rows, :] = col
        gr_ref[0, ci] = col.T
        for hd in range(ML_HEADS):
            v_t = v_ref[0, rows, hd * ML_DIM:(hd + 1) * ML_DIM].astype(F32).T.astype(BF16)
            vt_ref[0, ci, hd * VT_ROWS:hd * VT_ROWS + ML_DIM, :] = v_t
            vt_ref[0, ci, hd * VT_ROWS + ML_DIM:(hd + 1) * VT_ROWS, :] = ones_row


def _mlprep(mqk, conv_w8, gates, mv, n_ctx):
    b, t, w = mqk.shape
    tm = ROW_TILE
    nt = t // tm
    sub = tm // 8
    nsub = t // 8
    cpt = tm // ML_CHUNK
    kern = functools.partial(_mlprep_kernel, ctx_tiles=n_ctx // tm, n_tiles=nt)
    return pl.pallas_call(
        kern,
        grid=(b, nt),
        in_specs=[pl.BlockSpec((1, tm, w), lambda bi, i: (bi, i, 0)),
                  pl.BlockSpec((1, 8, w), lambda bi, i: (bi, jnp.maximum(i * sub - 1, 0), 0)),
                  pl.BlockSpec((1, 8, w), lambda bi, i: (bi, jnp.minimum((i + 1) * sub, nsub - 1), 0)),
                  pl.BlockSpec(conv_w8.shape, lambda bi, i: (0, 0)),
                  pl.BlockSpec((1, tm, LANES), lambda bi, i: (bi, i, 0)),
                  pl.BlockSpec((1, tm, w // 2), lambda bi, i: (bi, i, 0))],
        out_specs=[pl.BlockSpec((1, tm, w // 2), lambda bi, i: (bi, i, 0)),
                   pl.BlockSpec((1, tm, w // 2), lambda bi, i: (bi, i, 0)),
                   pl.BlockSpec((1, tm, LANES), lambda bi, i: (bi, i, 0)),
                   pl.BlockSpec((1, cpt, ML_CHUNK, LANES), lambda bi, i: (bi, i, 0, 0)),
                   pl.BlockSpec((1, cpt, ML_HEADS * VT_ROWS, ML_CHUNK), lambda bi, i: (bi, i, 0, 0))],
        out_shape=[jax.ShapeDtypeStruct((b, t, w // 2), BF16),
                   jax.ShapeDtypeStruct((b, t, w // 2), BF16),
                   jax.ShapeDtypeStruct((b, t, LANES), F32),
                   jax.ShapeDtypeStruct((b, t // ML_CHUNK, ML_CHUNK, LANES), F32),
                   jax.ShapeDtypeStruct((b, t // ML_CHUNK, ML_HEADS * VT_ROWS, ML_CHUNK), BF16)],
        compiler_params=_params("parallel", "parallel"),
        name="mlstm_prep",
    )(mqk, mqk, mqk, conv_w8, gates, mv)


def _mlstm_kernel(qf_ref, kf_ref, vf_ref, gcf_ref, grf_ref, qb_ref, kb_ref, vb_ref, gcb_ref, grb_ref,
                  hf_ref, hb_ref, *scr):
    s = pl.program_id(0)
    nb = qf_ref.shape[0]
    n_g = 4 * ML_HEADS
    state_refs = scr[:len(scr) // 2]
    m_refs = scr[len(scr) // 2:]

    @pl.when(s == 0)
    def _():
        for ref in scr:
            ref[...] = jnp.zeros(ref.shape, F32)

    ki = lax.broadcasted_iota(jnp.int32, (ML_CHUNK, ML_CHUNK), 0)
    qi = lax.broadcasted_iota(jnp.int32, (ML_CHUNK, ML_CHUNK), 1)

    for bi in range(nb):
        for direction in range(2):
            q_ref, k_ref, vt_ref, gc_ref, gr_ref, h_ref = (
                (qf_ref, kf_ref, vf_ref, gcf_ref, grf_ref, hf_ref) if direction == 0 else
                (qb_ref, kb_ref, vb_ref, gcb_ref, grb_ref, hb_ref))
            visible = (ki <= qi) if direction == 0 else (ki >= qi)
            gc = gc_ref[bi]
            gr = gr_ref[bi, 0]
            for hd in range(ML_HEADS):
                chain = (bi * 2 + direction) * ML_HEADS + hd
                c_lf = (2 * direction + 1) * ML_HEADS + hd
                c_cs = c_lf + (n_g if direction == 0 else 2 * n_g)
                c_u = 3 * n_g + direction * ML_HEADS + hd
                lo, hi = hd * ML_DIM, (hd + 1) * ML_DIM
                q = q_ref[bi, :, lo:hi]
                k = k_ref[bi, :, lo:hi]
                v_t = vt_ref[bi, 0, hd * VT_ROWS:(hd + 1) * VT_ROWS, :]
                u_row = gr[c_u:c_u + 1, :]
                lf_row = gr[c_lf:c_lf + 1, :]
                cs_row = gr[c_cs:c_cs + 1, :]
                m_st = m_refs[chain][:, 0:1]
                st_t = state_refs[chain][...]

                u_vis = jnp.where(visible, jnp.broadcast_to(gc[:, c_u:c_u + 1], visible.shape), -jnp.inf)
                v_row = jnp.maximum(m_st, jnp.max(u_vis, axis=0, keepdims=True))
                d_t = jnp.exp(u_vis - v_row)
                inter = jnp.exp(m_st - v_row)
                sc_t = (_dot_nt(k, q) * d_t).astype(BF16)
                tot_t = inter * _dot_nt(st_t.astype(BF16), q) + _dot(v_t, sc_t)
                den = jnp.maximum(jnp.abs(tot_t[ML_DIM:ML_DIM + 1, :]), jnp.exp(-(cs_row + v_row)))
                h_ref[bi, :, lo:hi] = (tot_t[:ML_DIM, :] / den).T

                b_last = jnp.sum(lf_row, axis=1, keepdims=True)
                m_sc = jnp.maximum(m_st, jnp.max(u_row, axis=1, keepdims=True))
                wgt = jnp.exp(u_row - m_sc)
                vw_t = (v_t.astype(F32) * wgt).astype(BF16)
                state_refs[chain][...] = jnp.exp(m_st - m_sc) * st_t + _dot(vw_t, k)
                m_refs[chain][...] = jnp.broadcast_to(b_last + m_sc, (1, LANES))


def _mlstm(mq, mk, mvt, gc, gr, n_ctx):
    b, t, w = mq.shape
    nc = t // ML_CHUNK
    ncc = n_ctx // ML_CHUNK
    fwd = lambda s: s
    bwd = lambda s: jnp.where(s < ncc, ncc - 1 - s, nc - 1 - s + ncc)
    tok = lambda f: pl.BlockSpec((b, ML_CHUNK, w), lambda s: (0, f(s), 0))
    gcs = lambda f: pl.BlockSpec((b, ML_CHUNK, LANES), lambda s: (0, f(s), 0))
    grs = lambda f: pl.BlockSpec((b, 1, ML_CHUNK, LANES), lambda s: (0, f(s), 0, 0))
    vts = lambda f: pl.BlockSpec((b, 1, ML_HEADS * VT_ROWS, ML_CHUNK), lambda s: (0, f(s), 0, 0))
    n_chain = b * 2 * ML_HEADS
    return pl.pallas_call(
        _mlstm_kernel,
        grid=(nc,),
        in_specs=[tok(fwd), tok(fwd), vts(fwd), gcs(fwd), grs(fwd),
                  tok(bwd), tok(bwd), vts(bwd), gcs(bwd), grs(bwd)],
        out_specs=[tok(fwd), tok(bwd)],
        out_shape=[jax.ShapeDtypeStruct((b, t, w), F32)] * 2,
        scratch_shapes=[pltpu.VMEM((VT_ROWS, ML_DIM), F32)] * n_chain
                       + [pltpu.VMEM((1, LANES), F32)] * n_chain,
        compiler_params=_params("arbitrary"),
        name="mlstm",
    )(mq, mk, mvt, gc, gr, mq, mk, mvt, gc, gr)


def _attn_kernel(q_ref, qn_ref, k_ref, vt_ref, lam_ref, sg_ref, o_ref, qm_scr, *scr,
                 ctx_tiles, n_ctx, key_block, n_blocks, lam_init):
    s_scr = (scr[0:2], scr[2:4])
    bm_scr = (scr[4:6], scr[6:8])
    m_scr = scr[8:10]
    acc_scr = scr[10:12]
    i = pl.program_id(2)
    dv = DA_V_DIM
    chunk = vt_ref.shape[4]
    q = q_ref[0]
    lane = lax.broadcasted_iota(jnp.int32, q.shape, 1)
    zero = jnp.zeros_like(q)

    def map_halves(qv):
        return jnp.where(lane < DA_QK_DIM, qv, zero), jnp.where(lane >= DA_QK_DIM, qv, zero)

    qm_scr[0], qm_scr[1] = map_halves(q)
    align = math.gcd(n_ctx, key_block)

    def scores(mp, start, size, q_maps=None):
        q_map = qm_scr[mp] if q_maps is None else q_maps[mp]
        return _dot_nt(k_ref[0, pl.ds(start, size), :], q_map)

    def weighted_values(sc, m, first_chunk):
        p = jnp.exp2(sc - m).astype(BF16)
        out = None
        for ci in range(sc.shape[0] // chunk):
            part = _dot(vt_ref[0, 0, first_chunk + ci], p[ci * chunk:(ci + 1) * chunk, :])
            out = part if out is None else out + part
        return out

    for mp in range(2):
        sc = scores(mp, 0, n_ctx)
        m = jnp.max(sc, axis=0, keepdims=True)
        m_scr[mp][...] = m
        acc_scr[mp][...] = weighted_values(sc, m, 0)

    def stage(j, slot, q_maps=None):
        start = pl.multiple_of(n_ctx + j * key_block, align)
        for mp in range(2):
            sc = scores(mp, start, key_block, q_maps)
            s_scr[mp][slot][...] = sc
            bm_scr[mp][slot][...] = jnp.max(sc, axis=0, keepdims=True)

    def consume(j, slot):
        first_chunk = (n_ctx + j * key_block) // chunk
        for mp in range(2):
            m_old = m_scr[mp][...]
            m_new = jnp.maximum(m_old, bm_scr[mp][slot][...])
            alpha = jnp.exp2(m_old - m_new)
            pv = weighted_values(s_scr[mp][slot][...], m_new, first_chunk)
            acc_scr[mp][...] = alpha * acc_scr[mp][...] + pv
            m_scr[mp][...] = m_new

    @pl.when(i >= ctx_tiles)
    def _():
        @pl.when(i == ctx_tiles)
        def _():
            stage(0, 0)

        def body(g, carry):
            stage(2 * g + 1, 1)
            consume(2 * g, 0)
            stage(2 * g + 2, 0)
            consume(2 * g + 1, 1)
            return carry
        lax.fori_loop(0, n_blocks // 2 - 1, body, 0)
        stage(n_blocks - 1, 1)
        consume(n_blocks - 2, 0)
        stage(0, 0, map_halves(qn_ref[0]))
        consume(n_blocks - 1, 1)

    lv = lam_ref[...]
    dot01 = jnp.sum(lv[0:1, :] * lv[1:2, :], axis=1, keepdims=True)
    dot23 = jnp.sum(lv[2:3, :] * lv[3:4, :], axis=1, keepdims=True)
    lam = jnp.exp(dot01) - jnp.exp(dot23) + lam_init
    a0 = acc_scr[0][...]
    a1 = acc_scr[1][...]
    o = (a0[0:dv, :] / a0[dv:dv + 1, :] - lam * (a1[0:dv, :] / a1[dv:dv + 1, :])).T
    o = o * lax.rsqrt(jnp.mean(o * o, axis=-1, keepdims=True) + NORM_EPS) * sg_ref[...]
    o_ref[0] = (o * (1.0 - lam_init)).astype(BF16)


def _diff_attention(q, k, vt, lam8, subln_g, n_ctx, lam_init):
    b, t, w = q.shape
    tq = ROW_TILE
    n_lat = t - n_ctx
    key_block = 1024 if n_lat % 2048 == 0 else 512
    n_blocks = n_lat // key_block
    _, _, n_chunks, vt_rows, chunk = vt.shape
    assert n_lat % (2 * key_block) == 0 and n_ctx % tq == 0 and n_ctx % chunk == 0
    assert key_block % chunk == 0
    kern = functools.partial(_attn_kernel, ctx_tiles=n_ctx // tq, n_ctx=n_ctx, key_block=key_block,
                             n_blocks=n_blocks, lam_init=lam_init)
    return pl.pallas_call(
        kern,
        grid=(b, DA_HEADS, t // tq),
        in_specs=[pl.BlockSpec((1, tq, LANES), lambda bi, h, i: (bi, i, h)),
                  pl.BlockSpec((1, tq, LANES), lambda bi, h, i: (bi, jnp.minimum(i + 1, t // tq - 1), h)),
                  pl.BlockSpec((1, t, LANES), lambda bi, h, i: (bi, 0, h)),
                  pl.BlockSpec((1, 1, n_chunks, vt_rows, chunk), lambda bi, h, i: (bi, h, 0, 0, 0)),
                  pl.BlockSpec(lam8.shape, lambda bi, h, i: (0, 0)),
                  pl.BlockSpec(subln_g.shape, lambda bi, h, i: (0, 0))],
        out_specs=pl.BlockSpec((1, tq, LANES), lambda bi, h, i: (bi, i, h)),
        out_shape=jax.ShapeDtypeStruct((b, t, w), BF16),
        scratch_shapes=[pltpu.VMEM((2, tq, LANES), BF16)]
                       + [pltpu.VMEM((key_block, tq), F32)] * 4
                       + [pltpu.VMEM((1, tq), F32)] * 6
                       + [pltpu.VMEM((vt_rows, tq), F32)] * 2,
        compiler_params=_params("arbitrary", "arbitrary", "arbitrary"),
        name="diff_attention",
    )(q, q, k, vt, lam8, subln_g)


def _top2_sum(a, b, c, d):
    hi1, lo1 = jnp.maximum(a, b), jnp.minimum(a, b)
    hi2, lo2 = jnp.maximum(c, d), jnp.minimum(c, d)
    return jnp.maximum(hi1, hi2) + jnp.maximum(jnp.minimum(hi1, hi2), jnp.maximum(lo1, lo2))


def _route(f, rwt_ref, rb_ref, cnt_ref):
    tm = f.shape[0]
    aff = _sigmoid(_dot_nt(rwt_ref[...], f, HI))
    biased = aff + rb_ref[:, 0:1]
    bz = [biased[e:e + 1, :] for e in range(N_EXPERTS)]
    af = [aff[e:e + 1, :] for e in range(N_EXPERTS)]
    n_grp = N_EXPERTS // EXPERTS_PER_GROUP
    scores = [_top2_sum(*bz[EXPERTS_PER_GROUP * g:EXPERTS_PER_GROUP * (g + 1)]) for g in range(n_grp)]
    best = scores[0]
    sel_grp = jnp.zeros_like(best, dtype=jnp.int32)
    for g in range(1, n_grp):
        better = scores[g] > best
        sel_grp = jnp.where(better, g, sel_grp)
        best = jnp.where(better, scores[g], best)
    chosen = []
    for e in range(N_EXPERTS):
        g = e // EXPERTS_PER_GROUP
        rank = jnp.zeros_like(sel_grp)
        for o in range(EXPERTS_PER_GROUP * g, EXPERTS_PER_GROUP * (g + 1)):
            if o == e:
                continue
            beats = (bz[o] > bz[e]) if o > e else (bz[o] >= bz[e])
            rank = rank + beats.astype(jnp.int32)
        chosen.append(jnp.logical_and(sel_grp == g, rank < 2))
    denom = sum(jnp.where(chosen[e], af[e], 0.0) for e in range(N_EXPERTS))
    erow = lax.broadcasted_iota(jnp.int32, (N_EXPERTS, tm), 0)
    one_hot = jnp.zeros((N_EXPERTS, tm), F32)
    for e in range(N_EXPERTS):
        one_hot = jnp.where(jnp.logical_and(erow == e, chosen[e]), 1.0, one_hot)
    earlier = (lax.broadcasted_iota(jnp.int32, (tm, tm), 0)
               < lax.broadcasted_iota(jnp.int32, (tm, tm), 1)).astype(BF16)
    rank_all = _dot(one_hot.astype(BF16), earlier) + cnt_ref[:, 0:1]
    cnt_ref[...] = cnt_ref[...] + jnp.sum(one_hot, axis=1, keepdims=True)

    seen = jnp.zeros((1, tm), jnp.bool_)
    e_a = e_b = jnp.zeros((1, tm), jnp.int32)
    r_a = r_b = w_a = w_b = jnp.zeros((1, tm), F32)
    for e in range(N_EXPERTS):
        first = jnp.logical_and(chosen[e], jnp.logical_not(seen))
        second = jnp.logical_and(chosen[e], seen)
        rk = rank_all[e:e + 1, :]
        wt = af[e] / denom
        e_a, e_b = jnp.where(first, e, e_a), jnp.where(second, e, e_b)
        r_a, r_b = jnp.where(first, rk, r_a), jnp.where(second, rk, r_b)
        w_a, w_b = jnp.where(first, wt, w_a), jnp.where(second, wt, w_b)
        seen = jnp.logical_or(seen, chosen[e])
    r8 = lax.broadcasted_iota(jnp.int32, (8, tm), 0)
    sel = jnp.where(r8 == 0, e_a, jnp.where(r8 == 1, e_b, jnp.where(
        r8 == 2, r_a.astype(jnp.int32), jnp.where(r8 == 3, r_b.astype(jnp.int32), 0))))
    row = lax.broadcasted_iota(jnp.int32, (LANES, tm), 0)
    w_t = jnp.where(row == 0, w_a, jnp.where(row == 1, w_b, 0.0))
    return sel, w_t.T


def _post_kernel(*refs, even):
    if even:
        (x_ref, a_ref, hf_ref, hb_ref, og_ref, mod_ref, wa_ref, wm_ref, gf_ref, rwt_ref, rb_ref,
         xo_ref, f_ref, sel_ref, w_ref, cnt_ref, cnt_scr) = refs
        m = ((hf_ref[0] + hb_ref[0]) * og_ref[0].astype(F32)).astype(BF16)
        o = _dot(a_ref[0], wa_ref[...]) + _dot(m, wm_ref[...])
    else:
        (x_ref, a_ref, mod_ref, wa_ref, gf_ref, rwt_ref, rb_ref,
         xo_ref, f_ref, sel_ref, w_ref, cnt_ref, cnt_scr) = refs
        o = _dot(a_ref[0].astype(BF16), wa_ref[...])

    @pl.when(jnp.logical_and(pl.program_id(0) == 0, pl.program_id(1) == 0))
    def _():
        cnt_scr[...] = jnp.zeros(cnt_scr.shape, F32)

    x = x_ref[0] + mod_ref[0, 2:3, :] * o
    xo_ref[0] = x
    f = _norm_mod(x, gf_ref[...], mod_ref[0, 3:4, :], mod_ref[0, 4:5, :])
    _store_row_tiles(f_ref.at[0], f)
    sel, w_col = _route(f, rwt_ref, rb_ref, cnt_scr)
    sel_ref[0, 0] = sel
    w_ref[0] = w_col
    cnt_ref[...] = cnt_scr[...]


def _post_mixer(x, acts, weights, modtab, gffn, rwt, rb, n_ctx, x_row_off, even):
    b, t, _ = acts[0].shape
    d = x.shape[2]
    tm = ROW_TILE
    nt = t // tm
    ctx_tiles = n_ctx // tm
    off = x_row_off // tm
    full = lambda a: pl.BlockSpec(a.shape, lambda bi, i: (0,) * a.ndim)
    row = lambda w: pl.BlockSpec((1, tm, w), lambda bi, i: (bi, i, 0))
    mod_spec = pl.BlockSpec(
        (1, 8, d), lambda bi, i: (2 * bi + (i + off >= ctx_tiles).astype(jnp.int32), 0, 0))
    in_specs = ([pl.BlockSpec((1, tm, d), lambda bi, i: (bi, i + off, 0))]
                + [row(a.shape[2]) for a in acts] + [mod_spec]
                + [full(w) for w in weights] + [full(gffn), full(rwt), full(rb)])
    return pl.pallas_call(
        functools.partial(_post_kernel, even=even),
        grid=(b, nt),
        in_specs=in_specs,
        out_specs=[row(d), pl.BlockSpec((1, tm * d // LANES, LANES), lambda bi, i: (bi, i, 0)),
                   pl.BlockSpec((1, 1, 8, tm), lambda bi, i: (bi, i, 0, 0)),
                   row(LANES),
                   pl.BlockSpec((N_EXPERTS, LANES), lambda bi, i: (0, 0))],
        out_shape=[jax.ShapeDtypeStruct((b, t, d), F32),
                   jax.ShapeDtypeStruct((b, t * d // LANES, LANES), F32),
                   jax.ShapeDtypeStruct((b, nt, 8, tm), jnp.int32),
                   jax.ShapeDtypeStruct((b, t, LANES), F32),
                   jax.ShapeDtypeStruct((N_EXPERTS, LANES), F32)],
        scratch_shapes=[pltpu.VMEM((N_EXPERTS, LANES), F32)],
        compiler_params=_params("arbitrary", "arbitrary"),
        name="post_mixer_even" if even else "post_mixer_odd",
    )(x, *acts, modtab, *weights, gffn, rwt, rb)


EXPERT_ROW_TILE = 256
DMA_ISSUE_UNROLL = 8


def _moe_plan(sel, counts, n_tok):
    e_a, e_b, r_a, r_b = (sel[:, :, k, :].reshape(-1) for k in range(4))
    cnt = counts[:, 0].astype(jnp.int32)
    padded = ((cnt + EXPERT_ROW_TILE - 1) // EXPERT_ROW_TILE) * EXPERT_ROW_TILE
    ends = jnp.cumsum(padded)
    starts = ends - padded
    pos = jnp.concatenate([starts[e_a] + r_a, starts[e_b] + r_b]).astype(jnp.int32)
    n_tiles = 2 * n_tok // EXPERT_ROW_TILE + N_EXPERTS
    tile_start = jnp.arange(n_tiles, dtype=jnp.int32) * EXPERT_ROW_TILE
    tile_expert = jnp.minimum(jnp.sum(tile_start[:, None] >= ends[None, :], axis=1),
                              N_EXPERTS - 1).astype(jnp.int32)
    tiles_used = (ends[-1:] // EXPERT_ROW_TILE).astype(jnp.int32)
    return pos, tile_expert, tiles_used, n_tiles


def _store_row_tiles(ref, x):
    rows, d = x.shape
    n_sub = d // LANES
    for s in range(n_sub):
        ref[pl.ds(s, rows, stride=n_sub), :] = x[:, s * LANES:(s + 1) * LANES]


def _load_row_tiles(ref, rows):
    n_sub = ref.shape[0] // rows
    return jnp.concatenate([ref[pl.ds(s, rows, stride=n_sub), :] for s in range(n_sub)], axis=1)


def _dispatch_kernel(pos_ref, f_ref, init_ref, out_ref, sem, *, n_tok, tm):
    del init_ref
    n_sub = f_ref.shape[1] // tm
    base = (pl.program_id(0) * pl.num_programs(1) + pl.program_id(1)) * tm

    def row_copy(r, k):
        dst = pos_ref[k * n_tok + base + r]
        src = f_ref.at[0, pl.ds(pl.multiple_of(r * n_sub, n_sub), n_sub), :]
        return pltpu.make_async_copy(src, out_ref.at[dst], sem)

    def issue(r, carry):
        row_copy(r, 0).start()
        row_copy(r, 1).start(priority=1)
        return carry

    lax.fori_loop(0, tm, issue, 0, unroll=DMA_ISSUE_UNROLL)
    for _ in range(2):
        pltpu.make_async_copy(out_ref.at[pl.ds(0, tm)], out_ref.at[pl.ds(0, tm)], sem).wait()


def _dispatch(pos, f, n_rows, n_tok):
    b, rows, _ = f.shape
    n_sub = rows * b // n_tok
    tm = ROW_TILE
    grid_spec = pltpu.PrefetchScalarGridSpec(
        num_scalar_prefetch=1,
        grid=(b, n_tok // b // tm),
        in_specs=[pl.BlockSpec((1, tm * n_sub, LANES), lambda bi, i, pos_ref: (bi, i, 0)),
                  pl.BlockSpec(memory_space=pl.ANY)],
        out_specs=pl.BlockSpec(memory_space=pl.ANY),
        scratch_shapes=[pltpu.SemaphoreType.DMA(())])
    return pl.pallas_call(
        functools.partial(_dispatch_kernel, n_tok=n_tok, tm=tm),
        grid_spec=grid_spec,
        out_shape=jax.ShapeDtypeStruct((n_rows, n_sub, LANES), F32),
        input_output_aliases={2: 0},
        compiler_params=_params("arbitrary", "arbitrary"),
        name="moe_dispatch",
    )(pos, f, jnp.zeros((n_rows, n_sub, LANES), F32))


def _expert_ffn_kernel(te_ref, used_ref, x_ref, wg_ref, wu_ref, wd_ref, y_ref):
    del te_ref
    live = pl.program_id(0) < used_ref[0]

    @pl.when(live)
    def _():
        xb = _load_row_tiles(x_ref, EXPERT_ROW_TILE).astype(BF16)
        he = _silu(_dot(xb, wg_ref[0])) * _dot(xb, wu_ref[0])
        _store_row_tiles(y_ref, _dot(he.astype(BF16), wd_ref[0]))

    @pl.when(jnp.logical_not(live))
    def _():
        y_ref[...] = jnp.zeros(y_ref.shape, F32)


def _expert_ffn(tile_expert, tiles_used, xs, wg, wu, wd, n_tiles):
    n_rows, n_sub, _ = xs.shape
    _, d, d_e = wg.shape
    tr = EXPERT_ROW_TILE
    grid_spec = pltpu.PrefetchScalarGridSpec(
        num_scalar_prefetch=2,
        grid=(n_tiles,),
        in_specs=[pl.BlockSpec((tr * n_sub, LANES), lambda j, te, used: (j, 0)),
                  pl.BlockSpec((1, d, d_e), lambda j, te, used: (te[j], 0, 0)),
                  pl.BlockSpec((1, d, d_e), lambda j, te, used: (te[j], 0, 0)),
                  pl.BlockSpec((1, d_e, d), lambda j, te, used: (te[j], 0, 0))],
        out_specs=pl.BlockSpec((tr * n_sub, LANES), lambda j, te, used: (j, 0)))
    ys = pl.pallas_call(
        _expert_ffn_kernel,
        grid_spec=grid_spec,
        out_shape=jax.ShapeDtypeStruct((n_rows * n_sub, LANES), F32),
        compiler_params=_params("arbitrary"),
        name="moe_expert_ffn",
    )(tile_expert, tiles_used, xs.reshape(n_rows * n_sub, LANES), wg, wu, wd)
    return ys.reshape(n_rows, n_sub, LANES)


def _combine_kernel(pos_ref, y_ref, w_ref, x_ref, modc_ref, modl_ref, fg_ref, o_ref, buf, sem,
                    *, n_tok, n_ctx, final_norm):
    i = pl.program_id(1)
    tm = x_ref.shape[1]
    base = (pl.program_id(0) * pl.num_programs(1) + i) * tm

    n_sub = y_ref.shape[1]

    def row_copy(r, k):
        src = pos_ref[k * n_tok + base + r]
        dst = buf.at[k, pl.ds(pl.multiple_of(r * n_sub, n_sub), n_sub), :]
        return pltpu.make_async_copy(y_ref.at[src], dst, sem)

    def issue(r, carry):
        row_copy(r, 0).start()
        row_copy(r, 1).start(priority=1)
        return carry

    lax.fori_loop(0, tm, issue, 0, unroll=DMA_ISSUE_UNROLL)
    for _ in range(2):
        pltpu.make_async_copy(y_ref.at[pl.ds(0, tm)], y_ref.at[pl.ds(0, tm)], sem).wait()
    w = w_ref[0]
    y = _load_row_tiles(buf.at[0], tm) * w[:, 0:1] + _load_row_tiles(buf.at[1], tm) * w[:, 1:2]
    rows = i * tm + lax.broadcasted_iota(jnp.int32, (tm, 1), 0)
    gate = jnp.where(rows < n_ctx, modc_ref[0, 5:6, :], modl_ref[0, 5:6, :])
    out = x_ref[0] + gate * y
    if final_norm:
        out = out * lax.rsqrt(jnp.mean(out * out, axis=-1, keepdims=True) + NORM_EPS) * fg_ref[...]
    o_ref[0] = out


def _combine(pos, ys, w_col, xmid, modtab, final_g, n_ctx, final_norm):
    b, t, d = xmid.shape
    tm = ROW_TILE
    row = lambda w: pl.BlockSpec((1, tm, w), lambda bi, i, pos_ref: (bi, i, 0))
    grid_spec = pltpu.PrefetchScalarGridSpec(
        num_scalar_prefetch=1,
        grid=(b, t // tm),
        in_specs=[pl.BlockSpec(memory_space=pl.ANY), row(LANES), row(d),
                  pl.BlockSpec((1, 8, d), lambda bi, i, pos_ref: (2 * bi, 0, 0)),
                  pl.BlockSpec((1, 8, d), lambda bi, i, pos_ref: (2 * bi + 1, 0, 0)),
                  pl.BlockSpec(final_g.shape, lambda bi, i, pos_ref: (0, 0))],
        out_specs=row(d),
        scratch_shapes=[pltpu.VMEM((2, tm * d // LANES, LANES), F32), pltpu.SemaphoreType.DMA(())])
    return pl.pallas_call(
        functools.partial(_combine_kernel, n_tok=b * t, n_ctx=n_ctx, final_norm=final_norm),
        grid_spec=grid_spec,
        out_shape=jax.ShapeDtypeStruct((b, t, d), F32),
        compiler_params=_params("arbitrary", "arbitrary"),
        name="moe_combine",
    )(pos, ys, w_col, xmid, modtab, modtab, final_g)


def _moe(f, sel, w_col, counts, wg, wu, wd, xmid, modtab, final_g, n_ctx, final_norm):
    b, t, _ = xmid.shape
    pos, tile_expert, tiles_used, n_tiles = _moe_plan(sel, counts, b * t)
    xs = _dispatch(pos, f, n_tiles * EXPERT_ROW_TILE, b * t)
    ys = _expert_ffn(tile_expert, tiles_used, xs, wg, wu, wd, n_tiles)
    return _combine(pos, ys, w_col, xmid, modtab, final_g, n_ctx, final_norm)


def _chan_dft_kernel(x_ref, mod_ref, g_ref, w_ref, zr_ref, zi_ref):
    h = _norm_mod(x_ref[0], g_ref[...], mod_ref[0, 0:1, :], mod_ref[0, 1:2, :])
    gd = w_ref.shape[0]
    for gi in range(h.shape[1] // gd):
        z = _dot_split(h[:, gi * gd:(gi + 1) * gd], w_ref[...])
        zr_ref[0, :, gi * gd:(gi + 1) * gd] = z[:, :gd]
        zi_ref[0, :, gi * gd:(gi + 1) * gd] = z[:, gd:]


def _chan_dft(x, modtab, g, w_cs, x_row_off, t):
    b, _, d = x.shape
    tm = ROW_TILE
    off = x_row_off // tm
    row = pl.BlockSpec((1, tm, d), lambda bi, i: (bi, i, 0))
    return pl.pallas_call(
        _chan_dft_kernel,
        grid=(b, t // tm),
        in_specs=[pl.BlockSpec((1, tm, d), lambda bi, i: (bi, i + off, 0)),
                  pl.BlockSpec((1, 8, d), lambda bi, i: (2 * bi + 1, 0, 0)),
                  pl.BlockSpec(g.shape, lambda bi, i: (0, 0)),
                  pl.BlockSpec(w_cs.shape, lambda bi, i: (0, 0))],
        out_specs=[row, row],
        out_shape=[jax.ShapeDtypeStruct((b, t, d), F32)] * 2,
        compiler_params=_params("parallel", "parallel"),
        name="chan_dft",
    )(x, modtab, g, w_cs)


def _dft1_kernel(zr_ref, zi_ref, w_ref, yr_ref, yi_ref):
    n1 = zr_ref.shape[1]
    y = _dot_split(w_ref[...], jnp.concatenate([zr_ref[0], zi_ref[0]], axis=0))
    yr_ref[0] = y[:n1]
    yi_ref[0] = y[n1:]


def _dft1(zr, zi, w1):
    b, n1, cols = zr.shape
    tn = min(cols, 4096)
    blk = pl.BlockSpec((1, n1, tn), lambda bi, j: (bi, 0, j))
    return pl.pallas_call(
        _dft1_kernel,
        grid=(b, cols // tn),
        in_specs=[blk, blk, pl.BlockSpec(w1.shape, lambda bi, j: (0, 0))],
        out_specs=[blk, blk],
        out_shape=[jax.ShapeDtypeStruct(zr.shape, F32)] * 2,
        compiler_params=_params("parallel", "parallel"),
        name="dft_stage1",
    )(zr, zi, w1)


def _dft2_kernel(yr_ref, yi_ref, tab_ref, o_ref):
    y = jnp.concatenate([yr_ref[0, 0], yi_ref[0, 0]], axis=0)
    o_ref[0, 0] = _dot_split(tab_ref[0], y)


def _dft2(yr, yi, tab):
    b, n1, n2, d = yr.shape
    blk = pl.BlockSpec((1, 1, n2, d), lambda bi, k1: (bi, k1, 0, 0))
    return pl.pallas_call(
        _dft2_kernel,
        grid=(b, n1),
        in_specs=[blk, blk, pl.BlockSpec((1, n2, 2 * n2), lambda bi, k1: (k1, 0, 0))],
        out_specs=blk,
        out_shape=jax.ShapeDtypeStruct(yr.shape, F32),
        compiler_params=_params("parallel", "parallel"),
        name="dft_stage2",
    )(yr, yi, tab)


def _dft_tables(t, gd):
    n2 = ML_CHUNK
    n1 = t // n2
    def cs(num, den):
        ang = (2.0 * np.pi / den) * (num % den).astype(np.float64)
        return np.cos(ang), np.sin(ang)
    c = np.arange(gd)
    cc, sc = cs(np.outer(c, c), gd)
    w_cs = np.concatenate([cc, -sc], axis=1)
    a = np.arange(n1)
    c1, s1 = cs(np.outer(a, a), n1)
    w1 = np.block([[c1, s1], [-s1, c1]])
    k = a[:, None, None] + n1 * np.arange(n2)[None, :, None]
    c2, s2 = cs(k * np.arange(n2)[None, None, :], t)
    tab = np.concatenate([c2, s2], axis=2) / math.sqrt(t * gd)
    return (jnp.asarray(w_cs, F32), jnp.asarray(w1, F32), jnp.asarray(tab, F32))


def _rope_tables(n_ctx, n_lat):
    pos = jnp.arange(n_lat, dtype=jnp.int32)
    n_axis = DA_QK_DIM // 4
    inv = ROPE_BASE ** (-jnp.arange(n_axis, dtype=F32) / n_axis)
    ang = jnp.concatenate([(pos // GRID_W).astype(F32)[:, None] * inv,
                           (pos % GRID_W).astype(F32)[:, None] * inv], axis=-1)
    cos, sin = jnp.cos(ang), jnp.sin(ang)
    cos = jnp.concatenate([jnp.ones((n_ctx, 2 * n_axis), F32), cos], axis=0)
    sin = jnp.concatenate([jnp.zeros((n_ctx, 2 * n_axis), F32), sin], axis=0)
    cos128 = jnp.concatenate([cos, cos, cos, cos], axis=1)
    sin128 = jnp.concatenate([-sin, sin, -sin, sin], axis=1)
    return cos128, sin128


def _deinterleave(w):
    d, n = w.shape
    w = w.reshape(d, n // DA_QK_DIM, DA_QK_DIM // 2, 2)
    return jnp.concatenate([w[..., 0], w[..., 1]], axis=-1).reshape(d, n)


def _pad_rows(a, rows):
    return jnp.concatenate([a, jnp.zeros((rows - a.shape[0],) + a.shape[1:], a.dtype)], axis=0)


def _pad_cols(a, cols):
    return jnp.concatenate([a, jnp.zeros(a.shape[:-1] + (cols - a.shape[-1],), a.dtype)], axis=-1)


def kernel(x, c, ctx, c_ctx, ada_w, ada_b, norm_mix_g, norm_ffn_g, even_w_in, even_w_out,
           even_conv_w, even_gate_b, even_lam, even_subln_g, odd_w_fnet, router_w, router_b,
           exp_w_gate, exp_w_up, exp_w_down, final_g):
    b, n_lat, d = x.shape
    n_ctx = ctx.shape[1]
    depth = ada_w.shape[0]
    assert depth == 2 and b + 1 <= 8
    assert n_ctx % ROW_TILE == 0 and n_lat % ROW_TILE == 0

    cond8 = _pad_rows(jnp.concatenate([c_ctx[None, :], c], axis=0), 8)
    rwt = router_w.T
    rb = jnp.broadcast_to(router_b[:, None], (N_EXPERTS, LANES))
    row2 = lambda v: v.reshape(1, -1)

    def modtab_for(layer):
        mods = _ada_mods(cond8, ada_w[layer], ada_b[layer]).reshape(8, 6, d)
        mods = jnp.concatenate([mods, jnp.zeros((8, 2, d), F32)], axis=1)
        idx = np.array([[0, 1 + bi] for bi in range(b)]).reshape(-1)
        return mods[idx]

    xs = jnp.concatenate([ctx, x], axis=1)
    modtab = modtab_for(0)
    w_in = even_w_in[0]
    o1 = DA_HEADS * 2 * DA_QK_DIM
    o2 = 2 * o1
    o3 = o2 + DA_HEADS * DA_V_DIM
    o4 = o3 + 2 * ML_HEADS * ML_DIM
    o5 = o4 + ML_HEADS * ML_DIM
    o6 = o5 + ML_HEADS * ML_DIM
    ws = [(_deinterleave(w_in[:, :o1]) * (DA_QK_DIM ** -0.5)).astype(BF16),
          _deinterleave(w_in[:, o1:o2]).astype(BF16),
          w_in[:, o2:o3].astype(BF16), w_in[:, o3:o4].astype(BF16),
          w_in[:, o4:o5].astype(BF16), w_in[:, o5:o6].astype(BF16),
          _pad_cols(w_in[:, o6:], LANES).astype(BF16)]
    gate_b = _pad_cols(even_gate_b[0].reshape(1, -1), LANES)
    cos128, sin128 = _rope_tables(n_ctx, n_lat)
    daq, dak, dav, mqk, mv, og, gates = _inproj(xs, modtab, row2(norm_mix_g[0]), cos128, sin128,
                                                 ws, gate_b, n_ctx)
    mq, mk, gc, gr, mvt = _mlprep(mqk, _pad_rows(even_conv_w[0], 8), gates, mv, n_ctx)
    hf, hb = _mlstm(mq, mk, mvt, gc, gr, n_ctx)
    lam_init = 0.8 - 0.6 * math.exp(-0.3 * 0)
    lam8 = _pad_rows(even_lam[0], 8)
    att = _diff_attention(daq, dak, dav, lam8, row2(even_subln_g[0]), n_ctx, lam_init)
    w_out = even_w_out[0].astype(BF16)
    half = DA_HEADS * DA_V_DIM
    xmid, f, sel, w_col, counts = _post_mixer(xs, [att, hf, hb, og], [w_out[:half], w_out[half:]],
                                              modtab, row2(norm_ffn_g[0]), rwt, rb, n_ctx, 0, True)
    xs = _moe(f, sel, w_col, counts, exp_w_gate[0].astype(BF16), exp_w_up[0].astype(BF16),
              exp_w_down[0].astype(BF16), xmid, modtab, row2(final_g), n_ctx, False)

    modtab = modtab_for(1)
    gd = d // FN_GROUPS
    w_cs, w1, tab = _dft_tables(n_lat, gd)
    n2 = ML_CHUNK
    n1 = n_lat // n2
    zr, zi = _chan_dft(xs, modtab, row2(norm_mix_g[1]), w_cs, n_ctx, n_lat)
    yr, yi = _dft1(zr.reshape(b, n1, n2 * d), zi.reshape(b, n1, n2 * d), w1)
    fo = _dft2(yr.reshape(b, n1, n2, d), yi.reshape(b, n1, n2, d), tab)
    fo = fo.transpose(0, 2, 1, 3).reshape(b, n_lat, d)
    xmid, f, sel, w_col, counts = _post_mixer(xs, [fo], [odd_w_fnet[0].astype(BF16)], modtab,
                                              row2(norm_ffn_g[1]), rwt, rb, 0, n_ctx, False)
    return _moe(f, sel, w_col, counts, exp_w_gate[1].astype(BF16), exp_w_up[1].astype(BF16),
                exp_w_down[1].astype(BF16), xmid, modtab, row2(final_g), 0, True)
```

```python
import functools
import math

import jax
import jax.numpy as jnp
import numpy as np
from jax import lax
from jax.experimental import pallas as pl
from jax.experimental.pallas import tpu as pltpu

F32 = jnp.float32
BF16 = jnp.bfloat16

NORM_EPS = 1e-6
GRID_W = 64
DA_HEADS = 4
DA_QK_DIM = 64
DA_V_DIM = 128
ML_HEADS = 4
ML_DIM = 128
ML_CHUNK = 128
FN_GROUPS = 4
N_EXPERTS = 16
EXPERTS_PER_GROUP = 4
ROPE_BASE = 10000.0
LANES = 128
ROW_TILE = 256
VMEM_LIMIT_BYTES = 56 * 1024 * 1024
HI = lax.Precision.HIGHEST
LOG2_E = math.log2(math.e)


def _params(*sem):
    return pltpu.CompilerParams(dimension_semantics=sem, vmem_limit_bytes=VMEM_LIMIT_BYTES)


def _dot(a, b, precision=None):
    return jnp.dot(a, b, preferred_element_type=F32, precision=precision)


def _dot_nt(a, b, precision=None):
    return lax.dot_general(a, b, (((1,), (1,)), ((), ())), preferred_element_type=F32,
                           precision=precision)


def _dot_split(a, b):
    a_hi = a.astype(BF16)
    b_hi = b.astype(BF16)
    a_lo = (a - a_hi.astype(F32)).astype(BF16)
    b_lo = (b - b_hi.astype(F32)).astype(BF16)
    return _dot(a_hi, b_hi) + (_dot(a_hi, b_lo) + _dot(a_lo, b_hi))


def _sigmoid(x):
    return 1.0 / (1.0 + jnp.exp(-x))


def _silu(x):
    return x * _sigmoid(x)


def _norm_mod(x, g, shift, scale):
    y = x * lax.rsqrt(jnp.mean(x * x, axis=-1, keepdims=True) + NORM_EPS) * g
    return y * (1.0 + scale) + shift


def _ada_kernel(c_ref, w_ref, b_ref, o_ref):
    o_ref[...] = _dot(_silu(c_ref[...]), w_ref[...], HI) + b_ref[...]


def _ada_mods(cond8, w, b):
    d, n = w.shape
    tn = n // 6
    return pl.pallas_call(
        _ada_kernel,
        grid=(6,),
        in_specs=[pl.BlockSpec((8, d), lambda j: (0, 0)),
                  pl.BlockSpec((d, tn), lambda j: (0, j)),
                  pl.BlockSpec((1, tn), lambda j: (0, j))],
        out_specs=pl.BlockSpec((8, tn), lambda j: (0, j)),
        out_shape=jax.ShapeDtypeStruct((8, n), F32),
        compiler_params=_params("arbitrary"),
        name="ada_mods",
    )(cond8, w, b.reshape(1, n))


def _inproj_kernel(x_ref, mod_ref, g_ref, cos_ref, sin_ref, wq_ref, wk_ref, wv_ref, wmqk_ref,
                   wmv_ref, wmo_ref, wg_ref, gb_ref,
                   q_ref, k_ref, v_ref, mqk_ref, mv_ref, og_ref, gate_ref):
    x = x_ref[0]
    h = _norm_mod(x, g_ref[...], mod_ref[0, 0:1, :], mod_ref[0, 1:2, :]).astype(BF16)
    tm = x.shape[0]
    width = q_ref.shape[2]
    cos = jnp.concatenate([cos_ref[...]] * (width // LANES), axis=1)
    sin = jnp.concatenate([sin_ref[...]] * (width // LANES), axis=1)
    lane = lax.broadcasted_iota(jnp.int32, (tm, width), 1)
    lower = (lane & (DA_QK_DIM - 1)) < (DA_QK_DIM // 2)

    def rope(u):
        swapped = jnp.where(lower, pltpu.roll(u, width - DA_QK_DIM // 2, 1),
                            pltpu.roll(u, DA_QK_DIM // 2, 1))
        return u * cos + swapped * sin

    q_ref[0] = (rope(_dot(h, wq_ref[...])) * LOG2_E).astype(BF16)
    k_ref[0] = rope(_dot(h, wk_ref[...])).astype(BF16)
    v = _dot(h, wv_ref[...])
    pad = VT_ROWS - DA_V_DIM
    ones_row = (lax.broadcasted_iota(jnp.int32, (pad, tm), 0) == 0).astype(BF16)
    for hd in range(DA_HEADS):
        v_ref[0, hd, 0, 0:DA_V_DIM, :] = v[:, hd * DA_V_DIM:(hd + 1) * DA_V_DIM].T.astype(BF16)
        v_ref[0, hd, 0, DA_V_DIM:VT_ROWS, :] = ones_row
    mqk_ref[0] = _dot(h, wmqk_ref[...]).astype(BF16)
    mv_ref[0] = _dot(h, wmv_ref[...]).astype(BF16)
    og_ref[0] = _sigmoid(_dot(h, wmo_ref[...])).astype(BF16)
    g = _dot(h, wg_ref[...]) + gb_ref[...]
    glane = lax.broadcasted_iota(jnp.int32, g.shape, 1)
    is_forget = ((glane // ML_HEADS) & 1) == 1
    log_sig = jnp.minimum(g, 0.0) - jnp.log(1.0 + jnp.exp(-jnp.abs(g)))
    gate_ref[0] = jnp.where(is_forget, log_sig, g)


def _inproj(xs, modtab, g, cos, sin, ws, gate_b, n_ctx):
    b, t, d = xs.shape
    tm = ROW_TILE
    nt = t // tm
    ctx_tiles = n_ctx // tm
    row = lambda w: pl.BlockSpec((1, tm, w), lambda bi, i: (bi, i, 0))
    full = lambda a: pl.BlockSpec(a.shape, lambda bi, i: (0,) * a.ndim)
    widths = [w.shape[1] for w in ws]
    out_dtypes = [BF16] * 6 + [F32]
    out_specs = [row(w) for w in widths]
    out_shape = [jax.ShapeDtypeStruct((b, t, w), dt) for w, dt in zip(widths, out_dtypes)]
    out_specs[2] = pl.BlockSpec((1, DA_HEADS, 1, VT_ROWS, tm), lambda bi, i: (bi, 0, i, 0, 0))
    out_shape[2] = jax.ShapeDtypeStruct((b, DA_HEADS, nt, VT_ROWS, tm), BF16)
    return pl.pallas_call(
        _inproj_kernel,
        grid=(b, nt),
        in_specs=[row(d),
                  pl.BlockSpec((1, 8, d), lambda bi, i: (2 * bi + (i >= ctx_tiles).astype(jnp.int32), 0, 0)),
                  full(g),
                  pl.BlockSpec((tm, LANES), lambda bi, i: (i, 0)),
                  pl.BlockSpec((tm, LANES), lambda bi, i: (i, 0))]
                 + [full(w) for w in ws] + [full(gate_b)],
        out_specs=out_specs,
        out_shape=out_shape,
        compiler_params=_params("parallel", "parallel"),
        name="inproj",
    )(xs, modtab, g, cos, sin, *ws, gate_b)


def _split3(x):
    x1 = x.astype(BF16)
    r1 = x - x1.astype(F32)
    x2 = r1.astype(BF16)
    x3 = (r1 - x2.astype(F32)).astype(BF16)
    return x1, x2, x3


VT_ROWS = ML_DIM + 16


def _mlprep_kernel(cur_ref, prev_ref, next_ref, cw_ref, gate_ref, v_ref,
                   mq_ref, mk_ref, gc_ref, gr_ref, vt_ref, *, ctx_tiles, n_tiles):
    i = pl.program_id(1)
    cur = cur_ref[0].astype(F32)
    tm, w = cur.shape
    prev_ok = i != ctx_tiles
    if ctx_tiles > 0:
        prev_ok = jnp.logical_and(prev_ok, i != 0)
        next_ok = jnp.logical_and(i != ctx_tiles - 1, i != n_tiles - 1)
    else:
        next_ok = i != n_tiles - 1
    prev_row = jnp.where(prev_ok, prev_ref[0, 7:8, :].astype(F32), 0.0)
    next_row = jnp.where(next_ok, next_ref[0, 0:1, :].astype(F32), 0.0)
    ridx = lax.broadcasted_iota(jnp.int32, (tm, w), 0)
    before = jnp.where(ridx == 0, prev_row, pltpu.roll(cur, 1, 0))
    after = jnp.where(ridx == tm - 1, next_row, pltpu.roll(cur, tm - 1, 0))
    y = _silu(before * cw_ref[0:1, :] + cur * cw_ref[1:2, :] + after * cw_ref[2:3, :])
    half = w // 2
    mq_ref[0] = y[:, :half].astype(BF16)
    mk_ref[0] = (y[:, half:] * (ML_DIM ** -0.5)).astype(BF16)

    n_g = 4 * ML_HEADS
    hds = ML_HEADS
    r = lax.broadcasted_iota(jnp.int32, (ML_CHUNK, ML_CHUNK), 0)
    c = lax.broadcasted_iota(jnp.int32, (ML_CHUNK, ML_CHUNK), 1)
    lower = (c <= r).astype(BF16)
    upper = (c >= r).astype(BF16)
    ones_row = (r == 0).astype(BF16)[0:VT_ROWS - ML_DIM]
    for ci in range(tm // ML_CHUNK):
        rows = slice(ci * ML_CHUNK, (ci + 1) * ML_CHUNK)
        gm = jnp.where(c < n_g, gate_ref[0, rows, :], 0.0)
        pre = sum(_dot(lower, p) for p in _split3(pltpu.roll(gm, n_g, 1)))
        suf = sum(_dot(upper, p) for p in _split3(pltpu.roll(gm, 2 * n_g, 1)))
        col = gm + pre + suf
        u_f = pltpu.roll(col, 3 * n_g, 1) - pltpu.roll(col, 3 * n_g - (n_g + hds), 1)
        u_b = pltpu.roll(col, 3 * n_g - hds, 1) - pltpu.roll(col, 2 * hds, 1)
        col = col + jnp.where(jnp.logical_and(c >= 3 * n_g, c < 3 * n_g + hds), u_f,
                              jnp.where(jnp.logical_and(c >= 3 * n_g + hds, c < 3 * n_g + 2 * hds),
                                        u_b, 0.0))
        gc_ref[0, rows, :] = col
        gr_ref[0, ci] = col.T
        for hd in range(ML_HEADS):
            v_t = v_ref[0, rows, hd * ML_DIM:(hd + 1) * ML_DIM].astype(F32).T.astype(BF16)
            vt_ref[0, ci, hd * VT_ROWS:hd * VT_ROWS + ML_DIM, :] = v_t
            vt_ref[0, ci, hd * VT_ROWS + ML_DIM:(hd + 1) * VT_ROWS, :] = ones_row


def _mlprep(mqk, conv_w8, gates, mv, n_ctx):
    b, t, w = mqk.shape
    tm = ROW_TILE
    nt = t // tm
    sub = tm // 8
    nsub = t // 8
    cpt = tm // ML_CHUNK
    kern = functools.partial(_mlprep_kernel, ctx_tiles=n_ctx // tm, n_tiles=nt)
    return pl.pallas_call(
        kern,
        grid=(b, nt),
        in_specs=[pl.BlockSpec((1, tm, w), lambda bi, i: (bi, i, 0)),
                  pl.BlockSpec((1, 8, w), lambda bi, i: (bi, jnp.maximum(i * sub - 1, 0), 0)),
                  pl.BlockSpec((1, 8, w), lambda bi, i: (bi, jnp.minimum((i + 1) * sub, nsub - 1), 0)),
                  pl.BlockSpec(conv_w8.shape, lambda bi, i: (0, 0)),
                  pl.BlockSpec((1, tm, LANES), lambda bi, i: (bi, i, 0)),
                  pl.BlockSpec((1, tm, w // 2), lambda bi, i: (bi, i, 0))],
        out_specs=[pl.BlockSpec((1, tm, w // 2), lambda bi, i: (bi, i, 0)),
                   pl.BlockSpec((1, tm, w // 2), lambda bi, i: (bi, i, 0)),
                   pl.BlockSpec((1, tm, LANES), lambda bi, i: (bi, i, 0)),
                   pl.BlockSpec((1, cpt, ML_CHUNK, LANES), lambda bi, i: (bi, i, 0, 0)),
                   pl.BlockSpec((1, cpt, ML_HEADS * VT_ROWS, ML_CHUNK), lambda bi, i: (bi, i, 0, 0))],
        out_shape=[jax.ShapeDtypeStruct((b, t, w // 2), BF16),
                   jax.ShapeDtypeStruct((b, t, w // 2), BF16),
                   jax.ShapeDtypeStruct((b, t, LANES), F32),
                   jax.ShapeDtypeStruct((b, t // ML_CHUNK, ML_CHUNK, LANES), F32),
                   jax.ShapeDtypeStruct((b, t // ML_CHUNK, ML_HEADS * VT_ROWS, ML_CHUNK), BF16)],
        compiler_params=_params("parallel", "parallel"),
        name="mlstm_prep",
    )(mqk, mqk, mqk, conv_w8, gates, mv)


def _mlstm_kernel(qf_ref, kf_ref, vf_ref, gcf_ref, grf_ref, qb_ref, kb_ref, vb_ref, gcb_ref, grb_ref,
                  hf_ref, hb_ref, *scr):
    s = pl.program_id(0)
    nb = qf_ref.shape[0]
    n_g = 4 * ML_HEADS
    state_refs = scr[:len(scr) // 2]
    m_refs = scr[len(scr) // 2:]

    @pl.when(s == 0)
    def _():
        for ref in scr:
            ref[...] = jnp.zeros(ref.shape, F32)

    ki = lax.broadcasted_iota(jnp.int32, (ML_CHUNK, ML_CHUNK), 0)
    qi = lax.broadcasted_iota(jnp.int32, (ML_CHUNK, ML_CHUNK), 1)

    for bi in range(nb):
        for direction in range(2):
            q_ref, k_ref, vt_ref, gc_ref, gr_ref, h_ref = (
                (qf_ref, kf_ref, vf_ref, gcf_ref, grf_ref, hf_ref) if direction == 0 else
                (qb_ref, kb_ref, vb_ref, gcb_ref, grb_ref, hb_ref))
            visible = (ki <= qi) if direction == 0 else (ki >= qi)
            gc = gc_ref[bi]
            gr = gr_ref[bi, 0]
            for hd in range(ML_HEADS):
                chain = (bi * 2 + direction) * ML_HEADS + hd
                c_lf = (2 * direction + 1) * ML_HEADS + hd
                c_cs = c_lf + (n_g if direction == 0 else 2 * n_g)
                c_u = 3 * n_g + direction * ML_HEADS + hd
                lo, hi = hd * ML_DIM, (hd + 1) * ML_DIM
                q = q_ref[bi, :, lo:hi]
                k = k_ref[bi, :, lo:hi]
                v_t = vt_ref[bi, 0, hd * VT_ROWS:(hd + 1) * VT_ROWS, :]
                u_row = gr[c_u:c_u + 1, :]
                lf_row = gr[c_lf:c_lf + 1, :]
                cs_row = gr[c_cs:c_cs + 1, :]
                m_st = m_refs[chain][:, 0:1]
                st_t = state_refs[chain][...]

                u_vis = jnp.where(visible, jnp.broadcast_to(gc[:, c_u:c_u + 1], visible.shape), -jnp.inf)
                v_row = jnp.maximum(m_st, jnp.max(u_vis, axis=0, keepdims=True))
                d_t = jnp.exp(u_vis - v_row)
                inter = jnp.exp(m_st - v_row)
                sc_t = (_dot_nt(k, q) * d_t).astype(BF16)
                tot_t = inter * _dot_nt(st_t.astype(BF16), q) + _dot(v_t, sc_t)
                den = jnp.maximum(jnp.abs(tot_t[ML_DIM:ML_DIM + 1, :]), jnp.exp(-(cs_row + v_row)))
                h_ref[bi, :, lo:hi] = (tot_t[:ML_DIM, :] / den).T

                b_last = jnp.sum(lf_row, axis=1, keepdims=True)
                m_sc = jnp.maximum(m_st, jnp.max(u_row, axis=1, keepdims=True))
                wgt = jnp.exp(u_row - m_sc)
                vw_t = (v_t.astype(F32) * wgt).astype(BF16)
                state_refs[chain][...] = jnp.exp(m_st - m_sc) * st_t + _dot(vw_t, k)
                m_refs[chain][...] = jnp.broadcast_to(b_last + m_sc, (1, LANES))


def _mlstm(mq, mk, mvt, gc, gr, n_ctx):
    b, t, w = mq.shape
    nc = t // ML_CHUNK
    ncc = n_ctx // ML_CHUNK
    fwd = lambda s: s
    bwd = lambda s: jnp.where(s < ncc, ncc - 1 - s, nc - 1 - s + ncc)
    tok = lambda f: pl.BlockSpec((b, ML_CHUNK, w), lambda s: (0, f(s), 0))
    gcs = lambda f: pl.BlockSpec((b, ML_CHUNK, LANES), lambda s: (0, f(s), 0))
    grs = lambda f: pl.BlockSpec((b, 1, ML_CHUNK, LANES), lambda s: (0, f(s), 0, 0))
    vts = lambda f: pl.BlockSpec((b, 1, ML_HEADS * VT_ROWS, ML_CHUNK), lambda s: (0, f(s), 0, 0))
    n_chain = b * 2 * ML_HEADS
    return pl.pallas_call(
        _mlstm_kernel,
        grid=(nc,),
        in_specs=[tok(fwd), tok(fwd), vts(fwd), gcs(fwd), grs(fwd),
                  tok(bwd), tok(bwd), vts(bwd), gcs(bwd), grs(bwd)],
        out_specs=[tok(fwd), tok(bwd)],
        out_shape=[jax.ShapeDtypeStruct((b, t, w), F32)] * 2,
        scratch_shapes=[pltpu.VMEM((VT_ROWS, ML_DIM), F32)] * n_chain
                       + [pltpu.VMEM((1, LANES), F32)] * n_chain,
        compiler_params=_params("arbitrary"),
        name="mlstm",
    )(mq, mk, mvt, gc, gr, mq, mk, mvt, gc, gr)


def _attn_kernel(q_ref, qn_ref, k_ref, vt_ref, lam_ref, sg_ref, o_ref, qm_scr, *scr,
                 ctx_tiles, n_ctx, key_block, n_blocks, lam_init):
    s_scr = (scr[0:2], scr[2:4])
    bm_scr = (scr[4:6], scr[6:8])
    m_scr = scr[8:10]
    acc_scr = scr[10:12]
    i = pl.program_id(2)
    dv = DA_V_DIM
    chunk = vt_ref.shape[4]
    q = q_ref[0]
    lane = lax.broadcasted_iota(jnp.int32, q.shape, 1)
    zero = jnp.zeros_like(q)

    def map_halves(qv):
        return jnp.where(lane < DA_QK_DIM, qv, zero), jnp.where(lane >= DA_QK_DIM, qv, zero)

    qm_scr[0], qm_scr[1] = map_halves(q)
    align = math.gcd(n_ctx, key_block)

    def scores(mp, start, size, q_maps=None):
        q_map = qm_scr[mp] if q_maps is None else q_maps[mp]
        return _dot_nt(k_ref[0, pl.ds(start, size), :], q_map)

    def weighted_values(sc, m, first_chunk):
        p = jnp.exp2(sc - m).astype(BF16)
        out = None
        for ci in range(sc.shape[0] // chunk):
            part = _dot(vt_ref[0, 0, first_chunk + ci], p[ci * chunk:(ci + 1) * chunk, :])
            out = part if out is None else out + part
        return out

    for mp in range(2):
        sc = scores(mp, 0, n_ctx)
        m = jnp.max(sc, axis=0, keepdims=True)
        m_scr[mp][...] = m
        acc_scr[mp][...] = weighted_values(sc, m, 0)

    def stage(j, slot, q_maps=None):
        start = pl.multiple_of(n_ctx + j * key_block, align)
        for mp in range(2):
            sc = scores(mp, start, key_block, q_maps)
            s_scr[mp][slot][...] = sc
            bm_scr[mp][slot][...] = jnp.max(sc, axis=0, keepdims=True)

    def consume(j, slot):
        first_chunk = (n_ctx + j * key_block) // chunk
        for mp in range(2):
            m_old = m_scr[mp][...]
            m_new = jnp.maximum(m_old, bm_scr[mp][slot][...])
            alpha = jnp.exp2(m_old - m_new)
            pv = weighted_values(s_scr[mp][slot][...], m_new, first_chunk)
            acc_scr[mp][...] = alpha * acc_scr[mp][...] + pv
            m_scr[mp][...] = m_new

    @pl.when(i >= ctx_tiles)
    def _():
        @pl.when(i == ctx_tiles)
        def _():
            stage(0, 0)

        def body(g, carry):
            stage(2 * g + 1, 1)
            consume(2 * g, 0)
            stage(2 * g + 2, 0)
            consume(2 * g + 1, 1)
            return carry
        lax.fori_loop(0, n_blocks // 2 - 1, body, 0)
        stage(n_blocks - 1, 1)
        consume(n_blocks - 2, 0)
        stage(0, 0, map_halves(qn_ref[0]))
        consume(n_blocks - 1, 1)

    lv = lam_ref[...]
    dot01 = jnp.sum(lv[0:1, :] * lv[1:2, :], axis=1, keepdims=True)
    dot23 = jnp.sum(lv[2:3, :] * lv[3:4, :], axis=1, keepdims=True)
    lam = jnp.exp(dot01) - jnp.exp(dot23) + lam_init
    a0 = acc_scr[0][...]
    a1 = acc_scr[1][...]
    o = (a0[0:dv, :] / a0[dv:dv + 1, :] - lam * (a1[0:dv, :] / a1[dv:dv + 1, :])).T
    o = o * lax.rsqrt(jnp.mean(o * o, axis=-1, keepdims=True) + NORM_EPS) * sg_ref[...]
    o_ref[0] = (o * (1.0 - lam_init)).astype(BF16)


def _diff_attention(q, k, vt, lam8, subln_g, n_ctx, lam_init):
    b, t, w = q.shape
    tq = ROW_TILE
    n_lat = t - n_ctx
    key_block = 1024 if n_lat % 2048 == 0 else 512
    n_blocks = n_lat // key_block
    _, _, n_chunks, vt_rows, chunk = vt.shape
    assert n_lat % (2 * key_block) == 0 and n_ctx % tq == 0 and n_ctx % chunk == 0
    assert key_block % chunk == 0
    kern = functools.partial(_attn_kernel, ctx_tiles=n_ctx // tq, n_ctx=n_ctx, key_block=key_block,
                             n_blocks=n_blocks, lam_init=lam_init)
    return pl.pallas_call(
        kern,
        grid=(b, DA_HEADS, t // tq),
        in_specs=[pl.BlockSpec((1, tq, LANES), lambda bi, h, i: (bi, i, h)),
                  pl.BlockSpec((1, tq, LANES), lambda bi, h, i: (bi, jnp.minimum(i + 1, t // tq - 1), h)),
                  pl.BlockSpec((1, t, LANES), lambda bi, h, i: (bi, 0, h)),
                  pl.BlockSpec((1, 1, n_chunks, vt_rows, chunk), lambda bi, h, i: (bi, h, 0, 0, 0)),
                  pl.BlockSpec(lam8.shape, lambda bi, h, i: (0, 0)),
                  pl.BlockSpec(subln_g.shape, lambda bi, h, i: (0, 0))],
        out_specs=pl.BlockSpec((1, tq, LANES), lambda bi, h, i: (bi, i, h)),
        out_shape=jax.ShapeDtypeStruct((b, t, w), BF16),
        scratch_shapes=[pltpu.VMEM((2, tq, LANES), BF16)]
                       + [pltpu.VMEM((key_block, tq), F32)] * 4
                       + [pltpu.VMEM((1, tq), F32)] * 6
                       + [pltpu.VMEM((vt_rows, tq), F32)] * 2,
        compiler_params=_params("arbitrary", "arbitrary", "arbitrary"),
        name="diff_attention",
    )(q, q, k, vt, lam8, subln_g)


def _top2_sum(a, b, c, d):
    hi1, lo1 = jnp.maximum(a, b), jnp.minimum(a, b)
    hi2, lo2 = jnp.maximum(c, d), jnp.minimum(c, d)
    return jnp.maximum(hi1, hi2) + jnp.maximum(jnp.minimum(hi1, hi2), jnp.maximum(lo1, lo2))


def _route(f, rwt_ref, rb_ref, cnt_ref):
    tm = f.shape[0]
    aff = _sigmoid(_dot_nt(rwt_ref[...], f, HI))
    biased = aff + rb_ref[:, 0:1]
    bz = [biased[e:e + 1, :] for e in range(N_EXPERTS)]
    af = [aff[e:e + 1, :] for e in range(N_EXPERTS)]
    n_grp = N_EXPERTS // EXPERTS_PER_GROUP
    scores = [_top2_sum(*bz[EXPERTS_PER_GROUP * g:EXPERTS_PER_GROUP * (g + 1)]) for g in range(n_grp)]
    best = scores[0]
    sel_grp = jnp.zeros_like(best, dtype=jnp.int32)
    for g in range(1, n_grp):
        better = scores[g] > best
        sel_grp = jnp.where(better, g, sel_grp)
        best = jnp.where(better, scores[g], best)
    chosen = []
    for e in range(N_EXPERTS):
        g = e // EXPERTS_PER_GROUP
        rank = jnp.zeros_like(sel_grp)
        for o in range(EXPERTS_PER_GROUP * g, EXPERTS_PER_GROUP * (g + 1)):
            if o == e:
                continue
            beats = (bz[o] > bz[e]) if o > e else (bz[o] >= bz[e])
            rank = rank + beats.astype(jnp.int32)
        chosen.append(jnp.logical_and(sel_grp == g, rank < 2))
    denom = sum(jnp.where(chosen[e], af[e], 0.0) for e in range(N_EXPERTS))
    erow = lax.broadcasted_iota(jnp.int32, (N_EXPERTS, tm), 0)
    one_hot = jnp.zeros((N_EXPERTS, tm), F32)
    for e in range(N_EXPERTS):
        one_hot = jnp.where(jnp.logical_and(erow == e, chosen[e]), 1.0, one_hot)
    earlier = (lax.broadcasted_iota(jnp.int32, (tm, tm), 0)
               < lax.broadcasted_iota(jnp.int32, (tm, tm), 1)).astype(BF16)
    rank_all = _dot(one_hot.astype(BF16), earlier) + cnt_ref[:, 0:1]
    cnt_ref[...] = cnt_ref[...] + jnp.sum(one_hot, axis=1, keepdims=True)

    seen = jnp.zeros((1, tm), jnp.bool_)
    e_a = e_b = jnp.zeros((1, tm), jnp.int32)
    r_a = r_b = w_a = w_b = jnp.zeros((1, tm), F32)
    for e in range(N_EXPERTS):
        first = jnp.logical_and(chosen[e], jnp.logical_not(seen))
        second = jnp.logical_and(chosen[e], seen)
        rk = rank_all[e:e + 1, :]
        wt = af[e] / denom
        e_a, e_b = jnp.where(first, e, e_a), jnp.where(second, e, e_b)
        r_a, r_b = jnp.where(first, rk, r_a), jnp.where(second, rk, r_b)
        w_a, w_b = jnp.where(first, wt, w_a), jnp.where(second, wt, w_b)
        seen = jnp.logical_or(seen, chosen[e])
    r8 = lax.broadcasted_iota(jnp.int32, (8, tm), 0)
    sel = jnp.where(r8 == 0, e_a, jnp.where(r8 == 1, e_b, jnp.where(
        r8 == 2, r_a.astype(jnp.int32), jnp.where(r8 == 3, r_b.astype(jnp.int32), 0))))
    row = lax.broadcasted_iota(jnp.int32, (LANES, tm), 0)
    w_t = jnp.where(row == 0, w_a, jnp.where(row == 1, w_b, 0.0))
    return sel, w_t.T


def _post_kernel(*refs, even):
    if even:
        (x_ref, a_ref, hf_ref, hb_ref, og_ref, mod_ref, wa_ref, wm_ref, gf_ref, rwt_ref, rb_ref,
         xo_ref, f_ref, sel_ref, w_ref, cnt_ref, cnt_scr) = refs
        m = ((hf_ref[0] + hb_ref[0]) * og_ref[0].astype(F32)).astype(BF16)
        o = _dot(a_ref[0], wa_ref[...]) + _dot(m, wm_ref[...])
    else:
        (x_ref, a_ref, mod_ref, wa_ref, gf_ref, rwt_ref, rb_ref,
         xo_ref, f_ref, sel_ref, w_ref, cnt_ref, cnt_scr) = refs
        o = _dot(a_ref[0].astype(BF16), wa_ref[...])

    @pl.when(jnp.logical_and(pl.program_id(0) == 0, pl.program_id(1) == 0))
    def _():
        cnt_scr[...] = jnp.zeros(cnt_scr.shape, F32)

    x = x_ref[0] + mod_ref[0, 2:3, :] * o
    xo_ref[0] = x
    f = _norm_mod(x, gf_ref[...], mod_ref[0, 3:4, :], mod_ref[0, 4:5, :])
    _store_row_tiles(f_ref.at[0], f)
    sel, w_col = _route(f, rwt_ref, rb_ref, cnt_scr)
    sel_ref[0, 0] = sel
    w_ref[0] = w_col
    cnt_ref[...] = cnt_scr[...]


def _post_mixer(x, acts, weights, modtab, gffn, rwt, rb, n_ctx, x_row_off, even):
    b, t, _ = acts[0].shape
    d = x.shape[2]
    tm = ROW_TILE
    nt = t // tm
    ctx_tiles = n_ctx // tm
    off = x_row_off // tm
    full = lambda a: pl.BlockSpec(a.shape, lambda bi, i: (0,) * a.ndim)
    row = lambda w: pl.BlockSpec((1, tm, w), lambda bi, i: (bi, i, 0))
    mod_spec = pl.BlockSpec(
        (1, 8, d), lambda bi, i: (2 * bi + (i + off >= ctx_tiles).astype(jnp.int32), 0, 0))
    in_specs = ([pl.BlockSpec((1, tm, d), lambda bi, i: (bi, i + off, 0))]
                + [row(a.shape[2]) for a in acts] + [mod_spec]
                + [full(w) for w in weights] + [full(gffn), full(rwt), full(rb)])
    return pl.pallas_call(
        functools.partial(_post_kernel, even=even),
        grid=(b, nt),
        in_specs=in_specs,
        out_specs=[row(d), pl.BlockSpec((1, tm * d // LANES, LANES), lambda bi, i: (bi, i, 0)),
                   pl.BlockSpec((1, 1, 8, tm), lambda bi, i: (bi, i, 0, 0)),
                   row(LANES),
                   pl.BlockSpec((N_EXPERTS, LANES), lambda bi, i: (0, 0))],
        out_shape=[jax.ShapeDtypeStruct((b, t, d), F32),
                   jax.ShapeDtypeStruct((b, t * d // LANES, LANES), F32),
                   jax.ShapeDtypeStruct((b, nt, 8, tm), jnp.int32),
                   jax.ShapeDtypeStruct((b, t, LANES), F32),
                   jax.ShapeDtypeStruct((N_EXPERTS, LANES), F32)],
        scratch_shapes=[pltpu.VMEM((N_EXPERTS, LANES), F32)],
        compiler_params=_params("arbitrary", "arbitrary"),
        name="post_mixer_even" if even else "post_mixer_odd",
    )(x, *acts, modtab, *weights, gffn, rwt, rb)


EXPERT_ROW_TILE = 256
DMA_ISSUE_UNROLL = 8


def _moe_plan(sel, counts, n_tok):
    e_a, e_b, r_a, r_b = (sel[:, :, k, :].reshape(-1) for k in range(4))
    cnt = counts[:, 0].astype(jnp.int32)
    padded = ((cnt + EXPERT_ROW_TILE - 1) // EXPERT_ROW_TILE) * EXPERT_ROW_TILE
    ends = jnp.cumsum(padded)
    starts = ends - padded
    pos = jnp.concatenate([starts[e_a] + r_a, starts[e_b] + r_b]).astype(jnp.int32)
    n_tiles = 2 * n_tok // EXPERT_ROW_TILE + N_EXPERTS
    tile_start = jnp.arange(n_tiles, dtype=jnp.int32) * EXPERT_ROW_TILE
    tile_expert = jnp.minimum(jnp.sum(tile_start[:, None] >= ends[None, :], axis=1),
                              N_EXPERTS - 1).astype(jnp.int32)
    tiles_used = (ends[-1:] // EXPERT_ROW_TILE).astype(jnp.int32)
    return pos, tile_expert, tiles_used, n_tiles


def _store_row_tiles(ref, x):
    rows, d = x.shape
    n_sub = d // LANES
    for s in range(n_sub):
        ref[pl.ds(s, rows, stride=n_sub), :] = x[:, s * LANES:(s + 1) * LANES]


def _load_row_tiles(ref, rows):
    n_sub = ref.shape[0] // rows
    return jnp.concatenate([ref[pl.ds(s, rows, stride=n_sub), :] for s in range(n_sub)], axis=1)


def _dispatch_kernel(pos_ref, f_ref, init_ref, out_ref, sem, *, n_tok, tm):
    del init_ref
    n_sub = f_ref.shape[1] // tm
    base = (pl.program_id(0) * pl.num_programs(1) + pl.program_id(1)) * tm

    def row_copy(r, k):
        dst = pos_ref[k * n_tok + base + r]
        src = f_ref.at[0, pl.ds(pl.multiple_of(r * n_sub, n_sub), n_sub), :]
        return pltpu.make_async_copy(src, out_ref.at[dst], sem)

    def issue(r, carry):
        row_copy(r, 0).start()
        row_copy(r, 1).start(priority=1)
        return carry

    lax.fori_loop(0, tm, issue, 0, unroll=DMA_ISSUE_UNROLL)
    for _ in range(2):
        pltpu.make_async_copy(out_ref.at[pl.ds(0, tm)], out_ref.at[pl.ds(0, tm)], sem).wait()


def _dispatch(pos, f, n_rows, n_tok):
    b, rows, _ = f.shape
    n_sub = rows * b // n_tok
    tm = ROW_TILE
    grid_spec = pltpu.PrefetchScalarGridSpec(
        num_scalar_prefetch=1,
        grid=(b, n_tok // b // tm),
        in_specs=[pl.BlockSpec((1, tm * n_sub, LANES), lambda bi, i, pos_ref: (bi, i, 0)),
                  pl.BlockSpec(memory_space=pl.ANY)],
        out_specs=pl.BlockSpec(memory_space=pl.ANY),
        scratch_shapes=[pltpu.SemaphoreType.DMA(())])
    return pl.pallas_call(
        functools.partial(_dispatch_kernel, n_tok=n_tok, tm=tm),
        grid_spec=grid_spec,
        out_shape=jax.ShapeDtypeStruct((n_rows, n_sub, LANES), F32),
        input_output_aliases={2: 0},
        compiler_params=_params("arbitrary", "arbitrary"),
        name="moe_dispatch",
    )(pos, f, jnp.zeros((n_rows, n_sub, LANES), F32))


def _expert_ffn_kernel(te_ref, used_ref, x_ref, wg_ref, wu_ref, wd_ref, y_ref):
    del te_ref
    live = pl.program_id(0) < used_ref[0]

    @pl.when(live)
    def _():
        xb = _load_row_tiles(x_ref, EXPERT_ROW_TILE).astype(BF16)
        he = _silu(_dot(xb, wg_ref[0])) * _dot(xb, wu_ref[0])
        _store_row_tiles(y_ref, _dot(he.astype(BF16), wd_ref[0]))

    @pl.when(jnp.logical_not(live))
    def _():
        y_ref[...] = jnp.zeros(y_ref.shape, F32)


def _expert_ffn(tile_expert, tiles_used, xs, wg, wu, wd, n_tiles):
    n_rows, n_sub, _ = xs.shape
    _, d, d_e = wg.shape
    tr = EXPERT_ROW_TILE
    grid_spec = pltpu.PrefetchScalarGridSpec(
        num_scalar_prefetch=2,
        grid=(n_tiles,),
        in_specs=[pl.BlockSpec((tr * n_sub, LANES), lambda j, te, used: (j, 0)),
                  pl.BlockSpec((1, d, d_e), lambda j, te, used: (te[j], 0, 0)),
                  pl.BlockSpec((1, d, d_e), lambda j, te, used: (te[j], 0, 0)),
                  pl.BlockSpec((1, d_e, d), lambda j, te, used: (te[j], 0, 0))],
        out_specs=pl.BlockSpec((tr * n_sub, LANES), lambda j, te, used: (j, 0)))
    ys = pl.pallas_call(
        _expert_ffn_kernel,
        grid_spec=grid_spec,
        out_shape=jax.ShapeDtypeStruct((n_rows * n_sub, LANES), F32),
        compiler_params=_params("arbitrary"),
        name="moe_expert_ffn",
    )(tile_expert, tiles_used, xs.reshape(n_rows * n_sub, LANES), wg, wu, wd)
    return ys.reshape(n_rows, n_sub, LANES)


def _combine_kernel(pos_ref, y_ref, w_ref, x_ref, modc_ref, modl_ref, fg_ref, o_ref, buf, sem,
                    *, n_tok, n_ctx, final_norm):
    i = pl.program_id(1)
    tm = x_ref.shape[1]
    base = (pl.program_id(0) * pl.num_programs(1) + i) * tm

    n_sub = y_ref.shape[1]

    def row_copy(r, k):
        src = pos_ref[k * n_tok + base + r]
        dst = buf.at[k, pl.ds(pl.multiple_of(r * n_sub, n_sub), n_sub), :]
        return pltpu.make_async_copy(y_ref.at[src], dst, sem)

    def issue(r, carry):
        row_copy(r, 0).start()
        row_copy(r, 1).start(priority=1)
        return carry

    lax.fori_loop(0, tm, issue, 0, unroll=DMA_ISSUE_UNROLL)
    for _ in range(2):
        pltpu.make_async_copy(y_ref.at[pl.ds(0, tm)], y_ref.at[pl.ds(0, tm)], sem).wait()
    w = w_ref[0]
    y = _load_row_tiles(buf.at[0], tm) * w[:, 0:1] + _load_row_tiles(buf.at[1], tm) * w[:, 1:2]
    rows = i * tm + lax.broadcasted_iota(jnp.int32, (tm, 1), 0)
    gate = jnp.where(rows < n_ctx, modc_ref[0, 5:6, :], modl_ref[0, 5:6, :])
    out = x_ref[0] + gate * y
    if final_norm:
        out = out * lax.rsqrt(jnp.mean(out * out, axis=-1, keepdims=True) + NORM_EPS) * fg_ref[...]
    o_ref[0] = out


def _combine(pos, ys, w_col, xmid, modtab, final_g, n_ctx, final_norm):
    b, t, d = xmid.shape
    tm = ROW_TILE
    row = lambda w: pl.BlockSpec((1, tm, w), lambda bi, i, pos_ref: (bi, i, 0))
    grid_spec = pltpu.PrefetchScalarGridSpec(
        num_scalar_prefetch=1,
        grid=(b, t // tm),
        in_specs=[pl.BlockSpec(memory_space=pl.ANY), row(LANES), row(d),
                  pl.BlockSpec((1, 8, d), lambda bi, i, pos_ref: (2 * bi, 0, 0)),
                  pl.BlockSpec((1, 8, d), lambda bi, i, pos_ref: (2 * bi + 1, 0, 0)),
                  pl.BlockSpec(final_g.shape, lambda bi, i, pos_ref: (0, 0))],
        out_specs=row(d),
        scratch_shapes=[pltpu.VMEM((2, tm * d // LANES, LANES), F32), pltpu.SemaphoreType.DMA(())])
    return pl.pallas_call(
        functools.partial(_combine_kernel, n_tok=b * t, n_ctx=n_ctx, final_norm=final_norm),
        grid_spec=grid_spec,
        out_shape=jax.ShapeDtypeStruct((b, t, d), F32),
        compiler_params=_params("arbitrary", "arbitrary"),
        name="moe_combine",
    )(pos, ys, w_col, xmid, modtab, modtab, final_g)


def _moe(f, sel, w_col, counts, wg, wu, wd, xmid, modtab, final_g, n_ctx, final_norm):
    b, t, _ = xmid.shape
    pos, tile_expert, tiles_used, n_tiles = _moe_plan(sel, counts, b * t)
    xs = _dispatch(pos, f, n_tiles * EXPERT_ROW_TILE, b * t)
    ys = _expert_ffn(tile_expert, tiles_used, xs, wg, wu, wd, n_tiles)
    return _combine(pos, ys, w_col, xmid, modtab, final_g, n_ctx, final_norm)


def _chan_dft_kernel(x_ref, mod_ref, g_ref, w_ref, zr_ref, zi_ref):
    h = _norm_mod(x_ref[0], g_ref[...], mod_ref[0, 0:1, :], mod_ref[0, 1:2, :])
    gd = w_ref.shape[0]
    for gi in range(h.shape[1] // gd):
        z = _dot_split(h[:, gi * gd:(gi + 1) * gd], w_ref[...])
        zr_ref[0, :, gi * gd:(gi + 1) * gd] = z[:, :gd]
        zi_ref[0, :, gi * gd:(gi + 1) * gd] = z[:, gd:]


def _chan_dft(x, modtab, g, w_cs, x_row_off, t):
    b, _, d = x.shape
    tm = ROW_TILE
    off = x_row_off // tm
    row = pl.BlockSpec((1, tm, d), lambda bi, i: (bi, i, 0))
    return pl.pallas_call(
        _chan_dft_kernel,
        grid=(b, t // tm),
        in_specs=[pl.BlockSpec((1, tm, d), lambda bi, i: (bi, i + off, 0)),
                  pl.BlockSpec((1, 8, d), lambda bi, i: (2 * bi + 1, 0, 0)),
                  pl.BlockSpec(g.shape, lambda bi, i: (0, 0)),
                  pl.BlockSpec(w_cs.shape, lambda bi, i: (0, 0))],
        out_specs=[row, row],
        out_shape=[jax.ShapeDtypeStruct((b, t, d), F32)] * 2,
        compiler_params=_params("parallel", "parallel"),
        name="chan_dft",
    )(x, modtab, g, w_cs)


DFT_SUB = 8


def _dft1_kernel(zr_ref, zi_ref, w_ref, yr_ref, yi_ref):
    _, n1, sub, cols = zr_ref.shape
    z = jnp.concatenate([jnp.concatenate([ref[0, :, j, :] for j in range(sub)], axis=1)
                         for ref in (zr_ref, zi_ref)], axis=0)
    y = _dot_split(w_ref[...], z)
    for j in range(sub):
        yr_ref[0, :, j, :] = y[:n1, j * cols:(j + 1) * cols]
        yi_ref[0, :, j, :] = y[n1:, j * cols:(j + 1) * cols]


def _dft1(zr, zi, w1):
    b, n1, n2, d = zr.shape
    cols = d // 2
    blk = pl.BlockSpec((1, n1, DFT_SUB, cols), lambda bi, j, c: (bi, 0, j, c))
    return pl.pallas_call(
        _dft1_kernel,
        grid=(b, n2 // DFT_SUB, d // cols),
        in_specs=[blk, blk, pl.BlockSpec(w1.shape, lambda bi, j, c: (0, 0))],
        out_specs=[blk, blk],
        out_shape=[jax.ShapeDtypeStruct(zr.shape, F32)] * 2,
        compiler_params=_params("parallel", "parallel", "parallel"),
        name="dft_stage1",
    )(zr, zi, w1)


def _dft2_kernel(yr_ref, yi_ref, tab_ref, o_ref):
    for j in range(yr_ref.shape[1]):
        y = jnp.concatenate([yr_ref[0, j], yi_ref[0, j]], axis=0)
        o_ref[0, :, j, :] = _dot_split(tab_ref[j], y)


def _dft2(yr, yi, tab):
    b, n1, n2, d = yr.shape
    blk = pl.BlockSpec((1, DFT_SUB, n2, d), lambda bi, k1: (bi, k1, 0, 0))
    return pl.pallas_call(
        _dft2_kernel,
        grid=(b, n1 // DFT_SUB),
        in_specs=[blk, blk, pl.BlockSpec((DFT_SUB, n2, 2 * n2), lambda bi, k1: (k1, 0, 0))],
        out_specs=pl.BlockSpec((1, n2, DFT_SUB, d), lambda bi, k1: (bi, 0, k1, 0)),
        out_shape=jax.ShapeDtypeStruct((b, n2, n1, d), F32),
        compiler_params=_params("parallel", "parallel"),
        name="dft_stage2",
    )(yr, yi, tab)


def _dft_tables(t, gd):
    n2 = ML_CHUNK
    n1 = t // n2
    def cs(num, den):
        ang = (2.0 * np.pi / den) * (num % den).astype(np.float64)
        return np.cos(ang), np.sin(ang)
    c = np.arange(gd)
    cc, sc = cs(np.outer(c, c), gd)
    w_cs = np.concatenate([cc, -sc], axis=1)
    a = np.arange(n1)
    c1, s1 = cs(np.outer(a, a), n1)
    w1 = np.block([[c1, s1], [-s1, c1]])
    k = a[:, None, None] + n1 * np.arange(n2)[None, :, None]
    c2, s2 = cs(k * np.arange(n2)[None, None, :], t)
    tab = np.concatenate([c2, s2], axis=2) / math.sqrt(t * gd)
    return (jnp.asarray(w_cs, F32), jnp.asarray(w1, F32), jnp.asarray(tab, F32))


def _rope_tables(n_ctx, n_lat):
    pos = jnp.arange(n_lat, dtype=jnp.int32)
    n_axis = DA_QK_DIM // 4
    inv = ROPE_BASE ** (-jnp.arange(n_axis, dtype=F32) / n_axis)
    ang = jnp.concatenate([(pos // GRID_W).astype(F32)[:, None] * inv,
                           (pos % GRID_W).astype(F32)[:, None] * inv], axis=-1)
    cos, sin = jnp.cos(ang), jnp.sin(ang)
    cos = jnp.concatenate([jnp.ones((n_ctx, 2 * n_axis), F32), cos], axis=0)
    sin = jnp.concatenate([jnp.zeros((n_ctx, 2 * n_axis), F32), sin], axis=0)
    cos128 = jnp.concatenate([cos, cos, cos, cos], axis=1)
    sin128 = jnp.concatenate([-sin, sin, -sin, sin], axis=1)
    return cos128, sin128


def _deinterleave(w):
    d, n = w.shape
    w = w.reshape(d, n // DA_QK_DIM, DA_QK_DIM // 2, 2)
    return jnp.concatenate([w[..., 0], w[..., 1]], axis=-1).reshape(d, n)


def _pad_rows(a, rows):
    return jnp.concatenate([a, jnp.zeros((rows - a.shape[0],) + a.shape[1:], a.dtype)], axis=0)


def _pad_cols(a, cols):
    return jnp.concatenate([a, jnp.zeros(a.shape[:-1] + (cols - a.shape[-1],), a.dtype)], axis=-1)


def kernel(x, c, ctx, c_ctx, ada_w, ada_b, norm_mix_g, norm_ffn_g, even_w_in, even_w_out,
           even_conv_w, even_gate_b, even_lam, even_subln_g, odd_w_fnet, router_w, router_b,
           exp_w_gate, exp_w_up, exp_w_down, final_g):
    b, n_lat, d = x.shape
    n_ctx = ctx.shape[1]
    depth = ada_w.shape[0]
    assert depth == 2 and b + 1 <= 8
    assert n_ctx % ROW_TILE == 0 and n_lat % ROW_TILE == 0

    cond8 = _pad_rows(jnp.concatenate([c_ctx[None, :], c], axis=0), 8)
    rwt = router_w.T
    rb = jnp.broadcast_to(router_b[:, None], (N_EXPERTS, LANES))
    row2 = lambda v: v.reshape(1, -1)

    def modtab_for(layer):
        mods = _ada_mods(cond8, ada_w[layer], ada_b[layer]).reshape(8, 6, d)
        mods = jnp.concatenate([mods, jnp.zeros((8, 2, d), F32)], axis=1)
        idx = np.array([[0, 1 + bi] for bi in range(b)]).reshape(-1)
        return mods[idx]

    xs = jnp.concatenate([ctx, x], axis=1)
    modtab = modtab_for(0)
    w_in = even_w_in[0]
    o1 = DA_HEADS * 2 * DA_QK_DIM
    o2 = 2 * o1
    o3 = o2 + DA_HEADS * DA_V_DIM
    o4 = o3 + 2 * ML_HEADS * ML_DIM
    o5 = o4 + ML_HEADS * ML_DIM
    o6 = o5 + ML_HEADS * ML_DIM
    ws = [(_deinterleave(w_in[:, :o1]) * (DA_QK_DIM ** -0.5)).astype(BF16),
          _deinterleave(w_in[:, o1:o2]).astype(BF16),
          w_in[:, o2:o3].astype(BF16), w_in[:, o3:o4].astype(BF16),
          w_in[:, o4:o5].astype(BF16), w_in[:, o5:o6].astype(BF16),
          _pad_cols(w_in[:, o6:], LANES).astype(BF16)]
    gate_b = _pad_cols(even_gate_b[0].reshape(1, -1), LANES)
    cos128, sin128 = _rope_tables(n_ctx, n_lat)
    daq, dak, dav, mqk, mv, og, gates = _inproj(xs, modtab, row2(norm_mix_g[0]), cos128, sin128,
                                                 ws, gate_b, n_ctx)
    mq, mk, gc, gr, mvt = _mlprep(mqk, _pad_rows(even_conv_w[0], 8), gates, mv, n_ctx)
    hf, hb = _mlstm(mq, mk, mvt, gc, gr, n_ctx)
    lam_init = 0.8 - 0.6 * math.exp(-0.3 * 0)
    lam8 = _pad_rows(even_lam[0], 8)
    att = _diff_attention(daq, dak, dav, lam8, row2(even_subln_g[0]), n_ctx, lam_init)
    w_out = even_w_out[0].astype(BF16)
    half = DA_HEADS * DA_V_DIM
    xmid, f, sel, w_col, counts = _post_mixer(xs, [att, hf, hb, og], [w_out[:half], w_out[half:]],
                                              modtab, row2(norm_ffn_g[0]), rwt, rb, n_ctx, 0, True)
    xs = _moe(f, sel, w_col, counts, exp_w_gate[0].astype(BF16), exp_w_up[0].astype(BF16),
              exp_w_down[0].astype(BF16), xmid, modtab, row2(final_g), n_ctx, False)

    modtab = modtab_for(1)
    gd = d // FN_GROUPS
    w_cs, w1, tab = _dft_tables(n_lat, gd)
    n2 = ML_CHUNK
    n1 = n_lat // n2
    zr, zi = _chan_dft(xs, modtab, row2(norm_mix_g[1]), w_cs, n_ctx, n_lat)
    yr, yi = _dft1(zr.reshape(b, n1, n2, d), zi.reshape(b, n1, n2, d), w1)
    fo = _dft2(yr, yi, tab).reshape(b, n_lat, d)
    xmid, f, sel, w_col, counts = _post_mixer(xs, [fo], [odd_w_fnet[0].astype(BF16)], modtab,
                                              row2(norm_ffn_g[1]), rwt, rb, 0, n_ctx, False)
    return _moe(f, sel, w_col, counts, exp_w_gate[1].astype(BF16), exp_w_up[1].astype(BF16),
                exp_w_down[1].astype(BF16), xmid, modtab, row2(final_g), 0, True)
```

```python
import functools
import math

import jax
import jax.numpy as jnp
import numpy as np
from jax import lax
from jax.experimental import pallas as pl
from jax.experimental.pallas import tpu as pltpu

F32 = jnp.float32
BF16 = jnp.bfloat16

NORM_EPS = 1e-6
GRID_W = 64
DA_HEADS = 4
DA_QK_DIM = 64
DA_V_DIM = 128
ML_HEADS = 4
ML_DIM = 128
ML_CHUNK = 128
FN_GROUPS = 4
N_EXPERTS = 16
EXPERTS_PER_GROUP = 4
ROPE_BASE = 10000.0
LANES = 128
ROW_TILE = 256
VMEM_LIMIT_BYTES = 56 * 1024 * 1024
HI = lax.Precision.HIGHEST
LOG2_E = math.log2(math.e)


def _params(*sem):
    return pltpu.CompilerParams(dimension_semantics=sem, vmem_limit_bytes=VMEM_LIMIT_BYTES)


def _dot(a, b, precision=None):
    return jnp.dot(a, b, preferred_element_type=F32, precision=precision)


def _dot_nt(a, b, precision=None):
    return lax.dot_general(a, b, (((1,), (1,)), ((), ())), preferred_element_type=F32,
                           precision=precision)


def _dot_split(a, b):
    a_hi = a.astype(BF16)
    b_hi = b.astype(BF16)
    a_lo = (a - a_hi.astype(F32)).astype(BF16)
    b_lo = (b - b_hi.astype(F32)).astype(BF16)
    return _dot(a_hi, b_hi) + (_dot(a_hi, b_lo) + _dot(a_lo, b_hi))


def _sigmoid(x):
    return 1.0 / (1.0 + jnp.exp(-x))


def _silu(x):
    return x * _sigmoid(x)


def _norm_mod(x, g, shift, scale):
    y = x * lax.rsqrt(jnp.mean(x * x, axis=-1, keepdims=True) + NORM_EPS) * g
    return y * (1.0 + scale) + shift


def _ada_kernel(c_ref, w_ref, b_ref, o_ref):
    o_ref[...] = _dot(_silu(c_ref[...]), w_ref[...], HI) + b_ref[...]


def _ada_mods(cond8, w, b):
    d, n = w.shape
    tn = n // 6
    return pl.pallas_call(
        _ada_kernel,
        grid=(6,),
        in_specs=[pl.BlockSpec((8, d), lambda j: (0, 0)),
                  pl.BlockSpec((d, tn), lambda j: (0, j)),
                  pl.BlockSpec((1, tn), lambda j: (0, j))],
        out_specs=pl.BlockSpec((8, tn), lambda j: (0, j)),
        out_shape=jax.ShapeDtypeStruct((8, n), F32),
        compiler_params=_params("arbitrary"),
        name="ada_mods",
    )(cond8, w, b.reshape(1, n))


def _inproj_kernel(x_ref, mod_ref, g_ref, cos_ref, sin_ref, wq_ref, wk_ref, wv_ref, wmqk_ref,
                   wmv_ref, wmo_ref, wg_ref, gb_ref,
                   q_ref, k_ref, v_ref, mqk_ref, mv_ref, og_ref, gate_ref):
    x = x_ref[0]
    h = _norm_mod(x, g_ref[...], mod_ref[0, 0:1, :], mod_ref[0, 1:2, :]).astype(BF16)
    tm = x.shape[0]
    width = q_ref.shape[2]
    cos = jnp.concatenate([cos_ref[...]] * (width // LANES), axis=1)
    sin = jnp.concatenate([sin_ref[...]] * (width // LANES), axis=1)
    lane = lax.broadcasted_iota(jnp.int32, (tm, width), 1)
    lower = (lane & (DA_QK_DIM - 1)) < (DA_QK_DIM // 2)

    def rope(u):
        swapped = jnp.where(lower, pltpu.roll(u, width - DA_QK_DIM // 2, 1),
                            pltpu.roll(u, DA_QK_DIM // 2, 1))
        return u * cos + swapped * sin

    q_ref[0] = (rope(_dot(h, wq_ref[...])) * LOG2_E).astype(BF16)
    k_ref[0] = rope(_dot(h, wk_ref[...])).astype(BF16)
    v = _dot(h, wv_ref[...])
    pad = VT_ROWS - DA_V_DIM
    ones_row = (lax.broadcasted_iota(jnp.int32, (pad, tm), 0) == 0).astype(BF16)
    for hd in range(DA_HEADS):
        v_ref[0, hd, 0, 0:DA_V_DIM, :] = v[:, hd * DA_V_DIM:(hd + 1) * DA_V_DIM].T.astype(BF16)
        v_ref[0, hd, 0, DA_V_DIM:VT_ROWS, :] = ones_row
    mqk_ref[0] = _dot(h, wmqk_ref[...]).astype(BF16)
    mv_ref[0] = _dot(h, wmv_ref[...]).astype(BF16)
    og_ref[0] = _sigmoid(_dot(h, wmo_ref[...])).astype(BF16)
    g = _dot(h, wg_ref[...]) + gb_ref[...]
    glane = lax.broadcasted_iota(jnp.int32, g.shape, 1)
    is_forget = ((glane // ML_HEADS) & 1) == 1
    log_sig = jnp.minimum(g, 0.0) - jnp.log(1.0 + jnp.exp(-jnp.abs(g)))
    gate_ref[0] = jnp.where(is_forget, log_sig, g)


def _inproj(xs, modtab, g, cos, sin, ws, gate_b, n_ctx):
    b, t, d = xs.shape
    tm = ROW_TILE
    nt = t // tm
    ctx_tiles = n_ctx // tm
    row = lambda w: pl.BlockSpec((1, tm, w), lambda bi, i: (bi, i, 0))
    full = lambda a: pl.BlockSpec(a.shape, lambda bi, i: (0,) * a.ndim)
    widths = [w.shape[1] for w in ws]
    out_dtypes = [BF16] * 6 + [F32]
    out_specs = [row(w) for w in widths]
    out_shape = [jax.ShapeDtypeStruct((b, t, w), dt) for w, dt in zip(widths, out_dtypes)]
    out_specs[2] = pl.BlockSpec((1, DA_HEADS, 1, VT_ROWS, tm), lambda bi, i: (bi, 0, i, 0, 0))
    out_shape[2] = jax.ShapeDtypeStruct((b, DA_HEADS, nt, VT_ROWS, tm), BF16)
    return pl.pallas_call(
        _inproj_kernel,
        grid=(b, nt),
        in_specs=[row(d),
                  pl.BlockSpec((1, 8, d), lambda bi, i: (2 * bi + (i >= ctx_tiles).astype(jnp.int32), 0, 0)),
                  full(g),
                  pl.BlockSpec((tm, LANES), lambda bi, i: (i, 0)),
                  pl.BlockSpec((tm, LANES), lambda bi, i: (i, 0))]
                 + [full(w) for w in ws] + [full(gate_b)],
        out_specs=out_specs,
        out_shape=out_shape,
        compiler_params=_params("parallel", "parallel"),
        name="inproj",
    )(xs, modtab, g, cos, sin, *ws, gate_b)


def _split3(x):
    x1 = x.astype(BF16)
    r1 = x - x1.astype(F32)
    x2 = r1.astype(BF16)
    x3 = (r1 - x2.astype(F32)).astype(BF16)
    return x1, x2, x3


VT_ROWS = ML_DIM + 16


def _mlprep_kernel(cur_ref, prev_ref, next_ref, cw_ref, gate_ref, v_ref,
                   mq_ref, mk_ref, gc_ref, gr_ref, vt_ref, *, ctx_tiles, n_tiles):
    i = pl.program_id(1)
    cur = cur_ref[0].astype(F32)
    tm, w = cur.shape
    prev_ok = i != ctx_tiles
    if ctx_tiles > 0:
        prev_ok = jnp.logical_and(prev_ok, i != 0)
        next_ok = jnp.logical_and(i != ctx_tiles - 1, i != n_tiles - 1)
    else:
        next_ok = i != n_tiles - 1
    prev_row = jnp.where(prev_ok, prev_ref[0, 7:8, :].astype(F32), 0.0)
    next_row = jnp.where(next_ok, next_ref[0, 0:1, :].astype(F32), 0.0)
    ridx = lax.broadcasted_iota(jnp.int32, (tm, w), 0)
    before = jnp.where(ridx == 0, prev_row, pltpu.roll(cur, 1, 0))
    after = jnp.where(ridx == tm - 1, next_row, pltpu.roll(cur, tm - 1, 0))
    y = _silu(before * cw_ref[0:1, :] + cur * cw_ref[1:2, :] + after * cw_ref[2:3, :])
    half = w // 2
    mq_ref[0] = y[:, :half].astype(BF16)
    mk_ref[0] = (y[:, half:] * (ML_DIM ** -0.5)).astype(BF16)

    n_g = 4 * ML_HEADS
    hds = ML_HEADS
    r = lax.broadcasted_iota(jnp.int32, (ML_CHUNK, ML_CHUNK), 0)
    c = lax.broadcasted_iota(jnp.int32, (ML_CHUNK, ML_CHUNK), 1)
    lower = (c <= r).astype(BF16)
    upper = (c >= r).astype(BF16)
    ones_row = (r == 0).astype(BF16)[0:VT_ROWS - ML_DIM]
    for ci in range(tm // ML_CHUNK):
        rows = slice(ci * ML_CHUNK, (ci + 1) * ML_CHUNK)
        gm = jnp.where(c < n_g, gate_ref[0, rows, :], 0.0)
        pre = sum(_dot(lower, p) for p in _split3(pltpu.roll(gm, n_g, 1)))
        suf = sum(_dot(upper, p) for p in _split3(pltpu.roll(gm, 2 * n_g, 1)))
        col = gm + pre + suf
        u_f = pltpu.roll(col, 3 * n_g, 1) - pltpu.roll(col, 3 * n_g - (n_g + hds), 1)
        u_b = pltpu.roll(col, 3 * n_g - hds, 1) - pltpu.roll(col, 2 * hds, 1)
        col = col + jnp.where(jnp.logical_and(c >= 3 * n_g, c < 3 * n_g + hds), u_f,
                              jnp.where(jnp.logical_and(c >= 3 * n_g + hds, c < 3 * n_g + 2 * hds),
                                        u_b, 0.0))
        gc_ref[0, rows, :] = col
        gr_ref[0, ci] = col.T
        for hd in range(ML_HEADS):
            v_t = v_ref[0, rows, hd * ML_DIM:(hd + 1) * ML_DIM].astype(F32).T.astype(BF16)
            vt_ref[0, ci, hd * VT_ROWS:hd * VT_ROWS + ML_DIM, :] = v_t
            vt_ref[0, ci, hd * VT_ROWS + ML_DIM:(hd + 1) * VT_ROWS, :] = ones_row


def _mlprep(mqk, conv_w8, gates, mv, n_ctx):
    b, t, w = mqk.shape
    tm = ROW_TILE
    nt = t // tm
    sub = tm // 8
    nsub = t // 8
    cpt = tm // ML_CHUNK
    kern = functools.partial(_mlprep_kernel, ctx_tiles=n_ctx // tm, n_tiles=nt)
    return pl.pallas_call(
        kern,
        grid=(b, nt),
        in_specs=[pl.BlockSpec((1, tm, w), lambda bi, i: (bi, i, 0)),
                  pl.BlockSpec((1, 8, w), lambda bi, i: (bi, jnp.maximum(i * sub - 1, 0), 0)),
                  pl.BlockSpec((1, 8, w), lambda bi, i: (bi, jnp.minimum((i + 1) * sub, nsub - 1), 0)),
                  pl.BlockSpec(conv_w8.shape, lambda bi, i: (0, 0)),
                  pl.BlockSpec((1, tm, LANES), lambda bi, i: (bi, i, 0)),
                  pl.BlockSpec((1, tm, w // 2), lambda bi, i: (bi, i, 0))],
        out_specs=[pl.BlockSpec((1, tm, w // 2), lambda bi, i: (bi, i, 0)),
                   pl.BlockSpec((1, tm, w // 2), lambda bi, i: (bi, i, 0)),
                   pl.BlockSpec((1, tm, LANES), lambda bi, i: (bi, i, 0)),
                   pl.BlockSpec((1, cpt, ML_CHUNK, LANES), lambda bi, i: (bi, i, 0, 0)),
                   pl.BlockSpec((1, cpt, ML_HEADS * VT_ROWS, ML_CHUNK), lambda bi, i: (bi, i, 0, 0))],
        out_shape=[jax.ShapeDtypeStruct((b, t, w // 2), BF16),
                   jax.ShapeDtypeStruct((b, t, w // 2), BF16),
                   jax.ShapeDtypeStruct((b, t, LANES), F32),
                   jax.ShapeDtypeStruct((b, t // ML_CHUNK, ML_CHUNK, LANES), F32),
                   jax.ShapeDtypeStruct((b, t // ML_CHUNK, ML_HEADS * VT_ROWS, ML_CHUNK), BF16)],
        compiler_params=_params("parallel", "parallel"),
        name="mlstm_prep",
    )(mqk, mqk, mqk, conv_w8, gates, mv)


def _mlstm_kernel(qf_ref, kf_ref, vf_ref, gcf_ref, grf_ref, qb_ref, kb_ref, vb_ref, gcb_ref, grb_ref,
                  hf_ref, hb_ref, *scr):
    s = pl.program_id(0)
    nb = qf_ref.shape[0]
    n_g = 4 * ML_HEADS
    state_refs = scr[:len(scr) // 2]
    m_refs = scr[len(scr) // 2:]

    @pl.when(s == 0)
    def _():
        for ref in scr:
            ref[...] = jnp.zeros(ref.shape, F32)

    ki = lax.broadcasted_iota(jnp.int32, (ML_CHUNK, ML_CHUNK), 0)
    qi = lax.broadcasted_iota(jnp.int32, (ML_CHUNK, ML_CHUNK), 1)

    for bi in range(nb):
        for direction in range(2):
            q_ref, k_ref, vt_ref, gc_ref, gr_ref, h_ref = (
                (qf_ref, kf_ref, vf_ref, gcf_ref, grf_ref, hf_ref) if direction == 0 else
                (qb_ref, kb_ref, vb_ref, gcb_ref, grb_ref, hb_ref))
            visible = (ki <= qi) if direction == 0 else (ki >= qi)
            gc = gc_ref[bi]
            gr = gr_ref[bi, 0]
            for hd in range(ML_HEADS):
                chain = (bi * 2 + direction) * ML_HEADS + hd
                c_lf = (2 * direction + 1) * ML_HEADS + hd
                c_cs = c_lf + (n_g if direction == 0 else 2 * n_g)
                c_u = 3 * n_g + direction * ML_HEADS + hd
                lo, hi = hd * ML_DIM, (hd + 1) * ML_DIM
                q = q_ref[bi, :, lo:hi]
                k = k_ref[bi, :, lo:hi]
                v_t = vt_ref[bi, 0, hd * VT_ROWS:(hd + 1) * VT_ROWS, :]
                u_row = gr[c_u:c_u + 1, :]
                lf_row = gr[c_lf:c_lf + 1, :]
                cs_row = gr[c_cs:c_cs + 1, :]
                m_st = m_refs[chain][:, 0:1]
                st_t = state_refs[chain][...]

                u_vis = jnp.where(visible, jnp.broadcast_to(gc[:, c_u:c_u + 1], visible.shape), -jnp.inf)
                v_row = jnp.maximum(m_st, jnp.max(u_vis, axis=0, keepdims=True))
                d_t = jnp.exp(u_vis - v_row)
                inter = jnp.exp(m_st - v_row)
                sc_t = (_dot_nt(k, q) * d_t).astype(BF16)
                tot_t = inter * _dot_nt(st_t.astype(BF16), q) + _dot(v_t, sc_t)
                den = jnp.maximum(jnp.abs(tot_t[ML_DIM:ML_DIM + 1, :]), jnp.exp(-(cs_row + v_row)))
                h_ref[bi, :, lo:hi] = (tot_t[:ML_DIM, :] / den).T

                b_last = jnp.sum(lf_row, axis=1, keepdims=True)
                m_sc = jnp.maximum(m_st, jnp.max(u_row, axis=1, keepdims=True))
                wgt = jnp.exp(u_row - m_sc)
                vw_t = (v_t.astype(F32) * wgt).astype(BF16)
                state_refs[chain][...] = jnp.exp(m_st - m_sc) * st_t + _dot(vw_t, k)
                m_refs[chain][...] = jnp.broadcast_to(b_last + m_sc, (1, LANES))


def _mlstm(mq, mk, mvt, gc, gr, n_ctx):
    b, t, w = mq.shape
    nc = t // ML_CHUNK
    ncc = n_ctx // ML_CHUNK
    fwd = lambda s: s
    bwd = lambda s: jnp.where(s < ncc, ncc - 1 - s, nc - 1 - s + ncc)
    tok = lambda f: pl.BlockSpec((b, ML_CHUNK, w), lambda s: (0, f(s), 0))
    gcs = lambda f: pl.BlockSpec((b, ML_CHUNK, LANES), lambda s: (0, f(s), 0))
    grs = lambda f: pl.BlockSpec((b, 1, ML_CHUNK, LANES), lambda s: (0, f(s), 0, 0))
    vts = lambda f: pl.BlockSpec((b, 1, ML_HEADS * VT_ROWS, ML_CHUNK), lambda s: (0, f(s), 0, 0))
    n_chain = b * 2 * ML_HEADS
    return pl.pallas_call(
        _mlstm_kernel,
        grid=(nc,),
        in_specs=[tok(fwd), tok(fwd), vts(fwd), gcs(fwd), grs(fwd),
                  tok(bwd), tok(bwd), vts(bwd), gcs(bwd), grs(bwd)],
        out_specs=[tok(fwd), tok(bwd)],
        out_shape=[jax.ShapeDtypeStruct((b, t, w), F32)] * 2,
        scratch_shapes=[pltpu.VMEM((VT_ROWS, ML_DIM), F32)] * n_chain
                       + [pltpu.VMEM((1, LANES), F32)] * n_chain,
        compiler_params=_params("arbitrary"),
        name="mlstm",
    )(mq, mk, mvt, gc, gr, mq, mk, mvt, gc, gr)


def _attn_kernel(q_ref, qn_ref, k_ref, vt_ref, lam_ref, sg_ref, o_ref, qm_scr, *scr,
                 ctx_tiles, n_ctx, key_block, n_blocks, lam_init):
    s_scr = (scr[0:2], scr[2:4])
    bm_scr = (scr[4:6], scr[6:8])
    m_scr = scr[8:10]
    acc_scr = scr[10:12]
    i = pl.program_id(2)
    dv = DA_V_DIM
    chunk = vt_ref.shape[4]
    q = q_ref[0]
    lane = lax.broadcasted_iota(jnp.int32, q.shape, 1)
    zero = jnp.zeros_like(q)

    def map_halves(qv):
        return jnp.where(lane < DA_QK_DIM, qv, zero), jnp.where(lane >= DA_QK_DIM, qv, zero)

    qm_scr[0], qm_scr[1] = map_halves(q)
    align = math.gcd(n_ctx, key_block)

    def scores(mp, start, size, q_maps=None):
        q_map = qm_scr[mp] if q_maps is None else q_maps[mp]
        return _dot_nt(k_ref[0, pl.ds(start, size), :], q_map)

    def weighted_values(sc, m, first_chunk):
        p = jnp.exp2(sc - m).astype(BF16)
        out = None
        for ci in range(sc.shape[0] // chunk):
            part = _dot(vt_ref[0, 0, first_chunk + ci], p[ci * chunk:(ci + 1) * chunk, :])
            out = part if out is None else out + part
        return out

    for mp in range(2):
        sc = scores(mp, 0, n_ctx)
        m = jnp.max(sc, axis=0, keepdims=True)
        m_scr[mp][...] = m
        acc_scr[mp][...] = weighted_values(sc, m, 0)

    def stage(j, slot, q_maps=None):
        start = pl.multiple_of(n_ctx + j * key_block, align)
        for mp in range(2):
            sc = scores(mp, start, key_block, q_maps)
            s_scr[mp][slot][...] = sc
            bm_scr[mp][slot][...] = jnp.max(sc, axis=0, keepdims=True)

    def consume(j, slot):
        first_chunk = (n_ctx + j * key_block) // chunk
        for mp in range(2):
            m_old = m_scr[mp][...]
            m_new = jnp.maximum(m_old, bm_scr[mp][slot][...])
            alpha = jnp.exp2(m_old - m_new)
            pv = weighted_values(s_scr[mp][slot][...], m_new, first_chunk)
            acc_scr[mp][...] = alpha * acc_scr[mp][...] + pv
            m_scr[mp][...] = m_new

    @pl.when(i >= ctx_tiles)
    def _():
        @pl.when(i == ctx_tiles)
        def _():
            stage(0, 0)

        def body(g, carry):
            stage(2 * g + 1, 1)
            consume(2 * g, 0)
            stage(2 * g + 2, 0)
            consume(2 * g + 1, 1)
            return carry
        lax.fori_loop(0, n_blocks // 2 - 1, body, 0)
        stage(n_blocks - 1, 1)
        consume(n_blocks - 2, 0)
        stage(0, 0, map_halves(qn_ref[0]))
        consume(n_blocks - 1, 1)

    lv = lam_ref[...]
    dot01 = jnp.sum(lv[0:1, :] * lv[1:2, :], axis=1, keepdims=True)
    dot23 = jnp.sum(lv[2:3, :] * lv[3:4, :], axis=1, keepdims=True)
    lam = jnp.exp(dot01) - jnp.exp(dot23) + lam_init
    a0 = acc_scr[0][...]
    a1 = acc_scr[1][...]
    o = (a0[0:dv, :] / a0[dv:dv + 1, :] - lam * (a1[0:dv, :] / a1[dv:dv + 1, :])).T
    o = o * lax.rsqrt(jnp.mean(o * o, axis=-1, keepdims=True) + NORM_EPS) * sg_ref[...]
    o_ref[0] = (o * (1.0 - lam_init)).astype(BF16)


def _diff_attention(q, k, vt, lam8, subln_g, n_ctx, lam_init):
    b, t, w = q.shape
    tq = ROW_TILE
    n_lat = t - n_ctx
    key_block = 2048 if n_lat % 4096 == 0 else 512
    n_blocks = n_lat // key_block
    _, _, n_chunks, vt_rows, chunk = vt.shape
    assert n_lat % (2 * key_block) == 0 and n_ctx % tq == 0 and n_ctx % chunk == 0
    assert key_block % chunk == 0
    kern = functools.partial(_attn_kernel, ctx_tiles=n_ctx // tq, n_ctx=n_ctx, key_block=key_block,
                             n_blocks=n_blocks, lam_init=lam_init)
    return pl.pallas_call(
        kern,
        grid=(b, DA_HEADS, t // tq),
        in_specs=[pl.BlockSpec((1, tq, LANES), lambda bi, h, i: (bi, i, h)),
                  pl.BlockSpec((1, tq, LANES), lambda bi, h, i: (bi, jnp.minimum(i + 1, t // tq - 1), h)),
                  pl.BlockSpec((1, t, LANES), lambda bi, h, i: (bi, 0, h)),
                  pl.BlockSpec((1, 1, n_chunks, vt_rows, chunk), lambda bi, h, i: (bi, h, 0, 0, 0)),
                  pl.BlockSpec(lam8.shape, lambda bi, h, i: (0, 0)),
                  pl.BlockSpec(subln_g.shape, lambda bi, h, i: (0, 0))],
        out_specs=pl.BlockSpec((1, tq, LANES), lambda bi, h, i: (bi, i, h)),
        out_shape=jax.ShapeDtypeStruct((b, t, w), BF16),
        scratch_shapes=[pltpu.VMEM((2, tq, LANES), BF16)]
                       + [pltpu.VMEM((key_block, tq), F32)] * 4
                       + [pltpu.VMEM((1, tq), F32)] * 6
                       + [pltpu.VMEM((vt_rows, tq), F32)] * 2,
        compiler_params=_params("arbitrary", "arbitrary", "arbitrary"),
        name="diff_attention",
    )(q, q, k, vt, lam8, subln_g)


def _top2_sum(a, b, c, d):
    hi1, lo1 = jnp.maximum(a, b), jnp.minimum(a, b)
    hi2, lo2 = jnp.maximum(c, d), jnp.minimum(c, d)
    return jnp.maximum(hi1, hi2) + jnp.maximum(jnp.minimum(hi1, hi2), jnp.maximum(lo1, lo2))


def _route(f, rwt_ref, rb_ref, cnt_ref):
    tm = f.shape[0]
    aff = _sigmoid(_dot_nt(rwt_ref[...], f, HI))
    biased = aff + rb_ref[:, 0:1]
    bz = [biased[e:e + 1, :] for e in range(N_EXPERTS)]
    af = [aff[e:e + 1, :] for e in range(N_EXPERTS)]
    n_grp = N_EXPERTS // EXPERTS_PER_GROUP
    scores = [_top2_sum(*bz[EXPERTS_PER_GROUP * g:EXPERTS_PER_GROUP * (g + 1)]) for g in range(n_grp)]
    best = scores[0]
    sel_grp = jnp.zeros_like(best, dtype=jnp.int32)
    for g in range(1, n_grp):
        better = scores[g] > best
        sel_grp = jnp.where(better, g, sel_grp)
        best = jnp.where(better, scores[g], best)
    chosen = []
    for e in range(N_EXPERTS):
        g = e // EXPERTS_PER_GROUP
        rank = jnp.zeros_like(sel_grp)
        for o in range(EXPERTS_PER_GROUP * g, EXPERTS_PER_GROUP * (g + 1)):
            if o == e:
                continue
            beats = (bz[o] > bz[e]) if o > e else (bz[o] >= bz[e])
            rank = rank + beats.astype(jnp.int32)
        chosen.append(jnp.logical_and(sel_grp == g, rank < 2))
    denom = sum(jnp.where(chosen[e], af[e], 0.0) for e in range(N_EXPERTS))
    erow = lax.broadcasted_iota(jnp.int32, (N_EXPERTS, tm), 0)
    one_hot = jnp.zeros((N_EXPERTS, tm), F32)
    for e in range(N_EXPERTS):
        one_hot = jnp.where(jnp.logical_and(erow == e, chosen[e]), 1.0, one_hot)
    earlier = (lax.broadcasted_iota(jnp.int32, (tm, tm), 0)
               < lax.broadcasted_iota(jnp.int32, (tm, tm), 1)).astype(BF16)
    rank_all = _dot(one_hot.astype(BF16), earlier) + cnt_ref[:, 0:1]
    cnt_ref[...] = cnt_ref[...] + jnp.sum(one_hot, axis=1, keepdims=True)

    seen = jnp.zeros((1, tm), jnp.bool_)
    e_a = e_b = jnp.zeros((1, tm), jnp.int32)
    r_a = r_b = w_a = w_b = jnp.zeros((1, tm), F32)
    for e in range(N_EXPERTS):
        first = jnp.logical_and(chosen[e], jnp.logical_not(seen))
        second = jnp.logical_and(chosen[e], seen)
        rk = rank_all[e:e + 1, :]
        wt = af[e] / denom
        e_a, e_b = jnp.where(first, e, e_a), jnp.where(second, e, e_b)
        r_a, r_b = jnp.where(first, rk, r_a), jnp.where(second, rk, r_b)
        w_a, w_b = jnp.where(first, wt, w_a), jnp.where(second, wt, w_b)
        seen = jnp.logical_or(seen, chosen[e])
    r8 = lax.broadcasted_iota(jnp.int32, (8, tm), 0)
    sel = jnp.where(r8 == 0, e_a, jnp.where(r8 == 1, e_b, jnp.where(
        r8 == 2, r_a.astype(jnp.int32), jnp.where(r8 == 3, r_b.astype(jnp.int32), 0))))
    row = lax.broadcasted_iota(jnp.int32, (LANES, tm), 0)
    w_t = jnp.where(row == 0, w_a, jnp.where(row == 1, w_b, 0.0))
    return sel, w_t.T


def _post_kernel(*refs, even):
    if even:
        (x_ref, a_ref, hf_ref, hb_ref, og_ref, mod_ref, wa_ref, wm_ref, gf_ref, rwt_ref, rb_ref,
         xo_ref, f_ref, sel_ref, w_ref, cnt_ref, cnt_scr) = refs
        m = ((hf_ref[0] + hb_ref[0]) * og_ref[0].astype(F32)).astype(BF16)
        o = _dot(a_ref[0], wa_ref[...]) + _dot(m, wm_ref[...])
    else:
        (x_ref, a_ref, mod_ref, wa_ref, gf_ref, rwt_ref, rb_ref,
         xo_ref, f_ref, sel_ref, w_ref, cnt_ref, cnt_scr) = refs
        o = _dot(a_ref[0].astype(BF16), wa_ref[...])

    @pl.when(jnp.logical_and(pl.program_id(0) == 0, pl.program_id(1) == 0))
    def _():
        cnt_scr[...] = jnp.zeros(cnt_scr.shape, F32)

    x = x_ref[0] + mod_ref[0, 2:3, :] * o
    xo_ref[0] = x
    f = _norm_mod(x, gf_ref[...], mod_ref[0, 3:4, :], mod_ref[0, 4:5, :])
    _store_row_tiles(f_ref.at[0], f)
    sel, w_col = _route(f, rwt_ref, rb_ref, cnt_scr)
    sel_ref[0, 0] = sel
    w_ref[0] = w_col
    cnt_ref[...] = cnt_scr[...]


def _post_mixer(x, acts, weights, modtab, gffn, rwt, rb, n_ctx, x_row_off, even):
    b, t, _ = acts[0].shape
    d = x.shape[2]
    tm = ROW_TILE
    nt = t // tm
    ctx_tiles = n_ctx // tm
    off = x_row_off // tm
    full = lambda a: pl.BlockSpec(a.shape, lambda bi, i: (0,) * a.ndim)
    row = lambda w: pl.BlockSpec((1, tm, w), lambda bi, i: (bi, i, 0))
    mod_spec = pl.BlockSpec(
        (1, 8, d), lambda bi, i: (2 * bi + (i + off >= ctx_tiles).astype(jnp.int32), 0, 0))
    in_specs = ([pl.BlockSpec((1, tm, d), lambda bi, i: (bi, i + off, 0))]
                + [row(a.shape[2]) for a in acts] + [mod_spec]
                + [full(w) for w in weights] + [full(gffn), full(rwt), full(rb)])
    return pl.pallas_call(
        functools.partial(_post_kernel, even=even),
        grid=(b, nt),
        in_specs=in_specs,
        out_specs=[row(d), pl.BlockSpec((1, tm * d // LANES, LANES), lambda bi, i: (bi, i, 0)),
                   pl.BlockSpec((1, 1, 8, tm), lambda bi, i: (bi, i, 0, 0)),
                   row(LANES),
                   pl.BlockSpec((N_EXPERTS, LANES), lambda bi, i: (0, 0))],
        out_shape=[jax.ShapeDtypeStruct((b, t, d), F32),
                   jax.ShapeDtypeStruct((b, t * d // LANES, LANES), F32),
                   jax.ShapeDtypeStruct((b, nt, 8, tm), jnp.int32),
                   jax.ShapeDtypeStruct((b, t, LANES), F32),
                   jax.ShapeDtypeStruct((N_EXPERTS, LANES), F32)],
        scratch_shapes=[pltpu.VMEM((N_EXPERTS, LANES), F32)],
        compiler_params=_params("arbitrary", "arbitrary"),
        name="post_mixer_even" if even else "post_mixer_odd",
    )(x, *acts, modtab, *weights, gffn, rwt, rb)


EXPERT_ROW_TILE = 256
DMA_ISSUE_UNROLL = 8


def _moe_plan(sel, counts, n_tok):
    e_a, e_b, r_a, r_b = (sel[:, :, k, :].reshape(-1) for k in range(4))
    cnt = counts[:, 0].astype(jnp.int32)
    padded = ((cnt + EXPERT_ROW_TILE - 1) // EXPERT_ROW_TILE) * EXPERT_ROW_TILE
    ends = jnp.cumsum(padded)
    starts = ends - padded
    pos = jnp.concatenate([starts[e_a] + r_a, starts[e_b] + r_b]).astype(jnp.int32)
    n_tiles = 2 * n_tok // EXPERT_ROW_TILE + N_EXPERTS
    tile_start = jnp.arange(n_tiles, dtype=jnp.int32) * EXPERT_ROW_TILE
    tile_expert = jnp.minimum(jnp.sum(tile_start[:, None] >= ends[None, :], axis=1),
                              N_EXPERTS - 1).astype(jnp.int32)
    tiles_used = (ends[-1:] // EXPERT_ROW_TILE).astype(jnp.int32)
    return pos, tile_expert, tiles_used, n_tiles


def _store_row_tiles(ref, x):
    rows, d = x.shape
    n_sub = d // LANES
    for s in range(n_sub):
        ref[pl.ds(s, rows, stride=n_sub), :] = x[:, s * LANES:(s + 1) * LANES]


def _load_row_tiles(ref, rows):
    n_sub = ref.shape[0] // rows
    return jnp.concatenate([ref[pl.ds(s, rows, stride=n_sub), :] for s in range(n_sub)], axis=1)


def _dispatch_kernel(pos_ref, f_ref, init_ref, out_ref, sem, *, n_tok, tm):
    del init_ref
    n_sub = f_ref.shape[1] // tm
    base = (pl.program_id(0) * pl.num_programs(1) + pl.program_id(1)) * tm

    def row_copy(r, k):
        dst = pos_ref[k * n_tok + base + r]
        src = f_ref.at[0, pl.ds(pl.multiple_of(r * n_sub, n_sub), n_sub), :]
        return pltpu.make_async_copy(src, out_ref.at[dst], sem)

    def issue(r, carry):
        row_copy(r, 0).start()
        row_copy(r, 1).start(priority=1)
        return carry

    lax.fori_loop(0, tm, issue, 0, unroll=DMA_ISSUE_UNROLL)
    for _ in range(2):
        pltpu.make_async_copy(out_ref.at[pl.ds(0, tm)], out_ref.at[pl.ds(0, tm)], sem).wait()


def _dispatch(pos, f, n_rows, n_tok):
    b, rows, _ = f.shape
    n_sub = rows * b // n_tok
    tm = ROW_TILE
    grid_spec = pltpu.PrefetchScalarGridSpec(
        num_scalar_prefetch=1,
        grid=(b, n_tok // b // tm),
        in_specs=[pl.BlockSpec((1, tm * n_sub, LANES), lambda bi, i, pos_ref: (bi, i, 0)),
                  pl.BlockSpec(memory_space=pl.ANY)],
        out_specs=pl.BlockSpec(memory_space=pl.ANY),
        scratch_shapes=[pltpu.SemaphoreType.DMA(())])
    return pl.pallas_call(
        functools.partial(_dispatch_kernel, n_tok=n_tok, tm=tm),
        grid_spec=grid_spec,
        out_shape=jax.ShapeDtypeStruct((n_rows, n_sub, LANES), F32),
        input_output_aliases={2: 0},
        compiler_params=_params("arbitrary", "arbitrary"),
        name="moe_dispatch",
    )(pos, f, jnp.zeros((n_rows, n_sub, LANES), F32))


def _expert_ffn_kernel(te_ref, used_ref, x_ref, wg_ref, wu_ref, wd_ref, y_ref, wg_scr, wu_scr, wd_scr):
    j = pl.program_id(0)
    live = j < used_ref[0]

    @pl.when(jnp.logical_or(j == 0, te_ref[j] != te_ref[jnp.maximum(j - 1, 0)]))
    def _():
        wg_scr[...] = wg_ref[0, 0].astype(BF16)
        wu_scr[...] = wu_ref[0, 0].astype(BF16)
        wd_scr[...] = wd_ref[0, 0].astype(BF16)

    @pl.when(live)
    def _():
        xb = _load_row_tiles(x_ref, EXPERT_ROW_TILE).astype(BF16)
        he = _silu(_dot(xb, wg_scr[...])) * _dot(xb, wu_scr[...])
        _store_row_tiles(y_ref, _dot(he.astype(BF16), wd_scr[...]))

    @pl.when(jnp.logical_not(live))
    def _():
        y_ref[...] = jnp.zeros(y_ref.shape, F32)


def _expert_ffn(tile_expert, tiles_used, xs, wg, wu, wd, layer, n_tiles):
    n_rows, n_sub, _ = xs.shape
    _, _, d, d_e = wg.shape
    tr = EXPERT_ROW_TILE
    grid_spec = pltpu.PrefetchScalarGridSpec(
        num_scalar_prefetch=2,
        grid=(n_tiles,),
        in_specs=[pl.BlockSpec((tr * n_sub, LANES), lambda j, te, used: (j, 0)),
                  pl.BlockSpec((1, 1, d, d_e), lambda j, te, used: (layer, te[j], 0, 0)),
                  pl.BlockSpec((1, 1, d, d_e), lambda j, te, used: (layer, te[j], 0, 0)),
                  pl.BlockSpec((1, 1, d_e, d), lambda j, te, used: (layer, te[j], 0, 0))],
        out_specs=pl.BlockSpec((tr * n_sub, LANES), lambda j, te, used: (j, 0)),
        scratch_shapes=[pltpu.VMEM((d, d_e), BF16), pltpu.VMEM((d, d_e), BF16),
                        pltpu.VMEM((d_e, d), BF16)])
    ys = pl.pallas_call(
        _expert_ffn_kernel,
        grid_spec=grid_spec,
        out_shape=jax.ShapeDtypeStruct((n_rows * n_sub, LANES), F32),
        compiler_params=_params("arbitrary"),
        name="moe_expert_ffn",
    )(tile_expert, tiles_used, xs.reshape(n_rows * n_sub, LANES), wg, wu, wd)
    return ys.reshape(n_rows, n_sub, LANES)


def _combine_kernel(pos_ref, y_ref, w_ref, x_ref, modc_ref, modl_ref, fg_ref, o_ref, buf, sem,
                    *, n_tok, n_ctx, final_norm):
    i = pl.program_id(1)
    tm = x_ref.shape[1]
    base = (pl.program_id(0) * pl.num_programs(1) + i) * tm

    n_sub = y_ref.shape[1]

    def row_copy(r, k):
        src = pos_ref[k * n_tok + base + r]
        dst = buf.at[k, pl.ds(pl.multiple_of(r * n_sub, n_sub), n_sub), :]
        return pltpu.make_async_copy(y_ref.at[src], dst, sem)

    def issue(r, carry):
        row_copy(r, 0).start()
        row_copy(r, 1).start(priority=1)
        return carry

    lax.fori_loop(0, tm, issue, 0, unroll=DMA_ISSUE_UNROLL)
    for _ in range(2):
        pltpu.make_async_copy(y_ref.at[pl.ds(0, tm)], y_ref.at[pl.ds(0, tm)], sem).wait()
    w = w_ref[0]
    y = _load_row_tiles(buf.at[0], tm) * w[:, 0:1] + _load_row_tiles(buf.at[1], tm) * w[:, 1:2]
    rows = i * tm + lax.broadcasted_iota(jnp.int32, (tm, 1), 0)
    gate = jnp.where(rows < n_ctx, modc_ref[0, 5:6, :], modl_ref[0, 5:6, :])
    out = x_ref[0] + gate * y
    if final_norm:
        out = out * lax.rsqrt(jnp.mean(out * out, axis=-1, keepdims=True) + NORM_EPS) * fg_ref[...]
    o_ref[0] = out


def _combine(pos, ys, w_col, xmid, modtab, final_g, n_ctx, final_norm):
    b, t, d = xmid.shape
    tm = ROW_TILE
    row = lambda w: pl.BlockSpec((1, tm, w), lambda bi, i, pos_ref: (bi, i, 0))
    grid_spec = pltpu.PrefetchScalarGridSpec(
        num_scalar_prefetch=1,
        grid=(b, t // tm),
        in_specs=[pl.BlockSpec(memory_space=pl.ANY), row(LANES), row(d),
                  pl.BlockSpec((1, 8, d), lambda bi, i, pos_ref: (2 * bi, 0, 0)),
                  pl.BlockSpec((1, 8, d), lambda bi, i, pos_ref: (2 * bi + 1, 0, 0)),
                  pl.BlockSpec(final_g.shape, lambda bi, i, pos_ref: (0, 0))],
        out_specs=row(d),
        scratch_shapes=[pltpu.VMEM((2, tm * d // LANES, LANES), F32), pltpu.SemaphoreType.DMA(())])
    return pl.pallas_call(
        functools.partial(_combine_kernel, n_tok=b * t, n_ctx=n_ctx, final_norm=final_norm),
        grid_spec=grid_spec,
        out_shape=jax.ShapeDtypeStruct((b, t, d), F32),
        compiler_params=_params("arbitrary", "arbitrary"),
        name="moe_combine",
    )(pos, ys, w_col, xmid, modtab, modtab, final_g)


def _moe(f, sel, w_col, counts, wg, wu, wd, layer, xmid, modtab, final_g, n_ctx, final_norm):
    b, t, _ = xmid.shape
    pos, tile_expert, tiles_used, n_tiles = _moe_plan(sel, counts, b * t)
    xs = _dispatch(pos, f, n_tiles * EXPERT_ROW_TILE, b * t)
    ys = _expert_ffn(tile_expert, tiles_used, xs, wg, wu, wd, layer, n_tiles)
    return _combine(pos, ys, w_col, xmid, modtab, final_g, n_ctx, final_norm)


def _chan_dft_kernel(x_ref, mod_ref, g_ref, w_ref, zr_ref, zi_ref):
    h = _norm_mod(x_ref[0], g_ref[...], mod_ref[0, 0:1, :], mod_ref[0, 1:2, :])
    gd = w_ref.shape[0]
    for gi in range(h.shape[1] // gd):
        z = _dot_split(h[:, gi * gd:(gi + 1) * gd], w_ref[...])
        zr_ref[0, :, gi * gd:(gi + 1) * gd] = z[:, :gd]
        zi_ref[0, :, gi * gd:(gi + 1) * gd] = z[:, gd:]


def _chan_dft(x, modtab, g, w_cs, x_row_off, t):
    b, _, d = x.shape
    tm = ROW_TILE
    off = x_row_off // tm
    row = pl.BlockSpec((1, tm, d), lambda bi, i: (bi, i, 0))
    return pl.pallas_call(
        _chan_dft_kernel,
        grid=(b, t // tm),
        in_specs=[pl.BlockSpec((1, tm, d), lambda bi, i: (bi, i + off, 0)),
                  pl.BlockSpec((1, 8, d), lambda bi, i: (2 * bi + 1, 0, 0)),
                  pl.BlockSpec(g.shape, lambda bi, i: (0, 0)),
                  pl.BlockSpec(w_cs.shape, lambda bi, i: (0, 0))],
        out_specs=[row, row],
        out_shape=[jax.ShapeDtypeStruct((b, t, d), F32)] * 2,
        compiler_params=_params("parallel", "parallel"),
        name="chan_dft",
    )(x, modtab, g, w_cs)


DFT_SUB = 8


def _dft1_kernel(zr_ref, zi_ref, w_ref, yr_ref, yi_ref):
    _, n1, sub, cols = zr_ref.shape
    z = jnp.concatenate([jnp.concatenate([ref[0, :, j, :] for j in range(sub)], axis=1)
                         for ref in (zr_ref, zi_ref)], axis=0)
    y = _dot_split(w_ref[...], z)
    for j in range(sub):
        yr_ref[0, :, j, :] = y[:n1, j * cols:(j + 1) * cols]
        yi_ref[0, :, j, :] = y[n1:, j * cols:(j + 1) * cols]


def _dft1(zr, zi, w1):
    b, n1, n2, d = zr.shape
    cols = d // 2
    blk = pl.BlockSpec((1, n1, DFT_SUB, cols), lambda bi, j, c: (bi, 0, j, c))
    return pl.pallas_call(
        _dft1_kernel,
        grid=(b, n2 // DFT_SUB, d // cols),
        in_specs=[blk, blk, pl.BlockSpec(w1.shape, lambda bi, j, c: (0, 0))],
        out_specs=[blk, blk],
        out_shape=[jax.ShapeDtypeStruct(zr.shape, F32)] * 2,
        compiler_params=_params("parallel", "parallel", "parallel"),
        name="dft_stage1",
    )(zr, zi, w1)


def _dft2_kernel(yr_ref, yi_ref, tab_ref, o_ref):
    for j in range(yr_ref.shape[1]):
        y = jnp.concatenate([yr_ref[0, j], yi_ref[0, j]], axis=0)
        o_ref[0, :, j, :] = _dot_split(tab_ref[j], y)


def _dft2(yr, yi, tab):
    b, n1, n2, d = yr.shape
    blk = pl.BlockSpec((1, DFT_SUB, n2, d), lambda bi, k1: (bi, k1, 0, 0))
    return pl.pallas_call(
        _dft2_kernel,
        grid=(b, n1 // DFT_SUB),
        in_specs=[blk, blk, pl.BlockSpec((DFT_SUB, n2, 2 * n2), lambda bi, k1: (k1, 0, 0))],
        out_specs=pl.BlockSpec((1, n2, DFT_SUB, d), lambda bi, k1: (bi, 0, k1, 0)),
        out_shape=jax.ShapeDtypeStruct((b, n2, n1, d), F32),
        compiler_params=_params("parallel", "parallel"),
        name="dft_stage2",
    )(yr, yi, tab)


def _dft_tables(t, gd):
    n2 = ML_CHUNK
    n1 = t // n2
    def cs(num, den):
        ang = (2.0 * np.pi / den) * (num % den).astype(np.float64)
        return np.cos(ang), np.sin(ang)
    c = np.arange(gd)
    cc, sc = cs(np.outer(c, c), gd)
    w_cs = np.concatenate([cc, -sc], axis=1)
    a = np.arange(n1)
    c1, s1 = cs(np.outer(a, a), n1)
    w1 = np.block([[c1, s1], [-s1, c1]])
    k = a[:, None, None] + n1 * np.arange(n2)[None, :, None]
    c2, s2 = cs(k * np.arange(n2)[None, None, :], t)
    tab = np.concatenate([c2, s2], axis=2) / math.sqrt(t * gd)
    return (jnp.asarray(w_cs, F32), jnp.asarray(w1, F32), jnp.asarray(tab, F32))


def _rope_tables(n_ctx, n_lat):
    pos = jnp.arange(n_lat, dtype=jnp.int32)
    n_axis = DA_QK_DIM // 4
    inv = ROPE_BASE ** (-jnp.arange(n_axis, dtype=F32) / n_axis)
    ang = jnp.concatenate([(pos // GRID_W).astype(F32)[:, None] * inv,
                           (pos % GRID_W).astype(F32)[:, None] * inv], axis=-1)
    cos, sin = jnp.cos(ang), jnp.sin(ang)
    cos = jnp.concatenate([jnp.ones((n_ctx, 2 * n_axis), F32), cos], axis=0)
    sin = jnp.concatenate([jnp.zeros((n_ctx, 2 * n_axis), F32), sin], axis=0)
    cos128 = jnp.concatenate([cos, cos, cos, cos], axis=1)
    sin128 = jnp.concatenate([-sin, sin, -sin, sin], axis=1)
    return cos128, sin128


def _deinterleave(w):
    d, n = w.shape
    w = w.reshape(d, n // DA_QK_DIM, DA_QK_DIM // 2, 2)
    return jnp.concatenate([w[..., 0], w[..., 1]], axis=-1).reshape(d, n)


def _pad_rows(a, rows):
    return jnp.concatenate([a, jnp.zeros((rows - a.shape[0],) + a.shape[1:], a.dtype)], axis=0)


def _pad_cols(a, cols):
    return jnp.concatenate([a, jnp.zeros(a.shape[:-1] + (cols - a.shape[-1],), a.dtype)], axis=-1)


def kernel(x, c, ctx, c_ctx, ada_w, ada_b, norm_mix_g, norm_ffn_g, even_w_in, even_w_out,
           even_conv_w, even_gate_b, even_lam, even_subln_g, odd_w_fnet, router_w, router_b,
           exp_w_gate, exp_w_up, exp_w_down, final_g):
    b, n_lat, d = x.shape
    n_ctx = ctx.shape[1]
    depth = ada_w.shape[0]
    assert depth == 2 and b + 1 <= 8
    assert n_ctx % ROW_TILE == 0 and n_lat % ROW_TILE == 0

    cond8 = _pad_rows(jnp.concatenate([c_ctx[None, :], c], axis=0), 8)
    rwt = router_w.T
    rb = jnp.broadcast_to(router_b[:, None], (N_EXPERTS, LANES))
    row2 = lambda v: v.reshape(1, -1)

    def modtab_for(layer):
        mods = _ada_mods(cond8, ada_w[layer], ada_b[layer]).reshape(8, 6, d)
        mods = jnp.concatenate([mods, jnp.zeros((8, 2, d), F32)], axis=1)
        idx = np.array([[0, 1 + bi] for bi in range(b)]).reshape(-1)
        return mods[idx]

    xs = jnp.concatenate([ctx, x], axis=1)
    modtab = modtab_for(0)
    w_in = even_w_in[0]
    o1 = DA_HEADS * 2 * DA_QK_DIM
    o2 = 2 * o1
    o3 = o2 + DA_HEADS * DA_V_DIM
    o4 = o3 + 2 * ML_HEADS * ML_DIM
    o5 = o4 + ML_HEADS * ML_DIM
    o6 = o5 + ML_HEADS * ML_DIM
    ws = [(_deinterleave(w_in[:, :o1]) * (DA_QK_DIM ** -0.5)).astype(BF16),
          _deinterleave(w_in[:, o1:o2]).astype(BF16),
          w_in[:, o2:o3].astype(BF16), w_in[:, o3:o4].astype(BF16),
          w_in[:, o4:o5].astype(BF16), w_in[:, o5:o6].astype(BF16),
          _pad_cols(w_in[:, o6:], LANES).astype(BF16)]
    gate_b = _pad_cols(even_gate_b[0].reshape(1, -1), LANES)
    cos128, sin128 = _rope_tables(n_ctx, n_lat)
    daq, dak, dav, mqk, mv, og, gates = _inproj(xs, modtab, row2(norm_mix_g[0]), cos128, sin128,
                                                 ws, gate_b, n_ctx)
    mq, mk, gc, gr, mvt = _mlprep(mqk, _pad_rows(even_conv_w[0], 8), gates, mv, n_ctx)
    hf, hb = _mlstm(mq, mk, mvt, gc, gr, n_ctx)
    lam_init = 0.8 - 0.6 * math.exp(-0.3 * 0)
    lam8 = _pad_rows(even_lam[0], 8)
    att = _diff_attention(daq, dak, dav, lam8, row2(even_subln_g[0]), n_ctx, lam_init)
    w_out = even_w_out[0].astype(BF16)
    half = DA_HEADS * DA_V_DIM
    xmid, f, sel, w_col, counts = _post_mixer(xs, [att, hf, hb, og], [w_out[:half], w_out[half:]],
                                              modtab, row2(norm_ffn_g[0]), rwt, rb, n_ctx, 0, True)
    xs = _moe(f, sel, w_col, counts, exp_w_gate, exp_w_up, exp_w_down, 0,
              xmid, modtab, row2(final_g), n_ctx, False)

    modtab = modtab_for(1)
    gd = d // FN_GROUPS
    w_cs, w1, tab = _dft_tables(n_lat, gd)
    n2 = ML_CHUNK
    n1 = n_lat // n2
    zr, zi = _chan_dft(xs, modtab, row2(norm_mix_g[1]), w_cs, n_ctx, n_lat)
    yr, yi = _dft1(zr.reshape(b, n1, n2, d), zi.reshape(b, n1, n2, d), w1)
    fo = _dft2(yr, yi, tab).reshape(b, n_lat, d)
    xmid, f, sel, w_col, counts = _post_mixer(xs, [fo], [odd_w_fnet[0].astype(BF16)], modtab,
                                              row2(norm_ffn_g[1]), rwt, rb, 0, n_ctx, False)
    return _moe(f, sel, w_col, counts, exp_w_gate, exp_w_up, exp_w_down, 1,
                xmid, modtab, row2(final_g), 0, True)
```

```python
import functools
import math

import jax
import jax.numpy as jnp
import numpy as np
from jax import lax
from jax.experimental import pallas as pl
from jax.experimental.pallas import tpu as pltpu

F32 = jnp.float32
BF16 = jnp.bfloat16

NORM_EPS = 1e-6
GRID_W = 64
DA_HEADS = 4
DA_QK_DIM = 64
DA_V_DIM = 128
ML_HEADS = 4
ML_DIM = 128
ML_CHUNK = 128
FN_GROUPS = 4
N_EXPERTS = 16
EXPERTS_PER_GROUP = 4
ROPE_BASE = 10000.0
LANES = 128
ROW_TILE = 256
VMEM_LIMIT_BYTES = 56 * 1024 * 1024
HI = lax.Precision.HIGHEST
LOG2_E = math.log2(math.e)


def _params(*sem):
    return pltpu.CompilerParams(dimension_semantics=sem, vmem_limit_bytes=VMEM_LIMIT_BYTES)


def _dot(a, b, precision=None):
    return jnp.dot(a, b, preferred_element_type=F32, precision=precision)


def _dot_nt(a, b, precision=None):
    return lax.dot_general(a, b, (((1,), (1,)), ((), ())), preferred_element_type=F32,
                           precision=precision)


def _dot_split(a, b):
    a_hi = a.astype(BF16)
    b_hi = b.astype(BF16)
    a_lo = (a - a_hi.astype(F32)).astype(BF16)
    b_lo = (b - b_hi.astype(F32)).astype(BF16)
    return _dot(a_hi, b_hi) + (_dot(a_hi, b_lo) + _dot(a_lo, b_hi))


def _sigmoid(x):
    return 1.0 / (1.0 + jnp.exp(-x))


def _silu(x):
    return x * _sigmoid(x)


def _norm_mod(x, g, shift, scale):
    y = x * lax.rsqrt(jnp.mean(x * x, axis=-1, keepdims=True) + NORM_EPS) * g
    return y * (1.0 + scale) + shift


def _ada_kernel(c_ref, w_ref, b_ref, o_ref):
    o_ref[...] = _dot(_silu(c_ref[...]), w_ref[...], HI) + b_ref[...]


def _ada_mods(cond8, w, b):
    d, n = w.shape
    tn = n // 6
    return pl.pallas_call(
        _ada_kernel,
        grid=(6,),
        in_specs=[pl.BlockSpec((8, d), lambda j: (0, 0)),
                  pl.BlockSpec((d, tn), lambda j: (0, j)),
                  pl.BlockSpec((1, tn), lambda j: (0, j))],
        out_specs=pl.BlockSpec((8, tn), lambda j: (0, j)),
        out_shape=jax.ShapeDtypeStruct((8, n), F32),
        compiler_params=_params("arbitrary"),
        name="ada_mods",
    )(cond8, w, b.reshape(1, n))


def _inproj_kernel(x_ref, mod_ref, g_ref, cos_ref, sin_ref, wq_ref, wk_ref, wv_ref, wmqk_ref,
                   wmv_ref, wmo_ref, wg_ref, gb_ref,
                   q_ref, k_ref, v_ref, mqk_ref, mv_ref, og_ref, gate_ref):
    x = x_ref[0]
    h = _norm_mod(x, g_ref[...], mod_ref[0, 0:1, :], mod_ref[0, 1:2, :]).astype(BF16)
    tm = x.shape[0]
    width = q_ref.shape[2]
    cos = jnp.concatenate([cos_ref[...]] * (width // LANES), axis=1)
    sin = jnp.concatenate([sin_ref[...]] * (width // LANES), axis=1)
    lane = lax.broadcasted_iota(jnp.int32, (tm, width), 1)
    lower = (lane & (DA_QK_DIM - 1)) < (DA_QK_DIM // 2)

    def rope(u):
        swapped = jnp.where(lower, pltpu.roll(u, width - DA_QK_DIM // 2, 1),
                            pltpu.roll(u, DA_QK_DIM // 2, 1))
        return u * cos + swapped * sin

    q_ref[0] = (rope(_dot(h, wq_ref[...])) * LOG2_E).astype(BF16)
    k_ref[0] = rope(_dot(h, wk_ref[...])).astype(BF16)
    v = _dot(h, wv_ref[...])
    pad = VT_ROWS - DA_V_DIM
    ones_row = (lax.broadcasted_iota(jnp.int32, (pad, tm), 0) == 0).astype(BF16)
    for hd in range(DA_HEADS):
        v_ref[0, hd, 0, 0:DA_V_DIM, :] = v[:, hd * DA_V_DIM:(hd + 1) * DA_V_DIM].T.astype(BF16)
        v_ref[0, hd, 0, DA_V_DIM:VT_ROWS, :] = ones_row
    mqk_ref[0] = _dot(h, wmqk_ref[...]).astype(BF16)
    mv_ref[0] = _dot(h, wmv_ref[...]).astype(BF16)
    og_ref[0] = _sigmoid(_dot(h, wmo_ref[...])).astype(BF16)
    g = _dot(h, wg_ref[...]) + gb_ref[...]
    glane = lax.broadcasted_iota(jnp.int32, g.shape, 1)
    is_forget = ((glane // ML_HEADS) & 1) == 1
    log_sig = jnp.minimum(g, 0.0) - jnp.log(1.0 + jnp.exp(-jnp.abs(g)))
    gate_ref[0] = jnp.where(is_forget, log_sig, g)


def _inproj(xs, modtab, g, cos, sin, ws, gate_b, n_ctx):
    b, t, d = xs.shape
    tm = ROW_TILE
    nt = t // tm
    ctx_tiles = n_ctx // tm
    row = lambda w: pl.BlockSpec((1, tm, w), lambda bi, i: (bi, i, 0))
    full = lambda a: pl.BlockSpec(a.shape, lambda bi, i: (0,) * a.ndim)
    widths = [w.shape[1] for w in ws]
    out_dtypes = [BF16] * 6 + [F32]
    out_specs = [row(w) for w in widths]
    out_shape = [jax.ShapeDtypeStruct((b, t, w), dt) for w, dt in zip(widths, out_dtypes)]
    out_specs[2] = pl.BlockSpec((1, DA_HEADS, 1, VT_ROWS, tm), lambda bi, i: (bi, 0, i, 0, 0))
    out_shape[2] = jax.ShapeDtypeStruct((b, DA_HEADS, nt, VT_ROWS, tm), BF16)
    return pl.pallas_call(
        _inproj_kernel,
        grid=(b, nt),
        in_specs=[row(d),
                  pl.BlockSpec((1, 8, d), lambda bi, i: (2 * bi + (i >= ctx_tiles).astype(jnp.int32), 0, 0)),
                  full(g),
                  pl.BlockSpec((tm, LANES), lambda bi, i: (i, 0)),
                  pl.BlockSpec((tm, LANES), lambda bi, i: (i, 0))]
                 + [full(w) for w in ws] + [full(gate_b)],
        out_specs=out_specs,
        out_shape=out_shape,
        compiler_params=_params("parallel", "parallel"),
        name="inproj",
    )(xs, modtab, g, cos, sin, *ws, gate_b)


def _split3(x):
    x1 = x.astype(BF16)
    r1 = x - x1.astype(F32)
    x2 = r1.astype(BF16)
    x3 = (r1 - x2.astype(F32)).astype(BF16)
    return x1, x2, x3


VT_ROWS = ML_DIM + 16


def _mlprep_kernel(cur_ref, prev_ref, next_ref, cw_ref, gate_ref, v_ref,
                   mq_ref, mk_ref, gc_ref, gr_ref, vt_ref, *, ctx_tiles, n_tiles):
    i = pl.program_id(1)
    cur = cur_ref[0].astype(F32)
    tm, w = cur.shape
    prev_ok = i != ctx_tiles
    if ctx_tiles > 0:
        prev_ok = jnp.logical_and(prev_ok, i != 0)
        next_ok = jnp.logical_and(i != ctx_tiles - 1, i != n_tiles - 1)
    else:
        next_ok = i != n_tiles - 1
    prev_row = jnp.where(prev_ok, prev_ref[0, 7:8, :].astype(F32), 0.0)
    next_row = jnp.where(next_ok, next_ref[0, 0:1, :].astype(F32), 0.0)
    ridx = lax.broadcasted_iota(jnp.int32, (tm, w), 0)
    before = jnp.where(ridx == 0, prev_row, pltpu.roll(cur, 1, 0))
    after = jnp.where(ridx == tm - 1, next_row, pltpu.roll(cur, tm - 1, 0))
    y = _silu(before * cw_ref[0:1, :] + cur * cw_ref[1:2, :] + after * cw_ref[2:3, :])
    half = w // 2
    mq_ref[0] = y[:, :half].astype(BF16)
    mk_ref[0] = (y[:, half:] * (ML_DIM ** -0.5)).astype(BF16)

    n_g = 4 * ML_HEADS
    hds = ML_HEADS
    r = lax.broadcasted_iota(jnp.int32, (ML_CHUNK, ML_CHUNK), 0)
    c = lax.broadcasted_iota(jnp.int32, (ML_CHUNK, ML_CHUNK), 1)
    lower = (c <= r).astype(BF16)
    upper = (c >= r).astype(BF16)
    ones_row = (r == 0).astype(BF16)[0:VT_ROWS - ML_DIM]
    for ci in range(tm // ML_CHUNK):
        rows = slice(ci * ML_CHUNK, (ci + 1) * ML_CHUNK)
        gm = jnp.where(c < n_g, gate_ref[0, rows, :], 0.0)
        pre = sum(_dot(lower, p) for p in _split3(pltpu.roll(gm, n_g, 1)))
        suf = sum(_dot(upper, p) for p in _split3(pltpu.roll(gm, 2 * n_g, 1)))
        col = gm + pre + suf
        u_f = pltpu.roll(col, 3 * n_g, 1) - pltpu.roll(col, 3 * n_g - (n_g + hds), 1)
        u_b = pltpu.roll(col, 3 * n_g - hds, 1) - pltpu.roll(col, 2 * hds, 1)
        col = col + jnp.where(jnp.logical_and(c >= 3 * n_g, c < 3 * n_g + hds), u_f,
                              jnp.where(jnp.logical_and(c >= 3 * n_g + hds, c < 3 * n_g + 2 * hds),
                                        u_b, 0.0))
        gc_ref[0, rows, :] = col
        gr_ref[0, ci] = col.T
        for hd in range(ML_HEADS):
            v_t = v_ref[0, rows, hd * ML_DIM:(hd + 1) * ML_DIM].astype(F32).T.astype(BF16)
            vt_ref[0, ci, hd * VT_ROWS:hd * VT_ROWS + ML_DIM, :] = v_t
            vt_ref[0, ci, hd * VT_ROWS + ML_DIM:(hd + 1) * VT_ROWS, :] = ones_row


def _mlprep(mqk, conv_w8, gates, mv, n_ctx):
    b, t, w = mqk.shape
    tm = ROW_TILE
    nt = t // tm
    sub = tm // 8
    nsub = t // 8
    cpt = tm // ML_CHUNK
    kern = functools.partial(_mlprep_kernel, ctx_tiles=n_ctx // tm, n_tiles=nt)
    return pl.pallas_call(
        kern,
        grid=(b, nt),
        in_specs=[pl.BlockSpec((1, tm, w), lambda bi, i: (bi, i, 0)),
                  pl.BlockSpec((1, 8, w), lambda bi, i: (bi, jnp.maximum(i * sub - 1, 0), 0)),
                  pl.BlockSpec((1, 8, w), lambda bi, i: (bi, jnp.minimum((i + 1) * sub, nsub - 1), 0)),
                  pl.BlockSpec(conv_w8.shape, lambda bi, i: (0, 0)),
                  pl.BlockSpec((1, tm, LANES), lambda bi, i: (bi, i, 0)),
                  pl.BlockSpec((1, tm, w // 2), lambda bi, i: (bi, i, 0))],
        out_specs=[pl.BlockSpec((1, tm, w // 2), lambda bi, i: (bi, i, 0)),
                   pl.BlockSpec((1, tm, w // 2), lambda bi, i: (bi, i, 0)),
                   pl.BlockSpec((1, tm, LANES), lambda bi, i: (bi, i, 0)),
                   pl.BlockSpec((1, cpt, ML_CHUNK, LANES), lambda bi, i: (bi, i, 0, 0)),
                   pl.BlockSpec((1, cpt, ML_HEADS * VT_ROWS, ML_CHUNK), lambda bi, i: (bi, i, 0, 0))],
        out_shape=[jax.ShapeDtypeStruct((b, t, w // 2), BF16),
                   jax.ShapeDtypeStruct((b, t, w // 2), BF16),
                   jax.ShapeDtypeStruct((b, t, LANES), F32),
                   jax.ShapeDtypeStruct((b, t // ML_CHUNK, ML_CHUNK, LANES), F32),
                   jax.ShapeDtypeStruct((b, t // ML_CHUNK, ML_HEADS * VT_ROWS, ML_CHUNK), BF16)],
        compiler_params=_params("parallel", "parallel"),
        name="mlstm_prep",
    )(mqk, mqk, mqk, conv_w8, gates, mv)


def _mlstm_kernel(qf_ref, kf_ref, vf_ref, gcf_ref, grf_ref, qb_ref, kb_ref, vb_ref, gcb_ref, grb_ref,
                  hf_ref, hb_ref, *scr):
    s = pl.program_id(0)
    nb = qf_ref.shape[0]
    n_g = 4 * ML_HEADS
    state_refs = scr[:len(scr) // 2]
    m_refs = scr[len(scr) // 2:]

    @pl.when(s == 0)
    def _():
        for ref in scr:
            ref[...] = jnp.zeros(ref.shape, F32)

    ki = lax.broadcasted_iota(jnp.int32, (ML_CHUNK, ML_CHUNK), 0)
    qi = lax.broadcasted_iota(jnp.int32, (ML_CHUNK, ML_CHUNK), 1)

    for bi in range(nb):
        for direction in range(2):
            q_ref, k_ref, vt_ref, gc_ref, gr_ref, h_ref = (
                (qf_ref, kf_ref, vf_ref, gcf_ref, grf_ref, hf_ref) if direction == 0 else
                (qb_ref, kb_ref, vb_ref, gcb_ref, grb_ref, hb_ref))
            visible = (ki <= qi) if direction == 0 else (ki >= qi)
            gc = gc_ref[bi]
            gr = gr_ref[bi, 0]
            for hd in range(ML_HEADS):
                chain = (bi * 2 + direction) * ML_HEADS + hd
                c_lf = (2 * direction + 1) * ML_HEADS + hd
                c_cs = c_lf + (n_g if direction == 0 else 2 * n_g)
                c_u = 3 * n_g + direction * ML_HEADS + hd
                lo, hi = hd * ML_DIM, (hd + 1) * ML_DIM
                q = q_ref[bi, :, lo:hi]
                k = k_ref[bi, :, lo:hi]
                v_t = vt_ref[bi, 0, hd * VT_ROWS:(hd + 1) * VT_ROWS, :]
                u_row = gr[c_u:c_u + 1, :]
                lf_row = gr[c_lf:c_lf + 1, :]
                cs_row = gr[c_cs:c_cs + 1, :]
                m_st = m_refs[chain][:, 0:1]
                st_t = state_refs[chain][...]

                u_vis = jnp.where(visible, jnp.broadcast_to(gc[:, c_u:c_u + 1], visible.shape), -jnp.inf)
                v_row = jnp.maximum(m_st, jnp.max(u_vis, axis=0, keepdims=True))
                d_t = jnp.exp(u_vis - v_row)
                inter = jnp.exp(m_st - v_row)
                sc_t = (_dot_nt(k, q) * d_t).astype(BF16)
                tot_t = inter * _dot_nt(st_t.astype(BF16), q) + _dot(v_t, sc_t)
                den = jnp.maximum(jnp.abs(tot_t[ML_DIM:ML_DIM + 1, :]), jnp.exp(-(cs_row + v_row)))
                h_ref[bi, :, lo:hi] = (tot_t[:ML_DIM, :] / den).T

                b_last = jnp.sum(lf_row, axis=1, keepdims=True)
                m_sc = jnp.maximum(m_st, jnp.max(u_row, axis=1, keepdims=True))
                wgt = jnp.exp(u_row - m_sc)
                vw_t = (v_t.astype(F32) * wgt).astype(BF16)
                state_refs[chain][...] = jnp.exp(m_st - m_sc) * st_t + _dot(vw_t, k)
                m_refs[chain][...] = jnp.broadcast_to(b_last + m_sc, (1, LANES))


def _mlstm(mq, mk, mvt, gc, gr, n_ctx):
    b, t, w = mq.shape
    nc = t // ML_CHUNK
    ncc = n_ctx // ML_CHUNK
    fwd = lambda s: s
    bwd = lambda s: jnp.where(s < ncc, ncc - 1 - s, nc - 1 - s + ncc)
    tok = lambda f: pl.BlockSpec((b, ML_CHUNK, w), lambda s: (0, f(s), 0))
    gcs = lambda f: pl.BlockSpec((b, ML_CHUNK, LANES), lambda s: (0, f(s), 0))
    grs = lambda f: pl.BlockSpec((b, 1, ML_CHUNK, LANES), lambda s: (0, f(s), 0, 0))
    vts = lambda f: pl.BlockSpec((b, 1, ML_HEADS * VT_ROWS, ML_CHUNK), lambda s: (0, f(s), 0, 0))
    n_chain = b * 2 * ML_HEADS
    return pl.pallas_call(
        _mlstm_kernel,
        grid=(nc,),
        in_specs=[tok(fwd), tok(fwd), vts(fwd), gcs(fwd), grs(fwd),
                  tok(bwd), tok(bwd), vts(bwd), gcs(bwd), grs(bwd)],
        out_specs=[tok(fwd), tok(bwd)],
        out_shape=[jax.ShapeDtypeStruct((b, t, w), F32)] * 2,
        scratch_shapes=[pltpu.VMEM((VT_ROWS, ML_DIM), F32)] * n_chain
                       + [pltpu.VMEM((1, LANES), F32)] * n_chain,
        compiler_params=_params("arbitrary"),
        name="mlstm",
    )(mq, mk, mvt, gc, gr, mq, mk, mvt, gc, gr)


def _attn_kernel(q_ref, qn_ref, k_ref, vt_ref, lam_ref, sg_ref, o_ref, qm_scr, *scr,
                 ctx_tiles, n_ctx, key_block, n_blocks, lam_init):
    s_scr = (scr[0:2], scr[2:4])
    bm_scr = (scr[4:6], scr[6:8])
    m_scr = scr[8:10]
    acc_scr = scr[10:12]
    i = pl.program_id(2)
    dv = DA_V_DIM
    chunk = vt_ref.shape[4]
    q = q_ref[0]
    lane = lax.broadcasted_iota(jnp.int32, q.shape, 1)
    zero = jnp.zeros_like(q)

    def map_halves(qv):
        return jnp.where(lane < DA_QK_DIM, qv, zero), jnp.where(lane >= DA_QK_DIM, qv, zero)

    qm_scr[0], qm_scr[1] = map_halves(q)
    align = math.gcd(n_ctx, key_block)

    def scores(mp, start, size, q_maps=None):
        q_map = qm_scr[mp] if q_maps is None else q_maps[mp]
        return _dot_nt(k_ref[0, pl.ds(start, size), :], q_map)

    def weighted_values(sc, m, first_chunk):
        p = jnp.exp2(sc - m).astype(BF16)
        out = None
        for ci in range(sc.shape[0] // chunk):
            part = _dot(vt_ref[0, 0, first_chunk + ci], p[ci * chunk:(ci + 1) * chunk, :])
            out = part if out is None else out + part
        return out

    for mp in range(2):
        sc = scores(mp, 0, n_ctx)
        m = jnp.max(sc, axis=0, keepdims=True)
        m_scr[mp][...] = m
        acc_scr[mp][...] = weighted_values(sc, m, 0)

    def stage(j, slot, q_maps=None):
        start = pl.multiple_of(n_ctx + j * key_block, align)
        for mp in range(2):
            sc = scores(mp, start, key_block, q_maps)
            s_scr[mp][slot][...] = sc
            bm_scr[mp][slot][...] = jnp.max(sc, axis=0, keepdims=True)

    def consume(j, slot):
        first_chunk = (n_ctx + j * key_block) // chunk
        for mp in range(2):
            m_old = m_scr[mp][...]
            m_new = jnp.maximum(m_old, bm_scr[mp][slot][...])
            alpha = jnp.exp2(m_old - m_new)
            pv = weighted_values(s_scr[mp][slot][...], m_new, first_chunk)
            acc_scr[mp][...] = alpha * acc_scr[mp][...] + pv
            m_scr[mp][...] = m_new

    @pl.when(i >= ctx_tiles)
    def _():
        @pl.when(i == ctx_tiles)
        def _():
            stage(0, 0)

        def body(g, carry):
            stage(2 * g + 1, 1)
            consume(2 * g, 0)
            stage(2 * g + 2, 0)
            consume(2 * g + 1, 1)
            return carry
        lax.fori_loop(0, n_blocks // 2 - 1, body, 0)
        stage(n_blocks - 1, 1)
        consume(n_blocks - 2, 0)
        stage(0, 0, map_halves(qn_ref[0]))
        consume(n_blocks - 1, 1)

    lv = lam_ref[...]
    dot01 = jnp.sum(lv[0:1, :] * lv[1:2, :], axis=1, keepdims=True)
    dot23 = jnp.sum(lv[2:3, :] * lv[3:4, :], axis=1, keepdims=True)
    lam = jnp.exp(dot01) - jnp.exp(dot23) + lam_init
    a0 = acc_scr[0][...]
    a1 = acc_scr[1][...]
    o = (a0[0:dv, :] / a0[dv:dv + 1, :] - lam * (a1[0:dv, :] / a1[dv:dv + 1, :])).T
    o = o * lax.rsqrt(jnp.mean(o * o, axis=-1, keepdims=True) + NORM_EPS) * sg_ref[...]
    o_ref[0] = (o * (1.0 - lam_init)).astype(BF16)


def _diff_attention(q, k, vt, lam8, subln_g, n_ctx, lam_init):
    b, t, w = q.shape
    tq = ROW_TILE
    n_lat = t - n_ctx
    key_block = 2048 if n_lat % 4096 == 0 else 512
    n_blocks = n_lat // key_block
    _, _, n_chunks, vt_rows, chunk = vt.shape
    assert n_lat % (2 * key_block) == 0 and n_ctx % tq == 0 and n_ctx % chunk == 0
    assert key_block % chunk == 0
    kern = functools.partial(_attn_kernel, ctx_tiles=n_ctx // tq, n_ctx=n_ctx, key_block=key_block,
                             n_blocks=n_blocks, lam_init=lam_init)
    return pl.pallas_call(
        kern,
        grid=(b, DA_HEADS, t // tq),
        in_specs=[pl.BlockSpec((1, tq, LANES), lambda bi, h, i: (bi, i, h)),
                  pl.BlockSpec((1, tq, LANES), lambda bi, h, i: (bi, jnp.minimum(i + 1, t // tq - 1), h)),
                  pl.BlockSpec((1, t, LANES), lambda bi, h, i: (bi, 0, h)),
                  pl.BlockSpec((1, 1, n_chunks, vt_rows, chunk), lambda bi, h, i: (bi, h, 0, 0, 0)),
                  pl.BlockSpec(lam8.shape, lambda bi, h, i: (0, 0)),
                  pl.BlockSpec(subln_g.shape, lambda bi, h, i: (0, 0))],
        out_specs=pl.BlockSpec((1, tq, LANES), lambda bi, h, i: (bi, i, h)),
        out_shape=jax.ShapeDtypeStruct((b, t, w), BF16),
        scratch_shapes=[pltpu.VMEM((2, tq, LANES), BF16)]
                       + [pltpu.VMEM((key_block, tq), F32)] * 4
                       + [pltpu.VMEM((1, tq), F32)] * 6
                       + [pltpu.VMEM((vt_rows, tq), F32)] * 2,
        compiler_params=_params("arbitrary", "arbitrary", "arbitrary"),
        name="diff_attention",
    )(q, q, k, vt, lam8, subln_g)


def _top2_sum(a, b, c, d):
    hi1, lo1 = jnp.maximum(a, b), jnp.minimum(a, b)
    hi2, lo2 = jnp.maximum(c, d), jnp.minimum(c, d)
    return jnp.maximum(hi1, hi2) + jnp.maximum(jnp.minimum(hi1, hi2), jnp.maximum(lo1, lo2))


def _route(f, rwt_ref, rb_ref, cnt_ref):
    tm = f.shape[0]
    aff = _sigmoid(_dot_nt(rwt_ref[...], f, HI))
    biased = aff + rb_ref[:, 0:1]
    bz = [biased[e:e + 1, :] for e in range(N_EXPERTS)]
    af = [aff[e:e + 1, :] for e in range(N_EXPERTS)]
    n_grp = N_EXPERTS // EXPERTS_PER_GROUP
    scores = [_top2_sum(*bz[EXPERTS_PER_GROUP * g:EXPERTS_PER_GROUP * (g + 1)]) for g in range(n_grp)]
    best = scores[0]
    sel_grp = jnp.zeros_like(best, dtype=jnp.int32)
    for g in range(1, n_grp):
        better = scores[g] > best
        sel_grp = jnp.where(better, g, sel_grp)
        best = jnp.where(better, scores[g], best)
    chosen = []
    for e in range(N_EXPERTS):
        g = e // EXPERTS_PER_GROUP
        rank = jnp.zeros_like(sel_grp)
        for o in range(EXPERTS_PER_GROUP * g, EXPERTS_PER_GROUP * (g + 1)):
            if o == e:
                continue
            beats = (bz[o] > bz[e]) if o > e else (bz[o] >= bz[e])
            rank = rank + beats.astype(jnp.int32)
        chosen.append(jnp.logical_and(sel_grp == g, rank < 2))
    denom = sum(jnp.where(chosen[e], af[e], 0.0) for e in range(N_EXPERTS))
    erow = lax.broadcasted_iota(jnp.int32, (N_EXPERTS, tm), 0)
    one_hot = jnp.zeros((N_EXPERTS, tm), F32)
    for e in range(N_EXPERTS):
        one_hot = jnp.where(jnp.logical_and(erow == e, chosen[e]), 1.0, one_hot)
    earlier = (lax.broadcasted_iota(jnp.int32, (tm, tm), 0)
               < lax.broadcasted_iota(jnp.int32, (tm, tm), 1)).astype(BF16)
    rank_all = _dot(one_hot.astype(BF16), earlier) + cnt_ref[:, 0:1]
    cnt_ref[...] = cnt_ref[...] + jnp.sum(one_hot, axis=1, keepdims=True)

    seen = jnp.zeros((1, tm), jnp.bool_)
    e_a = e_b = jnp.zeros((1, tm), jnp.int32)
    r_a = r_b = w_a = w_b = jnp.zeros((1, tm), F32)
    for e in range(N_EXPERTS):
        first = jnp.logical_and(chosen[e], jnp.logical_not(seen))
        second = jnp.logical_and(chosen[e], seen)
        rk = rank_all[e:e + 1, :]
        wt = af[e] / denom
        e_a, e_b = jnp.where(first, e, e_a), jnp.where(second, e, e_b)
        r_a, r_b = jnp.where(first, rk, r_a), jnp.where(second, rk, r_b)
        w_a, w_b = jnp.where(first, wt, w_a), jnp.where(second, wt, w_b)
        seen = jnp.logical_or(seen, chosen[e])
    r8 = lax.broadcasted_iota(jnp.int32, (8, tm), 0)
    sel = jnp.where(r8 == 0, e_a, jnp.where(r8 == 1, e_b, jnp.where(
        r8 == 2, r_a.astype(jnp.int32), jnp.where(r8 == 3, r_b.astype(jnp.int32), 0))))
    row = lax.broadcasted_iota(jnp.int32, (LANES, tm), 0)
    w_t = jnp.where(row == 0, w_a, jnp.where(row == 1, w_b, 0.0))
    return sel, w_t.T


def _post_kernel(*refs, even):
    if even:
        (x_ref, a_ref, hf_ref, hb_ref, og_ref, mod_ref, wa_ref, wm_ref, gf_ref, rwt_ref, rb_ref,
         xo_ref, f_ref, sel_ref, w_ref, cnt_ref, cnt_scr) = refs
        m = ((hf_ref[0] + hb_ref[0]) * og_ref[0].astype(F32)).astype(BF16)
        o = _dot(a_ref[0], wa_ref[...]) + _dot(m, wm_ref[...])
    else:
        (x_ref, a_ref, mod_ref, wa_ref, gf_ref, rwt_ref, rb_ref,
         xo_ref, f_ref, sel_ref, w_ref, cnt_ref, cnt_scr) = refs
        o = _dot(a_ref[0].astype(BF16), wa_ref[...])

    @pl.when(jnp.logical_and(pl.program_id(0) == 0, pl.program_id(1) == 0))
    def _():
        cnt_scr[...] = jnp.zeros(cnt_scr.shape, F32)

    x = x_ref[0] + mod_ref[0, 2:3, :] * o
    xo_ref[0] = x
    f = _norm_mod(x, gf_ref[...], mod_ref[0, 3:4, :], mod_ref[0, 4:5, :])
    _store_row_tiles(f_ref.at[0], f)
    sel, w_col = _route(f, rwt_ref, rb_ref, cnt_scr)
    sel_ref[0, 0] = sel
    w_ref[0] = w_col
    cnt_ref[...] = cnt_scr[...]


def _post_mixer(x, acts, weights, modtab, gffn, rwt, rb, n_ctx, x_row_off, even):
    b, t, _ = acts[0].shape
    d = x.shape[2]
    tm = ROW_TILE
    nt = t // tm
    ctx_tiles = n_ctx // tm
    off = x_row_off // tm
    full = lambda a: pl.BlockSpec(a.shape, lambda bi, i: (0,) * a.ndim)
    row = lambda w: pl.BlockSpec((1, tm, w), lambda bi, i: (bi, i, 0))
    mod_spec = pl.BlockSpec(
        (1, 8, d), lambda bi, i: (2 * bi + (i + off >= ctx_tiles).astype(jnp.int32), 0, 0))
    in_specs = ([pl.BlockSpec((1, tm, d), lambda bi, i: (bi, i + off, 0))]
                + [row(a.shape[2]) for a in acts] + [mod_spec]
                + [full(w) for w in weights] + [full(gffn), full(rwt), full(rb)])
    return pl.pallas_call(
        functools.partial(_post_kernel, even=even),
        grid=(b, nt),
        in_specs=in_specs,
        out_specs=[row(d), pl.BlockSpec((1, tm * d // LANES, LANES), lambda bi, i: (bi, i, 0)),
                   pl.BlockSpec((1, 1, 8, tm), lambda bi, i: (bi, i, 0, 0)),
                   row(LANES),
                   pl.BlockSpec((N_EXPERTS, LANES), lambda bi, i: (0, 0))],
        out_shape=[jax.ShapeDtypeStruct((b, t, d), F32),
                   jax.ShapeDtypeStruct((b, t * d // LANES, LANES), F32),
                   jax.ShapeDtypeStruct((b, nt, 8, tm), jnp.int32),
                   jax.ShapeDtypeStruct((b, t, LANES), F32),
                   jax.ShapeDtypeStruct((N_EXPERTS, LANES), F32)],
        scratch_shapes=[pltpu.VMEM((N_EXPERTS, LANES), F32)],
        compiler_params=_params("arbitrary", "arbitrary"),
        name="post_mixer_even" if even else "post_mixer_odd",
    )(x, *acts, modtab, *weights, gffn, rwt, rb)


EXPERT_ROW_TILE = 512
DMA_ISSUE_UNROLL = 8


def _moe_plan(sel, counts, n_tok):
    e_a, e_b, r_a, r_b = (sel[:, :, k, :].reshape(-1) for k in range(4))
    cnt = counts[:, 0].astype(jnp.int32)
    padded = ((cnt + EXPERT_ROW_TILE - 1) // EXPERT_ROW_TILE) * EXPERT_ROW_TILE
    ends = jnp.cumsum(padded)
    starts = ends - padded
    pos = jnp.concatenate([starts[e_a] + r_a, starts[e_b] + r_b]).astype(jnp.int32)
    n_tiles = 2 * n_tok // EXPERT_ROW_TILE + N_EXPERTS
    tile_start = jnp.arange(n_tiles, dtype=jnp.int32) * EXPERT_ROW_TILE
    tile_expert = jnp.minimum(jnp.sum(tile_start[:, None] >= ends[None, :], axis=1),
                              N_EXPERTS - 1).astype(jnp.int32)
    tiles_used = (ends[-1:] // EXPERT_ROW_TILE).astype(jnp.int32)
    return pos, tile_expert, tiles_used, n_tiles


def _store_row_tiles(ref, x):
    rows, d = x.shape
    n_sub = d // LANES
    for s in range(n_sub):
        ref[pl.ds(s, rows, stride=n_sub), :] = x[:, s * LANES:(s + 1) * LANES]


def _load_row_tiles(ref, rows):
    n_sub = ref.shape[0] // rows
    return jnp.concatenate([ref[pl.ds(s, rows, stride=n_sub), :] for s in range(n_sub)], axis=1)


def _dispatch_kernel(pos_ref, f_ref, init_ref, out_ref, sem, *, n_tok, tm):
    del init_ref
    n_sub = f_ref.shape[1] // tm
    base = (pl.program_id(0) * pl.num_programs(1) + pl.program_id(1)) * tm

    def row_copy(r, k):
        dst = pos_ref[k * n_tok + base + r]
        src = f_ref.at[0, pl.ds(pl.multiple_of(r * n_sub, n_sub), n_sub), :]
        return pltpu.make_async_copy(src, out_ref.at[dst], sem)

    def issue(r, carry):
        row_copy(r, 0).start()
        row_copy(r, 1).start(priority=1)
        return carry

    lax.fori_loop(0, tm, issue, 0, unroll=DMA_ISSUE_UNROLL)
    for _ in range(2):
        pltpu.make_async_copy(out_ref.at[pl.ds(0, tm)], out_ref.at[pl.ds(0, tm)], sem).wait()


def _dispatch(pos, f, n_rows, n_tok):
    b, rows, _ = f.shape
    n_sub = rows * b // n_tok
    tm = ROW_TILE
    grid_spec = pltpu.PrefetchScalarGridSpec(
        num_scalar_prefetch=1,
        grid=(b, n_tok // b // tm),
        in_specs=[pl.BlockSpec((1, tm * n_sub, LANES), lambda bi, i, pos_ref: (bi, i, 0)),
                  pl.BlockSpec(memory_space=pl.ANY)],
        out_specs=pl.BlockSpec(memory_space=pl.ANY),
        scratch_shapes=[pltpu.SemaphoreType.DMA(())])
    return pl.pallas_call(
        functools.partial(_dispatch_kernel, n_tok=n_tok, tm=tm),
        grid_spec=grid_spec,
        out_shape=jax.ShapeDtypeStruct((n_rows, n_sub, LANES), F32),
        input_output_aliases={2: 0},
        compiler_params=_params("arbitrary", "arbitrary"),
        name="moe_dispatch",
    )(pos, f, jnp.zeros((n_rows, n_sub, LANES), F32))


def _expert_ffn_kernel(te_ref, used_ref, x_ref, wg_ref, wu_ref, wd_ref, y_ref, wg_scr, wu_scr, wd_scr):
    j = pl.program_id(0)
    live = j < used_ref[0]

    @pl.when(jnp.logical_or(j == 0, te_ref[j] != te_ref[jnp.maximum(j - 1, 0)]))
    def _():
        wg_scr[...] = wg_ref[0, 0].astype(BF16)
        wu_scr[...] = wu_ref[0, 0].astype(BF16)
        wd_scr[...] = wd_ref[0, 0].astype(BF16)

    @pl.when(live)
    def _():
        xb = _load_row_tiles(x_ref, EXPERT_ROW_TILE).astype(BF16)
        he = _silu(_dot(xb, wg_scr[...])) * _dot(xb, wu_scr[...])
        _store_row_tiles(y_ref, _dot(he.astype(BF16), wd_scr[...]))

    @pl.when(jnp.logical_not(live))
    def _():
        y_ref[...] = jnp.zeros(y_ref.shape, F32)


def _expert_ffn(tile_expert, tiles_used, xs, wg, wu, wd, layer, n_tiles):
    n_rows, n_sub, _ = xs.shape
    _, _, d, d_e = wg.shape
    tr = EXPERT_ROW_TILE
    grid_spec = pltpu.PrefetchScalarGridSpec(
        num_scalar_prefetch=2,
        grid=(n_tiles,),
        in_specs=[pl.BlockSpec((tr * n_sub, LANES), lambda j, te, used: (j, 0)),
                  pl.BlockSpec((1, 1, d, d_e), lambda j, te, used: (layer, te[j], 0, 0)),
                  pl.BlockSpec((1, 1, d, d_e), lambda j, te, used: (layer, te[j], 0, 0)),
                  pl.BlockSpec((1, 1, d_e, d), lambda j, te, used: (layer, te[j], 0, 0))],
        out_specs=pl.BlockSpec((tr * n_sub, LANES), lambda j, te, used: (j, 0)),
        scratch_shapes=[pltpu.VMEM((d, d_e), BF16), pltpu.VMEM((d, d_e), BF16),
                        pltpu.VMEM((d_e, d), BF16)])
    ys = pl.pallas_call(
        _expert_ffn_kernel,
        grid_spec=grid_spec,
        out_shape=jax.ShapeDtypeStruct((n_rows * n_sub, LANES), F32),
        compiler_params=_params("arbitrary"),
        name="moe_expert_ffn",
    )(tile_expert, tiles_used, xs.reshape(n_rows * n_sub, LANES), wg, wu, wd)
    return ys.reshape(n_rows, n_sub, LANES)


def _combine_kernel(pos_ref, y_ref, w_ref, x_ref, modc_ref, modl_ref, fg_ref, o_ref, buf, sem,
                    *, n_tok, n_ctx, final_norm):
    i = pl.program_id(1)
    tm = x_ref.shape[1]
    step = pl.program_id(0) * pl.num_programs(1) + i
    n_steps = pl.num_programs(0) * pl.num_programs(1)
    n_sub = y_ref.shape[1]

    def gather(tile, slot):
        base = tile * tm

        def issue(r, carry):
            for k in range(2):
                src = pos_ref[k * n_tok + base + r]
                dst = buf.at[slot, k, pl.ds(pl.multiple_of(r * n_sub, n_sub), n_sub), :]
                pltpu.make_async_copy(y_ref.at[src], dst, sem.at[slot]).start(priority=k)
            return carry

        lax.fori_loop(0, tm, issue, 0, unroll=DMA_ISSUE_UNROLL)

    slot = lax.rem(step, 2)

    @pl.when(step == 0)
    def _():
        gather(0, 0)

    @pl.when(step + 1 < n_steps)
    def _():
        gather(step + 1, 1 - slot)

    for _ in range(2):
        pltpu.make_async_copy(y_ref.at[pl.ds(0, tm)], y_ref.at[pl.ds(0, tm)], sem.at[slot]).wait()
    w = w_ref[0]
    y = (_load_row_tiles(buf.at[slot, 0], tm) * w[:, 0:1]
         + _load_row_tiles(buf.at[slot, 1], tm) * w[:, 1:2])
    rows = i * tm + lax.broadcasted_iota(jnp.int32, (tm, 1), 0)
    gate = jnp.where(rows < n_ctx, modc_ref[0, 5:6, :], modl_ref[0, 5:6, :])
    out = x_ref[0] + gate * y
    if final_norm:
        out = out * lax.rsqrt(jnp.mean(out * out, axis=-1, keepdims=True) + NORM_EPS) * fg_ref[...]
    o_ref[0] = out


def _combine(pos, ys, w_col, xmid, modtab, final_g, n_ctx, final_norm):
    b, t, d = xmid.shape
    tm = ROW_TILE
    row = lambda w: pl.BlockSpec((1, tm, w), lambda bi, i, pos_ref: (bi, i, 0))
    grid_spec = pltpu.PrefetchScalarGridSpec(
        num_scalar_prefetch=1,
        grid=(b, t // tm),
        in_specs=[pl.BlockSpec(memory_space=pl.ANY), row(LANES), row(d),
                  pl.BlockSpec((1, 8, d), lambda bi, i, pos_ref: (2 * bi, 0, 0)),
                  pl.BlockSpec((1, 8, d), lambda bi, i, pos_ref: (2 * bi + 1, 0, 0)),
                  pl.BlockSpec(final_g.shape, lambda bi, i, pos_ref: (0, 0))],
        out_specs=row(d),
        scratch_shapes=[pltpu.VMEM((2, 2, tm * d // LANES, LANES), F32), pltpu.SemaphoreType.DMA((2,))])
    return pl.pallas_call(
        functools.partial(_combine_kernel, n_tok=b * t, n_ctx=n_ctx, final_norm=final_norm),
        grid_spec=grid_spec,
        out_shape=jax.ShapeDtypeStruct((b, t, d), F32),
        compiler_params=_params("arbitrary", "arbitrary"),
        name="moe_combine",
    )(pos, ys, w_col, xmid, modtab, modtab, final_g)


def _moe(f, sel, w_col, counts, wg, wu, wd, layer, xmid, modtab, final_g, n_ctx, final_norm):
    b, t, _ = xmid.shape
    pos, tile_expert, tiles_used, n_tiles = _moe_plan(sel, counts, b * t)
    xs = _dispatch(pos, f, n_tiles * EXPERT_ROW_TILE, b * t)
    ys = _expert_ffn(tile_expert, tiles_used, xs, wg, wu, wd, layer, n_tiles)
    return _combine(pos, ys, w_col, xmid, modtab, final_g, n_ctx, final_norm)


def _chan_dft_kernel(x_ref, mod_ref, g_ref, w_ref, zr_ref, zi_ref):
    h = _norm_mod(x_ref[0], g_ref[...], mod_ref[0, 0:1, :], mod_ref[0, 1:2, :])
    gd = w_ref.shape[0]
    for gi in range(h.shape[1] // gd):
        z = _dot_split(h[:, gi * gd:(gi + 1) * gd], w_ref[...])
        zr_ref[0, :, gi * gd:(gi + 1) * gd] = z[:, :gd]
        zi_ref[0, :, gi * gd:(gi + 1) * gd] = z[:, gd:]


def _chan_dft(x, modtab, g, w_cs, x_row_off, t):
    b, _, d = x.shape
    tm = ROW_TILE
    off = x_row_off // tm
    row = pl.BlockSpec((1, tm, d), lambda bi, i: (bi, i, 0))
    return pl.pallas_call(
        _chan_dft_kernel,
        grid=(b, t // tm),
        in_specs=[pl.BlockSpec((1, tm, d), lambda bi, i: (bi, i + off, 0)),
                  pl.BlockSpec((1, 8, d), lambda bi, i: (2 * bi + 1, 0, 0)),
                  pl.BlockSpec(g.shape, lambda bi, i: (0, 0)),
                  pl.BlockSpec(w_cs.shape, lambda bi, i: (0, 0))],
        out_specs=[row, row],
        out_shape=[jax.ShapeDtypeStruct((b, t, d), F32)] * 2,
        compiler_params=_params("parallel", "parallel"),
        name="chan_dft",
    )(x, modtab, g, w_cs)


DFT_SUB = 8


def _dft1_kernel(zr_ref, zi_ref, w_ref, yr_ref, yi_ref):
    _, n1, sub, cols = zr_ref.shape
    z = jnp.concatenate([jnp.concatenate([ref[0, :, j, :] for j in range(sub)], axis=1)
                         for ref in (zr_ref, zi_ref)], axis=0)
    y = _dot_split(w_ref[...], z)
    for j in range(sub):
        yr_ref[0, :, j, :] = y[:n1, j * cols:(j + 1) * cols]
        yi_ref[0, :, j, :] = y[n1:, j * cols:(j + 1) * cols]


def _dft1(zr, zi, w1):
    b, n1, n2, d = zr.shape
    cols = d // 2
    blk = pl.BlockSpec((1, n1, DFT_SUB, cols), lambda bi, j, c: (bi, 0, j, c))
    return pl.pallas_call(
        _dft1_kernel,
        grid=(b, n2 // DFT_SUB, d // cols),
        in_specs=[blk, blk, pl.BlockSpec(w1.shape, lambda bi, j, c: (0, 0))],
        out_specs=[blk, blk],
        out_shape=[jax.ShapeDtypeStruct(zr.shape, F32)] * 2,
        compiler_params=_params("parallel", "parallel", "parallel"),
        name="dft_stage1",
    )(zr, zi, w1)


def _dft2_kernel(yr_ref, yi_ref, tab_ref, o_ref):
    for j in range(yr_ref.shape[1]):
        y = jnp.concatenate([yr_ref[0, j], yi_ref[0, j]], axis=0)
        o_ref[0, :, j, :] = _dot_split(tab_ref[j], y)


def _dft2(yr, yi, tab):
    b, n1, n2, d = yr.shape
    blk = pl.BlockSpec((1, DFT_SUB, n2, d), lambda bi, k1: (bi, k1, 0, 0))
    return pl.pallas_call(
        _dft2_kernel,
        grid=(b, n1 // DFT_SUB),
        in_specs=[blk, blk, pl.BlockSpec((DFT_SUB, n2, 2 * n2), lambda bi, k1: (k1, 0, 0))],
        out_specs=pl.BlockSpec((1, n2, DFT_SUB, d), lambda bi, k1: (bi, 0, k1, 0)),
        out_shape=jax.ShapeDtypeStruct((b, n2, n1, d), F32),
        compiler_params=_params("parallel", "parallel"),
        name="dft_stage2",
    )(yr, yi, tab)


def _dft_tables(t, gd):
    n2 = ML_CHUNK
    n1 = t // n2
    def cs(num, den):
        ang = (2.0 * np.pi / den) * (num % den).astype(np.float64)
        return np.cos(ang), np.sin(ang)
    c = np.arange(gd)
    cc, sc = cs(np.outer(c, c), gd)
    w_cs = np.concatenate([cc, -sc], axis=1)
    a = np.arange(n1)
    c1, s1 = cs(np.outer(a, a), n1)
    w1 = np.block([[c1, s1], [-s1, c1]])
    k = a[:, None, None] + n1 * np.arange(n2)[None, :, None]
    c2, s2 = cs(k * np.arange(n2)[None, None, :], t)
    tab = np.concatenate([c2, s2], axis=2) / math.sqrt(t * gd)
    return (jnp.asarray(w_cs, F32), jnp.asarray(w1, F32), jnp.asarray(tab, F32))


def _rope_tables(n_ctx, n_lat):
    pos = jnp.arange(n_lat, dtype=jnp.int32)
    n_axis = DA_QK_DIM // 4
    inv = ROPE_BASE ** (-jnp.arange(n_axis, dtype=F32) / n_axis)
    ang = jnp.concatenate([(pos // GRID_W).astype(F32)[:, None] * inv,
                           (pos % GRID_W).astype(F32)[:, None] * inv], axis=-1)
    cos, sin = jnp.cos(ang), jnp.sin(ang)
    cos = jnp.concatenate([jnp.ones((n_ctx, 2 * n_axis), F32), cos], axis=0)
    sin = jnp.concatenate([jnp.zeros((n_ctx, 2 * n_axis), F32), sin], axis=0)
    cos128 = jnp.concatenate([cos, cos, cos, cos], axis=1)
    sin128 = jnp.concatenate([-sin, sin, -sin, sin], axis=1)
    return cos128, sin128


def _deinterleave(w):
    d, n = w.shape
    w = w.reshape(d, n // DA_QK_DIM, DA_QK_DIM // 2, 2)
    return jnp.concatenate([w[..., 0], w[..., 1]], axis=-1).reshape(d, n)


def _pad_rows(a, rows):
    return jnp.concatenate([a, jnp.zeros((rows - a.shape[0],) + a.shape[1:], a.dtype)], axis=0)


def _pad_cols(a, cols):
    return jnp.concatenate([a, jnp.zeros(a.shape[:-1] + (cols - a.shape[-1],), a.dtype)], axis=-1)


def kernel(x, c, ctx, c_ctx, ada_w, ada_b, norm_mix_g, norm_ffn_g, even_w_in, even_w_out,
           even_conv_w, even_gate_b, even_lam, even_subln_g, odd_w_fnet, router_w, router_b,
           exp_w_gate, exp_w_up, exp_w_down, final_g):
    b, n_lat, d = x.shape
    n_ctx = ctx.shape[1]
    depth = ada_w.shape[0]
    assert depth == 2 and b + 1 <= 8
    assert n_ctx % ROW_TILE == 0 and n_lat % ROW_TILE == 0

    cond8 = _pad_rows(jnp.concatenate([c_ctx[None, :], c], axis=0), 8)
    rwt = router_w.T
    rb = jnp.broadcast_to(router_b[:, None], (N_EXPERTS, LANES))
    row2 = lambda v: v.reshape(1, -1)

    def modtab_for(layer):
        mods = _ada_mods(cond8, ada_w[layer], ada_b[layer]).reshape(8, 6, d)
        mods = jnp.concatenate([mods, jnp.zeros((8, 2, d), F32)], axis=1)
        idx = np.array([[0, 1 + bi] for bi in range(b)]).reshape(-1)
        return mods[idx]

    xs = jnp.concatenate([ctx, x], axis=1)
    modtab = modtab_for(0)
    w_in = even_w_in[0]
    o1 = DA_HEADS * 2 * DA_QK_DIM
    o2 = 2 * o1
    o3 = o2 + DA_HEADS * DA_V_DIM
    o4 = o3 + 2 * ML_HEADS * ML_DIM
    o5 = o4 + ML_HEADS * ML_DIM
    o6 = o5 + ML_HEADS * ML_DIM
    ws = [(_deinterleave(w_in[:, :o1]) * (DA_QK_DIM ** -0.5)).astype(BF16),
          _deinterleave(w_in[:, o1:o2]).astype(BF16),
          w_in[:, o2:o3].astype(BF16), w_in[:, o3:o4].astype(BF16),
          w_in[:, o4:o5].astype(BF16), w_in[:, o5:o6].astype(BF16),
          _pad_cols(w_in[:, o6:], LANES).astype(BF16)]
    gate_b = _pad_cols(even_gate_b[0].reshape(1, -1), LANES)
    cos128, sin128 = _rope_tables(n_ctx, n_lat)
    daq, dak, dav, mqk, mv, og, gates = _inproj(xs, modtab, row2(norm_mix_g[0]), cos128, sin128,
                                                 ws, gate_b, n_ctx)
    mq, mk, gc, gr, mvt = _mlprep(mqk, _pad_rows(even_conv_w[0], 8), gates, mv, n_ctx)
    hf, hb = _mlstm(mq, mk, mvt, gc, gr, n_ctx)
    lam_init = 0.8 - 0.6 * math.exp(-0.3 * 0)
    lam8 = _pad_rows(even_lam[0], 8)
    att = _diff_attention(daq, dak, dav, lam8, row2(even_subln_g[0]), n_ctx, lam_init)
    w_out = even_w_out[0].astype(BF16)
    half = DA_HEADS * DA_V_DIM
    xmid, f, sel, w_col, counts = _post_mixer(xs, [att, hf, hb, og], [w_out[:half], w_out[half:]],
                                              modtab, row2(norm_ffn_g[0]), rwt, rb, n_ctx, 0, True)
    xs = _moe(f, sel, w_col, counts, exp_w_gate, exp_w_up, exp_w_down, 0,
              xmid, modtab, row2(final_g), n_ctx, False)

    modtab = modtab_for(1)
    gd = d // FN_GROUPS
    w_cs, w1, tab = _dft_tables(n_lat, gd)
    n2 = ML_CHUNK
    n1 = n_lat // n2
    zr, zi = _chan_dft(xs, modtab, row2(norm_mix_g[1]), w_cs, n_ctx, n_lat)
    yr, yi = _dft1(zr.reshape(b, n1, n2, d), zi.reshape(b, n1, n2, d), w1)
    fo = _dft2(yr, yi, tab).reshape(b, n_lat, d)
    xmid, f, sel, w_col, counts = _post_mixer(xs, [fo], [odd_w_fnet[0].astype(BF16)], modtab,
                                              row2(norm_ffn_g[1]), rwt, rb, 0, n_ctx, False)
    return _moe(f, sel, w_col, counts, exp_w_gate, exp_w_up, exp_w_down, 1,
                xmid, modtab, row2(final_g), 0, True)
```

```python
import functools
import math

import jax
import jax.numpy as jnp
import numpy as np
from jax import lax
from jax.experimental import pallas as pl
from jax.experimental.pallas import tpu as pltpu

F32 = jnp.float32
BF16 = jnp.bfloat16

NORM_EPS = 1e-6
GRID_W = 64
DA_HEADS = 4
DA_QK_DIM = 64
DA_V_DIM = 128
ML_HEADS = 4
ML_DIM = 128
ML_CHUNK = 128
FN_GROUPS = 4
N_EXPERTS = 16
EXPERTS_PER_GROUP = 4
ROPE_BASE = 10000.0
LANES = 128
ROW_TILE = 256
VMEM_LIMIT_BYTES = 56 * 1024 * 1024
HI = lax.Precision.HIGHEST
LOG2_E = math.log2(math.e)


def _params(*sem):
    return pltpu.CompilerParams(dimension_semantics=sem, vmem_limit_bytes=VMEM_LIMIT_BYTES)


def _dot(a, b, precision=None):
    return jnp.dot(a, b, preferred_element_type=F32, precision=precision)


def _dot_nt(a, b, precision=None):
    return lax.dot_general(a, b, (((1,), (1,)), ((), ())), preferred_element_type=F32,
                           precision=precision)


def _dot_split(a, b):
    a_hi = a.astype(BF16)
    b_hi = b.astype(BF16)
    a_lo = (a - a_hi.astype(F32)).astype(BF16)
    b_lo = (b - b_hi.astype(F32)).astype(BF16)
    return _dot(a_hi, b_hi) + (_dot(a_hi, b_lo) + _dot(a_lo, b_hi))


def _sigmoid(x):
    return 1.0 / (1.0 + jnp.exp(-x))


def _silu(x):
    return x * _sigmoid(x)


def _norm_mod(x, g, shift, scale):
    y = x * lax.rsqrt(jnp.mean(x * x, axis=-1, keepdims=True) + NORM_EPS) * g
    return y * (1.0 + scale) + shift


def _ada_kernel(c_ref, w_ref, b_ref, o_ref):
    o_ref[...] = _dot(_silu(c_ref[...]), w_ref[...], HI) + b_ref[...]


def _ada_mods(cond8, w, b):
    d, n = w.shape
    tn = n // 6
    return pl.pallas_call(
        _ada_kernel,
        grid=(6,),
        in_specs=[pl.BlockSpec((8, d), lambda j: (0, 0)),
                  pl.BlockSpec((d, tn), lambda j: (0, j)),
                  pl.BlockSpec((1, tn), lambda j: (0, j))],
        out_specs=pl.BlockSpec((8, tn), lambda j: (0, j)),
        out_shape=jax.ShapeDtypeStruct((8, n), F32),
        compiler_params=_params("arbitrary"),
        name="ada_mods",
    )(cond8, w, b.reshape(1, n))


def _inproj_kernel(x_ref, mod_ref, g_ref, cos_ref, sin_ref, wq_ref, wk_ref, wv_ref, wmqk_ref,
                   wmv_ref, wmo_ref, wg_ref, gb_ref,
                   q_ref, k_ref, v_ref, mqk_ref, mv_ref, og_ref, gate_ref):
    x = x_ref[0]
    h = _norm_mod(x, g_ref[...], mod_ref[0, 0:1, :], mod_ref[0, 1:2, :]).astype(BF16)
    tm = x.shape[0]
    width = q_ref.shape[2]
    cos = jnp.concatenate([cos_ref[...]] * (width // LANES), axis=1)
    sin = jnp.concatenate([sin_ref[...]] * (width // LANES), axis=1)
    lane = lax.broadcasted_iota(jnp.int32, (tm, width), 1)
    lower = (lane & (DA_QK_DIM - 1)) < (DA_QK_DIM // 2)

    def rope(u):
        swapped = jnp.where(lower, pltpu.roll(u, width - DA_QK_DIM // 2, 1),
                            pltpu.roll(u, DA_QK_DIM // 2, 1))
        return u * cos + swapped * sin

    q_ref[0] = (rope(_dot(h, wq_ref[...])) * LOG2_E).astype(BF16)
    k_ref[0] = rope(_dot(h, wk_ref[...])).astype(BF16)
    v = _dot(h, wv_ref[...])
    pad = VT_ROWS - DA_V_DIM
    ones_row = (lax.broadcasted_iota(jnp.int32, (pad, tm), 0) == 0).astype(BF16)
    for hd in range(DA_HEADS):
        v_ref[0, hd, 0, 0:DA_V_DIM, :] = v[:, hd * DA_V_DIM:(hd + 1) * DA_V_DIM].T.astype(BF16)
        v_ref[0, hd, 0, DA_V_DIM:VT_ROWS, :] = ones_row
    mqk_ref[0] = _dot(h, wmqk_ref[...]).astype(BF16)
    mv_ref[0] = _dot(h, wmv_ref[...]).astype(BF16)
    og_ref[0] = _sigmoid(_dot(h, wmo_ref[...])).astype(BF16)
    g = _dot(h, wg_ref[...]) + gb_ref[...]
    glane = lax.broadcasted_iota(jnp.int32, g.shape, 1)
    is_forget = ((glane // ML_HEADS) & 1) == 1
    log_sig = jnp.minimum(g, 0.0) - jnp.log(1.0 + jnp.exp(-jnp.abs(g)))
    gate_ref[0] = jnp.where(is_forget, log_sig, g)


def _inproj(xs, modtab, g, cos, sin, ws, gate_b, n_ctx):
    b, t, d = xs.shape
    tm = ROW_TILE
    nt = t // tm
    ctx_tiles = n_ctx // tm
    row = lambda w: pl.BlockSpec((1, tm, w), lambda bi, i: (bi, i, 0))
    full = lambda a: pl.BlockSpec(a.shape, lambda bi, i: (0,) * a.ndim)
    widths = [w.shape[1] for w in ws]
    out_dtypes = [BF16] * 6 + [F32]
    out_specs = [row(w) for w in widths]
    out_shape = [jax.ShapeDtypeStruct((b, t, w), dt) for w, dt in zip(widths, out_dtypes)]
    out_specs[2] = pl.BlockSpec((1, DA_HEADS, 1, VT_ROWS, tm), lambda bi, i: (bi, 0, i, 0, 0))
    out_shape[2] = jax.ShapeDtypeStruct((b, DA_HEADS, nt, VT_ROWS, tm), BF16)
    return pl.pallas_call(
        _inproj_kernel,
        grid=(b, nt),
        in_specs=[row(d),
                  pl.BlockSpec((1, 8, d), lambda bi, i: (2 * bi + (i >= ctx_tiles).astype(jnp.int32), 0, 0)),
                  full(g),
                  pl.BlockSpec((tm, LANES), lambda bi, i: (i, 0)),
                  pl.BlockSpec((tm, LANES), lambda bi, i: (i, 0))]
                 + [full(w) for w in ws] + [full(gate_b)],
        out_specs=out_specs,
        out_shape=out_shape,
        compiler_params=_params("parallel", "parallel"),
        name="inproj",
    )(xs, modtab, g, cos, sin, *ws, gate_b)


def _split3(x):
    x1 = x.astype(BF16)
    r1 = x - x1.astype(F32)
    x2 = r1.astype(BF16)
    x3 = (r1 - x2.astype(F32)).astype(BF16)
    return x1, x2, x3


VT_ROWS = ML_DIM + 16


def _mlprep_kernel(cur_ref, prev_ref, next_ref, cw_ref, gate_ref, v_ref,
                   mq_ref, mk_ref, gc_ref, gr_ref, vt_ref, *, ctx_tiles, n_tiles):
    i = pl.program_id(1)
    cur = cur_ref[0].astype(F32)
    tm, w = cur.shape
    prev_ok = i != ctx_tiles
    if ctx_tiles > 0:
        prev_ok = jnp.logical_and(prev_ok, i != 0)
        next_ok = jnp.logical_and(i != ctx_tiles - 1, i != n_tiles - 1)
    else:
        next_ok = i != n_tiles - 1
    prev_row = jnp.where(prev_ok, prev_ref[0, 7:8, :].astype(F32), 0.0)
    next_row = jnp.where(next_ok, next_ref[0, 0:1, :].astype(F32), 0.0)
    ridx = lax.broadcasted_iota(jnp.int32, (tm, w), 0)
    before = jnp.where(ridx == 0, prev_row, pltpu.roll(cur, 1, 0))
    after = jnp.where(ridx == tm - 1, next_row, pltpu.roll(cur, tm - 1, 0))
    y = _silu(before * cw_ref[0:1, :] + cur * cw_ref[1:2, :] + after * cw_ref[2:3, :])
    half = w // 2
    mq_ref[0] = y[:, :half].astype(BF16)
    mk_ref[0] = (y[:, half:] * (ML_DIM ** -0.5)).astype(BF16)

    n_g = 4 * ML_HEADS
    hds = ML_HEADS
    r = lax.broadcasted_iota(jnp.int32, (ML_CHUNK, ML_CHUNK), 0)
    c = lax.broadcasted_iota(jnp.int32, (ML_CHUNK, ML_CHUNK), 1)
    lower = (c <= r).astype(BF16)
    upper = (c >= r).astype(BF16)
    ones_row = (r == 0).astype(BF16)[0:VT_ROWS - ML_DIM]
    for ci in range(tm // ML_CHUNK):
        rows = slice(ci * ML_CHUNK, (ci + 1) * ML_CHUNK)
        gm = jnp.where(c < n_g, gate_ref[0, rows, :], 0.0)
        pre = sum(_dot(lower, p) for p in _split3(pltpu.roll(gm, n_g, 1)))
        suf = sum(_dot(upper, p) for p in _split3(pltpu.roll(gm, 2 * n_g, 1)))
        col = gm + pre + suf
        u_f = pltpu.roll(col, 3 * n_g, 1) - pltpu.roll(col, 3 * n_g - (n_g + hds), 1)
        u_b = pltpu.roll(col, 3 * n_g - hds, 1) - pltpu.roll(col, 2 * hds, 1)
        col = col + jnp.where(jnp.logical_and(c >= 3 * n_g, c < 3 * n_g + hds), u_f,
                              jnp.where(jnp.logical_and(c >= 3 * n_g + hds, c < 3 * n_g + 2 * hds),
                                        u_b, 0.0))
        gc_ref[0, rows, :] = col
        gr_ref[0, ci] = col.T
        for hd in range(ML_HEADS):
            v_t = v_ref[0, rows, hd * ML_DIM:(hd + 1) * ML_DIM].astype(F32).T.astype(BF16)
            vt_ref[0, ci, hd * VT_ROWS:hd * VT_ROWS + ML_DIM, :] = v_t
            vt_ref[0, ci, hd * VT_ROWS + ML_DIM:(hd + 1) * VT_ROWS, :] = ones_row


def _mlprep(mqk, conv_w8, gates, mv, n_ctx):
    b, t, w = mqk.shape
    tm = ROW_TILE
    nt = t // tm
    sub = tm // 8
    nsub = t // 8
    cpt = tm // ML_CHUNK
    kern = functools.partial(_mlprep_kernel, ctx_tiles=n_ctx // tm, n_tiles=nt)
    return pl.pallas_call(
        kern,
        grid=(b, nt),
        in_specs=[pl.BlockSpec((1, tm, w), lambda bi, i: (bi, i, 0)),
                  pl.BlockSpec((1, 8, w), lambda bi, i: (bi, jnp.maximum(i * sub - 1, 0), 0)),
                  pl.BlockSpec((1, 8, w), lambda bi, i: (bi, jnp.minimum((i + 1) * sub, nsub - 1), 0)),
                  pl.BlockSpec(conv_w8.shape, lambda bi, i: (0, 0)),
                  pl.BlockSpec((1, tm, LANES), lambda bi, i: (bi, i, 0)),
                  pl.BlockSpec((1, tm, w // 2), lambda bi, i: (bi, i, 0))],
        out_specs=[pl.BlockSpec((1, tm, w // 2), lambda bi, i: (bi, i, 0)),
                   pl.BlockSpec((1, tm, w // 2), lambda bi, i: (bi, i, 0)),
                   pl.BlockSpec((1, tm, LANES), lambda bi, i: (bi, i, 0)),
                   pl.BlockSpec((1, cpt, ML_CHUNK, LANES), lambda bi, i: (bi, i, 0, 0)),
                   pl.BlockSpec((1, cpt, ML_HEADS * VT_ROWS, ML_CHUNK), lambda bi, i: (bi, i, 0, 0))],
        out_shape=[jax.ShapeDtypeStruct((b, t, w // 2), BF16),
                   jax.ShapeDtypeStruct((b, t, w // 2), BF16),
                   jax.ShapeDtypeStruct((b, t, LANES), F32),
                   jax.ShapeDtypeStruct((b, t // ML_CHUNK, ML_CHUNK, LANES), F32),
                   jax.ShapeDtypeStruct((b, t // ML_CHUNK, ML_HEADS * VT_ROWS, ML_CHUNK), BF16)],
        compiler_params=_params("parallel", "parallel"),
        name="mlstm_prep",
    )(mqk, mqk, mqk, conv_w8, gates, mv)


def _mlstm_kernel(qf_ref, kf_ref, vf_ref, gcf_ref, grf_ref, qb_ref, kb_ref, vb_ref, gcb_ref, grb_ref,
                  hf_ref, hb_ref, *scr):
    s = pl.program_id(0)
    nb = qf_ref.shape[0]
    n_g = 4 * ML_HEADS
    state_refs = scr[:len(scr) // 2]
    m_refs = scr[len(scr) // 2:]

    @pl.when(s == 0)
    def _():
        for ref in scr:
            ref[...] = jnp.zeros(ref.shape, F32)

    ki = lax.broadcasted_iota(jnp.int32, (ML_CHUNK, ML_CHUNK), 0)
    qi = lax.broadcasted_iota(jnp.int32, (ML_CHUNK, ML_CHUNK), 1)

    for bi in range(nb):
        for direction in range(2):
            q_ref, k_ref, vt_ref, gc_ref, gr_ref, h_ref = (
                (qf_ref, kf_ref, vf_ref, gcf_ref, grf_ref, hf_ref) if direction == 0 else
                (qb_ref, kb_ref, vb_ref, gcb_ref, grb_ref, hb_ref))
            visible = (ki <= qi) if direction == 0 else (ki >= qi)
            gc = gc_ref[bi]
            gr = gr_ref[bi, 0]
            for hd in range(ML_HEADS):
                chain = (bi * 2 + direction) * ML_HEADS + hd
                c_lf = (2 * direction + 1) * ML_HEADS + hd
                c_cs = c_lf + (n_g if direction == 0 else 2 * n_g)
                c_u = 3 * n_g + direction * ML_HEADS + hd
                lo, hi = hd * ML_DIM, (hd + 1) * ML_DIM
                q = q_ref[bi, :, lo:hi]
                k = k_ref[bi, :, lo:hi]
                v_t = vt_ref[bi, 0, hd * VT_ROWS:(hd + 1) * VT_ROWS, :]
                u_row = gr[c_u:c_u + 1, :]
                lf_row = gr[c_lf:c_lf + 1, :]
                cs_row = gr[c_cs:c_cs + 1, :]
                m_st = m_refs[chain][:, 0:1]
                st_t = state_refs[chain][...]

                u_vis = jnp.where(visible, jnp.broadcast_to(gc[:, c_u:c_u + 1], visible.shape), -jnp.inf)
                v_row = jnp.maximum(m_st, jnp.max(u_vis, axis=0, keepdims=True))
                d_t = jnp.exp(u_vis - v_row)
                inter = jnp.exp(m_st - v_row)
                sc_t = (_dot_nt(k, q) * d_t).astype(BF16)
                tot_t = inter * _dot_nt(st_t.astype(BF16), q) + _dot(v_t, sc_t)
                den = jnp.maximum(jnp.abs(tot_t[ML_DIM:ML_DIM + 1, :]), jnp.exp(-(cs_row + v_row)))
                h_ref[bi, :, lo:hi] = (tot_t[:ML_DIM, :] / den).T

                b_last = jnp.sum(lf_row, axis=1, keepdims=True)
                m_sc = jnp.maximum(m_st, jnp.max(u_row, axis=1, keepdims=True))
                wgt = jnp.exp(u_row - m_sc)
                vw_t = (v_t.astype(F32) * wgt).astype(BF16)
                state_refs[chain][...] = jnp.exp(m_st - m_sc) * st_t + _dot(vw_t, k)
                m_refs[chain][...] = jnp.broadcast_to(b_last + m_sc, (1, LANES))


def _mlstm(mq, mk, mvt, gc, gr, n_ctx):
    b, t, w = mq.shape
    nc = t // ML_CHUNK
    ncc = n_ctx // ML_CHUNK
    fwd = lambda s: s
    bwd = lambda s: jnp.where(s < ncc, ncc - 1 - s, nc - 1 - s + ncc)
    tok = lambda f: pl.BlockSpec((b, ML_CHUNK, w), lambda s: (0, f(s), 0))
    gcs = lambda f: pl.BlockSpec((b, ML_CHUNK, LANES), lambda s: (0, f(s), 0))
    grs = lambda f: pl.BlockSpec((b, 1, ML_CHUNK, LANES), lambda s: (0, f(s), 0, 0))
    vts = lambda f: pl.BlockSpec((b, 1, ML_HEADS * VT_ROWS, ML_CHUNK), lambda s: (0, f(s), 0, 0))
    n_chain = b * 2 * ML_HEADS
    return pl.pallas_call(
        _mlstm_kernel,
        grid=(nc,),
        in_specs=[tok(fwd), tok(fwd), vts(fwd), gcs(fwd), grs(fwd),
                  tok(bwd), tok(bwd), vts(bwd), gcs(bwd), grs(bwd)],
        out_specs=[tok(fwd), tok(bwd)],
        out_shape=[jax.ShapeDtypeStruct((b, t, w), F32)] * 2,
        scratch_shapes=[pltpu.VMEM((VT_ROWS, ML_DIM), F32)] * n_chain
                       + [pltpu.VMEM((1, LANES), F32)] * n_chain,
        compiler_params=_params("arbitrary"),
        name="mlstm",
    )(mq, mk, mvt, gc, gr, mq, mk, mvt, gc, gr)


def _attn_kernel(q_ref, qn_ref, k_ref, vt_ref, lam_ref, sg_ref, o_ref, qm_scr, *scr,
                 ctx_tiles, n_ctx, key_block, n_blocks, lam_init):
    s_scr = (scr[0:2], scr[2:4])
    bm_scr = (scr[4:6], scr[6:8])
    m_scr = scr[8:10]
    acc_scr = scr[10:12]
    i = pl.program_id(2)
    dv = DA_V_DIM
    q = q_ref[0]
    lane = lax.broadcasted_iota(jnp.int32, q.shape, 1)
    zero = jnp.zeros_like(q)

    def map_halves(qv):
        return jnp.where(lane < DA_QK_DIM, qv, zero), jnp.where(lane >= DA_QK_DIM, qv, zero)

    qm_scr[0], qm_scr[1] = map_halves(q)

    def scores(mp, start, size, q_maps=None):
        q_map = qm_scr[mp] if q_maps is None else q_maps[mp]
        return _dot_nt(k_ref[0, pl.ds(start, size), :], q_map)

    @pl.when(i < ctx_tiles)
    def _():
        for mp in range(2):
            sc = scores(mp, 0, n_ctx)
            m = jnp.max(sc, axis=0, keepdims=True)
            p = jnp.exp2(sc - m).astype(BF16)
            acc_scr[mp][...] = _dot(vt_ref[0, 0, 0][:, 0:n_ctx], p)

    def stage(j, slot, q_maps=None):
        start = pl.multiple_of(j * key_block, LANES)
        for mp in range(2):
            sc = scores(mp, start, key_block, q_maps)
            s_scr[mp][slot][...] = sc
            bm_scr[mp][slot][...] = jnp.max(sc, axis=0, keepdims=True)

    def consume(j, slot):
        for mp in range(2):
            m_old = m_scr[mp][...]
            m_new = jnp.maximum(m_old, bm_scr[mp][slot][...])
            alpha = jnp.exp2(m_old - m_new)
            p = jnp.exp2(s_scr[mp][slot][...] - m_new).astype(BF16)
            acc_scr[mp][...] = alpha * acc_scr[mp][...] + _dot(vt_ref[0, 0, j], p)
            m_scr[mp][...] = m_new

    @pl.when(i >= ctx_tiles)
    def _():
        for mp in range(2):
            m_scr[mp][...] = jnp.full(m_scr[mp].shape, -jnp.inf, F32)
            acc_scr[mp][...] = jnp.zeros(acc_scr[mp].shape, F32)

        @pl.when(i == ctx_tiles)
        def _():
            stage(0, 0)

        def body(g, carry):
            stage(2 * g + 1, 1)
            consume(2 * g, 0)
            stage(2 * g + 2, 0)
            consume(2 * g + 1, 1)
            return carry
        lax.fori_loop(0, n_blocks // 2 - 1, body, 0)
        stage(n_blocks - 1, 1)
        consume(n_blocks - 2, 0)
        stage(0, 0, map_halves(qn_ref[0]))
        consume(n_blocks - 1, 1)

    lv = lam_ref[...]
    dot01 = jnp.sum(lv[0:1, :] * lv[1:2, :], axis=1, keepdims=True)
    dot23 = jnp.sum(lv[2:3, :] * lv[3:4, :], axis=1, keepdims=True)
    lam = jnp.exp(dot01) - jnp.exp(dot23) + lam_init
    a0 = acc_scr[0][...]
    a1 = acc_scr[1][...]
    o = (a0[0:dv, :] / a0[dv:dv + 1, :] - lam * (a1[0:dv, :] / a1[dv:dv + 1, :])).T
    o = o * lax.rsqrt(jnp.mean(o * o, axis=-1, keepdims=True) + NORM_EPS) * sg_ref[...]
    o_ref[0] = (o * (1.0 - lam_init)).astype(BF16)


def _diff_attention(q, k, vt, lam8, subln_g, n_ctx, lam_init):
    b, t, w = q.shape
    tq = ROW_TILE
    n_blocks = next(n for n in (10, 8, 6, 4, 2) if t % (n * LANES) == 0 and t // n >= n_ctx)
    key_block = t // n_blocks
    _, _, n_chunks, vt_rows, chunk = vt.shape
    assert n_ctx % tq == 0
    vt = vt.transpose(0, 1, 3, 2, 4).reshape(b, DA_HEADS, vt_rows, n_blocks, key_block)
    vt = vt.transpose(0, 1, 3, 2, 4)
    kern = functools.partial(_attn_kernel, ctx_tiles=n_ctx // tq, n_ctx=n_ctx, key_block=key_block,
                             n_blocks=n_blocks, lam_init=lam_init)
    return pl.pallas_call(
        kern,
        grid=(b, DA_HEADS, t // tq),
        in_specs=[pl.BlockSpec((1, tq, LANES), lambda bi, h, i: (bi, i, h)),
                  pl.BlockSpec((1, tq, LANES), lambda bi, h, i: (bi, jnp.minimum(i + 1, t // tq - 1), h)),
                  pl.BlockSpec((1, t, LANES), lambda bi, h, i: (bi, 0, h)),
                  pl.BlockSpec((1, 1, n_blocks, vt_rows, key_block), lambda bi, h, i: (bi, h, 0, 0, 0)),
                  pl.BlockSpec(lam8.shape, lambda bi, h, i: (0, 0)),
                  pl.BlockSpec(subln_g.shape, lambda bi, h, i: (0, 0))],
        out_specs=pl.BlockSpec((1, tq, LANES), lambda bi, h, i: (bi, i, h)),
        out_shape=jax.ShapeDtypeStruct((b, t, w), BF16),
        scratch_shapes=[pltpu.VMEM((2, tq, LANES), BF16)]
                       + [pltpu.VMEM((key_block, tq), F32)] * 4
                       + [pltpu.VMEM((1, tq), F32)] * 6
                       + [pltpu.VMEM((vt_rows, tq), F32)] * 2,
        compiler_params=_params("arbitrary", "arbitrary", "arbitrary"),
        name="diff_attention",
    )(q, q, k, vt, lam8, subln_g)


def _top2_sum(a, b, c, d):
    hi1, lo1 = jnp.maximum(a, b), jnp.minimum(a, b)
    hi2, lo2 = jnp.maximum(c, d), jnp.minimum(c, d)
    return jnp.maximum(hi1, hi2) + jnp.maximum(jnp.minimum(hi1, hi2), jnp.maximum(lo1, lo2))


def _route(f, rwt_ref, rb_ref, cnt_ref):
    tm = f.shape[0]
    aff = _sigmoid(_dot_nt(rwt_ref[...], f, HI))
    biased = aff + rb_ref[:, 0:1]
    bz = [biased[e:e + 1, :] for e in range(N_EXPERTS)]
    af = [aff[e:e + 1, :] for e in range(N_EXPERTS)]
    n_grp = N_EXPERTS // EXPERTS_PER_GROUP
    scores = [_top2_sum(*bz[EXPERTS_PER_GROUP * g:EXPERTS_PER_GROUP * (g + 1)]) for g in range(n_grp)]
    best = scores[0]
    sel_grp = jnp.zeros_like(best, dtype=jnp.int32)
    for g in range(1, n_grp):
        better = scores[g] > best
        sel_grp = jnp.where(better, g, sel_grp)
        best = jnp.where(better, scores[g], best)
    chosen = []
    for e in range(N_EXPERTS):
        g = e // EXPERTS_PER_GROUP
        rank = jnp.zeros_like(sel_grp)
        for o in range(EXPERTS_PER_GROUP * g, EXPERTS_PER_GROUP * (g + 1)):
            if o == e:
                continue
            beats = (bz[o] > bz[e]) if o > e else (bz[o] >= bz[e])
            rank = rank + beats.astype(jnp.int32)
        chosen.append(jnp.logical_and(sel_grp == g, rank < 2))
    denom = sum(jnp.where(chosen[e], af[e], 0.0) for e in range(N_EXPERTS))
    erow = lax.broadcasted_iota(jnp.int32, (N_EXPERTS, tm), 0)
    one_hot = jnp.zeros((N_EXPERTS, tm), F32)
    for e in range(N_EXPERTS):
        one_hot = jnp.where(jnp.logical_and(erow == e, chosen[e]), 1.0, one_hot)
    earlier = (lax.broadcasted_iota(jnp.int32, (tm, tm), 0)
               < lax.broadcasted_iota(jnp.int32, (tm, tm), 1)).astype(BF16)
    rank_all = _dot(one_hot.astype(BF16), earlier) + cnt_ref[:, 0:1]
    cnt_ref[...] = cnt_ref[...] + jnp.sum(one_hot, axis=1, keepdims=True)

    seen = jnp.zeros((1, tm), jnp.bool_)
    e_a = e_b = jnp.zeros((1, tm), jnp.int32)
    r_a = r_b = w_a = w_b = jnp.zeros((1, tm), F32)
    for e in range(N_EXPERTS):
        first = jnp.logical_and(chosen[e], jnp.logical_not(seen))
        second = jnp.logical_and(chosen[e], seen)
        rk = rank_all[e:e + 1, :]
        wt = af[e] / denom
        e_a, e_b = jnp.where(first, e, e_a), jnp.where(second, e, e_b)
        r_a, r_b = jnp.where(first, rk, r_a), jnp.where(second, rk, r_b)
        w_a, w_b = jnp.where(first, wt, w_a), jnp.where(second, wt, w_b)
        seen = jnp.logical_or(seen, chosen[e])
    r8 = lax.broadcasted_iota(jnp.int32, (8, tm), 0)
    sel = jnp.where(r8 == 0, e_a, jnp.where(r8 == 1, e_b, jnp.where(
        r8 == 2, r_a.astype(jnp.int32), jnp.where(r8 == 3, r_b.astype(jnp.int32), 0))))
    row = lax.broadcasted_iota(jnp.int32, (LANES, tm), 0)
    w_t = jnp.where(row == 0, w_a, jnp.where(row == 1, w_b, 0.0))
    return sel, w_t.T


def _post_kernel(*refs, even):
    if even:
        (x_ref, a_ref, hf_ref, hb_ref, og_ref, mod_ref, wa_ref, wm_ref, gf_ref, rwt_ref, rb_ref,
         xo_ref, f_ref, sel_ref, w_ref, cnt_ref, cnt_scr) = refs
        m = ((hf_ref[0] + hb_ref[0]) * og_ref[0].astype(F32)).astype(BF16)
        o = _dot(a_ref[0], wa_ref[...]) + _dot(m, wm_ref[...])
    else:
        (x_ref, a_ref, mod_ref, wa_ref, gf_ref, rwt_ref, rb_ref,
         xo_ref, f_ref, sel_ref, w_ref, cnt_ref, cnt_scr) = refs
        o = _dot(a_ref[0].astype(BF16), wa_ref[...])

    @pl.when(jnp.logical_and(pl.program_id(0) == 0, pl.program_id(1) == 0))
    def _():
        cnt_scr[...] = jnp.zeros(cnt_scr.shape, F32)

    x = x_ref[0] + mod_ref[0, 2:3, :] * o
    xo_ref[0] = x
    f = _norm_mod(x, gf_ref[...], mod_ref[0, 3:4, :], mod_ref[0, 4:5, :])
    _store_row_tiles(f_ref.at[0], f)
    sel, w_col = _route(f, rwt_ref, rb_ref, cnt_scr)
    sel_ref[0, 0] = sel
    w_ref[0] = w_col
    cnt_ref[...] = cnt_scr[...]


def _post_mixer(x, acts, weights, modtab, gffn, rwt, rb, n_ctx, x_row_off, even):
    b, t, _ = acts[0].shape
    d = x.shape[2]
    tm = ROW_TILE
    nt = t // tm
    ctx_tiles = n_ctx // tm
    off = x_row_off // tm
    full = lambda a: pl.BlockSpec(a.shape, lambda bi, i: (0,) * a.ndim)
    row = lambda w: pl.BlockSpec((1, tm, w), lambda bi, i: (bi, i, 0))
    mod_spec = pl.BlockSpec(
        (1, 8, d), lambda bi, i: (2 * bi + (i + off >= ctx_tiles).astype(jnp.int32), 0, 0))
    in_specs = ([pl.BlockSpec((1, tm, d), lambda bi, i: (bi, i + off, 0))]
                + [row(a.shape[2]) for a in acts] + [mod_spec]
                + [full(w) for w in weights] + [full(gffn), full(rwt), full(rb)])
    return pl.pallas_call(
        functools.partial(_post_kernel, even=even),
        grid=(b, nt),
        in_specs=in_specs,
        out_specs=[row(d), pl.BlockSpec((1, tm * d // LANES, LANES), lambda bi, i: (bi, i, 0)),
                   pl.BlockSpec((1, 1, 8, tm), lambda bi, i: (bi, i, 0, 0)),
                   row(LANES),
                   pl.BlockSpec((N_EXPERTS, LANES), lambda bi, i: (0, 0))],
        out_shape=[jax.ShapeDtypeStruct((b, t, d), F32),
                   jax.ShapeDtypeStruct((b, t * d // LANES, LANES), F32),
                   jax.ShapeDtypeStruct((b, nt, 8, tm), jnp.int32),
                   jax.ShapeDtypeStruct((b, t, LANES), F32),
                   jax.ShapeDtypeStruct((N_EXPERTS, LANES), F32)],
        scratch_shapes=[pltpu.VMEM((N_EXPERTS, LANES), F32)],
        compiler_params=_params("arbitrary", "arbitrary"),
        name="post_mixer_even" if even else "post_mixer_odd",
    )(x, *acts, modtab, *weights, gffn, rwt, rb)


EXPERT_ROW_TILE = 512
DMA_ISSUE_UNROLL = 8


def _moe_plan(sel, counts, n_tok):
    e_a, e_b, r_a, r_b = (sel[:, :, k, :].reshape(-1) for k in range(4))
    cnt = counts[:, 0].astype(jnp.int32)
    padded = ((cnt + EXPERT_ROW_TILE - 1) // EXPERT_ROW_TILE) * EXPERT_ROW_TILE
    ends = jnp.cumsum(padded)
    starts = ends - padded
    pos = jnp.concatenate([starts[e_a] + r_a, starts[e_b] + r_b]).astype(jnp.int32)
    n_tiles = 2 * n_tok // EXPERT_ROW_TILE + N_EXPERTS
    tile_start = jnp.arange(n_tiles, dtype=jnp.int32) * EXPERT_ROW_TILE
    tile_expert = jnp.minimum(jnp.sum(tile_start[:, None] >= ends[None, :], axis=1),
                              N_EXPERTS - 1).astype(jnp.int32)
    tiles_used = (ends[-1:] // EXPERT_ROW_TILE).astype(jnp.int32)
    return pos, tile_expert, tiles_used, n_tiles


def _store_row_tiles(ref, x):
    rows, d = x.shape
    n_sub = d // LANES
    for s in range(n_sub):
        ref[pl.ds(s, rows, stride=n_sub), :] = x[:, s * LANES:(s + 1) * LANES]


def _load_row_tiles(ref, rows):
    n_sub = ref.shape[0] // rows
    return jnp.concatenate([ref[pl.ds(s, rows, stride=n_sub), :] for s in range(n_sub)], axis=1)


def _dispatch_kernel(pos_ref, f_ref, init_ref, out_ref, sem, *, n_tok, tm):
    del init_ref
    n_sub = f_ref.shape[1] // tm
    base = (pl.program_id(0) * pl.num_programs(1) + pl.program_id(1)) * tm

    def row_copy(r, k):
        dst = pos_ref[k * n_tok + base + r]
        src = f_ref.at[0, pl.ds(pl.multiple_of(r * n_sub, n_sub), n_sub), :]
        return pltpu.make_async_copy(src, out_ref.at[dst], sem)

    def issue(r, carry):
        row_copy(r, 0).start()
        row_copy(r, 1).start(priority=1)
        return carry

    lax.fori_loop(0, tm, issue, 0, unroll=DMA_ISSUE_UNROLL)
    for _ in range(2):
        pltpu.make_async_copy(out_ref.at[pl.ds(0, tm)], out_ref.at[pl.ds(0, tm)], sem).wait()


def _dispatch(pos, f, n_rows, n_tok):
    b, rows, _ = f.shape
    n_sub = rows * b // n_tok
    tm = ROW_TILE
    grid_spec = pltpu.PrefetchScalarGridSpec(
        num_scalar_prefetch=1,
        grid=(b, n_tok // b // tm),
        in_specs=[pl.BlockSpec((1, tm * n_sub, LANES), lambda bi, i, pos_ref: (bi, i, 0)),
                  pl.BlockSpec(memory_space=pl.ANY)],
        out_specs=pl.BlockSpec(memory_space=pl.ANY),
        scratch_shapes=[pltpu.SemaphoreType.DMA(())])
    return pl.pallas_call(
        functools.partial(_dispatch_kernel, n_tok=n_tok, tm=tm),
        grid_spec=grid_spec,
        out_shape=jax.ShapeDtypeStruct((n_rows, n_sub, LANES), F32),
        input_output_aliases={2: 0},
        compiler_params=_params("arbitrary", "arbitrary"),
        name="moe_dispatch",
    )(pos, f, jnp.zeros((n_rows, n_sub, LANES), F32))


def _expert_ffn_kernel(te_ref, used_ref, x_ref, wg_ref, wu_ref, wd_ref, y_ref, wg_scr, wu_scr, wd_scr):
    j = pl.program_id(0)
    live = j < used_ref[0]

    @pl.when(jnp.logical_or(j == 0, te_ref[j] != te_ref[jnp.maximum(j - 1, 0)]))
    def _():
        wg_scr[...] = wg_ref[0, 0].astype(BF16)
        wu_scr[...] = wu_ref[0, 0].astype(BF16)
        wd_scr[...] = wd_ref[0, 0].astype(BF16)

    @pl.when(live)
    def _():
        xb = _load_row_tiles(x_ref, EXPERT_ROW_TILE).astype(BF16)
        he = _silu(_dot(xb, wg_scr[...])) * _dot(xb, wu_scr[...])
        _store_row_tiles(y_ref, _dot(he.astype(BF16), wd_scr[...]))

    @pl.when(jnp.logical_not(live))
    def _():
        y_ref[...] = jnp.zeros(y_ref.shape, F32)


def _expert_ffn(tile_expert, tiles_used, xs, wg, wu, wd, layer, n_tiles):
    n_rows, n_sub, _ = xs.shape
    _, _, d, d_e = wg.shape
    tr = EXPERT_ROW_TILE
    grid_spec = pltpu.PrefetchScalarGridSpec(
        num_scalar_prefetch=2,
        grid=(n_tiles,),
        in_specs=[pl.BlockSpec((tr * n_sub, LANES), lambda j, te, used: (j, 0)),
                  pl.BlockSpec((1, 1, d, d_e), lambda j, te, used: (layer, te[j], 0, 0)),
                  pl.BlockSpec((1, 1, d, d_e), lambda j, te, used: (layer, te[j], 0, 0)),
                  pl.BlockSpec((1, 1, d_e, d), lambda j, te, used: (layer, te[j], 0, 0))],
        out_specs=pl.BlockSpec((tr * n_sub, LANES), lambda j, te, used: (j, 0)),
        scratch_shapes=[pltpu.VMEM((d, d_e), BF16), pltpu.VMEM((d, d_e), BF16),
                        pltpu.VMEM((d_e, d), BF16)])
    ys = pl.pallas_call(
        _expert_ffn_kernel,
        grid_spec=grid_spec,
        out_shape=jax.ShapeDtypeStruct((n_rows * n_sub, LANES), F32),
        compiler_params=_params("arbitrary"),
        name="moe_expert_ffn",
    )(tile_expert, tiles_used, xs.reshape(n_rows * n_sub, LANES), wg, wu, wd)
    return ys.reshape(n_rows, n_sub, LANES)


def _combine_kernel(pos_ref, y_ref, w_ref, x_ref, modc_ref, modl_ref, fg_ref, o_ref, buf, sem,
                    *, n_tok, n_ctx, final_norm):
    i = pl.program_id(1)
    tm = x_ref.shape[1]
    step = pl.program_id(0) * pl.num_programs(1) + i
    n_steps = pl.num_programs(0) * pl.num_programs(1)
    n_sub = y_ref.shape[1]

    def gather(tile, slot):
        base = tile * tm

        def issue(r, carry):
            for k in range(2):
                src = pos_ref[k * n_tok + base + r]
                dst = buf.at[slot, k, pl.ds(pl.multiple_of(r * n_sub, n_sub), n_sub), :]
                pltpu.make_async_copy(y_ref.at[src], dst, sem.at[slot]).start(priority=k)
            return carry

        lax.fori_loop(0, tm, issue, 0, unroll=DMA_ISSUE_UNROLL)

    slot = lax.rem(step, 2)

    @pl.when(step == 0)
    def _():
        gather(0, 0)

    @pl.when(step + 1 < n_steps)
    def _():
        gather(step + 1, 1 - slot)

    for _ in range(2):
        pltpu.make_async_copy(y_ref.at[pl.ds(0, tm)], y_ref.at[pl.ds(0, tm)], sem.at[slot]).wait()
    w = w_ref[0]
    y = (_load_row_tiles(buf.at[slot, 0], tm) * w[:, 0:1]
         + _load_row_tiles(buf.at[slot, 1], tm) * w[:, 1:2])
    rows = i * tm + lax.broadcasted_iota(jnp.int32, (tm, 1), 0)
    gate = jnp.where(rows < n_ctx, modc_ref[0, 5:6, :], modl_ref[0, 5:6, :])
    out = x_ref[0] + gate * y
    if final_norm:
        out = out * lax.rsqrt(jnp.mean(out * out, axis=-1, keepdims=True) + NORM_EPS) * fg_ref[...]
    o_ref[0] = out


def _combine(pos, ys, w_col, xmid, modtab, final_g, n_ctx, final_norm):
    b, t, d = xmid.shape
    tm = ROW_TILE
    row = lambda w: pl.BlockSpec((1, tm, w), lambda bi, i, pos_ref: (bi, i, 0))
    grid_spec = pltpu.PrefetchScalarGridSpec(
        num_scalar_prefetch=1,
        grid=(b, t // tm),
        in_specs=[pl.BlockSpec(memory_space=pl.ANY), row(LANES), row(d),
                  pl.BlockSpec((1, 8, d), lambda bi, i, pos_ref: (2 * bi, 0, 0)),
                  pl.BlockSpec((1, 8, d), lambda bi, i, pos_ref: (2 * bi + 1, 0, 0)),
                  pl.BlockSpec(final_g.shape, lambda bi, i, pos_ref: (0, 0))],
        out_specs=row(d),
        scratch_shapes=[pltpu.VMEM((2, 2, tm * d // LANES, LANES), F32), pltpu.SemaphoreType.DMA((2,))])
    return pl.pallas_call(
        functools.partial(_combine_kernel, n_tok=b * t, n_ctx=n_ctx, final_norm=final_norm),
        grid_spec=grid_spec,
        out_shape=jax.ShapeDtypeStruct((b, t, d), F32),
        compiler_params=_params("arbitrary", "arbitrary"),
        name="moe_combine",
    )(pos, ys, w_col, xmid, modtab, modtab, final_g)


def _moe(f, sel, w_col, counts, wg, wu, wd, layer, xmid, modtab, final_g, n_ctx, final_norm):
    b, t, _ = xmid.shape
    pos, tile_expert, tiles_used, n_tiles = _moe_plan(sel, counts, b * t)
    xs = _dispatch(pos, f, n_tiles * EXPERT_ROW_TILE, b * t)
    ys = _expert_ffn(tile_expert, tiles_used, xs, wg, wu, wd, layer, n_tiles)
    return _combine(pos, ys, w_col, xmid, modtab, final_g, n_ctx, final_norm)


def _chan_dft_kernel(x_ref, mod_ref, g_ref, w_ref, zr_ref, zi_ref):
    h = _norm_mod(x_ref[0], g_ref[...], mod_ref[0, 0:1, :], mod_ref[0, 1:2, :])
    gd = w_ref.shape[0]
    for gi in range(h.shape[1] // gd):
        z = _dot_split(h[:, gi * gd:(gi + 1) * gd], w_ref[...])
        zr_ref[0, :, gi * gd:(gi + 1) * gd] = z[:, :gd]
        zi_ref[0, :, gi * gd:(gi + 1) * gd] = z[:, gd:]


def _chan_dft(x, modtab, g, w_cs, x_row_off, t):
    b, _, d = x.shape
    tm = ROW_TILE
    off = x_row_off // tm
    row = pl.BlockSpec((1, tm, d), lambda bi, i: (bi, i, 0))
    return pl.pallas_call(
        _chan_dft_kernel,
        grid=(b, t // tm),
        in_specs=[pl.BlockSpec((1, tm, d), lambda bi, i: (bi, i + off, 0)),
                  pl.BlockSpec((1, 8, d), lambda bi, i: (2 * bi + 1, 0, 0)),
                  pl.BlockSpec(g.shape, lambda bi, i: (0, 0)),
                  pl.BlockSpec(w_cs.shape, lambda bi, i: (0, 0))],
        out_specs=[row, row],
        out_shape=[jax.ShapeDtypeStruct((b, t, d), F32)] * 2,
        compiler_params=_params("parallel", "parallel"),
        name="chan_dft",
    )(x, modtab, g, w_cs)


DFT_SUB = 8


def _dft1_kernel(zr_ref, zi_ref, w_ref, yr_ref, yi_ref):
    _, n1, sub, cols = zr_ref.shape
    z = jnp.concatenate([jnp.concatenate([ref[0, :, j, :] for j in range(sub)], axis=1)
                         for ref in (zr_ref, zi_ref)], axis=0)
    y = _dot_split(w_ref[...], z)
    for j in range(sub):
        yr_ref[0, :, j, :] = y[:n1, j * cols:(j + 1) * cols]
        yi_ref[0, :, j, :] = y[n1:, j * cols:(j + 1) * cols]


def _dft1(zr, zi, w1):
    b, n1, n2, d = zr.shape
    cols = d // 2
    blk = pl.BlockSpec((1, n1, DFT_SUB, cols), lambda bi, j, c: (bi, 0, j, c))
    return pl.pallas_call(
        _dft1_kernel,
        grid=(b, n2 // DFT_SUB, d // cols),
        in_specs=[blk, blk, pl.BlockSpec(w1.shape, lambda bi, j, c: (0, 0))],
        out_specs=[blk, blk],
        out_shape=[jax.ShapeDtypeStruct(zr.shape, F32)] * 2,
        compiler_params=_params("parallel", "parallel", "parallel"),
        name="dft_stage1",
    )(zr, zi, w1)


def _dft2_kernel(yr_ref, yi_ref, tab_ref, o_ref):
    for j in range(yr_ref.shape[1]):
        y = jnp.concatenate([yr_ref[0, j], yi_ref[0, j]], axis=0)
        o_ref[0, :, j, :] = _dot_split(tab_ref[j], y)


def _dft2(yr, yi, tab):
    b, n1, n2, d = yr.shape
    blk = pl.BlockSpec((1, DFT_SUB, n2, d), lambda bi, k1: (bi, k1, 0, 0))
    return pl.pallas_call(
        _dft2_kernel,
        grid=(b, n1 // DFT_SUB),
        in_specs=[blk, blk, pl.BlockSpec((DFT_SUB, n2, 2 * n2), lambda bi, k1: (k1, 0, 0))],
        out_specs=pl.BlockSpec((1, n2, DFT_SUB, d), lambda bi, k1: (bi, 0, k1, 0)),
        out_shape=jax.ShapeDtypeStruct((b, n2, n1, d), F32),
        compiler_params=_params("parallel", "parallel"),
        name="dft_stage2",
    )(yr, yi, tab)


def _dft_tables(t, gd):
    n2 = ML_CHUNK
    n1 = t // n2
    def cs(num, den):
        ang = (2.0 * np.pi / den) * (num % den).astype(np.float64)
        return np.cos(ang), np.sin(ang)
    c = np.arange(gd)
    cc, sc = cs(np.outer(c, c), gd)
    w_cs = np.concatenate([cc, -sc], axis=1)
    a = np.arange(n1)
    c1, s1 = cs(np.outer(a, a), n1)
    w1 = np.block([[c1, s1], [-s1, c1]])
    k = a[:, None, None] + n1 * np.arange(n2)[None, :, None]
    c2, s2 = cs(k * np.arange(n2)[None, None, :], t)
    tab = np.concatenate([c2, s2], axis=2) / math.sqrt(t * gd)
    return (jnp.asarray(w_cs, F32), jnp.asarray(w1, F32), jnp.asarray(tab, F32))


def _rope_tables(n_ctx, n_lat):
    pos = jnp.arange(n_lat, dtype=jnp.int32)
    n_axis = DA_QK_DIM // 4
    inv = ROPE_BASE ** (-jnp.arange(n_axis, dtype=F32) / n_axis)
    ang = jnp.concatenate([(pos // GRID_W).astype(F32)[:, None] * inv,
                           (pos % GRID_W).astype(F32)[:, None] * inv], axis=-1)
    cos, sin = jnp.cos(ang), jnp.sin(ang)
    cos = jnp.concatenate([jnp.ones((n_ctx, 2 * n_axis), F32), cos], axis=0)
    sin = jnp.concatenate([jnp.zeros((n_ctx, 2 * n_axis), F32), sin], axis=0)
    cos128 = jnp.concatenate([cos, cos, cos, cos], axis=1)
    sin128 = jnp.concatenate([-sin, sin, -sin, sin], axis=1)
    return cos128, sin128


def _deinterleave(w):
    d, n = w.shape
    w = w.reshape(d, n // DA_QK_DIM, DA_QK_DIM // 2, 2)
    return jnp.concatenate([w[..., 0], w[..., 1]], axis=-1).reshape(d, n)


def _pad_rows(a, rows):
    return jnp.concatenate([a, jnp.zeros((rows - a.shape[0],) + a.shape[1:], a.dtype)], axis=0)


def _pad_cols(a, cols):
    return jnp.concatenate([a, jnp.zeros(a.shape[:-1] + (cols - a.shape[-1],), a.dtype)], axis=-1)


def kernel(x, c, ctx, c_ctx, ada_w, ada_b, norm_mix_g, norm_ffn_g, even_w_in, even_w_out,
           even_conv_w, even_gate_b, even_lam, even_subln_g, odd_w_fnet, router_w, router_b,
           exp_w_gate, exp_w_up, exp_w_down, final_g):
    b, n_lat, d = x.shape
    n_ctx = ctx.shape[1]
    depth = ada_w.shape[0]
    assert depth == 2 and b + 1 <= 8
    assert n_ctx % ROW_TILE == 0 and n_lat % ROW_TILE == 0

    cond8 = _pad_rows(jnp.concatenate([c_ctx[None, :], c], axis=0), 8)
    rwt = router_w.T
    rb = jnp.broadcast_to(router_b[:, None], (N_EXPERTS, LANES))
    row2 = lambda v: v.reshape(1, -1)

    def modtab_for(layer):
        mods = _ada_mods(cond8, ada_w[layer], ada_b[layer]).reshape(8, 6, d)
        mods = jnp.concatenate([mods, jnp.zeros((8, 2, d), F32)], axis=1)
        idx = np.array([[0, 1 + bi] for bi in range(b)]).reshape(-1)
        return mods[idx]

    xs = jnp.concatenate([ctx, x], axis=1)
    modtab = modtab_for(0)
    w_in = even_w_in[0]
    o1 = DA_HEADS * 2 * DA_QK_DIM
    o2 = 2 * o1
    o3 = o2 + DA_HEADS * DA_V_DIM
    o4 = o3 + 2 * ML_HEADS * ML_DIM
    o5 = o4 + ML_HEADS * ML_DIM
    o6 = o5 + ML_HEADS * ML_DIM
    ws = [(_deinterleave(w_in[:, :o1]) * (DA_QK_DIM ** -0.5)).astype(BF16),
          _deinterleave(w_in[:, o1:o2]).astype(BF16),
          w_in[:, o2:o3].astype(BF16), w_in[:, o3:o4].astype(BF16),
          w_in[:, o4:o5].astype(BF16), w_in[:, o5:o6].astype(BF16),
          _pad_cols(w_in[:, o6:], LANES).astype(BF16)]
    gate_b = _pad_cols(even_gate_b[0].reshape(1, -1), LANES)
    cos128, sin128 = _rope_tables(n_ctx, n_lat)
    daq, dak, dav, mqk, mv, og, gates = _inproj(xs, modtab, row2(norm_mix_g[0]), cos128, sin128,
                                                 ws, gate_b, n_ctx)
    mq, mk, gc, gr, mvt = _mlprep(mqk, _pad_rows(even_conv_w[0], 8), gates, mv, n_ctx)
    hf, hb = _mlstm(mq, mk, mvt, gc, gr, n_ctx)
    lam_init = 0.8 - 0.6 * math.exp(-0.3 * 0)
    lam8 = _pad_rows(even_lam[0], 8)
    att = _diff_attention(daq, dak, dav, lam8, row2(even_subln_g[0]), n_ctx, lam_init)
    w_out = even_w_out[0].astype(BF16)
    half = DA_HEADS * DA_V_DIM
    xmid, f, sel, w_col, counts = _post_mixer(xs, [att, hf, hb, og], [w_out[:half], w_out[half:]],
                                              modtab, row2(norm_ffn_g[0]), rwt, rb, n_ctx, 0, True)
    xs = _moe(f, sel, w_col, counts, exp_w_gate, exp_w_up, exp_w_down, 0,
              xmid, modtab, row2(final_g), n_ctx, False)

    modtab = modtab_for(1)
    gd = d // FN_GROUPS
    w_cs, w1, tab = _dft_tables(n_lat, gd)
    n2 = ML_CHUNK
    n1 = n_lat // n2
    zr, zi = _chan_dft(xs, modtab, row2(norm_mix_g[1]), w_cs, n_ctx, n_lat)
    yr, yi = _dft1(zr.reshape(b, n1, n2, d), zi.reshape(b, n1, n2, d), w1)
    fo = _dft2(yr, yi, tab).reshape(b, n_lat, d)
    xmid, f, sel, w_col, counts = _post_mixer(xs, [fo], [odd_w_fnet[0].astype(BF16)], modtab,
                                              row2(norm_ffn_g[1]), rwt, rb, 0, n_ctx, False)
    return _moe(f, sel, w_col, counts, exp_w_gate, exp_w_up, exp_w_down, 1,
                xmid, modtab, row2(final_g), 0, True)
```

```python
import functools
import math

import jax
import jax.numpy as jnp
import numpy as np
from jax import lax
from jax.experimental import pallas as pl
from jax.experimental.pallas import tpu as pltpu

F32 = jnp.float32
BF16 = jnp.bfloat16

NORM_EPS = 1e-6
GRID_W = 64
DA_HEADS = 4
DA_QK_DIM = 64
DA_V_DIM = 128
ML_HEADS = 4
ML_DIM = 128
ML_CHUNK = 128
FN_GROUPS = 4
N_EXPERTS = 16
EXPERTS_PER_GROUP = 4
ROPE_BASE = 10000.0
LANES = 128
ROW_TILE = 256
VMEM_LIMIT_BYTES = 56 * 1024 * 1024
HI = lax.Precision.HIGHEST
LOG2_E = math.log2(math.e)


def _params(*sem):
    return pltpu.CompilerParams(dimension_semantics=sem, vmem_limit_bytes=VMEM_LIMIT_BYTES)


def _dot(a, b, precision=None):
    return jnp.dot(a, b, preferred_element_type=F32, precision=precision)


def _dot_nt(a, b, precision=None):
    return lax.dot_general(a, b, (((1,), (1,)), ((), ())), preferred_element_type=F32,
                           precision=precision)


def _dot_split(a, b):
    a_hi = a.astype(BF16)
    b_hi = b.astype(BF16)
    a_lo = (a - a_hi.astype(F32)).astype(BF16)
    b_lo = (b - b_hi.astype(F32)).astype(BF16)
    return _dot(a_hi, b_hi) + (_dot(a_hi, b_lo) + _dot(a_lo, b_hi))


def _sigmoid(x):
    return 1.0 / (1.0 + jnp.exp(-x))


def _silu(x):
    return x * _sigmoid(x)


def _norm_mod(x, g, shift, scale):
    y = x * lax.rsqrt(jnp.mean(x * x, axis=-1, keepdims=True) + NORM_EPS) * g
    return y * (1.0 + scale) + shift


def _ada_kernel(c_ref, w_ref, b_ref, o_ref):
    o_ref[...] = _dot(_silu(c_ref[...]), w_ref[...], HI) + b_ref[...]


def _ada_mods(cond8, w, b):
    d, n = w.shape
    tn = n // 6
    return pl.pallas_call(
        _ada_kernel,
        grid=(6,),
        in_specs=[pl.BlockSpec((8, d), lambda j: (0, 0)),
                  pl.BlockSpec((d, tn), lambda j: (0, j)),
                  pl.BlockSpec((1, tn), lambda j: (0, j))],
        out_specs=pl.BlockSpec((8, tn), lambda j: (0, j)),
        out_shape=jax.ShapeDtypeStruct((8, n), F32),
        compiler_params=_params("arbitrary"),
        name="ada_mods",
    )(cond8, w, b.reshape(1, n))


def _inproj_kernel(x_ref, mod_ref, g_ref, cos_ref, sin_ref, wq_ref, wk_ref, wv_ref, wmqk_ref,
                   wmv_ref, wmo_ref, wg_ref, gb_ref,
                   q_ref, k_ref, v_ref, mqk_ref, mv_ref, og_ref, gate_ref):
    x = x_ref[0]
    h = _norm_mod(x, g_ref[...], mod_ref[0, 0:1, :], mod_ref[0, 1:2, :]).astype(BF16)
    tm = x.shape[0]
    width = q_ref.shape[2]
    cos = jnp.concatenate([cos_ref[...]] * (width // LANES), axis=1)
    sin = jnp.concatenate([sin_ref[...]] * (width // LANES), axis=1)
    lane = lax.broadcasted_iota(jnp.int32, (tm, width), 1)
    lower = (lane & (DA_QK_DIM - 1)) < (DA_QK_DIM // 2)

    def rope(u):
        swapped = jnp.where(lower, pltpu.roll(u, width - DA_QK_DIM // 2, 1),
                            pltpu.roll(u, DA_QK_DIM // 2, 1))
        return u * cos + swapped * sin

    q_ref[0] = (rope(_dot(h, wq_ref[...])) * LOG2_E).astype(BF16)
    k_ref[0] = rope(_dot(h, wk_ref[...])).astype(BF16)
    v = _dot(h, wv_ref[...])
    pad = VT_ROWS - DA_V_DIM
    ones_row = (lax.broadcasted_iota(jnp.int32, (pad, tm), 0) == 0).astype(BF16)
    for hd in range(DA_HEADS):
        v_ref[0, hd, 0:DA_V_DIM, :] = v[:, hd * DA_V_DIM:(hd + 1) * DA_V_DIM].T.astype(BF16)
        v_ref[0, hd, DA_V_DIM:VT_ROWS, :] = ones_row
    mqk_ref[0] = _dot(h, wmqk_ref[...]).astype(BF16)
    mv_ref[0] = _dot(h, wmv_ref[...]).astype(BF16)
    og_ref[0] = _sigmoid(_dot(h, wmo_ref[...])).astype(BF16)
    g = _dot(h, wg_ref[...]) + gb_ref[...]
    glane = lax.broadcasted_iota(jnp.int32, g.shape, 1)
    is_forget = ((glane // ML_HEADS) & 1) == 1
    log_sig = jnp.minimum(g, 0.0) - jnp.log(1.0 + jnp.exp(-jnp.abs(g)))
    gate_ref[0] = jnp.where(is_forget, log_sig, g)


def _inproj(xs, modtab, g, cos, sin, ws, gate_b, n_ctx):
    b, t, d = xs.shape
    tm = ROW_TILE
    nt = t // tm
    ctx_tiles = n_ctx // tm
    row = lambda w: pl.BlockSpec((1, tm, w), lambda bi, i: (bi, i, 0))
    full = lambda a: pl.BlockSpec(a.shape, lambda bi, i: (0,) * a.ndim)
    widths = [w.shape[1] for w in ws]
    out_dtypes = [BF16] * 6 + [F32]
    out_specs = [row(w) for w in widths]
    out_shape = [jax.ShapeDtypeStruct((b, t, w), dt) for w, dt in zip(widths, out_dtypes)]
    out_specs[2] = pl.BlockSpec((1, DA_HEADS, VT_ROWS, tm), lambda bi, i: (bi, 0, 0, i))
    out_shape[2] = jax.ShapeDtypeStruct((b, DA_HEADS, VT_ROWS, t), BF16)
    return pl.pallas_call(
        _inproj_kernel,
        grid=(b, nt),
        in_specs=[row(d),
                  pl.BlockSpec((1, 8, d), lambda bi, i: (2 * bi + (i >= ctx_tiles).astype(jnp.int32), 0, 0)),
                  full(g),
                  pl.BlockSpec((tm, LANES), lambda bi, i: (i, 0)),
                  pl.BlockSpec((tm, LANES), lambda bi, i: (i, 0))]
                 + [full(w) for w in ws] + [full(gate_b)],
        out_specs=out_specs,
        out_shape=out_shape,
        compiler_params=_params("parallel", "parallel"),
        name="inproj",
    )(xs, modtab, g, cos, sin, *ws, gate_b)


def _split3(x):
    x1 = x.astype(BF16)
    r1 = x - x1.astype(F32)
    x2 = r1.astype(BF16)
    x3 = (r1 - x2.astype(F32)).astype(BF16)
    return x1, x2, x3


VT_ROWS = ML_DIM + 16


def _mlprep_kernel(cur_ref, prev_ref, next_ref, cw_ref, gate_ref, v_ref,
                   mq_ref, mk_ref, gc_ref, gr_ref, vt_ref, *, ctx_tiles, n_tiles):
    i = pl.program_id(1)
    cur = cur_ref[0].astype(F32)
    tm, w = cur.shape
    prev_ok = i != ctx_tiles
    if ctx_tiles > 0:
        prev_ok = jnp.logical_and(prev_ok, i != 0)
        next_ok = jnp.logical_and(i != ctx_tiles - 1, i != n_tiles - 1)
    else:
        next_ok = i != n_tiles - 1
    prev_row = jnp.where(prev_ok, prev_ref[0, 7:8, :].astype(F32), 0.0)
    next_row = jnp.where(next_ok, next_ref[0, 0:1, :].astype(F32), 0.0)
    ridx = lax.broadcasted_iota(jnp.int32, (tm, w), 0)
    before = jnp.where(ridx == 0, prev_row, pltpu.roll(cur, 1, 0))
    after = jnp.where(ridx == tm - 1, next_row, pltpu.roll(cur, tm - 1, 0))
    y = _silu(before * cw_ref[0:1, :] + cur * cw_ref[1:2, :] + after * cw_ref[2:3, :])
    half = w // 2
    mq_ref[0] = y[:, :half].astype(BF16)
    mk_ref[0] = (y[:, half:] * (ML_DIM ** -0.5)).astype(BF16)

    n_g = 4 * ML_HEADS
    hds = ML_HEADS
    r = lax.broadcasted_iota(jnp.int32, (ML_CHUNK, ML_CHUNK), 0)
    c = lax.broadcasted_iota(jnp.int32, (ML_CHUNK, ML_CHUNK), 1)
    lower = (c <= r).astype(BF16)
    upper = (c >= r).astype(BF16)
    ones_row = (r == 0).astype(BF16)[0:VT_ROWS - ML_DIM]
    for ci in range(tm // ML_CHUNK):
        rows = slice(ci * ML_CHUNK, (ci + 1) * ML_CHUNK)
        gm = jnp.where(c < n_g, gate_ref[0, rows, :], 0.0)
        pre = sum(_dot(lower, p) for p in _split3(pltpu.roll(gm, n_g, 1)))
        suf = sum(_dot(upper, p) for p in _split3(pltpu.roll(gm, 2 * n_g, 1)))
        col = gm + pre + suf
        u_f = pltpu.roll(col, 3 * n_g, 1) - pltpu.roll(col, 3 * n_g - (n_g + hds), 1)
        u_b = pltpu.roll(col, 3 * n_g - hds, 1) - pltpu.roll(col, 2 * hds, 1)
        col = col + jnp.where(jnp.logical_and(c >= 3 * n_g, c < 3 * n_g + hds), u_f,
                              jnp.where(jnp.logical_and(c >= 3 * n_g + hds, c < 3 * n_g + 2 * hds),
                                        u_b, 0.0))
        gc_ref[0, rows, :] = col
        gr_ref[0, ci] = col.T
        for hd in range(ML_HEADS):
            v_t = v_ref[0, rows, hd * ML_DIM:(hd + 1) * ML_DIM].astype(F32).T.astype(BF16)
            vt_ref[0, ci, hd * VT_ROWS:hd * VT_ROWS + ML_DIM, :] = v_t
            vt_ref[0, ci, hd * VT_ROWS + ML_DIM:(hd + 1) * VT_ROWS, :] = ones_row


def _mlprep(mqk, conv_w8, gates, mv, n_ctx):
    b, t, w = mqk.shape
    tm = ROW_TILE
    nt = t // tm
    sub = tm // 8
    nsub = t // 8
    cpt = tm // ML_CHUNK
    kern = functools.partial(_mlprep_kernel, ctx_tiles=n_ctx // tm, n_tiles=nt)
    return pl.pallas_call(
        kern,
        grid=(b, nt),
        in_specs=[pl.BlockSpec((1, tm, w), lambda bi, i: (bi, i, 0)),
                  pl.BlockSpec((1, 8, w), lambda bi, i: (bi, jnp.maximum(i * sub - 1, 0), 0)),
                  pl.BlockSpec((1, 8, w), lambda bi, i: (bi, jnp.minimum((i + 1) * sub, nsub - 1), 0)),
                  pl.BlockSpec(conv_w8.shape, lambda bi, i: (0, 0)),
                  pl.BlockSpec((1, tm, LANES), lambda bi, i: (bi, i, 0)),
                  pl.BlockSpec((1, tm, w // 2), lambda bi, i: (bi, i, 0))],
        out_specs=[pl.BlockSpec((1, tm, w // 2), lambda bi, i: (bi, i, 0)),
                   pl.BlockSpec((1, tm, w // 2), lambda bi, i: (bi, i, 0)),
                   pl.BlockSpec((1, tm, LANES), lambda bi, i: (bi, i, 0)),
                   pl.BlockSpec((1, cpt, ML_CHUNK, LANES), lambda bi, i: (bi, i, 0, 0)),
                   pl.BlockSpec((1, cpt, ML_HEADS * VT_ROWS, ML_CHUNK), lambda bi, i: (bi, i, 0, 0))],
        out_shape=[jax.ShapeDtypeStruct((b, t, w // 2), BF16),
                   jax.ShapeDtypeStruct((b, t, w // 2), BF16),
                   jax.ShapeDtypeStruct((b, t, LANES), F32),
                   jax.ShapeDtypeStruct((b, t // ML_CHUNK, ML_CHUNK, LANES), F32),
                   jax.ShapeDtypeStruct((b, t // ML_CHUNK, ML_HEADS * VT_ROWS, ML_CHUNK), BF16)],
        compiler_params=_params("parallel", "parallel"),
        name="mlstm_prep",
    )(mqk, mqk, mqk, conv_w8, gates, mv)


def _mlstm_kernel(qf_ref, kf_ref, vf_ref, gcf_ref, grf_ref, qb_ref, kb_ref, vb_ref, gcb_ref, grb_ref,
                  hf_ref, hb_ref, *scr):
    s = pl.program_id(0)
    nb = qf_ref.shape[0]
    n_g = 4 * ML_HEADS
    state_refs = scr[:len(scr) // 2]
    m_refs = scr[len(scr) // 2:]

    @pl.when(s == 0)
    def _():
        for ref in scr:
            ref[...] = jnp.zeros(ref.shape, F32)

    ki = lax.broadcasted_iota(jnp.int32, (ML_CHUNK, ML_CHUNK), 0)
    qi = lax.broadcasted_iota(jnp.int32, (ML_CHUNK, ML_CHUNK), 1)

    for bi in range(nb):
        for direction in range(2):
            q_ref, k_ref, vt_ref, gc_ref, gr_ref, h_ref = (
                (qf_ref, kf_ref, vf_ref, gcf_ref, grf_ref, hf_ref) if direction == 0 else
                (qb_ref, kb_ref, vb_ref, gcb_ref, grb_ref, hb_ref))
            visible = (ki <= qi) if direction == 0 else (ki >= qi)
            gc = gc_ref[bi]
            gr = gr_ref[bi, 0]
            for hd in range(ML_HEADS):
                chain = (bi * 2 + direction) * ML_HEADS + hd
                c_lf = (2 * direction + 1) * ML_HEADS + hd
                c_cs = c_lf + (n_g if direction == 0 else 2 * n_g)
                c_u = 3 * n_g + direction * ML_HEADS + hd
                lo, hi = hd * ML_DIM, (hd + 1) * ML_DIM
                q = q_ref[bi, :, lo:hi]
                k = k_ref[bi, :, lo:hi]
                v_t = vt_ref[bi, 0, hd * VT_ROWS:(hd + 1) * VT_ROWS, :]
                u_row = gr[c_u:c_u + 1, :]
                lf_row = gr[c_lf:c_lf + 1, :]
                cs_row = gr[c_cs:c_cs + 1, :]
                m_st = m_refs[chain][:, 0:1]
                st_t = state_refs[chain][...]

                u_vis = jnp.where(visible, jnp.broadcast_to(gc[:, c_u:c_u + 1], visible.shape), -jnp.inf)
                v_row = jnp.maximum(m_st, jnp.max(u_vis, axis=0, keepdims=True))
                d_t = jnp.exp(u_vis - v_row)
                inter = jnp.exp(m_st - v_row)
                sc_t = (_dot_nt(k, q) * d_t).astype(BF16)
                tot_t = inter * _dot_nt(st_t.astype(BF16), q) + _dot(v_t, sc_t)
                den = jnp.maximum(jnp.abs(tot_t[ML_DIM:ML_DIM + 1, :]), jnp.exp(-(cs_row + v_row)))
                h_ref[bi, :, lo:hi] = (tot_t[:ML_DIM, :] / den).T

                b_last = jnp.sum(lf_row, axis=1, keepdims=True)
                m_sc = jnp.maximum(m_st, jnp.max(u_row, axis=1, keepdims=True))
                wgt = jnp.exp(u_row - m_sc)
                vw_t = (v_t.astype(F32) * wgt).astype(BF16)
                state_refs[chain][...] = jnp.exp(m_st - m_sc) * st_t + _dot(vw_t, k)
                m_refs[chain][...] = jnp.broadcast_to(b_last + m_sc, (1, LANES))


def _mlstm(mq, mk, mvt, gc, gr, n_ctx):
    b, t, w = mq.shape
    nc = t // ML_CHUNK
    ncc = n_ctx // ML_CHUNK
    fwd = lambda s: s
    bwd = lambda s: jnp.where(s < ncc, ncc - 1 - s, nc - 1 - s + ncc)
    tok = lambda f: pl.BlockSpec((b, ML_CHUNK, w), lambda s: (0, f(s), 0))
    gcs = lambda f: pl.BlockSpec((b, ML_CHUNK, LANES), lambda s: (0, f(s), 0))
    grs = lambda f: pl.BlockSpec((b, 1, ML_CHUNK, LANES), lambda s: (0, f(s), 0, 0))
    vts = lambda f: pl.BlockSpec((b, 1, ML_HEADS * VT_ROWS, ML_CHUNK), lambda s: (0, f(s), 0, 0))
    n_chain = b * 2 * ML_HEADS
    return pl.pallas_call(
        _mlstm_kernel,
        grid=(nc,),
        in_specs=[tok(fwd), tok(fwd), vts(fwd), gcs(fwd), grs(fwd),
                  tok(bwd), tok(bwd), vts(bwd), gcs(bwd), grs(bwd)],
        out_specs=[tok(fwd), tok(bwd)],
        out_shape=[jax.ShapeDtypeStruct((b, t, w), F32)] * 2,
        scratch_shapes=[pltpu.VMEM((VT_ROWS, ML_DIM), F32)] * n_chain
                       + [pltpu.VMEM((1, LANES), F32)] * n_chain,
        compiler_params=_params("arbitrary"),
        name="mlstm",
    )(mq, mk, mvt, gc, gr, mq, mk, mvt, gc, gr)


def _attn_kernel(q_ref, qn_ref, k_ref, vt_ref, lam_ref, sg_ref, o_ref, qm_scr, *scr,
                 ctx_tiles, n_ctx, key_block, n_blocks, lam_init):
    s_scr = (scr[0:2], scr[2:4])
    bm_scr = (scr[4:6], scr[6:8])
    m_scr = scr[8:10]
    acc_scr = scr[10:12]
    i = pl.program_id(2)
    dv = DA_V_DIM
    q = q_ref[0]
    lane = lax.broadcasted_iota(jnp.int32, q.shape, 1)
    zero = jnp.zeros_like(q)

    def map_halves(qv):
        return jnp.where(lane < DA_QK_DIM, qv, zero), jnp.where(lane >= DA_QK_DIM, qv, zero)

    qm_scr[0], qm_scr[1] = map_halves(q)

    def scores(mp, start, size, q_maps=None):
        q_map = qm_scr[mp] if q_maps is None else q_maps[mp]
        return _dot_nt(k_ref[0, pl.ds(start, size), :], q_map)

    @pl.when(i < ctx_tiles)
    def _():
        for mp in range(2):
            sc = scores(mp, 0, n_ctx)
            m = jnp.max(sc, axis=0, keepdims=True)
            p = jnp.exp2(sc - m).astype(BF16)
            acc_scr[mp][...] = _dot(vt_ref[0, 0, :, 0:n_ctx], p)

    def stage(j, slot, q_maps=None):
        start = pl.multiple_of(j * key_block, LANES)
        for mp in range(2):
            sc = scores(mp, start, key_block, q_maps)
            s_scr[mp][slot][...] = sc
            bm_scr[mp][slot][...] = jnp.max(sc, axis=0, keepdims=True)

    def consume(j, slot):
        v_t = vt_ref[0, 0, :, pl.ds(pl.multiple_of(j * key_block, LANES), key_block)]
        for mp in range(2):
            m_old = m_scr[mp][...]
            m_new = jnp.maximum(m_old, bm_scr[mp][slot][...])
            alpha = jnp.exp2(m_old - m_new)
            p = jnp.exp2(s_scr[mp][slot][...] - m_new).astype(BF16)
            acc_scr[mp][...] = alpha * acc_scr[mp][...] + _dot(v_t, p)
            m_scr[mp][...] = m_new

    @pl.when(i >= ctx_tiles)
    def _():
        for mp in range(2):
            m_scr[mp][...] = jnp.full(m_scr[mp].shape, -jnp.inf, F32)
            acc_scr[mp][...] = jnp.zeros(acc_scr[mp].shape, F32)

        @pl.when(i == ctx_tiles)
        def _():
            stage(0, 0)

        def body(g, carry):
            stage(2 * g + 1, 1)
            consume(2 * g, 0)
            stage(2 * g + 2, 0)
            consume(2 * g + 1, 1)
            return carry
        lax.fori_loop(0, n_blocks // 2 - 1, body, 0)
        stage(n_blocks - 1, 1)
        consume(n_blocks - 2, 0)
        stage(0, 0, map_halves(qn_ref[0]))
        consume(n_blocks - 1, 1)

    lv = lam_ref[...]
    dot01 = jnp.sum(lv[0:1, :] * lv[1:2, :], axis=1, keepdims=True)
    dot23 = jnp.sum(lv[2:3, :] * lv[3:4, :], axis=1, keepdims=True)
    lam = jnp.exp(dot01) - jnp.exp(dot23) + lam_init
    a0 = acc_scr[0][...]
    a1 = acc_scr[1][...]
    o = (a0[0:dv, :] / a0[dv:dv + 1, :] - lam * (a1[0:dv, :] / a1[dv:dv + 1, :])).T
    o = o * lax.rsqrt(jnp.mean(o * o, axis=-1, keepdims=True) + NORM_EPS) * sg_ref[...]
    o_ref[0] = (o * (1.0 - lam_init)).astype(BF16)


def _diff_attention(q, k, vt, lam8, subln_g, n_ctx, lam_init):
    b, t, w = q.shape
    tq = ROW_TILE
    n_blocks = next(n for n in (10, 8, 6, 4, 2) if t % (n * LANES) == 0 and t // n >= n_ctx)
    key_block = t // n_blocks
    vt_rows = vt.shape[2]
    assert n_ctx % tq == 0
    kern = functools.partial(_attn_kernel, ctx_tiles=n_ctx // tq, n_ctx=n_ctx, key_block=key_block,
                             n_blocks=n_blocks, lam_init=lam_init)
    return pl.pallas_call(
        kern,
        grid=(b, DA_HEADS, t // tq),
        in_specs=[pl.BlockSpec((1, tq, LANES), lambda bi, h, i: (bi, i, h)),
                  pl.BlockSpec((1, tq, LANES), lambda bi, h, i: (bi, jnp.minimum(i + 1, t // tq - 1), h)),
                  pl.BlockSpec((1, t, LANES), lambda bi, h, i: (bi, 0, h)),
                  pl.BlockSpec((1, 1, vt_rows, t), lambda bi, h, i: (bi, h, 0, 0)),
                  pl.BlockSpec(lam8.shape, lambda bi, h, i: (0, 0)),
                  pl.BlockSpec(subln_g.shape, lambda bi, h, i: (0, 0))],
        out_specs=pl.BlockSpec((1, tq, LANES), lambda bi, h, i: (bi, i, h)),
        out_shape=jax.ShapeDtypeStruct((b, t, w), BF16),
        scratch_shapes=[pltpu.VMEM((2, tq, LANES), BF16)]
                       + [pltpu.VMEM((key_block, tq), F32)] * 4
                       + [pltpu.VMEM((1, tq), F32)] * 6
                       + [pltpu.VMEM((vt_rows, tq), F32)] * 2,
        compiler_params=_params("arbitrary", "arbitrary", "arbitrary"),
        name="diff_attention",
    )(q, q, k, vt, lam8, subln_g)


def _top2_sum(a, b, c, d):
    hi1, lo1 = jnp.maximum(a, b), jnp.minimum(a, b)
    hi2, lo2 = jnp.maximum(c, d), jnp.minimum(c, d)
    return jnp.maximum(hi1, hi2) + jnp.maximum(jnp.minimum(hi1, hi2), jnp.maximum(lo1, lo2))


def _route(f, rwt_ref, rb_ref, cnt_ref):
    tm = f.shape[0]
    aff = _sigmoid(_dot_nt(rwt_ref[...], f, HI))
    biased = aff + rb_ref[:, 0:1]
    bz = [biased[e:e + 1, :] for e in range(N_EXPERTS)]
    af = [aff[e:e + 1, :] for e in range(N_EXPERTS)]
    n_grp = N_EXPERTS // EXPERTS_PER_GROUP
    scores = [_top2_sum(*bz[EXPERTS_PER_GROUP * g:EXPERTS_PER_GROUP * (g + 1)]) for g in range(n_grp)]
    best = scores[0]
    sel_grp = jnp.zeros_like(best, dtype=jnp.int32)
    for g in range(1, n_grp):
        better = scores[g] > best
        sel_grp = jnp.where(better, g, sel_grp)
        best = jnp.where(better, scores[g], best)
    chosen = []
    for e in range(N_EXPERTS):
        g = e // EXPERTS_PER_GROUP
        rank = jnp.zeros_like(sel_grp)
        for o in range(EXPERTS_PER_GROUP * g, EXPERTS_PER_GROUP * (g + 1)):
            if o == e:
                continue
            beats = (bz[o] > bz[e]) if o > e else (bz[o] >= bz[e])
            rank = rank + beats.astype(jnp.int32)
        chosen.append(jnp.logical_and(sel_grp == g, rank < 2))
    denom = sum(jnp.where(chosen[e], af[e], 0.0) for e in range(N_EXPERTS))
    erow = lax.broadcasted_iota(jnp.int32, (N_EXPERTS, tm), 0)
    one_hot = jnp.zeros((N_EXPERTS, tm), F32)
    for e in range(N_EXPERTS):
        one_hot = jnp.where(jnp.logical_and(erow == e, chosen[e]), 1.0, one_hot)
    earlier = (lax.broadcasted_iota(jnp.int32, (tm, tm), 0)
               < lax.broadcasted_iota(jnp.int32, (tm, tm), 1)).astype(BF16)
    rank_all = _dot(one_hot.astype(BF16), earlier) + cnt_ref[:, 0:1]
    cnt_ref[...] = cnt_ref[...] + jnp.sum(one_hot, axis=1, keepdims=True)

    seen = jnp.zeros((1, tm), jnp.bool_)
    e_a = e_b = jnp.zeros((1, tm), jnp.int32)
    r_a = r_b = w_a = w_b = jnp.zeros((1, tm), F32)
    for e in range(N_EXPERTS):
        first = jnp.logical_and(chosen[e], jnp.logical_not(seen))
        second = jnp.logical_and(chosen[e], seen)
        rk = rank_all[e:e + 1, :]
        wt = af[e] / denom
        e_a, e_b = jnp.where(first, e, e_a), jnp.where(second, e, e_b)
        r_a, r_b = jnp.where(first, rk, r_a), jnp.where(second, rk, r_b)
        w_a, w_b = jnp.where(first, wt, w_a), jnp.where(second, wt, w_b)
        seen = jnp.logical_or(seen, chosen[e])
    r8 = lax.broadcasted_iota(jnp.int32, (8, tm), 0)
    sel = jnp.where(r8 == 0, e_a, jnp.where(r8 == 1, e_b, jnp.where(
        r8 == 2, r_a.astype(jnp.int32), jnp.where(r8 == 3, r_b.astype(jnp.int32), 0))))
    row = lax.broadcasted_iota(jnp.int32, (LANES, tm), 0)
    w_t = jnp.where(row == 0, w_a, jnp.where(row == 1, w_b, 0.0))
    return sel, w_t.T


def _post_kernel(*refs, even):
    if even:
        (x_ref, a_ref, hf_ref, hb_ref, og_ref, mod_ref, wa_ref, wm_ref, gf_ref, rwt_ref, rb_ref,
         xo_ref, f_ref, sel_ref, w_ref, cnt_ref, cnt_scr) = refs
        m = ((hf_ref[0] + hb_ref[0]) * og_ref[0].astype(F32)).astype(BF16)
        o = _dot(a_ref[0], wa_ref[...]) + _dot(m, wm_ref[...])
    else:
        (x_ref, a_ref, mod_ref, wa_ref, gf_ref, rwt_ref, rb_ref,
         xo_ref, f_ref, sel_ref, w_ref, cnt_ref, cnt_scr) = refs
        o = _dot(a_ref[0].astype(BF16), wa_ref[...])

    @pl.when(jnp.logical_and(pl.program_id(0) == 0, pl.program_id(1) == 0))
    def _():
        cnt_scr[...] = jnp.zeros(cnt_scr.shape, F32)

    x = x_ref[0] + mod_ref[0, 2:3, :] * o
    xo_ref[0] = x
    f = _norm_mod(x, gf_ref[...], mod_ref[0, 3:4, :], mod_ref[0, 4:5, :])
    _store_row_tiles(f_ref.at[0], f)
    sel, w_col = _route(f, rwt_ref, rb_ref, cnt_scr)
    sel_ref[0, 0] = sel
    w_ref[0] = w_col
    cnt_ref[...] = cnt_scr[...]


def _post_mixer(x, acts, weights, modtab, gffn, rwt, rb, n_ctx, x_row_off, even):
    b, t, _ = acts[0].shape
    d = x.shape[2]
    tm = ROW_TILE
    nt = t // tm
    ctx_tiles = n_ctx // tm
    off = x_row_off // tm
    full = lambda a: pl.BlockSpec(a.shape, lambda bi, i: (0,) * a.ndim)
    row = lambda w: pl.BlockSpec((1, tm, w), lambda bi, i: (bi, i, 0))
    mod_spec = pl.BlockSpec(
        (1, 8, d), lambda bi, i: (2 * bi + (i + off >= ctx_tiles).astype(jnp.int32), 0, 0))
    in_specs = ([pl.BlockSpec((1, tm, d), lambda bi, i: (bi, i + off, 0))]
                + [row(a.shape[2]) for a in acts] + [mod_spec]
                + [full(w) for w in weights] + [full(gffn), full(rwt), full(rb)])
    return pl.pallas_call(
        functools.partial(_post_kernel, even=even),
        grid=(b, nt),
        in_specs=in_specs,
        out_specs=[row(d), pl.BlockSpec((1, tm * d // LANES, LANES), lambda bi, i: (bi, i, 0)),
                   pl.BlockSpec((1, 1, 8, tm), lambda bi, i: (bi, i, 0, 0)),
                   row(LANES),
                   pl.BlockSpec((N_EXPERTS, LANES), lambda bi, i: (0, 0))],
        out_shape=[jax.ShapeDtypeStruct((b, t, d), F32),
                   jax.ShapeDtypeStruct((b, t * d // LANES, LANES), F32),
                   jax.ShapeDtypeStruct((b, nt, 8, tm), jnp.int32),
                   jax.ShapeDtypeStruct((b, t, LANES), F32),
                   jax.ShapeDtypeStruct((N_EXPERTS, LANES), F32)],
        scratch_shapes=[pltpu.VMEM((N_EXPERTS, LANES), F32)],
        compiler_params=_params("arbitrary", "arbitrary"),
        name="post_mixer_even" if even else "post_mixer_odd",
    )(x, *acts, modtab, *weights, gffn, rwt, rb)


EXPERT_ROW_TILE = 512
DMA_ISSUE_UNROLL = 8


def _moe_plan(sel, counts, n_tok):
    e_a, e_b, r_a, r_b = (sel[:, :, k, :].reshape(-1) for k in range(4))
    cnt = counts[:, 0].astype(jnp.int32)
    padded = ((cnt + EXPERT_ROW_TILE - 1) // EXPERT_ROW_TILE) * EXPERT_ROW_TILE
    ends = jnp.cumsum(padded)
    starts = ends - padded
    pos = jnp.concatenate([starts[e_a] + r_a, starts[e_b] + r_b]).astype(jnp.int32)
    n_tiles = 2 * n_tok // EXPERT_ROW_TILE + N_EXPERTS
    tile_start = jnp.arange(n_tiles, dtype=jnp.int32) * EXPERT_ROW_TILE
    tile_expert = jnp.minimum(jnp.sum(tile_start[:, None] >= ends[None, :], axis=1),
                              N_EXPERTS - 1).astype(jnp.int32)
    tiles_used = (ends[-1:] // EXPERT_ROW_TILE).astype(jnp.int32)
    return pos, tile_expert, tiles_used, n_tiles


def _store_row_tiles(ref, x):
    rows, d = x.shape
    n_sub = d // LANES
    for s in range(n_sub):
        ref[pl.ds(s, rows, stride=n_sub), :] = x[:, s * LANES:(s + 1) * LANES]


def _load_row_tiles(ref, rows):
    n_sub = ref.shape[0] // rows
    return jnp.concatenate([ref[pl.ds(s, rows, stride=n_sub), :] for s in range(n_sub)], axis=1)


def _dispatch_kernel(pos_ref, f_ref, init_ref, out_ref, sem, *, n_tok, tm):
    del init_ref
    n_sub = f_ref.shape[1] // tm
    base = (pl.program_id(0) * pl.num_programs(1) + pl.program_id(1)) * tm

    def row_copy(r, k):
        dst = pos_ref[k * n_tok + base + r]
        src = f_ref.at[0, pl.ds(pl.multiple_of(r * n_sub, n_sub), n_sub), :]
        return pltpu.make_async_copy(src, out_ref.at[dst], sem)

    def issue(r, carry):
        row_copy(r, 0).start()
        row_copy(r, 1).start(priority=1)
        return carry

    lax.fori_loop(0, tm, issue, 0, unroll=DMA_ISSUE_UNROLL)
    for _ in range(2):
        pltpu.make_async_copy(out_ref.at[pl.ds(0, tm)], out_ref.at[pl.ds(0, tm)], sem).wait()


def _dispatch(pos, f, n_rows, n_tok):
    b, rows, _ = f.shape
    n_sub = rows * b // n_tok
    tm = ROW_TILE
    grid_spec = pltpu.PrefetchScalarGridSpec(
        num_scalar_prefetch=1,
        grid=(b, n_tok // b // tm),
        in_specs=[pl.BlockSpec((1, tm * n_sub, LANES), lambda bi, i, pos_ref: (bi, i, 0)),
                  pl.BlockSpec(memory_space=pl.ANY)],
        out_specs=pl.BlockSpec(memory_space=pl.ANY),
        scratch_shapes=[pltpu.SemaphoreType.DMA(())])
    return pl.pallas_call(
        functools.partial(_dispatch_kernel, n_tok=n_tok, tm=tm),
        grid_spec=grid_spec,
        out_shape=jax.ShapeDtypeStruct((n_rows, n_sub, LANES), F32),
        input_output_aliases={2: 0},
        compiler_params=_params("arbitrary", "arbitrary"),
        name="moe_dispatch",
    )(pos, f, jnp.zeros((n_rows, n_sub, LANES), F32))


def _expert_ffn_kernel(te_ref, used_ref, x_ref, wg_ref, wu_ref, wd_ref, y_ref, wg_scr, wu_scr, wd_scr):
    j = pl.program_id(0)
    live = j < used_ref[0]

    @pl.when(jnp.logical_or(j == 0, te_ref[j] != te_ref[jnp.maximum(j - 1, 0)]))
    def _():
        wg_scr[...] = wg_ref[0, 0].astype(BF16)
        wu_scr[...] = wu_ref[0, 0].astype(BF16)
        wd_scr[...] = wd_ref[0, 0].astype(BF16)

    @pl.when(live)
    def _():
        xb = _load_row_tiles(x_ref, EXPERT_ROW_TILE).astype(BF16)
        he = _silu(_dot(xb, wg_scr[...])) * _dot(xb, wu_scr[...])
        _store_row_tiles(y_ref, _dot(he.astype(BF16), wd_scr[...]))

    @pl.when(jnp.logical_not(live))
    def _():
        y_ref[...] = jnp.zeros(y_ref.shape, F32)


def _expert_ffn(tile_expert, tiles_used, xs, wg, wu, wd, layer, n_tiles):
    n_rows, n_sub, _ = xs.shape
    _, _, d, d_e = wg.shape
    tr = EXPERT_ROW_TILE
    grid_spec = pltpu.PrefetchScalarGridSpec(
        num_scalar_prefetch=2,
        grid=(n_tiles,),
        in_specs=[pl.BlockSpec((tr * n_sub, LANES), lambda j, te, used: (j, 0)),
                  pl.BlockSpec((1, 1, d, d_e), lambda j, te, used: (layer, te[j], 0, 0)),
                  pl.BlockSpec((1, 1, d, d_e), lambda j, te, used: (layer, te[j], 0, 0)),
                  pl.BlockSpec((1, 1, d_e, d), lambda j, te, used: (layer, te[j], 0, 0))],
        out_specs=pl.BlockSpec((tr * n_sub, LANES), lambda j, te, used: (j, 0)),
        scratch_shapes=[pltpu.VMEM((d, d_e), BF16), pltpu.VMEM((d, d_e), BF16),
                        pltpu.VMEM((d_e, d), BF16)])
    ys = pl.pallas_call(
        _expert_ffn_kernel,
        grid_spec=grid_spec,
        out_shape=jax.ShapeDtypeStruct((n_rows * n_sub, LANES), F32),
        compiler_params=_params("arbitrary"),
        name="moe_expert_ffn",
    )(tile_expert, tiles_used, xs.reshape(n_rows * n_sub, LANES), wg, wu, wd)
    return ys.reshape(n_rows, n_sub, LANES)


def _combine_kernel(pos_ref, y_ref, w_ref, x_ref, modc_ref, modl_ref, fg_ref, o_ref, buf, sem,
                    *, n_tok, n_ctx, final_norm):
    i = pl.program_id(1)
    tm = x_ref.shape[1]
    step = pl.program_id(0) * pl.num_programs(1) + i
    n_steps = pl.num_programs(0) * pl.num_programs(1)
    n_sub = y_ref.shape[1]

    def gather(tile, slot):
        base = tile * tm

        def issue(r, carry):
            for k in range(2):
                src = pos_ref[k * n_tok + base + r]
                dst = buf.at[slot, k, pl.ds(pl.multiple_of(r * n_sub, n_sub), n_sub), :]
                pltpu.make_async_copy(y_ref.at[src], dst, sem.at[slot]).start(priority=k)
            return carry

        lax.fori_loop(0, tm, issue, 0, unroll=DMA_ISSUE_UNROLL)

    slot = lax.rem(step, 2)

    @pl.when(step == 0)
    def _():
        gather(0, 0)

    @pl.when(step + 1 < n_steps)
    def _():
        gather(step + 1, 1 - slot)

    for _ in range(2):
        pltpu.make_async_copy(y_ref.at[pl.ds(0, tm)], y_ref.at[pl.ds(0, tm)], sem.at[slot]).wait()
    w = w_ref[0]
    y = (_load_row_tiles(buf.at[slot, 0], tm) * w[:, 0:1]
         + _load_row_tiles(buf.at[slot, 1], tm) * w[:, 1:2])
    rows = i * tm + lax.broadcasted_iota(jnp.int32, (tm, 1), 0)
    gate = jnp.where(rows < n_ctx, modc_ref[0, 5:6, :], modl_ref[0, 5:6, :])
    out = x_ref[0] + gate * y
    if final_norm:
        out = out * lax.rsqrt(jnp.mean(out * out, axis=-1, keepdims=True) + NORM_EPS) * fg_ref[...]
    o_ref[0] = out


def _combine(pos, ys, w_col, xmid, modtab, final_g, n_ctx, final_norm):
    b, t, d = xmid.shape
    tm = ROW_TILE
    row = lambda w: pl.BlockSpec((1, tm, w), lambda bi, i, pos_ref: (bi, i, 0))
    grid_spec = pltpu.PrefetchScalarGridSpec(
        num_scalar_prefetch=1,
        grid=(b, t // tm),
        in_specs=[pl.BlockSpec(memory_space=pl.ANY), row(LANES), row(d),
                  pl.BlockSpec((1, 8, d), lambda bi, i, pos_ref: (2 * bi, 0, 0)),
                  pl.BlockSpec((1, 8, d), lambda bi, i, pos_ref: (2 * bi + 1, 0, 0)),
                  pl.BlockSpec(final_g.shape, lambda bi, i, pos_ref: (0, 0))],
        out_specs=row(d),
        scratch_shapes=[pltpu.VMEM((2, 2, tm * d // LANES, LANES), F32), pltpu.SemaphoreType.DMA((2,))])
    return pl.pallas_call(
        functools.partial(_combine_kernel, n_tok=b * t, n_ctx=n_ctx, final_norm=final_norm),
        grid_spec=grid_spec,
        out_shape=jax.ShapeDtypeStruct((b, t, d), F32),
        compiler_params=_params("arbitrary", "arbitrary"),
        name="moe_combine",
    )(pos, ys, w_col, xmid, modtab, modtab, final_g)


def _moe(f, sel, w_col, counts, wg, wu, wd, layer, xmid, modtab, final_g, n_ctx, final_norm):
    b, t, _ = xmid.shape
    pos, tile_expert, tiles_used, n_tiles = _moe_plan(sel, counts, b * t)
    xs = _dispatch(pos, f, n_tiles * EXPERT_ROW_TILE, b * t)
    ys = _expert_ffn(tile_expert, tiles_used, xs, wg, wu, wd, layer, n_tiles)
    return _combine(pos, ys, w_col, xmid, modtab, final_g, n_ctx, final_norm)


def _chan_dft_kernel(x_ref, mod_ref, g_ref, w_ref, zr_ref, zi_ref):
    h = _norm_mod(x_ref[0], g_ref[...], mod_ref[0, 0:1, :], mod_ref[0, 1:2, :])
    gd = w_ref.shape[0]
    for gi in range(h.shape[1] // gd):
        z = _dot_split(h[:, gi * gd:(gi + 1) * gd], w_ref[...])
        zr_ref[0, :, gi * gd:(gi + 1) * gd] = z[:, :gd]
        zi_ref[0, :, gi * gd:(gi + 1) * gd] = z[:, gd:]


def _chan_dft(x, modtab, g, w_cs, x_row_off, t):
    b, _, d = x.shape
    tm = ROW_TILE
    off = x_row_off // tm
    row = pl.BlockSpec((1, tm, d), lambda bi, i: (bi, i, 0))
    return pl.pallas_call(
        _chan_dft_kernel,
        grid=(b, t // tm),
        in_specs=[pl.BlockSpec((1, tm, d), lambda bi, i: (bi, i + off, 0)),
                  pl.BlockSpec((1, 8, d), lambda bi, i: (2 * bi + 1, 0, 0)),
                  pl.BlockSpec(g.shape, lambda bi, i: (0, 0)),
                  pl.BlockSpec(w_cs.shape, lambda bi, i: (0, 0))],
        out_specs=[row, row],
        out_shape=[jax.ShapeDtypeStruct((b, t, d), F32)] * 2,
        compiler_params=_params("parallel", "parallel"),
        name="chan_dft",
    )(x, modtab, g, w_cs)


DFT_SUB = 8


def _dft1_kernel(zr_ref, zi_ref, w_ref, yr_ref, yi_ref):
    _, n1, sub, cols = zr_ref.shape
    z = jnp.concatenate([jnp.concatenate([ref[0, :, j, :] for j in range(sub)], axis=1)
                         for ref in (zr_ref, zi_ref)], axis=0)
    y = _dot_split(w_ref[...], z)
    for j in range(sub):
        yr_ref[0, :, j, :] = y[:n1, j * cols:(j + 1) * cols]
        yi_ref[0, :, j, :] = y[n1:, j * cols:(j + 1) * cols]


def _dft1(zr, zi, w1):
    b, n1, n2, d = zr.shape
    cols = d // 2
    blk = pl.BlockSpec((1, n1, DFT_SUB, cols), lambda bi, j, c: (bi, 0, j, c))
    return pl.pallas_call(
        _dft1_kernel,
        grid=(b, n2 // DFT_SUB, d // cols),
        in_specs=[blk, blk, pl.BlockSpec(w1.shape, lambda bi, j, c: (0, 0))],
        out_specs=[blk, blk],
        out_shape=[jax.ShapeDtypeStruct(zr.shape, F32)] * 2,
        compiler_params=_params("parallel", "parallel", "parallel"),
        name="dft_stage1",
    )(zr, zi, w1)


def _dft2_kernel(yr_ref, yi_ref, tab_ref, o_ref):
    for j in range(yr_ref.shape[1]):
        y = jnp.concatenate([yr_ref[0, j], yi_ref[0, j]], axis=0)
        o_ref[0, :, j, :] = _dot_split(tab_ref[j], y)


def _dft2(yr, yi, tab):
    b, n1, n2, d = yr.shape
    blk = pl.BlockSpec((1, DFT_SUB, n2, d), lambda bi, k1: (bi, k1, 0, 0))
    return pl.pallas_call(
        _dft2_kernel,
        grid=(b, n1 // DFT_SUB),
        in_specs=[blk, blk, pl.BlockSpec((DFT_SUB, n2, 2 * n2), lambda bi, k1: (k1, 0, 0))],
        out_specs=pl.BlockSpec((1, n2, DFT_SUB, d), lambda bi, k1: (bi, 0, k1, 0)),
        out_shape=jax.ShapeDtypeStruct((b, n2, n1, d), F32),
        compiler_params=_params("parallel", "parallel"),
        name="dft_stage2",
    )(yr, yi, tab)


def _dft_tables(t, gd):
    n2 = ML_CHUNK
    n1 = t // n2
    def cs(num, den):
        ang = (2.0 * np.pi / den) * (num % den).astype(np.float64)
        return np.cos(ang), np.sin(ang)
    c = np.arange(gd)
    cc, sc = cs(np.outer(c, c), gd)
    w_cs = np.concatenate([cc, -sc], axis=1)
    a = np.arange(n1)
    c1, s1 = cs(np.outer(a, a), n1)
    w1 = np.block([[c1, s1], [-s1, c1]])
    k = a[:, None, None] + n1 * np.arange(n2)[None, :, None]
    c2, s2 = cs(k * np.arange(n2)[None, None, :], t)
    tab = np.concatenate([c2, s2], axis=2) / math.sqrt(t * gd)
    return (jnp.asarray(w_cs, F32), jnp.asarray(w1, F32), jnp.asarray(tab, F32))


def _rope_tables(n_ctx, n_lat):
    pos = jnp.arange(n_lat, dtype=jnp.int32)
    n_axis = DA_QK_DIM // 4
    inv = ROPE_BASE ** (-jnp.arange(n_axis, dtype=F32) / n_axis)
    ang = jnp.concatenate([(pos // GRID_W).astype(F32)[:, None] * inv,
                           (pos % GRID_W).astype(F32)[:, None] * inv], axis=-1)
    cos, sin = jnp.cos(ang), jnp.sin(ang)
    cos = jnp.concatenate([jnp.ones((n_ctx, 2 * n_axis), F32), cos], axis=0)
    sin = jnp.concatenate([jnp.zeros((n_ctx, 2 * n_axis), F32), sin], axis=0)
    cos128 = jnp.concatenate([cos, cos, cos, cos], axis=1)
    sin128 = jnp.concatenate([-sin, sin, -sin, sin], axis=1)
    return cos128, sin128


def _deinterleave(w):
    d, n = w.shape
    w = w.reshape(d, n // DA_QK_DIM, DA_QK_DIM // 2, 2)
    return jnp.concatenate([w[..., 0], w[..., 1]], axis=-1).reshape(d, n)


def _pad_rows(a, rows):
    return jnp.concatenate([a, jnp.zeros((rows - a.shape[0],) + a.shape[1:], a.dtype)], axis=0)


def _pad_cols(a, cols):
    return jnp.concatenate([a, jnp.zeros(a.shape[:-1] + (cols - a.shape[-1],), a.dtype)], axis=-1)


def kernel(x, c, ctx, c_ctx, ada_w, ada_b, norm_mix_g, norm_ffn_g, even_w_in, even_w_out,
           even_conv_w, even_gate_b, even_lam, even_subln_g, odd_w_fnet, router_w, router_b,
           exp_w_gate, exp_w_up, exp_w_down, final_g):
    b, n_lat, d = x.shape
    n_ctx = ctx.shape[1]
    depth = ada_w.shape[0]
    assert depth == 2 and b + 1 <= 8
    assert n_ctx % ROW_TILE == 0 and n_lat % ROW_TILE == 0

    cond8 = _pad_rows(jnp.concatenate([c_ctx[None, :], c], axis=0), 8)
    rwt = router_w.T
    rb = jnp.broadcast_to(router_b[:, None], (N_EXPERTS, LANES))
    row2 = lambda v: v.reshape(1, -1)

    def modtab_for(layer):
        mods = _ada_mods(cond8, ada_w[layer], ada_b[layer]).reshape(8, 6, d)
        mods = jnp.concatenate([mods, jnp.zeros((8, 2, d), F32)], axis=1)
        idx = np.array([[0, 1 + bi] for bi in range(b)]).reshape(-1)
        return mods[idx]

    xs = jnp.concatenate([ctx, x], axis=1)
    modtab = modtab_for(0)
    w_in = even_w_in[0]
    o1 = DA_HEADS * 2 * DA_QK_DIM
    o2 = 2 * o1
    o3 = o2 + DA_HEADS * DA_V_DIM
    o4 = o3 + 2 * ML_HEADS * ML_DIM
    o5 = o4 + ML_HEADS * ML_DIM
    o6 = o5 + ML_HEADS * ML_DIM
    ws = [(_deinterleave(w_in[:, :o1]) * (DA_QK_DIM ** -0.5)).astype(BF16),
          _deinterleave(w_in[:, o1:o2]).astype(BF16),
          w_in[:, o2:o3].astype(BF16), w_in[:, o3:o4].astype(BF16),
          w_in[:, o4:o5].astype(BF16), w_in[:, o5:o6].astype(BF16),
          _pad_cols(w_in[:, o6:], LANES).astype(BF16)]
    gate_b = _pad_cols(even_gate_b[0].reshape(1, -1), LANES)
    cos128, sin128 = _rope_tables(n_ctx, n_lat)
    daq, dak, dav, mqk, mv, og, gates = _inproj(xs, modtab, row2(norm_mix_g[0]), cos128, sin128,
                                                 ws, gate_b, n_ctx)
    mq, mk, gc, gr, mvt = _mlprep(mqk, _pad_rows(even_conv_w[0], 8), gates, mv, n_ctx)
    hf, hb = _mlstm(mq, mk, mvt, gc, gr, n_ctx)
    lam_init = 0.8 - 0.6 * math.exp(-0.3 * 0)
    lam8 = _pad_rows(even_lam[0], 8)
    att = _diff_attention(daq, dak, dav, lam8, row2(even_subln_g[0]), n_ctx, lam_init)
    w_out = even_w_out[0].astype(BF16)
    half = DA_HEADS * DA_V_DIM
    xmid, f, sel, w_col, counts = _post_mixer(xs, [att, hf, hb, og], [w_out[:half], w_out[half:]],
                                              modtab, row2(norm_ffn_g[0]), rwt, rb, n_ctx, 0, True)
    xs = _moe(f, sel, w_col, counts, exp_w_gate, exp_w_up, exp_w_down, 0,
              xmid, modtab, row2(final_g), n_ctx, False)

    modtab = modtab_for(1)
    gd = d // FN_GROUPS
    w_cs, w1, tab = _dft_tables(n_lat, gd)
    n2 = ML_CHUNK
    n1 = n_lat // n2
    zr, zi = _chan_dft(xs, modtab, row2(norm_mix_g[1]), w_cs, n_ctx, n_lat)
    yr, yi = _dft1(zr.reshape(b, n1, n2, d), zi.reshape(b, n1, n2, d), w1)
    fo = _dft2(yr, yi, tab).reshape(b, n_lat, d)
    xmid, f, sel, w_col, counts = _post_mixer(xs, [fo], [odd_w_fnet[0].astype(BF16)], modtab,
                                              row2(norm_ffn_g[1]), rwt, rb, 0, n_ctx, False)
    return _moe(f, sel, w_col, counts, exp_w_gate, exp_w_up, exp_w_down, 1,
                xmid, modtab, row2(final_g), 0, True)
```

```python
import functools
import math

import jax
import jax.numpy as jnp
import numpy as np
from jax import lax
from jax.experimental import pallas as pl
from jax.experimental.pallas import tpu as pltpu

F32 = jnp.float32
BF16 = jnp.bfloat16

NORM_EPS = 1e-6
GRID_W = 64
DA_HEADS = 4
DA_QK_DIM = 64
DA_V_DIM = 128
ML_HEADS = 4
ML_DIM = 128
ML_CHUNK = 128
FN_GROUPS = 4
N_EXPERTS = 16
EXPERTS_PER_GROUP = 4
ROPE_BASE = 10000.0
LANES = 128
ROW_TILE = 256
VMEM_LIMIT_BYTES = 56 * 1024 * 1024
HI = lax.Precision.HIGHEST
LOG2_E = math.log2(math.e)


def _params(*sem):
    return pltpu.CompilerParams(dimension_semantics=sem, vmem_limit_bytes=VMEM_LIMIT_BYTES)


def _dot(a, b, precision=None):
    return jnp.dot(a, b, preferred_element_type=F32, precision=precision)


def _dot_nt(a, b, precision=None):
    return lax.dot_general(a, b, (((1,), (1,)), ((), ())), preferred_element_type=F32,
                           precision=precision)


def _dot_split(a, b):
    a_hi = a.astype(BF16)
    b_hi = b.astype(BF16)
    a_lo = (a - a_hi.astype(F32)).astype(BF16)
    b_lo = (b - b_hi.astype(F32)).astype(BF16)
    return _dot(a_hi, b_hi) + (_dot(a_hi, b_lo) + _dot(a_lo, b_hi))


def _sigmoid(x):
    return 1.0 / (1.0 + jnp.exp(-x))


def _silu(x):
    return x * _sigmoid(x)


def _norm_mod(x, g, shift, scale):
    y = x * lax.rsqrt(jnp.mean(x * x, axis=-1, keepdims=True) + NORM_EPS) * g
    return y * (1.0 + scale) + shift


def _ada_kernel(c_ref, w_ref, b_ref, o_ref):
    o_ref[...] = _dot(_silu(c_ref[...]), w_ref[...], HI) + b_ref[...]


def _ada_mods(cond8, w, b):
    d, n = w.shape
    tn = n // 6
    return pl.pallas_call(
        _ada_kernel,
        grid=(6,),
        in_specs=[pl.BlockSpec((8, d), lambda j: (0, 0)),
                  pl.BlockSpec((d, tn), lambda j: (0, j)),
                  pl.BlockSpec((1, tn), lambda j: (0, j))],
        out_specs=pl.BlockSpec((8, tn), lambda j: (0, j)),
        out_shape=jax.ShapeDtypeStruct((8, n), F32),
        compiler_params=_params("arbitrary"),
        name="ada_mods",
    )(cond8, w, b.reshape(1, n))


def _inproj_kernel(c_ref, x_ref, mod_ref, g_ref, cos_ref, sin_ref, wq_ref, wk_ref, wv_ref, wmqk_ref,
                   wmv_ref, wmo_ref, wg_ref, gb_ref,
                   q_ref, k_ref, v_ref, mqk_ref, mv_ref, og_ref, gate_ref, *, ctx_tiles):
    x = jnp.where(pl.program_id(1) < ctx_tiles, c_ref[0], x_ref[0])
    h = _norm_mod(x, g_ref[...], mod_ref[0, 0:1, :], mod_ref[0, 1:2, :]).astype(BF16)
    tm = x.shape[0]
    width = q_ref.shape[2]
    cos = jnp.concatenate([cos_ref[...]] * (width // LANES), axis=1)
    sin = jnp.concatenate([sin_ref[...]] * (width // LANES), axis=1)
    lane = lax.broadcasted_iota(jnp.int32, (tm, width), 1)
    lower = (lane & (DA_QK_DIM - 1)) < (DA_QK_DIM // 2)

    def rope(u):
        swapped = jnp.where(lower, pltpu.roll(u, width - DA_QK_DIM // 2, 1),
                            pltpu.roll(u, DA_QK_DIM // 2, 1))
        return u * cos + swapped * sin

    q_ref[0] = (rope(_dot(h, wq_ref[...])) * LOG2_E).astype(BF16)
    k_ref[0] = rope(_dot(h, wk_ref[...])).astype(BF16)
    v = _dot(h, wv_ref[...])
    pad = VT_ROWS - DA_V_DIM
    ones_row = (lax.broadcasted_iota(jnp.int32, (pad, tm), 0) == 0).astype(BF16)
    for hd in range(DA_HEADS):
        v_ref[0, hd, 0:DA_V_DIM, :] = v[:, hd * DA_V_DIM:(hd + 1) * DA_V_DIM].T.astype(BF16)
        v_ref[0, hd, DA_V_DIM:VT_ROWS, :] = ones_row
    mqk_ref[0] = _dot(h, wmqk_ref[...]).astype(BF16)
    mv_ref[0] = _dot(h, wmv_ref[...]).astype(BF16)
    og_ref[0] = _sigmoid(_dot(h, wmo_ref[...])).astype(BF16)
    g = _dot(h, wg_ref[...]) + gb_ref[...]
    glane = lax.broadcasted_iota(jnp.int32, g.shape, 1)
    is_forget = ((glane // ML_HEADS) & 1) == 1
    log_sig = jnp.minimum(g, 0.0) - jnp.log(1.0 + jnp.exp(-jnp.abs(g)))
    gate_ref[0] = jnp.where(is_forget, log_sig, g)


def _inproj(ctx, x, modtab, g, cos, sin, ws, gate_b):
    b, n_lat, d = x.shape
    n_ctx = ctx.shape[1]
    t = n_ctx + n_lat
    tm = ROW_TILE
    nt = t // tm
    ctx_tiles = n_ctx // tm
    assert ctx_tiles >= 1
    row = lambda w: pl.BlockSpec((1, tm, w), lambda bi, i: (bi, i, 0))
    full = lambda a: pl.BlockSpec(a.shape, lambda bi, i: (0,) * a.ndim)
    widths = [w.shape[1] for w in ws]
    out_dtypes = [BF16] * 6 + [F32]
    out_specs = [row(w) for w in widths]
    out_shape = [jax.ShapeDtypeStruct((b, t, w), dt) for w, dt in zip(widths, out_dtypes)]
    out_specs[2] = pl.BlockSpec((1, DA_HEADS, VT_ROWS, tm), lambda bi, i: (bi, 0, 0, i))
    out_shape[2] = jax.ShapeDtypeStruct((b, DA_HEADS, VT_ROWS, t), BF16)
    return pl.pallas_call(
        functools.partial(_inproj_kernel, ctx_tiles=ctx_tiles),
        grid=(b, nt),
        in_specs=[pl.BlockSpec((1, tm, d), lambda bi, i: (bi, jnp.minimum(i, ctx_tiles - 1), 0)),
                  pl.BlockSpec((1, tm, d), lambda bi, i: (bi, jnp.maximum(i - ctx_tiles, 0), 0)),
                  pl.BlockSpec((1, 8, d), lambda bi, i: (2 * bi + (i >= ctx_tiles).astype(jnp.int32), 0, 0)),
                  full(g),
                  pl.BlockSpec((tm, LANES), lambda bi, i: (i, 0)),
                  pl.BlockSpec((tm, LANES), lambda bi, i: (i, 0))]
                 + [full(w) for w in ws] + [full(gate_b)],
        out_specs=out_specs,
        out_shape=out_shape,
        compiler_params=_params("parallel", "parallel"),
        name="inproj",
    )(ctx, x, modtab, g, cos, sin, *ws, gate_b)


def _split3(x):
    x1 = x.astype(BF16)
    r1 = x - x1.astype(F32)
    x2 = r1.astype(BF16)
    x3 = (r1 - x2.astype(F32)).astype(BF16)
    return x1, x2, x3


VT_ROWS = ML_DIM + 16


def _mlprep_kernel(cur_ref, prev_ref, next_ref, cw_ref, gate_ref, v_ref,
                   mq_ref, mk_ref, gc_ref, gr_ref, vt_ref, *, ctx_tiles, n_tiles):
    i = pl.program_id(1)
    cur = cur_ref[0].astype(F32)
    tm, w = cur.shape
    prev_ok = i != ctx_tiles
    if ctx_tiles > 0:
        prev_ok = jnp.logical_and(prev_ok, i != 0)
        next_ok = jnp.logical_and(i != ctx_tiles - 1, i != n_tiles - 1)
    else:
        next_ok = i != n_tiles - 1
    prev_row = jnp.where(prev_ok, prev_ref[0, 7:8, :].astype(F32), 0.0)
    next_row = jnp.where(next_ok, next_ref[0, 0:1, :].astype(F32), 0.0)
    ridx = lax.broadcasted_iota(jnp.int32, (tm, w), 0)
    before = jnp.where(ridx == 0, prev_row, pltpu.roll(cur, 1, 0))
    after = jnp.where(ridx == tm - 1, next_row, pltpu.roll(cur, tm - 1, 0))
    y = _silu(before * cw_ref[0:1, :] + cur * cw_ref[1:2, :] + after * cw_ref[2:3, :])
    half = w // 2
    mq_ref[0] = y[:, :half].astype(BF16)
    mk_ref[0] = (y[:, half:] * (ML_DIM ** -0.5)).astype(BF16)

    n_g = 4 * ML_HEADS
    hds = ML_HEADS
    r = lax.broadcasted_iota(jnp.int32, (ML_CHUNK, ML_CHUNK), 0)
    c = lax.broadcasted_iota(jnp.int32, (ML_CHUNK, ML_CHUNK), 1)
    lower = (c <= r).astype(BF16)
    upper = (c >= r).astype(BF16)
    ones_row = (r == 0).astype(BF16)[0:VT_ROWS - ML_DIM]
    for ci in range(tm // ML_CHUNK):
        rows = slice(ci * ML_CHUNK, (ci + 1) * ML_CHUNK)
        gm = jnp.where(c < n_g, gate_ref[0, rows, :], 0.0)
        pre = sum(_dot(lower, p) for p in _split3(pltpu.roll(gm, n_g, 1)))
        suf = sum(_dot(upper, p) for p in _split3(pltpu.roll(gm, 2 * n_g, 1)))
        col = gm + pre + suf
        u_f = pltpu.roll(col, 3 * n_g, 1) - pltpu.roll(col, 3 * n_g - (n_g + hds), 1)
        u_b = pltpu.roll(col, 3 * n_g - hds, 1) - pltpu.roll(col, 2 * hds, 1)
        col = col + jnp.where(jnp.logical_and(c >= 3 * n_g, c < 3 * n_g + hds), u_f,
                              jnp.where(jnp.logical_and(c >= 3 * n_g + hds, c < 3 * n_g + 2 * hds),
                                        u_b, 0.0))
        gc_ref[0, rows, :] = col
        gr_ref[0, ci] = col.T
        for hd in range(ML_HEADS):
            v_t = v_ref[0, rows, hd * ML_DIM:(hd + 1) * ML_DIM].astype(F32).T.astype(BF16)
            vt_ref[0, ci, hd * VT_ROWS:hd * VT_ROWS + ML_DIM, :] = v_t
            vt_ref[0, ci, hd * VT_ROWS + ML_DIM:(hd + 1) * VT_ROWS, :] = ones_row


def _mlprep(mqk, conv_w8, gates, mv, n_ctx):
    b, t, w = mqk.shape
    tm = ROW_TILE
    nt = t // tm
    sub = tm // 8
    nsub = t // 8
    cpt = tm // ML_CHUNK
    kern = functools.partial(_mlprep_kernel, ctx_tiles=n_ctx // tm, n_tiles=nt)
    return pl.pallas_call(
        kern,
        grid=(b, nt),
        in_specs=[pl.BlockSpec((1, tm, w), lambda bi, i: (bi, i, 0)),
                  pl.BlockSpec((1, 8, w), lambda bi, i: (bi, jnp.maximum(i * sub - 1, 0), 0)),
                  pl.BlockSpec((1, 8, w), lambda bi, i: (bi, jnp.minimum((i + 1) * sub, nsub - 1), 0)),
                  pl.BlockSpec(conv_w8.shape, lambda bi, i: (0, 0)),
                  pl.BlockSpec((1, tm, LANES), lambda bi, i: (bi, i, 0)),
                  pl.BlockSpec((1, tm, w // 2), lambda bi, i: (bi, i, 0))],
        out_specs=[pl.BlockSpec((1, tm, w // 2), lambda bi, i: (bi, i, 0)),
                   pl.BlockSpec((1, tm, w // 2), lambda bi, i: (bi, i, 0)),
                   pl.BlockSpec((1, tm, LANES), lambda bi, i: (bi, i, 0)),
                   pl.BlockSpec((1, cpt, ML_CHUNK, LANES), lambda bi, i: (bi, i, 0, 0)),
                   pl.BlockSpec((1, cpt, ML_HEADS * VT_ROWS, ML_CHUNK), lambda bi, i: (bi, i, 0, 0))],
        out_shape=[jax.ShapeDtypeStruct((b, t, w // 2), BF16),
                   jax.ShapeDtypeStruct((b, t, w // 2), BF16),
                   jax.ShapeDtypeStruct((b, t, LANES), F32),
                   jax.ShapeDtypeStruct((b, t // ML_CHUNK, ML_CHUNK, LANES), F32),
                   jax.ShapeDtypeStruct((b, t // ML_CHUNK, ML_HEADS * VT_ROWS, ML_CHUNK), BF16)],
        compiler_params=_params("parallel", "parallel"),
        name="mlstm_prep",
    )(mqk, mqk, mqk, conv_w8, gates, mv)


def _mlstm_kernel(qf_ref, kf_ref, vf_ref, gcf_ref, grf_ref, qb_ref, kb_ref, vb_ref, gcb_ref, grb_ref,
                  hf_ref, hb_ref, *scr):
    s = pl.program_id(0)
    nb = qf_ref.shape[0]
    n_g = 4 * ML_HEADS
    state_refs = scr[:len(scr) // 2]
    m_refs = scr[len(scr) // 2:]

    @pl.when(s == 0)
    def _():
        for ref in scr:
            ref[...] = jnp.zeros(ref.shape, F32)

    ki = lax.broadcasted_iota(jnp.int32, (ML_CHUNK, ML_CHUNK), 0)
    qi = lax.broadcasted_iota(jnp.int32, (ML_CHUNK, ML_CHUNK), 1)

    for bi in range(nb):
        for direction in range(2):
            q_ref, k_ref, vt_ref, gc_ref, gr_ref, h_ref = (
                (qf_ref, kf_ref, vf_ref, gcf_ref, grf_ref, hf_ref) if direction == 0 else
                (qb_ref, kb_ref, vb_ref, gcb_ref, grb_ref, hb_ref))
            visible = (ki <= qi) if direction == 0 else (ki >= qi)
            gc = gc_ref[bi]
            gr = gr_ref[bi, 0]
            for hd in range(ML_HEADS):
                chain = (bi * 2 + direction) * ML_HEADS + hd
                c_lf = (2 * direction + 1) * ML_HEADS + hd
                c_cs = c_lf + (n_g if direction == 0 else 2 * n_g)
                c_u = 3 * n_g + direction * ML_HEADS + hd
                lo, hi = hd * ML_DIM, (hd + 1) * ML_DIM
                q = q_ref[bi, :, lo:hi]
                k = k_ref[bi, :, lo:hi]
                v_t = vt_ref[bi, 0, hd * VT_ROWS:(hd + 1) * VT_ROWS, :]
                u_row = gr[c_u:c_u + 1, :]
                lf_row = gr[c_lf:c_lf + 1, :]
                cs_row = gr[c_cs:c_cs + 1, :]
                m_st = m_refs[chain][:, 0:1]
                st_t = state_refs[chain][...]

                u_vis = jnp.where(visible, jnp.broadcast_to(gc[:, c_u:c_u + 1], visible.shape), -jnp.inf)
                v_row = jnp.maximum(m_st, jnp.max(u_vis, axis=0, keepdims=True))
                d_t = jnp.exp(u_vis - v_row)
                inter = jnp.exp(m_st - v_row)
                sc_t = (_dot_nt(k, q) * d_t).astype(BF16)
                tot_t = inter * _dot_nt(st_t.astype(BF16), q) + _dot(v_t, sc_t)
                den = jnp.maximum(jnp.abs(tot_t[ML_DIM:ML_DIM + 1, :]), jnp.exp(-(cs_row + v_row)))
                h_ref[bi, :, lo:hi] = (tot_t[:ML_DIM, :] / den).T

                b_last = jnp.sum(lf_row, axis=1, keepdims=True)
                m_sc = jnp.maximum(m_st, jnp.max(u_row, axis=1, keepdims=True))
                wgt = jnp.exp(u_row - m_sc)
                vw_t = (v_t.astype(F32) * wgt).astype(BF16)
                state_refs[chain][...] = jnp.exp(m_st - m_sc) * st_t + _dot(vw_t, k)
                m_refs[chain][...] = jnp.broadcast_to(b_last + m_sc, (1, LANES))


def _mlstm(mq, mk, mvt, gc, gr, n_ctx):
    b, t, w = mq.shape
    nc = t // ML_CHUNK
    ncc = n_ctx // ML_CHUNK
    fwd = lambda s: s
    bwd = lambda s: jnp.where(s < ncc, ncc - 1 - s, nc - 1 - s + ncc)
    tok = lambda f: pl.BlockSpec((b, ML_CHUNK, w), lambda s: (0, f(s), 0))
    gcs = lambda f: pl.BlockSpec((b, ML_CHUNK, LANES), lambda s: (0, f(s), 0))
    grs = lambda f: pl.BlockSpec((b, 1, ML_CHUNK, LANES), lambda s: (0, f(s), 0, 0))
    vts = lambda f: pl.BlockSpec((b, 1, ML_HEADS * VT_ROWS, ML_CHUNK), lambda s: (0, f(s), 0, 0))
    n_chain = b * 2 * ML_HEADS
    return pl.pallas_call(
        _mlstm_kernel,
        grid=(nc,),
        in_specs=[tok(fwd), tok(fwd), vts(fwd), gcs(fwd), grs(fwd),
                  tok(bwd), tok(bwd), vts(bwd), gcs(bwd), grs(bwd)],
        out_specs=[tok(fwd), tok(bwd)],
        out_shape=[jax.ShapeDtypeStruct((b, t, w), F32)] * 2,
        scratch_shapes=[pltpu.VMEM((VT_ROWS, ML_DIM), F32)] * n_chain
                       + [pltpu.VMEM((1, LANES), F32)] * n_chain,
        compiler_params=_params("arbitrary"),
        name="mlstm",
    )(mq, mk, mvt, gc, gr, mq, mk, mvt, gc, gr)


def _attn_kernel(q_ref, qn_ref, k_ref, vt_ref, lam_ref, sg_ref, o_ref, qm_scr, *scr,
                 ctx_tiles, n_ctx, key_block, n_blocks, lam_init):
    s_scr = (scr[0:2], scr[2:4])
    bm_scr = (scr[4:6], scr[6:8])
    m_scr = scr[8:10]
    acc_scr = scr[10:12]
    i = pl.program_id(2)
    dv = DA_V_DIM
    q = q_ref[0]
    lane = lax.broadcasted_iota(jnp.int32, q.shape, 1)
    zero = jnp.zeros_like(q)

    def map_halves(qv):
        return jnp.where(lane < DA_QK_DIM, qv, zero), jnp.where(lane >= DA_QK_DIM, qv, zero)

    qm_scr[0], qm_scr[1] = map_halves(q)

    def scores(mp, start, size, q_maps=None):
        q_map = qm_scr[mp] if q_maps is None else q_maps[mp]
        return _dot_nt(k_ref[0, pl.ds(start, size), :], q_map)

    @pl.when(i < ctx_tiles)
    def _():
        for mp in range(2):
            sc = scores(mp, 0, n_ctx)
            m = jnp.max(sc, axis=0, keepdims=True)
            p = jnp.exp2(sc - m).astype(BF16)
            acc_scr[mp][...] = _dot(vt_ref[0, 0, :, 0:n_ctx], p)

    def stage(j, slot, q_maps=None):
        start = pl.multiple_of(j * key_block, LANES)
        for mp in range(2):
            sc = scores(mp, start, key_block, q_maps)
            s_scr[mp][slot][...] = sc
            bm_scr[mp][slot][...] = jnp.max(sc, axis=0, keepdims=True)

    def consume(j, slot):
        v_t = vt_ref[0, 0, :, pl.ds(pl.multiple_of(j * key_block, LANES), key_block)]
        for mp in range(2):
            m_old = m_scr[mp][...]
            m_new = jnp.maximum(m_old, bm_scr[mp][slot][...])
            alpha = jnp.exp2(m_old - m_new)
            p = jnp.exp2(s_scr[mp][slot][...] - m_new).astype(BF16)
            acc_scr[mp][...] = alpha * acc_scr[mp][...] + _dot(v_t, p)
            m_scr[mp][...] = m_new

    @pl.when(i >= ctx_tiles)
    def _():
        for mp in range(2):
            m_scr[mp][...] = jnp.full(m_scr[mp].shape, -jnp.inf, F32)
            acc_scr[mp][...] = jnp.zeros(acc_scr[mp].shape, F32)

        @pl.when(i == ctx_tiles)
        def _():
            stage(0, 0)

        def body(g, carry):
            stage(2 * g + 1, 1)
            consume(2 * g, 0)
            stage(2 * g + 2, 0)
            consume(2 * g + 1, 1)
            return carry
        lax.fori_loop(0, n_blocks // 2 - 1, body, 0)
        stage(n_blocks - 1, 1)
        consume(n_blocks - 2, 0)
        stage(0, 0, map_halves(qn_ref[0]))
        consume(n_blocks - 1, 1)

    lv = lam_ref[...]
    dot01 = jnp.sum(lv[0:1, :] * lv[1:2, :], axis=1, keepdims=True)
    dot23 = jnp.sum(lv[2:3, :] * lv[3:4, :], axis=1, keepdims=True)
    lam = jnp.exp(dot01) - jnp.exp(dot23) + lam_init
    a0 = acc_scr[0][...]
    a1 = acc_scr[1][...]
    o = (a0[0:dv, :] / a0[dv:dv + 1, :] - lam * (a1[0:dv, :] / a1[dv:dv + 1, :])).T
    o = o * lax.rsqrt(jnp.mean(o * o, axis=-1, keepdims=True) + NORM_EPS) * sg_ref[...]
    o_ref[0] = (o * (1.0 - lam_init)).astype(BF16)


def _diff_attention(q, k, vt, lam8, subln_g, n_ctx, lam_init):
    b, t, w = q.shape
    tq = ROW_TILE
    n_blocks = next(n for n in (10, 8, 6, 4, 2) if t % (n * LANES) == 0 and t // n >= n_ctx)
    key_block = t // n_blocks
    vt_rows = vt.shape[2]
    assert n_ctx % tq == 0
    kern = functools.partial(_attn_kernel, ctx_tiles=n_ctx // tq, n_ctx=n_ctx, key_block=key_block,
                             n_blocks=n_blocks, lam_init=lam_init)
    return pl.pallas_call(
        kern,
        grid=(b, DA_HEADS, t // tq),
        in_specs=[pl.BlockSpec((1, tq, LANES), lambda bi, h, i: (bi, i, h)),
                  pl.BlockSpec((1, tq, LANES), lambda bi, h, i: (bi, jnp.minimum(i + 1, t // tq - 1), h)),
                  pl.BlockSpec((1, t, LANES), lambda bi, h, i: (bi, 0, h)),
                  pl.BlockSpec((1, 1, vt_rows, t), lambda bi, h, i: (bi, h, 0, 0)),
                  pl.BlockSpec(lam8.shape, lambda bi, h, i: (0, 0)),
                  pl.BlockSpec(subln_g.shape, lambda bi, h, i: (0, 0))],
        out_specs=pl.BlockSpec((1, tq, LANES), lambda bi, h, i: (bi, i, h)),
        out_shape=jax.ShapeDtypeStruct((b, t, w), BF16),
        scratch_shapes=[pltpu.VMEM((2, tq, LANES), BF16)]
                       + [pltpu.VMEM((key_block, tq), F32)] * 4
                       + [pltpu.VMEM((1, tq), F32)] * 6
                       + [pltpu.VMEM((vt_rows, tq), F32)] * 2,
        compiler_params=_params("arbitrary", "arbitrary", "arbitrary"),
        name="diff_attention",
    )(q, q, k, vt, lam8, subln_g)


def _top2_sum(a, b, c, d):
    hi1, lo1 = jnp.maximum(a, b), jnp.minimum(a, b)
    hi2, lo2 = jnp.maximum(c, d), jnp.minimum(c, d)
    return jnp.maximum(hi1, hi2) + jnp.maximum(jnp.minimum(hi1, hi2), jnp.maximum(lo1, lo2))


def _route(f, rwt_ref, rb_ref, cnt_ref):
    tm = f.shape[0]
    aff = _sigmoid(_dot_nt(rwt_ref[...], f, HI))
    biased = aff + rb_ref[:, 0:1]
    bz = [biased[e:e + 1, :] for e in range(N_EXPERTS)]
    af = [aff[e:e + 1, :] for e in range(N_EXPERTS)]
    n_grp = N_EXPERTS // EXPERTS_PER_GROUP
    scores = [_top2_sum(*bz[EXPERTS_PER_GROUP * g:EXPERTS_PER_GROUP * (g + 1)]) for g in range(n_grp)]
    best = scores[0]
    sel_grp = jnp.zeros_like(best, dtype=jnp.int32)
    for g in range(1, n_grp):
        better = scores[g] > best
        sel_grp = jnp.where(better, g, sel_grp)
        best = jnp.where(better, scores[g], best)
    chosen = []
    for e in range(N_EXPERTS):
        g = e // EXPERTS_PER_GROUP
        rank = jnp.zeros_like(sel_grp)
        for o in range(EXPERTS_PER_GROUP * g, EXPERTS_PER_GROUP * (g + 1)):
            if o == e:
                continue
            beats = (bz[o] > bz[e]) if o > e else (bz[o] >= bz[e])
            rank = rank + beats.astype(jnp.int32)
        chosen.append(jnp.logical_and(sel_grp == g, rank < 2))
    denom = sum(jnp.where(chosen[e], af[e], 0.0) for e in range(N_EXPERTS))
    erow = lax.broadcasted_iota(jnp.int32, (N_EXPERTS, tm), 0)
    one_hot = jnp.zeros((N_EXPERTS, tm), F32)
    for e in range(N_EXPERTS):
        one_hot = jnp.where(jnp.logical_and(erow == e, chosen[e]), 1.0, one_hot)
    earlier = (lax.broadcasted_iota(jnp.int32, (tm, tm), 0)
               < lax.broadcasted_iota(jnp.int32, (tm, tm), 1)).astype(BF16)
    rank_all = _dot(one_hot.astype(BF16), earlier) + cnt_ref[:, 0:1]
    cnt_ref[...] = cnt_ref[...] + jnp.sum(one_hot, axis=1, keepdims=True)

    seen = jnp.zeros((1, tm), jnp.bool_)
    e_a = e_b = jnp.zeros((1, tm), jnp.int32)
    r_a = r_b = w_a = w_b = jnp.zeros((1, tm), F32)
    for e in range(N_EXPERTS):
        first = jnp.logical_and(chosen[e], jnp.logical_not(seen))
        second = jnp.logical_and(chosen[e], seen)
        rk = rank_all[e:e + 1, :]
        wt = af[e] / denom
        e_a, e_b = jnp.where(first, e, e_a), jnp.where(second, e, e_b)
        r_a, r_b = jnp.where(first, rk, r_a), jnp.where(second, rk, r_b)
        w_a, w_b = jnp.where(first, wt, w_a), jnp.where(second, wt, w_b)
        seen = jnp.logical_or(seen, chosen[e])
    r8 = lax.broadcasted_iota(jnp.int32, (8, tm), 0)
    sel = jnp.where(r8 == 0, e_a, jnp.where(r8 == 1, e_b, jnp.where(
        r8 == 2, r_a.astype(jnp.int32), jnp.where(r8 == 3, r_b.astype(jnp.int32), 0))))
    row = lax.broadcasted_iota(jnp.int32, (LANES, tm), 0)
    w_t = jnp.where(row == 0, w_a, jnp.where(row == 1, w_b, 0.0))
    return sel, w_t.T


def _post_kernel(*refs, even, ctx_tiles):
    if even:
        (c_ref, x_ref, a_ref, hf_ref, hb_ref, og_ref, mod_ref, wa_ref, wm_ref, gf_ref, rwt_ref, rb_ref,
         xo_ref, f_ref, sel_ref, w_ref, cnt_ref, cnt_scr) = refs
        m = ((hf_ref[0] + hb_ref[0]) * og_ref[0].astype(F32)).astype(BF16)
        o = _dot(a_ref[0], wa_ref[...]) + _dot(m, wm_ref[...])
        x_in = jnp.where(pl.program_id(1) < ctx_tiles, c_ref[0], x_ref[0])
    else:
        (x_ref, a_ref, mod_ref, wa_ref, gf_ref, rwt_ref, rb_ref,
         xo_ref, f_ref, sel_ref, w_ref, cnt_ref, cnt_scr) = refs
        o = _dot(a_ref[0].astype(BF16), wa_ref[...])
        x_in = x_ref[0]

    @pl.when(jnp.logical_and(pl.program_id(0) == 0, pl.program_id(1) == 0))
    def _():
        cnt_scr[...] = jnp.zeros(cnt_scr.shape, F32)

    x = x_in + mod_ref[0, 2:3, :] * o
    xo_ref[0] = x
    f = _norm_mod(x, gf_ref[...], mod_ref[0, 3:4, :], mod_ref[0, 4:5, :])
    _store_row_tiles(f_ref.at[0], f)
    sel, w_col = _route(f, rwt_ref, rb_ref, cnt_scr)
    sel_ref[0, 0] = sel
    w_ref[0] = w_col
    cnt_ref[...] = cnt_scr[...]


def _post_mixer(x, acts, weights, modtab, gffn, rwt, rb, n_ctx, x_row_off, even, x_ctx=None):
    b, t, _ = acts[0].shape
    d = x.shape[2]
    tm = ROW_TILE
    nt = t // tm
    ctx_tiles = n_ctx // tm
    off = x_row_off // tm
    full = lambda a: pl.BlockSpec(a.shape, lambda bi, i: (0,) * a.ndim)
    row = lambda w: pl.BlockSpec((1, tm, w), lambda bi, i: (bi, i, 0))
    mod_spec = pl.BlockSpec(
        (1, 8, d), lambda bi, i: (2 * bi + (i + off >= ctx_tiles).astype(jnp.int32), 0, 0))
    if x_ctx is None:
        residual, res_specs = [x], [pl.BlockSpec((1, tm, d), lambda bi, i: (bi, i + off, 0))]
    else:
        assert even and x_row_off == 0 and ctx_tiles >= 1
        residual = [x_ctx, x]
        res_specs = [pl.BlockSpec((1, tm, d), lambda bi, i: (bi, jnp.minimum(i, ctx_tiles - 1), 0)),
                     pl.BlockSpec((1, tm, d), lambda bi, i: (bi, jnp.maximum(i - ctx_tiles, 0), 0))]
    in_specs = (res_specs + [row(a.shape[2]) for a in acts] + [mod_spec]
                + [full(w) for w in weights] + [full(gffn), full(rwt), full(rb)])
    return pl.pallas_call(
        functools.partial(_post_kernel, even=even, ctx_tiles=ctx_tiles),
        grid=(b, nt),
        in_specs=in_specs,
        out_specs=[row(d), pl.BlockSpec((1, tm * d // LANES, LANES), lambda bi, i: (bi, i, 0)),
                   pl.BlockSpec((1, 1, 8, tm), lambda bi, i: (bi, i, 0, 0)),
                   row(LANES),
                   pl.BlockSpec((N_EXPERTS, LANES), lambda bi, i: (0, 0))],
        out_shape=[jax.ShapeDtypeStruct((b, t, d), F32),
                   jax.ShapeDtypeStruct((b, t * d // LANES, LANES), F32),
                   jax.ShapeDtypeStruct((b, nt, 8, tm), jnp.int32),
                   jax.ShapeDtypeStruct((b, t, LANES), F32),
                   jax.ShapeDtypeStruct((N_EXPERTS, LANES), F32)],
        scratch_shapes=[pltpu.VMEM((N_EXPERTS, LANES), F32)],
        compiler_params=_params("arbitrary", "arbitrary"),
        name="post_mixer_even" if even else "post_mixer_odd",
    )(*residual, *acts, modtab, *weights, gffn, rwt, rb)


EXPERT_ROW_TILE = 512
DMA_ISSUE_UNROLL = 8


def _moe_plan(sel, counts, n_tok):
    e_a, e_b, r_a, r_b = (sel[:, :, k, :].reshape(-1) for k in range(4))
    cnt = counts[:, 0].astype(jnp.int32)
    padded = ((cnt + EXPERT_ROW_TILE - 1) // EXPERT_ROW_TILE) * EXPERT_ROW_TILE
    ends = jnp.cumsum(padded)
    starts = ends - padded
    pos = jnp.concatenate([starts[e_a] + r_a, starts[e_b] + r_b]).astype(jnp.int32)
    n_tiles = 2 * n_tok // EXPERT_ROW_TILE + N_EXPERTS
    tile_start = jnp.arange(n_tiles, dtype=jnp.int32) * EXPERT_ROW_TILE
    tile_expert = jnp.minimum(jnp.sum(tile_start[:, None] >= ends[None, :], axis=1),
                              N_EXPERTS - 1).astype(jnp.int32)
    tiles_used = (ends[-1:] // EXPERT_ROW_TILE).astype(jnp.int32)
    return pos, tile_expert, tiles_used, n_tiles


def _store_row_tiles(ref, x):
    rows, d = x.shape
    n_sub = d // LANES
    for s in range(n_sub):
        ref[pl.ds(s, rows, stride=n_sub), :] = x[:, s * LANES:(s + 1) * LANES]


def _load_row_tiles(ref, rows):
    n_sub = ref.shape[0] // rows
    return jnp.concatenate([ref[pl.ds(s, rows, stride=n_sub), :] for s in range(n_sub)], axis=1)


def _dispatch_kernel(pos_ref, f_ref, init_ref, out_ref, sem, *, n_tok, tm):
    del init_ref
    n_sub = f_ref.shape[1] // tm
    base = (pl.program_id(0) * pl.num_programs(1) + pl.program_id(1)) * tm

    def row_copy(r, k):
        dst = pos_ref[k * n_tok + base + r]
        src = f_ref.at[0, pl.ds(pl.multiple_of(r * n_sub, n_sub), n_sub), :]
        return pltpu.make_async_copy(src, out_ref.at[dst], sem)

    def issue(r, carry):
        row_copy(r, 0).start()
        row_copy(r, 1).start(priority=1)
        return carry

    lax.fori_loop(0, tm, issue, 0, unroll=DMA_ISSUE_UNROLL)
    for _ in range(2):
        pltpu.make_async_copy(out_ref.at[pl.ds(0, tm)], out_ref.at[pl.ds(0, tm)], sem).wait()


def _dispatch(pos, f, n_rows, n_tok):
    b, rows, _ = f.shape
    n_sub = rows * b // n_tok
    tm = ROW_TILE
    grid_spec = pltpu.PrefetchScalarGridSpec(
        num_scalar_prefetch=1,
        grid=(b, n_tok // b // tm),
        in_specs=[pl.BlockSpec((1, tm * n_sub, LANES), lambda bi, i, pos_ref: (bi, i, 0)),
                  pl.BlockSpec(memory_space=pl.ANY)],
        out_specs=pl.BlockSpec(memory_space=pl.ANY),
        scratch_shapes=[pltpu.SemaphoreType.DMA(())])
    return pl.pallas_call(
        functools.partial(_dispatch_kernel, n_tok=n_tok, tm=tm),
        grid_spec=grid_spec,
        out_shape=jax.ShapeDtypeStruct((n_rows, n_sub, LANES), F32),
        input_output_aliases={2: 0},
        compiler_params=_params("arbitrary", "arbitrary"),
        name="moe_dispatch",
    )(pos, f, jnp.zeros((n_rows, n_sub, LANES), F32))


def _expert_ffn_kernel(te_ref, used_ref, x_ref, wg_ref, wu_ref, wd_ref, y_ref, wg_scr, wu_scr, wd_scr):
    j = pl.program_id(0)
    live = j < used_ref[0]

    @pl.when(jnp.logical_or(j == 0, te_ref[j] != te_ref[jnp.maximum(j - 1, 0)]))
    def _():
        wg_scr[...] = wg_ref[0, 0].astype(BF16)
        wu_scr[...] = wu_ref[0, 0].astype(BF16)
        wd_scr[...] = wd_ref[0, 0].astype(BF16)

    @pl.when(live)
    def _():
        xb = _load_row_tiles(x_ref, EXPERT_ROW_TILE).astype(BF16)
        he = _silu(_dot(xb, wg_scr[...])) * _dot(xb, wu_scr[...])
        _store_row_tiles(y_ref, _dot(he.astype(BF16), wd_scr[...]))

    @pl.when(jnp.logical_not(live))
    def _():
        y_ref[...] = jnp.zeros(y_ref.shape, F32)


def _expert_ffn(tile_expert, tiles_used, xs, wg, wu, wd, layer, n_tiles):
    n_rows, n_sub, _ = xs.shape
    _, _, d, d_e = wg.shape
    tr = EXPERT_ROW_TILE
    grid_spec = pltpu.PrefetchScalarGridSpec(
        num_scalar_prefetch=2,
        grid=(n_tiles,),
        in_specs=[pl.BlockSpec((tr * n_sub, LANES), lambda j, te, used: (j, 0)),
                  pl.BlockSpec((1, 1, d, d_e), lambda j, te, used: (layer, te[j], 0, 0)),
                  pl.BlockSpec((1, 1, d, d_e), lambda j, te, used: (layer, te[j], 0, 0)),
                  pl.BlockSpec((1, 1, d_e, d), lambda j, te, used: (layer, te[j], 0, 0))],
        out_specs=pl.BlockSpec((tr * n_sub, LANES), lambda j, te, used: (j, 0)),
        scratch_shapes=[pltpu.VMEM((d, d_e), BF16), pltpu.VMEM((d, d_e), BF16),
                        pltpu.VMEM((d_e, d), BF16)])
    ys = pl.pallas_call(
        _expert_ffn_kernel,
        grid_spec=grid_spec,
        out_shape=jax.ShapeDtypeStruct((n_rows * n_sub, LANES), F32),
        compiler_params=_params("arbitrary"),
        name="moe_expert_ffn",
    )(tile_expert, tiles_used, xs.reshape(n_rows * n_sub, LANES), wg, wu, wd)
    return ys.reshape(n_rows, n_sub, LANES)


def _combine_kernel(pos_ref, y_ref, w_ref, x_ref, modc_ref, modl_ref, fg_ref, o_ref, buf, sem,
                    *, n_tok, n_ctx, final_norm):
    i = pl.program_id(1)
    tm = x_ref.shape[1]
    step = pl.program_id(0) * pl.num_programs(1) + i
    n_steps = pl.num_programs(0) * pl.num_programs(1)
    n_sub = y_ref.shape[1]

    def gather(tile, slot):
        base = tile * tm

        def issue(r, carry):
            for k in range(2):
                src = pos_ref[k * n_tok + base + r]
                dst = buf.at[slot, k, pl.ds(pl.multiple_of(r * n_sub, n_sub), n_sub), :]
                pltpu.make_async_copy(y_ref.at[src], dst, sem.at[slot]).start(priority=k)
            return carry

        lax.fori_loop(0, tm, issue, 0, unroll=DMA_ISSUE_UNROLL)

    slot = lax.rem(step, 2)

    @pl.when(step == 0)
    def _():
        gather(0, 0)

    @pl.when(step + 1 < n_steps)
    def _():
        gather(step + 1, 1 - slot)

    for _ in range(2):
        pltpu.make_async_copy(y_ref.at[pl.ds(0, tm)], y_ref.at[pl.ds(0, tm)], sem.at[slot]).wait()
    w = w_ref[0]
    y = (_load_row_tiles(buf.at[slot, 0], tm) * w[:, 0:1]
         + _load_row_tiles(buf.at[slot, 1], tm) * w[:, 1:2])
    rows = i * tm + lax.broadcasted_iota(jnp.int32, (tm, 1), 0)
    gate = jnp.where(rows < n_ctx, modc_ref[0, 5:6, :], modl_ref[0, 5:6, :])
    out = x_ref[0] + gate * y
    if final_norm:
        out = out * lax.rsqrt(jnp.mean(out * out, axis=-1, keepdims=True) + NORM_EPS) * fg_ref[...]
    o_ref[0] = out


def _combine(pos, ys, w_col, xmid, modtab, final_g, n_ctx, final_norm):
    b, t, d = xmid.shape
    tm = ROW_TILE
    row = lambda w: pl.BlockSpec((1, tm, w), lambda bi, i, pos_ref: (bi, i, 0))
    grid_spec = pltpu.PrefetchScalarGridSpec(
        num_scalar_prefetch=1,
        grid=(b, t // tm),
        in_specs=[pl.BlockSpec(memory_space=pl.ANY), row(LANES), row(d),
                  pl.BlockSpec((1, 8, d), lambda bi, i, pos_ref: (2 * bi, 0, 0)),
                  pl.BlockSpec((1, 8, d), lambda bi, i, pos_ref: (2 * bi + 1, 0, 0)),
                  pl.BlockSpec(final_g.shape, lambda bi, i, pos_ref: (0, 0))],
        out_specs=row(d),
        scratch_shapes=[pltpu.VMEM((2, 2, tm * d // LANES, LANES), F32), pltpu.SemaphoreType.DMA((2,))])
    return pl.pallas_call(
        functools.partial(_combine_kernel, n_tok=b * t, n_ctx=n_ctx, final_norm=final_norm),
        grid_spec=grid_spec,
        out_shape=jax.ShapeDtypeStruct((b, t, d), F32),
        compiler_params=_params("arbitrary", "arbitrary"),
        name="moe_combine",
    )(pos, ys, w_col, xmid, modtab, modtab, final_g)


def _moe(f, sel, w_col, counts, wg, wu, wd, layer, xmid, modtab, final_g, n_ctx, final_norm):
    b, t, _ = xmid.shape
    pos, tile_expert, tiles_used, n_tiles = _moe_plan(sel, counts, b * t)
    xs = _dispatch(pos, f, n_tiles * EXPERT_ROW_TILE, b * t)
    ys = _expert_ffn(tile_expert, tiles_used, xs, wg, wu, wd, layer, n_tiles)
    return _combine(pos, ys, w_col, xmid, modtab, final_g, n_ctx, final_norm)


def _chan_dft_kernel(x_ref, mod_ref, g_ref, w_ref, zr_ref, zi_ref):
    h = _norm_mod(x_ref[0], g_ref[...], mod_ref[0, 0:1, :], mod_ref[0, 1:2, :])
    gd = w_ref.shape[0]
    for gi in range(h.shape[1] // gd):
        z = _dot_split(h[:, gi * gd:(gi + 1) * gd], w_ref[...])
        zr_ref[0, :, gi * gd:(gi + 1) * gd] = z[:, :gd]
        zi_ref[0, :, gi * gd:(gi + 1) * gd] = z[:, gd:]


def _chan_dft(x, modtab, g, w_cs, x_row_off, t):
    b, _, d = x.shape
    tm = ROW_TILE
    off = x_row_off // tm
    row = pl.BlockSpec((1, tm, d), lambda bi, i: (bi, i, 0))
    return pl.pallas_call(
        _chan_dft_kernel,
        grid=(b, t // tm),
        in_specs=[pl.BlockSpec((1, tm, d), lambda bi, i: (bi, i + off, 0)),
                  pl.BlockSpec((1, 8, d), lambda bi, i: (2 * bi + 1, 0, 0)),
                  pl.BlockSpec(g.shape, lambda bi, i: (0, 0)),
                  pl.BlockSpec(w_cs.shape, lambda bi, i: (0, 0))],
        out_specs=[row, row],
        out_shape=[jax.ShapeDtypeStruct((b, t, d), F32)] * 2,
        compiler_params=_params("parallel", "parallel"),
        name="chan_dft",
    )(x, modtab, g, w_cs)


DFT_SUB = 8


def _dft1_kernel(zr_ref, zi_ref, w_ref, yr_ref, yi_ref):
    _, n1, sub, cols = zr_ref.shape
    z = jnp.concatenate([jnp.concatenate([ref[0, :, j, :] for j in range(sub)], axis=1)
                         for ref in (zr_ref, zi_ref)], axis=0)
    y = _dot_split(w_ref[...], z)
    for j in range(sub):
        yr_ref[0, :, j, :] = y[:n1, j * cols:(j + 1) * cols]
        yi_ref[0, :, j, :] = y[n1:, j * cols:(j + 1) * cols]


def _dft1(zr, zi, w1):
    b, n1, n2, d = zr.shape
    cols = d // 2
    blk = pl.BlockSpec((1, n1, DFT_SUB, cols), lambda bi, j, c: (bi, 0, j, c))
    return pl.pallas_call(
        _dft1_kernel,
        grid=(b, n2 // DFT_SUB, d // cols),
        in_specs=[blk, blk, pl.BlockSpec(w1.shape, lambda bi, j, c: (0, 0))],
        out_specs=[blk, blk],
        out_shape=[jax.ShapeDtypeStruct(zr.shape, F32)] * 2,
        compiler_params=_params("parallel", "parallel", "parallel"),
        name="dft_stage1",
    )(zr, zi, w1)


def _dft2_kernel(yr_ref, yi_ref, tab_ref, o_ref):
    for j in range(yr_ref.shape[1]):
        y = jnp.concatenate([yr_ref[0, j], yi_ref[0, j]], axis=0)
        o_ref[0, :, j, :] = _dot_split(tab_ref[j], y)


def _dft2(yr, yi, tab):
    b, n1, n2, d = yr.shape
    blk = pl.BlockSpec((1, DFT_SUB, n2, d), lambda bi, k1: (bi, k1, 0, 0))
    return pl.pallas_call(
        _dft2_kernel,
        grid=(b, n1 // DFT_SUB),
        in_specs=[blk, blk, pl.BlockSpec((DFT_SUB, n2, 2 * n2), lambda bi, k1: (k1, 0, 0))],
        out_specs=pl.BlockSpec((1, n2, DFT_SUB, d), lambda bi, k1: (bi, 0, k1, 0)),
        out_shape=jax.ShapeDtypeStruct((b, n2, n1, d), F32),
        compiler_params=_params("parallel", "parallel"),
        name="dft_stage2",
    )(yr, yi, tab)


def _dft_tables(t, gd):
    n2 = ML_CHUNK
    n1 = t // n2
    def cs(num, den):
        ang = (2.0 * np.pi / den) * (num % den).astype(np.float64)
        return np.cos(ang), np.sin(ang)
    c = np.arange(gd)
    cc, sc = cs(np.outer(c, c), gd)
    w_cs = np.concatenate([cc, -sc], axis=1)
    a = np.arange(n1)
    c1, s1 = cs(np.outer(a, a), n1)
    w1 = np.block([[c1, s1], [-s1, c1]])
    k = a[:, None, None] + n1 * np.arange(n2)[None, :, None]
    c2, s2 = cs(k * np.arange(n2)[None, None, :], t)
    tab = np.concatenate([c2, s2], axis=2) / math.sqrt(t * gd)
    return (jnp.asarray(w_cs, F32), jnp.asarray(w1, F32), jnp.asarray(tab, F32))


def _rope_tables(n_ctx, n_lat):
    pos = jnp.arange(n_lat, dtype=jnp.int32)
    n_axis = DA_QK_DIM // 4
    inv = ROPE_BASE ** (-jnp.arange(n_axis, dtype=F32) / n_axis)
    ang = jnp.concatenate([(pos // GRID_W).astype(F32)[:, None] * inv,
                           (pos % GRID_W).astype(F32)[:, None] * inv], axis=-1)
    cos, sin = jnp.cos(ang), jnp.sin(ang)
    cos = jnp.concatenate([jnp.ones((n_ctx, 2 * n_axis), F32), cos], axis=0)
    sin = jnp.concatenate([jnp.zeros((n_ctx, 2 * n_axis), F32), sin], axis=0)
    cos128 = jnp.concatenate([cos, cos, cos, cos], axis=1)
    sin128 = jnp.concatenate([-sin, sin, -sin, sin], axis=1)
    return cos128, sin128


def _deinterleave(w):
    d, n = w.shape
    w = w.reshape(d, n // DA_QK_DIM, DA_QK_DIM // 2, 2)
    return jnp.concatenate([w[..., 0], w[..., 1]], axis=-1).reshape(d, n)


def _pad_rows(a, rows):
    return jnp.concatenate([a, jnp.zeros((rows - a.shape[0],) + a.shape[1:], a.dtype)], axis=0)


def _pad_cols(a, cols):
    return jnp.concatenate([a, jnp.zeros(a.shape[:-1] + (cols - a.shape[-1],), a.dtype)], axis=-1)


def kernel(x, c, ctx, c_ctx, ada_w, ada_b, norm_mix_g, norm_ffn_g, even_w_in, even_w_out,
           even_conv_w, even_gate_b, even_lam, even_subln_g, odd_w_fnet, router_w, router_b,
           exp_w_gate, exp_w_up, exp_w_down, final_g):
    b, n_lat, d = x.shape
    n_ctx = ctx.shape[1]
    depth = ada_w.shape[0]
    assert depth == 2 and b + 1 <= 8
    assert n_ctx % ROW_TILE == 0 and n_lat % ROW_TILE == 0

    cond8 = _pad_rows(jnp.concatenate([c_ctx[None, :], c], axis=0), 8)
    rwt = router_w.T
    rb = jnp.broadcast_to(router_b[:, None], (N_EXPERTS, LANES))
    row2 = lambda v: v.reshape(1, -1)

    def modtab_for(layer):
        mods = _ada_mods(cond8, ada_w[layer], ada_b[layer]).reshape(8, 6, d)
        mods = jnp.concatenate([mods, jnp.zeros((8, 2, d), F32)], axis=1)
        idx = np.array([[0, 1 + bi] for bi in range(b)]).reshape(-1)
        return mods[idx]

    modtab = modtab_for(0)
    w_in = even_w_in[0]
    o1 = DA_HEADS * 2 * DA_QK_DIM
    o2 = 2 * o1
    o3 = o2 + DA_HEADS * DA_V_DIM
    o4 = o3 + 2 * ML_HEADS * ML_DIM
    o5 = o4 + ML_HEADS * ML_DIM
    o6 = o5 + ML_HEADS * ML_DIM
    ws = [(_deinterleave(w_in[:, :o1]) * (DA_QK_DIM ** -0.5)).astype(BF16),
          _deinterleave(w_in[:, o1:o2]).astype(BF16),
          w_in[:, o2:o3].astype(BF16), w_in[:, o3:o4].astype(BF16),
          w_in[:, o4:o5].astype(BF16), w_in[:, o5:o6].astype(BF16),
          _pad_cols(w_in[:, o6:], LANES).astype(BF16)]
    gate_b = _pad_cols(even_gate_b[0].reshape(1, -1), LANES)
    cos128, sin128 = _rope_tables(n_ctx, n_lat)
    daq, dak, dav, mqk, mv, og, gates = _inproj(ctx, x, modtab, row2(norm_mix_g[0]), cos128, sin128,
                                                 ws, gate_b)
    mq, mk, gc, gr, mvt = _mlprep(mqk, _pad_rows(even_conv_w[0], 8), gates, mv, n_ctx)
    hf, hb = _mlstm(mq, mk, mvt, gc, gr, n_ctx)
    lam_init = 0.8 - 0.6 * math.exp(-0.3 * 0)
    lam8 = _pad_rows(even_lam[0], 8)
    att = _diff_attention(daq, dak, dav, lam8, row2(even_subln_g[0]), n_ctx, lam_init)
    w_out = even_w_out[0].astype(BF16)
    half = DA_HEADS * DA_V_DIM
    xmid, f, sel, w_col, counts = _post_mixer(x, [att, hf, hb, og], [w_out[:half], w_out[half:]],
                                              modtab, row2(norm_ffn_g[0]), rwt, rb, n_ctx, 0, True,
                                              x_ctx=ctx)
    xs = _moe(f, sel, w_col, counts, exp_w_gate, exp_w_up, exp_w_down, 0,
              xmid, modtab, row2(final_g), n_ctx, False)

    modtab = modtab_for(1)
    gd = d // FN_GROUPS
    w_cs, w1, tab = _dft_tables(n_lat, gd)
    n2 = ML_CHUNK
    n1 = n_lat // n2
    zr, zi = _chan_dft(xs, modtab, row2(norm_mix_g[1]), w_cs, n_ctx, n_lat)
    yr, yi = _dft1(zr.reshape(b, n1, n2, d), zi.reshape(b, n1, n2, d), w1)
    fo = _dft2(yr, yi, tab).reshape(b, n_lat, d)
    xmid, f, sel, w_col, counts = _post_mixer(xs, [fo], [odd_w_fnet[0].astype(BF16)], modtab,
                                              row2(norm_ffn_g[1]), rwt, rb, 0, n_ctx, False)
    return _moe(f, sel, w_col, counts, exp_w_gate, exp_w_up, exp_w_down, 1,
                xmid, modtab, row2(final_g), 0, True)
```

```python
import functools
import math

import jax
import jax.numpy as jnp
import numpy as np
from jax import lax
from jax.experimental import pallas as pl
from jax.experimental.pallas import tpu as pltpu

F32 = jnp.float32
BF16 = jnp.bfloat16

NORM_EPS = 1e-6
GRID_W = 64
DA_HEADS = 4
DA_QK_DIM = 64
DA_V_DIM = 128
ML_HEADS = 4
ML_DIM = 128
ML_CHUNK = 128
FN_GROUPS = 4
N_EXPERTS = 16
EXPERTS_PER_GROUP = 4
ROPE_BASE = 10000.0
LANES = 128
ROW_TILE = 256
VMEM_LIMIT_BYTES = 56 * 1024 * 1024
HI = lax.Precision.HIGHEST
LOG2_E = math.log2(math.e)


def _params(*sem):
    return pltpu.CompilerParams(dimension_semantics=sem, vmem_limit_bytes=VMEM_LIMIT_BYTES)


def _dot(a, b, precision=None):
    return jnp.dot(a, b, preferred_element_type=F32, precision=precision)


def _dot_nt(a, b, precision=None):
    return lax.dot_general(a, b, (((1,), (1,)), ((), ())), preferred_element_type=F32,
                           precision=precision)


def _dot_split(a, b):
    a_hi = a.astype(BF16)
    b_hi = b.astype(BF16)
    a_lo = (a - a_hi.astype(F32)).astype(BF16)
    b_lo = (b - b_hi.astype(F32)).astype(BF16)
    return _dot(a_hi, b_hi) + (_dot(a_hi, b_lo) + _dot(a_lo, b_hi))


def _sigmoid(x):
    return 1.0 / (1.0 + jnp.exp(-x))


def _silu(x):
    return x * _sigmoid(x)


def _norm_mod(x, g, shift, scale):
    y = x * lax.rsqrt(jnp.mean(x * x, axis=-1, keepdims=True) + NORM_EPS) * g
    return y * (1.0 + scale) + shift


def _ada_kernel(c_ref, w_ref, b_ref, o_ref):
    o_ref[...] = _dot(_silu(c_ref[...]), w_ref[...], HI) + b_ref[...]


def _ada_mods(cond8, w, b):
    d, n = w.shape
    tn = n // 6
    return pl.pallas_call(
        _ada_kernel,
        grid=(6,),
        in_specs=[pl.BlockSpec((8, d), lambda j: (0, 0)),
                  pl.BlockSpec((d, tn), lambda j: (0, j)),
                  pl.BlockSpec((1, tn), lambda j: (0, j))],
        out_specs=pl.BlockSpec((8, tn), lambda j: (0, j)),
        out_shape=jax.ShapeDtypeStruct((8, n), F32),
        compiler_params=_params("arbitrary"),
        name="ada_mods",
    )(cond8, w, b.reshape(1, n))


def _inproj_kernel(c_ref, x_ref, mod_ref, g_ref, cos_ref, sin_ref, wq_ref, wk_ref, wv_ref, wmqk_ref,
                   wmv_ref, wmo_ref, wg_ref, gb_ref,
                   q_ref, k_ref, v_ref, mqk_ref, mv_ref, og_ref, gate_ref, *, ctx_tiles):
    x = jnp.where(pl.program_id(1) < ctx_tiles, c_ref[0], x_ref[0])
    h = _norm_mod(x, g_ref[...], mod_ref[0, 0:1, :], mod_ref[0, 1:2, :]).astype(BF16)
    tm = x.shape[0]
    width = q_ref.shape[2]
    cos = jnp.concatenate([cos_ref[...]] * (width // LANES), axis=1)
    sin = jnp.concatenate([sin_ref[...]] * (width // LANES), axis=1)
    lane = lax.broadcasted_iota(jnp.int32, (tm, width), 1)
    lower = (lane & (DA_QK_DIM - 1)) < (DA_QK_DIM // 2)

    def rope(u):
        swapped = jnp.where(lower, pltpu.roll(u, width - DA_QK_DIM // 2, 1),
                            pltpu.roll(u, DA_QK_DIM // 2, 1))
        return u * cos + swapped * sin

    q_ref[0] = (rope(_dot(h, wq_ref[...])) * LOG2_E).astype(BF16)
    k_ref[0] = rope(_dot(h, wk_ref[...])).astype(BF16)
    v = _dot(h, wv_ref[...])
    pad = VT_ROWS - DA_V_DIM
    ones_row = (lax.broadcasted_iota(jnp.int32, (pad, tm), 0) == 0).astype(BF16)
    for hd in range(DA_HEADS):
        v_ref[0, hd, 0:DA_V_DIM, :] = v[:, hd * DA_V_DIM:(hd + 1) * DA_V_DIM].T.astype(BF16)
        v_ref[0, hd, DA_V_DIM:VT_ROWS, :] = ones_row
    mqk_ref[0] = _dot(h, wmqk_ref[...]).astype(BF16)
    mv_ref[0] = _dot(h, wmv_ref[...]).astype(BF16)
    og_ref[0] = _sigmoid(_dot(h, wmo_ref[...])).astype(BF16)
    g = _dot(h, wg_ref[...]) + gb_ref[...]
    glane = lax.broadcasted_iota(jnp.int32, g.shape, 1)
    is_forget = ((glane // ML_HEADS) & 1) == 1
    log_sig = jnp.minimum(g, 0.0) - jnp.log(1.0 + jnp.exp(-jnp.abs(g)))
    gate_ref[0] = jnp.where(is_forget, log_sig, g)


def _inproj(ctx, x, modtab, g, cos, sin, ws, gate_b):
    b, n_lat, d = x.shape
    n_ctx = ctx.shape[1]
    t = n_ctx + n_lat
    tm = ROW_TILE
    nt = t // tm
    ctx_tiles = n_ctx // tm
    assert ctx_tiles >= 1
    row = lambda w: pl.BlockSpec((1, tm, w), lambda bi, i: (bi, i, 0))
    full = lambda a: pl.BlockSpec(a.shape, lambda bi, i: (0,) * a.ndim)
    widths = [w.shape[1] for w in ws]
    out_dtypes = [BF16] * 6 + [F32]
    out_specs = [row(w) for w in widths]
    out_shape = [jax.ShapeDtypeStruct((b, t, w), dt) for w, dt in zip(widths, out_dtypes)]
    out_specs[2] = pl.BlockSpec((1, DA_HEADS, VT_ROWS, tm), lambda bi, i: (bi, 0, 0, i))
    out_shape[2] = jax.ShapeDtypeStruct((b, DA_HEADS, VT_ROWS, t), BF16)
    return pl.pallas_call(
        functools.partial(_inproj_kernel, ctx_tiles=ctx_tiles),
        grid=(b, nt),
        in_specs=[pl.BlockSpec((1, tm, d), lambda bi, i: (bi, jnp.minimum(i, ctx_tiles - 1), 0)),
                  pl.BlockSpec((1, tm, d), lambda bi, i: (bi, jnp.maximum(i - ctx_tiles, 0), 0)),
                  pl.BlockSpec((1, 8, d), lambda bi, i: (2 * bi + (i >= ctx_tiles).astype(jnp.int32), 0, 0)),
                  full(g),
                  pl.BlockSpec((tm, LANES), lambda bi, i: (i, 0)),
                  pl.BlockSpec((tm, LANES), lambda bi, i: (i, 0))]
                 + [full(w) for w in ws] + [full(gate_b)],
        out_specs=out_specs,
        out_shape=out_shape,
        compiler_params=_params("parallel", "parallel"),
        name="inproj",
    )(ctx, x, modtab, g, cos, sin, *ws, gate_b)


def _split3(x):
    x1 = x.astype(BF16)
    r1 = x - x1.astype(F32)
    x2 = r1.astype(BF16)
    x3 = (r1 - x2.astype(F32)).astype(BF16)
    return x1, x2, x3


VT_ROWS = ML_DIM + 16


def _mlprep_kernel(cur_ref, prev_ref, next_ref, cw_ref, gate_ref, v_ref,
                   mq_ref, mk_ref, gc_ref, gr_ref, vt_ref, *, ctx_tiles, n_tiles):
    i = pl.program_id(1)
    cur = cur_ref[0].astype(F32)
    tm, w = cur.shape
    prev_ok = i != ctx_tiles
    if ctx_tiles > 0:
        prev_ok = jnp.logical_and(prev_ok, i != 0)
        next_ok = jnp.logical_and(i != ctx_tiles - 1, i != n_tiles - 1)
    else:
        next_ok = i != n_tiles - 1
    prev_row = jnp.where(prev_ok, prev_ref[0, 7:8, :].astype(F32), 0.0)
    next_row = jnp.where(next_ok, next_ref[0, 0:1, :].astype(F32), 0.0)
    ridx = lax.broadcasted_iota(jnp.int32, (tm, w), 0)
    before = jnp.where(ridx == 0, prev_row, pltpu.roll(cur, 1, 0))
    after = jnp.where(ridx == tm - 1, next_row, pltpu.roll(cur, tm - 1, 0))
    y = _silu(before * cw_ref[0:1, :] + cur * cw_ref[1:2, :] + after * cw_ref[2:3, :])
    half = w // 2
    mq_ref[0] = y[:, :half].astype(BF16)
    mk_ref[0] = (y[:, half:] * (ML_DIM ** -0.5)).astype(BF16)

    n_g = 4 * ML_HEADS
    hds = ML_HEADS
    r = lax.broadcasted_iota(jnp.int32, (ML_CHUNK, ML_CHUNK), 0)
    c = lax.broadcasted_iota(jnp.int32, (ML_CHUNK, ML_CHUNK), 1)
    lower = (c <= r).astype(BF16)
    upper = (c >= r).astype(BF16)
    ones_row = (r == 0).astype(BF16)[0:VT_ROWS - ML_DIM]
    for ci in range(tm // ML_CHUNK):
        rows = slice(ci * ML_CHUNK, (ci + 1) * ML_CHUNK)
        gm = jnp.where(c < n_g, gate_ref[0, rows, :], 0.0)
        pre = sum(_dot(lower, p) for p in _split3(pltpu.roll(gm, n_g, 1)))
        suf = sum(_dot(upper, p) for p in _split3(pltpu.roll(gm, 2 * n_g, 1)))
        col = gm + pre + suf
        u_f = pltpu.roll(col, 3 * n_g, 1) - pltpu.roll(col, 3 * n_g - (n_g + hds), 1)
        u_b = pltpu.roll(col, 3 * n_g - hds, 1) - pltpu.roll(col, 2 * hds, 1)
        col = col + jnp.where(jnp.logical_and(c >= 3 * n_g, c < 3 * n_g + hds), u_f,
                              jnp.where(jnp.logical_and(c >= 3 * n_g + hds, c < 3 * n_g + 2 * hds),
                                        u_b, 0.0))
        gc_ref[0, rows, :] = col
        gr_ref[0, ci] = col.T
        for hd in range(ML_HEADS):
            v_t = v_ref[0, rows, hd * ML_DIM:(hd + 1) * ML_DIM].astype(F32).T.astype(BF16)
            vt_ref[0, ci, hd * VT_ROWS:hd * VT_ROWS + ML_DIM, :] = v_t
            vt_ref[0, ci, hd * VT_ROWS + ML_DIM:(hd + 1) * VT_ROWS, :] = ones_row


def _mlprep(mqk, conv_w8, gates, mv, n_ctx):
    b, t, w = mqk.shape
    tm = ROW_TILE
    nt = t // tm
    sub = tm // 8
    nsub = t // 8
    cpt = tm // ML_CHUNK
    kern = functools.partial(_mlprep_kernel, ctx_tiles=n_ctx // tm, n_tiles=nt)
    return pl.pallas_call(
        kern,
        grid=(b, nt),
        in_specs=[pl.BlockSpec((1, tm, w), lambda bi, i: (bi, i, 0)),
                  pl.BlockSpec((1, 8, w), lambda bi, i: (bi, jnp.maximum(i * sub - 1, 0), 0)),
                  pl.BlockSpec((1, 8, w), lambda bi, i: (bi, jnp.minimum((i + 1) * sub, nsub - 1), 0)),
                  pl.BlockSpec(conv_w8.shape, lambda bi, i: (0, 0)),
                  pl.BlockSpec((1, tm, LANES), lambda bi, i: (bi, i, 0)),
                  pl.BlockSpec((1, tm, w // 2), lambda bi, i: (bi, i, 0))],
        out_specs=[pl.BlockSpec((1, tm, w // 2), lambda bi, i: (bi, i, 0)),
                   pl.BlockSpec((1, tm, w // 2), lambda bi, i: (bi, i, 0)),
                   pl.BlockSpec((1, tm, LANES), lambda bi, i: (bi, i, 0)),
                   pl.BlockSpec((1, cpt, ML_CHUNK, LANES), lambda bi, i: (bi, i, 0, 0)),
                   pl.BlockSpec((1, cpt, ML_HEADS * VT_ROWS, ML_CHUNK), lambda bi, i: (bi, i, 0, 0))],
        out_shape=[jax.ShapeDtypeStruct((b, t, w // 2), BF16),
                   jax.ShapeDtypeStruct((b, t, w // 2), BF16),
                   jax.ShapeDtypeStruct((b, t, LANES), F32),
                   jax.ShapeDtypeStruct((b, t // ML_CHUNK, ML_CHUNK, LANES), F32),
                   jax.ShapeDtypeStruct((b, t // ML_CHUNK, ML_HEADS * VT_ROWS, ML_CHUNK), BF16)],
        compiler_params=_params("parallel", "parallel"),
        name="mlstm_prep",
    )(mqk, mqk, mqk, conv_w8, gates, mv)


def _mlstm_kernel(qf_ref, kf_ref, vf_ref, gcf_ref, grf_ref, qb_ref, kb_ref, vb_ref, gcb_ref, grb_ref,
                  hf_ref, hb_ref, *scr):
    s = pl.program_id(0)
    nb = qf_ref.shape[0]
    n_g = 4 * ML_HEADS
    state_refs = scr[:len(scr) // 2]
    m_refs = scr[len(scr) // 2:]

    @pl.when(s == 0)
    def _():
        for ref in scr:
            ref[...] = jnp.zeros(ref.shape, F32)

    ki = lax.broadcasted_iota(jnp.int32, (ML_CHUNK, ML_CHUNK), 0)
    qi = lax.broadcasted_iota(jnp.int32, (ML_CHUNK, ML_CHUNK), 1)

    for bi in range(nb):
        for direction in range(2):
            q_ref, k_ref, vt_ref, gc_ref, gr_ref, h_ref = (
                (qf_ref, kf_ref, vf_ref, gcf_ref, grf_ref, hf_ref) if direction == 0 else
                (qb_ref, kb_ref, vb_ref, gcb_ref, grb_ref, hb_ref))
            visible = (ki <= qi) if direction == 0 else (ki >= qi)
            gc = gc_ref[bi]
            gr = gr_ref[bi, 0]
            for hd in range(ML_HEADS):
                chain = (bi * 2 + direction) * ML_HEADS + hd
                c_lf = (2 * direction + 1) * ML_HEADS + hd
                c_cs = c_lf + (n_g if direction == 0 else 2 * n_g)
                c_u = 3 * n_g + direction * ML_HEADS + hd
                lo, hi = hd * ML_DIM, (hd + 1) * ML_DIM
                q = q_ref[bi, :, lo:hi]
                k = k_ref[bi, :, lo:hi]
                v_t = vt_ref[bi, 0, hd * VT_ROWS:(hd + 1) * VT_ROWS, :]
                u_row = gr[c_u:c_u + 1, :]
                lf_row = gr[c_lf:c_lf + 1, :]
                cs_row = gr[c_cs:c_cs + 1, :]
                m_st = m_refs[chain][:, 0:1]
                st_t = state_refs[chain][...]

                u_vis = jnp.where(visible, jnp.broadcast_to(gc[:, c_u:c_u + 1], visible.shape), -jnp.inf)
                v_row = jnp.maximum(m_st, jnp.max(u_vis, axis=0, keepdims=True))
                d_t = jnp.exp(u_vis - v_row)
                inter = jnp.exp(m_st - v_row)
                sc_t = (_dot_nt(k, q) * d_t).astype(BF16)
                tot_t = inter * _dot_nt(st_t.astype(BF16), q) + _dot(v_t, sc_t)
                den = jnp.maximum(jnp.abs(tot_t[ML_DIM:ML_DIM + 1, :]), jnp.exp(-(cs_row + v_row)))
                h_ref[bi, :, lo:hi] = (tot_t[:ML_DIM, :] / den).T

                b_last = jnp.sum(lf_row, axis=1, keepdims=True)
                m_sc = jnp.maximum(m_st, jnp.max(u_row, axis=1, keepdims=True))
                wgt = jnp.exp(u_row - m_sc)
                vw_t = (v_t.astype(F32) * wgt).astype(BF16)
                state_refs[chain][...] = jnp.exp(m_st - m_sc) * st_t + _dot(vw_t, k)
                m_refs[chain][...] = jnp.broadcast_to(b_last + m_sc, (1, LANES))


def _mlstm(mq, mk, mvt, gc, gr, n_ctx):
    b, t, w = mq.shape
    nc = t // ML_CHUNK
    ncc = n_ctx // ML_CHUNK
    fwd = lambda s: s
    bwd = lambda s: jnp.where(s < ncc, ncc - 1 - s, nc - 1 - s + ncc)
    tok = lambda f: pl.BlockSpec((b, ML_CHUNK, w), lambda s: (0, f(s), 0))
    gcs = lambda f: pl.BlockSpec((b, ML_CHUNK, LANES), lambda s: (0, f(s), 0))
    grs = lambda f: pl.BlockSpec((b, 1, ML_CHUNK, LANES), lambda s: (0, f(s), 0, 0))
    vts = lambda f: pl.BlockSpec((b, 1, ML_HEADS * VT_ROWS, ML_CHUNK), lambda s: (0, f(s), 0, 0))
    n_chain = b * 2 * ML_HEADS
    return pl.pallas_call(
        _mlstm_kernel,
        grid=(nc,),
        in_specs=[tok(fwd), tok(fwd), vts(fwd), gcs(fwd), grs(fwd),
                  tok(bwd), tok(bwd), vts(bwd), gcs(bwd), grs(bwd)],
        out_specs=[tok(fwd), tok(bwd)],
        out_shape=[jax.ShapeDtypeStruct((b, t, w), F32)] * 2,
        scratch_shapes=[pltpu.VMEM((VT_ROWS, ML_DIM), F32)] * n_chain
                       + [pltpu.VMEM((1, LANES), F32)] * n_chain,
        compiler_params=_params("arbitrary"),
        name="mlstm",
    )(mq, mk, mvt, gc, gr, mq, mk, mvt, gc, gr)


def _attn_kernel(q_ref, qn_ref, k_ref, vt_ref, lam_ref, sg_ref, o_ref, qm_scr, *scr,
                 ctx_tiles, n_ctx, key_block, n_blocks, lam_init):
    s_scr = (scr[0:2], scr[2:4])
    bm_scr = (scr[4:6], scr[6:8])
    m_scr = scr[8:10]
    acc_scr = scr[10:12]
    i = pl.program_id(2)
    dv = DA_V_DIM
    q = q_ref[0]
    lane = lax.broadcasted_iota(jnp.int32, q.shape, 1)
    zero = jnp.zeros_like(q)

    def map_halves(qv):
        return jnp.where(lane < DA_QK_DIM, qv, zero), jnp.where(lane >= DA_QK_DIM, qv, zero)

    qm_scr[0], qm_scr[1] = map_halves(q)

    def scores(mp, start, size, q_maps=None):
        q_map = qm_scr[mp] if q_maps is None else q_maps[mp]
        return _dot_nt(k_ref[0, pl.ds(start, size), :], q_map)

    @pl.when(i < ctx_tiles)
    def _():
        for mp in range(2):
            sc = scores(mp, 0, n_ctx)
            m = jnp.max(sc, axis=0, keepdims=True)
            p = jnp.exp2(sc - m).astype(BF16)
            acc_scr[mp][...] = _dot(vt_ref[0, 0, :, 0:n_ctx], p)

    def stage(j, slot, q_maps=None):
        start = pl.multiple_of(j * key_block, LANES)
        for mp in range(2):
            sc = scores(mp, start, key_block, q_maps)
            s_scr[mp][slot][...] = sc
            bm_scr[mp][slot][...] = jnp.max(sc, axis=0, keepdims=True)

    def consume(j, slot):
        v_t = vt_ref[0, 0, :, pl.ds(pl.multiple_of(j * key_block, LANES), key_block)]
        for mp in range(2):
            m_old = m_scr[mp][...]
            m_new = jnp.maximum(m_old, bm_scr[mp][slot][...])
            alpha = jnp.exp2(m_old - m_new)
            p = jnp.exp2(s_scr[mp][slot][...] - m_new).astype(BF16)
            acc_scr[mp][...] = alpha * acc_scr[mp][...] + _dot(v_t, p)
            m_scr[mp][...] = m_new

    @pl.when(i >= ctx_tiles)
    def _():
        for mp in range(2):
            m_scr[mp][...] = jnp.full(m_scr[mp].shape, -jnp.inf, F32)
            acc_scr[mp][...] = jnp.zeros(acc_scr[mp].shape, F32)

        @pl.when(i == ctx_tiles)
        def _():
            stage(0, 0)

        def body(g, carry):
            stage(2 * g + 1, 1)
            consume(2 * g, 0)
            stage(2 * g + 2, 0)
            consume(2 * g + 1, 1)
            return carry
        lax.fori_loop(0, n_blocks // 2 - 1, body, 0)
        stage(n_blocks - 1, 1)
        consume(n_blocks - 2, 0)
        stage(0, 0, map_halves(qn_ref[0]))
        consume(n_blocks - 1, 1)

    lv = lam_ref[...]
    dot01 = jnp.sum(lv[0:1, :] * lv[1:2, :], axis=1, keepdims=True)
    dot23 = jnp.sum(lv[2:3, :] * lv[3:4, :], axis=1, keepdims=True)
    lam = jnp.exp(dot01) - jnp.exp(dot23) + lam_init
    a0 = acc_scr[0][...]
    a1 = acc_scr[1][...]
    o = (a0[0:dv, :] / a0[dv:dv + 1, :] - lam * (a1[0:dv, :] / a1[dv:dv + 1, :])).T
    o = o * lax.rsqrt(jnp.mean(o * o, axis=-1, keepdims=True) + NORM_EPS) * sg_ref[...]
    o_ref[0] = (o * (1.0 - lam_init)).astype(BF16)


def _diff_attention(q, k, vt, lam8, subln_g, n_ctx, lam_init):
    b, t, w = q.shape
    tq = ROW_TILE
    n_blocks = next(n for n in (10, 8, 6, 4, 2) if t % (n * LANES) == 0 and t // n >= n_ctx)
    key_block = t // n_blocks
    vt_rows = vt.shape[2]
    assert n_ctx % tq == 0
    kern = functools.partial(_attn_kernel, ctx_tiles=n_ctx // tq, n_ctx=n_ctx, key_block=key_block,
                             n_blocks=n_blocks, lam_init=lam_init)
    return pl.pallas_call(
        kern,
        grid=(b, DA_HEADS, t // tq),
        in_specs=[pl.BlockSpec((1, tq, LANES), lambda bi, h, i: (bi, i, h)),
                  pl.BlockSpec((1, tq, LANES), lambda bi, h, i: (bi, jnp.minimum(i + 1, t // tq - 1), h)),
                  pl.BlockSpec((1, t, LANES), lambda bi, h, i: (bi, 0, h)),
                  pl.BlockSpec((1, 1, vt_rows, t), lambda bi, h, i: (bi, h, 0, 0)),
                  pl.BlockSpec(lam8.shape, lambda bi, h, i: (0, 0)),
                  pl.BlockSpec(subln_g.shape, lambda bi, h, i: (0, 0))],
        out_specs=pl.BlockSpec((1, tq, LANES), lambda bi, h, i: (bi, i, h)),
        out_shape=jax.ShapeDtypeStruct((b, t, w), BF16),
        scratch_shapes=[pltpu.VMEM((2, tq, LANES), BF16)]
                       + [pltpu.VMEM((key_block, tq), F32)] * 4
                       + [pltpu.VMEM((1, tq), F32)] * 6
                       + [pltpu.VMEM((vt_rows, tq), F32)] * 2,
        compiler_params=_params("arbitrary", "arbitrary", "arbitrary"),
        name="diff_attention",
    )(q, q, k, vt, lam8, subln_g)


def _top2_sum(a, b, c, d):
    hi1, lo1 = jnp.maximum(a, b), jnp.minimum(a, b)
    hi2, lo2 = jnp.maximum(c, d), jnp.minimum(c, d)
    return jnp.maximum(hi1, hi2) + jnp.maximum(jnp.minimum(hi1, hi2), jnp.maximum(lo1, lo2))


def _route(f, rwt_ref, rb_ref, cnt_ref):
    tm = f.shape[0]
    aff = _sigmoid(_dot_nt(rwt_ref[...], f, HI))
    biased = aff + rb_ref[:, 0:1]
    bz = [biased[e:e + 1, :] for e in range(N_EXPERTS)]
    af = [aff[e:e + 1, :] for e in range(N_EXPERTS)]
    n_grp = N_EXPERTS // EXPERTS_PER_GROUP
    scores = [_top2_sum(*bz[EXPERTS_PER_GROUP * g:EXPERTS_PER_GROUP * (g + 1)]) for g in range(n_grp)]
    best = scores[0]
    sel_grp = jnp.zeros_like(best, dtype=jnp.int32)
    for g in range(1, n_grp):
        better = scores[g] > best
        sel_grp = jnp.where(better, g, sel_grp)
        best = jnp.where(better, scores[g], best)
    chosen = []
    for e in range(N_EXPERTS):
        g = e // EXPERTS_PER_GROUP
        rank = jnp.zeros_like(sel_grp)
        for o in range(EXPERTS_PER_GROUP * g, EXPERTS_PER_GROUP * (g + 1)):
            if o == e:
                continue
            beats = (bz[o] > bz[e]) if o > e else (bz[o] >= bz[e])
            rank = rank + beats.astype(jnp.int32)
        chosen.append(jnp.logical_and(sel_grp == g, rank < 2))
    denom = sum(jnp.where(chosen[e], af[e], 0.0) for e in range(N_EXPERTS))
    erow = lax.broadcasted_iota(jnp.int32, (N_EXPERTS, tm), 0)
    one_hot = jnp.zeros((N_EXPERTS, tm), F32)
    for e in range(N_EXPERTS):
        one_hot = jnp.where(jnp.logical_and(erow == e, chosen[e]), 1.0, one_hot)
    earlier = (lax.broadcasted_iota(jnp.int32, (tm, tm), 0)
               < lax.broadcasted_iota(jnp.int32, (tm, tm), 1)).astype(BF16)
    rank_all = _dot(one_hot.astype(BF16), earlier) + cnt_ref[:, 0:1]
    cnt_ref[...] = cnt_ref[...] + jnp.sum(one_hot, axis=1, keepdims=True)

    seen = jnp.zeros((1, tm), jnp.bool_)
    e_a = e_b = jnp.zeros((1, tm), jnp.int32)
    r_a = r_b = w_a = w_b = jnp.zeros((1, tm), F32)
    for e in range(N_EXPERTS):
        first = jnp.logical_and(chosen[e], jnp.logical_not(seen))
        second = jnp.logical_and(chosen[e], seen)
        rk = rank_all[e:e + 1, :]
        wt = af[e] / denom
        e_a, e_b = jnp.where(first, e, e_a), jnp.where(second, e, e_b)
        r_a, r_b = jnp.where(first, rk, r_a), jnp.where(second, rk, r_b)
        w_a, w_b = jnp.where(first, wt, w_a), jnp.where(second, wt, w_b)
        seen = jnp.logical_or(seen, chosen[e])
    r8 = lax.broadcasted_iota(jnp.int32, (8, tm), 0)
    sel = jnp.where(r8 == 0, e_a, jnp.where(r8 == 1, e_b, jnp.where(
        r8 == 2, r_a.astype(jnp.int32), jnp.where(r8 == 3, r_b.astype(jnp.int32), 0))))
    row = lax.broadcasted_iota(jnp.int32, (LANES, tm), 0)
    w_t = jnp.where(row == 0, w_a, jnp.where(row == 1, w_b, 0.0))
    return sel, w_t.T


def _post_kernel(*refs, even, ctx_tiles):
    if even:
        (c_ref, x_ref, a_ref, hf_ref, hb_ref, og_ref, mod_ref, wa_ref, wm_ref, gf_ref, rwt_ref, rb_ref,
         xo_ref, f_ref, sel_ref, w_ref, cnt_ref, cnt_scr) = refs
        m = ((hf_ref[0] + hb_ref[0]) * og_ref[0].astype(F32)).astype(BF16)
        o = _dot(a_ref[0], wa_ref[...]) + _dot(m, wm_ref[...])
        x_in = jnp.where(pl.program_id(1) < ctx_tiles, c_ref[0], x_ref[0])
    else:
        (x_ref, a_ref, mod_ref, wa_ref, gf_ref, rwt_ref, rb_ref,
         xo_ref, f_ref, sel_ref, w_ref, cnt_ref, cnt_scr) = refs
        o = _dot(a_ref[0].astype(BF16), wa_ref[...])
        x_in = x_ref[0]

    @pl.when(jnp.logical_and(pl.program_id(0) == 0, pl.program_id(1) == 0))
    def _():
        cnt_scr[...] = jnp.zeros(cnt_scr.shape, F32)

    x = x_in + mod_ref[0, 2:3, :] * o
    xo_ref[0] = x
    f = _norm_mod(x, gf_ref[...], mod_ref[0, 3:4, :], mod_ref[0, 4:5, :])
    _store_row_tiles(f_ref.at[0], f)
    sel, w_col = _route(f, rwt_ref, rb_ref, cnt_scr)
    sel_ref[0, 0] = sel
    w_ref[0] = w_col
    cnt_ref[...] = cnt_scr[...]


def _post_mixer(x, acts, weights, modtab, gffn, rwt, rb, n_ctx, x_row_off, even, x_ctx=None):
    b, t, _ = acts[0].shape
    d = x.shape[2]
    tm = ROW_TILE
    nt = t // tm
    ctx_tiles = n_ctx // tm
    off = x_row_off // tm
    full = lambda a: pl.BlockSpec(a.shape, lambda bi, i: (0,) * a.ndim)
    row = lambda w: pl.BlockSpec((1, tm, w), lambda bi, i: (bi, i, 0))
    mod_spec = pl.BlockSpec(
        (1, 8, d), lambda bi, i: (2 * bi + (i + off >= ctx_tiles).astype(jnp.int32), 0, 0))
    if x_ctx is None:
        residual, res_specs = [x], [pl.BlockSpec((1, tm, d), lambda bi, i: (bi, i + off, 0))]
    else:
        assert even and x_row_off == 0 and ctx_tiles >= 1
        residual = [x_ctx, x]
        res_specs = [pl.BlockSpec((1, tm, d), lambda bi, i: (bi, jnp.minimum(i, ctx_tiles - 1), 0)),
                     pl.BlockSpec((1, tm, d), lambda bi, i: (bi, jnp.maximum(i - ctx_tiles, 0), 0))]
    in_specs = (res_specs + [row(a.shape[2]) for a in acts] + [mod_spec]
                + [full(w) for w in weights] + [full(gffn), full(rwt), full(rb)])
    return pl.pallas_call(
        functools.partial(_post_kernel, even=even, ctx_tiles=ctx_tiles),
        grid=(b, nt),
        in_specs=in_specs,
        out_specs=[row(d), pl.BlockSpec((1, tm * d // LANES, LANES), lambda bi, i: (bi, i, 0)),
                   pl.BlockSpec((1, 1, 8, tm), lambda bi, i: (bi, i, 0, 0)),
                   row(LANES),
                   pl.BlockSpec((N_EXPERTS, LANES), lambda bi, i: (0, 0))],
        out_shape=[jax.ShapeDtypeStruct((b, t, d), F32),
                   jax.ShapeDtypeStruct((b, t * d // LANES, LANES), F32),
                   jax.ShapeDtypeStruct((b, nt, 8, tm), jnp.int32),
                   jax.ShapeDtypeStruct((b, t, LANES), F32),
                   jax.ShapeDtypeStruct((N_EXPERTS, LANES), F32)],
        scratch_shapes=[pltpu.VMEM((N_EXPERTS, LANES), F32)],
        compiler_params=_params("arbitrary", "arbitrary"),
        name="post_mixer_even" if even else "post_mixer_odd",
    )(*residual, *acts, modtab, *weights, gffn, rwt, rb)


EXPERT_ROW_TILE = 512
DMA_ISSUE_UNROLL = 8


def _moe_plan(sel, counts, n_tok):
    e_a, e_b, r_a, r_b = (sel[:, :, k, :].reshape(-1) for k in range(4))
    cnt = counts[:, 0].astype(jnp.int32)
    padded = ((cnt + EXPERT_ROW_TILE - 1) // EXPERT_ROW_TILE) * EXPERT_ROW_TILE
    ends = jnp.cumsum(padded)
    starts = ends - padded
    pos = jnp.concatenate([starts[e_a] + r_a, starts[e_b] + r_b]).astype(jnp.int32)
    n_tiles = 2 * n_tok // EXPERT_ROW_TILE + N_EXPERTS
    tile_start = jnp.arange(n_tiles, dtype=jnp.int32) * EXPERT_ROW_TILE
    tile_expert = jnp.minimum(jnp.sum(tile_start[:, None] >= ends[None, :], axis=1),
                              N_EXPERTS - 1).astype(jnp.int32)
    tiles_used = (ends[-1:] // EXPERT_ROW_TILE).astype(jnp.int32)
    return pos, tile_expert, tiles_used, n_tiles


def _store_row_tiles(ref, x):
    rows, d = x.shape
    n_sub = d // LANES
    for s in range(n_sub):
        ref[pl.ds(s, rows, stride=n_sub), :] = x[:, s * LANES:(s + 1) * LANES]


def _load_row_tiles(ref, rows):
    n_sub = ref.shape[0] // rows
    return jnp.concatenate([ref[pl.ds(s, rows, stride=n_sub), :] for s in range(n_sub)], axis=1)


def _dispatch_kernel(pos_ref, f_ref, init_ref, out_ref, sem, *, n_tok, tm):
    del init_ref
    n_sub = f_ref.shape[1] // tm
    base = (pl.program_id(0) * pl.num_programs(1) + pl.program_id(1)) * tm

    def row_copy(r, k):
        dst = pos_ref[k * n_tok + base + r]
        src = f_ref.at[0, pl.ds(pl.multiple_of(r * n_sub, n_sub), n_sub), :]
        return pltpu.make_async_copy(src, out_ref.at[dst], sem)

    def issue(r, carry):
        row_copy(r, 0).start()
        row_copy(r, 1).start(priority=1)
        return carry

    lax.fori_loop(0, tm, issue, 0, unroll=DMA_ISSUE_UNROLL)
    for _ in range(2):
        pltpu.make_async_copy(out_ref.at[pl.ds(0, tm)], out_ref.at[pl.ds(0, tm)], sem).wait()


def _dispatch(pos, f, n_rows, n_tok):
    b, rows, _ = f.shape
    n_sub = rows * b // n_tok
    tm = next(m * ROW_TILE for m in (8, 5, 4, 2, 1) if (n_tok // b) % (m * ROW_TILE) == 0)
    grid_spec = pltpu.PrefetchScalarGridSpec(
        num_scalar_prefetch=1,
        grid=(b, n_tok // b // tm),
        in_specs=[pl.BlockSpec((1, tm * n_sub, LANES), lambda bi, i, pos_ref: (bi, i, 0)),
                  pl.BlockSpec(memory_space=pl.ANY)],
        out_specs=pl.BlockSpec(memory_space=pl.ANY),
        scratch_shapes=[pltpu.SemaphoreType.DMA(())])
    return pl.pallas_call(
        functools.partial(_dispatch_kernel, n_tok=n_tok, tm=tm),
        grid_spec=grid_spec,
        out_shape=jax.ShapeDtypeStruct((n_rows, n_sub, LANES), F32),
        input_output_aliases={2: 0},
        compiler_params=_params("arbitrary", "arbitrary"),
        name="moe_dispatch",
    )(pos, f, jnp.zeros((n_rows, n_sub, LANES), F32))


def _expert_ffn_kernel(te_ref, used_ref, x_ref, wg_ref, wu_ref, wd_ref, y_ref, wg_scr, wu_scr, wd_scr):
    j = pl.program_id(0)
    live = j < used_ref[0]

    @pl.when(jnp.logical_or(j == 0, te_ref[j] != te_ref[jnp.maximum(j - 1, 0)]))
    def _():
        wg_scr[...] = wg_ref[0, 0].astype(BF16)
        wu_scr[...] = wu_ref[0, 0].astype(BF16)
        wd_scr[...] = wd_ref[0, 0].astype(BF16)

    @pl.when(live)
    def _():
        xb = _load_row_tiles(x_ref, EXPERT_ROW_TILE).astype(BF16)
        he = _silu(_dot(xb, wg_scr[...])) * _dot(xb, wu_scr[...])
        _store_row_tiles(y_ref, _dot(he.astype(BF16), wd_scr[...]))

    @pl.when(jnp.logical_not(live))
    def _():
        y_ref[...] = jnp.zeros(y_ref.shape, F32)


def _expert_ffn(tile_expert, tiles_used, xs, wg, wu, wd, layer, n_tiles):
    n_rows, n_sub, _ = xs.shape
    _, _, d, d_e = wg.shape
    tr = EXPERT_ROW_TILE
    grid_spec = pltpu.PrefetchScalarGridSpec(
        num_scalar_prefetch=2,
        grid=(n_tiles,),
        in_specs=[pl.BlockSpec((tr * n_sub, LANES), lambda j, te, used: (j, 0)),
                  pl.BlockSpec((1, 1, d, d_e), lambda j, te, used: (layer, te[j], 0, 0)),
                  pl.BlockSpec((1, 1, d, d_e), lambda j, te, used: (layer, te[j], 0, 0)),
                  pl.BlockSpec((1, 1, d_e, d), lambda j, te, used: (layer, te[j], 0, 0))],
        out_specs=pl.BlockSpec((tr * n_sub, LANES), lambda j, te, used: (j, 0)),
        scratch_shapes=[pltpu.VMEM((d, d_e), BF16), pltpu.VMEM((d, d_e), BF16),
                        pltpu.VMEM((d_e, d), BF16)])
    ys = pl.pallas_call(
        _expert_ffn_kernel,
        grid_spec=grid_spec,
        out_shape=jax.ShapeDtypeStruct((n_rows * n_sub, LANES), F32),
        compiler_params=_params("arbitrary"),
        name="moe_expert_ffn",
    )(tile_expert, tiles_used, xs.reshape(n_rows * n_sub, LANES), wg, wu, wd)
    return ys.reshape(n_rows, n_sub, LANES)


def _combine_kernel(pos_ref, y_ref, w_ref, x_ref, modc_ref, modl_ref, fg_ref, o_ref, buf, sem,
                    *, n_tok, n_ctx, final_norm):
    i = pl.program_id(1)
    tm = x_ref.shape[1]
    step = pl.program_id(0) * pl.num_programs(1) + i
    n_steps = pl.num_programs(0) * pl.num_programs(1)
    n_sub = y_ref.shape[1]

    def gather(tile, slot):
        base = tile * tm

        def issue(r, carry):
            for k in range(2):
                src = pos_ref[k * n_tok + base + r]
                dst = buf.at[slot, k, pl.ds(pl.multiple_of(r * n_sub, n_sub), n_sub), :]
                pltpu.make_async_copy(y_ref.at[src], dst, sem.at[slot]).start(priority=k)
            return carry

        lax.fori_loop(0, tm, issue, 0, unroll=DMA_ISSUE_UNROLL)

    slot = lax.rem(step, 2)

    @pl.when(step == 0)
    def _():
        gather(0, 0)

    @pl.when(step + 1 < n_steps)
    def _():
        gather(step + 1, 1 - slot)

    for _ in range(2):
        pltpu.make_async_copy(y_ref.at[pl.ds(0, tm)], y_ref.at[pl.ds(0, tm)], sem.at[slot]).wait()
    w = w_ref[0]
    y = (_load_row_tiles(buf.at[slot, 0], tm) * w[:, 0:1]
         + _load_row_tiles(buf.at[slot, 1], tm) * w[:, 1:2])
    rows = i * tm + lax.broadcasted_iota(jnp.int32, (tm, 1), 0)
    gate = jnp.where(rows < n_ctx, modc_ref[0, 5:6, :], modl_ref[0, 5:6, :])
    out = x_ref[0] + gate * y
    if final_norm:
        out = out * lax.rsqrt(jnp.mean(out * out, axis=-1, keepdims=True) + NORM_EPS) * fg_ref[...]
    o_ref[0] = out


def _combine(pos, ys, w_col, xmid, modtab, final_g, n_ctx, final_norm):
    b, t, d = xmid.shape
    tm = ROW_TILE
    row = lambda w: pl.BlockSpec((1, tm, w), lambda bi, i, pos_ref: (bi, i, 0))
    grid_spec = pltpu.PrefetchScalarGridSpec(
        num_scalar_prefetch=1,
        grid=(b, t // tm),
        in_specs=[pl.BlockSpec(memory_space=pl.ANY), row(LANES), row(d),
                  pl.BlockSpec((1, 8, d), lambda bi, i, pos_ref: (2 * bi, 0, 0)),
                  pl.BlockSpec((1, 8, d), lambda bi, i, pos_ref: (2 * bi + 1, 0, 0)),
                  pl.BlockSpec(final_g.shape, lambda bi, i, pos_ref: (0, 0))],
        out_specs=row(d),
        scratch_shapes=[pltpu.VMEM((2, 2, tm * d // LANES, LANES), F32), pltpu.SemaphoreType.DMA((2,))])
    return pl.pallas_call(
        functools.partial(_combine_kernel, n_tok=b * t, n_ctx=n_ctx, final_norm=final_norm),
        grid_spec=grid_spec,
        out_shape=jax.ShapeDtypeStruct((b, t, d), F32),
        compiler_params=_params("arbitrary", "arbitrary"),
        name="moe_combine",
    )(pos, ys, w_col, xmid, modtab, modtab, final_g)


def _moe(f, sel, w_col, counts, wg, wu, wd, layer, xmid, modtab, final_g, n_ctx, final_norm):
    b, t, _ = xmid.shape
    pos, tile_expert, tiles_used, n_tiles = _moe_plan(sel, counts, b * t)
    xs = _dispatch(pos, f, n_tiles * EXPERT_ROW_TILE, b * t)
    ys = _expert_ffn(tile_expert, tiles_used, xs, wg, wu, wd, layer, n_tiles)
    return _combine(pos, ys, w_col, xmid, modtab, final_g, n_ctx, final_norm)


def _chan_dft_kernel(x_ref, mod_ref, g_ref, w_ref, zr_ref, zi_ref):
    h = _norm_mod(x_ref[0], g_ref[...], mod_ref[0, 0:1, :], mod_ref[0, 1:2, :])
    gd = w_ref.shape[0]
    for gi in range(h.shape[1] // gd):
        z = _dot_split(h[:, gi * gd:(gi + 1) * gd], w_ref[...])
        zr_ref[0, :, gi * gd:(gi + 1) * gd] = z[:, :gd]
        zi_ref[0, :, gi * gd:(gi + 1) * gd] = z[:, gd:]


def _chan_dft(x, modtab, g, w_cs, x_row_off, t):
    b, _, d = x.shape
    tm = ROW_TILE
    off = x_row_off // tm
    row = pl.BlockSpec((1, tm, d), lambda bi, i: (bi, i, 0))
    return pl.pallas_call(
        _chan_dft_kernel,
        grid=(b, t // tm),
        in_specs=[pl.BlockSpec((1, tm, d), lambda bi, i: (bi, i + off, 0)),
                  pl.BlockSpec((1, 8, d), lambda bi, i: (2 * bi + 1, 0, 0)),
                  pl.BlockSpec(g.shape, lambda bi, i: (0, 0)),
                  pl.BlockSpec(w_cs.shape, lambda bi, i: (0, 0))],
        out_specs=[row, row],
        out_shape=[jax.ShapeDtypeStruct((b, t, d), F32)] * 2,
        compiler_params=_params("parallel", "parallel"),
        name="chan_dft",
    )(x, modtab, g, w_cs)


DFT_SUB = 8


def _dft1_kernel(zr_ref, zi_ref, w_ref, yr_ref, yi_ref):
    _, n1, sub, cols = zr_ref.shape
    z = jnp.concatenate([jnp.concatenate([ref[0, :, j, :] for j in range(sub)], axis=1)
                         for ref in (zr_ref, zi_ref)], axis=0)
    y = _dot_split(w_ref[...], z)
    for j in range(sub):
        yr_ref[0, :, j, :] = y[:n1, j * cols:(j + 1) * cols]
        yi_ref[0, :, j, :] = y[n1:, j * cols:(j + 1) * cols]


def _dft1(zr, zi, w1):
    b, n1, n2, d = zr.shape
    cols = d // 2
    blk = pl.BlockSpec((1, n1, DFT_SUB, cols), lambda bi, j, c: (bi, 0, j, c))
    return pl.pallas_call(
        _dft1_kernel,
        grid=(b, n2 // DFT_SUB, d // cols),
        in_specs=[blk, blk, pl.BlockSpec(w1.shape, lambda bi, j, c: (0, 0))],
        out_specs=[blk, blk],
        out_shape=[jax.ShapeDtypeStruct(zr.shape, F32)] * 2,
        compiler_params=_params("parallel", "parallel", "parallel"),
        name="dft_stage1",
    )(zr, zi, w1)


def _dft2_kernel(yr_ref, yi_ref, tab_ref, o_ref):
    for j in range(yr_ref.shape[1]):
        y = jnp.concatenate([yr_ref[0, j], yi_ref[0, j]], axis=0)
        o_ref[0, :, j, :] = _dot_split(tab_ref[j], y)


def _dft2(yr, yi, tab):
    b, n1, n2, d = yr.shape
    blk = pl.BlockSpec((1, DFT_SUB, n2, d), lambda bi, k1: (bi, k1, 0, 0))
    return pl.pallas_call(
        _dft2_kernel,
        grid=(b, n1 // DFT_SUB),
        in_specs=[blk, blk, pl.BlockSpec((DFT_SUB, n2, 2 * n2), lambda bi, k1: (k1, 0, 0))],
        out_specs=pl.BlockSpec((1, n2, DFT_SUB, d), lambda bi, k1: (bi, 0, k1, 0)),
        out_shape=jax.ShapeDtypeStruct((b, n2, n1, d), F32),
        compiler_params=_params("parallel", "parallel"),
        name="dft_stage2",
    )(yr, yi, tab)


def _dft_tables(t, gd):
    n2 = ML_CHUNK
    n1 = t // n2
    def cs(num, den):
        ang = (2.0 * np.pi / den) * (num % den).astype(np.float64)
        return np.cos(ang), np.sin(ang)
    c = np.arange(gd)
    cc, sc = cs(np.outer(c, c), gd)
    w_cs = np.concatenate([cc, -sc], axis=1)
    a = np.arange(n1)
    c1, s1 = cs(np.outer(a, a), n1)
    w1 = np.block([[c1, s1], [-s1, c1]])
    k = a[:, None, None] + n1 * np.arange(n2)[None, :, None]
    c2, s2 = cs(k * np.arange(n2)[None, None, :], t)
    tab = np.concatenate([c2, s2], axis=2) / math.sqrt(t * gd)
    return (jnp.asarray(w_cs, F32), jnp.asarray(w1, F32), jnp.asarray(tab, F32))


def _rope_tables(n_ctx, n_lat):
    pos = jnp.arange(n_lat, dtype=jnp.int32)
    n_axis = DA_QK_DIM // 4
    inv = ROPE_BASE ** (-jnp.arange(n_axis, dtype=F32) / n_axis)
    ang = jnp.concatenate([(pos // GRID_W).astype(F32)[:, None] * inv,
                           (pos % GRID_W).astype(F32)[:, None] * inv], axis=-1)
    cos, sin = jnp.cos(ang), jnp.sin(ang)
    cos = jnp.concatenate([jnp.ones((n_ctx, 2 * n_axis), F32), cos], axis=0)
    sin = jnp.concatenate([jnp.zeros((n_ctx, 2 * n_axis), F32), sin], axis=0)
    cos128 = jnp.concatenate([cos, cos, cos, cos], axis=1)
    sin128 = jnp.concatenate([-sin, sin, -sin, sin], axis=1)
    return cos128, sin128


def _deinterleave(w):
    d, n = w.shape
    w = w.reshape(d, n // DA_QK_DIM, DA_QK_DIM // 2, 2)
    return jnp.concatenate([w[..., 0], w[..., 1]], axis=-1).reshape(d, n)


def _pad_rows(a, rows):
    return jnp.concatenate([a, jnp.zeros((rows - a.shape[0],) + a.shape[1:], a.dtype)], axis=0)


def _pad_cols(a, cols):
    return jnp.concatenate([a, jnp.zeros(a.shape[:-1] + (cols - a.shape[-1],), a.dtype)], axis=-1)


def kernel(x, c, ctx, c_ctx, ada_w, ada_b, norm_mix_g, norm_ffn_g, even_w_in, even_w_out,
           even_conv_w, even_gate_b, even_lam, even_subln_g, odd_w_fnet, router_w, router_b,
           exp_w_gate, exp_w_up, exp_w_down, final_g):
    b, n_lat, d = x.shape
    n_ctx = ctx.shape[1]
    depth = ada_w.shape[0]
    assert depth == 2 and b + 1 <= 8
    assert n_ctx % ROW_TILE == 0 and n_lat % ROW_TILE == 0

    cond8 = _pad_rows(jnp.concatenate([c_ctx[None, :], c], axis=0), 8)
    rwt = router_w.T
    rb = jnp.broadcast_to(router_b[:, None], (N_EXPERTS, LANES))
    row2 = lambda v: v.reshape(1, -1)

    def modtab_for(layer):
        mods = _ada_mods(cond8, ada_w[layer], ada_b[layer]).reshape(8, 6, d)
        mods = jnp.concatenate([mods, jnp.zeros((8, 2, d), F32)], axis=1)
        idx = np.array([[0, 1 + bi] for bi in range(b)]).reshape(-1)
        return mods[idx]

    modtab = modtab_for(0)
    w_in = even_w_in[0]
    o1 = DA_HEADS * 2 * DA_QK_DIM
    o2 = 2 * o1
    o3 = o2 + DA_HEADS * DA_V_DIM
    o4 = o3 + 2 * ML_HEADS * ML_DIM
    o5 = o4 + ML_HEADS * ML_DIM
    o6 = o5 + ML_HEADS * ML_DIM
    ws = [(_deinterleave(w_in[:, :o1]) * (DA_QK_DIM ** -0.5)).astype(BF16),
          _deinterleave(w_in[:, o1:o2]).astype(BF16),
          w_in[:, o2:o3].astype(BF16), w_in[:, o3:o4].astype(BF16),
          w_in[:, o4:o5].astype(BF16), w_in[:, o5:o6].astype(BF16),
          _pad_cols(w_in[:, o6:], LANES).astype(BF16)]
    gate_b = _pad_cols(even_gate_b[0].reshape(1, -1), LANES)
    cos128, sin128 = _rope_tables(n_ctx, n_lat)
    daq, dak, dav, mqk, mv, og, gates = _inproj(ctx, x, modtab, row2(norm_mix_g[0]), cos128, sin128,
                                                 ws, gate_b)
    mq, mk, gc, gr, mvt = _mlprep(mqk, _pad_rows(even_conv_w[0], 8), gates, mv, n_ctx)
    hf, hb = _mlstm(mq, mk, mvt, gc, gr, n_ctx)
    lam_init = 0.8 - 0.6 * math.exp(-0.3 * 0)
    lam8 = _pad_rows(even_lam[0], 8)
    att = _diff_attention(daq, dak, dav, lam8, row2(even_subln_g[0]), n_ctx, lam_init)
    w_out = even_w_out[0].astype(BF16)
    half = DA_HEADS * DA_V_DIM
    xmid, f, sel, w_col, counts = _post_mixer(x, [att, hf, hb, og], [w_out[:half], w_out[half:]],
                                              modtab, row2(norm_ffn_g[0]), rwt, rb, n_ctx, 0, True,
                                              x_ctx=ctx)
    xs = _moe(f, sel, w_col, counts, exp_w_gate, exp_w_up, exp_w_down, 0,
              xmid, modtab, row2(final_g), n_ctx, False)

    modtab = modtab_for(1)
    gd = d // FN_GROUPS
    w_cs, w1, tab = _dft_tables(n_lat, gd)
    n2 = ML_CHUNK
    n1 = n_lat // n2
    zr, zi = _chan_dft(xs, modtab, row2(norm_mix_g[1]), w_cs, n_ctx, n_lat)
    yr, yi = _dft1(zr.reshape(b, n1, n2, d), zi.reshape(b, n1, n2, d), w1)
    fo = _dft2(yr, yi, tab).reshape(b, n_lat, d)
    xmid, f, sel, w_col, counts = _post_mixer(xs, [fo], [odd_w_fnet[0].astype(BF16)], modtab,
                                              row2(norm_ffn_g[1]), rwt, rb, 0, n_ctx, False)
    return _moe(f, sel, w_col, counts, exp_w_gate, exp_w_up, exp_w_down, 1,
                xmid, modtab, row2(final_g), 0, True)
```

```python
import functools
import math

import jax
import jax.numpy as jnp
import numpy as np
from jax import lax
from jax.experimental import pallas as pl
from jax.experimental.pallas import tpu as pltpu

F32 = jnp.float32
BF16 = jnp.bfloat16

NORM_EPS = 1e-6
GRID_W = 64
DA_HEADS = 4
DA_QK_DIM = 64
DA_V_DIM = 128
ML_HEADS = 4
ML_DIM = 128
ML_CHUNK = 128
FN_GROUPS = 4
N_EXPERTS = 16
EXPERTS_PER_GROUP = 4
ROPE_BASE = 10000.0
LANES = 128
ROW_TILE = 256
VMEM_LIMIT_BYTES = 56 * 1024 * 1024
HI = lax.Precision.HIGHEST
LOG2_E = math.log2(math.e)


def _params(*sem):
    return pltpu.CompilerParams(dimension_semantics=sem, vmem_limit_bytes=VMEM_LIMIT_BYTES)


def _dot(a, b, precision=None):
    return jnp.dot(a, b, preferred_element_type=F32, precision=precision)


def _dot_nt(a, b, precision=None):
    return lax.dot_general(a, b, (((1,), (1,)), ((), ())), preferred_element_type=F32,
                           precision=precision)


def _dot_split(a, b):
    a_hi = a.astype(BF16)
    b_hi = b.astype(BF16)
    a_lo = (a - a_hi.astype(F32)).astype(BF16)
    b_lo = (b - b_hi.astype(F32)).astype(BF16)
    return _dot(a_hi, b_hi) + (_dot(a_hi, b_lo) + _dot(a_lo, b_hi))


def _sigmoid(x):
    return 1.0 / (1.0 + jnp.exp(-x))


def _silu(x):
    return x * _sigmoid(x)


def _norm_mod(x, g, shift, scale):
    y = x * lax.rsqrt(jnp.mean(x * x, axis=-1, keepdims=True) + NORM_EPS) * g
    return y * (1.0 + scale) + shift


def _ada_kernel(c_ref, w_ref, b_ref, o_ref):
    o_ref[...] = _dot(_silu(c_ref[...]), w_ref[...], HI) + b_ref[...]


def _ada_mods(cond8, w, b):
    d, n = w.shape
    tn = n // 6
    return pl.pallas_call(
        _ada_kernel,
        grid=(6,),
        in_specs=[pl.BlockSpec((8, d), lambda j: (0, 0)),
                  pl.BlockSpec((d, tn), lambda j: (0, j)),
                  pl.BlockSpec((1, tn), lambda j: (0, j))],
        out_specs=pl.BlockSpec((8, tn), lambda j: (0, j)),
        out_shape=jax.ShapeDtypeStruct((8, n), F32),
        compiler_params=_params("arbitrary"),
        name="ada_mods",
    )(cond8, w, b.reshape(1, n))


def _inproj_kernel(c_ref, x_ref, mod_ref, g_ref, cos_ref, sin_ref, wq_ref, wk_ref, wv_ref, wmqk_ref,
                   wmv_ref, wmo_ref, wg_ref, gb_ref,
                   q_ref, k_ref, v_ref, mqk_ref, mv_ref, og_ref, gate_ref, *, ctx_tiles):
    x = jnp.where(pl.program_id(1) < ctx_tiles, c_ref[0], x_ref[0])
    h = _norm_mod(x, g_ref[...], mod_ref[0, 0:1, :], mod_ref[0, 1:2, :]).astype(BF16)
    tm = x.shape[0]
    width = q_ref.shape[2]
    cos = jnp.concatenate([cos_ref[...]] * (width // LANES), axis=1)
    sin = jnp.concatenate([sin_ref[...]] * (width // LANES), axis=1)
    lane = lax.broadcasted_iota(jnp.int32, (tm, width), 1)
    lower = (lane & (DA_QK_DIM - 1)) < (DA_QK_DIM // 2)

    def rope(u):
        swapped = jnp.where(lower, pltpu.roll(u, width - DA_QK_DIM // 2, 1),
                            pltpu.roll(u, DA_QK_DIM // 2, 1))
        return u * cos + swapped * sin

    q_ref[0] = (rope(_dot(h, wq_ref[...])) * LOG2_E).astype(BF16)
    k_ref[0] = rope(_dot(h, wk_ref[...])).astype(BF16)
    v = _dot(h, wv_ref[...])
    pad = VT_ROWS - DA_V_DIM
    ones_row = (lax.broadcasted_iota(jnp.int32, (pad, tm), 0) == 0).astype(BF16)
    for hd in range(DA_HEADS):
        v_ref[0, hd, 0:DA_V_DIM, :] = v[:, hd * DA_V_DIM:(hd + 1) * DA_V_DIM].T.astype(BF16)
        v_ref[0, hd, DA_V_DIM:VT_ROWS, :] = ones_row
    mqk_ref[0] = _dot(h, wmqk_ref[...]).astype(BF16)
    mv_ref[0] = _dot(h, wmv_ref[...]).astype(BF16)
    og_ref[0] = _sigmoid(_dot(h, wmo_ref[...])).astype(BF16)
    g = _dot(h, wg_ref[...]) + gb_ref[...]
    glane = lax.broadcasted_iota(jnp.int32, g.shape, 1)
    is_forget = ((glane // ML_HEADS) & 1) == 1
    log_sig = jnp.minimum(g, 0.0) - jnp.log(1.0 + jnp.exp(-jnp.abs(g)))
    gate_ref[0] = jnp.where(is_forget, log_sig, g)


def _inproj(ctx, x, modtab, g, cos, sin, ws, gate_b):
    b, n_lat, d = x.shape
    n_ctx = ctx.shape[1]
    t = n_ctx + n_lat
    tm = ROW_TILE
    nt = t // tm
    ctx_tiles = n_ctx // tm
    assert ctx_tiles >= 1
    row = lambda w: pl.BlockSpec((1, tm, w), lambda bi, i: (bi, i, 0))
    full = lambda a: pl.BlockSpec(a.shape, lambda bi, i: (0,) * a.ndim)
    widths = [w.shape[1] for w in ws]
    out_dtypes = [BF16] * 6 + [F32]
    out_specs = [row(w) for w in widths]
    out_shape = [jax.ShapeDtypeStruct((b, t, w), dt) for w, dt in zip(widths, out_dtypes)]
    out_specs[2] = pl.BlockSpec((1, DA_HEADS, VT_ROWS, tm), lambda bi, i: (bi, 0, 0, i))
    out_shape[2] = jax.ShapeDtypeStruct((b, DA_HEADS, VT_ROWS, t), BF16)
    return pl.pallas_call(
        functools.partial(_inproj_kernel, ctx_tiles=ctx_tiles),
        grid=(b, nt),
        in_specs=[pl.BlockSpec((1, tm, d), lambda bi, i: (bi, jnp.minimum(i, ctx_tiles - 1), 0)),
                  pl.BlockSpec((1, tm, d), lambda bi, i: (bi, jnp.maximum(i - ctx_tiles, 0), 0)),
                  pl.BlockSpec((1, 8, d), lambda bi, i: (2 * bi + (i >= ctx_tiles).astype(jnp.int32), 0, 0)),
                  full(g),
                  pl.BlockSpec((tm, LANES), lambda bi, i: (i, 0)),
                  pl.BlockSpec((tm, LANES), lambda bi, i: (i, 0))]
                 + [full(w) for w in ws] + [full(gate_b)],
        out_specs=out_specs,
        out_shape=out_shape,
        compiler_params=_params("parallel", "parallel"),
        name="inproj",
    )(ctx, x, modtab, g, cos, sin, *ws, gate_b)


def _split3(x):
    x1 = x.astype(BF16)
    r1 = x - x1.astype(F32)
    x2 = r1.astype(BF16)
    x3 = (r1 - x2.astype(F32)).astype(BF16)
    return x1, x2, x3


VT_ROWS = ML_DIM + 16


def _mlprep_kernel(cur_ref, prev_ref, next_ref, cw_ref, gate_ref, v_ref,
                   mq_ref, mk_ref, gc_ref, gr_ref, vt_ref, *, ctx_tiles, n_tiles):
    i = pl.program_id(1)
    cur = cur_ref[0].astype(F32)
    tm, w = cur.shape
    prev_ok = i != ctx_tiles
    if ctx_tiles > 0:
        prev_ok = jnp.logical_and(prev_ok, i != 0)
        next_ok = jnp.logical_and(i != ctx_tiles - 1, i != n_tiles - 1)
    else:
        next_ok = i != n_tiles - 1
    prev_row = jnp.where(prev_ok, prev_ref[0, 7:8, :].astype(F32), 0.0)
    next_row = jnp.where(next_ok, next_ref[0, 0:1, :].astype(F32), 0.0)
    ridx = lax.broadcasted_iota(jnp.int32, (tm, w), 0)
    before = jnp.where(ridx == 0, prev_row, pltpu.roll(cur, 1, 0))
    after = jnp.where(ridx == tm - 1, next_row, pltpu.roll(cur, tm - 1, 0))
    y = _silu(before * cw_ref[0:1, :] + cur * cw_ref[1:2, :] + after * cw_ref[2:3, :])
    half = w // 2
    mq_ref[0] = y[:, :half].astype(BF16)
    mk_ref[0] = (y[:, half:] * (ML_DIM ** -0.5)).astype(BF16)

    n_g = 4 * ML_HEADS
    hds = ML_HEADS
    r = lax.broadcasted_iota(jnp.int32, (ML_CHUNK, ML_CHUNK), 0)
    c = lax.broadcasted_iota(jnp.int32, (ML_CHUNK, ML_CHUNK), 1)
    lower = (c <= r).astype(BF16)
    upper = (c >= r).astype(BF16)
    ones_row = (r == 0).astype(BF16)[0:VT_ROWS - ML_DIM]
    for ci in range(tm // ML_CHUNK):
        rows = slice(ci * ML_CHUNK, (ci + 1) * ML_CHUNK)
        gm = jnp.where(c < n_g, gate_ref[0, rows, :], 0.0)
        pre = sum(_dot(lower, p) for p in _split3(pltpu.roll(gm, n_g, 1)))
        suf = sum(_dot(upper, p) for p in _split3(pltpu.roll(gm, 2 * n_g, 1)))
        col = gm + pre + suf
        u_f = pltpu.roll(col, 3 * n_g, 1) - pltpu.roll(col, 3 * n_g - (n_g + hds), 1)
        u_b = pltpu.roll(col, 3 * n_g - hds, 1) - pltpu.roll(col, 2 * hds, 1)
        col = col + jnp.where(jnp.logical_and(c >= 3 * n_g, c < 3 * n_g + hds), u_f,
                              jnp.where(jnp.logical_and(c >= 3 * n_g + hds, c < 3 * n_g + 2 * hds),
                                        u_b, 0.0))
        gc_ref[0, rows, :] = col
        gr_ref[0, ci] = col.T
        for hd in range(ML_HEADS):
            v_t = v_ref[0, rows, hd * ML_DIM:(hd + 1) * ML_DIM].astype(F32).T.astype(BF16)
            vt_ref[0, ci, hd * VT_ROWS:hd * VT_ROWS + ML_DIM, :] = v_t
            vt_ref[0, ci, hd * VT_ROWS + ML_DIM:(hd + 1) * VT_ROWS, :] = ones_row


def _mlprep(mqk, conv_w8, gates, mv, n_ctx):
    b, t, w = mqk.shape
    tm = ROW_TILE
    nt = t // tm
    sub = tm // 8
    nsub = t // 8
    cpt = tm // ML_CHUNK
    kern = functools.partial(_mlprep_kernel, ctx_tiles=n_ctx // tm, n_tiles=nt)
    return pl.pallas_call(
        kern,
        grid=(b, nt),
        in_specs=[pl.BlockSpec((1, tm, w), lambda bi, i: (bi, i, 0)),
                  pl.BlockSpec((1, 8, w), lambda bi, i: (bi, jnp.maximum(i * sub - 1, 0), 0)),
                  pl.BlockSpec((1, 8, w), lambda bi, i: (bi, jnp.minimum((i + 1) * sub, nsub - 1), 0)),
                  pl.BlockSpec(conv_w8.shape, lambda bi, i: (0, 0)),
                  pl.BlockSpec((1, tm, LANES), lambda bi, i: (bi, i, 0)),
                  pl.BlockSpec((1, tm, w // 2), lambda bi, i: (bi, i, 0))],
        out_specs=[pl.BlockSpec((1, tm, w // 2), lambda bi, i: (bi, i, 0)),
                   pl.BlockSpec((1, tm, w // 2), lambda bi, i: (bi, i, 0)),
                   pl.BlockSpec((1, tm, LANES), lambda bi, i: (bi, i, 0)),
                   pl.BlockSpec((1, cpt, ML_CHUNK, LANES), lambda bi, i: (bi, i, 0, 0)),
                   pl.BlockSpec((1, cpt, ML_HEADS * VT_ROWS, ML_CHUNK), lambda bi, i: (bi, i, 0, 0))],
        out_shape=[jax.ShapeDtypeStruct((b, t, w // 2), BF16),
                   jax.ShapeDtypeStruct((b, t, w // 2), BF16),
                   jax.ShapeDtypeStruct((b, t, LANES), F32),
                   jax.ShapeDtypeStruct((b, t // ML_CHUNK, ML_CHUNK, LANES), F32),
                   jax.ShapeDtypeStruct((b, t // ML_CHUNK, ML_HEADS * VT_ROWS, ML_CHUNK), BF16)],
        compiler_params=_params("parallel", "parallel"),
        name="mlstm_prep",
    )(mqk, mqk, mqk, conv_w8, gates, mv)


def _mlstm_kernel(qf_ref, kf_ref, vf_ref, gcf_ref, grf_ref, qb_ref, kb_ref, vb_ref, gcb_ref, grb_ref,
                  hf_ref, hb_ref, *scr):
    s = pl.program_id(0)
    nb = qf_ref.shape[0]
    n_g = 4 * ML_HEADS
    state_refs = scr[:len(scr) // 2]
    m_refs = scr[len(scr) // 2:]

    @pl.when(s == 0)
    def _():
        for ref in scr:
            ref[...] = jnp.zeros(ref.shape, F32)

    ki = lax.broadcasted_iota(jnp.int32, (ML_CHUNK, ML_CHUNK), 0)
    qi = lax.broadcasted_iota(jnp.int32, (ML_CHUNK, ML_CHUNK), 1)

    for bi in range(nb):
        for direction in range(2):
            q_ref, k_ref, vt_ref, gc_ref, gr_ref, h_ref = (
                (qf_ref, kf_ref, vf_ref, gcf_ref, grf_ref, hf_ref) if direction == 0 else
                (qb_ref, kb_ref, vb_ref, gcb_ref, grb_ref, hb_ref))
            visible = (ki <= qi) if direction == 0 else (ki >= qi)
            gc = gc_ref[bi]
            gr = gr_ref[bi, 0]
            for hd in range(ML_HEADS):
                chain = (bi * 2 + direction) * ML_HEADS + hd
                c_lf = (2 * direction + 1) * ML_HEADS + hd
                c_cs = c_lf + (n_g if direction == 0 else 2 * n_g)
                c_u = 3 * n_g + direction * ML_HEADS + hd
                lo, hi = hd * ML_DIM, (hd + 1) * ML_DIM
                q = q_ref[bi, :, lo:hi]
                k = k_ref[bi, :, lo:hi]
                v_t = vt_ref[bi, 0, hd * VT_ROWS:(hd + 1) * VT_ROWS, :]
                u_row = gr[c_u:c_u + 1, :]
                lf_row = gr[c_lf:c_lf + 1, :]
                cs_row = gr[c_cs:c_cs + 1, :]
                m_st = m_refs[chain][:, 0:1]
                st_t = state_refs[chain][...]

                u_vis = jnp.where(visible, jnp.broadcast_to(gc[:, c_u:c_u + 1], visible.shape), -jnp.inf)
                v_row = jnp.maximum(m_st, jnp.max(u_vis, axis=0, keepdims=True))
                d_t = jnp.exp(u_vis - v_row)
                inter = jnp.exp(m_st - v_row)
                sc_t = (_dot_nt(k, q) * d_t).astype(BF16)
                tot_t = inter * _dot_nt(st_t.astype(BF16), q) + _dot(v_t, sc_t)
                den = jnp.maximum(jnp.abs(tot_t[ML_DIM:ML_DIM + 1, :]), jnp.exp(-(cs_row + v_row)))
                h_ref[bi, :, lo:hi] = (tot_t[:ML_DIM, :] / den).T

                b_last = jnp.sum(lf_row, axis=1, keepdims=True)
                m_sc = jnp.maximum(m_st, jnp.max(u_row, axis=1, keepdims=True))
                wgt = jnp.exp(u_row - m_sc)
                vw_t = (v_t.astype(F32) * wgt).astype(BF16)
                state_refs[chain][...] = jnp.exp(m_st - m_sc) * st_t + _dot(vw_t, k)
                m_refs[chain][...] = jnp.broadcast_to(b_last + m_sc, (1, LANES))


def _mlstm(mq, mk, mvt, gc, gr, n_ctx):
    b, t, w = mq.shape
    nc = t // ML_CHUNK
    ncc = n_ctx // ML_CHUNK
    fwd = lambda s: s
    bwd = lambda s: jnp.where(s < ncc, ncc - 1 - s, nc - 1 - s + ncc)
    tok = lambda f: pl.BlockSpec((b, ML_CHUNK, w), lambda s: (0, f(s), 0))
    gcs = lambda f: pl.BlockSpec((b, ML_CHUNK, LANES), lambda s: (0, f(s), 0))
    grs = lambda f: pl.BlockSpec((b, 1, ML_CHUNK, LANES), lambda s: (0, f(s), 0, 0))
    vts = lambda f: pl.BlockSpec((b, 1, ML_HEADS * VT_ROWS, ML_CHUNK), lambda s: (0, f(s), 0, 0))
    n_chain = b * 2 * ML_HEADS
    return pl.pallas_call(
        _mlstm_kernel,
        grid=(nc,),
        in_specs=[tok(fwd), tok(fwd), vts(fwd), gcs(fwd), grs(fwd),
                  tok(bwd), tok(bwd), vts(bwd), gcs(bwd), grs(bwd)],
        out_specs=[tok(fwd), tok(bwd)],
        out_shape=[jax.ShapeDtypeStruct((b, t, w), F32)] * 2,
        scratch_shapes=[pltpu.VMEM((VT_ROWS, ML_DIM), F32)] * n_chain
                       + [pltpu.VMEM((1, LANES), F32)] * n_chain,
        compiler_params=_params("arbitrary"),
        name="mlstm",
    )(mq, mk, mvt, gc, gr, mq, mk, mvt, gc, gr)


def _attn_kernel(q_ref, qn_ref, k_ref, vt_ref, lam_ref, sg_ref, o_ref, qm_scr, *scr,
                 ctx_tiles, n_ctx, key_block, n_blocks, lam_init):
    s_scr = (scr[0:2], scr[2:4])
    bm_scr = (scr[4:6], scr[6:8])
    m_scr = scr[8:10]
    acc_scr = scr[10:12]
    i = pl.program_id(2)
    dv = DA_V_DIM
    q = q_ref[0]
    lane = lax.broadcasted_iota(jnp.int32, q.shape, 1)
    zero = jnp.zeros_like(q)

    def map_halves(qv):
        return jnp.where(lane < DA_QK_DIM, qv, zero), jnp.where(lane >= DA_QK_DIM, qv, zero)

    qm_scr[0], qm_scr[1] = map_halves(q)

    def scores(mp, start, size, q_maps=None):
        q_map = qm_scr[mp] if q_maps is None else q_maps[mp]
        return _dot_nt(k_ref[0, pl.ds(start, size), :], q_map)

    @pl.when(i < ctx_tiles)
    def _():
        for mp in range(2):
            sc = scores(mp, 0, n_ctx)
            m = jnp.max(sc, axis=0, keepdims=True)
            p = jnp.exp2(sc - m).astype(BF16)
            acc_scr[mp][...] = _dot(vt_ref[0, 0, :, 0:n_ctx], p)

    def stage(j, slot, q_maps=None):
        start = pl.multiple_of(j * key_block, LANES)
        for mp in range(2):
            sc = scores(mp, start, key_block, q_maps)
            s_scr[mp][slot][...] = sc
            bm_scr[mp][slot][...] = jnp.max(sc, axis=0, keepdims=True)

    def consume(j, slot):
        v_t = vt_ref[0, 0, :, pl.ds(pl.multiple_of(j * key_block, LANES), key_block)]
        for mp in range(2):
            m_old = m_scr[mp][...]
            m_new = jnp.maximum(m_old, bm_scr[mp][slot][...])
            alpha = jnp.exp2(m_old - m_new)
            p = jnp.exp2(s_scr[mp][slot][...] - m_new).astype(BF16)
            acc_scr[mp][...] = alpha * acc_scr[mp][...] + _dot(v_t, p)
            m_scr[mp][...] = m_new

    @pl.when(i >= ctx_tiles)
    def _():
        for mp in range(2):
            m_scr[mp][...] = jnp.full(m_scr[mp].shape, -jnp.inf, F32)
            acc_scr[mp][...] = jnp.zeros(acc_scr[mp].shape, F32)

        @pl.when(i == ctx_tiles)
        def _():
            stage(0, 0)

        def body(g, carry):
            stage(2 * g + 1, 1)
            consume(2 * g, 0)
            stage(2 * g + 2, 0)
            consume(2 * g + 1, 1)
            return carry
        lax.fori_loop(0, n_blocks // 2 - 1, body, 0)
        stage(n_blocks - 1, 1)
        consume(n_blocks - 2, 0)
        stage(0, 0, map_halves(qn_ref[0]))
        consume(n_blocks - 1, 1)

    lv = lam_ref[...]
    dot01 = jnp.sum(lv[0:1, :] * lv[1:2, :], axis=1, keepdims=True)
    dot23 = jnp.sum(lv[2:3, :] * lv[3:4, :], axis=1, keepdims=True)
    lam = jnp.exp(dot01) - jnp.exp(dot23) + lam_init
    a0 = acc_scr[0][...]
    a1 = acc_scr[1][...]
    o = (a0[0:dv, :] / a0[dv:dv + 1, :] - lam * (a1[0:dv, :] / a1[dv:dv + 1, :])).T
    o = o * lax.rsqrt(jnp.mean(o * o, axis=-1, keepdims=True) + NORM_EPS) * sg_ref[...]
    o_ref[0] = (o * (1.0 - lam_init)).astype(BF16)


def _diff_attention(q, k, vt, lam8, subln_g, n_ctx, lam_init):
    b, t, w = q.shape
    tq = ROW_TILE
    n_blocks = next(n for n in (10, 8, 6, 4, 2) if t % (n * LANES) == 0 and t // n >= n_ctx)
    key_block = t // n_blocks
    vt_rows = vt.shape[2]
    assert n_ctx % tq == 0
    kern = functools.partial(_attn_kernel, ctx_tiles=n_ctx // tq, n_ctx=n_ctx, key_block=key_block,
                             n_blocks=n_blocks, lam_init=lam_init)
    return pl.pallas_call(
        kern,
        grid=(b, DA_HEADS, t // tq),
        in_specs=[pl.BlockSpec((1, tq, LANES), lambda bi, h, i: (bi, i, h)),
                  pl.BlockSpec((1, tq, LANES), lambda bi, h, i: (bi, jnp.minimum(i + 1, t // tq - 1), h)),
                  pl.BlockSpec((1, t, LANES), lambda bi, h, i: (bi, 0, h)),
                  pl.BlockSpec((1, 1, vt_rows, t), lambda bi, h, i: (bi, h, 0, 0)),
                  pl.BlockSpec(lam8.shape, lambda bi, h, i: (0, 0)),
                  pl.BlockSpec(subln_g.shape, lambda bi, h, i: (0, 0))],
        out_specs=pl.BlockSpec((1, tq, LANES), lambda bi, h, i: (bi, i, h)),
        out_shape=jax.ShapeDtypeStruct((b, t, w), BF16),
        scratch_shapes=[pltpu.VMEM((2, tq, LANES), BF16)]
                       + [pltpu.VMEM((key_block, tq), F32)] * 4
                       + [pltpu.VMEM((1, tq), F32)] * 6
                       + [pltpu.VMEM((vt_rows, tq), F32)] * 2,
        compiler_params=_params("arbitrary", "arbitrary", "arbitrary"),
        name="diff_attention",
    )(q, q, k, vt, lam8, subln_g)


def _top2_sum(a, b, c, d):
    hi1, lo1 = jnp.maximum(a, b), jnp.minimum(a, b)
    hi2, lo2 = jnp.maximum(c, d), jnp.minimum(c, d)
    return jnp.maximum(hi1, hi2) + jnp.maximum(jnp.minimum(hi1, hi2), jnp.maximum(lo1, lo2))


def _route(f, rwt_ref, rb_ref, cnt_ref):
    tm = f.shape[0]
    w = rwt_ref[...]
    w_hi, f_hi = w.astype(BF16), f.astype(BF16)
    w_lo = (w - w_hi.astype(F32)).astype(BF16)
    f_lo = (f - f_hi.astype(F32)).astype(BF16)
    part = _dot_nt(jnp.concatenate([w_hi, w_lo], axis=0), f_hi)
    aff = _sigmoid(part[:N_EXPERTS] + part[N_EXPERTS:] + _dot_nt(w_hi, f_lo))
    biased = aff + rb_ref[:, 0:1]
    bz = [biased[e:e + 1, :] for e in range(N_EXPERTS)]
    af = [aff[e:e + 1, :] for e in range(N_EXPERTS)]
    n_grp = N_EXPERTS // EXPERTS_PER_GROUP
    scores = [_top2_sum(*bz[EXPERTS_PER_GROUP * g:EXPERTS_PER_GROUP * (g + 1)]) for g in range(n_grp)]
    best = scores[0]
    sel_grp = jnp.zeros_like(best, dtype=jnp.int32)
    for g in range(1, n_grp):
        better = scores[g] > best
        sel_grp = jnp.where(better, g, sel_grp)
        best = jnp.where(better, scores[g], best)
    chosen = []
    for e in range(N_EXPERTS):
        g = e // EXPERTS_PER_GROUP
        rank = jnp.zeros_like(sel_grp)
        for o in range(EXPERTS_PER_GROUP * g, EXPERTS_PER_GROUP * (g + 1)):
            if o == e:
                continue
            beats = (bz[o] > bz[e]) if o > e else (bz[o] >= bz[e])
            rank = rank + beats.astype(jnp.int32)
        chosen.append(jnp.logical_and(sel_grp == g, rank < 2))
    denom = sum(jnp.where(chosen[e], af[e], 0.0) for e in range(N_EXPERTS))
    erow = lax.broadcasted_iota(jnp.int32, (N_EXPERTS, tm), 0)
    one_hot = jnp.zeros((N_EXPERTS, tm), F32)
    for e in range(N_EXPERTS):
        one_hot = jnp.where(jnp.logical_and(erow == e, chosen[e]), 1.0, one_hot)
    earlier = (lax.broadcasted_iota(jnp.int32, (tm, tm), 0)
               < lax.broadcasted_iota(jnp.int32, (tm, tm), 1)).astype(BF16)
    rank_all = _dot(one_hot.astype(BF16), earlier) + cnt_ref[:, 0:1]
    cnt_ref[...] = cnt_ref[...] + jnp.sum(one_hot, axis=1, keepdims=True)

    seen = jnp.zeros((1, tm), jnp.bool_)
    e_a = e_b = jnp.zeros((1, tm), jnp.int32)
    r_a = r_b = w_a = w_b = jnp.zeros((1, tm), F32)
    for e in range(N_EXPERTS):
        first = jnp.logical_and(chosen[e], jnp.logical_not(seen))
        second = jnp.logical_and(chosen[e], seen)
        rk = rank_all[e:e + 1, :]
        wt = af[e] / denom
        e_a, e_b = jnp.where(first, e, e_a), jnp.where(second, e, e_b)
        r_a, r_b = jnp.where(first, rk, r_a), jnp.where(second, rk, r_b)
        w_a, w_b = jnp.where(first, wt, w_a), jnp.where(second, wt, w_b)
        seen = jnp.logical_or(seen, chosen[e])
    r8 = lax.broadcasted_iota(jnp.int32, (8, tm), 0)
    sel = jnp.where(r8 == 0, e_a, jnp.where(r8 == 1, e_b, jnp.where(
        r8 == 2, r_a.astype(jnp.int32), jnp.where(r8 == 3, r_b.astype(jnp.int32), 0))))
    row = lax.broadcasted_iota(jnp.int32, (LANES, tm), 0)
    w_t = jnp.where(row == 0, w_a, jnp.where(row == 1, w_b, 0.0))
    return sel, w_t.T


def _post_kernel(*refs, even, ctx_tiles):
    if even:
        (c_ref, x_ref, a_ref, hf_ref, hb_ref, og_ref, mod_ref, wa_ref, wm_ref, gf_ref, rwt_ref, rb_ref,
         xo_ref, f_ref, sel_ref, w_ref, cnt_ref, cnt_scr) = refs
        m = ((hf_ref[0] + hb_ref[0]) * og_ref[0].astype(F32)).astype(BF16)
        o = _dot(a_ref[0], wa_ref[...]) + _dot(m, wm_ref[...])
        x_in = jnp.where(pl.program_id(1) < ctx_tiles, c_ref[0], x_ref[0])
    else:
        (x_ref, a_ref, mod_ref, wa_ref, gf_ref, rwt_ref, rb_ref,
         xo_ref, f_ref, sel_ref, w_ref, cnt_ref, cnt_scr) = refs
        o = _dot(a_ref[0].astype(BF16), wa_ref[...])
        x_in = x_ref[0]

    @pl.when(jnp.logical_and(pl.program_id(0) == 0, pl.program_id(1) == 0))
    def _():
        cnt_scr[...] = jnp.zeros(cnt_scr.shape, F32)

    x = x_in + mod_ref[0, 2:3, :] * o
    xo_ref[0] = x
    f = _norm_mod(x, gf_ref[...], mod_ref[0, 3:4, :], mod_ref[0, 4:5, :])
    _store_row_tiles(f_ref.at[0], f)
    sel, w_col = _route(f, rwt_ref, rb_ref, cnt_scr)
    sel_ref[0, 0] = sel
    w_ref[0] = w_col
    cnt_ref[...] = cnt_scr[...]


def _post_mixer(x, acts, weights, modtab, gffn, rwt, rb, n_ctx, x_row_off, even, x_ctx=None):
    b, t, _ = acts[0].shape
    d = x.shape[2]
    tm = ROW_TILE
    nt = t // tm
    ctx_tiles = n_ctx // tm
    off = x_row_off // tm
    full = lambda a: pl.BlockSpec(a.shape, lambda bi, i: (0,) * a.ndim)
    row = lambda w: pl.BlockSpec((1, tm, w), lambda bi, i: (bi, i, 0))
    mod_spec = pl.BlockSpec(
        (1, 8, d), lambda bi, i: (2 * bi + (i + off >= ctx_tiles).astype(jnp.int32), 0, 0))
    if x_ctx is None:
        residual, res_specs = [x], [pl.BlockSpec((1, tm, d), lambda bi, i: (bi, i + off, 0))]
    else:
        assert even and x_row_off == 0 and ctx_tiles >= 1
        residual = [x_ctx, x]
        res_specs = [pl.BlockSpec((1, tm, d), lambda bi, i: (bi, jnp.minimum(i, ctx_tiles - 1), 0)),
                     pl.BlockSpec((1, tm, d), lambda bi, i: (bi, jnp.maximum(i - ctx_tiles, 0), 0))]
    in_specs = (res_specs + [row(a.shape[2]) for a in acts] + [mod_spec]
                + [full(w) for w in weights] + [full(gffn), full(rwt), full(rb)])
    return pl.pallas_call(
        functools.partial(_post_kernel, even=even, ctx_tiles=ctx_tiles),
        grid=(b, nt),
        in_specs=in_specs,
        out_specs=[row(d), pl.BlockSpec((1, tm * d // LANES, LANES), lambda bi, i: (bi, i, 0)),
                   pl.BlockSpec((1, 1, 8, tm), lambda bi, i: (bi, i, 0, 0)),
                   row(LANES),
                   pl.BlockSpec((N_EXPERTS, LANES), lambda bi, i: (0, 0))],
        out_shape=[jax.ShapeDtypeStruct((b, t, d), F32),
                   jax.ShapeDtypeStruct((b, t * d // LANES, LANES), F32),
                   jax.ShapeDtypeStruct((b, nt, 8, tm), jnp.int32),
                   jax.ShapeDtypeStruct((b, t, LANES), F32),
                   jax.ShapeDtypeStruct((N_EXPERTS, LANES), F32)],
        scratch_shapes=[pltpu.VMEM((N_EXPERTS, LANES), F32)],
        compiler_params=_params("arbitrary", "arbitrary"),
        name="post_mixer_even" if even else "post_mixer_odd",
    )(*residual, *acts, modtab, *weights, gffn, rwt, rb)


EXPERT_ROW_TILE = 512
DMA_ISSUE_UNROLL = 8


def _moe_plan(sel, counts, n_tok):
    e_a, e_b, r_a, r_b = (sel[:, :, k, :].reshape(-1) for k in range(4))
    cnt = counts[:, 0].astype(jnp.int32)
    padded = ((cnt + EXPERT_ROW_TILE - 1) // EXPERT_ROW_TILE) * EXPERT_ROW_TILE
    ends = jnp.cumsum(padded)
    starts = ends - padded
    pos = jnp.concatenate([starts[e_a] + r_a, starts[e_b] + r_b]).astype(jnp.int32)
    n_tiles = 2 * n_tok // EXPERT_ROW_TILE + N_EXPERTS
    tile_start = jnp.arange(n_tiles, dtype=jnp.int32) * EXPERT_ROW_TILE
    tile_expert = jnp.minimum(jnp.sum(tile_start[:, None] >= ends[None, :], axis=1),
                              N_EXPERTS - 1).astype(jnp.int32)
    tiles_used = (ends[-1:] // EXPERT_ROW_TILE).astype(jnp.int32)
    return pos, tile_expert, tiles_used, n_tiles


def _store_row_tiles(ref, x):
    rows, d = x.shape
    n_sub = d // LANES
    for s in range(n_sub):
        ref[pl.ds(s, rows, stride=n_sub), :] = x[:, s * LANES:(s + 1) * LANES]


def _load_row_tiles(ref, rows):
    n_sub = ref.shape[0] // rows
    return jnp.concatenate([ref[pl.ds(s, rows, stride=n_sub), :] for s in range(n_sub)], axis=1)


def _dispatch_kernel(pos_ref, f_ref, init_ref, out_ref, sem, *, n_tok, tm):
    del init_ref
    n_sub = f_ref.shape[1] // tm
    base = (pl.program_id(0) * pl.num_programs(1) + pl.program_id(1)) * tm

    def row_copy(r, k):
        dst = pos_ref[k * n_tok + base + r]
        src = f_ref.at[0, pl.ds(pl.multiple_of(r * n_sub, n_sub), n_sub), :]
        return pltpu.make_async_copy(src, out_ref.at[dst], sem)

    def issue(r, carry):
        row_copy(r, 0).start()
        row_copy(r, 1).start(priority=1)
        return carry

    lax.fori_loop(0, tm, issue, 0, unroll=DMA_ISSUE_UNROLL)
    for _ in range(2):
        pltpu.make_async_copy(out_ref.at[pl.ds(0, tm)], out_ref.at[pl.ds(0, tm)], sem).wait()


def _dispatch(pos, f, n_rows, n_tok):
    b, rows, _ = f.shape
    n_sub = rows * b // n_tok
    tm = next(m * ROW_TILE for m in (8, 5, 4, 2, 1) if (n_tok // b) % (m * ROW_TILE) == 0)
    grid_spec = pltpu.PrefetchScalarGridSpec(
        num_scalar_prefetch=1,
        grid=(b, n_tok // b // tm),
        in_specs=[pl.BlockSpec((1, tm * n_sub, LANES), lambda bi, i, pos_ref: (bi, i, 0)),
                  pl.BlockSpec(memory_space=pl.ANY)],
        out_specs=pl.BlockSpec(memory_space=pl.ANY),
        scratch_shapes=[pltpu.SemaphoreType.DMA(())])
    return pl.pallas_call(
        functools.partial(_dispatch_kernel, n_tok=n_tok, tm=tm),
        grid_spec=grid_spec,
        out_shape=jax.ShapeDtypeStruct((n_rows, n_sub, LANES), F32),
        input_output_aliases={2: 0},
        compiler_params=_params("arbitrary", "arbitrary"),
        name="moe_dispatch",
    )(pos, f, jnp.zeros((n_rows, n_sub, LANES), F32))


def _expert_ffn_kernel(te_ref, used_ref, x_ref, wg_ref, wu_ref, wd_ref, y_ref, wg_scr, wu_scr, wd_scr):
    j = pl.program_id(0)
    live = j < used_ref[0]

    @pl.when(jnp.logical_or(j == 0, te_ref[j] != te_ref[jnp.maximum(j - 1, 0)]))
    def _():
        wg_scr[...] = wg_ref[0, 0].astype(BF16)
        wu_scr[...] = wu_ref[0, 0].astype(BF16)
        wd_scr[...] = wd_ref[0, 0].astype(BF16)

    @pl.when(live)
    def _():
        xb = _load_row_tiles(x_ref, EXPERT_ROW_TILE).astype(BF16)
        he = _silu(_dot(xb, wg_scr[...])) * _dot(xb, wu_scr[...])
        _store_row_tiles(y_ref, _dot(he.astype(BF16), wd_scr[...]))

    @pl.when(jnp.logical_not(live))
    def _():
        y_ref[...] = jnp.zeros(y_ref.shape, F32)


def _expert_ffn(tile_expert, tiles_used, xs, wg, wu, wd, layer, n_tiles):
    n_rows, n_sub, _ = xs.shape
    _, _, d, d_e = wg.shape
    tr = EXPERT_ROW_TILE
    grid_spec = pltpu.PrefetchScalarGridSpec(
        num_scalar_prefetch=2,
        grid=(n_tiles,),
        in_specs=[pl.BlockSpec((tr * n_sub, LANES), lambda j, te, used: (j, 0)),
                  pl.BlockSpec((1, 1, d, d_e), lambda j, te, used: (layer, te[j], 0, 0)),
                  pl.BlockSpec((1, 1, d, d_e), lambda j, te, used: (layer, te[j], 0, 0)),
                  pl.BlockSpec((1, 1, d_e, d), lambda j, te, used: (layer, te[j], 0, 0))],
        out_specs=pl.BlockSpec((tr * n_sub, LANES), lambda j, te, used: (j, 0)),
        scratch_shapes=[pltpu.VMEM((d, d_e), BF16), pltpu.VMEM((d, d_e), BF16),
                        pltpu.VMEM((d_e, d), BF16)])
    ys = pl.pallas_call(
        _expert_ffn_kernel,
        grid_spec=grid_spec,
        out_shape=jax.ShapeDtypeStruct((n_rows * n_sub, LANES), F32),
        compiler_params=_params("arbitrary"),
        name="moe_expert_ffn",
    )(tile_expert, tiles_used, xs.reshape(n_rows * n_sub, LANES), wg, wu, wd)
    return ys.reshape(n_rows, n_sub, LANES)


def _combine_kernel(pos_ref, y_ref, w_ref, x_ref, modc_ref, modl_ref, fg_ref, o_ref, buf, sem,
                    *, n_tok, n_ctx, final_norm):
    i = pl.program_id(1)
    tm = x_ref.shape[1]
    step = pl.program_id(0) * pl.num_programs(1) + i
    n_steps = pl.num_programs(0) * pl.num_programs(1)
    n_sub = y_ref.shape[1]

    def gather(tile, slot):
        base = tile * tm

        def issue(r, carry):
            for k in range(2):
                src = pos_ref[k * n_tok + base + r]
                dst = buf.at[slot, k, pl.ds(pl.multiple_of(r * n_sub, n_sub), n_sub), :]
                pltpu.make_async_copy(y_ref.at[src], dst, sem.at[slot]).start(priority=k)
            return carry

        lax.fori_loop(0, tm, issue, 0, unroll=DMA_ISSUE_UNROLL)

    slot = lax.rem(step, 2)

    @pl.when(step == 0)
    def _():
        gather(0, 0)

    @pl.when(step + 1 < n_steps)
    def _():
        gather(step + 1, 1 - slot)

    for _ in range(2):
        pltpu.make_async_copy(y_ref.at[pl.ds(0, tm)], y_ref.at[pl.ds(0, tm)], sem.at[slot]).wait()
    w = w_ref[0]
    y = (_load_row_tiles(buf.at[slot, 0], tm) * w[:, 0:1]
         + _load_row_tiles(buf.at[slot, 1], tm) * w[:, 1:2])
    rows = i * tm + lax.broadcasted_iota(jnp.int32, (tm, 1), 0)
    gate = jnp.where(rows < n_ctx, modc_ref[0, 5:6, :], modl_ref[0, 5:6, :])
    out = x_ref[0] + gate * y
    if final_norm:
        out = out * lax.rsqrt(jnp.mean(out * out, axis=-1, keepdims=True) + NORM_EPS) * fg_ref[...]
    o_ref[0] = out


def _combine(pos, ys, w_col, xmid, modtab, final_g, n_ctx, final_norm):
    b, t, d = xmid.shape
    tm = ROW_TILE
    row = lambda w: pl.BlockSpec((1, tm, w), lambda bi, i, pos_ref: (bi, i, 0))
    grid_spec = pltpu.PrefetchScalarGridSpec(
        num_scalar_prefetch=1,
        grid=(b, t // tm),
        in_specs=[pl.BlockSpec(memory_space=pl.ANY), row(LANES), row(d),
                  pl.BlockSpec((1, 8, d), lambda bi, i, pos_ref: (2 * bi, 0, 0)),
                  pl.BlockSpec((1, 8, d), lambda bi, i, pos_ref: (2 * bi + 1, 0, 0)),
                  pl.BlockSpec(final_g.shape, lambda bi, i, pos_ref: (0, 0))],
        out_specs=row(d),
        scratch_shapes=[pltpu.VMEM((2, 2, tm * d // LANES, LANES), F32), pltpu.SemaphoreType.DMA((2,))])
    return pl.pallas_call(
        functools.partial(_combine_kernel, n_tok=b * t, n_ctx=n_ctx, final_norm=final_norm),
        grid_spec=grid_spec,
        out_shape=jax.ShapeDtypeStruct((b, t, d), F32),
        compiler_params=_params("arbitrary", "arbitrary"),
        name="moe_combine",
    )(pos, ys, w_col, xmid, modtab, modtab, final_g)


def _moe(f, sel, w_col, counts, wg, wu, wd, layer, xmid, modtab, final_g, n_ctx, final_norm):
    b, t, _ = xmid.shape
    pos, tile_expert, tiles_used, n_tiles = _moe_plan(sel, counts, b * t)
    xs = _dispatch(pos, f, n_tiles * EXPERT_ROW_TILE, b * t)
    ys = _expert_ffn(tile_expert, tiles_used, xs, wg, wu, wd, layer, n_tiles)
    return _combine(pos, ys, w_col, xmid, modtab, final_g, n_ctx, final_norm)


def _chan_dft_kernel(x_ref, mod_ref, g_ref, w_ref, zr_ref, zi_ref):
    h = _norm_mod(x_ref[0], g_ref[...], mod_ref[0, 0:1, :], mod_ref[0, 1:2, :])
    gd = w_ref.shape[0]
    for gi in range(h.shape[1] // gd):
        z = _dot_split(h[:, gi * gd:(gi + 1) * gd], w_ref[...])
        zr_ref[0, :, gi * gd:(gi + 1) * gd] = z[:, :gd]
        zi_ref[0, :, gi * gd:(gi + 1) * gd] = z[:, gd:]


def _chan_dft(x, modtab, g, w_cs, x_row_off, t):
    b, _, d = x.shape
    tm = ROW_TILE
    off = x_row_off // tm
    row = pl.BlockSpec((1, tm, d), lambda bi, i: (bi, i, 0))
    return pl.pallas_call(
        _chan_dft_kernel,
        grid=(b, t // tm),
        in_specs=[pl.BlockSpec((1, tm, d), lambda bi, i: (bi, i + off, 0)),
                  pl.BlockSpec((1, 8, d), lambda bi, i: (2 * bi + 1, 0, 0)),
                  pl.BlockSpec(g.shape, lambda bi, i: (0, 0)),
                  pl.BlockSpec(w_cs.shape, lambda bi, i: (0, 0))],
        out_specs=[row, row],
        out_shape=[jax.ShapeDtypeStruct((b, t, d), F32)] * 2,
        compiler_params=_params("parallel", "parallel"),
        name="chan_dft",
    )(x, modtab, g, w_cs)


DFT_SUB = 8


def _dft1_kernel(zr_ref, zi_ref, w_ref, yr_ref, yi_ref):
    _, n1, sub, cols = zr_ref.shape
    z = jnp.concatenate([jnp.concatenate([ref[0, :, j, :] for j in range(sub)], axis=1)
                         for ref in (zr_ref, zi_ref)], axis=0)
    y = _dot_split(w_ref[...], z)
    for j in range(sub):
        yr_ref[0, :, j, :] = y[:n1, j * cols:(j + 1) * cols]
        yi_ref[0, :, j, :] = y[n1:, j * cols:(j + 1) * cols]


def _dft1(zr, zi, w1):
    b, n1, n2, d = zr.shape
    cols = d // 2
    blk = pl.BlockSpec((1, n1, DFT_SUB, cols), lambda bi, j, c: (bi, 0, j, c))
    return pl.pallas_call(
        _dft1_kernel,
        grid=(b, n2 // DFT_SUB, d // cols),
        in_specs=[blk, blk, pl.BlockSpec(w1.shape, lambda bi, j, c: (0, 0))],
        out_specs=[blk, blk],
        out_shape=[jax.ShapeDtypeStruct(zr.shape, F32)] * 2,
        compiler_params=_params("parallel", "parallel", "parallel"),
        name="dft_stage1",
    )(zr, zi, w1)


def _dft2_kernel(yr_ref, yi_ref, tab_ref, o_ref):
    for j in range(yr_ref.shape[1]):
        y = jnp.concatenate([yr_ref[0, j], yi_ref[0, j]], axis=0)
        o_ref[0, :, j, :] = _dot_split(tab_ref[j], y)


def _dft2(yr, yi, tab):
    b, n1, n2, d = yr.shape
    blk = pl.BlockSpec((1, DFT_SUB, n2, d), lambda bi, k1: (bi, k1, 0, 0))
    return pl.pallas_call(
        _dft2_kernel,
        grid=(b, n1 // DFT_SUB),
        in_specs=[blk, blk, pl.BlockSpec((DFT_SUB, n2, 2 * n2), lambda bi, k1: (k1, 0, 0))],
        out_specs=pl.BlockSpec((1, n2, DFT_SUB, d), lambda bi, k1: (bi, 0, k1, 0)),
        out_shape=jax.ShapeDtypeStruct((b, n2, n1, d), F32),
        compiler_params=_params("parallel", "parallel"),
        name="dft_stage2",
    )(yr, yi, tab)


def _dft_tables(t, gd):
    n2 = ML_CHUNK
    n1 = t // n2
    def cs(num, den):
        ang = (2.0 * np.pi / den) * (num % den).astype(np.float64)
        return np.cos(ang), np.sin(ang)
    c = np.arange(gd)
    cc, sc = cs(np.outer(c, c), gd)
    w_cs = np.concatenate([cc, -sc], axis=1)
    a = np.arange(n1)
    c1, s1 = cs(np.outer(a, a), n1)
    w1 = np.block([[c1, s1], [-s1, c1]])
    k = a[:, None, None] + n1 * np.arange(n2)[None, :, None]
    c2, s2 = cs(k * np.arange(n2)[None, None, :], t)
    tab = np.concatenate([c2, s2], axis=2) / math.sqrt(t * gd)
    return (jnp.asarray(w_cs, F32), jnp.asarray(w1, F32), jnp.asarray(tab, F32))


def _rope_tables(n_ctx, n_lat):
    pos = jnp.arange(n_lat, dtype=jnp.int32)
    n_axis = DA_QK_DIM // 4
    inv = ROPE_BASE ** (-jnp.arange(n_axis, dtype=F32) / n_axis)
    ang = jnp.concatenate([(pos // GRID_W).astype(F32)[:, None] * inv,
                           (pos % GRID_W).astype(F32)[:, None] * inv], axis=-1)
    cos, sin = jnp.cos(ang), jnp.sin(ang)
    cos = jnp.concatenate([jnp.ones((n_ctx, 2 * n_axis), F32), cos], axis=0)
    sin = jnp.concatenate([jnp.zeros((n_ctx, 2 * n_axis), F32), sin], axis=0)
    cos128 = jnp.concatenate([cos, cos, cos, cos], axis=1)
    sin128 = jnp.concatenate([-sin, sin, -sin, sin], axis=1)
    return cos128, sin128


def _deinterleave(w):
    d, n = w.shape
    w = w.reshape(d, n // DA_QK_DIM, DA_QK_DIM // 2, 2)
    return jnp.concatenate([w[..., 0], w[..., 1]], axis=-1).reshape(d, n)


def _pad_rows(a, rows):
    return jnp.concatenate([a, jnp.zeros((rows - a.shape[0],) + a.shape[1:], a.dtype)], axis=0)


def _pad_cols(a, cols):
    return jnp.concatenate([a, jnp.zeros(a.shape[:-1] + (cols - a.shape[-1],), a.dtype)], axis=-1)


def kernel(x, c, ctx, c_ctx, ada_w, ada_b, norm_mix_g, norm_ffn_g, even_w_in, even_w_out,
           even_conv_w, even_gate_b, even_lam, even_subln_g, odd_w_fnet, router_w, router_b,
           exp_w_gate, exp_w_up, exp_w_down, final_g):
    b, n_lat, d = x.shape
    n_ctx = ctx.shape[1]
    depth = ada_w.shape[0]
    assert depth == 2 and b + 1 <= 8
    assert n_ctx % ROW_TILE == 0 and n_lat % ROW_TILE == 0

    cond8 = _pad_rows(jnp.concatenate([c_ctx[None, :], c], axis=0), 8)
    rwt = router_w.T
    rb = jnp.broadcast_to(router_b[:, None], (N_EXPERTS, LANES))
    row2 = lambda v: v.reshape(1, -1)

    def modtab_for(layer):
        mods = _ada_mods(cond8, ada_w[layer], ada_b[layer]).reshape(8, 6, d)
        mods = jnp.concatenate([mods, jnp.zeros((8, 2, d), F32)], axis=1)
        idx = np.array([[0, 1 + bi] for bi in range(b)]).reshape(-1)
        return mods[idx]

    modtab = modtab_for(0)
    w_in = even_w_in[0]
    o1 = DA_HEADS * 2 * DA_QK_DIM
    o2 = 2 * o1
    o3 = o2 + DA_HEADS * DA_V_DIM
    o4 = o3 + 2 * ML_HEADS * ML_DIM
    o5 = o4 + ML_HEADS * ML_DIM
    o6 = o5 + ML_HEADS * ML_DIM
    ws = [(_deinterleave(w_in[:, :o1]) * (DA_QK_DIM ** -0.5)).astype(BF16),
          _deinterleave(w_in[:, o1:o2]).astype(BF16),
          w_in[:, o2:o3].astype(BF16), w_in[:, o3:o4].astype(BF16),
          w_in[:, o4:o5].astype(BF16), w_in[:, o5:o6].astype(BF16),
          _pad_cols(w_in[:, o6:], LANES).astype(BF16)]
    gate_b = _pad_cols(even_gate_b[0].reshape(1, -1), LANES)
    cos128, sin128 = _rope_tables(n_ctx, n_lat)
    daq, dak, dav, mqk, mv, og, gates = _inproj(ctx, x, modtab, row2(norm_mix_g[0]), cos128, sin128,
                                                 ws, gate_b)
    mq, mk, gc, gr, mvt = _mlprep(mqk, _pad_rows(even_conv_w[0], 8), gates, mv, n_ctx)
    hf, hb = _mlstm(mq, mk, mvt, gc, gr, n_ctx)
    lam_init = 0.8 - 0.6 * math.exp(-0.3 * 0)
    lam8 = _pad_rows(even_lam[0], 8)
    att = _diff_attention(daq, dak, dav, lam8, row2(even_subln_g[0]), n_ctx, lam_init)
    w_out = even_w_out[0].astype(BF16)
    half = DA_HEADS * DA_V_DIM
    xmid, f, sel, w_col, counts = _post_mixer(x, [att, hf, hb, og], [w_out[:half], w_out[half:]],
                                              modtab, row2(norm_ffn_g[0]), rwt, rb, n_ctx, 0, True,
                                              x_ctx=ctx)
    xs = _moe(f, sel, w_col, counts, exp_w_gate, exp_w_up, exp_w_down, 0,
              xmid, modtab, row2(final_g), n_ctx, False)

    modtab = modtab_for(1)
    gd = d // FN_GROUPS
    w_cs, w1, tab = _dft_tables(n_lat, gd)
    n2 = ML_CHUNK
    n1 = n_lat // n2
    zr, zi = _chan_dft(xs, modtab, row2(norm_mix_g[1]), w_cs, n_ctx, n_lat)
    yr, yi = _dft1(zr.reshape(b, n1, n2, d), zi.reshape(b, n1, n2, d), w1)
    fo = _dft2(yr, yi, tab).reshape(b, n_lat, d)
    xmid, f, sel, w_col, counts = _post_mixer(xs, [fo], [odd_w_fnet[0].astype(BF16)], modtab,
                                              row2(norm_ffn_g[1]), rwt, rb, 0, n_ctx, False)
    return _moe(f, sel, w_col, counts, exp_w_gate, exp_w_up, exp_w_down, 1,
                xmid, modtab, row2(final_g), 0, True)
```
